```python
import math
import jax, jax.numpy as jnp
from jax import lax
import numpy as np

D_MODEL = 1024
BATCH = 8
SEQ = 4096
DEPTH = 4

N_MIXERS = 2
N_GLA_LAYERS = (DEPTH + 1) // 2
N_ATTN_LAYERS = DEPTH // 2

GRID_W = 64

GLA_HEADS = 4
GLA_KEY_DIM = D_MODEL // 2
GLA_VAL_DIM = D_MODEL
GLA_DK = GLA_KEY_DIM // GLA_HEADS
GLA_DV = GLA_VAL_DIM // GLA_HEADS
GLA_GATE_RANK = 16
GLA_GATE_NORMALIZER = 16.0
GLA_CHUNK = 64
GLA_IN_DIM = 2 * GLA_KEY_DIM + 2 * GLA_VAL_DIM + 2 * GLA_GATE_RANK

ATTN_HEAD_DIM = 128
ATTN_Q_HEADS = D_MODEL // ATTN_HEAD_DIM
ATTN_KV_HEADS = 2
ATTN_GROUP = ATTN_Q_HEADS // ATTN_KV_HEADS
ATTN_QKV_DIM = (ATTN_Q_HEADS + 2 * ATTN_KV_HEADS) * ATTN_HEAD_DIM
QUERY_BLOCK = 128
ROPE_THETA = 10000.0
ROPE_PAIRS_PER_AXIS = ATTN_HEAD_DIM // 4

D_FF = 2816
CONV_WIDTH = 3

NORM_EPS = 1e-6

kernel_name = 'hybrid_gla_gqa2drope_convffn_encoder'


def rmsnorm(x, w):
    xf = x.astype(jnp.float32)
    y = xf * lax.rsqrt(jnp.mean(xf * xf, axis=-1, keepdims=True) + NORM_EPS)
    return (y * w.astype(jnp.float32)).astype(x.dtype)


def gla_chunked(q, k, v, log_a, strict):
    B, H, S, dk = q.shape
    dv = v.shape[-1]
    n = S // GLA_CHUNK
    c = lambda t: t.reshape(B, H, n, GLA_CHUNK, t.shape[-1])
    q, k, v, log_a = c(q), c(k), c(v), c(log_a)
    b = jnp.cumsum(log_a, axis=3)
    b_end = b[:, :, :, -1:, :]
    q_dec = q * jnp.exp(b)
    k_inv = k * jnp.exp(-b)
    mask = jnp.tril(jnp.ones((GLA_CHUNK, GLA_CHUNK), dtype=bool), k=-1 if strict else 0)
    att = jnp.where(mask, jnp.einsum('bhncd,bhnsd->bhncs', q_dec, k_inv), 0.0)
    o_intra = jnp.einsum('bhncs,bhnsv->bhncv', att, v)
    kv_chunk = jnp.einsum('bhncd,bhncv->bhndv', k * jnp.exp(b_end - b), v)
    decay_chunk = jnp.exp(b_end[:, :, :, 0, :])

    def step(state, inp):
        d, kv_c = inp
        return d[..., None] * state + kv_c, state

    _, s_prev = lax.scan(step, jnp.zeros((B, H, dk, dv), jnp.float32),
                         (jnp.moveaxis(decay_chunk, 2, 0), jnp.moveaxis(kv_chunk, 2, 0)))
    s_prev = jnp.moveaxis(s_prev, 0, 2)
    o_inter = jnp.einsum('bhncd,bhndv->bhncv', q_dec, s_prev)
    return (o_intra + o_inter).reshape(B, H, S, dv)


def gla_mixer(h, w_in, w_gate_up_f, b_gate_f, w_gate_up_b, b_gate_b, norm_w, w_out):
    B, S, _ = h.shape
    f32 = jnp.float32
    proj = h @ w_in
    q, k, v, g, r = jnp.split(
        proj, [GLA_KEY_DIM, 2 * GLA_KEY_DIM, 2 * GLA_KEY_DIM + GLA_VAL_DIM,
               2 * GLA_KEY_DIM + 2 * GLA_VAL_DIM], axis=-1)
    r_f, r_b = jnp.split(r, 2, axis=-1)

    def heads(t, d):
        return t.reshape(B, S, GLA_HEADS, d).transpose(0, 2, 1, 3).astype(f32)

    def log_decay(r_dir, w_up, b_up):
        logits = (r_dir @ w_up + b_up).astype(f32)
        return heads(jax.nn.log_sigmoid(logits) / GLA_GATE_NORMALIZER, GLA_DK)

    q = heads(q, GLA_DK) * (GLA_DK ** -0.5)
    k = heads(k, GLA_DK)
    v = heads(v, GLA_DV)
    la_f = log_decay(r_f, w_gate_up_f, b_gate_f)
    la_b = log_decay(r_b, w_gate_up_b, b_gate_b)
    flip = lambda t: jnp.flip(t, axis=2)
    o_f = gla_chunked(q, k, v, la_f, strict=False)
    o_b = flip(gla_chunked(flip(q), flip(k), flip(v), flip(la_b), strict=True))
    o = (o_f + o_b).transpose(0, 2, 1, 3)
    o = rmsnorm(o, norm_w) * jax.nn.silu(g.astype(f32).reshape(B, S, GLA_HEADS, GLA_DV))
    return o.reshape(B, S, GLA_VAL_DIM).astype(h.dtype) @ w_out


def apply_rope(x, cos, sin):
    half = x.shape[-1] // 2
    x1, x2 = x[..., :half], x[..., half:]
    return jnp.concatenate([x1 * cos - x2 * sin, x1 * sin + x2 * cos], axis=-1)


def attn_mixer(h, w_qkv, q_norm, k_norm, w_out, cos, sin):
    B, S, _ = h.shape
    proj = h @ w_qkv
    q, k, v = jnp.split(proj, [ATTN_Q_HEADS * ATTN_HEAD_DIM,
                               (ATTN_Q_HEADS + ATTN_KV_HEADS) * ATTN_HEAD_DIM], axis=-1)
    q = q.reshape(B, S, ATTN_Q_HEADS, ATTN_HEAD_DIM)
    k = k.reshape(B, S, ATTN_KV_HEADS, ATTN_HEAD_DIM)
    v = v.reshape(B, S, ATTN_KV_HEADS, ATTN_HEAD_DIM)
    q = apply_rope(rmsnorm(q, q_norm).astype(jnp.float32), cos, sin).astype(h.dtype)
    k = apply_rope(rmsnorm(k, k_norm).astype(jnp.float32), cos, sin).astype(h.dtype)
    n_blk = S // QUERY_BLOCK
    qb = q.reshape(B, S, ATTN_KV_HEADS, ATTN_GROUP, ATTN_HEAD_DIM).transpose(0, 2, 3, 1, 4)
    qb = qb.reshape(B, ATTN_KV_HEADS, ATTN_GROUP, n_blk, QUERY_BLOCK, ATTN_HEAD_DIM)
    qb = qb.transpose(3, 0, 1, 2, 4, 5)
    k = k.transpose(0, 2, 1, 3)
    v = v.transpose(0, 2, 1, 3)
    scale = ATTN_HEAD_DIM ** -0.5

    def block(q_blk):
        s = jnp.einsum('bkgqd,bksd->bkgqs', q_blk, k).astype(jnp.float32) * scale
        p = jax.nn.softmax(s, axis=-1).astype(v.dtype)
        return jnp.einsum('bkgqs,bksd->bkgqd', p, v)

    o = lax.map(block, qb)
    o = o.transpose(1, 0, 4, 2, 3, 5).reshape(B, S, ATTN_Q_HEADS * ATTN_HEAD_DIM)
    return o @ w_out


def conv_ffn(h, w_up, w_conv, b_conv, w_down):
    u = h @ w_up
    u = lax.conv_general_dilated(
        u, w_conv[:, None, :], window_strides=(1,), padding=[(1, 1)],
        dimension_numbers=('NWC', 'WIO', 'NWC'), feature_group_count=u.shape[-1]) + b_conv
    val, gate = jnp.split(u, 2, axis=-1)
    return (jax.nn.silu(gate) * val) @ w_down


def _fwd_setup_inputs(seed: int = 0) -> dict:
    key = jax.random.key(seed)
    ks = jax.random.split(key, 20)
    nrm = lambda k, shape, s: jax.random.normal(k, shape, jnp.float32) * s
    NG, NA = N_GLA_LAYERS, N_ATTN_LAYERS
    return {
        'x': nrm(ks[0], (BATCH, SEQ, D_MODEL), 1.0),
        'norm_mix': 1.0 + nrm(ks[1], (DEPTH, D_MODEL), 0.01),
        'norm_ffn': 1.0 + nrm(ks[2], (DEPTH, D_MODEL), 0.01),
        'gla_w_in': nrm(ks[3], (NG, D_MODEL, GLA_IN_DIM), D_MODEL ** -0.5),
        'gla_w_gate_up_f': nrm(ks[4], (NG, GLA_GATE_RANK, GLA_KEY_DIM), GLA_GATE_RANK ** -0.5),
        'gla_b_gate_f': nrm(ks[5], (NG, GLA_KEY_DIM), 0.1),
        'gla_w_gate_up_b': nrm(ks[6], (NG, GLA_GATE_RANK, GLA_KEY_DIM), GLA_GATE_RANK ** -0.5),
        'gla_b_gate_b': nrm(ks[7], (NG, GLA_KEY_DIM), 0.1),
        'gla_norm': 1.0 + nrm(ks[8], (NG, GLA_DV), 0.01),
        'gla_w_out': nrm(ks[9], (NG, GLA_VAL_DIM, D_MODEL), GLA_VAL_DIM ** -0.5),
        'attn_w_qkv': nrm(ks[10], (NA, D_MODEL, ATTN_QKV_DIM), D_MODEL ** -0.5),
        'attn_q_norm': 1.0 + nrm(ks[11], (NA, ATTN_HEAD_DIM), 0.01),
        'attn_k_norm': 1.0 + nrm(ks[12], (NA, ATTN_HEAD_DIM), 0.01),
        'attn_w_out': nrm(ks[13], (NA, ATTN_Q_HEADS * ATTN_HEAD_DIM, D_MODEL),
                          (ATTN_Q_HEADS * ATTN_HEAD_DIM) ** -0.5),
        'ffn_w_up': nrm(ks[14], (DEPTH, D_MODEL, 2 * D_FF), D_MODEL ** -0.5),
        'ffn_w_conv': nrm(ks[15], (DEPTH, CONV_WIDTH, 2 * D_FF), CONV_WIDTH ** -0.5),
        'ffn_b_conv': nrm(ks[16], (DEPTH, 2 * D_FF), 0.01),
        'ffn_w_down': nrm(ks[17], (DEPTH, D_FF, D_MODEL), D_FF ** -0.5),
    }


def _fwd_reference(x, norm_mix, norm_ffn, gla_w_in, gla_w_gate_up_f, gla_b_gate_f,
              gla_w_gate_up_b, gla_b_gate_b, gla_norm, gla_w_out,
              attn_w_qkv, attn_q_norm, attn_k_norm, attn_w_out,
              ffn_w_up, ffn_w_conv, ffn_b_conv, ffn_w_down):
    S = x.shape[1]
    rows = S // GRID_W
    f32 = jnp.float32
    row_idx = jnp.repeat(jnp.arange(rows, dtype=f32), GRID_W)
    col_idx = jnp.tile(jnp.arange(GRID_W, dtype=f32), rows)
    inv_freq = ROPE_THETA ** (-jnp.arange(ROPE_PAIRS_PER_AXIS, dtype=f32) / ROPE_PAIRS_PER_AXIS)
    ang = jnp.concatenate([row_idx[:, None] * inv_freq, col_idx[:, None] * inv_freq], axis=-1)
    cos = jnp.cos(ang)[None, :, None, :]
    sin = jnp.sin(ang)[None, :, None, :]

    for i in range(DEPTH):
        h = rmsnorm(x, norm_mix[i])
        j = i // N_MIXERS
        if i % N_MIXERS == 0:
            x = x + gla_mixer(h, gla_w_in[j], gla_w_gate_up_f[j], gla_b_gate_f[j],
                              gla_w_gate_up_b[j], gla_b_gate_b[j], gla_norm[j], gla_w_out[j])
        else:
            x = x + attn_mixer(h, attn_w_qkv[j], attn_q_norm[j], attn_k_norm[j],
                               attn_w_out[j], cos, sin)
        x = x + conv_ffn(rmsnorm(x, norm_ffn[i]), ffn_w_up[i], ffn_w_conv[i],
                         ffn_b_conv[i], ffn_w_down[i])
    return x


import jax as _jax
import jax.numpy as _jnp

TWIN_FORMAT = 'train_step'
FWD_PARAMS = ['x', 'norm_mix', 'norm_ffn', 'gla_w_in', 'gla_w_gate_up_f', 'gla_b_gate_f', 'gla_w_gate_up_b', 'gla_b_gate_b', 'gla_norm', 'gla_w_out', 'attn_w_qkv', 'attn_q_norm', 'attn_k_norm', 'attn_w_out', 'ffn_w_up', 'ffn_w_conv', 'ffn_b_conv', 'ffn_w_down']
TWIN_WEIGHTS = ['norm_mix', 'norm_ffn', 'gla_w_in', 'gla_w_gate_up_f', 'gla_b_gate_f', 'gla_w_gate_up_b', 'gla_b_gate_b', 'gla_norm', 'gla_w_out', 'attn_w_qkv', 'attn_q_norm', 'attn_k_norm', 'attn_w_out', 'ffn_w_up', 'ffn_w_conv', 'ffn_b_conv', 'ffn_w_down']
TWIN_DIFF_INPUT = 'x'
TWIN_INPUTS = ['x', 'norm_mix', 'norm_ffn', 'gla_w_in', 'gla_w_gate_up_f', 'gla_b_gate_f', 'gla_w_gate_up_b', 'gla_b_gate_b', 'gla_norm', 'gla_w_out', 'attn_w_qkv', 'attn_q_norm', 'attn_k_norm', 'attn_w_out', 'ffn_w_up', 'ffn_w_conv', 'ffn_b_conv', 'ffn_w_down', 'loss_target', 'm_norm_mix', 'm_norm_ffn', 'm_gla_w_in', 'm_gla_w_gate_up_f', 'm_gla_b_gate_f', 'm_gla_w_gate_up_b', 'm_gla_b_gate_b', 'm_gla_norm', 'm_gla_w_out', 'm_attn_w_qkv', 'm_attn_q_norm', 'm_attn_k_norm', 'm_attn_w_out', 'm_ffn_w_up', 'm_ffn_w_conv', 'm_ffn_b_conv', 'm_ffn_w_down', 'v_norm_mix', 'v_norm_ffn', 'v_gla_w_in', 'v_gla_w_gate_up_f', 'v_gla_b_gate_f', 'v_gla_w_gate_up_b', 'v_gla_b_gate_b', 'v_gla_norm', 'v_gla_w_out', 'v_attn_w_qkv', 'v_attn_q_norm', 'v_attn_k_norm', 'v_attn_w_out', 'v_ffn_w_up', 'v_ffn_w_conv', 'v_ffn_b_conv', 'v_ffn_w_down']
TWIN_OUTPUTS = ['loss', 'grad_x', 'grad_norm_mix', 'grad_norm_ffn', 'grad_gla_w_in', 'grad_gla_w_gate_up_f', 'grad_gla_b_gate_f', 'grad_gla_w_gate_up_b', 'grad_gla_b_gate_b', 'grad_gla_norm', 'grad_gla_w_out', 'grad_attn_w_qkv', 'grad_attn_q_norm', 'grad_attn_k_norm', 'grad_attn_w_out', 'grad_ffn_w_up', 'grad_ffn_w_conv', 'grad_ffn_b_conv', 'grad_ffn_w_down', 'delta_norm_mix', 'delta_norm_ffn', 'delta_gla_w_in', 'delta_gla_w_gate_up_f', 'delta_gla_b_gate_f', 'delta_gla_w_gate_up_b', 'delta_gla_b_gate_b', 'delta_gla_norm', 'delta_gla_w_out', 'delta_attn_w_qkv', 'delta_attn_q_norm', 'delta_attn_k_norm', 'delta_attn_w_out', 'delta_ffn_w_up', 'delta_ffn_w_conv', 'delta_ffn_b_conv', 'delta_ffn_w_down', 'new_m_norm_mix', 'new_m_norm_ffn', 'new_m_gla_w_in', 'new_m_gla_w_gate_up_f', 'new_m_gla_b_gate_f', 'new_m_gla_w_gate_up_b', 'new_m_gla_b_gate_b', 'new_m_gla_norm', 'new_m_gla_w_out', 'new_m_attn_w_qkv', 'new_m_attn_q_norm', 'new_m_attn_k_norm', 'new_m_attn_w_out', 'new_m_ffn_w_up', 'new_m_ffn_w_conv', 'new_m_ffn_b_conv', 'new_m_ffn_w_down', 'new_v_norm_mix', 'new_v_norm_ffn', 'new_v_gla_w_in', 'new_v_gla_w_gate_up_f', 'new_v_gla_b_gate_f', 'new_v_gla_w_gate_up_b', 'new_v_gla_b_gate_b', 'new_v_gla_norm', 'new_v_gla_w_out', 'new_v_attn_w_qkv', 'new_v_attn_q_norm', 'new_v_attn_k_norm', 'new_v_attn_w_out', 'new_v_ffn_w_up', 'new_v_ffn_w_conv', 'new_v_ffn_b_conv', 'new_v_ffn_w_down']
TWIN_LEAF_KINDS = {'loss': 'loss', 'grad_x': 'grad_x', 'grad_norm_mix': 'grad_w', 'grad_norm_ffn': 'grad_w', 'grad_gla_w_in': 'grad_w', 'grad_gla_w_gate_up_f': 'grad_w', 'grad_gla_b_gate_f': 'grad_w', 'grad_gla_w_gate_up_b': 'grad_w', 'grad_gla_b_gate_b': 'grad_w', 'grad_gla_norm': 'grad_w', 'grad_gla_w_out': 'grad_w', 'grad_attn_w_qkv': 'grad_w', 'grad_attn_q_norm': 'grad_w', 'grad_attn_k_norm': 'grad_w', 'grad_attn_w_out': 'grad_w', 'grad_ffn_w_up': 'grad_w', 'grad_ffn_w_conv': 'grad_w', 'grad_ffn_b_conv': 'grad_w', 'grad_ffn_w_down': 'grad_w', 'delta_norm_mix': 'delta_w', 'delta_norm_ffn': 'delta_w', 'delta_gla_w_in': 'delta_w', 'delta_gla_w_gate_up_f': 'delta_w', 'delta_gla_b_gate_f': 'delta_w', 'delta_gla_w_gate_up_b': 'delta_w', 'delta_gla_b_gate_b': 'delta_w', 'delta_gla_norm': 'delta_w', 'delta_gla_w_out': 'delta_w', 'delta_attn_w_qkv': 'delta_w', 'delta_attn_q_norm': 'delta_w', 'delta_attn_k_norm': 'delta_w', 'delta_attn_w_out': 'delta_w', 'delta_ffn_w_up': 'delta_w', 'delta_ffn_w_conv': 'delta_w', 'delta_ffn_b_conv': 'delta_w', 'delta_ffn_w_down': 'delta_w', 'new_m_norm_mix': 'new_m', 'new_m_norm_ffn': 'new_m', 'new_m_gla_w_in': 'new_m', 'new_m_gla_w_gate_up_f': 'new_m', 'new_m_gla_b_gate_f': 'new_m', 'new_m_gla_w_gate_up_b': 'new_m', 'new_m_gla_b_gate_b': 'new_m', 'new_m_gla_norm': 'new_m', 'new_m_gla_w_out': 'new_m', 'new_m_attn_w_qkv': 'new_m', 'new_m_attn_q_norm': 'new_m', 'new_m_attn_k_norm': 'new_m', 'new_m_attn_w_out': 'new_m', 'new_m_ffn_w_up': 'new_m', 'new_m_ffn_w_conv': 'new_m', 'new_m_ffn_b_conv': 'new_m', 'new_m_ffn_w_down': 'new_m', 'new_v_norm_mix': 'new_v', 'new_v_norm_ffn': 'new_v', 'new_v_gla_w_in': 'new_v', 'new_v_gla_w_gate_up_f': 'new_v', 'new_v_gla_b_gate_f': 'new_v', 'new_v_gla_w_gate_up_b': 'new_v', 'new_v_gla_b_gate_b': 'new_v', 'new_v_gla_norm': 'new_v', 'new_v_gla_w_out': 'new_v', 'new_v_attn_w_qkv': 'new_v', 'new_v_attn_q_norm': 'new_v', 'new_v_attn_k_norm': 'new_v', 'new_v_attn_w_out': 'new_v', 'new_v_ffn_w_up': 'new_v', 'new_v_ffn_w_conv': 'new_v', 'new_v_ffn_b_conv': 'new_v', 'new_v_ffn_w_down': 'new_v'}


def _forward(args):
    return _fwd_reference(*[args[k] for k in FWD_PARAMS])


def _output_shape():
    def fwd():
        inp = _fwd_setup_inputs(0)
        return _fwd_reference(*[inp[k] for k in FWD_PARAMS])
    out = _jax.eval_shape(fwd)
    return out.shape, out.dtype

N_MICROBATCH = 1
ADAM_LR = 0.001
ADAM_B1 = 0.9
ADAM_B2 = 0.999
ADAM_EPS = 1e-08
ADAM_WD = 0.01
ADAM_STEP = 10
PER_EXAMPLE_BATCH_AXIS = {'x': 0, 'loss_target': 0}
SHARED_INPUTS = []
_WEIGHT_DTYPES = {'norm_mix': _jnp.float32, 'norm_ffn': _jnp.float32, 'gla_w_in': _jnp.float32, 'gla_w_gate_up_f': _jnp.float32, 'gla_b_gate_f': _jnp.float32, 'gla_w_gate_up_b': _jnp.float32, 'gla_b_gate_b': _jnp.float32, 'gla_norm': _jnp.float32, 'gla_w_out': _jnp.float32, 'attn_w_qkv': _jnp.float32, 'attn_q_norm': _jnp.float32, 'attn_k_norm': _jnp.float32, 'attn_w_out': _jnp.float32, 'ffn_w_up': _jnp.float32, 'ffn_w_conv': _jnp.float32, 'ffn_b_conv': _jnp.float32, 'ffn_w_down': _jnp.float32}
MOMENT_SCALE = {'norm_mix': 9.194677e+00, 'norm_ffn': 2.551972e+01, 'gla_w_in': 6.783067e-01, 'gla_w_gate_up_f': 6.309284e-02, 'gla_b_gate_f': 2.678959e-01, 'gla_w_gate_up_b': 6.227975e-02, 'gla_b_gate_b': 2.516166e-01, 'gla_norm': 4.537316e+01, 'gla_w_out': 6.084097e-01, 'attn_w_qkv': 1.239709e-01, 'attn_q_norm': 6.714703e-01, 'attn_k_norm': 6.708039e-01, 'attn_w_out': 1.053434e-01, 'ffn_w_up': 3.364363e-01, 'ffn_w_conv': 3.524427e+00, 'ffn_b_conv': 3.276250e+00, 'ffn_w_down': 5.180741e-01}


def _to_microbatches(a, axis):
    t = _jnp.moveaxis(a, axis, 0)
    t = t.reshape((N_MICROBATCH, t.shape[0] // N_MICROBATCH) + t.shape[1:])
    return _jnp.moveaxis(t, 1, axis + 1)


def setup_inputs(seed: int = 0) -> dict:
    inp = _fwd_setup_inputs(seed)
    key = _jax.random.fold_in(_jax.random.key(seed), 7919)
    shape, _ = _output_shape()
    out = dict(inp)
    out["loss_target"] = _jax.random.normal(_jax.random.fold_in(key, 0), shape, _jnp.float32)
    for i, name in enumerate(TWIN_WEIGHTS):
        w = inp[name].astype(_jnp.float32)
        if MOMENT_SCALE is None:
            s = _jnp.sqrt(_jnp.mean(_jnp.square(w)) + 1e-30)
        else:
            s = MOMENT_SCALE[name]
        km, kv = _jax.random.split(_jax.random.fold_in(key, i + 1))
        out[name] = w
        out["m_" + name] = s * _jax.random.normal(km, w.shape, _jnp.float32)
        out["v_" + name] = (s * s) * _jax.random.uniform(kv, w.shape, _jnp.float32, 0.5, 1.5)
    if N_MICROBATCH > 1:
        for name, axis in PER_EXAMPLE_BATCH_AXIS.items():
            out[name] = _to_microbatches(out[name], axis)
    return {'x': out['x'], 'norm_mix': out['norm_mix'], 'norm_ffn': out['norm_ffn'], 'gla_w_in': out['gla_w_in'], 'gla_w_gate_up_f': out['gla_w_gate_up_f'], 'gla_b_gate_f': out['gla_b_gate_f'], 'gla_w_gate_up_b': out['gla_w_gate_up_b'], 'gla_b_gate_b': out['gla_b_gate_b'], 'gla_norm': out['gla_norm'], 'gla_w_out': out['gla_w_out'], 'attn_w_qkv': out['attn_w_qkv'], 'attn_q_norm': out['attn_q_norm'], 'attn_k_norm': out['attn_k_norm'], 'attn_w_out': out['attn_w_out'], 'ffn_w_up': out['ffn_w_up'], 'ffn_w_conv': out['ffn_w_conv'], 'ffn_b_conv': out['ffn_b_conv'], 'ffn_w_down': out['ffn_w_down'], 'loss_target': out['loss_target'], 'm_norm_mix': out['m_norm_mix'], 'm_norm_ffn': out['m_norm_ffn'], 'm_gla_w_in': out['m_gla_w_in'], 'm_gla_w_gate_up_f': out['m_gla_w_gate_up_f'], 'm_gla_b_gate_f': out['m_gla_b_gate_f'], 'm_gla_w_gate_up_b': out['m_gla_w_gate_up_b'], 'm_gla_b_gate_b': out['m_gla_b_gate_b'], 'm_gla_norm': out['m_gla_norm'], 'm_gla_w_out': out['m_gla_w_out'], 'm_attn_w_qkv': out['m_attn_w_qkv'], 'm_attn_q_norm': out['m_attn_q_norm'], 'm_attn_k_norm': out['m_attn_k_norm'], 'm_attn_w_out': out['m_attn_w_out'], 'm_ffn_w_up': out['m_ffn_w_up'], 'm_ffn_w_conv': out['m_ffn_w_conv'], 'm_ffn_b_conv': out['m_ffn_b_conv'], 'm_ffn_w_down': out['m_ffn_w_down'], 'v_norm_mix': out['v_norm_mix'], 'v_norm_ffn': out['v_norm_ffn'], 'v_gla_w_in': out['v_gla_w_in'], 'v_gla_w_gate_up_f': out['v_gla_w_gate_up_f'], 'v_gla_b_gate_f': out['v_gla_b_gate_f'], 'v_gla_w_gate_up_b': out['v_gla_w_gate_up_b'], 'v_gla_b_gate_b': out['v_gla_b_gate_b'], 'v_gla_norm': out['v_gla_norm'], 'v_gla_w_out': out['v_gla_w_out'], 'v_attn_w_qkv': out['v_attn_w_qkv'], 'v_attn_q_norm': out['v_attn_q_norm'], 'v_attn_k_norm': out['v_attn_k_norm'], 'v_attn_w_out': out['v_attn_w_out'], 'v_ffn_w_up': out['v_ffn_w_up'], 'v_ffn_w_conv': out['v_ffn_w_conv'], 'v_ffn_b_conv': out['v_ffn_b_conv'], 'v_ffn_w_down': out['v_ffn_w_down']}


def _loss(weights, diff, rest, loss_target):
    with _jax.named_scope("forward"):
        args = {**rest, TWIN_DIFF_INPUT: diff, **{k: w.astype(_WEIGHT_DTYPES[k]) for k, w in weights.items()}}
        y = _forward(args)
    with _jax.named_scope("loss_head"):
        err = _jnp.square(y.astype(_jnp.float32) - loss_target)
        return 0.5 * _jnp.sum(_jnp.mean(err, axis=-1)) if err.ndim else 0.5 * err


def _adamw(w, g, m, v):
    m = ADAM_B1 * m + (1.0 - ADAM_B1) * g
    v = ADAM_B2 * v + (1.0 - ADAM_B2) * _jnp.square(g)
    m_hat = m / (1.0 - ADAM_B1 ** ADAM_STEP)
    v_hat = v / (1.0 - ADAM_B2 ** ADAM_STEP)
    delta = -ADAM_LR * (m_hat / (_jnp.sqrt(v_hat) + ADAM_EPS) + ADAM_WD * w)
    return delta, m, v


def reference(x, norm_mix, norm_ffn, gla_w_in, gla_w_gate_up_f, gla_b_gate_f, gla_w_gate_up_b, gla_b_gate_b, gla_norm, gla_w_out, attn_w_qkv, attn_q_norm, attn_k_norm, attn_w_out, ffn_w_up, ffn_w_conv, ffn_b_conv, ffn_w_down, loss_target, m_norm_mix, m_norm_ffn, m_gla_w_in, m_gla_w_gate_up_f, m_gla_b_gate_f, m_gla_w_gate_up_b, m_gla_b_gate_b, m_gla_norm, m_gla_w_out, m_attn_w_qkv, m_attn_q_norm, m_attn_k_norm, m_attn_w_out, m_ffn_w_up, m_ffn_w_conv, m_ffn_b_conv, m_ffn_w_down, v_norm_mix, v_norm_ffn, v_gla_w_in, v_gla_w_gate_up_f, v_gla_b_gate_f, v_gla_w_gate_up_b, v_gla_b_gate_b, v_gla_norm, v_gla_w_out, v_attn_w_qkv, v_attn_q_norm, v_attn_k_norm, v_attn_w_out, v_ffn_w_up, v_ffn_w_conv, v_ffn_b_conv, v_ffn_w_down):
    given = dict(x=x, norm_mix=norm_mix, norm_ffn=norm_ffn, gla_w_in=gla_w_in, gla_w_gate_up_f=gla_w_gate_up_f, gla_b_gate_f=gla_b_gate_f, gla_w_gate_up_b=gla_w_gate_up_b, gla_b_gate_b=gla_b_gate_b, gla_norm=gla_norm, gla_w_out=gla_w_out, attn_w_qkv=attn_w_qkv, attn_q_norm=attn_q_norm, attn_k_norm=attn_k_norm, attn_w_out=attn_w_out, ffn_w_up=ffn_w_up, ffn_w_conv=ffn_w_conv, ffn_b_conv=ffn_b_conv, ffn_w_down=ffn_w_down, loss_target=loss_target, m_norm_mix=m_norm_mix, m_norm_ffn=m_norm_ffn, m_gla_w_in=m_gla_w_in, m_gla_w_gate_up_f=m_gla_w_gate_up_f, m_gla_b_gate_f=m_gla_b_gate_f, m_gla_w_gate_up_b=m_gla_w_gate_up_b, m_gla_b_gate_b=m_gla_b_gate_b, m_gla_norm=m_gla_norm, m_gla_w_out=m_gla_w_out, m_attn_w_qkv=m_attn_w_qkv, m_attn_q_norm=m_attn_q_norm, m_attn_k_norm=m_attn_k_norm, m_attn_w_out=m_attn_w_out, m_ffn_w_up=m_ffn_w_up, m_ffn_w_conv=m_ffn_w_conv, m_ffn_b_conv=m_ffn_b_conv, m_ffn_w_down=m_ffn_w_down, v_norm_mix=v_norm_mix, v_norm_ffn=v_norm_ffn, v_gla_w_in=v_gla_w_in, v_gla_w_gate_up_f=v_gla_w_gate_up_f, v_gla_b_gate_f=v_gla_b_gate_f, v_gla_w_gate_up_b=v_gla_w_gate_up_b, v_gla_b_gate_b=v_gla_b_gate_b, v_gla_norm=v_gla_norm, v_gla_w_out=v_gla_w_out, v_attn_w_qkv=v_attn_w_qkv, v_attn_q_norm=v_attn_q_norm, v_attn_k_norm=v_attn_k_norm, v_attn_w_out=v_attn_w_out, v_ffn_w_up=v_ffn_w_up, v_ffn_w_conv=v_ffn_w_conv, v_ffn_b_conv=v_ffn_b_conv, v_ffn_w_down=v_ffn_w_down)
    weights = {n: given[n] for n in TWIN_WEIGHTS}
    shared = {n: given[n] for n in SHARED_INPUTS}
    per_example = {n: given[n] for n in ['x']}
    grad_fn = _jax.value_and_grad(_loss, argnums=(0, 1))

    def one_microbatch(ex, loss_target):
        ex = dict(ex)
        diff = ex.pop(TWIN_DIFF_INPUT)
        return grad_fn(weights, diff, {**shared, **ex}, loss_target)

    if N_MICROBATCH == 1:
        loss, (grad_w, grad_x) = one_microbatch(per_example, given["loss_target"])
    else:
        def body(carry, xs):
            loss_sum, grad_sum = carry
            l_k, (gw_k, gx_k) = one_microbatch(xs[0], xs[1])
            with _jax.named_scope("update"):
                return (loss_sum + l_k, _jax.tree.map(_jnp.add, grad_sum, gw_k)), gx_k

        init = (_jnp.zeros((), _jnp.float32), _jax.tree.map(_jnp.zeros_like, weights))
        (loss, grad_w), grad_x = _jax.lax.scan(body, init, (per_example, given["loss_target"]))
    with _jax.named_scope("update"):
        delta_w, new_m, new_v = {}, {}, {}
        for n in TWIN_WEIGHTS:
            delta_w[n], new_m[n], new_v[n] = _adamw(weights[n], grad_w[n], given["m_" + n], given["v_" + n])
    return (loss, grad_x, *[grad_w[n] for n in TWIN_WEIGHTS], *[delta_w[n] for n in TWIN_WEIGHTS],
            *[new_m[n] for n in TWIN_WEIGHTS], *[new_v[n] for n in TWIN_WEIGHTS])
```

```python
import jax
import jax.numpy as jnp
from jax import lax
from jax.experimental import pallas as pl
from jax.experimental.pallas import tpu as pltpu

F32 = jnp.float32
BF16 = jnp.bfloat16
MESH = pl.DeviceIdType.MESH

N_DEV = 8
LANES = 128
SUBLANES = 8
VMEM_LIMIT = 56 * 1024 * 1024

NORM_EPS = 1e-6
GRID_W = 64
ROPE_THETA = 10000.0
GLA_HEADS = 4
GLA_DK = 128
GLA_DV = 256
GLA_CHUNK = 64
GLA_GATE_RANK = 16
GLA_GATE_NORMALIZER = 16.0
ATTN_HD = 128
ATTN_Q_HEADS = 8
ATTN_KV_HEADS = 2
ATTN_GROUP = ATTN_Q_HEADS // ATTN_KV_HEADS

ADAM_LR = 0.001
ADAM_B1 = 0.9
ADAM_B2 = 0.999
ADAM_EPS = 1e-08
ADAM_WD = 0.01
ADAM_STEP = 10


def _tile(n, target, align=LANES):
    if n <= target:
        return n
    t = (target // align) * align
    while t >= align:
        if n % t == 0:
            return t
        t -= align
    return n


def _params(*sem):
    return pltpu.CompilerParams(dimension_semantics=sem, vmem_limit_bytes=VMEM_LIMIT)


def _dot(a, b):
    return lax.dot_general(a, b, (((1,), (0,)), ((), ())), preferred_element_type=F32)


def _dot_nt(a, b):
    return lax.dot_general(a, b, (((1,), (1,)), ((), ())), preferred_element_type=F32)


def _dot_tn(a, b):
    return lax.dot_general(a, b, (((0,), (0,)), ((), ())), preferred_element_type=F32)


def _sigmoid(x):
    return 1.0 / (1.0 + jnp.exp(-x))


def _log_sigmoid(x):
    return jnp.minimum(x, 0.0) - jnp.log(1.0 + jnp.exp(-jnp.abs(x)))


def _colsum(x):
    return jnp.sum(x, axis=0, keepdims=True)


def _mm(a, b, *, ta=False, tb=False, res=None, out_dtype=F32, name):
    if ta:
        K, M = a.shape
    else:
        M, K = a.shape
    if tb:
        N, K2 = b.shape
    else:
        K2, N = b.shape
    assert K == K2, (a.shape, b.shape, ta, tb)
    tm = _tile(M, 1408) if ta else _tile(M, 512, 16)
    tn = _tile(N, 1408)
    tk = _tile(K, 512, 16) if ta else _tile(K, 1408)
    nk = K // tk
    dims = (((0 if ta else 1,), (1 if tb else 0,)), ((), ()))

    def body(*refs):
        if res is not None:
            a_ref, b_ref, r_ref, o_ref = refs[:4]
            scr = refs[4:]
        else:
            a_ref, b_ref, o_ref = refs[:3]
            r_ref = None
            scr = refs[3:]
        part = lax.dot_general(a_ref[...].astype(BF16), b_ref[...].astype(BF16), dims, preferred_element_type=F32)

        def finish(acc):
            if r_ref is not None:
                acc = acc + r_ref[...]
            o_ref[...] = acc.astype(out_dtype)

        if nk == 1:
            finish(part)
        else:
            acc_ref = scr[0]
            k = pl.program_id(2)

            @pl.when(k == 0)
            def _():
                acc_ref[...] = part

            @pl.when(k > 0)
            def _():
                acc_ref[...] += part

            @pl.when(k == nk - 1)
            def _():
                finish(acc_ref[...])

    a_spec = pl.BlockSpec((tk, tm), lambda i, j, k: (k, i)) if ta else pl.BlockSpec((tm, tk), lambda i, j, k: (i, k))
    b_spec = pl.BlockSpec((tn, tk), lambda i, j, k: (j, k)) if tb else pl.BlockSpec((tk, tn), lambda i, j, k: (k, j))
    o_spec = pl.BlockSpec((tm, tn), lambda i, j, k: (i, j))
    in_specs = [a_spec, b_spec] + ([o_spec] if res is not None else [])
    args = (a, b) + ((res,) if res is not None else ())
    return pl.pallas_call(
        body, name=name, grid=(M // tm, N // tn, nk),
        in_specs=in_specs, out_specs=o_spec,
        out_shape=jax.ShapeDtypeStruct((M, N), out_dtype),
        scratch_shapes=[pltpu.VMEM((tm, tn), F32)] if nk > 1 else [],
        compiler_params=_params("parallel", "parallel", "arbitrary"),
    )(*args)


def _rmsnorm_fwd(x, w, *, name):
    S, D = x.shape
    ts = _tile(S, 512, 16)

    def body(x_ref, w_ref, o_ref):
        xv = x_ref[...]
        r = lax.rsqrt(jnp.mean(xv * xv, axis=-1, keepdims=True) + NORM_EPS)
        o_ref[...] = (xv * r * w_ref[...]).astype(BF16)

    return pl.pallas_call(
        body, name=name, grid=(S // ts,),
        in_specs=[pl.BlockSpec((ts, D), lambda i: (i, 0)), pl.BlockSpec((1, D), lambda i: (0, 0))],
        out_specs=pl.BlockSpec((ts, D), lambda i: (i, 0)),
        out_shape=jax.ShapeDtypeStruct((S, D), BF16),
        compiler_params=_params("parallel"),
    )(x, w)


def _rmsnorm_bwd(x, w, dh, dres, *, name):
    S, D = x.shape
    ts = _tile(S, 512, 16)
    n = S // ts

    def body(x_ref, w_ref, dh_ref, dr_ref, dx_ref, dw_ref):
        i = pl.program_id(0)
        xv = x_ref[...]
        r = lax.rsqrt(jnp.mean(xv * xv, axis=-1, keepdims=True) + NORM_EPS)
        xh = xv * r
        d = dh_ref[...]
        g = d * w_ref[...]
        dx_ref[...] = dr_ref[...] + r * (g - xh * jnp.mean(g * xh, axis=-1, keepdims=True))
        part = _colsum(d * xh)

        @pl.when(i == 0)
        def _():
            dw_ref[...] = part

        @pl.when(i > 0)
        def _():
            dw_ref[...] += part

    row = pl.BlockSpec((ts, D), lambda i: (i, 0))
    vec = pl.BlockSpec((1, D), lambda i: (0, 0))
    return pl.pallas_call(
        body, name=name, grid=(n,),
        in_specs=[row, vec, row, row], out_specs=[row, vec],
        out_shape=[jax.ShapeDtypeStruct((S, D), F32), jax.ShapeDtypeStruct((1, D), F32)],
        compiler_params=_params("arbitrary"),
    )(x, w, dh, dres)


def _halo_specs(S, ts, tf, row_axis):
    g = ts // SUBLANES
    last = S // SUBLANES - 1
    col_axis = 1 - row_axis
    main = pl.BlockSpec((ts, tf), lambda *ij: (ij[row_axis], ij[col_axis]))
    prev = pl.BlockSpec((SUBLANES, tf), lambda *ij: (jnp.maximum(ij[row_axis] * g - 1, 0), ij[col_axis]))
    nxt = pl.BlockSpec((SUBLANES, tf), lambda *ij: (jnp.minimum((ij[row_axis] + 1) * g, last), ij[col_axis]))
    return [main, prev, nxt]


def _shifted(u, prev_ref, next_ref, i, n):
    ts = u.shape[0]
    rid = lax.broadcasted_iota(jnp.int32, u.shape, 0)
    before = jnp.where(i > 0, prev_ref[SUBLANES - 1:SUBLANES, :], 0.0)
    after = jnp.where(i < n - 1, next_ref[0:1, :], 0.0)
    um1 = jnp.where(rid == 0, before, pltpu.roll(u, 1, 0))
    up1 = jnp.where(rid == ts - 1, after, pltpu.roll(u, ts - 1, 0))
    return um1, up1


def _conv3(u, prev_ref, next_ref, w_ref, i, n):
    um1, up1 = _shifted(u, prev_ref, next_ref, i, n)
    return w_ref[0:1, :] * um1 + w_ref[1:2, :] * u + w_ref[2:3, :] * up1


def _conv_act_fwd(uv, ug, wv, wg, bv, bg, *, name):
    S, F = uv.shape
    ts = _tile(S, 512, 16)
    tf = _tile(F, 1408)
    n = S // ts

    def body(v_ref, vp_ref, vn_ref, g_ref, gp_ref, gn_ref, wv_ref, wg_ref, bv_ref, bg_ref, o_ref):
        i = pl.program_id(0)
        val = _conv3(v_ref[...], vp_ref, vn_ref, wv_ref, i, n) + bv_ref[...]
        gate = _conv3(g_ref[...], gp_ref, gn_ref, wg_ref, i, n) + bg_ref[...]
        o_ref[...] = (gate * _sigmoid(gate) * val).astype(BF16)

    halo = _halo_specs(S, ts, tf, 0)
    w3 = pl.BlockSpec((3, tf), lambda i, j: (0, j))
    b1 = pl.BlockSpec((1, tf), lambda i, j: (0, j))
    return pl.pallas_call(
        body, name=name, grid=(n, F // tf),
        in_specs=halo + halo + [w3, w3, b1, b1],
        out_specs=pl.BlockSpec((ts, tf), lambda i, j: (i, j)),
        out_shape=jax.ShapeDtypeStruct((S, F), BF16),
        compiler_params=_params("parallel", "parallel"),
    )(uv, uv, uv, ug, ug, ug, wv, wg, bv, bg)


def _conv_act_bwd(uv, ug, wv, wg, bv, bg, dact, *, name):
    S, F = uv.shape
    ts = _tile(S, 512, 16)
    tf = _tile(F, 1408)
    n = S // ts

    def body(v_ref, vp_ref, vn_ref, g_ref, gp_ref, gn_ref, wv_ref, wg_ref, bv_ref, bg_ref, da_ref,
             dv_ref, dg_ref, dwv_ref, dwg_ref, dbv_ref, dbg_ref):
        i = pl.program_id(1)
        uvv, ugv = v_ref[...], g_ref[...]
        vm1, vp1 = _shifted(uvv, vp_ref, vn_ref, i, n)
        gm1, gp1 = _shifted(ugv, gp_ref, gn_ref, i, n)
        val = wv_ref[0:1, :] * vm1 + wv_ref[1:2, :] * uvv + wv_ref[2:3, :] * vp1 + bv_ref[...]
        gate = wg_ref[0:1, :] * gm1 + wg_ref[1:2, :] * ugv + wg_ref[2:3, :] * gp1 + bg_ref[...]
        sg = _sigmoid(gate)
        da = da_ref[...]
        dval = da * (gate * sg)
        dgate = da * val * (sg * (1.0 + gate * (1.0 - sg)))
        dv_ref[...] = dval
        dg_ref[...] = dgate
        sums = [(dwv_ref, 0, vm1 * dval), (dwv_ref, 1, uvv * dval), (dwv_ref, 2, vp1 * dval),
                (dwg_ref, 0, gm1 * dgate), (dwg_ref, 1, ugv * dgate), (dwg_ref, 2, gp1 * dgate),
                (dbv_ref, 0, dval), (dbg_ref, 0, dgate)]
        parts = [(ref, r, _colsum(t)) for ref, r, t in sums]

        @pl.when(i == 0)
        def _():
            for ref, r, part in parts:
                ref[r:r + 1, :] = part

        @pl.when(i > 0)
        def _():
            for ref, r, part in parts:
                ref[r:r + 1, :] += part

    halo = _halo_specs(S, ts, tf, 1)
    w3 = pl.BlockSpec((3, tf), lambda j, i: (0, j))
    b1 = pl.BlockSpec((1, tf), lambda j, i: (0, j))
    blk = pl.BlockSpec((ts, tf), lambda j, i: (i, j))
    return pl.pallas_call(
        body, name=name, grid=(F // tf, n),
        in_specs=halo + halo + [w3, w3, b1, b1, blk],
        out_specs=[blk, blk, w3, w3, b1, b1],
        out_shape=[jax.ShapeDtypeStruct((S, F), F32), jax.ShapeDtypeStruct((S, F), F32),
                   jax.ShapeDtypeStruct((3, F), F32), jax.ShapeDtypeStruct((3, F), F32),
                   jax.ShapeDtypeStruct((1, F), F32), jax.ShapeDtypeStruct((1, F), F32)],
        compiler_params=_params("parallel", "arbitrary"),
    )(uv, uv, uv, ug, ug, ug, wv, wg, bv, bg, dact)


def _conv_t(duc, w, *, name):
    S, F = duc.shape
    ts = _tile(S, 512, 16)
    tf = _tile(F, 1408)
    n = S // ts

    def body(d_ref, dp_ref, dn_ref, w_ref, o_ref):
        i = pl.program_id(0)
        d = d_ref[...]
        dm1, dp1 = _shifted(d, dp_ref, dn_ref, i, n)
        o_ref[...] = (w_ref[0:1, :] * dp1 + w_ref[1:2, :] * d + w_ref[2:3, :] * dm1).astype(BF16)

    return pl.pallas_call(
        body, name=name, grid=(n, F // tf),
        in_specs=_halo_specs(S, ts, tf, 0) + [pl.BlockSpec((3, tf), lambda i, j: (0, j))],
        out_specs=pl.BlockSpec((ts, tf), lambda i, j: (i, j)),
        out_shape=jax.ShapeDtypeStruct((S, F), BF16),
        compiler_params=_params("parallel", "parallel"),
    )(duc, duc, duc, w)


N_QK = ATTN_Q_HEADS + ATTN_KV_HEADS
QKV_DIM = (ATTN_Q_HEADS + 2 * ATTN_KV_HEADS) * ATTN_HD


def _head(hd):
    return slice(hd * ATTN_HD, (hd + 1) * ATTN_HD)


def _attn_prep(proj, cs, sn, qn, kn, *, name):
    S = proj.shape[0]
    ts = _tile(S, 256, 16)

    def body(p_ref, c_ref, s_ref, qn_ref, kn_ref, o_ref):
        c, s = c_ref[...], s_ref[...]
        for hd in range(N_QK):
            xv = p_ref[:, _head(hd)]
            w = qn_ref[...] if hd < ATTN_Q_HEADS else kn_ref[...]
            r = lax.rsqrt(jnp.mean(xv * xv, axis=-1, keepdims=True) + NORM_EPS)
            nrm = xv * r * w
            o_ref[:, _head(hd)] = (nrm * c + pltpu.roll(nrm, ATTN_HD // 2, 1) * s).astype(BF16)
        o_ref[:, N_QK * ATTN_HD:] = p_ref[:, N_QK * ATTN_HD:].astype(BF16)

    row = pl.BlockSpec((ts, QKV_DIM), lambda i: (i, 0))
    rot = pl.BlockSpec((ts, ATTN_HD), lambda i: (i, 0))
    vec = pl.BlockSpec((1, ATTN_HD), lambda i: (0, 0))
    return pl.pallas_call(
        body, name=name, grid=(S // ts,),
        in_specs=[row, rot, rot, vec, vec], out_specs=row,
        out_shape=jax.ShapeDtypeStruct((S, QKV_DIM), BF16),
        compiler_params=_params("parallel"),
    )(proj, cs, sn, qn, kn)


def _attn_prep_bwd(proj, dq, dk, dv, cs, sn, qn, kn, *, name):
    S = proj.shape[0]
    ts = _tile(S, 256, 16)
    nq = ATTN_Q_HEADS * ATTN_HD
    nkv = ATTN_KV_HEADS * ATTN_HD

    def body(p_ref, dq_ref, dk_ref, dv_ref, c_ref, s_ref, qn_ref, kn_ref, o_ref, dqn_ref, dkn_ref):
        i = pl.program_id(0)
        c, s = c_ref[...], s_ref[...]
        acc = [jnp.zeros((1, ATTN_HD), F32), jnp.zeros((1, ATTN_HD), F32)]
        for hd in range(N_QK):
            is_k = hd >= ATTN_Q_HEADS
            xv = p_ref[:, _head(hd)]
            w = kn_ref[...] if is_k else qn_ref[...]
            r = lax.rsqrt(jnp.mean(xv * xv, axis=-1, keepdims=True) + NORM_EPS)
            xh = xv * r
            dout = dk_ref[:, _head(hd - ATTN_Q_HEADS)] if is_k else dq_ref[:, _head(hd)]
            dn = dout * c + pltpu.roll(dout * s, ATTN_HD // 2, 1)
            acc[int(is_k)] = acc[int(is_k)] + _colsum(dn * xh)
            g = dn * w
            o_ref[:, _head(hd)] = (r * (g - xh * jnp.mean(g * xh, axis=-1, keepdims=True))).astype(BF16)
        o_ref[:, N_QK * ATTN_HD:] = dv_ref[...].astype(BF16)

        @pl.when(i == 0)
        def _():
            dqn_ref[...] = acc[0]
            dkn_ref[...] = acc[1]

        @pl.when(i > 0)
        def _():
            dqn_ref[...] += acc[0]
            dkn_ref[...] += acc[1]

    row = pl.BlockSpec((ts, QKV_DIM), lambda i: (i, 0))
    rot = pl.BlockSpec((ts, ATTN_HD), lambda i: (i, 0))
    vec = pl.BlockSpec((1, ATTN_HD), lambda i: (0, 0))
    return pl.pallas_call(
        body, name=name, grid=(S // ts,),
        in_specs=[row, pl.BlockSpec((ts, nq), lambda i: (i, 0)), pl.BlockSpec((ts, nkv), lambda i: (i, 0)),
                  pl.BlockSpec((ts, nkv), lambda i: (i, 0)), rot, rot, vec, vec],
        out_specs=[row, vec, vec],
        out_shape=[jax.ShapeDtypeStruct((S, QKV_DIM), BF16), jax.ShapeDtypeStruct((1, ATTN_HD), F32),
                   jax.ShapeDtypeStruct((1, ATTN_HD), F32)],
        compiler_params=_params("arbitrary"),
    )(proj, dq, dk, dv, cs, sn, qn, kn)


def _softmax_rows(q, k):
    s = _dot_nt(q, k) * (ATTN_HD ** -0.5)
    e = jnp.exp(s - jnp.max(s, axis=-1, keepdims=True))
    return e * (1.0 / jnp.sum(e, axis=-1, keepdims=True))


def _attn_fwd(qkv, *, name):
    S = qkv.shape[0]
    tq = _tile(S, 256, 16)

    def body(q_ref, k_ref, v_ref, o_ref):
        p = _softmax_rows(q_ref[...], k_ref[...])
        o_ref[...] = _dot(p.astype(BF16), v_ref[...]).astype(BF16)

    return pl.pallas_call(
        body, name=name, grid=(ATTN_Q_HEADS, S // tq),
        in_specs=[pl.BlockSpec((tq, ATTN_HD), lambda h, i: (i, h)),
                  pl.BlockSpec((S, ATTN_HD), lambda h, i: (0, ATTN_Q_HEADS + h // ATTN_GROUP)),
                  pl.BlockSpec((S, ATTN_HD), lambda h, i: (0, N_QK + h // ATTN_GROUP))],
        out_specs=pl.BlockSpec((tq, ATTN_HD), lambda h, i: (i, h)),
        out_shape=jax.ShapeDtypeStruct((S, ATTN_Q_HEADS * ATTN_HD), BF16),
        compiler_params=_params("parallel", "parallel"),
    )(qkv, qkv, qkv)


def _attn_bwd(qkv, do, *, name):
    S = qkv.shape[0]
    tq = _tile(S, 256, 16)

    def body(q_ref, k_ref, v_ref, do_ref, dq_ref, dk_ref, dv_ref):
        first = jnp.logical_and(pl.program_id(1) == 0, pl.program_id(2) == 0)
        q, k, dov = q_ref[...], k_ref[...], do_ref[...]
        p = _softmax_rows(q, k)
        dp = _dot_nt(dov, v_ref[...])
        ds = p * (dp - jnp.sum(p * dp, axis=-1, keepdims=True)) * (ATTN_HD ** -0.5)
        dsb = ds.astype(BF16)
        dq_ref[...] = _dot(dsb, k)
        dk = _dot_tn(dsb, q)
        dv = _dot_tn(p.astype(BF16), dov)

        @pl.when(first)
        def _():
            dk_ref[...] = dk
            dv_ref[...] = dv

        @pl.when(jnp.logical_not(first))
        def _():
            dk_ref[...] += dk
            dv_ref[...] += dv

    qblk = pl.BlockSpec((tq, ATTN_HD), lambda kv, g, i: (i, kv * ATTN_GROUP + g))
    kvacc = pl.BlockSpec((S, ATTN_HD), lambda kv, g, i: (0, kv))
    return pl.pallas_call(
        body, name=name, grid=(ATTN_KV_HEADS, ATTN_GROUP, S // tq),
        in_specs=[qblk,
                  pl.BlockSpec((S, ATTN_HD), lambda kv, g, i: (0, ATTN_Q_HEADS + kv)),
                  pl.BlockSpec((S, ATTN_HD), lambda kv, g, i: (0, N_QK + kv)),
                  qblk],
        out_specs=[qblk, kvacc, kvacc],
        out_shape=[jax.ShapeDtypeStruct((S, ATTN_Q_HEADS * ATTN_HD), F32),
                   jax.ShapeDtypeStruct((S, ATTN_KV_HEADS * ATTN_HD), F32),
                   jax.ShapeDtypeStruct((S, ATTN_KV_HEADS * ATTN_HD), F32)],
        compiler_params=_params("parallel", "arbitrary", "arbitrary"),
    )(qkv, qkv, qkv, do)


GLA_KD = GLA_HEADS * GLA_DK
GLA_VD = GLA_HEADS * GLA_DV
GLA_PROJ = 2 * GLA_KD + 2 * GLA_VD + LANES
GLA_SCALE = GLA_DK ** -0.5


def _split3(x):
    hi = x.astype(BF16)
    r1 = x - hi.astype(F32)
    mid = r1.astype(BF16)
    lo = (r1 - mid.astype(F32)).astype(BF16)
    return hi, mid, lo


def _cumdot(t, x):
    hi, mid, lo = _split3(x)
    return _dot(t, hi) + _dot(t, mid) + _dot(t, lo)


def _gla_masks(d):
    c = GLA_CHUNK
    row = lax.broadcasted_iota(jnp.int32, (c, c), 0)
    col = lax.broadcasted_iota(jnp.int32, (c, c), 1)
    lower, upper = col <= row, col >= row
    if d == 0:
        return lower.astype(BF16), upper.astype(BF16), lower
    return upper.astype(BF16), lower.astype(BF16), col > row


def _gla_decay(lg, bias, cum, d):
    xl = lg + bias
    la = _log_sigmoid(xl) * (1.0 / GLA_GATE_NORMALIZER)
    b = _cumdot(cum, la)
    b_end = b[GLA_CHUNK - 1:GLA_CHUNK, :] if d == 0 else b[0:1, :]
    return xl, b, b_end


def _gla_specs(S, n):
    c = GLA_CHUNK
    up = lambda i: i
    down = lambda i: n - 1 - i
    def specs(order):
        return dict(
            q=pl.BlockSpec((c, GLA_KD), lambda i: (order(i), 0)),
            k=pl.BlockSpec((c, GLA_KD), lambda i: (order(i), 1)),
            v=pl.BlockSpec((c, GLA_VD), lambda i: (order(i), 1)),
            st=pl.BlockSpec((1, GLA_HEADS, GLA_DV, GLA_DK), lambda i: (order(i), 0, 0, 0)),
            wide=pl.BlockSpec((c, GLA_VD), lambda i: (order(i), 0)),
            qkv=pl.BlockSpec((c, 2 * GLA_KD + GLA_VD), lambda i: (order(i), 0)),
        )
    return specs(up), specs(down), up, down


def _gla_fwd(proj, logits, bias, *, name):
    S = proj.shape[0]
    c = GLA_CHUNK
    n = S // c
    su, sd, up, down = _gla_specs(S, n)

    def body(qf, kf, vf, lf, qb, kb, vb, lb, bias_ref, of, ob, sf, sb, st):
        @pl.when(pl.program_id(0) == 0)
        def _():
            st[...] = jnp.zeros_like(st)

        for d, (q_r, k_r, v_r, l_r, o_r, s_r) in enumerate(((qf, kf, vf, lf, of, sf), (qb, kb, vb, lb, ob, sb))):
            cum, _, mask = _gla_masks(d)
            _, b, b_end = _gla_decay(l_r[...], bias_ref[:, d * GLA_KD:(d + 1) * GLA_KD], cum, d)
            dend = jnp.exp(b_end)
            k = k_r[...]
            qd = (q_r[...] * GLA_SCALE * jnp.exp(b)).astype(BF16)
            ki = (k * jnp.exp(-b)).astype(BF16)
            ke = (k * jnp.exp(b_end - b)).astype(BF16)
            for h in range(GLA_HEADS):
                ks = slice(h * GLA_DK, (h + 1) * GLA_DK)
                vs = slice(h * GLA_DV, (h + 1) * GLA_DV)
                stp = st[d * GLA_HEADS + h]
                s_r[0, h] = stp
                v = v_r[:, vs].astype(BF16)
                att = jnp.where(mask, _dot_nt(qd[:, ks], ki[:, ks]), 0.0).astype(BF16)
                o_r[:, vs] = _dot(att, v) + _dot_nt(qd[:, ks], stp.astype(BF16))
                st[d * GLA_HEADS + h] = stp * dend[:, ks] + _dot_tn(v, ke[:, ks])

    lg_f = pl.BlockSpec((c, GLA_KD), lambda i: (up(i), 0))
    lg_b = pl.BlockSpec((c, GLA_KD), lambda i: (down(i), 1))
    return pl.pallas_call(
        body, name=name, grid=(n,),
        in_specs=[su["q"], su["k"], su["v"], lg_f, sd["q"], sd["k"], sd["v"], lg_b,
                  pl.BlockSpec((1, 2 * GLA_KD), lambda i: (0, 0))],
        out_specs=[su["wide"], sd["wide"], su["st"], sd["st"]],
        out_shape=[jax.ShapeDtypeStruct((S, GLA_VD), F32), jax.ShapeDtypeStruct((S, GLA_VD), F32),
                   jax.ShapeDtypeStruct((n, GLA_HEADS, GLA_DV, GLA_DK), F32),
                   jax.ShapeDtypeStruct((n, GLA_HEADS, GLA_DV, GLA_DK), F32)],
        scratch_shapes=[pltpu.VMEM((2 * GLA_HEADS, GLA_DV, GLA_DK), F32)],
        compiler_params=_params("arbitrary"),
    )(proj, proj, proj, logits, proj, proj, proj, logits, bias)


def _gla_bwd(proj, logits, bias, sf, sb, do, *, name):
    S = proj.shape[0]
    c = GLA_CHUNK
    n = S // c
    su, sd, up, down = _gla_specs(S, n)

    def body(qf, kf, vf, lf, stf, dof, qb, kb, vb, lb, stb, dob, bias_ref,
             dqkv_f, dlg_f, dqkv_b, dlg_b, dbias, dst):
        first = pl.program_id(0) == 0

        @pl.when(first)
        def _():
            dst[...] = jnp.zeros_like(dst)

        dbias_parts = []
        for d, (q_r, k_r, v_r, l_r, s_r, do_r, dqkv_r, dlg_r) in enumerate(
                ((qf, kf, vf, lf, stf, dof, dqkv_f, dlg_f), (qb, kb, vb, lb, stb, dob, dqkv_b, dlg_b))):
            cum, cum_t, mask = _gla_masks(d)
            xl, b, b_end = _gla_decay(l_r[...], bias_ref[:, d * GLA_KD:(d + 1) * GLA_KD], cum, d)
            e, ei, ee, dend = jnp.exp(b), jnp.exp(-b), jnp.exp(b_end - b), jnp.exp(b_end)
            k = k_r[...]
            qd32 = q_r[...] * GLA_SCALE * e
            ki32 = k * ei
            ke32 = k * ee
            qd, ki, ke = qd32.astype(BF16), ki32.astype(BF16), ke32.astype(BF16)
            db_parts, dbe_parts = [], []
            for h in range(GLA_HEADS):
                ks = slice(h * GLA_DK, (h + 1) * GLA_DK)
                vs = slice(h * GLA_DV, (h + 1) * GLA_DV)
                stp = s_r[0, h]
                dstn = dst[d * GLA_HEADS + h]
                dstn_b = dstn.astype(BF16)
                v = v_r[:, vs].astype(BF16)
                dov = do_r[:, vs]
                att = jnp.where(mask, _dot_nt(qd[:, ks], ki[:, ks]), 0.0).astype(BF16)
                datt = jnp.where(mask, _dot_nt(dov, v), 0.0).astype(BF16)
                dqkv_r[:, 2 * GLA_KD + h * GLA_DV:2 * GLA_KD + (h + 1) * GLA_DV] = (
                    _dot_tn(att, dov) + _dot_nt(ke[:, ks], dstn_b))
                dqd = _dot(datt, ki[:, ks]) + _dot(dov, stp.astype(BF16))
                dki = _dot_tn(datt, qd[:, ks])
                dke = _dot(v, dstn_b)
                d_dend = _colsum(stp * dstn)
                dst[d * GLA_HEADS + h] = _dot_tn(dov, qd[:, ks]) + dstn * dend[:, ks]
                dqkv_r[:, ks] = dqd * e[:, ks] * GLA_SCALE
                dqkv_r[:, GLA_KD + h * GLA_DK:GLA_KD + (h + 1) * GLA_DK] = dki * ei[:, ks] + dke * ee[:, ks]
                dke_ke = dke * ke32[:, ks]
                db_parts.append(dqd * qd32[:, ks] - dki * ki32[:, ks] - dke_ke)
                dbe_parts.append(_colsum(dke_ke) + d_dend * dend[:, ks])
            db = jnp.concatenate(db_parts, axis=1)
            db_end = jnp.concatenate(dbe_parts, axis=1)
            dla = _cumdot(cum_t, db) + db_end
            dlg = dla * (1.0 / GLA_GATE_NORMALIZER) * _sigmoid(-xl)
            dlg_r[...] = dlg
            dbias_parts.append(_colsum(dlg))
        dbv = jnp.concatenate(dbias_parts, axis=1)

        @pl.when(first)
        def _():
            dbias[...] = dbv

        @pl.when(jnp.logical_not(first))
        def _():
            dbias[...] += dbv

    lg_f = pl.BlockSpec((c, GLA_KD), lambda i: (down(i), 0))
    lg_b = pl.BlockSpec((c, GLA_KD), lambda i: (up(i), 1))
    dlg_f = pl.BlockSpec((c, GLA_KD), lambda i: (down(i), 0))
    dlg_b = pl.BlockSpec((c, GLA_KD), lambda i: (up(i), 0))
    return pl.pallas_call(
        body, name=name, grid=(n,),
        in_specs=[sd["q"], sd["k"], sd["v"], lg_f, sd["st"], sd["wide"],
                  su["q"], su["k"], su["v"], lg_b, su["st"], su["wide"],
                  pl.BlockSpec((1, 2 * GLA_KD), lambda i: (0, 0))],
        out_specs=[sd["qkv"], dlg_f, su["qkv"], dlg_b, pl.BlockSpec((1, 2 * GLA_KD), lambda i: (0, 0))],
        out_shape=[jax.ShapeDtypeStruct((S, 2 * GLA_KD + GLA_VD), F32), jax.ShapeDtypeStruct((S, GLA_KD), F32),
                   jax.ShapeDtypeStruct((S, 2 * GLA_KD + GLA_VD), F32), jax.ShapeDtypeStruct((S, GLA_KD), F32),
                   jax.ShapeDtypeStruct((1, 2 * GLA_KD), F32)],
        scratch_shapes=[pltpu.VMEM((2 * GLA_HEADS, GLA_DV, GLA_DK), F32)],
        compiler_params=_params("arbitrary"),
    )(proj, proj, proj, logits, sf, do, proj, proj, proj, logits, sb, do, bias)


def _gla_gate_fwd(of, ob, proj, w, *, name):
    S = of.shape[0]
    ts = _tile(S, 256, 16)

    def body(of_ref, ob_ref, g_ref, w_ref, y_ref):
        for h in range(GLA_HEADS):
            vs = slice(h * GLA_DV, (h + 1) * GLA_DV)
            o = of_ref[:, vs] + ob_ref[:, vs]
            r = lax.rsqrt(jnp.mean(o * o, axis=-1, keepdims=True) + NORM_EPS)
            g = g_ref[:, vs]
            y_ref[:, vs] = (o * r * w_ref[...] * (g * _sigmoid(g))).astype(BF16)

    wide = pl.BlockSpec((ts, GLA_VD), lambda i: (i, 0))
    return pl.pallas_call(
        body, name=name, grid=(S // ts,),
        in_specs=[wide, wide, pl.BlockSpec((ts, GLA_VD), lambda i: (i, 2)), pl.BlockSpec((1, GLA_DV), lambda i: (0, 0))],
        out_specs=wide,
        out_shape=jax.ShapeDtypeStruct((S, GLA_VD), BF16),
        compiler_params=_params("parallel"),
    )(of, ob, proj, w)


def _gla_gate_bwd(of, ob, proj, w, dy, *, name):
    S = of.shape[0]
    ts = _tile(S, 256, 16)

    def body(of_ref, ob_ref, g_ref, w_ref, dy_ref, do_ref, dg_ref, dw_ref):
        i = pl.program_id(0)
        acc = jnp.zeros((1, GLA_DV), F32)
        for h in range(GLA_HEADS):
            vs = slice(h * GLA_DV, (h + 1) * GLA_DV)
            o = of_ref[:, vs] + ob_ref[:, vs]
            r = lax.rsqrt(jnp.mean(o * o, axis=-1, keepdims=True) + NORM_EPS)
            oh = o * r
            g = g_ref[:, vs]
            sg = _sigmoid(g)
            dyv = dy_ref[:, vs]
            dn = dyv * (g * sg)
            dg_ref[:, vs] = dyv * (oh * w_ref[...]) * (sg * (1.0 + g * (1.0 - sg)))
            acc = acc + _colsum(dn * oh)
            gg = dn * w_ref[...]
            do_ref[:, vs] = (r * (gg - oh * jnp.mean(gg * oh, axis=-1, keepdims=True))).astype(BF16)

        @pl.when(i == 0)
        def _():
            dw_ref[...] = acc

        @pl.when(i > 0)
        def _():
            dw_ref[...] += acc

    wide = pl.BlockSpec((ts, GLA_VD), lambda i: (i, 0))
    vec = pl.BlockSpec((1, GLA_DV), lambda i: (0, 0))
    return pl.pallas_call(
        body, name=name, grid=(S // ts,),
        in_specs=[wide, wide, pl.BlockSpec((ts, GLA_VD), lambda i: (i, 2)), vec, wide],
        out_specs=[wide, wide, vec],
        out_shape=[jax.ShapeDtypeStruct((S, GLA_VD), BF16), jax.ShapeDtypeStruct((S, GLA_VD), F32),
                   jax.ShapeDtypeStruct((1, GLA_DV), F32)],
        compiler_params=_params("arbitrary"),
    )(of, ob, proj, w, dy)


def _gla_combine(dqkv_f, dqkv_b, dg, dr, *, name):
    S = dg.shape[0]
    ts = _tile(S, 512, 16)
    nqkv = 2 * GLA_KD + GLA_VD

    def body(f_ref, b_ref, g_ref, r_ref, o_ref):
        o_ref[:, :nqkv] = (f_ref[...] + b_ref[...]).astype(BF16)
        o_ref[:, nqkv:nqkv + GLA_VD] = g_ref[...].astype(BF16)
        o_ref[:, nqkv + GLA_VD:] = r_ref[...].astype(BF16)

    return pl.pallas_call(
        body, name=name, grid=(S // ts,),
        in_specs=[pl.BlockSpec((ts, nqkv), lambda i: (i, 0)), pl.BlockSpec((ts, nqkv), lambda i: (i, 0)),
                  pl.BlockSpec((ts, GLA_VD), lambda i: (i, 0)), pl.BlockSpec((ts, LANES), lambda i: (i, 0))],
        out_specs=pl.BlockSpec((ts, GLA_PROJ), lambda i: (i, 0)),
        out_shape=jax.ShapeDtypeStruct((S, GLA_PROJ), BF16),
        compiler_params=_params("parallel"),
    )(dqkv_f, dqkv_b, dg, dr)


def _loss_head(y, t, *, name):
    S, D = y.shape
    ts = _tile(S, 512, 16)
    n = S // ts

    def body(y_ref, t_ref, dy_ref, l_ref, acc):
        i = pl.program_id(0)
        diff = y_ref[...] - t_ref[...]
        dy_ref[...] = diff * (1.0 / D)
        part = _colsum(diff * diff)

        @pl.when(i == 0)
        def _():
            acc[...] = part

        @pl.when(i > 0)
        def _():
            acc[...] += part

        @pl.when(i == n - 1)
        def _():
            l_ref[...] = jnp.full(l_ref.shape, 0.5 / D, F32) * jnp.sum(acc[...])

    row = pl.BlockSpec((ts, D), lambda i: (i, 0))
    return pl.pallas_call(
        body, name=name, grid=(n,),
        in_specs=[row, row],
        out_specs=[row, pl.BlockSpec((SUBLANES, LANES), lambda i: (0, 0))],
        out_shape=[jax.ShapeDtypeStruct((S, D), F32), jax.ShapeDtypeStruct((SUBLANES, LANES), F32)],
        scratch_shapes=[pltpu.VMEM((1, D), F32)],
        compiler_params=_params("arbitrary"),
    )(y, t)


def _adamw(w, g, m, v, *, name):
    R, C = w.shape
    tr = _tile(R, 256, SUBLANES)

    def body(w_ref, g_ref, m_ref, v_ref, d_ref, nm_ref, nv_ref):
        gv = g_ref[...]
        nm = ADAM_B1 * m_ref[...] + (1.0 - ADAM_B1) * gv
        nv = ADAM_B2 * v_ref[...] + (1.0 - ADAM_B2) * (gv * gv)
        m_hat = nm / (1.0 - ADAM_B1 ** ADAM_STEP)
        v_hat = nv / (1.0 - ADAM_B2 ** ADAM_STEP)
        d_ref[...] = -ADAM_LR * (m_hat / (jnp.sqrt(v_hat) + ADAM_EPS) + ADAM_WD * w_ref[...])
        nm_ref[...] = nm
        nv_ref[...] = nv

    blk = pl.BlockSpec((tr, C), lambda i: (i, 0))
    shp = jax.ShapeDtypeStruct((R, C), F32)
    return pl.pallas_call(
        body, name=name, grid=(R // tr,),
        in_specs=[blk] * 4, out_specs=[blk] * 3, out_shape=[shp] * 3,
        compiler_params=_params("parallel"),
    )(w, g, m, v)


ANY = pl.BlockSpec(memory_space=pl.ANY)


def _place():
    return lax.axis_index("x"), lax.axis_index("y"), lax.axis_index("c")


def _all_gather(block, *, name):
    R, L = block.shape

    def body(x_ref, out_ref, send_sems, recv_sems, local_sem):
        x, y, c = _place()
        me, sibling = (x, y, c), (x, y, 1 - c)
        chips = [(1 - x, y), (x, 1 - y), (1 - x, 1 - y)]

        def slot(px, py, pc):
            return out_ref.at[4 * px + 2 * py + pc]

        def copy(k, blk, to, src=None):
            return pltpu.make_async_remote_copy(
                src_ref=slot(*blk) if src is None else src, dst_ref=slot(*blk),
                send_sem=send_sems.at[k], recv_sem=recv_sems.at[k], device_id=to, device_id_type=MESH)

        mine = pltpu.make_async_copy(x_ref, slot(*me), local_sem)
        mine.start()
        first = [copy(0, me, sibling, src=x_ref)]
        first += [copy(1 + j, me, (*chip, c), src=x_ref) for j, chip in enumerate(chips)]
        for cp in first:
            cp.start()
        passed = [copy(4 + j, (*chip, c), sibling) for j, chip in enumerate(chips)]
        for j, chip in enumerate(chips):
            copy(1 + j, (*chip, c), me).wait_recv()
            passed[j].start()
        copy(0, sibling, me).wait_recv()
        for j, chip in enumerate(chips):
            copy(4 + j, (*chip, 1 - c), me).wait_recv()
        for cp in first + passed:
            cp.wait_send()
        mine.wait()

    return pl.pallas_call(
        body, name=name, in_specs=[ANY], out_specs=ANY,
        out_shape=jax.ShapeDtypeStruct((N_DEV, R, L), block.dtype),
        scratch_shapes=[pltpu.SemaphoreType.DMA((7,)), pltpu.SemaphoreType.DMA((7,)), pltpu.SemaphoreType.DMA],
    )(block)


def _sibling_exchange(g, *, name):
    _, _, R, L = g.shape

    def body(g_ref, buf_ref, send_sems, recv_sems):
        x, y, c = _place()
        copies = [pltpu.make_async_remote_copy(
            src_ref=g_ref.at[k, 1 - c], dst_ref=buf_ref.at[k], send_sem=send_sems.at[k], recv_sem=recv_sems.at[k],
            device_id=(x, y, 1 - c), device_id_type=MESH) for k in range(4)]
        for cp in copies:
            cp.start()
        for cp in copies:
            cp.wait()

    return pl.pallas_call(
        body, name=name, in_specs=[ANY], out_specs=ANY,
        out_shape=jax.ShapeDtypeStruct((4, R, L), g.dtype),
        scratch_shapes=[pltpu.SemaphoreType.DMA((4,)), pltpu.SemaphoreType.DMA((4,))],
    )(g)


def _add_sibling(g, buf, c_idx, *, name):
    _, _, R, L = g.shape
    tr = _tile(R, 1024, SUBLANES)

    def body(c_ref, g_ref, b_ref, o_ref):
        o_ref[...] = g_ref[...] + b_ref[...]

    return pl.pallas_call(
        body, name=name,
        grid_spec=pltpu.PrefetchScalarGridSpec(
            num_scalar_prefetch=1, grid=(4, R // tr),
            in_specs=[pl.BlockSpec((None, None, tr, L), lambda k, i, c_ref: (k, c_ref[0], i, 0)),
                      pl.BlockSpec((None, tr, L), lambda k, i, c_ref: (k, i, 0))],
            out_specs=pl.BlockSpec((None, tr, L), lambda k, i, c_ref: (k, i, 0))),
        out_shape=jax.ShapeDtypeStruct((4, R, L), g.dtype),
        compiler_params=_params("parallel", "parallel"),
    )(c_idx, g, buf)


def _chip_exchange(p, *, name):
    _, R, L = p.shape

    def body(p_ref, buf_ref, send_sems, recv_sems, local_sem):
        x, y, c = _place()
        mine = 2 * x + y
        own = pltpu.make_async_copy(p_ref.at[mine], buf_ref.at[mine], local_sem)
        own.start()
        chips = [(1 - x, y), (x, 1 - y), (1 - x, 1 - y)]
        copies = [pltpu.make_async_remote_copy(
            src_ref=p_ref.at[2 * cx + cy], dst_ref=buf_ref.at[mine], send_sem=send_sems.at[j], recv_sem=recv_sems.at[j],
            device_id=(cx, cy, c), device_id_type=MESH) for j, (cx, cy) in enumerate(chips)]
        for cp in copies:
            cp.start()
        for cp in copies:
            cp.wait()
        own.wait()

    return pl.pallas_call(
        body, name=name, in_specs=[ANY], out_specs=ANY,
        out_shape=jax.ShapeDtypeStruct((4, R, L), p.dtype),
        scratch_shapes=[pltpu.SemaphoreType.DMA((3,)), pltpu.SemaphoreType.DMA((3,)), pltpu.SemaphoreType.DMA],
    )(p)


def _sum_slots(buf, *, name):
    n, R, L = buf.shape
    tr = _tile(R, 512, SUBLANES)

    def body(b_ref, o_ref):
        acc = b_ref[0]
        for j in range(1, n):
            acc = acc + b_ref[j]
        o_ref[...] = acc

    return pl.pallas_call(
        body, name=name, grid=(R // tr,),
        in_specs=[pl.BlockSpec((n, tr, L), lambda i: (0, i, 0))],
        out_specs=pl.BlockSpec((tr, L), lambda i: (i, 0)),
        out_shape=jax.ShapeDtypeStruct((R, L), buf.dtype),
        compiler_params=_params("parallel"),
    )(buf)


SHARDED = ("gla_w_in", "gla_w_gate_up_f", "gla_w_gate_up_b", "gla_w_out", "attn_w_qkv", "attn_w_out",
           "ffn_w_up", "ffn_w_conv", "ffn_w_down")
SHARD_AXIS = {"gla_w_in": 2, "gla_w_gate_up_f": 2, "gla_w_gate_up_b": 2, "gla_w_out": 1, "attn_w_qkv": 2,
              "attn_w_out": 1, "ffn_w_up": 2, "ffn_w_conv": 2, "ffn_w_down": 1}
KEPT_F32 = ("ffn_w_conv",)
REPLICATED = ("norm_mix", "norm_ffn", "gla_b_gate_f", "gla_b_gate_b", "gla_norm", "attn_q_norm", "attn_k_norm",
              "ffn_b_conv")
WEIGHTS = ("norm_mix", "norm_ffn", "gla_w_in", "gla_w_gate_up_f", "gla_b_gate_f", "gla_w_gate_up_b", "gla_b_gate_b",
           "gla_norm", "gla_w_out", "attn_w_qkv", "attn_q_norm", "attn_k_norm", "attn_w_out", "ffn_w_up", "ffn_w_conv",
           "ffn_b_conv", "ffn_w_down")


def _rows(flat, row_align):
    n = flat.shape[0]
    per = row_align * LANES
    padded = -(-n // per) * per
    return jnp.pad(flat, (0, padded - n)).reshape(padded // LANES, LANES)


def _pack_weights(shards):
    parts = []
    for name in SHARDED:
        w = shards[name]
        if name in KEPT_F32:
            parts.append(lax.bitcast_convert_type(w, BF16).reshape(-1))
        else:
            parts.append(w.astype(BF16).reshape(-1))
    return _rows(jnp.concatenate(parts), 16)


def _unshard(stacked, axis):
    if axis == 2:
        n, l, a, b = stacked.shape
        return jnp.transpose(stacked, (1, 2, 0, 3)).reshape(l, a, n * b)
    n, l, a, b = stacked.shape
    return jnp.transpose(stacked, (1, 0, 2, 3)).reshape(l, n * a, b)


def _unpack_weights(gathered, shards):
    flat = gathered.reshape(N_DEV, -1)
    full, off = {}, 0
    for name in SHARDED:
        shape = shards[name].shape
        n = 1
        for s in shape:
            n *= s
        if name in KEPT_F32:
            seg = flat[:, off:off + 2 * n].reshape((N_DEV,) + shape + (2,))
            seg = lax.bitcast_convert_type(seg, F32)
            off += 2 * n
        else:
            seg = flat[:, off:off + n].reshape((N_DEV,) + shape)
            off += n
        full[name] = _unshard(seg, SHARD_AXIS[name])
    return full


def _to_shards(full, axis):
    l, a, b = full.shape
    if axis == 2:
        return jnp.transpose(full.reshape(l, a, N_DEV, b // N_DEV), (2, 0, 1, 3))
    return jnp.transpose(full.reshape(l, N_DEV, a // N_DEV, b), (1, 0, 2, 3))


def _pack_grads(grads):
    parts = [_to_shards(grads[name], SHARD_AXIS[name]).reshape(N_DEV, -1) for name in SHARDED]
    flat = jnp.concatenate(parts, axis=1)
    n = flat.shape[1]
    per = SUBLANES * LANES
    padded = -(-n // per) * per
    flat = jnp.pad(flat, ((0, 0), (0, padded - n)))
    return flat.reshape(4, 2, padded // LANES, LANES)


def _unpack_grads(reduced, shards):
    flat = reduced.reshape(-1)
    out, off = {}, 0
    for name in SHARDED:
        shape = shards[name].shape
        n = 1
        for s in shape:
            n *= s
        out[name] = flat[off:off + n].reshape(shape)
        off += n
    return out


def _rope_tables(S):
    rows = S // GRID_W
    pairs = ATTN_HD // 4
    row_idx = jnp.repeat(jnp.arange(rows, dtype=F32), GRID_W)
    col_idx = jnp.tile(jnp.arange(GRID_W, dtype=F32), rows)
    inv_freq = ROPE_THETA ** (-jnp.arange(pairs, dtype=F32) / pairs)
    ang = jnp.concatenate([row_idx[:, None] * inv_freq, col_idx[:, None] * inv_freq], axis=-1)
    cos, sin = jnp.cos(ang), jnp.sin(ang)
    return jnp.concatenate([cos, cos], axis=-1), jnp.concatenate([-sin, sin], axis=-1)


def _gate_matrix(w_f, w_b):
    rk = w_f.shape[0]
    top = jnp.concatenate([w_f, jnp.zeros_like(w_f)], axis=1)
    mid = jnp.concatenate([jnp.zeros_like(w_b), w_b], axis=1)
    pad = jnp.zeros((LANES - 2 * rk, 2 * GLA_KD), w_f.dtype)
    return jnp.concatenate([top, mid, pad], axis=0)


def _local_step(x, target, rep, full):
    S, D = x.shape
    depth = rep["norm_mix"].shape[0]
    F = full["ffn_w_down"].shape[1]
    cs, sn = _rope_tables(S)
    row = lambda a: a.reshape(1, -1)

    saved = []
    for i in range(depth):
        j = i // 2
        sv = {"x0": x}
        h1 = _rmsnorm_fwd(x, row(rep["norm_mix"][i]), name="norm_mix_fwd")
        sv["h1"] = h1
        if i % 2 == 0:
            w_in = jnp.pad(full["gla_w_in"][j], ((0, 0), (0, GLA_PROJ - full["gla_w_in"].shape[2])))
            w_gate = _gate_matrix(full["gla_w_gate_up_f"][j], full["gla_w_gate_up_b"][j])
            bias = jnp.concatenate([rep["gla_b_gate_f"][j], rep["gla_b_gate_b"][j]]).reshape(1, -1)
            proj = _mm(h1, w_in, name="gla_in_proj")
            ranks = proj[:, GLA_PROJ - LANES:]
            logits = _mm(ranks, w_gate, name="gla_gate_logits")
            of, ob, sf, sb = _gla_fwd(proj, logits, bias, name="gla_fwd")
            y = _gla_gate_fwd(of, ob, proj, row(rep["gla_norm"][j]), name="gla_gate_fwd")
            x = _mm(y, full["gla_w_out"][j], res=x, name="gla_out_proj")
            sv.update(w_in=w_in, w_gate=w_gate, bias=bias, proj=proj, ranks=ranks, logits=logits, of=of, ob=ob,
                      sf=sf, sb=sb, y=y)
        else:
            proj = _mm(h1, full["attn_w_qkv"][j], name="attn_qkv_proj")
            qkv = _attn_prep(proj, cs, sn, row(rep["attn_q_norm"][j]), row(rep["attn_k_norm"][j]), name="attn_prep")
            o = _attn_fwd(qkv, name="attn_fwd")
            x = _mm(o, full["attn_w_out"][j], res=x, name="attn_out_proj")
            sv.update(proj=proj, qkv=qkv, o=o)
        sv["x1"] = x
        h2 = _rmsnorm_fwd(x, row(rep["norm_ffn"][i]), name="norm_ffn_fwd")
        wu, wc, bc = full["ffn_w_up"][i], full["ffn_w_conv"][i], rep["ffn_b_conv"][i]
        wuv, wug, wcv, wcg, bcv, bcg = wu[:, :F], wu[:, F:], wc[:, :F], wc[:, F:], row(bc[:F]), row(bc[F:])
        uv = _mm(h2, wuv, name="ffn_up_val")
        ug = _mm(h2, wug, name="ffn_up_gate")
        act = _conv_act_fwd(uv, ug, wcv, wcg, bcv, bcg, name="ffn_conv_act")
        x = _mm(act, full["ffn_w_down"][i], res=x, name="ffn_down")
        sv.update(h2=h2, uv=uv, ug=ug, act=act, wuv=wuv, wug=wug, wcv=wcv, wcg=wcg, bcv=bcv, bcg=bcg)
        saved.append(sv)

    dx, loss_tile = _loss_head(x, target, name="loss_head")
    loss = loss_tile[0, 0]

    gl = {k: [None] * depth for k in ("norm_mix", "norm_ffn", "ffn_w_up", "ffn_w_conv", "ffn_b_conv", "ffn_w_down")}
    gm = {k: [None] * (depth // 2) for k in ("gla_w_in", "gla_w_gate_up_f", "gla_b_gate_f", "gla_w_gate_up_b",
                                             "gla_b_gate_b", "gla_norm", "gla_w_out", "attn_w_qkv", "attn_q_norm",
                                             "attn_k_norm", "attn_w_out")}
    rk = GLA_GATE_RANK
    for i in reversed(range(depth)):
        j = i // 2
        sv = saved[i]
        dact = _mm(dx, full["ffn_w_down"][i], tb=True, name="ffn_down_dgrad")
        gl["ffn_w_down"][i] = _mm(sv["act"], dx, ta=True, name="ffn_down_wgrad")
        dcv, dcg, dwv, dwg, dbv, dbg = _conv_act_bwd(sv["uv"], sv["ug"], sv["wcv"], sv["wcg"], sv["bcv"], sv["bcg"],
                                                     dact, name="ffn_conv_act_bwd")
        gl["ffn_w_conv"][i] = jnp.concatenate([dwv, dwg], axis=1)
        gl["ffn_b_conv"][i] = jnp.concatenate([dbv, dbg], axis=1)[0]
        duv = _conv_t(dcv, sv["wcv"], name="ffn_conv_t")
        dug = _conv_t(dcg, sv["wcg"], name="ffn_conv_t")
        dh2 = _mm(duv, sv["wuv"], tb=True, name="ffn_up_dgrad_val")
        dh2 = _mm(dug, sv["wug"], tb=True, res=dh2, name="ffn_up_dgrad_gate")
        gl["ffn_w_up"][i] = jnp.concatenate([_mm(sv["h2"], duv, ta=True, name="ffn_up_wgrad"),
                                             _mm(sv["h2"], dug, ta=True, name="ffn_up_wgrad")], axis=1)
        dx, dn = _rmsnorm_bwd(sv["x1"], row(rep["norm_ffn"][i]), dh2, dx, name="norm_ffn_bwd")
        gl["norm_ffn"][i] = dn[0]
        if i % 2 == 0:
            dy = _mm(dx, full["gla_w_out"][j], tb=True, name="gla_out_dgrad")
            gm["gla_w_out"][j] = _mm(sv["y"], dx, ta=True, name="gla_out_wgrad")
            do, dg, dgn = _gla_gate_bwd(sv["of"], sv["ob"], sv["proj"], row(rep["gla_norm"][j]), dy, name="gla_gate_bwd")
            gm["gla_norm"][j] = dgn[0]
            dqkv_f, dlg_f, dqkv_b, dlg_b, dbias = _gla_bwd(sv["proj"], sv["logits"], sv["bias"], sv["sf"], sv["sb"], do,
                                                           name="gla_bwd")
            gm["gla_b_gate_f"][j] = dbias[0, :GLA_KD]
            gm["gla_b_gate_b"][j] = dbias[0, GLA_KD:]
            dlogits = jnp.concatenate([dlg_f, dlg_b], axis=1)
            dr = _mm(dlogits, sv["w_gate"], tb=True, name="gla_gate_dgrad")
            dwg_full = _mm(sv["ranks"], dlogits, ta=True, name="gla_gate_wgrad")
            gm["gla_w_gate_up_f"][j] = dwg_full[:rk, :GLA_KD]
            gm["gla_w_gate_up_b"][j] = dwg_full[rk:2 * rk, GLA_KD:]
            dproj = _gla_combine(dqkv_f, dqkv_b, dg, dr, name="gla_combine")
            dh1 = _mm(dproj, sv["w_in"], tb=True, name="gla_in_dgrad")
            gm["gla_w_in"][j] = _mm(sv["h1"], dproj, ta=True, name="gla_in_wgrad")[:, :full["gla_w_in"].shape[2]]
        else:
            do = _mm(dx, full["attn_w_out"][j], tb=True, out_dtype=BF16, name="attn_out_dgrad")
            gm["attn_w_out"][j] = _mm(sv["o"], dx, ta=True, name="attn_out_wgrad")
            dq, dk, dv = _attn_bwd(sv["qkv"], do, name="attn_bwd")
            dproj, dqn, dkn = _attn_prep_bwd(sv["proj"], dq, dk, dv, cs, sn, row(rep["attn_q_norm"][j]),
                                             row(rep["attn_k_norm"][j]), name="attn_prep_bwd")
            gm["attn_q_norm"][j] = dqn[0]
            gm["attn_k_norm"][j] = dkn[0]
            dh1 = _mm(dproj, full["attn_w_qkv"][j], tb=True, name="attn_qkv_dgrad")
            gm["attn_w_qkv"][j] = _mm(sv["h1"], dproj, ta=True, name="attn_qkv_wgrad")
        dx, dn = _rmsnorm_bwd(sv["x0"], row(rep["norm_mix"][i]), dh1, dx, name="norm_mix_bwd")
        gl["norm_mix"][i] = dn[0]

    grads = {k: jnp.stack(v) for k, v in {**gl, **gm}.items()}
    return loss, dx, grads


def kernel(x, norm_mix, norm_ffn, gla_w_in, gla_w_gate_up_f, gla_b_gate_f, gla_w_gate_up_b, gla_b_gate_b, gla_norm, gla_w_out, attn_w_qkv, attn_q_norm, attn_k_norm, attn_w_out, ffn_w_up, ffn_w_conv, ffn_b_conv, ffn_w_down, loss_target, m_norm_mix, m_norm_ffn, m_gla_w_in, m_gla_w_gate_up_f, m_gla_b_gate_f, m_gla_w_gate_up_b, m_gla_b_gate_b, m_gla_norm, m_gla_w_out, m_attn_w_qkv, m_attn_q_norm, m_attn_k_norm, m_attn_w_out, m_ffn_w_up, m_ffn_w_conv, m_ffn_b_conv, m_ffn_w_down, v_norm_mix, v_norm_ffn, v_gla_w_in, v_gla_w_gate_up_f, v_gla_b_gate_f, v_gla_w_gate_up_b, v_gla_b_gate_b, v_gla_norm, v_gla_w_out, v_attn_w_qkv, v_attn_q_norm, v_attn_k_norm, v_attn_w_out, v_ffn_w_up, v_ffn_w_conv, v_ffn_b_conv, v_ffn_w_down):
    given = dict(locals())
    w = {n: given[n] for n in WEIGHTS}
    m = {n: given["m_" + n] for n in WEIGHTS}
    v = {n: given["v_" + n] for n in WEIGHTS}
    shards = {n: w[n] for n in SHARDED}
    rep = {n: w[n] for n in REPLICATED}

    gathered = _all_gather(_pack_weights(shards), name="weights_all_gather")
    full = _unpack_weights(gathered, shards)

    loss_local, grad_x, grads = _local_step(x[0], loss_target[0], rep, full)
    loss = lax.psum(loss_local, ("x", "y", "c"))

    packed = _pack_grads(grads)
    c_idx = lax.axis_index("c").astype(jnp.int32).reshape(1)
    from_sibling = _sibling_exchange(packed, name="grads_sibling_exchange")
    chip_sums = _add_sibling(packed, from_sibling, c_idx, name="grads_add_sibling")
    from_chips = _chip_exchange(chip_sums, name="grads_chip_exchange")
    reduced = _sum_slots(from_chips, name="grads_sum_chips")
    g = _unpack_grads(reduced, shards)

    rep_flat = _rows(jnp.concatenate([grads[n].reshape(-1) for n in REPLICATED]), SUBLANES)
    rep_sum = _sum_slots(_all_gather(rep_flat, name="small_grads_all_gather"), name="small_grads_sum").reshape(-1)
    off = 0
    for n in REPLICATED:
        g[n] = rep_sum[off:off + w[n].size].reshape(w[n].shape)
        off += w[n].size

    delta, new_m, new_v = {}, {}, {}
    for n in WEIGHTS:
        shape = w[n].shape
        two_d = (-1, shape[-1])
        d2, m2, v2 = _adamw(w[n].reshape(two_d), g[n].reshape(two_d), m[n].reshape(two_d), v[n].reshape(two_d),
                            name="adamw_" + n)
        delta[n], new_m[n], new_v[n] = d2.reshape(shape), m2.reshape(shape), v2.reshape(shape)

    return (loss, grad_x[None], *[g[n] for n in WEIGHTS], *[delta[n] for n in WEIGHTS],
            *[new_m[n] for n in WEIGHTS], *[new_v[n] for n in WEIGHTS])
```

```python
import math

import jax
import jax.numpy as jnp
from jax import lax
from jax.experimental import pallas as pl
from jax.experimental.pallas import tpu as pltpu

F32 = jnp.float32
BF16 = jnp.bfloat16
MESH = pl.DeviceIdType.MESH

N_DEV = 8
LANES = 128
SUBLANES = 8
VMEM_LIMIT = 56 * 1024 * 1024

NORM_EPS = 1e-6
GRID_W = 64
ROPE_THETA = 10000.0
GLA_HEADS = 4
GLA_DK = 128
GLA_DV = 256
GLA_CHUNK = 64
GLA_GATE_RANK = 16
GLA_GATE_NORMALIZER = 16.0
ATTN_HD = 128
ATTN_Q_HEADS = 8
ATTN_KV_HEADS = 2
ATTN_GROUP = ATTN_Q_HEADS // ATTN_KV_HEADS

ADAM_LR = 0.001
ADAM_B1 = 0.9
ADAM_B2 = 0.999
ADAM_EPS = 1e-08
ADAM_WD = 0.01
ADAM_STEP = 10


def _tile(n, target, align=LANES):
    if n <= target:
        return n
    t = (target // align) * align
    while t >= align:
        if n % t == 0:
            return t
        t -= align
    return n


def _params(*sem):
    return pltpu.CompilerParams(dimension_semantics=sem, vmem_limit_bytes=VMEM_LIMIT)


def _dot(a, b):
    return lax.dot_general(a, b, (((1,), (0,)), ((), ())), preferred_element_type=F32)


def _dot_nt(a, b):
    return lax.dot_general(a, b, (((1,), (1,)), ((), ())), preferred_element_type=F32)


def _dot_tn(a, b):
    return lax.dot_general(a, b, (((0,), (0,)), ((), ())), preferred_element_type=F32)


def _sigmoid(x):
    return 1.0 / (1.0 + jnp.exp(-x))


def _log_sigmoid(x):
    return jnp.minimum(x, 0.0) - jnp.log(1.0 + jnp.exp(-jnp.abs(x)))


def _colsum(x):
    return jnp.sum(x, axis=0, keepdims=True)


def _mm(a, b, *, ta=False, tb=False, res=None, out_dtype=F32, layer=None, a_cols=None, name):
    if tb:
        N, K = b.shape[-2:]
    else:
        K, N = b.shape[-2:]
    a_rows, a_width = a.shape
    a_off = 0
    if a_cols is not None:
        a_off, a_width = a_cols
    if ta:
        M = a_width
        assert a_rows == K, (a.shape, b.shape, ta, tb)
    else:
        M = a_rows
        assert a_width == K, (a.shape, b.shape, ta, tb)
    tm = _tile(M, 1408) if ta else _tile(M, 512, 16)
    tn = _tile(N, 1408)
    tk = _tile(K, 512, 16) if ta else _tile(K, 1408)
    nk = K // tk
    dims = (((0 if ta else 1,), (1 if tb else 0,)), ((), ()))

    def body(*refs):
        if res is not None:
            a_ref, b_ref, r_ref, o_ref = refs[:4]
            scr = refs[4:]
        else:
            a_ref, b_ref, o_ref = refs[:3]
            r_ref = None
            scr = refs[3:]
        part = lax.dot_general(a_ref[...].astype(BF16), b_ref[...].astype(BF16), dims, preferred_element_type=F32)

        def finish(acc):
            if r_ref is not None:
                acc = acc + r_ref[...]
            o_ref[...] = acc.astype(out_dtype)

        if nk == 1:
            finish(part)
        else:
            acc_ref = scr[0]
            k = pl.program_id(2)

            @pl.when(k == 0)
            def _():
                acc_ref[...] = part

            @pl.when(k > 0)
            def _():
                acc_ref[...] += part

            @pl.when(k == nk - 1)
            def _():
                finish(acc_ref[...])

    a_blk = a_off // (tm if ta else tk)
    assert a_off % (tm if ta else tk) == 0
    a_spec = (pl.BlockSpec((tk, tm), lambda i, j, k: (k, a_blk + i)) if ta
              else pl.BlockSpec((tm, tk), lambda i, j, k: (i, a_blk + k)))
    if layer is None:
        b_spec = pl.BlockSpec((tn, tk), lambda i, j, k: (j, k)) if tb else pl.BlockSpec((tk, tn), lambda i, j, k: (k, j))
    else:
        b_spec = (pl.BlockSpec((None, tn, tk), lambda i, j, k: (layer, j, k)) if tb
                  else pl.BlockSpec((None, tk, tn), lambda i, j, k: (layer, k, j)))
    o_spec = pl.BlockSpec((tm, tn), lambda i, j, k: (i, j))
    in_specs = [a_spec, b_spec] + ([o_spec] if res is not None else [])
    args = (a, b) + ((res,) if res is not None else ())
    return pl.pallas_call(
        body, name=name, grid=(M // tm, N // tn, nk),
        in_specs=in_specs, out_specs=o_spec,
        out_shape=jax.ShapeDtypeStruct((M, N), out_dtype),
        scratch_shapes=[pltpu.VMEM((tm, tn), F32)] if nk > 1 else [],
        compiler_params=_params("parallel", "parallel", "arbitrary"),
    )(*args)


def _wgrad(a, b, stack, layer, *, shard, group=0, name):
    S, N = b.shape
    M = a.shape[1]
    As, Bs = stack.shape[-2:]
    tk = _tile(S, 512, 16)
    nk = S // tk
    if shard == "cols":
        n = N // Bs
        tm = _tile(M, 512)
        tn = N
        grid = (M // tm, 1, nk)
        o_spec = pl.BlockSpec((n, None, tm, Bs), lambda i, j, k: (group, layer, i, 0))
    else:
        per = As * LANES // math.gcd(As, LANES)
        tm = M if M <= 1408 else _tile(M, 1408, per)
        n = tm // As
        tn = _tile(N, 1024)
        grid = (M // tm, N // tn, nk)
        o_spec = pl.BlockSpec((n, None, As, tn), lambda i, j, k: (i, layer, 0, j))

    def body(a_ref, b_ref, _, o_ref, acc_ref):
        k = pl.program_id(2)
        part = _dot_tn(a_ref[...].astype(BF16), b_ref[...].astype(BF16))

        @pl.when(k == 0)
        def _():
            acc_ref[...] = part

        @pl.when(k > 0)
        def _():
            acc_ref[...] += part

        @pl.when(k == nk - 1)
        def _():
            for q in range(n):
                if shard == "cols":
                    o_ref[q] = acc_ref[:, q * Bs:(q + 1) * Bs]
                else:
                    o_ref[q] = acc_ref[q * As:(q + 1) * As, :]

    return pl.pallas_call(
        body, name=name, grid=grid,
        in_specs=[pl.BlockSpec((tk, tm), lambda i, j, k: (k, i)), pl.BlockSpec((tk, tn), lambda i, j, k: (k, j)),
                  pl.BlockSpec(memory_space=pl.ANY)],
        out_specs=o_spec,
        out_shape=jax.ShapeDtypeStruct(stack.shape, stack.dtype),
        input_output_aliases={2: 0},
        scratch_shapes=[pltpu.VMEM((tm, tn), F32)],
        compiler_params=_params("parallel", "parallel", "arbitrary"),
    )(a, b, stack)


def _rmsnorm_fwd(x, w, *, name):
    S, D = x.shape
    ts = _tile(S, 512, 16)

    def body(x_ref, w_ref, o_ref):
        xv = x_ref[...]
        r = lax.rsqrt(jnp.mean(xv * xv, axis=-1, keepdims=True) + NORM_EPS)
        o_ref[...] = (xv * r * w_ref[...]).astype(BF16)

    return pl.pallas_call(
        body, name=name, grid=(S // ts,),
        in_specs=[pl.BlockSpec((ts, D), lambda i: (i, 0)), pl.BlockSpec((1, D), lambda i: (0, 0))],
        out_specs=pl.BlockSpec((ts, D), lambda i: (i, 0)),
        out_shape=jax.ShapeDtypeStruct((S, D), BF16),
        compiler_params=_params("parallel"),
    )(x, w)


def _rmsnorm_bwd(x, w, dh, dres, *, name):
    S, D = x.shape
    ts = _tile(S, 512, 16)
    n = S // ts

    def body(x_ref, w_ref, dh_ref, dr_ref, dx_ref, dw_ref):
        i = pl.program_id(0)
        xv = x_ref[...]
        r = lax.rsqrt(jnp.mean(xv * xv, axis=-1, keepdims=True) + NORM_EPS)
        xh = xv * r
        d = dh_ref[...]
        g = d * w_ref[...]
        dx_ref[...] = dr_ref[...] + r * (g - xh * jnp.mean(g * xh, axis=-1, keepdims=True))
        part = _colsum(d * xh)

        @pl.when(i == 0)
        def _():
            dw_ref[...] = part

        @pl.when(i > 0)
        def _():
            dw_ref[...] += part

    row = pl.BlockSpec((ts, D), lambda i: (i, 0))
    vec = pl.BlockSpec((1, D), lambda i: (0, 0))
    return pl.pallas_call(
        body, name=name, grid=(n,),
        in_specs=[row, vec, row, row], out_specs=[row, vec],
        out_shape=[jax.ShapeDtypeStruct((S, D), F32), jax.ShapeDtypeStruct((1, D), F32)],
        compiler_params=_params("arbitrary"),
    )(x, w, dh, dres)


def _halo_specs(S, ts, tf, row_axis):
    g = ts // SUBLANES
    last = S // SUBLANES - 1
    col_axis = 1 - row_axis
    main = pl.BlockSpec((ts, tf), lambda *ij: (ij[row_axis], ij[col_axis]))
    prev = pl.BlockSpec((SUBLANES, tf), lambda *ij: (jnp.maximum(ij[row_axis] * g - 1, 0), ij[col_axis]))
    nxt = pl.BlockSpec((SUBLANES, tf), lambda *ij: (jnp.minimum((ij[row_axis] + 1) * g, last), ij[col_axis]))
    return [main, prev, nxt]


def _shifted(u, prev_ref, next_ref, i, n):
    ts = u.shape[0]
    rid = lax.broadcasted_iota(jnp.int32, u.shape, 0)
    before = jnp.where(i > 0, prev_ref[SUBLANES - 1:SUBLANES, :], 0.0)
    after = jnp.where(i < n - 1, next_ref[0:1, :], 0.0)
    um1 = jnp.where(rid == 0, before, pltpu.roll(u, 1, 0))
    up1 = jnp.where(rid == ts - 1, after, pltpu.roll(u, ts - 1, 0))
    return um1, up1


def _conv3(u, prev_ref, next_ref, w_ref, i, n):
    um1, up1 = _shifted(u, prev_ref, next_ref, i, n)
    return w_ref[0:1, :] * um1 + w_ref[1:2, :] * u + w_ref[2:3, :] * up1


def _conv_act_fwd(uv, ug, wv, wg, bv, bg, *, name):
    S, F = uv.shape
    ts = _tile(S, 512, 16)
    tf = _tile(F, 1408)
    n = S // ts

    def body(v_ref, vp_ref, vn_ref, g_ref, gp_ref, gn_ref, wv_ref, wg_ref, bv_ref, bg_ref, o_ref):
        i = pl.program_id(0)
        val = _conv3(v_ref[...], vp_ref, vn_ref, wv_ref, i, n) + bv_ref[...]
        gate = _conv3(g_ref[...], gp_ref, gn_ref, wg_ref, i, n) + bg_ref[...]
        o_ref[...] = (gate * _sigmoid(gate) * val).astype(BF16)

    halo = _halo_specs(S, ts, tf, 0)
    w3 = pl.BlockSpec((3, tf), lambda i, j: (0, j))
    b1 = pl.BlockSpec((1, tf), lambda i, j: (0, j))
    return pl.pallas_call(
        body, name=name, grid=(n, F // tf),
        in_specs=halo + halo + [w3, w3, b1, b1],
        out_specs=pl.BlockSpec((ts, tf), lambda i, j: (i, j)),
        out_shape=jax.ShapeDtypeStruct((S, F), BF16),
        compiler_params=_params("parallel", "parallel"),
    )(uv, uv, uv, ug, ug, ug, wv, wg, bv, bg)


def _conv_act_bwd(uv, ug, wv, wg, bv, bg, dact, *, name):
    S, F = uv.shape
    ts = _tile(S, 512, 16)
    tf = _tile(F, 1408)
    n = S // ts

    def body(v_ref, vp_ref, vn_ref, g_ref, gp_ref, gn_ref, wv_ref, wg_ref, bv_ref, bg_ref, da_ref,
             dv_ref, dg_ref, dwv_ref, dwg_ref, dbv_ref, dbg_ref):
        i = pl.program_id(1)
        uvv, ugv = v_ref[...], g_ref[...]
        vm1, vp1 = _shifted(uvv, vp_ref, vn_ref, i, n)
        gm1, gp1 = _shifted(ugv, gp_ref, gn_ref, i, n)
        val = wv_ref[0:1, :] * vm1 + wv_ref[1:2, :] * uvv + wv_ref[2:3, :] * vp1 + bv_ref[...]
        gate = wg_ref[0:1, :] * gm1 + wg_ref[1:2, :] * ugv + wg_ref[2:3, :] * gp1 + bg_ref[...]
        sg = _sigmoid(gate)
        da = da_ref[...]
        dval = da * (gate * sg)
        dgate = da * val * (sg * (1.0 + gate * (1.0 - sg)))
        dv_ref[...] = dval
        dg_ref[...] = dgate
        sums = [(dwv_ref, 0, vm1 * dval), (dwv_ref, 1, uvv * dval), (dwv_ref, 2, vp1 * dval),
                (dwg_ref, 0, gm1 * dgate), (dwg_ref, 1, ugv * dgate), (dwg_ref, 2, gp1 * dgate),
                (dbv_ref, 0, dval), (dbg_ref, 0, dgate)]
        parts = [(ref, r, _colsum(t)) for ref, r, t in sums]

        @pl.when(i == 0)
        def _():
            for ref, r, part in parts:
                ref[r:r + 1, :] = part

        @pl.when(i > 0)
        def _():
            for ref, r, part in parts:
                ref[r:r + 1, :] += part

    halo = _halo_specs(S, ts, tf, 1)
    w3 = pl.BlockSpec((3, tf), lambda j, i: (0, j))
    b1 = pl.BlockSpec((1, tf), lambda j, i: (0, j))
    blk = pl.BlockSpec((ts, tf), lambda j, i: (i, j))
    return pl.pallas_call(
        body, name=name, grid=(F // tf, n),
        in_specs=halo + halo + [w3, w3, b1, b1, blk],
        out_specs=[blk, blk, w3, w3, b1, b1],
        out_shape=[jax.ShapeDtypeStruct((S, F), F32), jax.ShapeDtypeStruct((S, F), F32),
                   jax.ShapeDtypeStruct((3, F), F32), jax.ShapeDtypeStruct((3, F), F32),
                   jax.ShapeDtypeStruct((1, F), F32), jax.ShapeDtypeStruct((1, F), F32)],
        compiler_params=_params("parallel", "arbitrary"),
    )(uv, uv, uv, ug, ug, ug, wv, wg, bv, bg, dact)


def _conv_t(duc, w, *, name):
    S, F = duc.shape
    ts = _tile(S, 512, 16)
    tf = _tile(F, 1408)
    n = S // ts

    def body(d_ref, dp_ref, dn_ref, w_ref, o_ref):
        i = pl.program_id(0)
        d = d_ref[...]
        dm1, dp1 = _shifted(d, dp_ref, dn_ref, i, n)
        o_ref[...] = (w_ref[0:1, :] * dp1 + w_ref[1:2, :] * d + w_ref[2:3, :] * dm1).astype(BF16)

    return pl.pallas_call(
        body, name=name, grid=(n, F // tf),
        in_specs=_halo_specs(S, ts, tf, 0) + [pl.BlockSpec((3, tf), lambda i, j: (0, j))],
        out_specs=pl.BlockSpec((ts, tf), lambda i, j: (i, j)),
        out_shape=jax.ShapeDtypeStruct((S, F), BF16),
        compiler_params=_params("parallel", "parallel"),
    )(duc, duc, duc, w)


N_QK = ATTN_Q_HEADS + ATTN_KV_HEADS
QKV_DIM = (ATTN_Q_HEADS + 2 * ATTN_KV_HEADS) * ATTN_HD


def _head(hd):
    return slice(hd * ATTN_HD, (hd + 1) * ATTN_HD)


def _attn_prep(proj, cs, sn, qn, kn, *, name):
    S = proj.shape[0]
    ts = _tile(S, 256, 16)

    def body(p_ref, c_ref, s_ref, qn_ref, kn_ref, o_ref):
        c, s = c_ref[...], s_ref[...]
        for hd in range(N_QK):
            xv = p_ref[:, _head(hd)]
            w = qn_ref[...] if hd < ATTN_Q_HEADS else kn_ref[...]
            r = lax.rsqrt(jnp.mean(xv * xv, axis=-1, keepdims=True) + NORM_EPS)
            nrm = xv * r * w
            o_ref[:, _head(hd)] = (nrm * c + pltpu.roll(nrm, ATTN_HD // 2, 1) * s).astype(BF16)
        o_ref[:, N_QK * ATTN_HD:] = p_ref[:, N_QK * ATTN_HD:].astype(BF16)

    row = pl.BlockSpec((ts, QKV_DIM), lambda i: (i, 0))
    rot = pl.BlockSpec((ts, ATTN_HD), lambda i: (i, 0))
    vec = pl.BlockSpec((1, ATTN_HD), lambda i: (0, 0))
    return pl.pallas_call(
        body, name=name, grid=(S // ts,),
        in_specs=[row, rot, rot, vec, vec], out_specs=row,
        out_shape=jax.ShapeDtypeStruct((S, QKV_DIM), BF16),
        compiler_params=_params("parallel"),
    )(proj, cs, sn, qn, kn)


def _attn_prep_bwd(proj, dq, dk, dv, cs, sn, qn, kn, *, name):
    S = proj.shape[0]
    ts = _tile(S, 256, 16)
    nq = ATTN_Q_HEADS * ATTN_HD
    nkv = ATTN_KV_HEADS * ATTN_HD

    def body(p_ref, dq_ref, dk_ref, dv_ref, c_ref, s_ref, qn_ref, kn_ref, o_ref, dqn_ref, dkn_ref):
        i = pl.program_id(0)
        c, s = c_ref[...], s_ref[...]
        acc = [jnp.zeros((1, ATTN_HD), F32), jnp.zeros((1, ATTN_HD), F32)]
        for hd in range(N_QK):
            is_k = hd >= ATTN_Q_HEADS
            xv = p_ref[:, _head(hd)]
            w = kn_ref[...] if is_k else qn_ref[...]
            r = lax.rsqrt(jnp.mean(xv * xv, axis=-1, keepdims=True) + NORM_EPS)
            xh = xv * r
            dout = dk_ref[:, _head(hd - ATTN_Q_HEADS)] if is_k else dq_ref[:, _head(hd)]
            dn = dout * c + pltpu.roll(dout * s, ATTN_HD // 2, 1)
            acc[int(is_k)] = acc[int(is_k)] + _colsum(dn * xh)
            g = dn * w
            o_ref[:, _head(hd)] = (r * (g - xh * jnp.mean(g * xh, axis=-1, keepdims=True))).astype(BF16)
        o_ref[:, N_QK * ATTN_HD:] = dv_ref[...].astype(BF16)

        @pl.when(i == 0)
        def _():
            dqn_ref[...] = acc[0]
            dkn_ref[...] = acc[1]

        @pl.when(i > 0)
        def _():
            dqn_ref[...] += acc[0]
            dkn_ref[...] += acc[1]

    row = pl.BlockSpec((ts, QKV_DIM), lambda i: (i, 0))
    rot = pl.BlockSpec((ts, ATTN_HD), lambda i: (i, 0))
    vec = pl.BlockSpec((1, ATTN_HD), lambda i: (0, 0))
    return pl.pallas_call(
        body, name=name, grid=(S // ts,),
        in_specs=[row, pl.BlockSpec((ts, nq), lambda i: (i, 0)), pl.BlockSpec((ts, nkv), lambda i: (i, 0)),
                  pl.BlockSpec((ts, nkv), lambda i: (i, 0)), rot, rot, vec, vec],
        out_specs=[row, vec, vec],
        out_shape=[jax.ShapeDtypeStruct((S, QKV_DIM), BF16), jax.ShapeDtypeStruct((1, ATTN_HD), F32),
                   jax.ShapeDtypeStruct((1, ATTN_HD), F32)],
        compiler_params=_params("arbitrary"),
    )(proj, dq, dk, dv, cs, sn, qn, kn)


def _softmax_rows(q, k):
    s = _dot_nt(q, k) * (ATTN_HD ** -0.5)
    e = jnp.exp(s - jnp.max(s, axis=-1, keepdims=True))
    return e * (1.0 / jnp.sum(e, axis=-1, keepdims=True))


def _attn_fwd(qkv, *, name):
    S = qkv.shape[0]
    tq = _tile(S, 256, 16)

    def body(q_ref, k_ref, v_ref, o_ref):
        p = _softmax_rows(q_ref[...], k_ref[...])
        o_ref[...] = _dot(p.astype(BF16), v_ref[...]).astype(BF16)

    return pl.pallas_call(
        body, name=name, grid=(ATTN_Q_HEADS, S // tq),
        in_specs=[pl.BlockSpec((tq, ATTN_HD), lambda h, i: (i, h)),
                  pl.BlockSpec((S, ATTN_HD), lambda h, i: (0, ATTN_Q_HEADS + h // ATTN_GROUP)),
                  pl.BlockSpec((S, ATTN_HD), lambda h, i: (0, N_QK + h // ATTN_GROUP))],
        out_specs=pl.BlockSpec((tq, ATTN_HD), lambda h, i: (i, h)),
        out_shape=jax.ShapeDtypeStruct((S, ATTN_Q_HEADS * ATTN_HD), BF16),
        compiler_params=_params("parallel", "parallel"),
    )(qkv, qkv, qkv)


def _attn_bwd(qkv, do, *, name):
    S = qkv.shape[0]
    tq = _tile(S, 256, 16)

    def body(q_ref, k_ref, v_ref, do_ref, dq_ref, dk_ref, dv_ref):
        first = jnp.logical_and(pl.program_id(1) == 0, pl.program_id(2) == 0)
        q, k, dov = q_ref[...], k_ref[...], do_ref[...]
        p = _softmax_rows(q, k)
        dp = _dot_nt(dov, v_ref[...])
        ds = p * (dp - jnp.sum(p * dp, axis=-1, keepdims=True)) * (ATTN_HD ** -0.5)
        dsb = ds.astype(BF16)
        dq_ref[...] = _dot(dsb, k)
        dk = _dot_tn(dsb, q)
        dv = _dot_tn(p.astype(BF16), dov)

        @pl.when(first)
        def _():
            dk_ref[...] = dk
            dv_ref[...] = dv

        @pl.when(jnp.logical_not(first))
        def _():
            dk_ref[...] += dk
            dv_ref[...] += dv

    qblk = pl.BlockSpec((tq, ATTN_HD), lambda kv, g, i: (i, kv * ATTN_GROUP + g))
    kvacc = pl.BlockSpec((S, ATTN_HD), lambda kv, g, i: (0, kv))
    return pl.pallas_call(
        body, name=name, grid=(ATTN_KV_HEADS, ATTN_GROUP, S // tq),
        in_specs=[qblk,
                  pl.BlockSpec((S, ATTN_HD), lambda kv, g, i: (0, ATTN_Q_HEADS + kv)),
                  pl.BlockSpec((S, ATTN_HD), lambda kv, g, i: (0, N_QK + kv)),
                  qblk],
        out_specs=[qblk, kvacc, kvacc],
        out_shape=[jax.ShapeDtypeStruct((S, ATTN_Q_HEADS * ATTN_HD), F32),
                   jax.ShapeDtypeStruct((S, ATTN_KV_HEADS * ATTN_HD), F32),
                   jax.ShapeDtypeStruct((S, ATTN_KV_HEADS * ATTN_HD), F32)],
        compiler_params=_params("parallel", "arbitrary", "arbitrary"),
    )(qkv, qkv, qkv, do)


GLA_KD = GLA_HEADS * GLA_DK
GLA_VD = GLA_HEADS * GLA_DV
GLA_PROJ = 2 * GLA_KD + 2 * GLA_VD + LANES
GLA_SCALE = GLA_DK ** -0.5


def _split3(x):
    hi = x.astype(BF16)
    r1 = x - hi.astype(F32)
    mid = r1.astype(BF16)
    lo = (r1 - mid.astype(F32)).astype(BF16)
    return hi, mid, lo


def _cumdot(t, x):
    hi, mid, lo = _split3(x)
    return _dot(t, hi) + _dot(t, mid) + _dot(t, lo)


def _gla_masks(d):
    c = GLA_CHUNK
    row = lax.broadcasted_iota(jnp.int32, (c, c), 0)
    col = lax.broadcasted_iota(jnp.int32, (c, c), 1)
    lower, upper = col <= row, col >= row
    if d == 0:
        return lower.astype(BF16), upper.astype(BF16), lower
    return upper.astype(BF16), lower.astype(BF16), col > row


def _gla_decay(lg, bias, cum, d):
    xl = lg + bias
    la = _log_sigmoid(xl) * (1.0 / GLA_GATE_NORMALIZER)
    b = _cumdot(cum, la)
    b_end = b[GLA_CHUNK - 1:GLA_CHUNK, :] if d == 0 else b[0:1, :]
    return xl, b, b_end


def _gla_specs(S, n):
    c = GLA_CHUNK
    up = lambda i: i
    down = lambda i: n - 1 - i
    def specs(order):
        return dict(
            q=pl.BlockSpec((c, GLA_KD), lambda i: (order(i), 0)),
            k=pl.BlockSpec((c, GLA_KD), lambda i: (order(i), 1)),
            v=pl.BlockSpec((c, GLA_VD), lambda i: (order(i), 1)),
            st=pl.BlockSpec((1, GLA_HEADS, GLA_DV, GLA_DK), lambda i: (order(i), 0, 0, 0)),
            wide=pl.BlockSpec((c, GLA_VD), lambda i: (order(i), 0)),
            qkv=pl.BlockSpec((c, 2 * GLA_KD + GLA_VD), lambda i: (order(i), 0)),
        )
    return specs(up), specs(down), up, down


def _gla_fwd(proj, logits, bias, *, name):
    S = proj.shape[0]
    c = GLA_CHUNK
    n = S // c
    su, sd, up, down = _gla_specs(S, n)

    def body(qf, kf, vf, lf, qb, kb, vb, lb, bias_ref, of, ob, sf, sb, st):
        @pl.when(pl.program_id(0) == 0)
        def _():
            st[...] = jnp.zeros_like(st)

        for d, (q_r, k_r, v_r, l_r, o_r, s_r) in enumerate(((qf, kf, vf, lf, of, sf), (qb, kb, vb, lb, ob, sb))):
            cum, _, mask = _gla_masks(d)
            _, b, b_end = _gla_decay(l_r[...], bias_ref[:, d * GLA_KD:(d + 1) * GLA_KD], cum, d)
            dend = jnp.exp(b_end)
            k = k_r[...]
            qd = (q_r[...] * GLA_SCALE * jnp.exp(b)).astype(BF16)
            ki = (k * jnp.exp(-b)).astype(BF16)
            ke = (k * jnp.exp(b_end - b)).astype(BF16)
            for h in range(GLA_HEADS):
                ks = slice(h * GLA_DK, (h + 1) * GLA_DK)
                vs = slice(h * GLA_DV, (h + 1) * GLA_DV)
                stp = st[d * GLA_HEADS + h]
                s_r[0, h] = stp
                v = v_r[:, vs].astype(BF16)
                att = jnp.where(mask, _dot_nt(qd[:, ks], ki[:, ks]), 0.0).astype(BF16)
                o_r[:, vs] = _dot(att, v) + _dot_nt(qd[:, ks], stp.astype(BF16))
                st[d * GLA_HEADS + h] = stp * dend[:, ks] + _dot_tn(v, ke[:, ks])

    lg_f = pl.BlockSpec((c, GLA_KD), lambda i: (up(i), 0))
    lg_b = pl.BlockSpec((c, GLA_KD), lambda i: (down(i), 1))
    return pl.pallas_call(
        body, name=name, grid=(n,),
        in_specs=[su["q"], su["k"], su["v"], lg_f, sd["q"], sd["k"], sd["v"], lg_b,
                  pl.BlockSpec((1, 2 * GLA_KD), lambda i: (0, 0))],
        out_specs=[su["wide"], sd["wide"], su["st"], sd["st"]],
        out_shape=[jax.ShapeDtypeStruct((S, GLA_VD), F32), jax.ShapeDtypeStruct((S, GLA_VD), F32),
                   jax.ShapeDtypeStruct((n, GLA_HEADS, GLA_DV, GLA_DK), F32),
                   jax.ShapeDtypeStruct((n, GLA_HEADS, GLA_DV, GLA_DK), F32)],
        scratch_shapes=[pltpu.VMEM((2 * GLA_HEADS, GLA_DV, GLA_DK), F32)],
        compiler_params=_params("arbitrary"),
    )(proj, proj, proj, logits, proj, proj, proj, logits, bias)


def _gla_bwd(proj, logits, bias, sf, sb, do, *, name):
    S = proj.shape[0]
    c = GLA_CHUNK
    n = S // c
    su, sd, up, down = _gla_specs(S, n)

    def body(qf, kf, vf, lf, stf, dof, qb, kb, vb, lb, stb, dob, bias_ref,
             dqkv_f, dlg_f, dqkv_b, dlg_b, dbias, dst):
        first = pl.program_id(0) == 0

        @pl.when(first)
        def _():
            dst[...] = jnp.zeros_like(dst)

        dbias_parts = []
        for d, (q_r, k_r, v_r, l_r, s_r, do_r, dqkv_r, dlg_r) in enumerate(
                ((qf, kf, vf, lf, stf, dof, dqkv_f, dlg_f), (qb, kb, vb, lb, stb, dob, dqkv_b, dlg_b))):
            cum, cum_t, mask = _gla_masks(d)
            xl, b, b_end = _gla_decay(l_r[...], bias_ref[:, d * GLA_KD:(d + 1) * GLA_KD], cum, d)
            e, ei, ee, dend = jnp.exp(b), jnp.exp(-b), jnp.exp(b_end - b), jnp.exp(b_end)
            k = k_r[...]
            qd32 = q_r[...] * GLA_SCALE * e
            ki32 = k * ei
            ke32 = k * ee
            qd, ki, ke = qd32.astype(BF16), ki32.astype(BF16), ke32.astype(BF16)
            db_parts, dbe_parts = [], []
            for h in range(GLA_HEADS):
                ks = slice(h * GLA_DK, (h + 1) * GLA_DK)
                vs = slice(h * GLA_DV, (h + 1) * GLA_DV)
                stp = s_r[0, h]
                dstn = dst[d * GLA_HEADS + h]
                dstn_b = dstn.astype(BF16)
                v = v_r[:, vs].astype(BF16)
                dov = do_r[:, vs]
                att = jnp.where(mask, _dot_nt(qd[:, ks], ki[:, ks]), 0.0).astype(BF16)
                datt = jnp.where(mask, _dot_nt(dov, v), 0.0).astype(BF16)
                dqkv_r[:, 2 * GLA_KD + h * GLA_DV:2 * GLA_KD + (h + 1) * GLA_DV] = (
                    _dot_tn(att, dov) + _dot_nt(ke[:, ks], dstn_b))
                dqd = _dot(datt, ki[:, ks]) + _dot(dov, stp.astype(BF16))
                dki = _dot_tn(datt, qd[:, ks])
                dke = _dot(v, dstn_b)
                d_dend = _colsum(stp * dstn)
                dst[d * GLA_HEADS + h] = _dot_tn(dov, qd[:, ks]) + dstn * dend[:, ks]
                dqkv_r[:, ks] = dqd * e[:, ks] * GLA_SCALE
                dqkv_r[:, GLA_KD + h * GLA_DK:GLA_KD + (h + 1) * GLA_DK] = dki * ei[:, ks] + dke * ee[:, ks]
                dke_ke = dke * ke32[:, ks]
                db_parts.append(dqd * qd32[:, ks] - dki * ki32[:, ks] - dke_ke)
                dbe_parts.append(_colsum(dke_ke) + d_dend * dend[:, ks])
            db = jnp.concatenate(db_parts, axis=1)
            db_end = jnp.concatenate(dbe_parts, axis=1)
            dla = _cumdot(cum_t, db) + db_end
            dlg = dla * (1.0 / GLA_GATE_NORMALIZER) * _sigmoid(-xl)
            dlg_r[...] = dlg
            dbias_parts.append(_colsum(dlg))
        dbv = jnp.concatenate(dbias_parts, axis=1)

        @pl.when(first)
        def _():
            dbias[...] = dbv

        @pl.when(jnp.logical_not(first))
        def _():
            dbias[...] += dbv

    lg_f = pl.BlockSpec((c, GLA_KD), lambda i: (down(i), 0))
    lg_b = pl.BlockSpec((c, GLA_KD), lambda i: (up(i), 1))
    dlg_f = pl.BlockSpec((c, GLA_KD), lambda i: (down(i), 0))
    dlg_b = pl.BlockSpec((c, GLA_KD), lambda i: (up(i), 0))
    return pl.pallas_call(
        body, name=name, grid=(n,),
        in_specs=[sd["q"], sd["k"], sd["v"], lg_f, sd["st"], sd["wide"],
                  su["q"], su["k"], su["v"], lg_b, su["st"], su["wide"],
                  pl.BlockSpec((1, 2 * GLA_KD), lambda i: (0, 0))],
        out_specs=[sd["qkv"], dlg_f, su["qkv"], dlg_b, pl.BlockSpec((1, 2 * GLA_KD), lambda i: (0, 0))],
        out_shape=[jax.ShapeDtypeStruct((S, 2 * GLA_KD + GLA_VD), F32), jax.ShapeDtypeStruct((S, GLA_KD), F32),
                   jax.ShapeDtypeStruct((S, 2 * GLA_KD + GLA_VD), F32), jax.ShapeDtypeStruct((S, GLA_KD), F32),
                   jax.ShapeDtypeStruct((1, 2 * GLA_KD), F32)],
        scratch_shapes=[pltpu.VMEM((2 * GLA_HEADS, GLA_DV, GLA_DK), F32)],
        compiler_params=_params("arbitrary"),
    )(proj, proj, proj, logits, sf, do, proj, proj, proj, logits, sb, do, bias)


def _gla_gate_fwd(of, ob, proj, w, *, name):
    S = of.shape[0]
    ts = _tile(S, 256, 16)

    def body(of_ref, ob_ref, g_ref, w_ref, y_ref):
        for h in range(GLA_HEADS):
            vs = slice(h * GLA_DV, (h + 1) * GLA_DV)
            o = of_ref[:, vs] + ob_ref[:, vs]
            r = lax.rsqrt(jnp.mean(o * o, axis=-1, keepdims=True) + NORM_EPS)
            g = g_ref[:, vs]
            y_ref[:, vs] = (o * r * w_ref[...] * (g * _sigmoid(g))).astype(BF16)

    wide = pl.BlockSpec((ts, GLA_VD), lambda i: (i, 0))
    return pl.pallas_call(
        body, name=name, grid=(S // ts,),
        in_specs=[wide, wide, pl.BlockSpec((ts, GLA_VD), lambda i: (i, 2)), pl.BlockSpec((1, GLA_DV), lambda i: (0, 0))],
        out_specs=wide,
        out_shape=jax.ShapeDtypeStruct((S, GLA_VD), BF16),
        compiler_params=_params("parallel"),
    )(of, ob, proj, w)


def _gla_gate_bwd(of, ob, proj, w, dy, *, name):
    S = of.shape[0]
    ts = _tile(S, 256, 16)

    def body(of_ref, ob_ref, g_ref, w_ref, dy_ref, do_ref, dg_ref, dw_ref):
        i = pl.program_id(0)
        acc = jnp.zeros((1, GLA_DV), F32)
        for h in range(GLA_HEADS):
            vs = slice(h * GLA_DV, (h + 1) * GLA_DV)
            o = of_ref[:, vs] + ob_ref[:, vs]
            r = lax.rsqrt(jnp.mean(o * o, axis=-1, keepdims=True) + NORM_EPS)
            oh = o * r
            g = g_ref[:, vs]
            sg = _sigmoid(g)
            dyv = dy_ref[:, vs]
            dn = dyv * (g * sg)
            dg_ref[:, vs] = dyv * (oh * w_ref[...]) * (sg * (1.0 + g * (1.0 - sg)))
            acc = acc + _colsum(dn * oh)
            gg = dn * w_ref[...]
            do_ref[:, vs] = (r * (gg - oh * jnp.mean(gg * oh, axis=-1, keepdims=True))).astype(BF16)

        @pl.when(i == 0)
        def _():
            dw_ref[...] = acc

        @pl.when(i > 0)
        def _():
            dw_ref[...] += acc

    wide = pl.BlockSpec((ts, GLA_VD), lambda i: (i, 0))
    vec = pl.BlockSpec((1, GLA_DV), lambda i: (0, 0))
    return pl.pallas_call(
        body, name=name, grid=(S // ts,),
        in_specs=[wide, wide, pl.BlockSpec((ts, GLA_VD), lambda i: (i, 2)), vec, wide],
        out_specs=[wide, wide, vec],
        out_shape=[jax.ShapeDtypeStruct((S, GLA_VD), BF16), jax.ShapeDtypeStruct((S, GLA_VD), F32),
                   jax.ShapeDtypeStruct((1, GLA_DV), F32)],
        compiler_params=_params("arbitrary"),
    )(of, ob, proj, w, dy)


def _gla_combine(dqkv_f, dqkv_b, dg, dr, *, name):
    S = dg.shape[0]
    ts = _tile(S, 512, 16)
    nqkv = 2 * GLA_KD + GLA_VD

    def body(f_ref, b_ref, g_ref, r_ref, o_ref):
        o_ref[:, :nqkv] = (f_ref[...] + b_ref[...]).astype(BF16)
        o_ref[:, nqkv:nqkv + GLA_VD] = g_ref[...].astype(BF16)
        o_ref[:, nqkv + GLA_VD:] = r_ref[...].astype(BF16)

    return pl.pallas_call(
        body, name=name, grid=(S // ts,),
        in_specs=[pl.BlockSpec((ts, nqkv), lambda i: (i, 0)), pl.BlockSpec((ts, nqkv), lambda i: (i, 0)),
                  pl.BlockSpec((ts, GLA_VD), lambda i: (i, 0)), pl.BlockSpec((ts, LANES), lambda i: (i, 0))],
        out_specs=pl.BlockSpec((ts, GLA_PROJ), lambda i: (i, 0)),
        out_shape=jax.ShapeDtypeStruct((S, GLA_PROJ), BF16),
        compiler_params=_params("parallel"),
    )(dqkv_f, dqkv_b, dg, dr)


def _loss_head(y, t, *, name):
    S, D = y.shape
    ts = _tile(S, 512, 16)
    n = S // ts

    def body(y_ref, t_ref, dy_ref, l_ref, acc):
        i = pl.program_id(0)
        diff = y_ref[...] - t_ref[...]
        dy_ref[...] = diff * (1.0 / D)
        part = _colsum(diff * diff)

        @pl.when(i == 0)
        def _():
            acc[...] = part

        @pl.when(i > 0)
        def _():
            acc[...] += part

        @pl.when(i == n - 1)
        def _():
            l_ref[...] = jnp.full(l_ref.shape, 0.5 / D, F32) * jnp.sum(acc[...])

    row = pl.BlockSpec((ts, D), lambda i: (i, 0))
    return pl.pallas_call(
        body, name=name, grid=(n,),
        in_specs=[row, row],
        out_specs=[row, pl.BlockSpec((SUBLANES, LANES), lambda i: (0, 0))],
        out_shape=[jax.ShapeDtypeStruct((S, D), F32), jax.ShapeDtypeStruct((SUBLANES, LANES), F32)],
        scratch_shapes=[pltpu.VMEM((1, D), F32)],
        compiler_params=_params("arbitrary"),
    )(y, t)


def _adamw(w, g, m, v, *, name):
    R, C = w.shape
    tr = _tile(R, 256, SUBLANES)

    def body(w_ref, g_ref, m_ref, v_ref, d_ref, nm_ref, nv_ref):
        gv = g_ref[...]
        nm = ADAM_B1 * m_ref[...] + (1.0 - ADAM_B1) * gv
        nv = ADAM_B2 * v_ref[...] + (1.0 - ADAM_B2) * (gv * gv)
        m_hat = nm / (1.0 - ADAM_B1 ** ADAM_STEP)
        v_hat = nv / (1.0 - ADAM_B2 ** ADAM_STEP)
        d_ref[...] = -ADAM_LR * (m_hat / (jnp.sqrt(v_hat) + ADAM_EPS) + ADAM_WD * w_ref[...])
        nm_ref[...] = nm
        nv_ref[...] = nv

    blk = pl.BlockSpec((tr, C), lambda i: (i, 0))
    shp = jax.ShapeDtypeStruct((R, C), F32)
    return pl.pallas_call(
        body, name=name, grid=(R // tr,),
        in_specs=[blk] * 4, out_specs=[blk] * 3, out_shape=[shp] * 3,
        compiler_params=_params("parallel"),
    )(w, g, m, v)


ANY = pl.BlockSpec(memory_space=pl.ANY)


def _place():
    return lax.axis_index("x"), lax.axis_index("y"), lax.axis_index("c")


def _all_gather(block, *, name):
    R, L = block.shape

    def body(x_ref, out_ref, send_sems, recv_sems, local_sem):
        x, y, c = _place()
        me, sibling = (x, y, c), (x, y, 1 - c)
        chips = [(1 - x, y), (x, 1 - y), (1 - x, 1 - y)]

        def slot(px, py, pc):
            return out_ref.at[4 * px + 2 * py + pc]

        def copy(k, blk, to, src=None):
            return pltpu.make_async_remote_copy(
                src_ref=slot(*blk) if src is None else src, dst_ref=slot(*blk),
                send_sem=send_sems.at[k], recv_sem=recv_sems.at[k], device_id=to, device_id_type=MESH)

        mine = pltpu.make_async_copy(x_ref, slot(*me), local_sem)
        mine.start()
        first = [copy(0, me, sibling, src=x_ref)]
        first += [copy(1 + j, me, (*chip, c), src=x_ref) for j, chip in enumerate(chips)]
        for cp in first:
            cp.start()
        passed = [copy(4 + j, (*chip, c), sibling) for j, chip in enumerate(chips)]
        for j, chip in enumerate(chips):
            copy(1 + j, (*chip, c), me).wait_recv()
            passed[j].start()
        copy(0, sibling, me).wait_recv()
        for j, chip in enumerate(chips):
            copy(4 + j, (*chip, 1 - c), me).wait_recv()
        for cp in first + passed:
            cp.wait_send()
        mine.wait()

    return pl.pallas_call(
        body, name=name, in_specs=[ANY], out_specs=ANY,
        out_shape=jax.ShapeDtypeStruct((N_DEV, R, L), block.dtype),
        scratch_shapes=[pltpu.SemaphoreType.DMA((7,)), pltpu.SemaphoreType.DMA((7,)), pltpu.SemaphoreType.DMA],
    )(block)


def _all_gather_weights(srcs, row_sharded, *, name):
    n = len(srcs)

    def body(*refs):
        ins, outs = refs[:n], refs[n:2 * n]
        send_sems, recv_sems, local_sems = refs[2 * n:]
        x, y, c = _place()
        me, sibling = (x, y, c), (x, y, 1 - c)
        chips = [(1 - x, y), (x, 1 - y), (1 - x, 1 - y)]

        def slot(t, px, py, pc):
            d = 4 * px + 2 * py + pc
            if row_sharded[t]:
                rows = srcs[t].shape[1]
                return outs[t].at[:, pl.ds(pl.multiple_of(d * rows, 16), rows), :]
            return outs[t].at[d]

        def copy(t, k, blk, to, src=None):
            return pltpu.make_async_remote_copy(
                src_ref=slot(t, *blk) if src is None else src, dst_ref=slot(t, *blk),
                send_sem=send_sems.at[t, k], recv_sem=recv_sems.at[t, k], device_id=to, device_id_type=MESH)

        mine = [pltpu.make_async_copy(ins[t], slot(t, *me), local_sems.at[t]) for t in range(n)]
        for cp in mine:
            cp.start()
        first = []
        for t in range(n):
            first.append(copy(t, 0, me, sibling, src=ins[t]))
            first += [copy(t, 1 + j, me, (*chip, c), src=ins[t]) for j, chip in enumerate(chips)]
        for cp in first:
            cp.start()
        passed = []
        for j, chip in enumerate(chips):
            for t in range(n):
                copy(t, 1 + j, (*chip, c), me).wait_recv()
                fwd = copy(t, 4 + j, (*chip, c), sibling)
                fwd.start()
                passed.append(fwd)
        for t in range(n):
            copy(t, 0, sibling, me).wait_recv()
        for j, chip in enumerate(chips):
            for t in range(n):
                copy(t, 4 + j, (*chip, 1 - c), me).wait_recv()
        for cp in first + passed:
            cp.wait_send()
        for cp in mine:
            cp.wait()

    def out_shape(t):
        s = srcs[t].shape
        if row_sharded[t]:
            return jax.ShapeDtypeStruct((s[0], N_DEV * s[1], s[2]), srcs[t].dtype)
        return jax.ShapeDtypeStruct((N_DEV,) + s, srcs[t].dtype)

    return pl.pallas_call(
        body, name=name, in_specs=[ANY] * n, out_specs=[ANY] * n,
        out_shape=[out_shape(t) for t in range(n)],
        scratch_shapes=[pltpu.SemaphoreType.DMA((n, 7)), pltpu.SemaphoreType.DMA((n, 7)), pltpu.SemaphoreType.DMA((n,))],
    )(*srcs)


def _unshard_cols(g, groups, width, *, name):
    _, L, A, Bs = g.shape
    ta = _tile(A, 256, 16)

    def body(g_ref, *o_refs):
        for o_ref, devs in zip(o_refs, groups):
            for q, d in enumerate(devs):
                o_ref[:, q * Bs:(q + 1) * Bs] = g_ref[d]
            if len(devs) * Bs < width:
                o_ref[:, len(devs) * Bs:] = jnp.zeros((ta, width - len(devs) * Bs), g.dtype)

    return pl.pallas_call(
        body, name=name, grid=(L, A // ta),
        in_specs=[pl.BlockSpec((N_DEV, None, ta, Bs), lambda l, i: (0, l, i, 0))],
        out_specs=[pl.BlockSpec((None, ta, width), lambda l, i: (l, i, 0)) for _ in groups],
        out_shape=[jax.ShapeDtypeStruct((L, A, width), g.dtype) for _ in groups],
        compiler_params=_params("parallel", "parallel"),
    )(g)


def _sibling_exchange(gs, *, name):
    n = len(gs)

    def body(*refs):
        g_refs, buf_refs = refs[:n], refs[n:2 * n]
        send_sems, recv_sems = refs[2 * n:]
        x, y, c = _place()
        copies = [pltpu.make_async_remote_copy(
            src_ref=g_refs[t].at[2 * k + 1 - c], dst_ref=buf_refs[t].at[k],
            send_sem=send_sems.at[t, k], recv_sem=recv_sems.at[t, k],
            device_id=(x, y, 1 - c), device_id_type=MESH) for t in range(n) for k in range(4)]
        for cp in copies:
            cp.start()
        for cp in copies:
            cp.wait()

    return pl.pallas_call(
        body, name=name, in_specs=[ANY] * n, out_specs=[ANY] * n,
        out_shape=[jax.ShapeDtypeStruct((4,) + g.shape[1:], g.dtype) for g in gs],
        scratch_shapes=[pltpu.SemaphoreType.DMA((n, 4)), pltpu.SemaphoreType.DMA((n, 4))],
    )(*gs)


def _add_sibling(g, buf, c_idx, *, name):
    _, L, A, B = g.shape
    ta = _tile(A, 256, 16)

    def body(c_ref, g_ref, b_ref, o_ref):
        o_ref[...] = (g_ref[...] + b_ref[...]).astype(BF16)

    blk = pl.BlockSpec((None, None, ta, B), lambda k, l, i, c_ref: (k, l, i, 0))
    return pl.pallas_call(
        body, name=name,
        grid_spec=pltpu.PrefetchScalarGridSpec(
            num_scalar_prefetch=1, grid=(4, L, A // ta),
            in_specs=[pl.BlockSpec((None, None, ta, B), lambda k, l, i, c_ref: (2 * k + c_ref[0], l, i, 0)), blk],
            out_specs=blk),
        out_shape=jax.ShapeDtypeStruct((4, L, A, B), BF16),
        compiler_params=_params("parallel", "parallel", "parallel"),
    )(c_idx, g, buf)


def _chip_exchange(ps, *, name):
    n = len(ps)

    def body(*refs):
        p_refs, buf_refs = refs[:n], refs[n:2 * n]
        send_sems, recv_sems, local_sems = refs[2 * n:]
        x, y, c = _place()
        mine = 2 * x + y
        chips = [(1 - x, y), (x, 1 - y), (1 - x, 1 - y)]
        own = [pltpu.make_async_copy(p_refs[t].at[mine], buf_refs[t].at[mine], local_sems.at[t]) for t in range(n)]
        for cp in own:
            cp.start()
        copies = [pltpu.make_async_remote_copy(
            src_ref=p_refs[t].at[2 * cx + cy], dst_ref=buf_refs[t].at[mine],
            send_sem=send_sems.at[t, j], recv_sem=recv_sems.at[t, j],
            device_id=(cx, cy, c), device_id_type=MESH) for t in range(n) for j, (cx, cy) in enumerate(chips)]
        for cp in copies:
            cp.start()
        for cp in copies:
            cp.wait()
        for cp in own:
            cp.wait()

    return pl.pallas_call(
        body, name=name, in_specs=[ANY] * n, out_specs=[ANY] * n,
        out_shape=[jax.ShapeDtypeStruct(p.shape, p.dtype) for p in ps],
        scratch_shapes=[pltpu.SemaphoreType.DMA((n, 3)), pltpu.SemaphoreType.DMA((n, 3)), pltpu.SemaphoreType.DMA((n,))],
    )(*ps)


def _adamw_sum(w, m, v, parts, *, name):
    L, A, B = w.shape
    ta = _tile(A, 256, SUBLANES)

    def body(w_ref, m_ref, v_ref, p_ref, g_ref, d_ref, nm_ref, nv_ref):
        gv = p_ref[0].astype(F32)
        for j in range(1, 4):
            gv = gv + p_ref[j].astype(F32)
        nm = ADAM_B1 * m_ref[...] + (1.0 - ADAM_B1) * gv
        nv = ADAM_B2 * v_ref[...] + (1.0 - ADAM_B2) * (gv * gv)
        m_hat = nm / (1.0 - ADAM_B1 ** ADAM_STEP)
        v_hat = nv / (1.0 - ADAM_B2 ** ADAM_STEP)
        g_ref[...] = gv
        d_ref[...] = -ADAM_LR * (m_hat / (jnp.sqrt(v_hat) + ADAM_EPS) + ADAM_WD * w_ref[...])
        nm_ref[...] = nm
        nv_ref[...] = nv

    blk = pl.BlockSpec((None, ta, B), lambda l, i: (l, i, 0))
    shp = jax.ShapeDtypeStruct((L, A, B), F32)
    return pl.pallas_call(
        body, name=name, grid=(L, A // ta),
        in_specs=[blk, blk, blk, pl.BlockSpec((4, None, ta, B), lambda l, i: (0, l, i, 0))],
        out_specs=[blk] * 4, out_shape=[shp] * 4,
        compiler_params=_params("parallel", "parallel"),
    )(w, m, v, parts)


def _sum_slots(buf, *, name):
    n, R, L = buf.shape
    tr = _tile(R, 512, SUBLANES)

    def body(b_ref, o_ref):
        acc = b_ref[0]
        for j in range(1, n):
            acc = acc + b_ref[j]
        o_ref[...] = acc

    return pl.pallas_call(
        body, name=name, grid=(R // tr,),
        in_specs=[pl.BlockSpec((n, tr, L), lambda i: (0, i, 0))],
        out_specs=pl.BlockSpec((tr, L), lambda i: (i, 0)),
        out_shape=jax.ShapeDtypeStruct((R, L), buf.dtype),
        compiler_params=_params("parallel"),
    )(buf)


BIG = ("gla_w_in", "gla_w_out", "attn_w_qkv", "attn_w_out", "ffn_w_up", "ffn_w_down")
SMALL_SHARDED = ("gla_w_gate_up_f", "gla_w_gate_up_b", "ffn_w_conv")
SHARD_AXIS = {"gla_w_in": 2, "gla_w_gate_up_f": 2, "gla_w_gate_up_b": 2, "gla_w_out": 1, "attn_w_qkv": 2,
              "attn_w_out": 1, "ffn_w_up": 2, "ffn_w_conv": 2, "ffn_w_down": 1}
KEPT_F32 = ("ffn_w_conv",)
REPLICATED = ("norm_mix", "norm_ffn", "gla_b_gate_f", "gla_b_gate_b", "gla_norm", "attn_q_norm", "attn_k_norm",
              "ffn_b_conv")
WEIGHTS = ("norm_mix", "norm_ffn", "gla_w_in", "gla_w_gate_up_f", "gla_b_gate_f", "gla_w_gate_up_b", "gla_b_gate_b",
           "gla_norm", "gla_w_out", "attn_w_qkv", "attn_q_norm", "attn_k_norm", "attn_w_out", "ffn_w_up", "ffn_w_conv",
           "ffn_b_conv", "ffn_w_down")


def _rows(flat, row_align):
    n = flat.shape[0]
    per = row_align * LANES
    padded = -(-n // per) * per
    return jnp.pad(flat, (0, padded - n)).reshape(padded // LANES, LANES)


def _side_by_side(gathered):
    n, l, a, b = gathered.shape
    return jnp.transpose(gathered, (1, 2, 0, 3)).reshape(l, a, n * b)


def _gather_weights(shards):
    order = list(BIG) + list(SMALL_SHARDED)
    srcs = [shards[n] if n in KEPT_F32 else shards[n].astype(BF16) for n in order]
    got = dict(zip(order, _all_gather_weights(srcs, [SHARD_AXIS[n] == 1 for n in order], name="weights_all_gather")))
    every = tuple(range(N_DEV))
    half = N_DEV // 2
    f2 = got["ffn_w_up"].shape[-1] * N_DEV
    (w_in,) = _unshard_cols(got["gla_w_in"], [every], GLA_PROJ, name="unshard_gla_w_in")
    (w_qkv,) = _unshard_cols(got["attn_w_qkv"], [every], QKV_DIM, name="unshard_attn_w_qkv")
    w_upv, w_upg = _unshard_cols(got["ffn_w_up"], [every[:half], every[half:]], f2 // 2, name="unshard_ffn_w_up")
    gate_f, gate_b = _side_by_side(got["gla_w_gate_up_f"]), _side_by_side(got["gla_w_gate_up_b"])
    return dict(
        gla_w_in=w_in, attn_w_qkv=w_qkv, ffn_w_up_val=w_upv, ffn_w_up_gate=w_upg,
        gla_w_out=got["gla_w_out"], attn_w_out=got["attn_w_out"], ffn_w_down=got["ffn_w_down"],
        ffn_w_conv=_side_by_side(got["ffn_w_conv"]),
        gla_w_gate=jnp.stack([_gate_matrix(gate_f[j], gate_b[j]) for j in range(gate_f.shape[0])]))


def _rope_tables(S):
    rows = S // GRID_W
    pairs = ATTN_HD // 4
    row_idx = jnp.repeat(jnp.arange(rows, dtype=F32), GRID_W)
    col_idx = jnp.tile(jnp.arange(GRID_W, dtype=F32), rows)
    inv_freq = ROPE_THETA ** (-jnp.arange(pairs, dtype=F32) / pairs)
    ang = jnp.concatenate([row_idx[:, None] * inv_freq, col_idx[:, None] * inv_freq], axis=-1)
    cos, sin = jnp.cos(ang), jnp.sin(ang)
    return jnp.concatenate([cos, cos], axis=-1), jnp.concatenate([-sin, sin], axis=-1)


def _gate_matrix(w_f, w_b):
    rk = w_f.shape[0]
    top = jnp.concatenate([w_f, jnp.zeros_like(w_f)], axis=1)
    mid = jnp.concatenate([jnp.zeros_like(w_b), w_b], axis=1)
    pad = jnp.zeros((LANES - 2 * rk, 2 * GLA_KD), w_f.dtype)
    return jnp.concatenate([top, mid, pad], axis=0)


def _local_step(x, target, rep, wts, shard_shapes):
    S, D = x.shape
    depth = rep["norm_mix"].shape[0]
    F = wts["ffn_w_down"].shape[1]
    cs, sn = _rope_tables(S)
    row = lambda a: a.reshape(1, -1)
    ranks_cols = (GLA_PROJ - LANES, LANES)

    saved = []
    for i in range(depth):
        j = i // 2
        sv = {"x0": x}
        h1 = _rmsnorm_fwd(x, row(rep["norm_mix"][i]), name="norm_mix_fwd")
        sv["h1"] = h1
        if i % 2 == 0:
            bias = jnp.concatenate([rep["gla_b_gate_f"][j], rep["gla_b_gate_b"][j]]).reshape(1, -1)
            proj = _mm(h1, wts["gla_w_in"], layer=j, name="gla_in_proj")
            logits = _mm(proj, wts["gla_w_gate"], layer=j, a_cols=ranks_cols, name="gla_gate_logits")
            of, ob, sf, sb = _gla_fwd(proj, logits, bias, name="gla_fwd")
            y = _gla_gate_fwd(of, ob, proj, row(rep["gla_norm"][j]), name="gla_gate_fwd")
            x = _mm(y, wts["gla_w_out"], layer=j, res=x, name="gla_out_proj")
            sv.update(bias=bias, proj=proj, logits=logits, of=of, ob=ob, sf=sf, sb=sb, y=y)
        else:
            proj = _mm(h1, wts["attn_w_qkv"], layer=j, name="attn_qkv_proj")
            qkv = _attn_prep(proj, cs, sn, row(rep["attn_q_norm"][j]), row(rep["attn_k_norm"][j]), name="attn_prep")
            o = _attn_fwd(qkv, name="attn_fwd")
            x = _mm(o, wts["attn_w_out"], layer=j, res=x, name="attn_out_proj")
            sv.update(proj=proj, qkv=qkv, o=o)
        sv["x1"] = x
        h2 = _rmsnorm_fwd(x, row(rep["norm_ffn"][i]), name="norm_ffn_fwd")
        wc, bc = wts["ffn_w_conv"][i], rep["ffn_b_conv"][i]
        wcv, wcg, bcv, bcg = wc[:, :F], wc[:, F:], row(bc[:F]), row(bc[F:])
        uv = _mm(h2, wts["ffn_w_up_val"], layer=i, name="ffn_up_val")
        ug = _mm(h2, wts["ffn_w_up_gate"], layer=i, name="ffn_up_gate")
        act = _conv_act_fwd(uv, ug, wcv, wcg, bcv, bcg, name="ffn_conv_act")
        x = _mm(act, wts["ffn_w_down"], layer=i, res=x, name="ffn_down")
        sv.update(h2=h2, uv=uv, ug=ug, act=act, wcv=wcv, wcg=wcg, bcv=bcv, bcg=bcg)
        saved.append(sv)

    dx, loss_tile = _loss_head(x, target, name="loss_head")
    loss = loss_tile[0, 0]

    stacks = {n: lax.empty((N_DEV,) + tuple(shard_shapes[n]), F32) for n in BIG}
    gl = {k: [None] * depth for k in ("norm_mix", "norm_ffn", "ffn_w_conv", "ffn_b_conv")}
    gm = {k: [None] * (depth // 2) for k in ("gla_w_gate_up_f", "gla_b_gate_f", "gla_w_gate_up_b", "gla_b_gate_b",
                                             "gla_norm", "attn_q_norm", "attn_k_norm")}
    rk = GLA_GATE_RANK
    for i in reversed(range(depth)):
        j = i // 2
        sv = saved[i]
        dact = _mm(dx, wts["ffn_w_down"], layer=i, tb=True, name="ffn_down_dgrad")
        stacks["ffn_w_down"] = _wgrad(sv["act"], dx, stacks["ffn_w_down"], i, shard="rows", name="ffn_down_wgrad")
        dcv, dcg, dwv, dwg, dbv, dbg = _conv_act_bwd(sv["uv"], sv["ug"], sv["wcv"], sv["wcg"], sv["bcv"], sv["bcg"],
                                                     dact, name="ffn_conv_act_bwd")
        gl["ffn_w_conv"][i] = jnp.concatenate([dwv, dwg], axis=1)
        gl["ffn_b_conv"][i] = jnp.concatenate([dbv, dbg], axis=1)[0]
        duv = _conv_t(dcv, sv["wcv"], name="ffn_conv_t")
        dug = _conv_t(dcg, sv["wcg"], name="ffn_conv_t")
        dh2 = _mm(duv, wts["ffn_w_up_val"], layer=i, tb=True, name="ffn_up_dgrad_val")
        dh2 = _mm(dug, wts["ffn_w_up_gate"], layer=i, tb=True, res=dh2, name="ffn_up_dgrad_gate")
        stacks["ffn_w_up"] = _wgrad(sv["h2"], duv, stacks["ffn_w_up"], i, shard="cols", group=0, name="ffn_up_wgrad_val")
        stacks["ffn_w_up"] = _wgrad(sv["h2"], dug, stacks["ffn_w_up"], i, shard="cols", group=1, name="ffn_up_wgrad_gate")
        dx, dn = _rmsnorm_bwd(sv["x1"], row(rep["norm_ffn"][i]), dh2, dx, name="norm_ffn_bwd")
        gl["norm_ffn"][i] = dn[0]
        if i % 2 == 0:
            dy = _mm(dx, wts["gla_w_out"], layer=j, tb=True, name="gla_out_dgrad")
            stacks["gla_w_out"] = _wgrad(sv["y"], dx, stacks["gla_w_out"], j, shard="rows", name="gla_out_wgrad")
            do, dg, dgn = _gla_gate_bwd(sv["of"], sv["ob"], sv["proj"], row(rep["gla_norm"][j]), dy, name="gla_gate_bwd")
            gm["gla_norm"][j] = dgn[0]
            dqkv_f, dlg_f, dqkv_b, dlg_b, dbias = _gla_bwd(sv["proj"], sv["logits"], sv["bias"], sv["sf"], sv["sb"], do,
                                                           name="gla_bwd")
            gm["gla_b_gate_f"][j] = dbias[0, :GLA_KD]
            gm["gla_b_gate_b"][j] = dbias[0, GLA_KD:]
            dlogits = jnp.concatenate([dlg_f, dlg_b], axis=1)
            dr = _mm(dlogits, wts["gla_w_gate"], layer=j, tb=True, name="gla_gate_dgrad")
            dwg_full = _mm(sv["proj"], dlogits, ta=True, a_cols=ranks_cols, name="gla_gate_wgrad")
            gm["gla_w_gate_up_f"][j] = dwg_full[:rk, :GLA_KD]
            gm["gla_w_gate_up_b"][j] = dwg_full[rk:2 * rk, GLA_KD:]
            dproj = _gla_combine(dqkv_f, dqkv_b, dg, dr, name="gla_combine")
            dh1 = _mm(dproj, wts["gla_w_in"], layer=j, tb=True, name="gla_in_dgrad")
            stacks["gla_w_in"] = _wgrad(sv["h1"], dproj, stacks["gla_w_in"], j, shard="cols", name="gla_in_wgrad")
        else:
            do = _mm(dx, wts["attn_w_out"], layer=j, tb=True, out_dtype=BF16, name="attn_out_dgrad")
            stacks["attn_w_out"] = _wgrad(sv["o"], dx, stacks["attn_w_out"], j, shard="rows", name="attn_out_wgrad")
            dq, dk, dv = _attn_bwd(sv["qkv"], do, name="attn_bwd")
            dproj, dqn, dkn = _attn_prep_bwd(sv["proj"], dq, dk, dv, cs, sn, row(rep["attn_q_norm"][j]),
                                             row(rep["attn_k_norm"][j]), name="attn_prep_bwd")
            gm["attn_q_norm"][j] = dqn[0]
            gm["attn_k_norm"][j] = dkn[0]
            dh1 = _mm(dproj, wts["attn_w_qkv"], layer=j, tb=True, name="attn_qkv_dgrad")
            stacks["attn_w_qkv"] = _wgrad(sv["h1"], dproj, stacks["attn_w_qkv"], j, shard="cols", name="attn_qkv_wgrad")
        dx, dn = _rmsnorm_bwd(sv["x0"], row(rep["norm_mix"][i]), dh1, dx, name="norm_mix_bwd")
        gl["norm_mix"][i] = dn[0]

    small = {k: jnp.stack(v) for k, v in {**gl, **gm}.items()}
    return loss, dx, stacks, small


def kernel(x, norm_mix, norm_ffn, gla_w_in, gla_w_gate_up_f, gla_b_gate_f, gla_w_gate_up_b, gla_b_gate_b, gla_norm, gla_w_out, attn_w_qkv, attn_q_norm, attn_k_norm, attn_w_out, ffn_w_up, ffn_w_conv, ffn_b_conv, ffn_w_down, loss_target, m_norm_mix, m_norm_ffn, m_gla_w_in, m_gla_w_gate_up_f, m_gla_b_gate_f, m_gla_w_gate_up_b, m_gla_b_gate_b, m_gla_norm, m_gla_w_out, m_attn_w_qkv, m_attn_q_norm, m_attn_k_norm, m_attn_w_out, m_ffn_w_up, m_ffn_w_conv, m_ffn_b_conv, m_ffn_w_down, v_norm_mix, v_norm_ffn, v_gla_w_in, v_gla_w_gate_up_f, v_gla_b_gate_f, v_gla_w_gate_up_b, v_gla_b_gate_b, v_gla_norm, v_gla_w_out, v_attn_w_qkv, v_attn_q_norm, v_attn_k_norm, v_attn_w_out, v_ffn_w_up, v_ffn_w_conv, v_ffn_b_conv, v_ffn_w_down):
    given = dict(locals())
    w = {n: given[n] for n in WEIGHTS}
    m = {n: given["m_" + n] for n in WEIGHTS}
    v = {n: given["v_" + n] for n in WEIGHTS}
    shards = {n: w[n] for n in BIG + SMALL_SHARDED}
    rep = {n: w[n] for n in REPLICATED}

    wts = _gather_weights(shards)
    loss_local, grad_x, stacks, small = _local_step(x[0], loss_target[0], rep, wts, {n: w[n].shape for n in BIG})
    loss = lax.psum(loss_local, ("x", "y", "c"))

    c_idx = lax.axis_index("c").astype(jnp.int32).reshape(1)
    from_sibling = _sibling_exchange([stacks[n] for n in BIG], name="grads_sibling_exchange")
    chip_sums = [_add_sibling(stacks[n], buf, c_idx, name="grads_add_sibling_" + n) for n, buf in zip(BIG, from_sibling)]
    from_chips = dict(zip(BIG, _chip_exchange(chip_sums, name="grads_chip_exchange")))

    rest = REPLICATED + SMALL_SHARDED
    flat = _rows(jnp.concatenate([small[n].reshape(-1) for n in rest]), SUBLANES)
    total = _sum_slots(_all_gather(flat, name="small_grads_all_gather"), name="small_grads_sum").reshape(-1)
    dev = 4 * lax.axis_index("x") + 2 * lax.axis_index("y") + lax.axis_index("c")
    g, off = {}, 0
    for n in rest:
        whole = total[off:off + small[n].size].reshape(small[n].shape)
        off += small[n].size
        width = w[n].shape[-1]
        g[n] = whole if n in REPLICATED else lax.dynamic_slice_in_dim(whole, dev * width, width, axis=whole.ndim - 1)

    delta, new_m, new_v = {}, {}, {}
    for n in WEIGHTS:
        if n in BIG:
            g[n], delta[n], new_m[n], new_v[n] = _adamw_sum(w[n], m[n], v[n], from_chips[n], name="adamw_" + n)
        else:
            shape = w[n].shape
            two_d = (-1, shape[-1])
            d2, m2, v2 = _adamw(w[n].reshape(two_d), g[n].reshape(two_d), m[n].reshape(two_d), v[n].reshape(two_d),
                                name="adamw_" + n)
            delta[n], new_m[n], new_v[n] = d2.reshape(shape), m2.reshape(shape), v2.reshape(shape)

    return (loss, grad_x[None], *[g[n] for n in WEIGHTS], *[delta[n] for n in WEIGHTS],
            *[new_m[n] for n in WEIGHTS], *[new_v[n] for n in WEIGHTS])
```

```python
import math

import jax
import jax.numpy as jnp
from jax import lax
from jax.experimental import pallas as pl
from jax.experimental.pallas import tpu as pltpu

F32 = jnp.float32
BF16 = jnp.bfloat16
MESH = pl.DeviceIdType.MESH

N_DEV = 8
LANES = 128
SUBLANES = 8
VMEM_LIMIT = 56 * 1024 * 1024

NORM_EPS = 1e-6
GRID_W = 64
ROPE_THETA = 10000.0
GLA_HEADS = 4
GLA_DK = 128
GLA_DV = 256
GLA_CHUNK = 64
GLA_GATE_RANK = 16
GLA_GATE_NORMALIZER = 16.0
ATTN_HD = 128
ATTN_Q_HEADS = 8
ATTN_KV_HEADS = 2
ATTN_GROUP = ATTN_Q_HEADS // ATTN_KV_HEADS

ADAM_LR = 0.001
ADAM_B1 = 0.9
ADAM_B2 = 0.999
ADAM_EPS = 1e-08
ADAM_WD = 0.01
ADAM_STEP = 10


def _tile(n, target, align=LANES):
    if n <= target:
        return n
    t = (target // align) * align
    while t >= align:
        if n % t == 0:
            return t
        t -= align
    return n


def _params(*sem):
    return pltpu.CompilerParams(dimension_semantics=sem, vmem_limit_bytes=VMEM_LIMIT)


def _dot(a, b):
    return lax.dot_general(a, b, (((1,), (0,)), ((), ())), preferred_element_type=F32)


def _dot_nt(a, b):
    return lax.dot_general(a, b, (((1,), (1,)), ((), ())), preferred_element_type=F32)


def _dot_tn(a, b):
    return lax.dot_general(a, b, (((0,), (0,)), ((), ())), preferred_element_type=F32)


def _sigmoid(x):
    return 1.0 / (1.0 + jnp.exp(-x))


def _log_sigmoid(x):
    return jnp.minimum(x, 0.0) - jnp.log(1.0 + jnp.exp(-jnp.abs(x)))


def _colsum(x):
    return jnp.sum(x, axis=0, keepdims=True)


def _mm(a, b, *, ta=False, tb=False, res=None, out_dtype=F32, layer=None, a_cols=None, name):
    if tb:
        N, K = b.shape[-2:]
    else:
        K, N = b.shape[-2:]
    a_rows, a_width = a.shape
    a_off = 0
    if a_cols is not None:
        a_off, a_width = a_cols
    if ta:
        M = a_width
        assert a_rows == K, (a.shape, b.shape, ta, tb)
    else:
        M = a_rows
        assert a_width == K, (a.shape, b.shape, ta, tb)
    tm = _tile(M, 1408) if ta else _tile(M, 1024, 16)
    tn = _tile(N, 1408)
    tk = _tile(K, 512, 16) if ta else _tile(K, 1408)
    nk = K // tk
    dims = (((0 if ta else 1,), (1 if tb else 0,)), ((), ()))

    def body(*refs):
        if res is not None:
            a_ref, b_ref, r_ref, o_ref = refs[:4]
            scr = refs[4:]
        else:
            a_ref, b_ref, o_ref = refs[:3]
            r_ref = None
            scr = refs[3:]
        part = lax.dot_general(a_ref[...].astype(BF16), b_ref[...].astype(BF16), dims, preferred_element_type=F32)

        def finish(acc):
            if r_ref is not None:
                acc = acc + r_ref[...]
            o_ref[...] = acc.astype(out_dtype)

        if nk == 1:
            finish(part)
        else:
            acc_ref = scr[0]
            k = pl.program_id(2)

            @pl.when(k == 0)
            def _():
                acc_ref[...] = part

            @pl.when(k > 0)
            def _():
                acc_ref[...] += part

            @pl.when(k == nk - 1)
            def _():
                finish(acc_ref[...])

    a_blk = a_off // (tm if ta else tk)
    assert a_off % (tm if ta else tk) == 0
    a_spec = (pl.BlockSpec((tk, tm), lambda i, j, k: (k, a_blk + i)) if ta
              else pl.BlockSpec((tm, tk), lambda i, j, k: (i, a_blk + k)))
    if layer is None:
        b_spec = pl.BlockSpec((tn, tk), lambda i, j, k: (j, k)) if tb else pl.BlockSpec((tk, tn), lambda i, j, k: (k, j))
    else:
        b_spec = (pl.BlockSpec((None, tn, tk), lambda i, j, k: (layer, j, k)) if tb
                  else pl.BlockSpec((None, tk, tn), lambda i, j, k: (layer, k, j)))
    o_spec = pl.BlockSpec((tm, tn), lambda i, j, k: (i, j))
    in_specs = [a_spec, b_spec] + ([o_spec] if res is not None else [])
    args = (a, b) + ((res,) if res is not None else ())
    return pl.pallas_call(
        body, name=name, grid=(M // tm, N // tn, nk),
        in_specs=in_specs, out_specs=o_spec,
        out_shape=jax.ShapeDtypeStruct((M, N), out_dtype),
        scratch_shapes=[pltpu.VMEM((tm, tn), F32)] if nk > 1 else [],
        compiler_params=_params("parallel", "parallel", "arbitrary"),
    )(*args)


def _wgrad(a, b, stack, layer, *, shard, group=0, name):
    S, N = b.shape
    M = a.shape[1]
    As, Bs = stack.shape[-2:]
    tk = _tile(S, 1024, 16)
    nk = S // tk
    if shard == "cols":
        n = N // Bs
        tm = _tile(M, 512)
        tn = N
        grid = (M // tm, 1, nk)
        o_spec = pl.BlockSpec((n, None, tm, Bs), lambda i, j, k: (group, layer, i, 0))
    else:
        per = As * LANES // math.gcd(As, LANES)
        tm = M if M <= 1408 else _tile(M, 1408, per)
        n = tm // As
        tn = _tile(N, 1024)
        grid = (M // tm, N // tn, nk)
        o_spec = pl.BlockSpec((n, None, As, tn), lambda i, j, k: (i, layer, 0, j))

    def body(a_ref, b_ref, _, o_ref, acc_ref):
        k = pl.program_id(2)
        part = _dot_tn(a_ref[...].astype(BF16), b_ref[...].astype(BF16))

        @pl.when(k == 0)
        def _():
            acc_ref[...] = part

        @pl.when(k > 0)
        def _():
            acc_ref[...] += part

        @pl.when(k == nk - 1)
        def _():
            for q in range(n):
                if shard == "cols":
                    o_ref[q] = acc_ref[:, q * Bs:(q + 1) * Bs]
                else:
                    o_ref[q] = acc_ref[q * As:(q + 1) * As, :]

    return pl.pallas_call(
        body, name=name, grid=grid,
        in_specs=[pl.BlockSpec((tk, tm), lambda i, j, k: (k, i)), pl.BlockSpec((tk, tn), lambda i, j, k: (k, j)),
                  pl.BlockSpec(memory_space=pl.ANY)],
        out_specs=o_spec,
        out_shape=jax.ShapeDtypeStruct(stack.shape, stack.dtype),
        input_output_aliases={2: 0},
        scratch_shapes=[pltpu.VMEM((tm, tn), F32)],
        compiler_params=_params("parallel", "parallel", "arbitrary"),
    )(a, b, stack)


def _rmsnorm_fwd(x, w, *, name):
    S, D = x.shape
    ts = _tile(S, 512, 16)

    def body(x_ref, w_ref, o_ref):
        xv = x_ref[...]
        r = lax.rsqrt(jnp.mean(xv * xv, axis=-1, keepdims=True) + NORM_EPS)
        o_ref[...] = (xv * r * w_ref[...]).astype(BF16)

    return pl.pallas_call(
        body, name=name, grid=(S // ts,),
        in_specs=[pl.BlockSpec((ts, D), lambda i: (i, 0)), pl.BlockSpec((1, D), lambda i: (0, 0))],
        out_specs=pl.BlockSpec((ts, D), lambda i: (i, 0)),
        out_shape=jax.ShapeDtypeStruct((S, D), BF16),
        compiler_params=_params("parallel"),
    )(x, w)


def _rmsnorm_bwd(x, w, dh, dres, *, name):
    S, D = x.shape
    ts = _tile(S, 512, 16)
    n = S // ts

    def body(x_ref, w_ref, dh_ref, dr_ref, dx_ref, dw_ref):
        i = pl.program_id(0)
        xv = x_ref[...]
        r = lax.rsqrt(jnp.mean(xv * xv, axis=-1, keepdims=True) + NORM_EPS)
        xh = xv * r
        d = dh_ref[...]
        g = d * w_ref[...]
        dx_ref[...] = dr_ref[...] + r * (g - xh * jnp.mean(g * xh, axis=-1, keepdims=True))
        part = _colsum(d * xh)

        @pl.when(i == 0)
        def _():
            dw_ref[...] = part

        @pl.when(i > 0)
        def _():
            dw_ref[...] += part

    row = pl.BlockSpec((ts, D), lambda i: (i, 0))
    vec = pl.BlockSpec((1, D), lambda i: (0, 0))
    return pl.pallas_call(
        body, name=name, grid=(n,),
        in_specs=[row, vec, row, row], out_specs=[row, vec],
        out_shape=[jax.ShapeDtypeStruct((S, D), F32), jax.ShapeDtypeStruct((1, D), F32)],
        compiler_params=_params("arbitrary"),
    )(x, w, dh, dres)


def _halo_specs(S, ts, tf, row_axis):
    g = ts // SUBLANES
    last = S // SUBLANES - 1
    col_axis = 1 - row_axis
    main = pl.BlockSpec((ts, tf), lambda *ij: (ij[row_axis], ij[col_axis]))
    prev = pl.BlockSpec((SUBLANES, tf), lambda *ij: (jnp.maximum(ij[row_axis] * g - 1, 0), ij[col_axis]))
    nxt = pl.BlockSpec((SUBLANES, tf), lambda *ij: (jnp.minimum((ij[row_axis] + 1) * g, last), ij[col_axis]))
    return [main, prev, nxt]


def _shifted(u, prev_ref, next_ref, i, n):
    ts = u.shape[0]
    rid = lax.broadcasted_iota(jnp.int32, u.shape, 0)
    before = jnp.where(i > 0, prev_ref[SUBLANES - 1:SUBLANES, :], 0.0)
    after = jnp.where(i < n - 1, next_ref[0:1, :], 0.0)
    um1 = jnp.where(rid == 0, before, pltpu.roll(u, 1, 0))
    up1 = jnp.where(rid == ts - 1, after, pltpu.roll(u, ts - 1, 0))
    return um1, up1


def _conv3(u, prev_ref, next_ref, w_ref, i, n):
    um1, up1 = _shifted(u, prev_ref, next_ref, i, n)
    return w_ref[0:1, :] * um1 + w_ref[1:2, :] * u + w_ref[2:3, :] * up1


def _conv_act_fwd(uv, ug, wv, wg, bv, bg, *, name):
    S, F = uv.shape
    ts = _tile(S, 512, 16)
    tf = _tile(F, 1408)
    n = S // ts

    def body(v_ref, vp_ref, vn_ref, g_ref, gp_ref, gn_ref, wv_ref, wg_ref, bv_ref, bg_ref, o_ref):
        i = pl.program_id(0)
        val = _conv3(v_ref[...], vp_ref, vn_ref, wv_ref, i, n) + bv_ref[...]
        gate = _conv3(g_ref[...], gp_ref, gn_ref, wg_ref, i, n) + bg_ref[...]
        o_ref[...] = (gate * _sigmoid(gate) * val).astype(BF16)

    halo = _halo_specs(S, ts, tf, 0)
    w3 = pl.BlockSpec((3, tf), lambda i, j: (0, j))
    b1 = pl.BlockSpec((1, tf), lambda i, j: (0, j))
    return pl.pallas_call(
        body, name=name, grid=(n, F // tf),
        in_specs=halo + halo + [w3, w3, b1, b1],
        out_specs=pl.BlockSpec((ts, tf), lambda i, j: (i, j)),
        out_shape=jax.ShapeDtypeStruct((S, F), BF16),
        compiler_params=_params("parallel", "parallel"),
    )(uv, uv, uv, ug, ug, ug, wv, wg, bv, bg)


def _conv_act_bwd(uv, ug, wv, wg, bv, bg, dact, *, name):
    S, F = uv.shape
    ts = _tile(S, 512, 16)
    tf = _tile(F, 1408)
    n = S // ts

    def body(v_ref, vp_ref, vn_ref, g_ref, gp_ref, gn_ref, wv_ref, wg_ref, bv_ref, bg_ref, da_ref,
             dv_ref, dg_ref, dwv_ref, dwg_ref, dbv_ref, dbg_ref):
        i = pl.program_id(1)
        uvv, ugv = v_ref[...], g_ref[...]
        vm1, vp1 = _shifted(uvv, vp_ref, vn_ref, i, n)
        gm1, gp1 = _shifted(ugv, gp_ref, gn_ref, i, n)
        val = wv_ref[0:1, :] * vm1 + wv_ref[1:2, :] * uvv + wv_ref[2:3, :] * vp1 + bv_ref[...]
        gate = wg_ref[0:1, :] * gm1 + wg_ref[1:2, :] * ugv + wg_ref[2:3, :] * gp1 + bg_ref[...]
        sg = _sigmoid(gate)
        da = da_ref[...]
        dval = da * (gate * sg)
        dgate = da * val * (sg * (1.0 + gate * (1.0 - sg)))
        dv_ref[...] = dval
        dg_ref[...] = dgate
        sums = [(dwv_ref, 0, vm1 * dval), (dwv_ref, 1, uvv * dval), (dwv_ref, 2, vp1 * dval),
                (dwg_ref, 0, gm1 * dgate), (dwg_ref, 1, ugv * dgate), (dwg_ref, 2, gp1 * dgate),
                (dbv_ref, 0, dval), (dbg_ref, 0, dgate)]
        parts = [(ref, r, _colsum(t)) for ref, r, t in sums]

        @pl.when(i == 0)
        def _():
            for ref, r, part in parts:
                ref[r:r + 1, :] = part

        @pl.when(i > 0)
        def _():
            for ref, r, part in parts:
                ref[r:r + 1, :] += part

    halo = _halo_specs(S, ts, tf, 1)
    w3 = pl.BlockSpec((3, tf), lambda j, i: (0, j))
    b1 = pl.BlockSpec((1, tf), lambda j, i: (0, j))
    blk = pl.BlockSpec((ts, tf), lambda j, i: (i, j))
    return pl.pallas_call(
        body, name=name, grid=(F // tf, n),
        in_specs=halo + halo + [w3, w3, b1, b1, blk],
        out_specs=[blk, blk, w3, w3, b1, b1],
        out_shape=[jax.ShapeDtypeStruct((S, F), F32), jax.ShapeDtypeStruct((S, F), F32),
                   jax.ShapeDtypeStruct((3, F), F32), jax.ShapeDtypeStruct((3, F), F32),
                   jax.ShapeDtypeStruct((1, F), F32), jax.ShapeDtypeStruct((1, F), F32)],
        compiler_params=_params("parallel", "arbitrary"),
    )(uv, uv, uv, ug, ug, ug, wv, wg, bv, bg, dact)


def _conv_t(duc, w, *, name):
    S, F = duc.shape
    ts = _tile(S, 512, 16)
    tf = _tile(F, 1408)
    n = S // ts

    def body(d_ref, dp_ref, dn_ref, w_ref, o_ref):
        i = pl.program_id(0)
        d = d_ref[...]
        dm1, dp1 = _shifted(d, dp_ref, dn_ref, i, n)
        o_ref[...] = (w_ref[0:1, :] * dp1 + w_ref[1:2, :] * d + w_ref[2:3, :] * dm1).astype(BF16)

    return pl.pallas_call(
        body, name=name, grid=(n, F // tf),
        in_specs=_halo_specs(S, ts, tf, 0) + [pl.BlockSpec((3, tf), lambda i, j: (0, j))],
        out_specs=pl.BlockSpec((ts, tf), lambda i, j: (i, j)),
        out_shape=jax.ShapeDtypeStruct((S, F), BF16),
        compiler_params=_params("parallel", "parallel"),
    )(duc, duc, duc, w)


N_QK = ATTN_Q_HEADS + ATTN_KV_HEADS
QKV_DIM = (ATTN_Q_HEADS + 2 * ATTN_KV_HEADS) * ATTN_HD


def _head(hd):
    return slice(hd * ATTN_HD, (hd + 1) * ATTN_HD)


def _attn_prep(proj, cs, sn, qn, kn, *, name):
    S = proj.shape[0]
    ts = _tile(S, 256, 16)

    def body(p_ref, c_ref, s_ref, qn_ref, kn_ref, o_ref):
        c, s = c_ref[...], s_ref[...]
        for hd in range(N_QK):
            xv = p_ref[:, _head(hd)]
            w = qn_ref[...] if hd < ATTN_Q_HEADS else kn_ref[...]
            r = lax.rsqrt(jnp.mean(xv * xv, axis=-1, keepdims=True) + NORM_EPS)
            nrm = xv * r * w
            o_ref[:, _head(hd)] = (nrm * c + pltpu.roll(nrm, ATTN_HD // 2, 1) * s).astype(BF16)
        o_ref[:, N_QK * ATTN_HD:] = p_ref[:, N_QK * ATTN_HD:].astype(BF16)

    row = pl.BlockSpec((ts, QKV_DIM), lambda i: (i, 0))
    rot = pl.BlockSpec((ts, ATTN_HD), lambda i: (i, 0))
    vec = pl.BlockSpec((1, ATTN_HD), lambda i: (0, 0))
    return pl.pallas_call(
        body, name=name, grid=(S // ts,),
        in_specs=[row, rot, rot, vec, vec], out_specs=row,
        out_shape=jax.ShapeDtypeStruct((S, QKV_DIM), BF16),
        compiler_params=_params("parallel"),
    )(proj, cs, sn, qn, kn)


def _attn_prep_bwd(proj, dq, dk, dv, cs, sn, qn, kn, *, name):
    S = proj.shape[0]
    ts = _tile(S, 256, 16)
    nq = ATTN_Q_HEADS * ATTN_HD
    nkv = ATTN_KV_HEADS * ATTN_HD

    def body(p_ref, dq_ref, dk_ref, dv_ref, c_ref, s_ref, qn_ref, kn_ref, o_ref, dqn_ref, dkn_ref):
        i = pl.program_id(0)
        c, s = c_ref[...], s_ref[...]
        acc = [jnp.zeros((1, ATTN_HD), F32), jnp.zeros((1, ATTN_HD), F32)]
        for hd in range(N_QK):
            is_k = hd >= ATTN_Q_HEADS
            xv = p_ref[:, _head(hd)]
            w = kn_ref[...] if is_k else qn_ref[...]
            r = lax.rsqrt(jnp.mean(xv * xv, axis=-1, keepdims=True) + NORM_EPS)
            xh = xv * r
            dout = dk_ref[:, _head(hd - ATTN_Q_HEADS)] if is_k else dq_ref[:, _head(hd)]
            dn = dout * c + pltpu.roll(dout * s, ATTN_HD // 2, 1)
            acc[int(is_k)] = acc[int(is_k)] + _colsum(dn * xh)
            g = dn * w
            o_ref[:, _head(hd)] = (r * (g - xh * jnp.mean(g * xh, axis=-1, keepdims=True))).astype(BF16)
        o_ref[:, N_QK * ATTN_HD:] = dv_ref[...].astype(BF16)

        @pl.when(i == 0)
        def _():
            dqn_ref[...] = acc[0]
            dkn_ref[...] = acc[1]

        @pl.when(i > 0)
        def _():
            dqn_ref[...] += acc[0]
            dkn_ref[...] += acc[1]

    row = pl.BlockSpec((ts, QKV_DIM), lambda i: (i, 0))
    rot = pl.BlockSpec((ts, ATTN_HD), lambda i: (i, 0))
    vec = pl.BlockSpec((1, ATTN_HD), lambda i: (0, 0))
    return pl.pallas_call(
        body, name=name, grid=(S // ts,),
        in_specs=[row, pl.BlockSpec((ts, nq), lambda i: (i, 0)), pl.BlockSpec((ts, nkv), lambda i: (i, 0)),
                  pl.BlockSpec((ts, nkv), lambda i: (i, 0)), rot, rot, vec, vec],
        out_specs=[row, vec, vec],
        out_shape=[jax.ShapeDtypeStruct((S, QKV_DIM), BF16), jax.ShapeDtypeStruct((1, ATTN_HD), F32),
                   jax.ShapeDtypeStruct((1, ATTN_HD), F32)],
        compiler_params=_params("arbitrary"),
    )(proj, dq, dk, dv, cs, sn, qn, kn)


ATTN_SCALE = ATTN_HD ** -0.5


def _softmax_parts(q, k):
    s = _dot_nt(q, k)
    e = jnp.exp2((s - jnp.max(s, axis=-1, keepdims=True)) * (ATTN_SCALE * math.log2(math.e)))
    return e, 1.0 / jnp.sum(e, axis=-1, keepdims=True)


def _attn_fwd(qkv, *, name):
    S = qkv.shape[0]
    tq = _tile(S, 256, 16)

    def body(q_ref, k_ref, v_ref, o_ref):
        e, rl = _softmax_parts(q_ref[...], k_ref[...])
        o_ref[...] = (_dot(e.astype(BF16), v_ref[...]) * rl).astype(BF16)

    return pl.pallas_call(
        body, name=name, grid=(ATTN_Q_HEADS, S // tq),
        in_specs=[pl.BlockSpec((tq, ATTN_HD), lambda h, i: (i, h)),
                  pl.BlockSpec((S, ATTN_HD), lambda h, i: (0, ATTN_Q_HEADS + h // ATTN_GROUP)),
                  pl.BlockSpec((S, ATTN_HD), lambda h, i: (0, N_QK + h // ATTN_GROUP))],
        out_specs=pl.BlockSpec((tq, ATTN_HD), lambda h, i: (i, h)),
        out_shape=jax.ShapeDtypeStruct((S, ATTN_Q_HEADS * ATTN_HD), BF16),
        compiler_params=_params("parallel", "parallel"),
    )(qkv, qkv, qkv)


def _attn_bwd(qkv, do, *, name):
    S = qkv.shape[0]
    tq = _tile(S, 256, 16)

    def body(q_ref, k_ref, v_ref, do_ref, dq_ref, dk_ref, dv_ref):
        first = jnp.logical_and(pl.program_id(1) == 0, pl.program_id(2) == 0)
        q, k, dov = q_ref[...], k_ref[...], do_ref[...]
        e, rl = _softmax_parts(q, k)
        dp = _dot_nt(dov, v_ref[...])
        delta = jnp.sum(e * dp, axis=-1, keepdims=True) * rl
        dsb = (e * (dp - delta) * (rl * ATTN_SCALE)).astype(BF16)
        dq_ref[...] = _dot(dsb, k)
        dk = _dot_tn(dsb, q)
        dv = _dot_tn(e.astype(BF16), (dov.astype(F32) * rl).astype(BF16))

        @pl.when(first)
        def _():
            dk_ref[...] = dk
            dv_ref[...] = dv

        @pl.when(jnp.logical_not(first))
        def _():
            dk_ref[...] += dk
            dv_ref[...] += dv

    qblk = pl.BlockSpec((tq, ATTN_HD), lambda kv, g, i: (i, kv * ATTN_GROUP + g))
    kvacc = pl.BlockSpec((S, ATTN_HD), lambda kv, g, i: (0, kv))
    return pl.pallas_call(
        body, name=name, grid=(ATTN_KV_HEADS, ATTN_GROUP, S // tq),
        in_specs=[qblk,
                  pl.BlockSpec((S, ATTN_HD), lambda kv, g, i: (0, ATTN_Q_HEADS + kv)),
                  pl.BlockSpec((S, ATTN_HD), lambda kv, g, i: (0, N_QK + kv)),
                  qblk],
        out_specs=[qblk, kvacc, kvacc],
        out_shape=[jax.ShapeDtypeStruct((S, ATTN_Q_HEADS * ATTN_HD), F32),
                   jax.ShapeDtypeStruct((S, ATTN_KV_HEADS * ATTN_HD), F32),
                   jax.ShapeDtypeStruct((S, ATTN_KV_HEADS * ATTN_HD), F32)],
        compiler_params=_params("parallel", "arbitrary", "arbitrary"),
    )(qkv, qkv, qkv, do)


GLA_KD = GLA_HEADS * GLA_DK
GLA_VD = GLA_HEADS * GLA_DV
GLA_PROJ = 2 * GLA_KD + 2 * GLA_VD + LANES
GLA_SCALE = GLA_DK ** -0.5


def _split3(x):
    hi = x.astype(BF16)
    r1 = x - hi.astype(F32)
    mid = r1.astype(BF16)
    lo = (r1 - mid.astype(F32)).astype(BF16)
    return hi, mid, lo


def _cumdot(t, x):
    hi, mid, lo = _split3(x)
    return _dot(t, hi) + _dot(t, mid) + _dot(t, lo)


def _gla_masks(d):
    c = GLA_CHUNK
    row = lax.broadcasted_iota(jnp.int32, (c, c), 0)
    col = lax.broadcasted_iota(jnp.int32, (c, c), 1)
    lower, upper = col <= row, col >= row
    if d == 0:
        return lower.astype(BF16), upper.astype(BF16), lower
    return upper.astype(BF16), lower.astype(BF16), col > row


def _gla_decay(lg, bias, cum, d):
    xl = lg + bias
    la = _log_sigmoid(xl) * (1.0 / GLA_GATE_NORMALIZER)
    b = _cumdot(cum, la)
    b_end = b[GLA_CHUNK - 1:GLA_CHUNK, :] if d == 0 else b[0:1, :]
    return xl, b, b_end


def _gla_specs(S, n):
    c = GLA_CHUNK
    up = lambda i: i
    down = lambda i: n - 1 - i
    def specs(order):
        return dict(
            q=pl.BlockSpec((c, GLA_KD), lambda i: (order(i), 0)),
            k=pl.BlockSpec((c, GLA_KD), lambda i: (order(i), 1)),
            v=pl.BlockSpec((c, GLA_VD), lambda i: (order(i), 1)),
            st=pl.BlockSpec((1, GLA_HEADS, GLA_DV, GLA_DK), lambda i: (order(i), 0, 0, 0)),
            wide=pl.BlockSpec((c, GLA_VD), lambda i: (order(i), 0)),
            qkv=pl.BlockSpec((c, 2 * GLA_KD + GLA_VD), lambda i: (order(i), 0)),
        )
    return specs(up), specs(down), up, down


def _gla_fwd(proj, logits, bias, *, name):
    S = proj.shape[0]
    c = GLA_CHUNK
    n = S // c
    su, sd, up, down = _gla_specs(S, n)

    def body(qf, kf, vf, lf, qb, kb, vb, lb, bias_ref, of, ob, sf, sb, st):
        @pl.when(pl.program_id(0) == 0)
        def _():
            st[...] = jnp.zeros_like(st)

        for d, (q_r, k_r, v_r, l_r, o_r, s_r) in enumerate(((qf, kf, vf, lf, of, sf), (qb, kb, vb, lb, ob, sb))):
            cum, _, mask = _gla_masks(d)
            _, b, b_end = _gla_decay(l_r[...], bias_ref[:, d * GLA_KD:(d + 1) * GLA_KD], cum, d)
            dend = jnp.exp(b_end)
            k = k_r[...]
            qd = (q_r[...] * GLA_SCALE * jnp.exp(b)).astype(BF16)
            ki = (k * jnp.exp(-b)).astype(BF16)
            ke = (k * jnp.exp(b_end - b)).astype(BF16)
            for h in range(GLA_HEADS):
                ks = slice(h * GLA_DK, (h + 1) * GLA_DK)
                vs = slice(h * GLA_DV, (h + 1) * GLA_DV)
                stp = st[d * GLA_HEADS + h]
                s_r[0, h] = stp
                v = v_r[:, vs].astype(BF16)
                att = jnp.where(mask, _dot_nt(qd[:, ks], ki[:, ks]), 0.0).astype(BF16)
                o_r[:, vs] = _dot(att, v) + _dot_nt(qd[:, ks], stp.astype(BF16))
                st[d * GLA_HEADS + h] = stp * dend[:, ks] + _dot_tn(v, ke[:, ks])

    lg_f = pl.BlockSpec((c, GLA_KD), lambda i: (up(i), 0))
    lg_b = pl.BlockSpec((c, GLA_KD), lambda i: (down(i), 1))
    return pl.pallas_call(
        body, name=name, grid=(n,),
        in_specs=[su["q"], su["k"], su["v"], lg_f, sd["q"], sd["k"], sd["v"], lg_b,
                  pl.BlockSpec((1, 2 * GLA_KD), lambda i: (0, 0))],
        out_specs=[su["wide"], sd["wide"], su["st"], sd["st"]],
        out_shape=[jax.ShapeDtypeStruct((S, GLA_VD), F32), jax.ShapeDtypeStruct((S, GLA_VD), F32),
                   jax.ShapeDtypeStruct((n, GLA_HEADS, GLA_DV, GLA_DK), F32),
                   jax.ShapeDtypeStruct((n, GLA_HEADS, GLA_DV, GLA_DK), F32)],
        scratch_shapes=[pltpu.VMEM((2 * GLA_HEADS, GLA_DV, GLA_DK), F32)],
        compiler_params=_params("arbitrary"),
    )(proj, proj, proj, logits, proj, proj, proj, logits, bias)


def _gla_bwd(proj, logits, bias, sf, sb, do, *, name):
    S = proj.shape[0]
    c = GLA_CHUNK
    n = S // c
    su, sd, up, down = _gla_specs(S, n)

    def body(qf, kf, vf, lf, stf, dof, qb, kb, vb, lb, stb, dob, bias_ref,
             dqkv_f, dlg_f, dqkv_b, dlg_b, dbias, dst):
        first = pl.program_id(0) == 0

        @pl.when(first)
        def _():
            dst[...] = jnp.zeros_like(dst)

        dbias_parts = []
        for d, (q_r, k_r, v_r, l_r, s_r, do_r, dqkv_r, dlg_r) in enumerate(
                ((qf, kf, vf, lf, stf, dof, dqkv_f, dlg_f), (qb, kb, vb, lb, stb, dob, dqkv_b, dlg_b))):
            cum, cum_t, mask = _gla_masks(d)
            xl, b, b_end = _gla_decay(l_r[...], bias_ref[:, d * GLA_KD:(d + 1) * GLA_KD], cum, d)
            e, ei, ee, dend = jnp.exp(b), jnp.exp(-b), jnp.exp(b_end - b), jnp.exp(b_end)
            k = k_r[...]
            qd32 = q_r[...] * GLA_SCALE * e
            ki32 = k * ei
            ke32 = k * ee
            qd, ki, ke = qd32.astype(BF16), ki32.astype(BF16), ke32.astype(BF16)
            db_parts, dbe_parts = [], []
            for h in range(GLA_HEADS):
                ks = slice(h * GLA_DK, (h + 1) * GLA_DK)
                vs = slice(h * GLA_DV, (h + 1) * GLA_DV)
                stp = s_r[0, h]
                dstn = dst[d * GLA_HEADS + h]
                dstn_b = dstn.astype(BF16)
                v = v_r[:, vs].astype(BF16)
                dov = do_r[:, vs]
                att = jnp.where(mask, _dot_nt(qd[:, ks], ki[:, ks]), 0.0).astype(BF16)
                datt = jnp.where(mask, _dot_nt(dov, v), 0.0).astype(BF16)
                dqkv_r[:, 2 * GLA_KD + h * GLA_DV:2 * GLA_KD + (h + 1) * GLA_DV] = (
                    _dot_tn(att, dov) + _dot_nt(ke[:, ks], dstn_b))
                dqd = _dot(datt, ki[:, ks]) + _dot(dov, stp.astype(BF16))
                dki = _dot_tn(datt, qd[:, ks])
                dke = _dot(v, dstn_b)
                d_dend = _colsum(stp * dstn)
                dst[d * GLA_HEADS + h] = _dot_tn(dov, qd[:, ks]) + dstn * dend[:, ks]
                dqkv_r[:, ks] = dqd * e[:, ks] * GLA_SCALE
                dqkv_r[:, GLA_KD + h * GLA_DK:GLA_KD + (h + 1) * GLA_DK] = dki * ei[:, ks] + dke * ee[:, ks]
                dke_ke = dke * ke32[:, ks]
                db_parts.append(dqd * qd32[:, ks] - dki * ki32[:, ks] - dke_ke)
                dbe_parts.append(_colsum(dke_ke) + d_dend * dend[:, ks])
            db = jnp.concatenate(db_parts, axis=1)
            db_end = jnp.concatenate(dbe_parts, axis=1)
            dla = _cumdot(cum_t, db) + db_end
            dlg = dla * (1.0 / GLA_GATE_NORMALIZER) * _sigmoid(-xl)
            dlg_r[...] = dlg
            dbias_parts.append(_colsum(dlg))
        dbv = jnp.concatenate(dbias_parts, axis=1)

        @pl.when(first)
        def _():
            dbias[...] = dbv

        @pl.when(jnp.logical_not(first))
        def _():
            dbias[...] += dbv

    lg_f = pl.BlockSpec((c, GLA_KD), lambda i: (down(i), 0))
    lg_b = pl.BlockSpec((c, GLA_KD), lambda i: (up(i), 1))
    dlg_f = pl.BlockSpec((c, GLA_KD), lambda i: (down(i), 0))
    dlg_b = pl.BlockSpec((c, GLA_KD), lambda i: (up(i), 0))
    return pl.pallas_call(
        body, name=name, grid=(n,),
        in_specs=[sd["q"], sd["k"], sd["v"], lg_f, sd["st"], sd["wide"],
                  su["q"], su["k"], su["v"], lg_b, su["st"], su["wide"],
                  pl.BlockSpec((1, 2 * GLA_KD), lambda i: (0, 0))],
        out_specs=[sd["qkv"], dlg_f, su["qkv"], dlg_b, pl.BlockSpec((1, 2 * GLA_KD), lambda i: (0, 0))],
        out_shape=[jax.ShapeDtypeStruct((S, 2 * GLA_KD + GLA_VD), F32), jax.ShapeDtypeStruct((S, GLA_KD), F32),
                   jax.ShapeDtypeStruct((S, 2 * GLA_KD + GLA_VD), F32), jax.ShapeDtypeStruct((S, GLA_KD), F32),
                   jax.ShapeDtypeStruct((1, 2 * GLA_KD), F32)],
        scratch_shapes=[pltpu.VMEM((2 * GLA_HEADS, GLA_DV, GLA_DK), F32)],
        compiler_params=_params("arbitrary"),
    )(proj, proj, proj, logits, sf, do, proj, proj, proj, logits, sb, do, bias)


def _gla_gate_fwd(of, ob, proj, w, *, name):
    S = of.shape[0]
    ts = _tile(S, 256, 16)

    def body(of_ref, ob_ref, g_ref, w_ref, y_ref):
        for h in range(GLA_HEADS):
            vs = slice(h * GLA_DV, (h + 1) * GLA_DV)
            o = of_ref[:, vs] + ob_ref[:, vs]
            r = lax.rsqrt(jnp.mean(o * o, axis=-1, keepdims=True) + NORM_EPS)
            g = g_ref[:, vs]
            y_ref[:, vs] = (o * r * w_ref[...] * (g * _sigmoid(g))).astype(BF16)

    wide = pl.BlockSpec((ts, GLA_VD), lambda i: (i, 0))
    return pl.pallas_call(
        body, name=name, grid=(S // ts,),
        in_specs=[wide, wide, pl.BlockSpec((ts, GLA_VD), lambda i: (i, 2)), pl.BlockSpec((1, GLA_DV), lambda i: (0, 0))],
        out_specs=wide,
        out_shape=jax.ShapeDtypeStruct((S, GLA_VD), BF16),
        compiler_params=_params("parallel"),
    )(of, ob, proj, w)


def _gla_gate_bwd(of, ob, proj, w, dy, *, name):
    S = of.shape[0]
    ts = _tile(S, 256, 16)

    def body(of_ref, ob_ref, g_ref, w_ref, dy_ref, do_ref, dg_ref, dw_ref):
        i = pl.program_id(0)
        acc = jnp.zeros((1, GLA_DV), F32)
        for h in range(GLA_HEADS):
            vs = slice(h * GLA_DV, (h + 1) * GLA_DV)
            o = of_ref[:, vs] + ob_ref[:, vs]
            r = lax.rsqrt(jnp.mean(o * o, axis=-1, keepdims=True) + NORM_EPS)
            oh = o * r
            g = g_ref[:, vs]
            sg = _sigmoid(g)
            dyv = dy_ref[:, vs]
            dn = dyv * (g * sg)
            dg_ref[:, vs] = dyv * (oh * w_ref[...]) * (sg * (1.0 + g * (1.0 - sg)))
            acc = acc + _colsum(dn * oh)
            gg = dn * w_ref[...]
            do_ref[:, vs] = (r * (gg - oh * jnp.mean(gg * oh, axis=-1, keepdims=True))).astype(BF16)

        @pl.when(i == 0)
        def _():
            dw_ref[...] = acc

        @pl.when(i > 0)
        def _():
            dw_ref[...] += acc

    wide = pl.BlockSpec((ts, GLA_VD), lambda i: (i, 0))
    vec = pl.BlockSpec((1, GLA_DV), lambda i: (0, 0))
    return pl.pallas_call(
        body, name=name, grid=(S // ts,),
        in_specs=[wide, wide, pl.BlockSpec((ts, GLA_VD), lambda i: (i, 2)), vec, wide],
        out_specs=[wide, wide, vec],
        out_shape=[jax.ShapeDtypeStruct((S, GLA_VD), BF16), jax.ShapeDtypeStruct((S, GLA_VD), F32),
                   jax.ShapeDtypeStruct((1, GLA_DV), F32)],
        compiler_params=_params("arbitrary"),
    )(of, ob, proj, w, dy)


def _gla_combine(dqkv_f, dqkv_b, dg, dr, *, name):
    S = dg.shape[0]
    ts = _tile(S, 512, 16)
    nqkv = 2 * GLA_KD + GLA_VD

    def body(f_ref, b_ref, g_ref, r_ref, o_ref):
        o_ref[:, :nqkv] = (f_ref[...] + b_ref[...]).astype(BF16)
        o_ref[:, nqkv:nqkv + GLA_VD] = g_ref[...].astype(BF16)
        o_ref[:, nqkv + GLA_VD:] = r_ref[...].astype(BF16)

    return pl.pallas_call(
        body, name=name, grid=(S // ts,),
        in_specs=[pl.BlockSpec((ts, nqkv), lambda i: (i, 0)), pl.BlockSpec((ts, nqkv), lambda i: (i, 0)),
                  pl.BlockSpec((ts, GLA_VD), lambda i: (i, 0)), pl.BlockSpec((ts, LANES), lambda i: (i, 0))],
        out_specs=pl.BlockSpec((ts, GLA_PROJ), lambda i: (i, 0)),
        out_shape=jax.ShapeDtypeStruct((S, GLA_PROJ), BF16),
        compiler_params=_params("parallel"),
    )(dqkv_f, dqkv_b, dg, dr)


def _loss_head(y, t, *, name):
    S, D = y.shape
    ts = _tile(S, 512, 16)
    n = S // ts

    def body(y_ref, t_ref, dy_ref, l_ref, acc):
        i = pl.program_id(0)
        diff = y_ref[...] - t_ref[...]
        dy_ref[...] = diff * (1.0 / D)
        part = _colsum(diff * diff)

        @pl.when(i == 0)
        def _():
            acc[...] = part

        @pl.when(i > 0)
        def _():
            acc[...] += part

        @pl.when(i == n - 1)
        def _():
            l_ref[...] = jnp.full(l_ref.shape, 0.5 / D, F32) * jnp.sum(acc[...])

    row = pl.BlockSpec((ts, D), lambda i: (i, 0))
    return pl.pallas_call(
        body, name=name, grid=(n,),
        in_specs=[row, row],
        out_specs=[row, pl.BlockSpec((SUBLANES, LANES), lambda i: (0, 0))],
        out_shape=[jax.ShapeDtypeStruct((S, D), F32), jax.ShapeDtypeStruct((SUBLANES, LANES), F32)],
        scratch_shapes=[pltpu.VMEM((1, D), F32)],
        compiler_params=_params("arbitrary"),
    )(y, t)


def _adamw(w, g, m, v, *, name):
    R, C = w.shape
    tr = _tile(R, 256, SUBLANES)

    def body(w_ref, g_ref, m_ref, v_ref, d_ref, nm_ref, nv_ref):
        gv = g_ref[...]
        nm = ADAM_B1 * m_ref[...] + (1.0 - ADAM_B1) * gv
        nv = ADAM_B2 * v_ref[...] + (1.0 - ADAM_B2) * (gv * gv)
        m_hat = nm / (1.0 - ADAM_B1 ** ADAM_STEP)
        v_hat = nv / (1.0 - ADAM_B2 ** ADAM_STEP)
        d_ref[...] = -ADAM_LR * (m_hat / (jnp.sqrt(v_hat) + ADAM_EPS) + ADAM_WD * w_ref[...])
        nm_ref[...] = nm
        nv_ref[...] = nv

    blk = pl.BlockSpec((tr, C), lambda i: (i, 0))
    shp = jax.ShapeDtypeStruct((R, C), F32)
    return pl.pallas_call(
        body, name=name, grid=(R // tr,),
        in_specs=[blk] * 4, out_specs=[blk] * 3, out_shape=[shp] * 3,
        compiler_params=_params("parallel"),
    )(w, g, m, v)


ANY = pl.BlockSpec(memory_space=pl.ANY)


def _place():
    return lax.axis_index("x"), lax.axis_index("y"), lax.axis_index("c")


def _all_gather(block, *, name):
    R, L = block.shape

    def body(x_ref, out_ref, send_sems, recv_sems, local_sem):
        x, y, c = _place()
        me, sibling = (x, y, c), (x, y, 1 - c)
        chips = [(1 - x, y), (x, 1 - y), (1 - x, 1 - y)]

        def slot(px, py, pc):
            return out_ref.at[4 * px + 2 * py + pc]

        def copy(k, blk, to, src=None):
            return pltpu.make_async_remote_copy(
                src_ref=slot(*blk) if src is None else src, dst_ref=slot(*blk),
                send_sem=send_sems.at[k], recv_sem=recv_sems.at[k], device_id=to, device_id_type=MESH)

        mine = pltpu.make_async_copy(x_ref, slot(*me), local_sem)
        mine.start()
        first = [copy(0, me, sibling, src=x_ref)]
        first += [copy(1 + j, me, (*chip, c), src=x_ref) for j, chip in enumerate(chips)]
        for cp in first:
            cp.start()
        passed = [copy(4 + j, (*chip, c), sibling) for j, chip in enumerate(chips)]
        for j, chip in enumerate(chips):
            copy(1 + j, (*chip, c), me).wait_recv()
            passed[j].start()
        copy(0, sibling, me).wait_recv()
        for j, chip in enumerate(chips):
            copy(4 + j, (*chip, 1 - c), me).wait_recv()
        for cp in first + passed:
            cp.wait_send()
        mine.wait()

    return pl.pallas_call(
        body, name=name, in_specs=[ANY], out_specs=ANY,
        out_shape=jax.ShapeDtypeStruct((N_DEV, R, L), block.dtype),
        scratch_shapes=[pltpu.SemaphoreType.DMA((7,)), pltpu.SemaphoreType.DMA((7,)), pltpu.SemaphoreType.DMA],
    )(block)


def _all_gather_weights(srcs, row_sharded, *, name):
    n = len(srcs)

    def body(*refs):
        ins, outs = refs[:n], refs[n:2 * n]
        send_sems, recv_sems, local_sems = refs[2 * n:]
        x, y, c = _place()
        me, sibling = (x, y, c), (x, y, 1 - c)
        chips = [(1 - x, y), (x, 1 - y), (1 - x, 1 - y)]

        def slot(t, px, py, pc):
            d = 4 * px + 2 * py + pc
            if row_sharded[t]:
                rows = srcs[t].shape[1]
                return outs[t].at[:, pl.ds(pl.multiple_of(d * rows, 16), rows), :]
            return outs[t].at[d]

        def copy(t, k, blk, to, src=None):
            return pltpu.make_async_remote_copy(
                src_ref=slot(t, *blk) if src is None else src, dst_ref=slot(t, *blk),
                send_sem=send_sems.at[t, k], recv_sem=recv_sems.at[t, k], device_id=to, device_id_type=MESH)

        mine = [pltpu.make_async_copy(ins[t], slot(t, *me), local_sems.at[t]) for t in range(n)]
        for cp in mine:
            cp.start()
        first = []
        for t in range(n):
            first.append(copy(t, 0, me, sibling, src=ins[t]))
            first += [copy(t, 1 + j, me, (*chip, c), src=ins[t]) for j, chip in enumerate(chips)]
        for cp in first:
            cp.start()
        passed = []
        for j, chip in enumerate(chips):
            for t in range(n):
                copy(t, 1 + j, (*chip, c), me).wait_recv()
                fwd = copy(t, 4 + j, (*chip, c), sibling)
                fwd.start()
                passed.append(fwd)
        for t in range(n):
            copy(t, 0, sibling, me).wait_recv()
        for j, chip in enumerate(chips):
            for t in range(n):
                copy(t, 4 + j, (*chip, 1 - c), me).wait_recv()
        for cp in first + passed:
            cp.wait_send()
        for cp in mine:
            cp.wait()

    def out_shape(t):
        s = srcs[t].shape
        if row_sharded[t]:
            return jax.ShapeDtypeStruct((s[0], N_DEV * s[1], s[2]), srcs[t].dtype)
        return jax.ShapeDtypeStruct((N_DEV,) + s, srcs[t].dtype)

    return pl.pallas_call(
        body, name=name, in_specs=[ANY] * n, out_specs=[ANY] * n,
        out_shape=[out_shape(t) for t in range(n)],
        scratch_shapes=[pltpu.SemaphoreType.DMA((n, 7)), pltpu.SemaphoreType.DMA((n, 7)), pltpu.SemaphoreType.DMA((n,))],
    )(*srcs)


def _unshard_cols(g, groups, width, *, name):
    _, L, A, Bs = g.shape
    ta = _tile(A, 256, 16)

    def body(g_ref, *o_refs):
        for o_ref, devs in zip(o_refs, groups):
            for q, d in enumerate(devs):
                o_ref[:, q * Bs:(q + 1) * Bs] = g_ref[d]
            if len(devs) * Bs < width:
                o_ref[:, len(devs) * Bs:] = jnp.zeros((ta, width - len(devs) * Bs), g.dtype)

    return pl.pallas_call(
        body, name=name, grid=(L, A // ta),
        in_specs=[pl.BlockSpec((N_DEV, None, ta, Bs), lambda l, i: (0, l, i, 0))],
        out_specs=[pl.BlockSpec((None, ta, width), lambda l, i: (l, i, 0)) for _ in groups],
        out_shape=[jax.ShapeDtypeStruct((L, A, width), g.dtype) for _ in groups],
        compiler_params=_params("parallel", "parallel"),
    )(g)


def _sibling_exchange(gs, *, name):
    n = len(gs)

    def body(*refs):
        g_refs, buf_refs = refs[:n], refs[n:2 * n]
        send_sems, recv_sems = refs[2 * n:]
        x, y, c = _place()
        copies = [pltpu.make_async_remote_copy(
            src_ref=g_refs[t].at[2 * k + 1 - c], dst_ref=buf_refs[t].at[k],
            send_sem=send_sems.at[t, k], recv_sem=recv_sems.at[t, k],
            device_id=(x, y, 1 - c), device_id_type=MESH) for t in range(n) for k in range(4)]
        for cp in copies:
            cp.start()
        for cp in copies:
            cp.wait()

    return pl.pallas_call(
        body, name=name, in_specs=[ANY] * n, out_specs=[ANY] * n,
        out_shape=[jax.ShapeDtypeStruct((4,) + g.shape[1:], g.dtype) for g in gs],
        scratch_shapes=[pltpu.SemaphoreType.DMA((n, 4)), pltpu.SemaphoreType.DMA((n, 4))],
    )(*gs)


def _add_sibling(g, buf, c_idx, *, name):
    _, L, A, B = g.shape
    ta = _tile(A, 256, 16)

    def body(c_ref, g_ref, b_ref, o_ref):
        o_ref[...] = (g_ref[...] + b_ref[...]).astype(BF16)

    blk = pl.BlockSpec((None, None, ta, B), lambda k, l, i, c_ref: (k, l, i, 0))
    return pl.pallas_call(
        body, name=name,
        grid_spec=pltpu.PrefetchScalarGridSpec(
            num_scalar_prefetch=1, grid=(4, L, A // ta),
            in_specs=[pl.BlockSpec((None, None, ta, B), lambda k, l, i, c_ref: (2 * k + c_ref[0], l, i, 0)), blk],
            out_specs=blk),
        out_shape=jax.ShapeDtypeStruct((4, L, A, B), BF16),
        compiler_params=_params("parallel", "parallel", "parallel"),
    )(c_idx, g, buf)


def _chip_exchange(ps, *, name):
    n = len(ps)

    def body(*refs):
        p_refs, buf_refs = refs[:n], refs[n:2 * n]
        send_sems, recv_sems, local_sems = refs[2 * n:]
        x, y, c = _place()
        mine = 2 * x + y
        chips = [(1 - x, y), (x, 1 - y), (1 - x, 1 - y)]
        own = [pltpu.make_async_copy(p_refs[t].at[mine], buf_refs[t].at[mine], local_sems.at[t]) for t in range(n)]
        for cp in own:
            cp.start()
        copies = [pltpu.make_async_remote_copy(
            src_ref=p_refs[t].at[2 * cx + cy], dst_ref=buf_refs[t].at[mine],
            send_sem=send_sems.at[t, j], recv_sem=recv_sems.at[t, j],
            device_id=(cx, cy, c), device_id_type=MESH) for t in range(n) for j, (cx, cy) in enumerate(chips)]
        for cp in copies:
            cp.start()
        for cp in copies:
            cp.wait()
        for cp in own:
            cp.wait()

    return pl.pallas_call(
        body, name=name, in_specs=[ANY] * n, out_specs=[ANY] * n,
        out_shape=[jax.ShapeDtypeStruct(p.shape, p.dtype) for p in ps],
        scratch_shapes=[pltpu.SemaphoreType.DMA((n, 3)), pltpu.SemaphoreType.DMA((n, 3)), pltpu.SemaphoreType.DMA((n,))],
    )(*ps)


def _adamw_sum(w, m, v, parts, *, name):
    L, A, B = w.shape
    ta = _tile(A, 256, SUBLANES)

    def body(w_ref, m_ref, v_ref, p_ref, g_ref, d_ref, nm_ref, nv_ref):
        gv = p_ref[0].astype(F32)
        for j in range(1, 4):
            gv = gv + p_ref[j].astype(F32)
        nm = ADAM_B1 * m_ref[...] + (1.0 - ADAM_B1) * gv
        nv = ADAM_B2 * v_ref[...] + (1.0 - ADAM_B2) * (gv * gv)
        m_hat = nm / (1.0 - ADAM_B1 ** ADAM_STEP)
        v_hat = nv / (1.0 - ADAM_B2 ** ADAM_STEP)
        g_ref[...] = gv
        d_ref[...] = -ADAM_LR * (m_hat / (jnp.sqrt(v_hat) + ADAM_EPS) + ADAM_WD * w_ref[...])
        nm_ref[...] = nm
        nv_ref[...] = nv

    blk = pl.BlockSpec((None, ta, B), lambda l, i: (l, i, 0))
    shp = jax.ShapeDtypeStruct((L, A, B), F32)
    return pl.pallas_call(
        body, name=name, grid=(L, A // ta),
        in_specs=[blk, blk, blk, pl.BlockSpec((4, None, ta, B), lambda l, i: (0, l, i, 0))],
        out_specs=[blk] * 4, out_shape=[shp] * 4,
        compiler_params=_params("parallel", "parallel"),
    )(w, m, v, parts)


def _sum_slots(buf, *, name):
    n, R, L = buf.shape
    tr = _tile(R, 512, SUBLANES)

    def body(b_ref, o_ref):
        acc = b_ref[0]
        for j in range(1, n):
            acc = acc + b_ref[j]
        o_ref[...] = acc

    return pl.pallas_call(
        body, name=name, grid=(R // tr,),
        in_specs=[pl.BlockSpec((n, tr, L), lambda i: (0, i, 0))],
        out_specs=pl.BlockSpec((tr, L), lambda i: (i, 0)),
        out_shape=jax.ShapeDtypeStruct((R, L), buf.dtype),
        compiler_params=_params("parallel"),
    )(buf)


BIG = ("gla_w_in", "gla_w_out", "attn_w_qkv", "attn_w_out", "ffn_w_up", "ffn_w_down")
SMALL_SHARDED = ("gla_w_gate_up_f", "gla_w_gate_up_b", "ffn_w_conv")
SHARD_AXIS = {"gla_w_in": 2, "gla_w_gate_up_f": 2, "gla_w_gate_up_b": 2, "gla_w_out": 1, "attn_w_qkv": 2,
              "attn_w_out": 1, "ffn_w_up": 2, "ffn_w_conv": 2, "ffn_w_down": 1}
KEPT_F32 = ("ffn_w_conv",)
REPLICATED = ("norm_mix", "norm_ffn", "gla_b_gate_f", "gla_b_gate_b", "gla_norm", "attn_q_norm", "attn_k_norm",
              "ffn_b_conv")
WEIGHTS = ("norm_mix", "norm_ffn", "gla_w_in", "gla_w_gate_up_f", "gla_b_gate_f", "gla_w_gate_up_b", "gla_b_gate_b",
           "gla_norm", "gla_w_out", "attn_w_qkv", "attn_q_norm", "attn_k_norm", "attn_w_out", "ffn_w_up", "ffn_w_conv",
           "ffn_b_conv", "ffn_w_down")


def _rows(flat, row_align):
    n = flat.shape[0]
    per = row_align * LANES
    padded = -(-n // per) * per
    return jnp.pad(flat, (0, padded - n)).reshape(padded // LANES, LANES)


def _side_by_side(gathered):
    n, l, a, b = gathered.shape
    return jnp.transpose(gathered, (1, 2, 0, 3)).reshape(l, a, n * b)


def _gather_weights(shards):
    order = list(BIG) + list(SMALL_SHARDED)
    srcs = [shards[n] if n in KEPT_F32 else shards[n].astype(BF16) for n in order]
    got = dict(zip(order, _all_gather_weights(srcs, [SHARD_AXIS[n] == 1 for n in order], name="weights_all_gather")))
    every = tuple(range(N_DEV))
    half = N_DEV // 2
    f2 = got["ffn_w_up"].shape[-1] * N_DEV
    (w_in,) = _unshard_cols(got["gla_w_in"], [every], GLA_PROJ, name="unshard_gla_w_in")
    (w_qkv,) = _unshard_cols(got["attn_w_qkv"], [every], QKV_DIM, name="unshard_attn_w_qkv")
    w_upv, w_upg = _unshard_cols(got["ffn_w_up"], [every[:half], every[half:]], f2 // 2, name="unshard_ffn_w_up")
    gate_f, gate_b = _side_by_side(got["gla_w_gate_up_f"]), _side_by_side(got["gla_w_gate_up_b"])
    return dict(
        gla_w_in=w_in, attn_w_qkv=w_qkv, ffn_w_up_val=w_upv, ffn_w_up_gate=w_upg,
        gla_w_out=got["gla_w_out"], attn_w_out=got["attn_w_out"], ffn_w_down=got["ffn_w_down"],
        ffn_w_conv=_side_by_side(got["ffn_w_conv"]),
        gla_w_gate=jnp.stack([_gate_matrix(gate_f[j], gate_b[j]) for j in range(gate_f.shape[0])]))


def _rope_tables(S):
    rows = S // GRID_W
    pairs = ATTN_HD // 4
    row_idx = jnp.repeat(jnp.arange(rows, dtype=F32), GRID_W)
    col_idx = jnp.tile(jnp.arange(GRID_W, dtype=F32), rows)
    inv_freq = ROPE_THETA ** (-jnp.arange(pairs, dtype=F32) / pairs)
    ang = jnp.concatenate([row_idx[:, None] * inv_freq, col_idx[:, None] * inv_freq], axis=-1)
    cos, sin = jnp.cos(ang), jnp.sin(ang)
    return jnp.concatenate([cos, cos], axis=-1), jnp.concatenate([-sin, sin], axis=-1)


def _gate_matrix(w_f, w_b):
    rk = w_f.shape[0]
    top = jnp.concatenate([w_f, jnp.zeros_like(w_f)], axis=1)
    mid = jnp.concatenate([jnp.zeros_like(w_b), w_b], axis=1)
    pad = jnp.zeros((LANES - 2 * rk, 2 * GLA_KD), w_f.dtype)
    return jnp.concatenate([top, mid, pad], axis=0)


def _local_step(x, target, rep, wts, shard_shapes):
    S, D = x.shape
    depth = rep["norm_mix"].shape[0]
    F = wts["ffn_w_down"].shape[1]
    cs, sn = _rope_tables(S)
    row = lambda a: a.reshape(1, -1)
    ranks_cols = (GLA_PROJ - LANES, LANES)

    saved = []
    for i in range(depth):
        j = i // 2
        sv = {"x0": x}
        h1 = _rmsnorm_fwd(x, row(rep["norm_mix"][i]), name="norm_mix_fwd")
        sv["h1"] = h1
        if i % 2 == 0:
            bias = jnp.concatenate([rep["gla_b_gate_f"][j], rep["gla_b_gate_b"][j]]).reshape(1, -1)
            proj = _mm(h1, wts["gla_w_in"], layer=j, name="gla_in_proj")
            logits = _mm(proj, wts["gla_w_gate"], layer=j, a_cols=ranks_cols, name="gla_gate_logits")
            of, ob, sf, sb = _gla_fwd(proj, logits, bias, name="gla_fwd")
            y = _gla_gate_fwd(of, ob, proj, row(rep["gla_norm"][j]), name="gla_gate_fwd")
            x = _mm(y, wts["gla_w_out"], layer=j, res=x, name="gla_out_proj")
            sv.update(bias=bias, proj=proj, logits=logits, of=of, ob=ob, sf=sf, sb=sb, y=y)
        else:
            proj = _mm(h1, wts["attn_w_qkv"], layer=j, name="attn_qkv_proj")
            qkv = _attn_prep(proj, cs, sn, row(rep["attn_q_norm"][j]), row(rep["attn_k_norm"][j]), name="attn_prep")
            o = _attn_fwd(qkv, name="attn_fwd")
            x = _mm(o, wts["attn_w_out"], layer=j, res=x, name="attn_out_proj")
            sv.update(proj=proj, qkv=qkv, o=o)
        sv["x1"] = x
        h2 = _rmsnorm_fwd(x, row(rep["norm_ffn"][i]), name="norm_ffn_fwd")
        wc, bc = wts["ffn_w_conv"][i], rep["ffn_b_conv"][i]
        wcv, wcg, bcv, bcg = wc[:, :F], wc[:, F:], row(bc[:F]), row(bc[F:])
        uv = _mm(h2, wts["ffn_w_up_val"], layer=i, name="ffn_up_val")
        ug = _mm(h2, wts["ffn_w_up_gate"], layer=i, name="ffn_up_gate")
        act = _conv_act_fwd(uv, ug, wcv, wcg, bcv, bcg, name="ffn_conv_act")
        x = _mm(act, wts["ffn_w_down"], layer=i, res=x, name="ffn_down")
        sv.update(h2=h2, uv=uv, ug=ug, act=act, wcv=wcv, wcg=wcg, bcv=bcv, bcg=bcg)
        saved.append(sv)

    dx, loss_tile = _loss_head(x, target, name="loss_head")
    loss = loss_tile[0, 0]

    stacks = {n: lax.empty((N_DEV,) + tuple(shard_shapes[n]), F32) for n in BIG}
    gl = {k: [None] * depth for k in ("norm_mix", "norm_ffn", "ffn_w_conv", "ffn_b_conv")}
    gm = {k: [None] * (depth // 2) for k in ("gla_w_gate_up_f", "gla_b_gate_f", "gla_w_gate_up_b", "gla_b_gate_b",
                                             "gla_norm", "attn_q_norm", "attn_k_norm")}
    rk = GLA_GATE_RANK
    for i in reversed(range(depth)):
        j = i // 2
        sv = saved[i]
        dact = _mm(dx, wts["ffn_w_down"], layer=i, tb=True, name="ffn_down_dgrad")
        stacks["ffn_w_down"] = _wgrad(sv["act"], dx, stacks["ffn_w_down"], i, shard="rows", name="ffn_down_wgrad")
        dcv, dcg, dwv, dwg, dbv, dbg = _conv_act_bwd(sv["uv"], sv["ug"], sv["wcv"], sv["wcg"], sv["bcv"], sv["bcg"],
                                                     dact, name="ffn_conv_act_bwd")
        gl["ffn_w_conv"][i] = jnp.concatenate([dwv, dwg], axis=1)
        gl["ffn_b_conv"][i] = jnp.concatenate([dbv, dbg], axis=1)[0]
        duv = _conv_t(dcv, sv["wcv"], name="ffn_conv_t")
        dug = _conv_t(dcg, sv["wcg"], name="ffn_conv_t")
        dh2 = _mm(duv, wts["ffn_w_up_val"], layer=i, tb=True, name="ffn_up_dgrad_val")
        dh2 = _mm(dug, wts["ffn_w_up_gate"], layer=i, tb=True, res=dh2, name="ffn_up_dgrad_gate")
        stacks["ffn_w_up"] = _wgrad(sv["h2"], duv, stacks["ffn_w_up"], i, shard="cols", group=0, name="ffn_up_wgrad_val")
        stacks["ffn_w_up"] = _wgrad(sv["h2"], dug, stacks["ffn_w_up"], i, shard="cols", group=1, name="ffn_up_wgrad_gate")
        dx, dn = _rmsnorm_bwd(sv["x1"], row(rep["norm_ffn"][i]), dh2, dx, name="norm_ffn_bwd")
        gl["norm_ffn"][i] = dn[0]
        if i % 2 == 0:
            dy = _mm(dx, wts["gla_w_out"], layer=j, tb=True, name="gla_out_dgrad")
            stacks["gla_w_out"] = _wgrad(sv["y"], dx, stacks["gla_w_out"], j, shard="rows", name="gla_out_wgrad")
            do, dg, dgn = _gla_gate_bwd(sv["of"], sv["ob"], sv["proj"], row(rep["gla_norm"][j]), dy, name="gla_gate_bwd")
            gm["gla_norm"][j] = dgn[0]
            dqkv_f, dlg_f, dqkv_b, dlg_b, dbias = _gla_bwd(sv["proj"], sv["logits"], sv["bias"], sv["sf"], sv["sb"], do,
                                                           name="gla_bwd")
            gm["gla_b_gate_f"][j] = dbias[0, :GLA_KD]
            gm["gla_b_gate_b"][j] = dbias[0, GLA_KD:]
            dlogits = jnp.concatenate([dlg_f, dlg_b], axis=1)
            dr = _mm(dlogits, wts["gla_w_gate"], layer=j, tb=True, name="gla_gate_dgrad")
            dwg_full = _mm(sv["proj"], dlogits, ta=True, a_cols=ranks_cols, name="gla_gate_wgrad")
            gm["gla_w_gate_up_f"][j] = dwg_full[:rk, :GLA_KD]
            gm["gla_w_gate_up_b"][j] = dwg_full[rk:2 * rk, GLA_KD:]
            dproj = _gla_combine(dqkv_f, dqkv_b, dg, dr, name="gla_combine")
            dh1 = _mm(dproj, wts["gla_w_in"], layer=j, tb=True, name="gla_in_dgrad")
            stacks["gla_w_in"] = _wgrad(sv["h1"], dproj, stacks["gla_w_in"], j, shard="cols", name="gla_in_wgrad")
        else:
            do = _mm(dx, wts["attn_w_out"], layer=j, tb=True, out_dtype=BF16, name="attn_out_dgrad")
            stacks["attn_w_out"] = _wgrad(sv["o"], dx, stacks["attn_w_out"], j, shard="rows", name="attn_out_wgrad")
            dq, dk, dv = _attn_bwd(sv["qkv"], do, name="attn_bwd")
            dproj, dqn, dkn = _attn_prep_bwd(sv["proj"], dq, dk, dv, cs, sn, row(rep["attn_q_norm"][j]),
                                             row(rep["attn_k_norm"][j]), name="attn_prep_bwd")
            gm["attn_q_norm"][j] = dqn[0]
            gm["attn_k_norm"][j] = dkn[0]
            dh1 = _mm(dproj, wts["attn_w_qkv"], layer=j, tb=True, name="attn_qkv_dgrad")
            stacks["attn_w_qkv"] = _wgrad(sv["h1"], dproj, stacks["attn_w_qkv"], j, shard="cols", name="attn_qkv_wgrad")
        dx, dn = _rmsnorm_bwd(sv["x0"], row(rep["norm_mix"][i]), dh1, dx, name="norm_mix_bwd")
        gl["norm_mix"][i] = dn[0]

    small = {k: jnp.stack(v) for k, v in {**gl, **gm}.items()}
    return loss, dx, stacks, small


def kernel(x, norm_mix, norm_ffn, gla_w_in, gla_w_gate_up_f, gla_b_gate_f, gla_w_gate_up_b, gla_b_gate_b, gla_norm, gla_w_out, attn_w_qkv, attn_q_norm, attn_k_norm, attn_w_out, ffn_w_up, ffn_w_conv, ffn_b_conv, ffn_w_down, loss_target, m_norm_mix, m_norm_ffn, m_gla_w_in, m_gla_w_gate_up_f, m_gla_b_gate_f, m_gla_w_gate_up_b, m_gla_b_gate_b, m_gla_norm, m_gla_w_out, m_attn_w_qkv, m_attn_q_norm, m_attn_k_norm, m_attn_w_out, m_ffn_w_up, m_ffn_w_conv, m_ffn_b_conv, m_ffn_w_down, v_norm_mix, v_norm_ffn, v_gla_w_in, v_gla_w_gate_up_f, v_gla_b_gate_f, v_gla_w_gate_up_b, v_gla_b_gate_b, v_gla_norm, v_gla_w_out, v_attn_w_qkv, v_attn_q_norm, v_attn_k_norm, v_attn_w_out, v_ffn_w_up, v_ffn_w_conv, v_ffn_b_conv, v_ffn_w_down):
    given = dict(locals())
    w = {n: given[n] for n in WEIGHTS}
    m = {n: given["m_" + n] for n in WEIGHTS}
    v = {n: given["v_" + n] for n in WEIGHTS}
    shards = {n: w[n] for n in BIG + SMALL_SHARDED}
    rep = {n: w[n] for n in REPLICATED}

    wts = _gather_weights(shards)
    loss_local, grad_x, stacks, small = _local_step(x[0], loss_target[0], rep, wts, {n: w[n].shape for n in BIG})
    loss = lax.psum(loss_local, ("x", "y", "c"))

    c_idx = lax.axis_index("c").astype(jnp.int32).reshape(1)
    from_sibling = _sibling_exchange([stacks[n] for n in BIG], name="grads_sibling_exchange")
    chip_sums = [_add_sibling(stacks[n], buf, c_idx, name="grads_add_sibling_" + n) for n, buf in zip(BIG, from_sibling)]
    from_chips = dict(zip(BIG, _chip_exchange(chip_sums, name="grads_chip_exchange")))

    rest = REPLICATED + SMALL_SHARDED
    flat = _rows(jnp.concatenate([small[n].reshape(-1) for n in rest]), SUBLANES)
    total = _sum_slots(_all_gather(flat, name="small_grads_all_gather"), name="small_grads_sum").reshape(-1)
    dev = 4 * lax.axis_index("x") + 2 * lax.axis_index("y") + lax.axis_index("c")
    g, off = {}, 0
    for n in rest:
        whole = total[off:off + small[n].size].reshape(small[n].shape)
        off += small[n].size
        width = w[n].shape[-1]
        g[n] = whole if n in REPLICATED else lax.dynamic_slice_in_dim(whole, dev * width, width, axis=whole.ndim - 1)

    delta, new_m, new_v = {}, {}, {}
    for n in WEIGHTS:
        if n in BIG:
            g[n], delta[n], new_m[n], new_v[n] = _adamw_sum(w[n], m[n], v[n], from_chips[n], name="adamw_" + n)
        else:
            shape = w[n].shape
            two_d = (-1, shape[-1])
            d2, m2, v2 = _adamw(w[n].reshape(two_d), g[n].reshape(two_d), m[n].reshape(two_d), v[n].reshape(two_d),
                                name="adamw_" + n)
            delta[n], new_m[n], new_v[n] = d2.reshape(shape), m2.reshape(shape), v2.reshape(shape)

    return (loss, grad_x[None], *[g[n] for n in WEIGHTS], *[delta[n] for n in WEIGHTS],
            *[new_m[n] for n in WEIGHTS], *[new_v[n] for n in WEIGHTS])
```

```python
import math

import jax
import jax.numpy as jnp
from jax import lax
from jax.experimental import pallas as pl
from jax.experimental.pallas import tpu as pltpu

F32 = jnp.float32
BF16 = jnp.bfloat16
MESH = pl.DeviceIdType.MESH

N_DEV = 8
LANES = 128
SUBLANES = 8
VMEM_LIMIT = 56 * 1024 * 1024

NORM_EPS = 1e-6
GRID_W = 64
ROPE_THETA = 10000.0
GLA_HEADS = 4
GLA_DK = 128
GLA_DV = 256
GLA_CHUNK = 64
GLA_GATE_RANK = 16
GLA_GATE_NORMALIZER = 16.0
ATTN_HD = 128
ATTN_Q_HEADS = 8
ATTN_KV_HEADS = 2
ATTN_GROUP = ATTN_Q_HEADS // ATTN_KV_HEADS

ADAM_LR = 0.001
ADAM_B1 = 0.9
ADAM_B2 = 0.999
ADAM_EPS = 1e-08
ADAM_WD = 0.01
ADAM_STEP = 10


def _tile(n, target, align=LANES):
    if n <= target:
        return n
    t = (target // align) * align
    while t >= align:
        if n % t == 0:
            return t
        t -= align
    return n


def _params(*sem):
    return pltpu.CompilerParams(dimension_semantics=sem, vmem_limit_bytes=VMEM_LIMIT)


def _dot(a, b):
    return lax.dot_general(a, b, (((1,), (0,)), ((), ())), preferred_element_type=F32)


def _dot_nt(a, b):
    return lax.dot_general(a, b, (((1,), (1,)), ((), ())), preferred_element_type=F32)


def _dot_tn(a, b):
    return lax.dot_general(a, b, (((0,), (0,)), ((), ())), preferred_element_type=F32)


def _sigmoid(x):
    return 1.0 / (1.0 + jnp.exp(-x))


def _log_sigmoid(x):
    return jnp.minimum(x, 0.0) - jnp.log(1.0 + jnp.exp(-jnp.abs(x)))


def _colsum(x):
    return jnp.sum(x, axis=0, keepdims=True)


ANY = pl.BlockSpec(memory_space=pl.ANY)


def _mm(a, b, *, ta=False, tb=False, res=None, out_dtype=F32, layer=None, a_cols=None, dep=(), name):
    if tb:
        N, K = b.shape[-2:]
    else:
        K, N = b.shape[-2:]
    a_rows, a_width = a.shape
    a_off = 0
    if a_cols is not None:
        a_off, a_width = a_cols
    if ta:
        M = a_width
        assert a_rows == K, (a.shape, b.shape, ta, tb)
    else:
        M = a_rows
        assert a_width == K, (a.shape, b.shape, ta, tb)
    tm = _tile(M, 1408) if ta else _tile(M, 1024, 16)
    tn = _tile(N, 1408)
    tk = _tile(K, 512, 16) if ta else _tile(K, 1408)
    nk = K // tk
    dims = (((0 if ta else 1,), (1 if tb else 0,)), ((), ()))

    n_in = 2 + (res is not None) + len(dep)

    def body(*refs):
        a_ref, b_ref = refs[:2]
        r_ref = refs[2] if res is not None else None
        o_ref = refs[n_in]
        scr = refs[n_in + 1:]
        part = lax.dot_general(a_ref[...].astype(BF16), b_ref[...].astype(BF16), dims, preferred_element_type=F32)

        def finish(acc):
            if r_ref is not None:
                acc = acc + r_ref[...]
            o_ref[...] = acc.astype(out_dtype)

        if nk == 1:
            finish(part)
        else:
            acc_ref = scr[0]
            k = pl.program_id(2)

            @pl.when(k == 0)
            def _():
                acc_ref[...] = part

            @pl.when(k > 0)
            def _():
                acc_ref[...] += part

            @pl.when(k == nk - 1)
            def _():
                finish(acc_ref[...])

    a_blk = a_off // (tm if ta else tk)
    assert a_off % (tm if ta else tk) == 0
    a_spec = (pl.BlockSpec((tk, tm), lambda i, j, k: (k, a_blk + i)) if ta
              else pl.BlockSpec((tm, tk), lambda i, j, k: (i, a_blk + k)))
    if layer is None:
        b_spec = pl.BlockSpec((tn, tk), lambda i, j, k: (j, k)) if tb else pl.BlockSpec((tk, tn), lambda i, j, k: (k, j))
    else:
        b_spec = (pl.BlockSpec((None, tn, tk), lambda i, j, k: (layer, j, k)) if tb
                  else pl.BlockSpec((None, tk, tn), lambda i, j, k: (layer, k, j)))
    o_spec = pl.BlockSpec((tm, tn), lambda i, j, k: (i, j))
    in_specs = [a_spec, b_spec] + ([o_spec] if res is not None else []) + [ANY] * len(dep)
    args = (a, b) + ((res,) if res is not None else ()) + tuple(dep)
    return pl.pallas_call(
        body, name=name, grid=(M // tm, N // tn, nk),
        in_specs=in_specs, out_specs=o_spec,
        out_shape=jax.ShapeDtypeStruct((M, N), out_dtype),
        scratch_shapes=[pltpu.VMEM((tm, tn), F32)] if nk > 1 else [],
        compiler_params=_params("parallel", "parallel", "arbitrary"),
    )(*args)


def _wgrad(a, b, stack, *, shard, group=0, name):
    S, N = b.shape
    M = a.shape[1]
    As, Bs = stack.shape[-2:]
    tk = _tile(S, 1024, 16)
    nk = S // tk
    if shard == "cols":
        n = N // Bs
        tm = _tile(M, 512)
        tn = N
        grid = (M // tm, 1, nk)
        o_spec = pl.BlockSpec((n, tm, Bs), lambda i, j, k: (group, i, 0))
    else:
        per = As * LANES // math.gcd(As, LANES)
        tm = M if M <= 1408 else _tile(M, 1408, per)
        n = tm // As
        tn = _tile(N, 1024)
        grid = (M // tm, N // tn, nk)
        o_spec = pl.BlockSpec((n, As, tn), lambda i, j, k: (i, 0, j))

    def body(a_ref, b_ref, _, o_ref, acc_ref):
        k = pl.program_id(2)
        part = _dot_tn(a_ref[...].astype(BF16), b_ref[...].astype(BF16))

        @pl.when(k == 0)
        def _():
            acc_ref[...] = part

        @pl.when(k > 0)
        def _():
            acc_ref[...] += part

        @pl.when(k == nk - 1)
        def _():
            for q in range(n):
                if shard == "cols":
                    o_ref[q] = acc_ref[:, q * Bs:(q + 1) * Bs].astype(stack.dtype)
                else:
                    o_ref[q] = acc_ref[q * As:(q + 1) * As, :].astype(stack.dtype)

    return pl.pallas_call(
        body, name=name, grid=grid,
        in_specs=[pl.BlockSpec((tk, tm), lambda i, j, k: (k, i)), pl.BlockSpec((tk, tn), lambda i, j, k: (k, j)),
                  pl.BlockSpec(memory_space=pl.ANY)],
        out_specs=o_spec,
        out_shape=jax.ShapeDtypeStruct(stack.shape, stack.dtype),
        input_output_aliases={2: 0},
        scratch_shapes=[pltpu.VMEM((tm, tn), F32)],
        compiler_params=_params("parallel", "parallel", "arbitrary"),
    )(a, b, stack)


def _rmsnorm_fwd(x, w, *, dep=(), name):
    S, D = x.shape
    ts = _tile(S, 512, 16)

    def body(x_ref, w_ref, *rest):
        o_ref = rest[-1]
        xv = x_ref[...]
        r = lax.rsqrt(jnp.mean(xv * xv, axis=-1, keepdims=True) + NORM_EPS)
        o_ref[...] = (xv * r * w_ref[...]).astype(BF16)

    return pl.pallas_call(
        body, name=name, grid=(S // ts,),
        in_specs=[pl.BlockSpec((ts, D), lambda i: (i, 0)), pl.BlockSpec((1, D), lambda i: (0, 0))] + [ANY] * len(dep),
        out_specs=pl.BlockSpec((ts, D), lambda i: (i, 0)),
        out_shape=jax.ShapeDtypeStruct((S, D), BF16),
        compiler_params=_params("parallel"),
    )(x, w, *dep)


def _rmsnorm_bwd(x, w, dh, dres, *, name):
    S, D = x.shape
    ts = _tile(S, 512, 16)
    n = S // ts

    def body(x_ref, w_ref, dh_ref, dr_ref, dx_ref, dw_ref):
        i = pl.program_id(0)
        xv = x_ref[...]
        r = lax.rsqrt(jnp.mean(xv * xv, axis=-1, keepdims=True) + NORM_EPS)
        xh = xv * r
        d = dh_ref[...]
        g = d * w_ref[...]
        dx_ref[...] = dr_ref[...] + r * (g - xh * jnp.mean(g * xh, axis=-1, keepdims=True))
        part = _colsum(d * xh)

        @pl.when(i == 0)
        def _():
            dw_ref[...] = part

        @pl.when(i > 0)
        def _():
            dw_ref[...] += part

    row = pl.BlockSpec((ts, D), lambda i: (i, 0))
    vec = pl.BlockSpec((1, D), lambda i: (0, 0))
    return pl.pallas_call(
        body, name=name, grid=(n,),
        in_specs=[row, vec, row, row], out_specs=[row, vec],
        out_shape=[jax.ShapeDtypeStruct((S, D), F32), jax.ShapeDtypeStruct((1, D), F32)],
        compiler_params=_params("arbitrary"),
    )(x, w, dh, dres)


def _halo_specs(S, ts, tf, row_axis):
    g = ts // SUBLANES
    last = S // SUBLANES - 1
    col_axis = 1 - row_axis
    main = pl.BlockSpec((ts, tf), lambda *ij: (ij[row_axis], ij[col_axis]))
    prev = pl.BlockSpec((SUBLANES, tf), lambda *ij: (jnp.maximum(ij[row_axis] * g - 1, 0), ij[col_axis]))
    nxt = pl.BlockSpec((SUBLANES, tf), lambda *ij: (jnp.minimum((ij[row_axis] + 1) * g, last), ij[col_axis]))
    return [main, prev, nxt]


def _shifted(u, prev_ref, next_ref, i, n):
    ts = u.shape[0]
    rid = lax.broadcasted_iota(jnp.int32, u.shape, 0)
    before = jnp.where(i > 0, prev_ref[SUBLANES - 1:SUBLANES, :], 0.0)
    after = jnp.where(i < n - 1, next_ref[0:1, :], 0.0)
    um1 = jnp.where(rid == 0, before, pltpu.roll(u, 1, 0))
    up1 = jnp.where(rid == ts - 1, after, pltpu.roll(u, ts - 1, 0))
    return um1, up1


def _conv3(u, prev_ref, next_ref, w_ref, i, n):
    um1, up1 = _shifted(u, prev_ref, next_ref, i, n)
    return w_ref[0:1, :] * um1 + w_ref[1:2, :] * u + w_ref[2:3, :] * up1


def _conv_act_fwd(uv, ug, wv, wg, bv, bg, *, name):
    S, F = uv.shape
    ts = _tile(S, 512, 16)
    tf = _tile(F, 1408)
    n = S // ts

    def body(v_ref, vp_ref, vn_ref, g_ref, gp_ref, gn_ref, wv_ref, wg_ref, bv_ref, bg_ref, o_ref):
        i = pl.program_id(0)
        val = _conv3(v_ref[...], vp_ref, vn_ref, wv_ref, i, n) + bv_ref[...]
        gate = _conv3(g_ref[...], gp_ref, gn_ref, wg_ref, i, n) + bg_ref[...]
        o_ref[...] = (gate * _sigmoid(gate) * val).astype(BF16)

    halo = _halo_specs(S, ts, tf, 0)
    w3 = pl.BlockSpec((3, tf), lambda i, j: (0, j))
    b1 = pl.BlockSpec((1, tf), lambda i, j: (0, j))
    return pl.pallas_call(
        body, name=name, grid=(n, F // tf),
        in_specs=halo + halo + [w3, w3, b1, b1],
        out_specs=pl.BlockSpec((ts, tf), lambda i, j: (i, j)),
        out_shape=jax.ShapeDtypeStruct((S, F), BF16),
        compiler_params=_params("parallel", "parallel"),
    )(uv, uv, uv, ug, ug, ug, wv, wg, bv, bg)


def _conv_act_bwd(uv, ug, wv, wg, bv, bg, dact, *, name):
    S, F = uv.shape
    ts = _tile(S, 512, 16)
    tf = _tile(F, 1408)
    n = S // ts

    def body(v_ref, vp_ref, vn_ref, g_ref, gp_ref, gn_ref, wv_ref, wg_ref, bv_ref, bg_ref, da_ref,
             dv_ref, dg_ref, dwv_ref, dwg_ref, dbv_ref, dbg_ref):
        i = pl.program_id(1)
        uvv, ugv = v_ref[...], g_ref[...]
        vm1, vp1 = _shifted(uvv, vp_ref, vn_ref, i, n)
        gm1, gp1 = _shifted(ugv, gp_ref, gn_ref, i, n)
        val = wv_ref[0:1, :] * vm1 + wv_ref[1:2, :] * uvv + wv_ref[2:3, :] * vp1 + bv_ref[...]
        gate = wg_ref[0:1, :] * gm1 + wg_ref[1:2, :] * ugv + wg_ref[2:3, :] * gp1 + bg_ref[...]
        sg = _sigmoid(gate)
        da = da_ref[...]
        dval = da * (gate * sg)
        dgate = da * val * (sg * (1.0 + gate * (1.0 - sg)))
        dv_ref[...] = dval
        dg_ref[...] = dgate
        sums = [(dwv_ref, 0, vm1 * dval), (dwv_ref, 1, uvv * dval), (dwv_ref, 2, vp1 * dval),
                (dwg_ref, 0, gm1 * dgate), (dwg_ref, 1, ugv * dgate), (dwg_ref, 2, gp1 * dgate),
                (dbv_ref, 0, dval), (dbg_ref, 0, dgate)]
        parts = [(ref, r, _colsum(t)) for ref, r, t in sums]

        @pl.when(i == 0)
        def _():
            for ref, r, part in parts:
                ref[r:r + 1, :] = part

        @pl.when(i > 0)
        def _():
            for ref, r, part in parts:
                ref[r:r + 1, :] += part

    halo = _halo_specs(S, ts, tf, 1)
    w3 = pl.BlockSpec((3, tf), lambda j, i: (0, j))
    b1 = pl.BlockSpec((1, tf), lambda j, i: (0, j))
    blk = pl.BlockSpec((ts, tf), lambda j, i: (i, j))
    return pl.pallas_call(
        body, name=name, grid=(F // tf, n),
        in_specs=halo + halo + [w3, w3, b1, b1, blk],
        out_specs=[blk, blk, w3, w3, b1, b1],
        out_shape=[jax.ShapeDtypeStruct((S, F), F32), jax.ShapeDtypeStruct((S, F), F32),
                   jax.ShapeDtypeStruct((3, F), F32), jax.ShapeDtypeStruct((3, F), F32),
                   jax.ShapeDtypeStruct((1, F), F32), jax.ShapeDtypeStruct((1, F), F32)],
        compiler_params=_params("parallel", "arbitrary"),
    )(uv, uv, uv, ug, ug, ug, wv, wg, bv, bg, dact)


def _conv_t(duc, w, *, name):
    S, F = duc.shape
    ts = _tile(S, 512, 16)
    tf = _tile(F, 1408)
    n = S // ts

    def body(d_ref, dp_ref, dn_ref, w_ref, o_ref):
        i = pl.program_id(0)
        d = d_ref[...]
        dm1, dp1 = _shifted(d, dp_ref, dn_ref, i, n)
        o_ref[...] = (w_ref[0:1, :] * dp1 + w_ref[1:2, :] * d + w_ref[2:3, :] * dm1).astype(BF16)

    return pl.pallas_call(
        body, name=name, grid=(n, F // tf),
        in_specs=_halo_specs(S, ts, tf, 0) + [pl.BlockSpec((3, tf), lambda i, j: (0, j))],
        out_specs=pl.BlockSpec((ts, tf), lambda i, j: (i, j)),
        out_shape=jax.ShapeDtypeStruct((S, F), BF16),
        compiler_params=_params("parallel", "parallel"),
    )(duc, duc, duc, w)


N_QK = ATTN_Q_HEADS + ATTN_KV_HEADS
QKV_DIM = (ATTN_Q_HEADS + 2 * ATTN_KV_HEADS) * ATTN_HD


def _head(hd):
    return slice(hd * ATTN_HD, (hd + 1) * ATTN_HD)


def _attn_prep(proj, cs, sn, qn, kn, *, name):
    S = proj.shape[0]
    ts = _tile(S, 256, 16)

    def body(p_ref, c_ref, s_ref, qn_ref, kn_ref, o_ref):
        c, s = c_ref[...], s_ref[...]
        for hd in range(N_QK):
            xv = p_ref[:, _head(hd)]
            w = qn_ref[...] if hd < ATTN_Q_HEADS else kn_ref[...]
            r = lax.rsqrt(jnp.mean(xv * xv, axis=-1, keepdims=True) + NORM_EPS)
            nrm = xv * r * w
            o_ref[:, _head(hd)] = (nrm * c + pltpu.roll(nrm, ATTN_HD // 2, 1) * s).astype(BF16)
        o_ref[:, N_QK * ATTN_HD:] = p_ref[:, N_QK * ATTN_HD:].astype(BF16)

    row = pl.BlockSpec((ts, QKV_DIM), lambda i: (i, 0))
    rot = pl.BlockSpec((ts, ATTN_HD), lambda i: (i, 0))
    vec = pl.BlockSpec((1, ATTN_HD), lambda i: (0, 0))
    return pl.pallas_call(
        body, name=name, grid=(S // ts,),
        in_specs=[row, rot, rot, vec, vec], out_specs=row,
        out_shape=jax.ShapeDtypeStruct((S, QKV_DIM), BF16),
        compiler_params=_params("parallel"),
    )(proj, cs, sn, qn, kn)


def _attn_prep_bwd(proj, dq, dk, dv, cs, sn, qn, kn, *, name):
    S = proj.shape[0]
    ts = _tile(S, 256, 16)
    nq = ATTN_Q_HEADS * ATTN_HD
    nkv = ATTN_KV_HEADS * ATTN_HD

    def body(p_ref, dq_ref, dk_ref, dv_ref, c_ref, s_ref, qn_ref, kn_ref, o_ref, dqn_ref, dkn_ref):
        i = pl.program_id(0)
        c, s = c_ref[...], s_ref[...]
        acc = [jnp.zeros((1, ATTN_HD), F32), jnp.zeros((1, ATTN_HD), F32)]
        for hd in range(N_QK):
            is_k = hd >= ATTN_Q_HEADS
            xv = p_ref[:, _head(hd)]
            w = kn_ref[...] if is_k else qn_ref[...]
            r = lax.rsqrt(jnp.mean(xv * xv, axis=-1, keepdims=True) + NORM_EPS)
            xh = xv * r
            dout = dk_ref[:, _head(hd - ATTN_Q_HEADS)] if is_k else dq_ref[:, _head(hd)]
            dn = dout * c + pltpu.roll(dout * s, ATTN_HD // 2, 1)
            acc[int(is_k)] = acc[int(is_k)] + _colsum(dn * xh)
            g = dn * w
            o_ref[:, _head(hd)] = (r * (g - xh * jnp.mean(g * xh, axis=-1, keepdims=True))).astype(BF16)
        o_ref[:, N_QK * ATTN_HD:] = dv_ref[...].astype(BF16)

        @pl.when(i == 0)
        def _():
            dqn_ref[...] = acc[0]
            dkn_ref[...] = acc[1]

        @pl.when(i > 0)
        def _():
            dqn_ref[...] += acc[0]
            dkn_ref[...] += acc[1]

    row = pl.BlockSpec((ts, QKV_DIM), lambda i: (i, 0))
    rot = pl.BlockSpec((ts, ATTN_HD), lambda i: (i, 0))
    vec = pl.BlockSpec((1, ATTN_HD), lambda i: (0, 0))
    return pl.pallas_call(
        body, name=name, grid=(S // ts,),
        in_specs=[row, pl.BlockSpec((ts, nq), lambda i: (i, 0)), pl.BlockSpec((ts, nkv), lambda i: (i, 0)),
                  pl.BlockSpec((ts, nkv), lambda i: (i, 0)), rot, rot, vec, vec],
        out_specs=[row, vec, vec],
        out_shape=[jax.ShapeDtypeStruct((S, QKV_DIM), BF16), jax.ShapeDtypeStruct((1, ATTN_HD), F32),
                   jax.ShapeDtypeStruct((1, ATTN_HD), F32)],
        compiler_params=_params("arbitrary"),
    )(proj, dq, dk, dv, cs, sn, qn, kn)


ATTN_SCALE = ATTN_HD ** -0.5


def _softmax_parts(q, k):
    s = _dot_nt(q, k)
    e = jnp.exp2((s - jnp.max(s, axis=-1, keepdims=True)) * (ATTN_SCALE * math.log2(math.e)))
    return e, 1.0 / jnp.sum(e, axis=-1, keepdims=True)


def _attn_fwd(qkv, *, name):
    S = qkv.shape[0]
    tq = _tile(S, 256, 16)

    def body(q_ref, k_ref, v_ref, o_ref):
        e, rl = _softmax_parts(q_ref[...], k_ref[...])
        o_ref[...] = (_dot(e.astype(BF16), v_ref[...]) * rl).astype(BF16)

    return pl.pallas_call(
        body, name=name, grid=(ATTN_Q_HEADS, S // tq),
        in_specs=[pl.BlockSpec((tq, ATTN_HD), lambda h, i: (i, h)),
                  pl.BlockSpec((S, ATTN_HD), lambda h, i: (0, ATTN_Q_HEADS + h // ATTN_GROUP)),
                  pl.BlockSpec((S, ATTN_HD), lambda h, i: (0, N_QK + h // ATTN_GROUP))],
        out_specs=pl.BlockSpec((tq, ATTN_HD), lambda h, i: (i, h)),
        out_shape=jax.ShapeDtypeStruct((S, ATTN_Q_HEADS * ATTN_HD), BF16),
        compiler_params=_params("parallel", "parallel"),
    )(qkv, qkv, qkv)


def _attn_bwd(qkv, do, *, name):
    S = qkv.shape[0]
    tq = _tile(S, 256, 16)

    def body(q_ref, k_ref, v_ref, do_ref, dq_ref, dk_ref, dv_ref):
        first = jnp.logical_and(pl.program_id(1) == 0, pl.program_id(2) == 0)
        q, k, dov = q_ref[...], k_ref[...], do_ref[...]
        e, rl = _softmax_parts(q, k)
        dp = _dot_nt(dov, v_ref[...])
        delta = jnp.sum(e * dp, axis=-1, keepdims=True) * rl
        dsb = (e * (dp - delta) * (rl * ATTN_SCALE)).astype(BF16)
        dq_ref[...] = _dot(dsb, k)
        dk = _dot_tn(dsb, q)
        dv = _dot_tn(e.astype(BF16), (dov.astype(F32) * rl).astype(BF16))

        @pl.when(first)
        def _():
            dk_ref[...] = dk
            dv_ref[...] = dv

        @pl.when(jnp.logical_not(first))
        def _():
            dk_ref[...] += dk
            dv_ref[...] += dv

    qblk = pl.BlockSpec((tq, ATTN_HD), lambda kv, g, i: (i, kv * ATTN_GROUP + g))
    kvacc = pl.BlockSpec((S, ATTN_HD), lambda kv, g, i: (0, kv))
    return pl.pallas_call(
        body, name=name, grid=(ATTN_KV_HEADS, ATTN_GROUP, S // tq),
        in_specs=[qblk,
                  pl.BlockSpec((S, ATTN_HD), lambda kv, g, i: (0, ATTN_Q_HEADS + kv)),
                  pl.BlockSpec((S, ATTN_HD), lambda kv, g, i: (0, N_QK + kv)),
                  qblk],
        out_specs=[qblk, kvacc, kvacc],
        out_shape=[jax.ShapeDtypeStruct((S, ATTN_Q_HEADS * ATTN_HD), F32),
                   jax.ShapeDtypeStruct((S, ATTN_KV_HEADS * ATTN_HD), F32),
                   jax.ShapeDtypeStruct((S, ATTN_KV_HEADS * ATTN_HD), F32)],
        compiler_params=_params("parallel", "arbitrary", "arbitrary"),
    )(qkv, qkv, qkv, do)


GLA_KD = GLA_HEADS * GLA_DK
GLA_VD = GLA_HEADS * GLA_DV
GLA_PROJ = 2 * GLA_KD + 2 * GLA_VD + LANES
GLA_SCALE = GLA_DK ** -0.5


def _split3(x):
    hi = x.astype(BF16)
    r1 = x - hi.astype(F32)
    mid = r1.astype(BF16)
    lo = (r1 - mid.astype(F32)).astype(BF16)
    return hi, mid, lo


def _cumdot(t, x):
    hi, mid, lo = _split3(x)
    return _dot(t, hi) + _dot(t, mid) + _dot(t, lo)


def _gla_masks(d):
    c = GLA_CHUNK
    row = lax.broadcasted_iota(jnp.int32, (c, c), 0)
    col = lax.broadcasted_iota(jnp.int32, (c, c), 1)
    lower, upper = col <= row, col >= row
    if d == 0:
        return lower.astype(BF16), upper.astype(BF16), lower
    return upper.astype(BF16), lower.astype(BF16), col > row


def _gla_decay(lg, bias, cum, d):
    xl = lg + bias
    la = _log_sigmoid(xl) * (1.0 / GLA_GATE_NORMALIZER)
    b = _cumdot(cum, la)
    b_end = b[GLA_CHUNK - 1:GLA_CHUNK, :] if d == 0 else b[0:1, :]
    return xl, b, b_end


def _gla_specs(S, n):
    c = GLA_CHUNK
    up = lambda i: i
    down = lambda i: n - 1 - i
    def specs(order):
        return dict(
            q=pl.BlockSpec((c, GLA_KD), lambda i: (order(i), 0)),
            k=pl.BlockSpec((c, GLA_KD), lambda i: (order(i), 1)),
            v=pl.BlockSpec((c, GLA_VD), lambda i: (order(i), 1)),
            st=pl.BlockSpec((1, GLA_HEADS, GLA_DV, GLA_DK), lambda i: (order(i), 0, 0, 0)),
            wide=pl.BlockSpec((c, GLA_VD), lambda i: (order(i), 0)),
            qkv=pl.BlockSpec((c, 2 * GLA_KD + GLA_VD), lambda i: (order(i), 0)),
        )
    return specs(up), specs(down), up, down


def _gla_fwd(proj, logits, bias, *, name):
    S = proj.shape[0]
    c = GLA_CHUNK
    n = S // c
    su, sd, up, down = _gla_specs(S, n)

    def body(qf, kf, vf, lf, qb, kb, vb, lb, bias_ref, of, ob, sf, sb, st):
        @pl.when(pl.program_id(0) == 0)
        def _():
            st[...] = jnp.zeros_like(st)

        for d, (q_r, k_r, v_r, l_r, o_r, s_r) in enumerate(((qf, kf, vf, lf, of, sf), (qb, kb, vb, lb, ob, sb))):
            cum, _, mask = _gla_masks(d)
            _, b, b_end = _gla_decay(l_r[...], bias_ref[:, d * GLA_KD:(d + 1) * GLA_KD], cum, d)
            dend = jnp.exp(b_end)
            k = k_r[...]
            qd = (q_r[...] * GLA_SCALE * jnp.exp(b)).astype(BF16)
            ki = (k * jnp.exp(-b)).astype(BF16)
            ke = (k * jnp.exp(b_end - b)).astype(BF16)
            for h in range(GLA_HEADS):
                ks = slice(h * GLA_DK, (h + 1) * GLA_DK)
                vs = slice(h * GLA_DV, (h + 1) * GLA_DV)
                stp = st[d * GLA_HEADS + h]
                s_r[0, h] = stp
                v = v_r[:, vs].astype(BF16)
                att = jnp.where(mask, _dot_nt(qd[:, ks], ki[:, ks]), 0.0).astype(BF16)
                o_r[:, vs] = _dot(att, v) + _dot_nt(qd[:, ks], stp.astype(BF16))
                st[d * GLA_HEADS + h] = stp * dend[:, ks] + _dot_tn(v, ke[:, ks])

    lg_f = pl.BlockSpec((c, GLA_KD), lambda i: (up(i), 0))
    lg_b = pl.BlockSpec((c, GLA_KD), lambda i: (down(i), 1))
    return pl.pallas_call(
        body, name=name, grid=(n,),
        in_specs=[su["q"], su["k"], su["v"], lg_f, sd["q"], sd["k"], sd["v"], lg_b,
                  pl.BlockSpec((1, 2 * GLA_KD), lambda i: (0, 0))],
        out_specs=[su["wide"], sd["wide"], su["st"], sd["st"]],
        out_shape=[jax.ShapeDtypeStruct((S, GLA_VD), F32), jax.ShapeDtypeStruct((S, GLA_VD), F32),
                   jax.ShapeDtypeStruct((n, GLA_HEADS, GLA_DV, GLA_DK), F32),
                   jax.ShapeDtypeStruct((n, GLA_HEADS, GLA_DV, GLA_DK), F32)],
        scratch_shapes=[pltpu.VMEM((2 * GLA_HEADS, GLA_DV, GLA_DK), F32)],
        compiler_params=_params("arbitrary"),
    )(proj, proj, proj, logits, proj, proj, proj, logits, bias)


def _gla_bwd(proj, logits, bias, sf, sb, do, *, name):
    S = proj.shape[0]
    c = GLA_CHUNK
    n = S // c
    su, sd, up, down = _gla_specs(S, n)

    def body(qf, kf, vf, lf, stf, dof, qb, kb, vb, lb, stb, dob, bias_ref,
             dqkv_f, dlg_f, dqkv_b, dlg_b, dbias, dst):
        first = pl.program_id(0) == 0

        @pl.when(first)
        def _():
            dst[...] = jnp.zeros_like(dst)

        dbias_parts = []
        for d, (q_r, k_r, v_r, l_r, s_r, do_r, dqkv_r, dlg_r) in enumerate(
                ((qf, kf, vf, lf, stf, dof, dqkv_f, dlg_f), (qb, kb, vb, lb, stb, dob, dqkv_b, dlg_b))):
            cum, cum_t, mask = _gla_masks(d)
            xl, b, b_end = _gla_decay(l_r[...], bias_ref[:, d * GLA_KD:(d + 1) * GLA_KD], cum, d)
            e, ei, ee, dend = jnp.exp(b), jnp.exp(-b), jnp.exp(b_end - b), jnp.exp(b_end)
            k = k_r[...]
            qd32 = q_r[...] * GLA_SCALE * e
            ki32 = k * ei
            ke32 = k * ee
            qd, ki, ke = qd32.astype(BF16), ki32.astype(BF16), ke32.astype(BF16)
            db_parts, dbe_parts = [], []
            for h in range(GLA_HEADS):
                ks = slice(h * GLA_DK, (h + 1) * GLA_DK)
                vs = slice(h * GLA_DV, (h + 1) * GLA_DV)
                stp = s_r[0, h]
                dstn = dst[d * GLA_HEADS + h]
                dstn_b = dstn.astype(BF16)
                v = v_r[:, vs].astype(BF16)
                dov = do_r[:, vs]
                att = jnp.where(mask, _dot_nt(qd[:, ks], ki[:, ks]), 0.0).astype(BF16)
                datt = jnp.where(mask, _dot_nt(dov, v), 0.0).astype(BF16)
                dqkv_r[:, 2 * GLA_KD + h * GLA_DV:2 * GLA_KD + (h + 1) * GLA_DV] = (
                    _dot_tn(att, dov) + _dot_nt(ke[:, ks], dstn_b))
                dqd = _dot(datt, ki[:, ks]) + _dot(dov, stp.astype(BF16))
                dki = _dot_tn(datt, qd[:, ks])
                dke = _dot(v, dstn_b)
                d_dend = _colsum(stp * dstn)
                dst[d * GLA_HEADS + h] = _dot_tn(dov, qd[:, ks]) + dstn * dend[:, ks]
                dqkv_r[:, ks] = dqd * e[:, ks] * GLA_SCALE
                dqkv_r[:, GLA_KD + h * GLA_DK:GLA_KD + (h + 1) * GLA_DK] = dki * ei[:, ks] + dke * ee[:, ks]
                dke_ke = dke * ke32[:, ks]
                db_parts.append(dqd * qd32[:, ks] - dki * ki32[:, ks] - dke_ke)
                dbe_parts.append(_colsum(dke_ke) + d_dend * dend[:, ks])
            db = jnp.concatenate(db_parts, axis=1)
            db_end = jnp.concatenate(dbe_parts, axis=1)
            dla = _cumdot(cum_t, db) + db_end
            dlg = dla * (1.0 / GLA_GATE_NORMALIZER) * _sigmoid(-xl)
            dlg_r[...] = dlg
            dbias_parts.append(_colsum(dlg))
        dbv = jnp.concatenate(dbias_parts, axis=1)

        @pl.when(first)
        def _():
            dbias[...] = dbv

        @pl.when(jnp.logical_not(first))
        def _():
            dbias[...] += dbv

    lg_f = pl.BlockSpec((c, GLA_KD), lambda i: (down(i), 0))
    lg_b = pl.BlockSpec((c, GLA_KD), lambda i: (up(i), 1))
    dlg_f = pl.BlockSpec((c, GLA_KD), lambda i: (down(i), 0))
    dlg_b = pl.BlockSpec((c, GLA_KD), lambda i: (up(i), 0))
    return pl.pallas_call(
        body, name=name, grid=(n,),
        in_specs=[sd["q"], sd["k"], sd["v"], lg_f, sd["st"], sd["wide"],
                  su["q"], su["k"], su["v"], lg_b, su["st"], su["wide"],
                  pl.BlockSpec((1, 2 * GLA_KD), lambda i: (0, 0))],
        out_specs=[sd["qkv"], dlg_f, su["qkv"], dlg_b, pl.BlockSpec((1, 2 * GLA_KD), lambda i: (0, 0))],
        out_shape=[jax.ShapeDtypeStruct((S, 2 * GLA_KD + GLA_VD), F32), jax.ShapeDtypeStruct((S, GLA_KD), F32),
                   jax.ShapeDtypeStruct((S, 2 * GLA_KD + GLA_VD), F32), jax.ShapeDtypeStruct((S, GLA_KD), F32),
                   jax.ShapeDtypeStruct((1, 2 * GLA_KD), F32)],
        scratch_shapes=[pltpu.VMEM((2 * GLA_HEADS, GLA_DV, GLA_DK), F32)],
        compiler_params=_params("arbitrary"),
    )(proj, proj, proj, logits, sf, do, proj, proj, proj, logits, sb, do, bias)


def _gla_gate_fwd(of, ob, proj, w, *, name):
    S = of.shape[0]
    ts = _tile(S, 256, 16)

    def body(of_ref, ob_ref, g_ref, w_ref, y_ref):
        for h in range(GLA_HEADS):
            vs = slice(h * GLA_DV, (h + 1) * GLA_DV)
            o = of_ref[:, vs] + ob_ref[:, vs]
            r = lax.rsqrt(jnp.mean(o * o, axis=-1, keepdims=True) + NORM_EPS)
            g = g_ref[:, vs]
            y_ref[:, vs] = (o * r * w_ref[...] * (g * _sigmoid(g))).astype(BF16)

    wide = pl.BlockSpec((ts, GLA_VD), lambda i: (i, 0))
    return pl.pallas_call(
        body, name=name, grid=(S // ts,),
        in_specs=[wide, wide, pl.BlockSpec((ts, GLA_VD), lambda i: (i, 2)), pl.BlockSpec((1, GLA_DV), lambda i: (0, 0))],
        out_specs=wide,
        out_shape=jax.ShapeDtypeStruct((S, GLA_VD), BF16),
        compiler_params=_params("parallel"),
    )(of, ob, proj, w)


def _gla_gate_bwd(of, ob, proj, w, dy, *, name):
    S = of.shape[0]
    ts = _tile(S, 256, 16)

    def body(of_ref, ob_ref, g_ref, w_ref, dy_ref, do_ref, dg_ref, dw_ref):
        i = pl.program_id(0)
        acc = jnp.zeros((1, GLA_DV), F32)
        for h in range(GLA_HEADS):
            vs = slice(h * GLA_DV, (h + 1) * GLA_DV)
            o = of_ref[:, vs] + ob_ref[:, vs]
            r = lax.rsqrt(jnp.mean(o * o, axis=-1, keepdims=True) + NORM_EPS)
            oh = o * r
            g = g_ref[:, vs]
            sg = _sigmoid(g)
            dyv = dy_ref[:, vs]
            dn = dyv * (g * sg)
            dg_ref[:, vs] = dyv * (oh * w_ref[...]) * (sg * (1.0 + g * (1.0 - sg)))
            acc = acc + _colsum(dn * oh)
            gg = dn * w_ref[...]
            do_ref[:, vs] = (r * (gg - oh * jnp.mean(gg * oh, axis=-1, keepdims=True))).astype(BF16)

        @pl.when(i == 0)
        def _():
            dw_ref[...] = acc

        @pl.when(i > 0)
        def _():
            dw_ref[...] += acc

    wide = pl.BlockSpec((ts, GLA_VD), lambda i: (i, 0))
    vec = pl.BlockSpec((1, GLA_DV), lambda i: (0, 0))
    return pl.pallas_call(
        body, name=name, grid=(S // ts,),
        in_specs=[wide, wide, pl.BlockSpec((ts, GLA_VD), lambda i: (i, 2)), vec, wide],
        out_specs=[wide, wide, vec],
        out_shape=[jax.ShapeDtypeStruct((S, GLA_VD), BF16), jax.ShapeDtypeStruct((S, GLA_VD), F32),
                   jax.ShapeDtypeStruct((1, GLA_DV), F32)],
        compiler_params=_params("arbitrary"),
    )(of, ob, proj, w, dy)


def _gla_combine(dqkv_f, dqkv_b, dg, dr, *, name):
    S = dg.shape[0]
    ts = _tile(S, 512, 16)
    nqkv = 2 * GLA_KD + GLA_VD

    def body(f_ref, b_ref, g_ref, r_ref, o_ref):
        o_ref[:, :nqkv] = (f_ref[...] + b_ref[...]).astype(BF16)
        o_ref[:, nqkv:nqkv + GLA_VD] = g_ref[...].astype(BF16)
        o_ref[:, nqkv + GLA_VD:] = r_ref[...].astype(BF16)

    return pl.pallas_call(
        body, name=name, grid=(S // ts,),
        in_specs=[pl.BlockSpec((ts, nqkv), lambda i: (i, 0)), pl.BlockSpec((ts, nqkv), lambda i: (i, 0)),
                  pl.BlockSpec((ts, GLA_VD), lambda i: (i, 0)), pl.BlockSpec((ts, LANES), lambda i: (i, 0))],
        out_specs=pl.BlockSpec((ts, GLA_PROJ), lambda i: (i, 0)),
        out_shape=jax.ShapeDtypeStruct((S, GLA_PROJ), BF16),
        compiler_params=_params("parallel"),
    )(dqkv_f, dqkv_b, dg, dr)


def _loss_head(y, t, *, name):
    S, D = y.shape
    ts = _tile(S, 512, 16)
    n = S // ts

    def body(y_ref, t_ref, dy_ref, l_ref, acc):
        i = pl.program_id(0)
        diff = y_ref[...] - t_ref[...]
        dy_ref[...] = diff * (1.0 / D)
        part = _colsum(diff * diff)

        @pl.when(i == 0)
        def _():
            acc[...] = part

        @pl.when(i > 0)
        def _():
            acc[...] += part

        @pl.when(i == n - 1)
        def _():
            l_ref[...] = jnp.full(l_ref.shape, 0.5 / D, F32) * jnp.sum(acc[...])

    row = pl.BlockSpec((ts, D), lambda i: (i, 0))
    return pl.pallas_call(
        body, name=name, grid=(n,),
        in_specs=[row, row],
        out_specs=[row, pl.BlockSpec((SUBLANES, LANES), lambda i: (0, 0))],
        out_shape=[jax.ShapeDtypeStruct((S, D), F32), jax.ShapeDtypeStruct((SUBLANES, LANES), F32)],
        scratch_shapes=[pltpu.VMEM((1, D), F32)],
        compiler_params=_params("arbitrary"),
    )(y, t)


def _adamw(w, g, m, v, *, name):
    R, C = w.shape
    tr = _tile(R, 256, SUBLANES)

    def body(w_ref, g_ref, m_ref, v_ref, d_ref, nm_ref, nv_ref):
        gv = g_ref[...]
        nm = ADAM_B1 * m_ref[...] + (1.0 - ADAM_B1) * gv
        nv = ADAM_B2 * v_ref[...] + (1.0 - ADAM_B2) * (gv * gv)
        m_hat = nm / (1.0 - ADAM_B1 ** ADAM_STEP)
        v_hat = nv / (1.0 - ADAM_B2 ** ADAM_STEP)
        d_ref[...] = -ADAM_LR * (m_hat / (jnp.sqrt(v_hat) + ADAM_EPS) + ADAM_WD * w_ref[...])
        nm_ref[...] = nm
        nv_ref[...] = nv

    blk = pl.BlockSpec((tr, C), lambda i: (i, 0))
    shp = jax.ShapeDtypeStruct((R, C), F32)
    return pl.pallas_call(
        body, name=name, grid=(R // tr,),
        in_specs=[blk] * 4, out_specs=[blk] * 3, out_shape=[shp] * 3,
        compiler_params=_params("parallel"),
    )(w, g, m, v)


def _place():
    return lax.axis_index("x"), lax.axis_index("y"), lax.axis_index("c")


def _all_gather(block, *, name):
    R, L = block.shape

    def body(x_ref, out_ref, send_sems, recv_sems, local_sem):
        x, y, c = _place()
        me, sibling = (x, y, c), (x, y, 1 - c)
        chips = [(1 - x, y), (x, 1 - y), (1 - x, 1 - y)]

        def slot(px, py, pc):
            return out_ref.at[4 * px + 2 * py + pc]

        def copy(k, blk, to, src=None):
            return pltpu.make_async_remote_copy(
                src_ref=slot(*blk) if src is None else src, dst_ref=slot(*blk),
                send_sem=send_sems.at[k], recv_sem=recv_sems.at[k], device_id=to, device_id_type=MESH)

        mine = pltpu.make_async_copy(x_ref, slot(*me), local_sem)
        mine.start()
        first = [copy(0, me, sibling, src=x_ref)]
        first += [copy(1 + j, me, (*chip, c), src=x_ref) for j, chip in enumerate(chips)]
        for cp in first:
            cp.start()
        passed = [copy(4 + j, (*chip, c), sibling) for j, chip in enumerate(chips)]
        for j, chip in enumerate(chips):
            copy(1 + j, (*chip, c), me).wait_recv()
            passed[j].start()
        copy(0, sibling, me).wait_recv()
        for j, chip in enumerate(chips):
            copy(4 + j, (*chip, 1 - c), me).wait_recv()
        for cp in first + passed:
            cp.wait_send()
        mine.wait()

    return pl.pallas_call(
        body, name=name, in_specs=[ANY], out_specs=ANY,
        out_shape=jax.ShapeDtypeStruct((N_DEV, R, L), block.dtype),
        scratch_shapes=[pltpu.SemaphoreType.DMA((7,)), pltpu.SemaphoreType.DMA((7,)), pltpu.SemaphoreType.DMA],
    )(block)


HBM_SPEC = pl.BlockSpec(memory_space=pltpu.HBM)
SEM_SPEC = pl.BlockSpec(memory_space=pltpu.SEMAPHORE)
DATAFLOW = pltpu.SideEffectType.DATAFLOW_SIDE_EFFECTING


def _split_start(plan, srcs, lands, *, name):
    make_copies, count = plan
    ns, nb = len(srcs), len(srcs) + len(lands)
    bufs = [pltpu.with_memory_space_constraint(a, pltpu.HBM) for a in list(srcs) + list(lands)]

    def body(*refs):
        send_sems, recv_sems, token = refs[nb], refs[nb + 1], refs[-1]
        for cp in make_copies(refs[:ns], refs[ns:nb], send_sems, recv_sems):
            cp.start()
        token[...] = jnp.zeros_like(token)

    outs = pl.pallas_call(
        body, name=name, in_specs=[HBM_SPEC] * nb,
        out_specs=(SEM_SPEC, SEM_SPEC, *[HBM_SPEC] * nb, pl.BlockSpec(memory_space=pltpu.VMEM)),
        out_shape=(pltpu.SemaphoreType.DMA((count,)), pltpu.SemaphoreType.DMA((count,)),
                   *[pltpu.HBM(a.shape, a.dtype) for a in bufs], jax.ShapeDtypeStruct((SUBLANES, LANES), F32)),
        input_output_aliases={i: 2 + i for i in range(nb)},
        compiler_params=pltpu.CompilerParams(has_side_effects=DATAFLOW),
    )(*bufs)
    return dict(plan=plan, ns=ns, send=outs[0], recv=outs[1], bufs=list(outs[2:2 + nb]), token=outs[-1])


def _split_wait(started, after, *, name):
    make_copies, _ = started["plan"]
    ns, nb = started["ns"], len(started["bufs"])

    def body(*refs):
        for cp in make_copies(refs[:ns], refs[ns:nb], refs[nb], refs[nb + 1]):
            cp.wait_send()
            cp.wait_recv()

    outs = pl.pallas_call(
        body, name=name, in_specs=[HBM_SPEC] * nb + [SEM_SPEC, SEM_SPEC, ANY],
        out_specs=[HBM_SPEC] * nb,
        out_shape=[pltpu.HBM(a.shape, a.dtype) for a in started["bufs"]],
        input_output_aliases={i: i for i in range(nb)},
        compiler_params=pltpu.CompilerParams(has_side_effects=DATAFLOW),
    )(*started["bufs"], started["send"], started["recv"], after)
    return list(outs[:ns]), list(outs[ns:])


def _remote(src, dst, send_sems, recv_sems, k, to):
    return pltpu.make_async_remote_copy(src_ref=src, dst_ref=dst, send_sem=send_sems.at[k], recv_sem=recv_sems.at[k],
                                        device_id=to, device_id_type=MESH)


def _other_chips(x, y):
    return [(1 - x, y), (x, 1 - y), (1 - x, 1 - y)]


def _gather_send_plan(n):
    def make(srcs, lands, send_sems, recv_sems):
        x, y, c = _place()
        targets = [(x, y, 1 - c)] + [(cx, cy, c) for cx, cy in _other_chips(x, y)]
        return [_remote(srcs[t], lands[t].at[4 * x + 2 * y + c], send_sems, recv_sems, 4 * t + k, to)
                for t in range(n) for k, to in enumerate(targets)]
    return make, 4 * n


def _gather_pass_plan(n):
    def make(srcs, lands, send_sems, recv_sems):
        x, y, c = _place()
        cps = []
        for t in range(n):
            for j, (cx, cy) in enumerate(_other_chips(x, y)):
                slot = lands[t].at[4 * cx + 2 * cy + c]
                cps.append(_remote(slot, slot, send_sems, recv_sems, 3 * t + j, (x, y, 1 - c)))
        return cps
    return make, 3 * n


def _reduce_sibling_plan(n):
    def make(srcs, lands, send_sems, recv_sems):
        x, y, c = _place()
        return [_remote(srcs[t].at[2 * k + 1 - c], lands[t].at[k], send_sems, recv_sems, 4 * t + k, (x, y, 1 - c))
                for t in range(n) for k in range(4)]
    return make, 4 * n


def _reduce_chip_plan(n):
    def make(srcs, lands, send_sems, recv_sems):
        x, y, c = _place()
        return [_remote(srcs[t].at[2 * cx + cy], lands[t].at[2 * x + y], send_sems, recv_sems, 3 * t + j, (cx, cy, c))
                for t in range(n) for j, (cx, cy) in enumerate(_other_chips(x, y))]
    return make, 3 * n


def _unshard_cols(g, own, dev_idx, groups, width, *, name):
    _, A, Bs = g.shape
    ta = _tile(A, 256, 16)

    def body(dev_ref, g_ref, own_ref, *o_refs):
        for o_ref, devs in zip(o_refs, groups):
            for q, d in enumerate(devs):
                o_ref[:, q * Bs:(q + 1) * Bs] = jnp.where(dev_ref[0] == d, own_ref[...], g_ref[d])
            if len(devs) * Bs < width:
                o_ref[:, len(devs) * Bs:] = jnp.zeros((ta, width - len(devs) * Bs), g.dtype)

    return pl.pallas_call(
        body, name=name,
        grid_spec=pltpu.PrefetchScalarGridSpec(
            num_scalar_prefetch=1, grid=(A // ta,),
            in_specs=[pl.BlockSpec((N_DEV, ta, Bs), lambda i, d: (0, i, 0)), pl.BlockSpec((ta, Bs), lambda i, d: (i, 0))],
            out_specs=[pl.BlockSpec((ta, width), lambda i, d: (i, 0)) for _ in groups]),
        out_shape=[jax.ShapeDtypeStruct((A, width), g.dtype) for _ in groups],
        compiler_params=_params("parallel"),
    )(dev_idx, g, own)


def _place_own(g, own, dev_idx, *, name):
    _, As, B = g.shape
    ta = _tile(As, 256, 16)

    def body(dev_ref, _, own_ref, o_ref):
        o_ref[...] = own_ref[...]

    out = pl.pallas_call(
        body, name=name,
        grid_spec=pltpu.PrefetchScalarGridSpec(
            num_scalar_prefetch=1, grid=(As // ta,),
            in_specs=[ANY, pl.BlockSpec((ta, B), lambda i, d: (i, 0))],
            out_specs=pl.BlockSpec((None, ta, B), lambda i, d: (d[0], i, 0))),
        out_shape=jax.ShapeDtypeStruct(g.shape, g.dtype),
        input_output_aliases={1: 0},
        compiler_params=_params("parallel"),
    )(dev_idx, g, own)
    return out.reshape(N_DEV * As, B)


def _add_sibling(g, buf, c_idx, *, name):
    _, A, B = g.shape
    ta = _tile(A, 256, 16)

    def body(c_ref, g_ref, b_ref, o_ref):
        o_ref[...] = (g_ref[...].astype(F32) + b_ref[...].astype(F32)).astype(BF16)

    blk = pl.BlockSpec((None, ta, B), lambda k, i, c_ref: (k, i, 0))
    return pl.pallas_call(
        body, name=name,
        grid_spec=pltpu.PrefetchScalarGridSpec(
            num_scalar_prefetch=1, grid=(4, A // ta),
            in_specs=[pl.BlockSpec((None, ta, B), lambda k, i, c_ref: (2 * k + c_ref[0], i, 0)), blk],
            out_specs=blk),
        out_shape=jax.ShapeDtypeStruct((4, A, B), BF16),
        compiler_params=_params("parallel", "parallel"),
    )(c_idx, g, buf)


def _adamw_layer(w, m, v, own, parts, chip_idx, outs, layer, *, name):
    _, A, B = w.shape
    ta = _tile(A, 256, 16)

    def body(chip_ref, w_ref, m_ref, v_ref, own_ref, p_ref, *rest):
        g_ref, d_ref, nm_ref, nv_ref = rest[4:]
        gv = None
        for j in range(4):
            part = jnp.where(chip_ref[0] == j, own_ref[...], p_ref[j]).astype(F32)
            gv = part if gv is None else gv + part
        nm = ADAM_B1 * m_ref[...] + (1.0 - ADAM_B1) * gv
        nv = ADAM_B2 * v_ref[...] + (1.0 - ADAM_B2) * (gv * gv)
        m_hat = nm / (1.0 - ADAM_B1 ** ADAM_STEP)
        v_hat = nv / (1.0 - ADAM_B2 ** ADAM_STEP)
        g_ref[...] = gv
        d_ref[...] = -ADAM_LR * (m_hat / (jnp.sqrt(v_hat) + ADAM_EPS) + ADAM_WD * w_ref[...])
        nm_ref[...] = nm
        nv_ref[...] = nv

    blk = pl.BlockSpec((None, ta, B), lambda i, ch: (layer, i, 0))
    return pl.pallas_call(
        body, name=name,
        grid_spec=pltpu.PrefetchScalarGridSpec(
            num_scalar_prefetch=1, grid=(A // ta,),
            in_specs=[blk, blk, blk, pl.BlockSpec((None, ta, B), lambda i, ch: (ch[0], i, 0)),
                      pl.BlockSpec((4, ta, B), lambda i, ch: (0, i, 0))] + [ANY] * 4,
            out_specs=[blk] * 4),
        out_shape=[jax.ShapeDtypeStruct(o.shape, o.dtype) for o in outs],
        input_output_aliases={6 + q: q for q in range(4)},
        compiler_params=_params("parallel"),
    )(chip_idx, w, m, v, own, parts, *outs)


def _sum_slots(buf, *, name):
    n, R, L = buf.shape
    tr = _tile(R, 512, SUBLANES)

    def body(b_ref, o_ref):
        acc = b_ref[0]
        for j in range(1, n):
            acc = acc + b_ref[j]
        o_ref[...] = acc

    return pl.pallas_call(
        body, name=name, grid=(R // tr,),
        in_specs=[pl.BlockSpec((n, tr, L), lambda i: (0, i, 0))],
        out_specs=pl.BlockSpec((tr, L), lambda i: (i, 0)),
        out_shape=jax.ShapeDtypeStruct((R, L), buf.dtype),
        compiler_params=_params("parallel"),
    )(buf)


BIG = ("gla_w_in", "gla_w_out", "attn_w_qkv", "attn_w_out", "ffn_w_up", "ffn_w_down")
SMALL_SHARDED = ("gla_w_gate_up_f", "gla_w_gate_up_b", "ffn_w_conv")
REPLICATED = ("norm_mix", "norm_ffn", "gla_b_gate_f", "gla_b_gate_b", "gla_norm", "attn_q_norm", "attn_k_norm",
              "ffn_b_conv")
WEIGHTS = ("norm_mix", "norm_ffn", "gla_w_in", "gla_w_gate_up_f", "gla_b_gate_f", "gla_w_gate_up_b", "gla_b_gate_b",
           "gla_norm", "gla_w_out", "attn_w_qkv", "attn_q_norm", "attn_k_norm", "attn_w_out", "ffn_w_up", "ffn_w_conv",
           "ffn_b_conv", "ffn_w_down")


def _rows(flat, row_align):
    n = flat.shape[0]
    per = row_align * LANES
    padded = -(-n // per) * per
    return jnp.pad(flat, (0, padded - n)).reshape(padded // LANES, LANES)


def _side_by_side(gathered, own, dev):
    n, a, b = gathered.shape
    whole = lax.dynamic_update_index_in_dim(gathered, own, dev, 0)
    return jnp.transpose(whole, (1, 0, 2)).reshape(a, n * b)


def _layer_shards(w, i):
    j = i // 2
    if i % 2 == 0:
        mixer = [("mix_in", w["gla_w_in"][j]), ("mix_out", w["gla_w_out"][j]),
                 ("gate_f", w["gla_w_gate_up_f"][j]), ("gate_b", w["gla_w_gate_up_b"][j])]
    else:
        mixer = [("mix_in", w["attn_w_qkv"][j]), ("mix_out", w["attn_w_out"][j])]
    ffn = [("up", w["ffn_w_up"][i]), ("down", w["ffn_w_down"][i])]
    names = [n for n, _ in mixer + ffn] + ["conv"]
    return names, [a.astype(BF16) for _, a in mixer + ffn] + [w["ffn_w_conv"][i]]


def _layer_weights(names, own, gathered, dev, dev_idx, i):
    own, got = dict(zip(names, own)), dict(zip(names, gathered))
    every = tuple(range(N_DEV))
    half = N_DEV // 2
    tag = "_l%d" % i
    width = GLA_PROJ if i % 2 == 0 else QKV_DIM
    (mix_in,) = _unshard_cols(got["mix_in"], own["mix_in"], dev_idx, [every], width, name="unshard_mix_in" + tag)
    f = got["up"].shape[-1] * half
    up_val, up_gate = _unshard_cols(got["up"], own["up"], dev_idx, [every[:half], every[half:]], f,
                                    name="unshard_ffn_up" + tag)
    out = dict(mix_in=mix_in, up_val=up_val, up_gate=up_gate,
               mix_out=_place_own(got["mix_out"], own["mix_out"], dev_idx, name="place_mix_out" + tag),
               down=_place_own(got["down"], own["down"], dev_idx, name="place_ffn_down" + tag),
               conv=_side_by_side(got["conv"], own["conv"], dev))
    if i % 2 == 0:
        out["gate"] = _gate_matrix(_side_by_side(got["gate_f"], own["gate_f"], dev),
                                   _side_by_side(got["gate_b"], own["gate_b"], dev))
    return out


def _rope_tables(S):
    rows = S // GRID_W
    pairs = ATTN_HD // 4
    row_idx = jnp.repeat(jnp.arange(rows, dtype=F32), GRID_W)
    col_idx = jnp.tile(jnp.arange(GRID_W, dtype=F32), rows)
    inv_freq = ROPE_THETA ** (-jnp.arange(pairs, dtype=F32) / pairs)
    ang = jnp.concatenate([row_idx[:, None] * inv_freq, col_idx[:, None] * inv_freq], axis=-1)
    cos, sin = jnp.cos(ang), jnp.sin(ang)
    return jnp.concatenate([cos, cos], axis=-1), jnp.concatenate([-sin, sin], axis=-1)


def _gate_matrix(w_f, w_b):
    rk = w_f.shape[0]
    top = jnp.concatenate([w_f, jnp.zeros_like(w_f)], axis=1)
    mid = jnp.concatenate([jnp.zeros_like(w_b), w_b], axis=1)
    pad = jnp.zeros((LANES - 2 * rk, 2 * GLA_KD), w_f.dtype)
    return jnp.concatenate([top, mid, pad], axis=0)


def _local_step(x, target, rep, w, dev, idx):
    S, D = x.shape
    depth = rep["norm_mix"].shape[0]
    cs, sn = _rope_tables(S)
    row = lambda a: a.reshape(1, -1)
    ranks_cols = (GLA_PROJ - LANES, LANES)

    sent = []
    for i in range(depth):
        names, srcs = _layer_shards(w, i)
        lands = [lax.empty((N_DEV,) + a.shape, a.dtype) for a in srcs]
        sent.append((names, _split_start(_gather_send_plan(len(srcs)), srcs, lands, name="weights_send_l%d" % i)))

    def arrive(i, after):
        names, started = sent[i]
        own, lands = _split_wait(started, after, name="weights_arrive_l%d" % i)
        return names, own, _split_start(_gather_pass_plan(len(lands)), [], lands, name="weights_pass_l%d" % i)

    def ready(i, passing, after):
        names, own, started = passing
        _, lands = _split_wait(started, after, name="weights_passed_l%d" % i)
        return _layer_weights(names, own, lands, dev, idx["dev"], i)

    passing = arrive(0, sent[-1][1]["token"])
    wl = ready(0, passing, passing[2]["token"])

    saved = []
    for i in range(depth):
        j = i // 2
        sv = {"x0": x, "w": wl}
        h1 = _rmsnorm_fwd(x, row(rep["norm_mix"][i]), name="norm_mix_fwd")
        sv["h1"] = h1
        if i % 2 == 0:
            bias = jnp.concatenate([rep["gla_b_gate_f"][j], rep["gla_b_gate_b"][j]]).reshape(1, -1)
            proj = _mm(h1, wl["mix_in"], name="gla_in_proj")
            logits = _mm(proj, wl["gate"], a_cols=ranks_cols, name="gla_gate_logits")
            of, ob, sf, sb = _gla_fwd(proj, logits, bias, name="gla_fwd")
            y = _gla_gate_fwd(of, ob, proj, row(rep["gla_norm"][j]), name="gla_gate_fwd")
            x = _mm(y, wl["mix_out"], res=x, name="gla_out_proj")
            sv.update(bias=bias, proj=proj, logits=logits, of=of, ob=ob, sf=sf, sb=sb, y=y)
        else:
            proj = _mm(h1, wl["mix_in"], name="attn_qkv_proj")
            qkv = _attn_prep(proj, cs, sn, row(rep["attn_q_norm"][j]), row(rep["attn_k_norm"][j]), name="attn_prep")
            o = _attn_fwd(qkv, name="attn_fwd")
            x = _mm(o, wl["mix_out"], res=x, name="attn_out_proj")
            sv.update(proj=proj, qkv=qkv, o=o)
        sv["x1"] = x
        dep = ()
        if i + 1 < depth:
            passing = arrive(i + 1, x)
            dep = (passing[2]["token"],)
        h2 = _rmsnorm_fwd(x, row(rep["norm_ffn"][i]), dep=dep, name="norm_ffn_fwd")
        F = wl["down"].shape[0]
        wc, bc = wl["conv"], rep["ffn_b_conv"][i]
        wcv, wcg, bcv, bcg = wc[:, :F], wc[:, F:], row(bc[:F]), row(bc[F:])
        uv = _mm(h2, wl["up_val"], name="ffn_up_val")
        ug = _mm(h2, wl["up_gate"], name="ffn_up_gate")
        act = _conv_act_fwd(uv, ug, wcv, wcg, bcv, bcg, name="ffn_conv_act")
        x = _mm(act, wl["down"], res=x, name="ffn_down")
        sv.update(h2=h2, uv=uv, ug=ug, act=act, wcv=wcv, wcg=wcg, bcv=bcv, bcg=bcg)
        saved.append(sv)
        if i + 1 < depth:
            wl = ready(i + 1, passing, x)

    dx, loss_tile = _loss_head(x, target, name="loss_head")
    loss = loss_tile[0, 0]

    in_sibling_stage, in_chip_stage, reduced = [], [], []

    def advance(group, after):
        tokens = []
        for tag, keys, started in in_chip_stage:
            partial, lands = _split_wait(started, after, name="grads_chips_arrive_" + tag)
            reduced.append((keys, partial, lands))
        in_chip_stage.clear()
        for tag, keys, started in in_sibling_stage:
            stacks, lands = _split_wait(started, after, name="grads_sibling_arrive_" + tag)
            partial = [_add_sibling(s, b, idx["core"], name="grads_add_sibling_%s_%d" % (tag, q))
                       for q, (s, b) in enumerate(zip(stacks, lands))]
            bufs = [lax.empty(p.shape, p.dtype) for p in partial]
            started = _split_start(_reduce_chip_plan(len(partial)), partial, bufs, name="grads_chips_send_" + tag)
            in_chip_stage.append((tag, keys, started))
            tokens.append(started["token"])
        in_sibling_stage.clear()
        if group is not None:
            tag, keys, stacks = group
            bufs = [lax.empty((4,) + s.shape[1:], s.dtype) for s in stacks]
            started = _split_start(_reduce_sibling_plan(len(stacks)), stacks, bufs, name="grads_sibling_send_" + tag)
            in_sibling_stage.append((tag, keys, started))
            tokens.append(started["token"])
        return tuple(tokens)

    def stack_for(name):
        return lax.empty((N_DEV,) + tuple(w[name].shape[1:]), BF16)

    gl = {k: [None] * depth for k in ("norm_mix", "norm_ffn", "ffn_w_conv", "ffn_b_conv")}
    gm = {k: [None] * (depth // 2) for k in ("gla_w_gate_up_f", "gla_b_gate_f", "gla_w_gate_up_b", "gla_b_gate_b",
                                             "gla_norm", "attn_q_norm", "attn_k_norm")}
    rk = GLA_GATE_RANK
    dep = ()
    for i in reversed(range(depth)):
        j = i // 2
        sv = saved[i]
        wl = sv["w"]
        dact = _mm(dx, wl["down"], tb=True, dep=dep, name="ffn_down_dgrad")
        g_down = _wgrad(sv["act"], dx, stack_for("ffn_w_down"), shard="rows", name="ffn_down_wgrad")
        dcv, dcg, dwv, dwg, dbv, dbg = _conv_act_bwd(sv["uv"], sv["ug"], sv["wcv"], sv["wcg"], sv["bcv"], sv["bcg"],
                                                     dact, name="ffn_conv_act_bwd")
        gl["ffn_w_conv"][i] = jnp.concatenate([dwv, dwg], axis=1)
        gl["ffn_b_conv"][i] = jnp.concatenate([dbv, dbg], axis=1)[0]
        duv = _conv_t(dcv, sv["wcv"], name="ffn_conv_t")
        dug = _conv_t(dcg, sv["wcg"], name="ffn_conv_t")
        dh2 = _mm(duv, wl["up_val"], tb=True, name="ffn_up_dgrad_val")
        dh2 = _mm(dug, wl["up_gate"], tb=True, res=dh2, name="ffn_up_dgrad_gate")
        g_up = _wgrad(sv["h2"], duv, stack_for("ffn_w_up"), shard="cols", group=0, name="ffn_up_wgrad_val")
        g_up = _wgrad(sv["h2"], dug, g_up, shard="cols", group=1, name="ffn_up_wgrad_gate")
        dx, dn = _rmsnorm_bwd(sv["x1"], row(rep["norm_ffn"][i]), dh2, dx, name="norm_ffn_bwd")
        gl["norm_ffn"][i] = dn[0]
        dep = advance(("ffn_l%d" % i, [("ffn_w_up", i), ("ffn_w_down", i)], [g_up, g_down]), dx)
        if i % 2 == 0:
            dy = _mm(dx, wl["mix_out"], tb=True, dep=dep, name="gla_out_dgrad")
            g_out = _wgrad(sv["y"], dx, stack_for("gla_w_out"), shard="rows", name="gla_out_wgrad")
            do, dg, dgn = _gla_gate_bwd(sv["of"], sv["ob"], sv["proj"], row(rep["gla_norm"][j]), dy, name="gla_gate_bwd")
            gm["gla_norm"][j] = dgn[0]
            dqkv_f, dlg_f, dqkv_b, dlg_b, dbias = _gla_bwd(sv["proj"], sv["logits"], sv["bias"], sv["sf"], sv["sb"], do,
                                                           name="gla_bwd")
            gm["gla_b_gate_f"][j] = dbias[0, :GLA_KD]
            gm["gla_b_gate_b"][j] = dbias[0, GLA_KD:]
            dlogits = jnp.concatenate([dlg_f, dlg_b], axis=1)
            dr = _mm(dlogits, wl["gate"], tb=True, name="gla_gate_dgrad")
            dwg_full = _mm(sv["proj"], dlogits, ta=True, a_cols=ranks_cols, name="gla_gate_wgrad")
            gm["gla_w_gate_up_f"][j] = dwg_full[:rk, :GLA_KD]
            gm["gla_w_gate_up_b"][j] = dwg_full[rk:2 * rk, GLA_KD:]
            dproj = _gla_combine(dqkv_f, dqkv_b, dg, dr, name="gla_combine")
            dh1 = _mm(dproj, wl["mix_in"], tb=True, name="gla_in_dgrad")
            g_in = _wgrad(sv["h1"], dproj, stack_for("gla_w_in"), shard="cols", name="gla_in_wgrad")
            keys = [("gla_w_in", j), ("gla_w_out", j)]
        else:
            do = _mm(dx, wl["mix_out"], tb=True, out_dtype=BF16, dep=dep, name="attn_out_dgrad")
            g_out = _wgrad(sv["o"], dx, stack_for("attn_w_out"), shard="rows", name="attn_out_wgrad")
            dq, dk, dv = _attn_bwd(sv["qkv"], do, name="attn_bwd")
            dproj, dqn, dkn = _attn_prep_bwd(sv["proj"], dq, dk, dv, cs, sn, row(rep["attn_q_norm"][j]),
                                             row(rep["attn_k_norm"][j]), name="attn_prep_bwd")
            gm["attn_q_norm"][j] = dqn[0]
            gm["attn_k_norm"][j] = dkn[0]
            dh1 = _mm(dproj, wl["mix_in"], tb=True, name="attn_qkv_dgrad")
            g_in = _wgrad(sv["h1"], dproj, stack_for("attn_w_qkv"), shard="cols", name="attn_qkv_wgrad")
            keys = [("attn_w_qkv", j), ("attn_w_out", j)]
        dx, dn = _rmsnorm_bwd(sv["x0"], row(rep["norm_mix"][i]), dh1, dx, name="norm_mix_bwd")
        gl["norm_mix"][i] = dn[0]
        dep = advance(("mix_l%d" % i, keys, [g_in, g_out]), dx)
    advance(None, dx)
    advance(None, dx)

    small = {k: jnp.stack(v) for k, v in {**gl, **gm}.items()}
    return loss, dx, reduced, small


def kernel(x, norm_mix, norm_ffn, gla_w_in, gla_w_gate_up_f, gla_b_gate_f, gla_w_gate_up_b, gla_b_gate_b, gla_norm, gla_w_out, attn_w_qkv, attn_q_norm, attn_k_norm, attn_w_out, ffn_w_up, ffn_w_conv, ffn_b_conv, ffn_w_down, loss_target, m_norm_mix, m_norm_ffn, m_gla_w_in, m_gla_w_gate_up_f, m_gla_b_gate_f, m_gla_w_gate_up_b, m_gla_b_gate_b, m_gla_norm, m_gla_w_out, m_attn_w_qkv, m_attn_q_norm, m_attn_k_norm, m_attn_w_out, m_ffn_w_up, m_ffn_w_conv, m_ffn_b_conv, m_ffn_w_down, v_norm_mix, v_norm_ffn, v_gla_w_in, v_gla_w_gate_up_f, v_gla_b_gate_f, v_gla_w_gate_up_b, v_gla_b_gate_b, v_gla_norm, v_gla_w_out, v_attn_w_qkv, v_attn_q_norm, v_attn_k_norm, v_attn_w_out, v_ffn_w_up, v_ffn_w_conv, v_ffn_b_conv, v_ffn_w_down):
    given = dict(locals())
    w = {n: given[n] for n in WEIGHTS}
    m = {n: given["m_" + n] for n in WEIGHTS}
    v = {n: given["v_" + n] for n in WEIGHTS}
    shards = {n: w[n] for n in BIG + SMALL_SHARDED}
    rep = {n: w[n] for n in REPLICATED}

    x_pos, y_pos, c_pos = _place()
    dev = 4 * x_pos + 2 * y_pos + c_pos
    as_operand = lambda s: jnp.asarray(s, jnp.int32).reshape(1)
    idx = dict(dev=as_operand(dev), chip=as_operand(2 * x_pos + y_pos), core=as_operand(c_pos))

    loss_local, grad_x, reduced, small = _local_step(x[0], loss_target[0], rep, shards, dev, idx)
    loss = lax.psum(loss_local, ("x", "y", "c"))

    big = {n: [lax.empty(w[n].shape, F32) for _ in range(4)] for n in BIG}
    for keys, own, parts in reduced:
        for (n, layer), p_own, p_others in zip(keys, own, parts):
            big[n] = _adamw_layer(w[n], m[n], v[n], p_own, p_others, idx["chip"], big[n], layer,
                                  name="adamw_%s_l%d" % (n, layer))

    rest = REPLICATED + SMALL_SHARDED
    flat = _rows(jnp.concatenate([small[n].reshape(-1) for n in rest]), SUBLANES)
    total = _sum_slots(_all_gather(flat, name="small_grads_all_gather"), name="small_grads_sum").reshape(-1)
    g, off = {}, 0
    for n in rest:
        whole = total[off:off + small[n].size].reshape(small[n].shape)
        off += small[n].size
        width = w[n].shape[-1]
        g[n] = whole if n in REPLICATED else lax.dynamic_slice_in_dim(whole, dev * width, width, axis=whole.ndim - 1)

    delta, new_m, new_v = {}, {}, {}
    for n in WEIGHTS:
        if n in BIG:
            g[n], delta[n], new_m[n], new_v[n] = big[n]
        else:
            shape = w[n].shape
            two_d = (-1, shape[-1])
            d2, m2, v2 = _adamw(w[n].reshape(two_d), g[n].reshape(two_d), m[n].reshape(two_d), v[n].reshape(two_d),
                                name="adamw_" + n)
            delta[n], new_m[n], new_v[n] = d2.reshape(shape), m2.reshape(shape), v2.reshape(shape)

    return (loss, grad_x[None], *[g[n] for n in WEIGHTS], *[delta[n] for n in WEIGHTS],
            *[new_m[n] for n in WEIGHTS], *[new_v[n] for n in WEIGHTS])
```

```python
import math

import jax
import jax.numpy as jnp
from jax import lax
from jax.experimental import pallas as pl
from jax.experimental.pallas import tpu as pltpu

F32 = jnp.float32
BF16 = jnp.bfloat16
MESH = pl.DeviceIdType.MESH

N_DEV = 8
LANES = 128
SUBLANES = 8
VMEM_LIMIT = 56 * 1024 * 1024

NORM_EPS = 1e-6
GRID_W = 64
ROPE_THETA = 10000.0
GLA_HEADS = 4
GLA_DK = 128
GLA_DV = 256
GLA_CHUNK = 64
GLA_GATE_RANK = 16
GLA_GATE_NORMALIZER = 16.0
ATTN_HD = 128
ATTN_Q_HEADS = 8
ATTN_KV_HEADS = 2
ATTN_GROUP = ATTN_Q_HEADS // ATTN_KV_HEADS

ADAM_LR = 0.001
ADAM_B1 = 0.9
ADAM_B2 = 0.999
ADAM_EPS = 1e-08
ADAM_WD = 0.01
ADAM_STEP = 10


def _tile(n, target, align=LANES):
    if n <= target:
        return n
    t = (target // align) * align
    while t >= align:
        if n % t == 0:
            return t
        t -= align
    return n


def _params(*sem):
    return pltpu.CompilerParams(dimension_semantics=sem, vmem_limit_bytes=VMEM_LIMIT)


def _dot(a, b):
    return lax.dot_general(a, b, (((1,), (0,)), ((), ())), preferred_element_type=F32)


def _dot_nt(a, b):
    return lax.dot_general(a, b, (((1,), (1,)), ((), ())), preferred_element_type=F32)


def _dot_tn(a, b):
    return lax.dot_general(a, b, (((0,), (0,)), ((), ())), preferred_element_type=F32)


def _sigmoid(x):
    return 1.0 / (1.0 + jnp.exp(-x))


def _log_sigmoid(x):
    return jnp.minimum(x, 0.0) - jnp.log(1.0 + jnp.exp(-jnp.abs(x)))


def _colsum(x):
    return jnp.sum(x, axis=0, keepdims=True)


ANY = pl.BlockSpec(memory_space=pl.ANY)


def _mm(a, b, *, ta=False, tb=False, res=None, out_dtype=F32, layer=None, a_cols=None, dep=(), name):
    if tb:
        N, K = b.shape[-2:]
    else:
        K, N = b.shape[-2:]
    a_rows, a_width = a.shape
    a_off = 0
    if a_cols is not None:
        a_off, a_width = a_cols
    if ta:
        M = a_width
        assert a_rows == K, (a.shape, b.shape, ta, tb)
    else:
        M = a_rows
        assert a_width == K, (a.shape, b.shape, ta, tb)
    tm = _tile(M, 1408) if ta else _tile(M, 1024, 16)
    tn = _tile(N, 1408)
    tk = _tile(K, 512, 16) if ta else _tile(K, 1408)
    nk = K // tk
    dims = (((0 if ta else 1,), (1 if tb else 0,)), ((), ()))

    n_in = 2 + (res is not None) + len(dep)

    def body(*refs):
        a_ref, b_ref = refs[:2]
        r_ref = refs[2] if res is not None else None
        o_ref = refs[n_in]
        scr = refs[n_in + 1:]
        part = lax.dot_general(a_ref[...].astype(BF16), b_ref[...].astype(BF16), dims, preferred_element_type=F32)

        def finish(acc):
            if r_ref is not None:
                acc = acc + r_ref[...]
            o_ref[...] = acc.astype(out_dtype)

        if nk == 1:
            finish(part)
        else:
            acc_ref = scr[0]
            k = pl.program_id(2)

            @pl.when(k == 0)
            def _():
                acc_ref[...] = part

            @pl.when(k > 0)
            def _():
                acc_ref[...] += part

            @pl.when(k == nk - 1)
            def _():
                finish(acc_ref[...])

    a_blk = a_off // (tm if ta else tk)
    assert a_off % (tm if ta else tk) == 0
    a_spec = (pl.BlockSpec((tk, tm), lambda i, j, k: (k, a_blk + i)) if ta
              else pl.BlockSpec((tm, tk), lambda i, j, k: (i, a_blk + k)))
    if layer is None:
        b_spec = pl.BlockSpec((tn, tk), lambda i, j, k: (j, k)) if tb else pl.BlockSpec((tk, tn), lambda i, j, k: (k, j))
    else:
        b_spec = (pl.BlockSpec((None, tn, tk), lambda i, j, k: (layer, j, k)) if tb
                  else pl.BlockSpec((None, tk, tn), lambda i, j, k: (layer, k, j)))
    o_spec = pl.BlockSpec((tm, tn), lambda i, j, k: (i, j))
    in_specs = [a_spec, b_spec] + ([o_spec] if res is not None else []) + [ANY] * len(dep)
    args = (a, b) + ((res,) if res is not None else ()) + tuple(dep)
    return pl.pallas_call(
        body, name=name, grid=(M // tm, N // tn, nk),
        in_specs=in_specs, out_specs=o_spec,
        out_shape=jax.ShapeDtypeStruct((M, N), out_dtype),
        scratch_shapes=[pltpu.VMEM((tm, tn), F32)] if nk > 1 else [],
        compiler_params=_params("parallel", "parallel", "arbitrary"),
    )(*args)


def _wgrad(a, b, stack, *, shard, group=0, name):
    S, N = b.shape
    M = a.shape[1]
    As, Bs = stack.shape[-2:]
    tk = _tile(S, 1024, 16)
    nk = S // tk
    if shard == "cols":
        n = N // Bs
        tm = _tile(M, 512)
        tn = N
        grid = (M // tm, 1, nk)
        o_spec = pl.BlockSpec((n, tm, Bs), lambda i, j, k: (group, i, 0))
    else:
        per = As * LANES // math.gcd(As, LANES)
        tm = M if M <= 1408 else _tile(M, 1408, per)
        n = tm // As
        tn = _tile(N, 1024)
        grid = (M // tm, N // tn, nk)
        o_spec = pl.BlockSpec((n, As, tn), lambda i, j, k: (i, 0, j))

    def body(a_ref, b_ref, _, o_ref, acc_ref):
        k = pl.program_id(2)
        part = _dot_tn(a_ref[...].astype(BF16), b_ref[...].astype(BF16))

        @pl.when(k == 0)
        def _():
            acc_ref[...] = part

        @pl.when(k > 0)
        def _():
            acc_ref[...] += part

        @pl.when(k == nk - 1)
        def _():
            for q in range(n):
                if shard == "cols":
                    o_ref[q] = acc_ref[:, q * Bs:(q + 1) * Bs].astype(stack.dtype)
                else:
                    o_ref[q] = acc_ref[q * As:(q + 1) * As, :].astype(stack.dtype)

    return pl.pallas_call(
        body, name=name, grid=grid,
        in_specs=[pl.BlockSpec((tk, tm), lambda i, j, k: (k, i)), pl.BlockSpec((tk, tn), lambda i, j, k: (k, j)),
                  pl.BlockSpec(memory_space=pl.ANY)],
        out_specs=o_spec,
        out_shape=jax.ShapeDtypeStruct(stack.shape, stack.dtype),
        input_output_aliases={2: 0},
        scratch_shapes=[pltpu.VMEM((tm, tn), F32)],
        compiler_params=_params("parallel", "parallel", "arbitrary"),
    )(a, b, stack)


def _rmsnorm_fwd(x, w, *, dep=(), name):
    S, D = x.shape
    ts = _tile(S, 512, 16)

    def body(x_ref, w_ref, *rest):
        o_ref = rest[-1]
        xv = x_ref[...]
        r = lax.rsqrt(jnp.mean(xv * xv, axis=-1, keepdims=True) + NORM_EPS)
        o_ref[...] = (xv * r * w_ref[...]).astype(BF16)

    return pl.pallas_call(
        body, name=name, grid=(S // ts,),
        in_specs=[pl.BlockSpec((ts, D), lambda i: (i, 0)), pl.BlockSpec((1, D), lambda i: (0, 0))] + [ANY] * len(dep),
        out_specs=pl.BlockSpec((ts, D), lambda i: (i, 0)),
        out_shape=jax.ShapeDtypeStruct((S, D), BF16),
        compiler_params=_params("parallel"),
    )(x, w, *dep)


def _rmsnorm_bwd(x, w, dh, dres, *, name):
    S, D = x.shape
    ts = _tile(S, 512, 16)
    n = S // ts

    def body(x_ref, w_ref, dh_ref, dr_ref, dx_ref, dw_ref):
        i = pl.program_id(0)
        xv = x_ref[...]
        r = lax.rsqrt(jnp.mean(xv * xv, axis=-1, keepdims=True) + NORM_EPS)
        xh = xv * r
        d = dh_ref[...]
        g = d * w_ref[...]
        dx_ref[...] = dr_ref[...] + r * (g - xh * jnp.mean(g * xh, axis=-1, keepdims=True))
        part = _colsum(d * xh)

        @pl.when(i == 0)
        def _():
            dw_ref[...] = part

        @pl.when(i > 0)
        def _():
            dw_ref[...] += part

    row = pl.BlockSpec((ts, D), lambda i: (i, 0))
    vec = pl.BlockSpec((1, D), lambda i: (0, 0))
    return pl.pallas_call(
        body, name=name, grid=(n,),
        in_specs=[row, vec, row, row], out_specs=[row, vec],
        out_shape=[jax.ShapeDtypeStruct((S, D), F32), jax.ShapeDtypeStruct((1, D), F32)],
        compiler_params=_params("arbitrary"),
    )(x, w, dh, dres)


def _halo_specs(S, ts, tf, row_axis):
    g = ts // SUBLANES
    last = S // SUBLANES - 1
    col_axis = 1 - row_axis
    main = pl.BlockSpec((ts, tf), lambda *ij: (ij[row_axis], ij[col_axis]))
    prev = pl.BlockSpec((SUBLANES, tf), lambda *ij: (jnp.maximum(ij[row_axis] * g - 1, 0), ij[col_axis]))
    nxt = pl.BlockSpec((SUBLANES, tf), lambda *ij: (jnp.minimum((ij[row_axis] + 1) * g, last), ij[col_axis]))
    return [main, prev, nxt]


def _shifted(u, prev_ref, next_ref, i, n):
    ts = u.shape[0]
    rid = lax.broadcasted_iota(jnp.int32, u.shape, 0)
    before = jnp.where(i > 0, prev_ref[SUBLANES - 1:SUBLANES, :], 0.0)
    after = jnp.where(i < n - 1, next_ref[0:1, :], 0.0)
    um1 = jnp.where(rid == 0, before, pltpu.roll(u, 1, 0))
    up1 = jnp.where(rid == ts - 1, after, pltpu.roll(u, ts - 1, 0))
    return um1, up1


def _conv3(u, prev_ref, next_ref, w_ref, i, n):
    um1, up1 = _shifted(u, prev_ref, next_ref, i, n)
    return w_ref[0:1, :] * um1 + w_ref[1:2, :] * u + w_ref[2:3, :] * up1


def _conv_act_fwd(uv, ug, wv, wg, bv, bg, *, name):
    S, F = uv.shape
    ts = _tile(S, 512, 16)
    tf = _tile(F, 1408)
    n = S // ts

    def body(v_ref, vp_ref, vn_ref, g_ref, gp_ref, gn_ref, wv_ref, wg_ref, bv_ref, bg_ref, o_ref):
        i = pl.program_id(0)
        val = _conv3(v_ref[...], vp_ref, vn_ref, wv_ref, i, n) + bv_ref[...]
        gate = _conv3(g_ref[...], gp_ref, gn_ref, wg_ref, i, n) + bg_ref[...]
        o_ref[...] = (gate * _sigmoid(gate) * val).astype(BF16)

    halo = _halo_specs(S, ts, tf, 0)
    w3 = pl.BlockSpec((3, tf), lambda i, j: (0, j))
    b1 = pl.BlockSpec((1, tf), lambda i, j: (0, j))
    return pl.pallas_call(
        body, name=name, grid=(n, F // tf),
        in_specs=halo + halo + [w3, w3, b1, b1],
        out_specs=pl.BlockSpec((ts, tf), lambda i, j: (i, j)),
        out_shape=jax.ShapeDtypeStruct((S, F), BF16),
        compiler_params=_params("parallel", "parallel"),
    )(uv, uv, uv, ug, ug, ug, wv, wg, bv, bg)


def _conv_act_bwd(uv, ug, wv, wg, bv, bg, dact, *, name):
    S, F = uv.shape
    ts = _tile(S, 512, 16)
    tf = _tile(F, 1408)
    n = S // ts

    def body(v_ref, vp_ref, vn_ref, g_ref, gp_ref, gn_ref, wv_ref, wg_ref, bv_ref, bg_ref, da_ref,
             dv_ref, dg_ref, dwv_ref, dwg_ref, dbv_ref, dbg_ref):
        i = pl.program_id(1)
        uvv, ugv = v_ref[...], g_ref[...]
        vm1, vp1 = _shifted(uvv, vp_ref, vn_ref, i, n)
        gm1, gp1 = _shifted(ugv, gp_ref, gn_ref, i, n)
        val = wv_ref[0:1, :] * vm1 + wv_ref[1:2, :] * uvv + wv_ref[2:3, :] * vp1 + bv_ref[...]
        gate = wg_ref[0:1, :] * gm1 + wg_ref[1:2, :] * ugv + wg_ref[2:3, :] * gp1 + bg_ref[...]
        sg = _sigmoid(gate)
        da = da_ref[...]
        dval = da * (gate * sg)
        dgate = da * val * (sg * (1.0 + gate * (1.0 - sg)))
        dv_ref[...] = dval
        dg_ref[...] = dgate
        sums = [(dwv_ref, 0, vm1 * dval), (dwv_ref, 1, uvv * dval), (dwv_ref, 2, vp1 * dval),
                (dwg_ref, 0, gm1 * dgate), (dwg_ref, 1, ugv * dgate), (dwg_ref, 2, gp1 * dgate),
                (dbv_ref, 0, dval), (dbg_ref, 0, dgate)]
        parts = [(ref, r, _colsum(t)) for ref, r, t in sums]

        @pl.when(i == 0)
        def _():
            for ref, r, part in parts:
                ref[r:r + 1, :] = part

        @pl.when(i > 0)
        def _():
            for ref, r, part in parts:
                ref[r:r + 1, :] += part

    halo = _halo_specs(S, ts, tf, 1)
    w3 = pl.BlockSpec((3, tf), lambda j, i: (0, j))
    b1 = pl.BlockSpec((1, tf), lambda j, i: (0, j))
    blk = pl.BlockSpec((ts, tf), lambda j, i: (i, j))
    return pl.pallas_call(
        body, name=name, grid=(F // tf, n),
        in_specs=halo + halo + [w3, w3, b1, b1, blk],
        out_specs=[blk, blk, w3, w3, b1, b1],
        out_shape=[jax.ShapeDtypeStruct((S, F), F32), jax.ShapeDtypeStruct((S, F), F32),
                   jax.ShapeDtypeStruct((3, F), F32), jax.ShapeDtypeStruct((3, F), F32),
                   jax.ShapeDtypeStruct((1, F), F32), jax.ShapeDtypeStruct((1, F), F32)],
        compiler_params=_params("parallel", "arbitrary"),
    )(uv, uv, uv, ug, ug, ug, wv, wg, bv, bg, dact)


def _conv_t(duc, w, *, name):
    S, F = duc.shape
    ts = _tile(S, 512, 16)
    tf = _tile(F, 1408)
    n = S // ts

    def body(d_ref, dp_ref, dn_ref, w_ref, o_ref):
        i = pl.program_id(0)
        d = d_ref[...]
        dm1, dp1 = _shifted(d, dp_ref, dn_ref, i, n)
        o_ref[...] = (w_ref[0:1, :] * dp1 + w_ref[1:2, :] * d + w_ref[2:3, :] * dm1).astype(BF16)

    return pl.pallas_call(
        body, name=name, grid=(n, F // tf),
        in_specs=_halo_specs(S, ts, tf, 0) + [pl.BlockSpec((3, tf), lambda i, j: (0, j))],
        out_specs=pl.BlockSpec((ts, tf), lambda i, j: (i, j)),
        out_shape=jax.ShapeDtypeStruct((S, F), BF16),
        compiler_params=_params("parallel", "parallel"),
    )(duc, duc, duc, w)


N_QK = ATTN_Q_HEADS + ATTN_KV_HEADS
QKV_DIM = (ATTN_Q_HEADS + 2 * ATTN_KV_HEADS) * ATTN_HD


def _head(hd):
    return slice(hd * ATTN_HD, (hd + 1) * ATTN_HD)


def _attn_prep(proj, cs, sn, qn, kn, *, name):
    S = proj.shape[0]
    ts = _tile(S, 256, 16)

    def body(p_ref, c_ref, s_ref, qn_ref, kn_ref, o_ref):
        c, s = c_ref[...], s_ref[...]
        for hd in range(N_QK):
            xv = p_ref[:, _head(hd)]
            w = qn_ref[...] if hd < ATTN_Q_HEADS else kn_ref[...]
            r = lax.rsqrt(jnp.mean(xv * xv, axis=-1, keepdims=True) + NORM_EPS)
            nrm = xv * r * w
            o_ref[:, _head(hd)] = (nrm * c + pltpu.roll(nrm, ATTN_HD // 2, 1) * s).astype(BF16)
        o_ref[:, N_QK * ATTN_HD:] = p_ref[:, N_QK * ATTN_HD:].astype(BF16)

    row = pl.BlockSpec((ts, QKV_DIM), lambda i: (i, 0))
    rot = pl.BlockSpec((ts, ATTN_HD), lambda i: (i, 0))
    vec = pl.BlockSpec((1, ATTN_HD), lambda i: (0, 0))
    return pl.pallas_call(
        body, name=name, grid=(S // ts,),
        in_specs=[row, rot, rot, vec, vec], out_specs=row,
        out_shape=jax.ShapeDtypeStruct((S, QKV_DIM), BF16),
        compiler_params=_params("parallel"),
    )(proj, cs, sn, qn, kn)


def _attn_prep_bwd(proj, dq, dk, dv, cs, sn, qn, kn, *, name):
    S = proj.shape[0]
    ts = _tile(S, 256, 16)
    nq = ATTN_Q_HEADS * ATTN_HD
    nkv = ATTN_KV_HEADS * ATTN_HD

    def body(p_ref, dq_ref, dk_ref, dv_ref, c_ref, s_ref, qn_ref, kn_ref, o_ref, dqn_ref, dkn_ref):
        i = pl.program_id(0)
        c, s = c_ref[...], s_ref[...]
        acc = [jnp.zeros((1, ATTN_HD), F32), jnp.zeros((1, ATTN_HD), F32)]
        for hd in range(N_QK):
            is_k = hd >= ATTN_Q_HEADS
            xv = p_ref[:, _head(hd)]
            w = kn_ref[...] if is_k else qn_ref[...]
            r = lax.rsqrt(jnp.mean(xv * xv, axis=-1, keepdims=True) + NORM_EPS)
            xh = xv * r
            dout = dk_ref[:, _head(hd - ATTN_Q_HEADS)] if is_k else dq_ref[:, _head(hd)]
            dn = dout * c + pltpu.roll(dout * s, ATTN_HD // 2, 1)
            acc[int(is_k)] = acc[int(is_k)] + _colsum(dn * xh)
            g = dn * w
            o_ref[:, _head(hd)] = (r * (g - xh * jnp.mean(g * xh, axis=-1, keepdims=True))).astype(BF16)
        o_ref[:, N_QK * ATTN_HD:] = dv_ref[...].astype(BF16)

        @pl.when(i == 0)
        def _():
            dqn_ref[...] = acc[0]
            dkn_ref[...] = acc[1]

        @pl.when(i > 0)
        def _():
            dqn_ref[...] += acc[0]
            dkn_ref[...] += acc[1]

    row = pl.BlockSpec((ts, QKV_DIM), lambda i: (i, 0))
    rot = pl.BlockSpec((ts, ATTN_HD), lambda i: (i, 0))
    vec = pl.BlockSpec((1, ATTN_HD), lambda i: (0, 0))
    return pl.pallas_call(
        body, name=name, grid=(S // ts,),
        in_specs=[row, pl.BlockSpec((ts, nq), lambda i: (i, 0)), pl.BlockSpec((ts, nkv), lambda i: (i, 0)),
                  pl.BlockSpec((ts, nkv), lambda i: (i, 0)), rot, rot, vec, vec],
        out_specs=[row, vec, vec],
        out_shape=[jax.ShapeDtypeStruct((S, QKV_DIM), BF16), jax.ShapeDtypeStruct((1, ATTN_HD), F32),
                   jax.ShapeDtypeStruct((1, ATTN_HD), F32)],
        compiler_params=_params("arbitrary"),
    )(proj, dq, dk, dv, cs, sn, qn, kn)


ATTN_SCALE = ATTN_HD ** -0.5


def _softmax_parts(q, k):
    s = _dot_nt(q, k)
    e = jnp.exp2((s - jnp.max(s, axis=-1, keepdims=True)) * (ATTN_SCALE * math.log2(math.e)))
    return e, 1.0 / jnp.sum(e, axis=-1, keepdims=True)


def _attn_fwd(qkv, *, name):
    S = qkv.shape[0]
    tq = _tile(S, 256, 16)

    def body(q_ref, k_ref, v_ref, o_ref):
        e, rl = _softmax_parts(q_ref[...], k_ref[...])
        o_ref[...] = (_dot(e.astype(BF16), v_ref[...]) * rl).astype(BF16)

    return pl.pallas_call(
        body, name=name, grid=(ATTN_Q_HEADS, S // tq),
        in_specs=[pl.BlockSpec((tq, ATTN_HD), lambda h, i: (i, h)),
                  pl.BlockSpec((S, ATTN_HD), lambda h, i: (0, ATTN_Q_HEADS + h // ATTN_GROUP)),
                  pl.BlockSpec((S, ATTN_HD), lambda h, i: (0, N_QK + h // ATTN_GROUP))],
        out_specs=pl.BlockSpec((tq, ATTN_HD), lambda h, i: (i, h)),
        out_shape=jax.ShapeDtypeStruct((S, ATTN_Q_HEADS * ATTN_HD), BF16),
        compiler_params=_params("parallel", "parallel"),
    )(qkv, qkv, qkv)


def _attn_bwd(qkv, do, *, name):
    S = qkv.shape[0]
    tq = _tile(S, 256, 16)

    def body(q_ref, k_ref, v_ref, do_ref, dq_ref, dk_ref, dv_ref):
        first = jnp.logical_and(pl.program_id(1) == 0, pl.program_id(2) == 0)
        q, k, dov = q_ref[...], k_ref[...], do_ref[...]
        e, rl = _softmax_parts(q, k)
        dp = _dot_nt(dov, v_ref[...])
        delta = jnp.sum(e * dp, axis=-1, keepdims=True) * rl
        dsb = (e * (dp - delta) * (rl * ATTN_SCALE)).astype(BF16)
        dq_ref[...] = _dot(dsb, k)
        dk = _dot_tn(dsb, q)
        dv = _dot_tn(e.astype(BF16), (dov.astype(F32) * rl).astype(BF16))

        @pl.when(first)
        def _():
            dk_ref[...] = dk
            dv_ref[...] = dv

        @pl.when(jnp.logical_not(first))
        def _():
            dk_ref[...] += dk
            dv_ref[...] += dv

    qblk = pl.BlockSpec((tq, ATTN_HD), lambda kv, g, i: (i, kv * ATTN_GROUP + g))
    kvacc = pl.BlockSpec((S, ATTN_HD), lambda kv, g, i: (0, kv))
    return pl.pallas_call(
        body, name=name, grid=(ATTN_KV_HEADS, ATTN_GROUP, S // tq),
        in_specs=[qblk,
                  pl.BlockSpec((S, ATTN_HD), lambda kv, g, i: (0, ATTN_Q_HEADS + kv)),
                  pl.BlockSpec((S, ATTN_HD), lambda kv, g, i: (0, N_QK + kv)),
                  qblk],
        out_specs=[qblk, kvacc, kvacc],
        out_shape=[jax.ShapeDtypeStruct((S, ATTN_Q_HEADS * ATTN_HD), F32),
                   jax.ShapeDtypeStruct((S, ATTN_KV_HEADS * ATTN_HD), F32),
                   jax.ShapeDtypeStruct((S, ATTN_KV_HEADS * ATTN_HD), F32)],
        compiler_params=_params("parallel", "arbitrary", "arbitrary"),
    )(qkv, qkv, qkv, do)


GLA_KD = GLA_HEADS * GLA_DK
GLA_VD = GLA_HEADS * GLA_DV
GLA_PROJ = 2 * GLA_KD + 2 * GLA_VD + LANES
GLA_SCALE = GLA_DK ** -0.5


def _split3(x):
    hi = x.astype(BF16)
    r1 = x - hi.astype(F32)
    mid = r1.astype(BF16)
    lo = (r1 - mid.astype(F32)).astype(BF16)
    return hi, mid, lo


def _cumdot(t, x):
    hi, mid, lo = _split3(x)
    return _dot(t, hi) + _dot(t, mid) + _dot(t, lo)


def _gla_masks(d):
    c = GLA_CHUNK
    row = lax.broadcasted_iota(jnp.int32, (c, c), 0)
    col = lax.broadcasted_iota(jnp.int32, (c, c), 1)
    lower, upper = col <= row, col >= row
    if d == 0:
        return lower.astype(BF16), upper.astype(BF16), lower
    return upper.astype(BF16), lower.astype(BF16), col > row


def _gla_decay(lg, bias, cum, d):
    xl = lg + bias
    la = _log_sigmoid(xl) * (1.0 / GLA_GATE_NORMALIZER)
    b = _cumdot(cum, la)
    b_end = b[GLA_CHUNK - 1:GLA_CHUNK, :] if d == 0 else b[0:1, :]
    return xl, b, b_end


def _gla_specs(S, n):
    c = GLA_CHUNK
    up = lambda i: i
    down = lambda i: n - 1 - i
    def specs(order):
        return dict(
            q=pl.BlockSpec((c, GLA_KD), lambda i: (order(i), 0)),
            k=pl.BlockSpec((c, GLA_KD), lambda i: (order(i), 1)),
            v=pl.BlockSpec((c, GLA_VD), lambda i: (order(i), 1)),
            st=pl.BlockSpec((1, GLA_HEADS, GLA_DV, GLA_DK), lambda i: (order(i), 0, 0, 0)),
            wide=pl.BlockSpec((c, GLA_VD), lambda i: (order(i), 0)),
            qkv=pl.BlockSpec((c, 2 * GLA_KD + GLA_VD), lambda i: (order(i), 0)),
        )
    return specs(up), specs(down), up, down


def _gla_fwd(proj, logits, bias, *, name):
    S = proj.shape[0]
    c = GLA_CHUNK
    n = S // c
    su, sd, up, down = _gla_specs(S, n)

    def body(qf, kf, vf, lf, qb, kb, vb, lb, bias_ref, of, ob, sf, sb, st):
        @pl.when(pl.program_id(0) == 0)
        def _():
            st[...] = jnp.zeros_like(st)

        for d, (q_r, k_r, v_r, l_r, o_r, s_r) in enumerate(((qf, kf, vf, lf, of, sf), (qb, kb, vb, lb, ob, sb))):
            cum, _, mask = _gla_masks(d)
            _, b, b_end = _gla_decay(l_r[...], bias_ref[:, d * GLA_KD:(d + 1) * GLA_KD], cum, d)
            dend = jnp.exp(b_end)
            k = k_r[...]
            qd = (q_r[...] * GLA_SCALE * jnp.exp(b)).astype(BF16)
            ki = (k * jnp.exp(-b)).astype(BF16)
            ke = (k * jnp.exp(b_end - b)).astype(BF16)
            for h in range(GLA_HEADS):
                ks = slice(h * GLA_DK, (h + 1) * GLA_DK)
                vs = slice(h * GLA_DV, (h + 1) * GLA_DV)
                stp = st[d * GLA_HEADS + h]
                s_r[0, h] = stp
                v = v_r[:, vs].astype(BF16)
                att = jnp.where(mask, _dot_nt(qd[:, ks], ki[:, ks]), 0.0).astype(BF16)
                o_r[:, vs] = _dot(att, v) + _dot_nt(qd[:, ks], stp.astype(BF16))
                st[d * GLA_HEADS + h] = stp * dend[:, ks] + _dot_tn(v, ke[:, ks])

    lg_f = pl.BlockSpec((c, GLA_KD), lambda i: (up(i), 0))
    lg_b = pl.BlockSpec((c, GLA_KD), lambda i: (down(i), 1))
    return pl.pallas_call(
        body, name=name, grid=(n,),
        in_specs=[su["q"], su["k"], su["v"], lg_f, sd["q"], sd["k"], sd["v"], lg_b,
                  pl.BlockSpec((1, 2 * GLA_KD), lambda i: (0, 0))],
        out_specs=[su["wide"], sd["wide"], su["st"], sd["st"]],
        out_shape=[jax.ShapeDtypeStruct((S, GLA_VD), F32), jax.ShapeDtypeStruct((S, GLA_VD), F32),
                   jax.ShapeDtypeStruct((n, GLA_HEADS, GLA_DV, GLA_DK), F32),
                   jax.ShapeDtypeStruct((n, GLA_HEADS, GLA_DV, GLA_DK), F32)],
        scratch_shapes=[pltpu.VMEM((2 * GLA_HEADS, GLA_DV, GLA_DK), F32)],
        compiler_params=_params("arbitrary"),
    )(proj, proj, proj, logits, proj, proj, proj, logits, bias)


def _gla_bwd(proj, logits, bias, sf, sb, do, *, name):
    S = proj.shape[0]
    c = GLA_CHUNK
    n = S // c
    su, sd, up, down = _gla_specs(S, n)

    def body(qf, kf, vf, lf, stf, dof, qb, kb, vb, lb, stb, dob, bias_ref,
             dqkv_f, dlg_f, dqkv_b, dlg_b, dbias, dst):
        first = pl.program_id(0) == 0

        @pl.when(first)
        def _():
            dst[...] = jnp.zeros_like(dst)

        dbias_parts = []
        for d, (q_r, k_r, v_r, l_r, s_r, do_r, dqkv_r, dlg_r) in enumerate(
                ((qf, kf, vf, lf, stf, dof, dqkv_f, dlg_f), (qb, kb, vb, lb, stb, dob, dqkv_b, dlg_b))):
            cum, cum_t, mask = _gla_masks(d)
            xl, b, b_end = _gla_decay(l_r[...], bias_ref[:, d * GLA_KD:(d + 1) * GLA_KD], cum, d)
            e, ei, ee, dend = jnp.exp(b), jnp.exp(-b), jnp.exp(b_end - b), jnp.exp(b_end)
            k = k_r[...]
            qd32 = q_r[...] * GLA_SCALE * e
            ki32 = k * ei
            ke32 = k * ee
            qd, ki, ke = qd32.astype(BF16), ki32.astype(BF16), ke32.astype(BF16)
            db_parts, dbe_parts = [], []
            for h in range(GLA_HEADS):
                ks = slice(h * GLA_DK, (h + 1) * GLA_DK)
                vs = slice(h * GLA_DV, (h + 1) * GLA_DV)
                stp = s_r[0, h]
                dstn = dst[d * GLA_HEADS + h]
                dstn_b = dstn.astype(BF16)
                v = v_r[:, vs].astype(BF16)
                dov = do_r[:, vs]
                att = jnp.where(mask, _dot_nt(qd[:, ks], ki[:, ks]), 0.0).astype(BF16)
                datt = jnp.where(mask, _dot_nt(dov, v), 0.0).astype(BF16)
                dqkv_r[:, 2 * GLA_KD + h * GLA_DV:2 * GLA_KD + (h + 1) * GLA_DV] = (
                    _dot_tn(att, dov) + _dot_nt(ke[:, ks], dstn_b))
                dqd = _dot(datt, ki[:, ks]) + _dot(dov, stp.astype(BF16))
                dki = _dot_tn(datt, qd[:, ks])
                dke = _dot(v, dstn_b)
                d_dend = _colsum(stp * dstn)
                dst[d * GLA_HEADS + h] = _dot_tn(dov, qd[:, ks]) + dstn * dend[:, ks]
                dqkv_r[:, ks] = dqd * e[:, ks] * GLA_SCALE
                dqkv_r[:, GLA_KD + h * GLA_DK:GLA_KD + (h + 1) * GLA_DK] = dki * ei[:, ks] + dke * ee[:, ks]
                dke_ke = dke * ke32[:, ks]
                db_parts.append(dqd * qd32[:, ks] - dki * ki32[:, ks] - dke_ke)
                dbe_parts.append(_colsum(dke_ke) + d_dend * dend[:, ks])
            db = jnp.concatenate(db_parts, axis=1)
            db_end = jnp.concatenate(dbe_parts, axis=1)
            dla = _cumdot(cum_t, db) + db_end
            dlg = dla * (1.0 / GLA_GATE_NORMALIZER) * _sigmoid(-xl)
            dlg_r[...] = dlg
            dbias_parts.append(_colsum(dlg))
        dbv = jnp.concatenate(dbias_parts, axis=1)

        @pl.when(first)
        def _():
            dbias[...] = dbv

        @pl.when(jnp.logical_not(first))
        def _():
            dbias[...] += dbv

    lg_f = pl.BlockSpec((c, GLA_KD), lambda i: (down(i), 0))
    lg_b = pl.BlockSpec((c, GLA_KD), lambda i: (up(i), 1))
    dlg_f = pl.BlockSpec((c, GLA_KD), lambda i: (down(i), 0))
    dlg_b = pl.BlockSpec((c, GLA_KD), lambda i: (up(i), 0))
    return pl.pallas_call(
        body, name=name, grid=(n,),
        in_specs=[sd["q"], sd["k"], sd["v"], lg_f, sd["st"], sd["wide"],
                  su["q"], su["k"], su["v"], lg_b, su["st"], su["wide"],
                  pl.BlockSpec((1, 2 * GLA_KD), lambda i: (0, 0))],
        out_specs=[sd["qkv"], dlg_f, su["qkv"], dlg_b, pl.BlockSpec((1, 2 * GLA_KD), lambda i: (0, 0))],
        out_shape=[jax.ShapeDtypeStruct((S, 2 * GLA_KD + GLA_VD), F32), jax.ShapeDtypeStruct((S, GLA_KD), F32),
                   jax.ShapeDtypeStruct((S, 2 * GLA_KD + GLA_VD), F32), jax.ShapeDtypeStruct((S, GLA_KD), F32),
                   jax.ShapeDtypeStruct((1, 2 * GLA_KD), F32)],
        scratch_shapes=[pltpu.VMEM((2 * GLA_HEADS, GLA_DV, GLA_DK), F32)],
        compiler_params=_params("arbitrary"),
    )(proj, proj, proj, logits, sf, do, proj, proj, proj, logits, sb, do, bias)


def _gla_gate_fwd(of, ob, proj, w, *, name):
    S = of.shape[0]
    ts = _tile(S, 256, 16)

    def body(of_ref, ob_ref, g_ref, w_ref, y_ref):
        for h in range(GLA_HEADS):
            vs = slice(h * GLA_DV, (h + 1) * GLA_DV)
            o = of_ref[:, vs] + ob_ref[:, vs]
            r = lax.rsqrt(jnp.mean(o * o, axis=-1, keepdims=True) + NORM_EPS)
            g = g_ref[:, vs]
            y_ref[:, vs] = (o * r * w_ref[...] * (g * _sigmoid(g))).astype(BF16)

    wide = pl.BlockSpec((ts, GLA_VD), lambda i: (i, 0))
    return pl.pallas_call(
        body, name=name, grid=(S // ts,),
        in_specs=[wide, wide, pl.BlockSpec((ts, GLA_VD), lambda i: (i, 2)), pl.BlockSpec((1, GLA_DV), lambda i: (0, 0))],
        out_specs=wide,
        out_shape=jax.ShapeDtypeStruct((S, GLA_VD), BF16),
        compiler_params=_params("parallel"),
    )(of, ob, proj, w)


def _gla_gate_bwd(of, ob, proj, w, dy, *, name):
    S = of.shape[0]
    ts = _tile(S, 256, 16)

    def body(of_ref, ob_ref, g_ref, w_ref, dy_ref, do_ref, dg_ref, dw_ref):
        i = pl.program_id(0)
        acc = jnp.zeros((1, GLA_DV), F32)
        for h in range(GLA_HEADS):
            vs = slice(h * GLA_DV, (h + 1) * GLA_DV)
            o = of_ref[:, vs] + ob_ref[:, vs]
            r = lax.rsqrt(jnp.mean(o * o, axis=-1, keepdims=True) + NORM_EPS)
            oh = o * r
            g = g_ref[:, vs]
            sg = _sigmoid(g)
            dyv = dy_ref[:, vs]
            dn = dyv * (g * sg)
            dg_ref[:, vs] = dyv * (oh * w_ref[...]) * (sg * (1.0 + g * (1.0 - sg)))
            acc = acc + _colsum(dn * oh)
            gg = dn * w_ref[...]
            do_ref[:, vs] = (r * (gg - oh * jnp.mean(gg * oh, axis=-1, keepdims=True))).astype(BF16)

        @pl.when(i == 0)
        def _():
            dw_ref[...] = acc

        @pl.when(i > 0)
        def _():
            dw_ref[...] += acc

    wide = pl.BlockSpec((ts, GLA_VD), lambda i: (i, 0))
    vec = pl.BlockSpec((1, GLA_DV), lambda i: (0, 0))
    return pl.pallas_call(
        body, name=name, grid=(S // ts,),
        in_specs=[wide, wide, pl.BlockSpec((ts, GLA_VD), lambda i: (i, 2)), vec, wide],
        out_specs=[wide, wide, vec],
        out_shape=[jax.ShapeDtypeStruct((S, GLA_VD), BF16), jax.ShapeDtypeStruct((S, GLA_VD), F32),
                   jax.ShapeDtypeStruct((1, GLA_DV), F32)],
        compiler_params=_params("arbitrary"),
    )(of, ob, proj, w, dy)


def _gla_combine(dqkv_f, dqkv_b, dg, dr, *, name):
    S = dg.shape[0]
    ts = _tile(S, 512, 16)
    nqkv = 2 * GLA_KD + GLA_VD

    def body(f_ref, b_ref, g_ref, r_ref, o_ref):
        o_ref[:, :nqkv] = (f_ref[...] + b_ref[...]).astype(BF16)
        o_ref[:, nqkv:nqkv + GLA_VD] = g_ref[...].astype(BF16)
        o_ref[:, nqkv + GLA_VD:] = r_ref[...].astype(BF16)

    return pl.pallas_call(
        body, name=name, grid=(S // ts,),
        in_specs=[pl.BlockSpec((ts, nqkv), lambda i: (i, 0)), pl.BlockSpec((ts, nqkv), lambda i: (i, 0)),
                  pl.BlockSpec((ts, GLA_VD), lambda i: (i, 0)), pl.BlockSpec((ts, LANES), lambda i: (i, 0))],
        out_specs=pl.BlockSpec((ts, GLA_PROJ), lambda i: (i, 0)),
        out_shape=jax.ShapeDtypeStruct((S, GLA_PROJ), BF16),
        compiler_params=_params("parallel"),
    )(dqkv_f, dqkv_b, dg, dr)


def _loss_head(y, t, *, name):
    S, D = y.shape
    ts = _tile(S, 512, 16)
    n = S // ts

    def body(y_ref, t_ref, dy_ref, l_ref, acc):
        i = pl.program_id(0)
        diff = y_ref[...] - t_ref[...]
        dy_ref[...] = diff * (1.0 / D)
        part = _colsum(diff * diff)

        @pl.when(i == 0)
        def _():
            acc[...] = part

        @pl.when(i > 0)
        def _():
            acc[...] += part

        @pl.when(i == n - 1)
        def _():
            l_ref[...] = jnp.full(l_ref.shape, 0.5 / D, F32) * jnp.sum(acc[...])

    row = pl.BlockSpec((ts, D), lambda i: (i, 0))
    return pl.pallas_call(
        body, name=name, grid=(n,),
        in_specs=[row, row],
        out_specs=[row, pl.BlockSpec((SUBLANES, LANES), lambda i: (0, 0))],
        out_shape=[jax.ShapeDtypeStruct((S, D), F32), jax.ShapeDtypeStruct((SUBLANES, LANES), F32)],
        scratch_shapes=[pltpu.VMEM((1, D), F32)],
        compiler_params=_params("arbitrary"),
    )(y, t)


def _adamw(w, g, m, v, *, name):
    R, C = w.shape
    tr = _tile(R, 256, SUBLANES)

    def body(w_ref, g_ref, m_ref, v_ref, d_ref, nm_ref, nv_ref):
        gv = g_ref[...]
        nm = ADAM_B1 * m_ref[...] + (1.0 - ADAM_B1) * gv
        nv = ADAM_B2 * v_ref[...] + (1.0 - ADAM_B2) * (gv * gv)
        m_hat = nm / (1.0 - ADAM_B1 ** ADAM_STEP)
        v_hat = nv / (1.0 - ADAM_B2 ** ADAM_STEP)
        d_ref[...] = -ADAM_LR * (m_hat / (jnp.sqrt(v_hat) + ADAM_EPS) + ADAM_WD * w_ref[...])
        nm_ref[...] = nm
        nv_ref[...] = nv

    blk = pl.BlockSpec((tr, C), lambda i: (i, 0))
    shp = jax.ShapeDtypeStruct((R, C), F32)
    return pl.pallas_call(
        body, name=name, grid=(R // tr,),
        in_specs=[blk] * 4, out_specs=[blk] * 3, out_shape=[shp] * 3,
        compiler_params=_params("parallel"),
    )(w, g, m, v)


def _place():
    return lax.axis_index("x"), lax.axis_index("y"), lax.axis_index("c")


def _all_gather(block, *, name):
    R, L = block.shape

    def body(x_ref, out_ref, send_sems, recv_sems, local_sem):
        x, y, c = _place()
        me, sibling = (x, y, c), (x, y, 1 - c)
        chips = [(1 - x, y), (x, 1 - y), (1 - x, 1 - y)]

        def slot(px, py, pc):
            return out_ref.at[4 * px + 2 * py + pc]

        def copy(k, blk, to, src=None):
            return pltpu.make_async_remote_copy(
                src_ref=slot(*blk) if src is None else src, dst_ref=slot(*blk),
                send_sem=send_sems.at[k], recv_sem=recv_sems.at[k], device_id=to, device_id_type=MESH)

        mine = pltpu.make_async_copy(x_ref, slot(*me), local_sem)
        mine.start()
        first = [copy(0, me, sibling, src=x_ref)]
        first += [copy(1 + j, me, (*chip, c), src=x_ref) for j, chip in enumerate(chips)]
        for cp in first:
            cp.start()
        passed = [copy(4 + j, (*chip, c), sibling) for j, chip in enumerate(chips)]
        for j, chip in enumerate(chips):
            copy(1 + j, (*chip, c), me).wait_recv()
            passed[j].start()
        copy(0, sibling, me).wait_recv()
        for j, chip in enumerate(chips):
            copy(4 + j, (*chip, 1 - c), me).wait_recv()
        for cp in first + passed:
            cp.wait_send()
        mine.wait()

    return pl.pallas_call(
        body, name=name, in_specs=[ANY], out_specs=ANY,
        out_shape=jax.ShapeDtypeStruct((N_DEV, R, L), block.dtype),
        scratch_shapes=[pltpu.SemaphoreType.DMA((7,)), pltpu.SemaphoreType.DMA((7,)), pltpu.SemaphoreType.DMA],
    )(block)


HBM_SPEC = pl.BlockSpec(memory_space=pltpu.HBM)
SEM_SPEC = pl.BlockSpec(memory_space=pltpu.SEMAPHORE)
DATAFLOW = pltpu.SideEffectType.DATAFLOW_SIDE_EFFECTING


def _split_start(plan, srcs, lands, *, dep=(), name):
    make_copies, count = plan
    ns, nb = len(srcs), len(srcs) + len(lands)
    bufs = [pltpu.with_memory_space_constraint(a, pltpu.HBM) for a in list(srcs) + list(lands)]
    n_in = nb + len(dep)

    def body(*refs):
        send_sems, recv_sems, token = refs[n_in], refs[n_in + 1], refs[-1]
        for cp in make_copies(refs[:ns], refs[ns:nb], send_sems, recv_sems):
            cp.start()
        token[...] = jnp.zeros_like(token)

    outs = pl.pallas_call(
        body, name=name, in_specs=[HBM_SPEC] * nb + [ANY] * len(dep),
        out_specs=(SEM_SPEC, SEM_SPEC, *[HBM_SPEC] * nb, pl.BlockSpec(memory_space=pltpu.VMEM)),
        out_shape=(pltpu.SemaphoreType.DMA((count,)), pltpu.SemaphoreType.DMA((count,)),
                   *[pltpu.HBM(a.shape, a.dtype) for a in bufs], jax.ShapeDtypeStruct((SUBLANES, LANES), F32)),
        input_output_aliases={i: 2 + i for i in range(nb)},
        compiler_params=pltpu.CompilerParams(has_side_effects=DATAFLOW),
    )(*bufs, *dep)
    return dict(plan=plan, ns=ns, send=outs[0], recv=outs[1], bufs=list(outs[2:2 + nb]), token=outs[-1])


def _split_wait(started, after, *, name):
    make_copies, _ = started["plan"]
    ns, nb = started["ns"], len(started["bufs"])
    after = tuple(after) if isinstance(after, (tuple, list)) else (after,)

    def body(*refs):
        for cp in make_copies(refs[:ns], refs[ns:nb], refs[nb], refs[nb + 1]):
            cp.wait_send()
            cp.wait_recv()

    outs = pl.pallas_call(
        body, name=name, in_specs=[HBM_SPEC] * nb + [SEM_SPEC, SEM_SPEC] + [ANY] * len(after),
        out_specs=[HBM_SPEC] * nb,
        out_shape=[pltpu.HBM(a.shape, a.dtype) for a in started["bufs"]],
        input_output_aliases={i: i for i in range(nb)},
        compiler_params=pltpu.CompilerParams(has_side_effects=DATAFLOW),
    )(*started["bufs"], started["send"], started["recv"], *after)
    return list(outs[:ns]), list(outs[ns:])


def _remote(src, dst, send_sems, recv_sems, k, to):
    return pltpu.make_async_remote_copy(src_ref=src, dst_ref=dst, send_sem=send_sems.at[k], recv_sem=recv_sems.at[k],
                                        device_id=to, device_id_type=MESH)


def _other_chips(x, y):
    return [(1 - x, y), (x, 1 - y), (1 - x, 1 - y)]


def _gather_send_plan(n):
    def make(srcs, lands, send_sems, recv_sems):
        x, y, c = _place()
        targets = [(x, y, 1 - c)] + [(cx, cy, c) for cx, cy in _other_chips(x, y)]
        return [_remote(srcs[t], lands[t].at[4 * x + 2 * y + c], send_sems, recv_sems, 4 * t + k, to)
                for t in range(n) for k, to in enumerate(targets)]
    return make, 4 * n


def _gather_pass_plan(n):
    def make(srcs, lands, send_sems, recv_sems):
        x, y, c = _place()
        cps = []
        for t in range(n):
            for j, (cx, cy) in enumerate(_other_chips(x, y)):
                slot = lands[t].at[4 * cx + 2 * cy + c]
                cps.append(_remote(slot, slot, send_sems, recv_sems, 3 * t + j, (x, y, 1 - c)))
        return cps
    return make, 3 * n


def _reduce_sibling_plan(n):
    def make(srcs, lands, send_sems, recv_sems):
        x, y, c = _place()
        return [_remote(srcs[t].at[2 * k + 1 - c], lands[t].at[k], send_sems, recv_sems, 4 * t + k, (x, y, 1 - c))
                for t in range(n) for k in range(4)]
    return make, 4 * n


def _reduce_chip_plan(n):
    def make(srcs, lands, send_sems, recv_sems):
        x, y, c = _place()
        return [_remote(srcs[t].at[2 * cx + cy], lands[t].at[2 * x + y], send_sems, recv_sems, 3 * t + j, (cx, cy, c))
                for t in range(n) for j, (cx, cy) in enumerate(_other_chips(x, y))]
    return make, 3 * n


def _unshard_cols(g, own, dev_idx, groups, width, *, name):
    _, A, Bs = g.shape
    ta = _tile(A, 256, 16)

    def body(dev_ref, g_ref, own_ref, *o_refs):
        for o_ref, devs in zip(o_refs, groups):
            for q, d in enumerate(devs):
                o_ref[:, q * Bs:(q + 1) * Bs] = jnp.where(dev_ref[0] == d, own_ref[...], g_ref[d])
            if len(devs) * Bs < width:
                o_ref[:, len(devs) * Bs:] = jnp.zeros((ta, width - len(devs) * Bs), g.dtype)

    return pl.pallas_call(
        body, name=name,
        grid_spec=pltpu.PrefetchScalarGridSpec(
            num_scalar_prefetch=1, grid=(A // ta,),
            in_specs=[pl.BlockSpec((N_DEV, ta, Bs), lambda i, d: (0, i, 0)), pl.BlockSpec((ta, Bs), lambda i, d: (i, 0))],
            out_specs=[pl.BlockSpec((ta, width), lambda i, d: (i, 0)) for _ in groups]),
        out_shape=[jax.ShapeDtypeStruct((A, width), g.dtype) for _ in groups],
        compiler_params=_params("parallel"),
    )(dev_idx, g, own)


def _place_own(g, own, dev_idx, *, name):
    _, As, B = g.shape
    ta = _tile(As, 256, 16)

    def body(dev_ref, _, own_ref, o_ref):
        o_ref[...] = own_ref[...]

    out = pl.pallas_call(
        body, name=name,
        grid_spec=pltpu.PrefetchScalarGridSpec(
            num_scalar_prefetch=1, grid=(As // ta,),
            in_specs=[ANY, pl.BlockSpec((ta, B), lambda i, d: (i, 0))],
            out_specs=pl.BlockSpec((None, ta, B), lambda i, d: (d[0], i, 0))),
        out_shape=jax.ShapeDtypeStruct(g.shape, g.dtype),
        input_output_aliases={1: 0},
        compiler_params=_params("parallel"),
    )(dev_idx, g, own)
    return out.reshape(N_DEV * As, B)


def _add_sibling(g, buf, c_idx, *, name):
    _, A, B = g.shape
    ta = _tile(A, 256, 16)

    def body(c_ref, g_ref, b_ref, o_ref):
        o_ref[...] = (g_ref[...].astype(F32) + b_ref[...].astype(F32)).astype(BF16)

    blk = pl.BlockSpec((None, ta, B), lambda k, i, c_ref: (k, i, 0))
    return pl.pallas_call(
        body, name=name,
        grid_spec=pltpu.PrefetchScalarGridSpec(
            num_scalar_prefetch=1, grid=(4, A // ta),
            in_specs=[pl.BlockSpec((None, ta, B), lambda k, i, c_ref: (2 * k + c_ref[0], i, 0)), blk],
            out_specs=blk),
        out_shape=jax.ShapeDtypeStruct((4, A, B), BF16),
        compiler_params=_params("parallel", "parallel"),
    )(c_idx, g, buf)


def _adamw_layer(w, m, v, own, parts, chip_idx, outs, layer, *, name):
    _, A, B = w.shape
    ta = _tile(A, 256, 16)

    def body(chip_ref, w_ref, m_ref, v_ref, own_ref, p_ref, *rest):
        g_ref, d_ref, nm_ref, nv_ref = rest[4:]
        gv = None
        for j in range(4):
            part = jnp.where(chip_ref[0] == j, own_ref[...], p_ref[j]).astype(F32)
            gv = part if gv is None else gv + part
        nm = ADAM_B1 * m_ref[...] + (1.0 - ADAM_B1) * gv
        nv = ADAM_B2 * v_ref[...] + (1.0 - ADAM_B2) * (gv * gv)
        m_hat = nm / (1.0 - ADAM_B1 ** ADAM_STEP)
        v_hat = nv / (1.0 - ADAM_B2 ** ADAM_STEP)
        g_ref[...] = gv
        d_ref[...] = -ADAM_LR * (m_hat / (jnp.sqrt(v_hat) + ADAM_EPS) + ADAM_WD * w_ref[...])
        nm_ref[...] = nm
        nv_ref[...] = nv

    blk = pl.BlockSpec((None, ta, B), lambda i, ch: (layer, i, 0))
    return pl.pallas_call(
        body, name=name,
        grid_spec=pltpu.PrefetchScalarGridSpec(
            num_scalar_prefetch=1, grid=(A // ta,),
            in_specs=[blk, blk, blk, pl.BlockSpec((None, ta, B), lambda i, ch: (ch[0], i, 0)),
                      pl.BlockSpec((4, ta, B), lambda i, ch: (0, i, 0))] + [ANY] * 4,
            out_specs=[blk] * 4),
        out_shape=[jax.ShapeDtypeStruct(o.shape, o.dtype) for o in outs],
        input_output_aliases={6 + q: q for q in range(4)},
        compiler_params=_params("parallel"),
    )(chip_idx, w, m, v, own, parts, *outs)


def _sum_slots(buf, *, name):
    n, R, L = buf.shape
    tr = _tile(R, 512, SUBLANES)

    def body(b_ref, o_ref):
        acc = b_ref[0]
        for j in range(1, n):
            acc = acc + b_ref[j]
        o_ref[...] = acc

    return pl.pallas_call(
        body, name=name, grid=(R // tr,),
        in_specs=[pl.BlockSpec((n, tr, L), lambda i: (0, i, 0))],
        out_specs=pl.BlockSpec((tr, L), lambda i: (i, 0)),
        out_shape=jax.ShapeDtypeStruct((R, L), buf.dtype),
        compiler_params=_params("parallel"),
    )(buf)


BIG = ("gla_w_in", "gla_w_out", "attn_w_qkv", "attn_w_out", "ffn_w_up", "ffn_w_down")
SMALL_SHARDED = ("gla_w_gate_up_f", "gla_w_gate_up_b", "ffn_w_conv")
REPLICATED = ("norm_mix", "norm_ffn", "gla_b_gate_f", "gla_b_gate_b", "gla_norm", "attn_q_norm", "attn_k_norm",
              "ffn_b_conv")
WEIGHTS = ("norm_mix", "norm_ffn", "gla_w_in", "gla_w_gate_up_f", "gla_b_gate_f", "gla_w_gate_up_b", "gla_b_gate_b",
           "gla_norm", "gla_w_out", "attn_w_qkv", "attn_q_norm", "attn_k_norm", "attn_w_out", "ffn_w_up", "ffn_w_conv",
           "ffn_b_conv", "ffn_w_down")


def _rows(flat, row_align):
    n = flat.shape[0]
    per = row_align * LANES
    padded = -(-n // per) * per
    return jnp.pad(flat, (0, padded - n)).reshape(padded // LANES, LANES)


def _side_by_side(gathered, own, dev):
    n, a, b = gathered.shape
    whole = lax.dynamic_update_index_in_dim(gathered, own, dev, 0)
    return jnp.transpose(whole, (1, 0, 2)).reshape(a, n * b)


def _layer_shards(w, i):
    j = i // 2
    if i % 2 == 0:
        mixer = [("mix_in", w["gla_w_in"][j]), ("mix_out", w["gla_w_out"][j]),
                 ("gate_f", w["gla_w_gate_up_f"][j]), ("gate_b", w["gla_w_gate_up_b"][j])]
    else:
        mixer = [("mix_in", w["attn_w_qkv"][j]), ("mix_out", w["attn_w_out"][j])]
    ffn = [("up", w["ffn_w_up"][i]), ("down", w["ffn_w_down"][i])]
    names = [n for n, _ in mixer + ffn] + ["conv"]
    return names, [a.astype(BF16) for _, a in mixer + ffn] + [w["ffn_w_conv"][i]]


def _layer_weights(names, own, gathered, dev, dev_idx, i):
    own, got = dict(zip(names, own)), dict(zip(names, gathered))
    every = tuple(range(N_DEV))
    half = N_DEV // 2
    tag = "_l%d" % i
    width = GLA_PROJ if i % 2 == 0 else QKV_DIM
    (mix_in,) = _unshard_cols(got["mix_in"], own["mix_in"], dev_idx, [every], width, name="unshard_mix_in" + tag)
    f = got["up"].shape[-1] * half
    up_val, up_gate = _unshard_cols(got["up"], own["up"], dev_idx, [every[:half], every[half:]], f,
                                    name="unshard_ffn_up" + tag)
    out = dict(mix_in=mix_in, up_val=up_val, up_gate=up_gate,
               mix_out=_place_own(got["mix_out"], own["mix_out"], dev_idx, name="place_mix_out" + tag),
               down=_place_own(got["down"], own["down"], dev_idx, name="place_ffn_down" + tag),
               conv=_side_by_side(got["conv"], own["conv"], dev))
    if i % 2 == 0:
        out["gate"] = _gate_matrix(_side_by_side(got["gate_f"], own["gate_f"], dev),
                                   _side_by_side(got["gate_b"], own["gate_b"], dev))
    return out


def _rope_tables(S):
    rows = S // GRID_W
    pairs = ATTN_HD // 4
    row_idx = jnp.repeat(jnp.arange(rows, dtype=F32), GRID_W)
    col_idx = jnp.tile(jnp.arange(GRID_W, dtype=F32), rows)
    inv_freq = ROPE_THETA ** (-jnp.arange(pairs, dtype=F32) / pairs)
    ang = jnp.concatenate([row_idx[:, None] * inv_freq, col_idx[:, None] * inv_freq], axis=-1)
    cos, sin = jnp.cos(ang), jnp.sin(ang)
    return jnp.concatenate([cos, cos], axis=-1), jnp.concatenate([-sin, sin], axis=-1)


def _gate_matrix(w_f, w_b):
    rk = w_f.shape[0]
    top = jnp.concatenate([w_f, jnp.zeros_like(w_f)], axis=1)
    mid = jnp.concatenate([jnp.zeros_like(w_b), w_b], axis=1)
    pad = jnp.zeros((LANES - 2 * rk, 2 * GLA_KD), w_f.dtype)
    return jnp.concatenate([top, mid, pad], axis=0)


def _local_step(x, target, rep, w, dev, idx):
    S, D = x.shape
    depth = rep["norm_mix"].shape[0]
    cs, sn = _rope_tables(S)
    row = lambda a: a.reshape(1, -1)
    ranks_cols = (GLA_PROJ - LANES, LANES)

    sent = []
    for i in range(depth):
        names, srcs = _layer_shards(w, i)
        lands = [lax.empty((N_DEV,) + a.shape, a.dtype) for a in srcs]
        dep = (sent[-1][1]["token"],) if sent else ()
        sent.append((names, _split_start(_gather_send_plan(len(srcs)), srcs, lands, dep=dep,
                                         name="weights_send_l%d" % i)))

    def arrive(i, after):
        names, started = sent[i]
        own, lands = _split_wait(started, after, name="weights_arrive_l%d" % i)
        return names, own, _split_start(_gather_pass_plan(len(lands)), [], lands, name="weights_pass_l%d" % i)

    def ready(i, passing, after):
        names, own, started = passing
        _, lands = _split_wait(started, after, name="weights_passed_l%d" % i)
        return _layer_weights(names, own, lands, dev, idx["dev"], i)

    passing = arrive(0, sent[-1][1]["token"])
    wl = ready(0, passing, passing[2]["token"])

    saved = []
    for i in range(depth):
        j = i // 2
        sv = {"x0": x, "w": wl}
        h1 = _rmsnorm_fwd(x, row(rep["norm_mix"][i]), name="norm_mix_fwd")
        sv["h1"] = h1
        if i % 2 == 0:
            bias = jnp.concatenate([rep["gla_b_gate_f"][j], rep["gla_b_gate_b"][j]]).reshape(1, -1)
            proj = _mm(h1, wl["mix_in"], name="gla_in_proj")
            logits = _mm(proj, wl["gate"], a_cols=ranks_cols, name="gla_gate_logits")
            of, ob, sf, sb = _gla_fwd(proj, logits, bias, name="gla_fwd")
            y = _gla_gate_fwd(of, ob, proj, row(rep["gla_norm"][j]), name="gla_gate_fwd")
            x = _mm(y, wl["mix_out"], res=x, name="gla_out_proj")
            sv.update(bias=bias, proj=proj, logits=logits, of=of, ob=ob, sf=sf, sb=sb, y=y)
        else:
            proj = _mm(h1, wl["mix_in"], name="attn_qkv_proj")
            qkv = _attn_prep(proj, cs, sn, row(rep["attn_q_norm"][j]), row(rep["attn_k_norm"][j]), name="attn_prep")
            o = _attn_fwd(qkv, name="attn_fwd")
            x = _mm(o, wl["mix_out"], res=x, name="attn_out_proj")
            sv.update(proj=proj, qkv=qkv, o=o)
        sv["x1"] = x
        dep = ()
        if i + 1 < depth:
            passing = arrive(i + 1, x)
            dep = (passing[2]["token"],)
        h2 = _rmsnorm_fwd(x, row(rep["norm_ffn"][i]), dep=dep, name="norm_ffn_fwd")
        F = wl["down"].shape[0]
        wc, bc = wl["conv"], rep["ffn_b_conv"][i]
        wcv, wcg, bcv, bcg = wc[:, :F], wc[:, F:], row(bc[:F]), row(bc[F:])
        uv = _mm(h2, wl["up_val"], name="ffn_up_val")
        ug = _mm(h2, wl["up_gate"], name="ffn_up_gate")
        act = _conv_act_fwd(uv, ug, wcv, wcg, bcv, bcg, name="ffn_conv_act")
        x = _mm(act, wl["down"], res=x, name="ffn_down")
        sv.update(h2=h2, uv=uv, ug=ug, act=act, wcv=wcv, wcg=wcg, bcv=bcv, bcg=bcg)
        saved.append(sv)
        if i + 1 < depth:
            wl = ready(i + 1, passing, x)

    dx, loss_tile = _loss_head(x, target, name="loss_head")
    loss = loss_tile[0, 0]

    in_sibling_stage, in_chip_stage, reduced = [], [], []

    def advance(group, after):
        tokens = []
        for tag, keys, started in in_chip_stage:
            partial, lands = _split_wait(started, after, name="grads_chips_arrive_" + tag)
            reduced.append((keys, partial, lands))
        in_chip_stage.clear()
        for tag, keys, started in in_sibling_stage:
            stacks, lands = _split_wait(started, after, name="grads_sibling_arrive_" + tag)
            partial = [_add_sibling(s, b, idx["core"], name="grads_add_sibling_%s_%d" % (tag, q))
                       for q, (s, b) in enumerate(zip(stacks, lands))]
            bufs = [lax.empty(p.shape, p.dtype) for p in partial]
            started = _split_start(_reduce_chip_plan(len(partial)), partial, bufs, name="grads_chips_send_" + tag)
            in_chip_stage.append((tag, keys, started))
            tokens.append(started["token"])
        in_sibling_stage.clear()
        if group is not None:
            tag, keys, stacks = group
            bufs = [lax.empty((4,) + s.shape[1:], s.dtype) for s in stacks]
            started = _split_start(_reduce_sibling_plan(len(stacks)), stacks, bufs, name="grads_sibling_send_" + tag)
            in_sibling_stage.append((tag, keys, started))
            tokens.append(started["token"])
        return tuple(tokens)

    def stack_for(name):
        return lax.empty((N_DEV,) + tuple(w[name].shape[1:]), BF16)

    gl = {k: [None] * depth for k in ("norm_mix", "norm_ffn", "ffn_w_conv", "ffn_b_conv")}
    gm = {k: [None] * (depth // 2) for k in ("gla_w_gate_up_f", "gla_b_gate_f", "gla_w_gate_up_b", "gla_b_gate_b",
                                             "gla_norm", "attn_q_norm", "attn_k_norm")}
    rk = GLA_GATE_RANK
    dep = ()
    for i in reversed(range(depth)):
        j = i // 2
        sv = saved[i]
        wl = sv["w"]
        dact = _mm(dx, wl["down"], tb=True, dep=dep, name="ffn_down_dgrad")
        g_down = _wgrad(sv["act"], dx, stack_for("ffn_w_down"), shard="rows", name="ffn_down_wgrad")
        dcv, dcg, dwv, dwg, dbv, dbg = _conv_act_bwd(sv["uv"], sv["ug"], sv["wcv"], sv["wcg"], sv["bcv"], sv["bcg"],
                                                     dact, name="ffn_conv_act_bwd")
        gl["ffn_w_conv"][i] = jnp.concatenate([dwv, dwg], axis=1)
        gl["ffn_b_conv"][i] = jnp.concatenate([dbv, dbg], axis=1)[0]
        duv = _conv_t(dcv, sv["wcv"], name="ffn_conv_t")
        dug = _conv_t(dcg, sv["wcg"], name="ffn_conv_t")
        dh2 = _mm(duv, wl["up_val"], tb=True, name="ffn_up_dgrad_val")
        dh2 = _mm(dug, wl["up_gate"], tb=True, res=dh2, name="ffn_up_dgrad_gate")
        g_up = _wgrad(sv["h2"], duv, stack_for("ffn_w_up"), shard="cols", group=0, name="ffn_up_wgrad_val")
        g_up = _wgrad(sv["h2"], dug, g_up, shard="cols", group=1, name="ffn_up_wgrad_gate")
        dx, dn = _rmsnorm_bwd(sv["x1"], row(rep["norm_ffn"][i]), dh2, dx, name="norm_ffn_bwd")
        gl["norm_ffn"][i] = dn[0]
        dep = advance(("ffn_l%d" % i, [("ffn_w_up", i), ("ffn_w_down", i)], [g_up, g_down]), dx)
        if i % 2 == 0:
            dy = _mm(dx, wl["mix_out"], tb=True, dep=dep, name="gla_out_dgrad")
            g_out = _wgrad(sv["y"], dx, stack_for("gla_w_out"), shard="rows", name="gla_out_wgrad")
            do, dg, dgn = _gla_gate_bwd(sv["of"], sv["ob"], sv["proj"], row(rep["gla_norm"][j]), dy, name="gla_gate_bwd")
            gm["gla_norm"][j] = dgn[0]
            dqkv_f, dlg_f, dqkv_b, dlg_b, dbias = _gla_bwd(sv["proj"], sv["logits"], sv["bias"], sv["sf"], sv["sb"], do,
                                                           name="gla_bwd")
            gm["gla_b_gate_f"][j] = dbias[0, :GLA_KD]
            gm["gla_b_gate_b"][j] = dbias[0, GLA_KD:]
            dlogits = jnp.concatenate([dlg_f, dlg_b], axis=1)
            dr = _mm(dlogits, wl["gate"], tb=True, name="gla_gate_dgrad")
            dwg_full = _mm(sv["proj"], dlogits, ta=True, a_cols=ranks_cols, name="gla_gate_wgrad")
            gm["gla_w_gate_up_f"][j] = dwg_full[:rk, :GLA_KD]
            gm["gla_w_gate_up_b"][j] = dwg_full[rk:2 * rk, GLA_KD:]
            dproj = _gla_combine(dqkv_f, dqkv_b, dg, dr, name="gla_combine")
            dh1 = _mm(dproj, wl["mix_in"], tb=True, name="gla_in_dgrad")
            g_in = _wgrad(sv["h1"], dproj, stack_for("gla_w_in"), shard="cols", name="gla_in_wgrad")
            keys = [("gla_w_in", j), ("gla_w_out", j)]
        else:
            do = _mm(dx, wl["mix_out"], tb=True, out_dtype=BF16, dep=dep, name="attn_out_dgrad")
            g_out = _wgrad(sv["o"], dx, stack_for("attn_w_out"), shard="rows", name="attn_out_wgrad")
            dq, dk, dv = _attn_bwd(sv["qkv"], do, name="attn_bwd")
            dproj, dqn, dkn = _attn_prep_bwd(sv["proj"], dq, dk, dv, cs, sn, row(rep["attn_q_norm"][j]),
                                             row(rep["attn_k_norm"][j]), name="attn_prep_bwd")
            gm["attn_q_norm"][j] = dqn[0]
            gm["attn_k_norm"][j] = dkn[0]
            dh1 = _mm(dproj, wl["mix_in"], tb=True, name="attn_qkv_dgrad")
            g_in = _wgrad(sv["h1"], dproj, stack_for("attn_w_qkv"), shard="cols", name="attn_qkv_wgrad")
            keys = [("attn_w_qkv", j), ("attn_w_out", j)]
        dx, dn = _rmsnorm_bwd(sv["x0"], row(rep["norm_mix"][i]), dh1, dx, name="norm_mix_bwd")
        gl["norm_mix"][i] = dn[0]
        dep = advance(("mix_l%d" % i, keys, [g_in, g_out]), dx)

    small = {k: jnp.stack(v) for k, v in {**gl, **gm}.items()}
    return loss, dx, reduced, small, advance


def kernel(x, norm_mix, norm_ffn, gla_w_in, gla_w_gate_up_f, gla_b_gate_f, gla_w_gate_up_b, gla_b_gate_b, gla_norm, gla_w_out, attn_w_qkv, attn_q_norm, attn_k_norm, attn_w_out, ffn_w_up, ffn_w_conv, ffn_b_conv, ffn_w_down, loss_target, m_norm_mix, m_norm_ffn, m_gla_w_in, m_gla_w_gate_up_f, m_gla_b_gate_f, m_gla_w_gate_up_b, m_gla_b_gate_b, m_gla_norm, m_gla_w_out, m_attn_w_qkv, m_attn_q_norm, m_attn_k_norm, m_attn_w_out, m_ffn_w_up, m_ffn_w_conv, m_ffn_b_conv, m_ffn_w_down, v_norm_mix, v_norm_ffn, v_gla_w_in, v_gla_w_gate_up_f, v_gla_b_gate_f, v_gla_w_gate_up_b, v_gla_b_gate_b, v_gla_norm, v_gla_w_out, v_attn_w_qkv, v_attn_q_norm, v_attn_k_norm, v_attn_w_out, v_ffn_w_up, v_ffn_w_conv, v_ffn_b_conv, v_ffn_w_down):
    given = dict(locals())
    w = {n: given[n] for n in WEIGHTS}
    m = {n: given["m_" + n] for n in WEIGHTS}
    v = {n: given["v_" + n] for n in WEIGHTS}
    shards = {n: w[n] for n in BIG + SMALL_SHARDED}
    rep = {n: w[n] for n in REPLICATED}

    x_pos, y_pos, c_pos = _place()
    dev = 4 * x_pos + 2 * y_pos + c_pos
    as_operand = lambda s: jnp.asarray(s, jnp.int32).reshape(1)
    idx = dict(dev=as_operand(dev), chip=as_operand(2 * x_pos + y_pos), core=as_operand(c_pos))

    loss_local, grad_x, reduced, small, advance = _local_step(x[0], loss_target[0], rep, shards, dev, idx)
    loss = lax.psum(loss_local, ("x", "y", "c"))

    big = {n: [lax.empty(w[n].shape, F32) for _ in range(4)] for n in BIG}

    def update_reduced():
        for keys, own, parts in reduced:
            for (n, layer), p_own, p_others in zip(keys, own, parts):
                big[n] = _adamw_layer(w[n], m[n], v[n], p_own, p_others, idx["chip"], big[n], layer,
                                      name="adamw_%s_l%d" % (n, layer))
        reduced.clear()

    update_reduced()

    rest = REPLICATED + SMALL_SHARDED
    flat = _rows(jnp.concatenate([small[n].reshape(-1) for n in rest]), SUBLANES)
    total = _sum_slots(_all_gather(flat, name="small_grads_all_gather"), name="small_grads_sum").reshape(-1)
    advance(None, [total, *big["ffn_w_up"]])
    update_reduced()
    advance(None, list(big["ffn_w_up"]))
    update_reduced()
    g, off = {}, 0
    for n in rest:
        whole = total[off:off + small[n].size].reshape(small[n].shape)
        off += small[n].size
        width = w[n].shape[-1]
        g[n] = whole if n in REPLICATED else lax.dynamic_slice_in_dim(whole, dev * width, width, axis=whole.ndim - 1)

    delta, new_m, new_v = {}, {}, {}
    for n in WEIGHTS:
        if n in BIG:
            g[n], delta[n], new_m[n], new_v[n] = big[n]
        else:
            shape = w[n].shape
            two_d = (-1, shape[-1])
            d2, m2, v2 = _adamw(w[n].reshape(two_d), g[n].reshape(two_d), m[n].reshape(two_d), v[n].reshape(two_d),
                                name="adamw_" + n)
            delta[n], new_m[n], new_v[n] = d2.reshape(shape), m2.reshape(shape), v2.reshape(shape)

    return (loss, grad_x[None], *[g[n] for n in WEIGHTS], *[delta[n] for n in WEIGHTS],
            *[new_m[n] for n in WEIGHTS], *[new_v[n] for n in WEIGHTS])
```

```python
import math

import jax
import jax.numpy as jnp
from jax import lax
from jax.experimental import pallas as pl
from jax.experimental.pallas import tpu as pltpu

F32 = jnp.float32
BF16 = jnp.bfloat16
MESH = pl.DeviceIdType.MESH

N_DEV = 8
LANES = 128
SUBLANES = 8
VMEM_LIMIT = 56 * 1024 * 1024

NORM_EPS = 1e-6
GRID_W = 64
ROPE_THETA = 10000.0
GLA_HEADS = 4
GLA_DK = 128
GLA_DV = 256
GLA_CHUNK = 64
GLA_GATE_RANK = 16
GLA_GATE_NORMALIZER = 16.0
ATTN_HD = 128
ATTN_Q_HEADS = 8
ATTN_KV_HEADS = 2
ATTN_GROUP = ATTN_Q_HEADS // ATTN_KV_HEADS

ADAM_LR = 0.001
ADAM_B1 = 0.9
ADAM_B2 = 0.999
ADAM_EPS = 1e-08
ADAM_WD = 0.01
ADAM_STEP = 10


def _tile(n, target, align=LANES):
    if n <= target:
        return n
    t = (target // align) * align
    while t >= align:
        if n % t == 0:
            return t
        t -= align
    return n


def _params(*sem):
    return pltpu.CompilerParams(dimension_semantics=sem, vmem_limit_bytes=VMEM_LIMIT)


def _dot(a, b):
    return lax.dot_general(a, b, (((1,), (0,)), ((), ())), preferred_element_type=F32)


def _dot_nt(a, b):
    return lax.dot_general(a, b, (((1,), (1,)), ((), ())), preferred_element_type=F32)


def _dot_tn(a, b):
    return lax.dot_general(a, b, (((0,), (0,)), ((), ())), preferred_element_type=F32)


def _sigmoid(x):
    return 1.0 / (1.0 + jnp.exp(-x))


def _log_sigmoid(x):
    return jnp.minimum(x, 0.0) - jnp.log(1.0 + jnp.exp(-jnp.abs(x)))


def _colsum(x):
    return jnp.sum(x, axis=0, keepdims=True)


ANY = pl.BlockSpec(memory_space=pl.ANY)


def _mm(a, b, *, ta=False, tb=False, res=None, out_dtype=F32, layer=None, a_cols=None, dep=(), name):
    if tb:
        N, K = b.shape[-2:]
    else:
        K, N = b.shape[-2:]
    a_rows, a_width = a.shape
    a_off = 0
    if a_cols is not None:
        a_off, a_width = a_cols
    if ta:
        M = a_width
        assert a_rows == K, (a.shape, b.shape, ta, tb)
    else:
        M = a_rows
        assert a_width == K, (a.shape, b.shape, ta, tb)
    tm = _tile(M, 1408) if ta else _tile(M, 1024, 16)
    tn = _tile(N, 1408)
    tk = _tile(K, 512, 16) if ta else _tile(K, 1408)
    nk = K // tk
    dims = (((0 if ta else 1,), (1 if tb else 0,)), ((), ()))

    n_in = 2 + (res is not None) + len(dep)

    def body(*refs):
        a_ref, b_ref = refs[:2]
        r_ref = refs[2] if res is not None else None
        o_ref = refs[n_in]
        scr = refs[n_in + 1:]
        part = lax.dot_general(a_ref[...].astype(BF16), b_ref[...].astype(BF16), dims, preferred_element_type=F32)

        def finish(acc):
            if r_ref is not None:
                acc = acc + r_ref[...]
            o_ref[...] = acc.astype(out_dtype)

        if nk == 1:
            finish(part)
        else:
            acc_ref = scr[0]
            k = pl.program_id(2)

            @pl.when(k == 0)
            def _():
                acc_ref[...] = part

            @pl.when(k > 0)
            def _():
                acc_ref[...] += part

            @pl.when(k == nk - 1)
            def _():
                finish(acc_ref[...])

    a_blk = a_off // (tm if ta else tk)
    assert a_off % (tm if ta else tk) == 0
    a_spec = (pl.BlockSpec((tk, tm), lambda i, j, k: (k, a_blk + i)) if ta
              else pl.BlockSpec((tm, tk), lambda i, j, k: (i, a_blk + k)))
    if layer is None:
        b_spec = pl.BlockSpec((tn, tk), lambda i, j, k: (j, k)) if tb else pl.BlockSpec((tk, tn), lambda i, j, k: (k, j))
    else:
        b_spec = (pl.BlockSpec((None, tn, tk), lambda i, j, k: (layer, j, k)) if tb
                  else pl.BlockSpec((None, tk, tn), lambda i, j, k: (layer, k, j)))
    o_spec = pl.BlockSpec((tm, tn), lambda i, j, k: (i, j))
    in_specs = [a_spec, b_spec] + ([o_spec] if res is not None else []) + [ANY] * len(dep)
    args = (a, b) + ((res,) if res is not None else ()) + tuple(dep)
    return pl.pallas_call(
        body, name=name, grid=(M // tm, N // tn, nk),
        in_specs=in_specs, out_specs=o_spec,
        out_shape=jax.ShapeDtypeStruct((M, N), out_dtype),
        scratch_shapes=[pltpu.VMEM((tm, tn), F32)] if nk > 1 else [],
        compiler_params=_params("parallel", "parallel", "arbitrary"),
    )(*args)


def _wgrad(a, b, stack, *, shard, group=0, name):
    S, N = b.shape
    M = a.shape[1]
    As, Bs = stack.shape[-2:]
    tk = _tile(S, 1024, 16)
    nk = S // tk
    if shard == "cols":
        n = N // Bs
        tm = _tile(M, 512)
        tn = N
        grid = (M // tm, 1, nk)
        o_spec = pl.BlockSpec((n, tm, Bs), lambda i, j, k: (group, i, 0))
    else:
        per = As * LANES // math.gcd(As, LANES)
        tm = M if M <= 1408 else _tile(M, 1408, per)
        n = tm // As
        tn = _tile(N, 1024)
        grid = (M // tm, N // tn, nk)
        o_spec = pl.BlockSpec((n, As, tn), lambda i, j, k: (i, 0, j))

    def body(a_ref, b_ref, _, o_ref, acc_ref):
        k = pl.program_id(2)
        part = _dot_tn(a_ref[...].astype(BF16), b_ref[...].astype(BF16))

        @pl.when(k == 0)
        def _():
            acc_ref[...] = part

        @pl.when(k > 0)
        def _():
            acc_ref[...] += part

        @pl.when(k == nk - 1)
        def _():
            for q in range(n):
                if shard == "cols":
                    o_ref[q] = acc_ref[:, q * Bs:(q + 1) * Bs].astype(stack.dtype)
                else:
                    o_ref[q] = acc_ref[q * As:(q + 1) * As, :].astype(stack.dtype)

    return pl.pallas_call(
        body, name=name, grid=grid,
        in_specs=[pl.BlockSpec((tk, tm), lambda i, j, k: (k, i)), pl.BlockSpec((tk, tn), lambda i, j, k: (k, j)),
                  pl.BlockSpec(memory_space=pl.ANY)],
        out_specs=o_spec,
        out_shape=jax.ShapeDtypeStruct(stack.shape, stack.dtype),
        input_output_aliases={2: 0},
        scratch_shapes=[pltpu.VMEM((tm, tn), F32)],
        compiler_params=_params("parallel", "parallel", "arbitrary"),
    )(a, b, stack)


def _rmsnorm_fwd(x, w, *, dep=(), name):
    S, D = x.shape
    ts = _tile(S, 512, 16)

    def body(x_ref, w_ref, *rest):
        o_ref = rest[-1]
        xv = x_ref[...]
        r = lax.rsqrt(jnp.mean(xv * xv, axis=-1, keepdims=True) + NORM_EPS)
        o_ref[...] = (xv * r * w_ref[...]).astype(BF16)

    return pl.pallas_call(
        body, name=name, grid=(S // ts,),
        in_specs=[pl.BlockSpec((ts, D), lambda i: (i, 0)), pl.BlockSpec((1, D), lambda i: (0, 0))] + [ANY] * len(dep),
        out_specs=pl.BlockSpec((ts, D), lambda i: (i, 0)),
        out_shape=jax.ShapeDtypeStruct((S, D), BF16),
        compiler_params=_params("parallel"),
    )(x, w, *dep)


def _rmsnorm_bwd(x, w, dh, dres, *, name):
    S, D = x.shape
    ts = _tile(S, 512, 16)
    n = S // ts

    def body(x_ref, w_ref, dh_ref, dr_ref, dx_ref, dw_ref):
        i = pl.program_id(0)
        xv = x_ref[...]
        r = lax.rsqrt(jnp.mean(xv * xv, axis=-1, keepdims=True) + NORM_EPS)
        xh = xv * r
        d = dh_ref[...]
        g = d * w_ref[...]
        dx_ref[...] = dr_ref[...] + r * (g - xh * jnp.mean(g * xh, axis=-1, keepdims=True))
        part = _colsum(d * xh)

        @pl.when(i == 0)
        def _():
            dw_ref[...] = part

        @pl.when(i > 0)
        def _():
            dw_ref[...] += part

    row = pl.BlockSpec((ts, D), lambda i: (i, 0))
    vec = pl.BlockSpec((1, D), lambda i: (0, 0))
    return pl.pallas_call(
        body, name=name, grid=(n,),
        in_specs=[row, vec, row, row], out_specs=[row, vec],
        out_shape=[jax.ShapeDtypeStruct((S, D), F32), jax.ShapeDtypeStruct((1, D), F32)],
        compiler_params=_params("arbitrary"),
    )(x, w, dh, dres)


def _halo_specs(S, ts, tf, row_axis):
    g = ts // SUBLANES
    last = S // SUBLANES - 1
    col_axis = 1 - row_axis
    main = pl.BlockSpec((ts, tf), lambda *ij: (ij[row_axis], ij[col_axis]))
    prev = pl.BlockSpec((SUBLANES, tf), lambda *ij: (jnp.maximum(ij[row_axis] * g - 1, 0), ij[col_axis]))
    nxt = pl.BlockSpec((SUBLANES, tf), lambda *ij: (jnp.minimum((ij[row_axis] + 1) * g, last), ij[col_axis]))
    return [main, prev, nxt]


def _shifted(u, prev_ref, next_ref, i, n):
    ts = u.shape[0]
    rid = lax.broadcasted_iota(jnp.int32, u.shape, 0)
    before = jnp.where(i > 0, prev_ref[SUBLANES - 1:SUBLANES, :], 0.0)
    after = jnp.where(i < n - 1, next_ref[0:1, :], 0.0)
    um1 = jnp.where(rid == 0, before, pltpu.roll(u, 1, 0))
    up1 = jnp.where(rid == ts - 1, after, pltpu.roll(u, ts - 1, 0))
    return um1, up1


def _conv3(u, prev_ref, next_ref, w_ref, i, n):
    um1, up1 = _shifted(u, prev_ref, next_ref, i, n)
    return w_ref[0:1, :] * um1 + w_ref[1:2, :] * u + w_ref[2:3, :] * up1


def _conv_act_fwd(uv, ug, wv, wg, bv, bg, *, name):
    S, F = uv.shape
    ts = _tile(S, 512, 16)
    tf = _tile(F, 1408)
    n = S // ts

    def body(v_ref, vp_ref, vn_ref, g_ref, gp_ref, gn_ref, wv_ref, wg_ref, bv_ref, bg_ref, o_ref):
        i = pl.program_id(0)
        val = _conv3(v_ref[...], vp_ref, vn_ref, wv_ref, i, n) + bv_ref[...]
        gate = _conv3(g_ref[...], gp_ref, gn_ref, wg_ref, i, n) + bg_ref[...]
        o_ref[...] = (gate * _sigmoid(gate) * val).astype(BF16)

    halo = _halo_specs(S, ts, tf, 0)
    w3 = pl.BlockSpec((3, tf), lambda i, j: (0, j))
    b1 = pl.BlockSpec((1, tf), lambda i, j: (0, j))
    return pl.pallas_call(
        body, name=name, grid=(n, F // tf),
        in_specs=halo + halo + [w3, w3, b1, b1],
        out_specs=pl.BlockSpec((ts, tf), lambda i, j: (i, j)),
        out_shape=jax.ShapeDtypeStruct((S, F), BF16),
        compiler_params=_params("parallel", "parallel"),
    )(uv, uv, uv, ug, ug, ug, wv, wg, bv, bg)


def _conv_act_bwd(uv, ug, wv, wg, bv, bg, dact, *, name):
    S, F = uv.shape
    ts = _tile(S, 512, 16)
    tf = _tile(F, 1408)
    n = S // ts

    def body(v_ref, vp_ref, vn_ref, g_ref, gp_ref, gn_ref, wv_ref, wg_ref, bv_ref, bg_ref, da_ref,
             dv_ref, dg_ref, dwv_ref, dwg_ref, dbv_ref, dbg_ref):
        i = pl.program_id(1)
        uvv, ugv = v_ref[...], g_ref[...]
        vm1, vp1 = _shifted(uvv, vp_ref, vn_ref, i, n)
        gm1, gp1 = _shifted(ugv, gp_ref, gn_ref, i, n)
        val = wv_ref[0:1, :] * vm1 + wv_ref[1:2, :] * uvv + wv_ref[2:3, :] * vp1 + bv_ref[...]
        gate = wg_ref[0:1, :] * gm1 + wg_ref[1:2, :] * ugv + wg_ref[2:3, :] * gp1 + bg_ref[...]
        sg = _sigmoid(gate)
        da = da_ref[...]
        dval = da * (gate * sg)
        dgate = da * val * (sg * (1.0 + gate * (1.0 - sg)))
        dv_ref[...] = dval
        dg_ref[...] = dgate
        sums = [(dwv_ref, 0, vm1 * dval), (dwv_ref, 1, uvv * dval), (dwv_ref, 2, vp1 * dval),
                (dwg_ref, 0, gm1 * dgate), (dwg_ref, 1, ugv * dgate), (dwg_ref, 2, gp1 * dgate),
                (dbv_ref, 0, dval), (dbg_ref, 0, dgate)]
        parts = [(ref, r, _colsum(t)) for ref, r, t in sums]

        @pl.when(i == 0)
        def _():
            for ref, r, part in parts:
                ref[r:r + 1, :] = part

        @pl.when(i > 0)
        def _():
            for ref, r, part in parts:
                ref[r:r + 1, :] += part

    halo = _halo_specs(S, ts, tf, 1)
    w3 = pl.BlockSpec((3, tf), lambda j, i: (0, j))
    b1 = pl.BlockSpec((1, tf), lambda j, i: (0, j))
    blk = pl.BlockSpec((ts, tf), lambda j, i: (i, j))
    return pl.pallas_call(
        body, name=name, grid=(F // tf, n),
        in_specs=halo + halo + [w3, w3, b1, b1, blk],
        out_specs=[blk, blk, w3, w3, b1, b1],
        out_shape=[jax.ShapeDtypeStruct((S, F), F32), jax.ShapeDtypeStruct((S, F), F32),
                   jax.ShapeDtypeStruct((3, F), F32), jax.ShapeDtypeStruct((3, F), F32),
                   jax.ShapeDtypeStruct((1, F), F32), jax.ShapeDtypeStruct((1, F), F32)],
        compiler_params=_params("parallel", "arbitrary"),
    )(uv, uv, uv, ug, ug, ug, wv, wg, bv, bg, dact)


def _conv_t(duc, w, *, name):
    S, F = duc.shape
    ts = _tile(S, 512, 16)
    tf = _tile(F, 1408)
    n = S // ts

    def body(d_ref, dp_ref, dn_ref, w_ref, o_ref):
        i = pl.program_id(0)
        d = d_ref[...]
        dm1, dp1 = _shifted(d, dp_ref, dn_ref, i, n)
        o_ref[...] = (w_ref[0:1, :] * dp1 + w_ref[1:2, :] * d + w_ref[2:3, :] * dm1).astype(BF16)

    return pl.pallas_call(
        body, name=name, grid=(n, F // tf),
        in_specs=_halo_specs(S, ts, tf, 0) + [pl.BlockSpec((3, tf), lambda i, j: (0, j))],
        out_specs=pl.BlockSpec((ts, tf), lambda i, j: (i, j)),
        out_shape=jax.ShapeDtypeStruct((S, F), BF16),
        compiler_params=_params("parallel", "parallel"),
    )(duc, duc, duc, w)


N_QK = ATTN_Q_HEADS + ATTN_KV_HEADS
QKV_DIM = (ATTN_Q_HEADS + 2 * ATTN_KV_HEADS) * ATTN_HD


def _head(hd):
    return slice(hd * ATTN_HD, (hd + 1) * ATTN_HD)


def _attn_prep(proj, cs, sn, qn, kn, *, name):
    S = proj.shape[0]
    ts = _tile(S, 256, 16)

    def body(p_ref, c_ref, s_ref, qn_ref, kn_ref, o_ref):
        c, s = c_ref[...], s_ref[...]
        for hd in range(N_QK):
            xv = p_ref[:, _head(hd)]
            w = qn_ref[...] if hd < ATTN_Q_HEADS else kn_ref[...]
            r = lax.rsqrt(jnp.mean(xv * xv, axis=-1, keepdims=True) + NORM_EPS)
            nrm = xv * r * w
            o_ref[:, _head(hd)] = (nrm * c + pltpu.roll(nrm, ATTN_HD // 2, 1) * s).astype(BF16)
        o_ref[:, N_QK * ATTN_HD:] = p_ref[:, N_QK * ATTN_HD:].astype(BF16)

    row = pl.BlockSpec((ts, QKV_DIM), lambda i: (i, 0))
    rot = pl.BlockSpec((ts, ATTN_HD), lambda i: (i, 0))
    vec = pl.BlockSpec((1, ATTN_HD), lambda i: (0, 0))
    return pl.pallas_call(
        body, name=name, grid=(S // ts,),
        in_specs=[row, rot, rot, vec, vec], out_specs=row,
        out_shape=jax.ShapeDtypeStruct((S, QKV_DIM), BF16),
        compiler_params=_params("parallel"),
    )(proj, cs, sn, qn, kn)


def _attn_prep_bwd(proj, dq, dk, dv, cs, sn, qn, kn, *, name):
    S = proj.shape[0]
    ts = _tile(S, 256, 16)
    nq = ATTN_Q_HEADS * ATTN_HD
    nkv = ATTN_KV_HEADS * ATTN_HD

    def body(p_ref, dq_ref, dk_ref, dv_ref, c_ref, s_ref, qn_ref, kn_ref, o_ref, dqn_ref, dkn_ref):
        i = pl.program_id(0)
        c, s = c_ref[...], s_ref[...]
        acc = [jnp.zeros((1, ATTN_HD), F32), jnp.zeros((1, ATTN_HD), F32)]
        for hd in range(N_QK):
            is_k = hd >= ATTN_Q_HEADS
            xv = p_ref[:, _head(hd)]
            w = kn_ref[...] if is_k else qn_ref[...]
            r = lax.rsqrt(jnp.mean(xv * xv, axis=-1, keepdims=True) + NORM_EPS)
            xh = xv * r
            dout = dk_ref[:, _head(hd - ATTN_Q_HEADS)] if is_k else dq_ref[:, _head(hd)]
            dn = dout * c + pltpu.roll(dout * s, ATTN_HD // 2, 1)
            acc[int(is_k)] = acc[int(is_k)] + _colsum(dn * xh)
            g = dn * w
            o_ref[:, _head(hd)] = (r * (g - xh * jnp.mean(g * xh, axis=-1, keepdims=True))).astype(BF16)
        o_ref[:, N_QK * ATTN_HD:] = dv_ref[...].astype(BF16)

        @pl.when(i == 0)
        def _():
            dqn_ref[...] = acc[0]
            dkn_ref[...] = acc[1]

        @pl.when(i > 0)
        def _():
            dqn_ref[...] += acc[0]
            dkn_ref[...] += acc[1]

    row = pl.BlockSpec((ts, QKV_DIM), lambda i: (i, 0))
    rot = pl.BlockSpec((ts, ATTN_HD), lambda i: (i, 0))
    vec = pl.BlockSpec((1, ATTN_HD), lambda i: (0, 0))
    return pl.pallas_call(
        body, name=name, grid=(S // ts,),
        in_specs=[row, pl.BlockSpec((ts, nq), lambda i: (i, 0)), pl.BlockSpec((ts, nkv), lambda i: (i, 0)),
                  pl.BlockSpec((ts, nkv), lambda i: (i, 0)), rot, rot, vec, vec],
        out_specs=[row, vec, vec],
        out_shape=[jax.ShapeDtypeStruct((S, QKV_DIM), BF16), jax.ShapeDtypeStruct((1, ATTN_HD), F32),
                   jax.ShapeDtypeStruct((1, ATTN_HD), F32)],
        compiler_params=_params("arbitrary"),
    )(proj, dq, dk, dv, cs, sn, qn, kn)


ATTN_SCALE = ATTN_HD ** -0.5


def _softmax_parts(q, k):
    s = _dot_nt(q, k)
    e = jnp.exp2((s - jnp.max(s, axis=-1, keepdims=True)) * (ATTN_SCALE * math.log2(math.e)))
    return e, 1.0 / jnp.sum(e, axis=-1, keepdims=True)


def _attn_fwd(qkv, *, name):
    S = qkv.shape[0]
    tq = _tile(S, 512, 16)
    sub = _tile(tq, 256, 16)

    def body(q_ref, k_ref, v_ref, o_ref):
        k, v = k_ref[...], v_ref[...]
        for r in range(tq // sub):
            rows = slice(r * sub, (r + 1) * sub)
            e, rl = _softmax_parts(q_ref[rows, :], k)
            o_ref[rows, :] = (_dot(e.astype(BF16), v) * rl).astype(BF16)

    return pl.pallas_call(
        body, name=name, grid=(ATTN_Q_HEADS, S // tq),
        in_specs=[pl.BlockSpec((tq, ATTN_HD), lambda h, i: (i, h)),
                  pl.BlockSpec((S, ATTN_HD), lambda h, i: (0, ATTN_Q_HEADS + h // ATTN_GROUP)),
                  pl.BlockSpec((S, ATTN_HD), lambda h, i: (0, N_QK + h // ATTN_GROUP))],
        out_specs=pl.BlockSpec((tq, ATTN_HD), lambda h, i: (i, h)),
        out_shape=jax.ShapeDtypeStruct((S, ATTN_Q_HEADS * ATTN_HD), BF16),
        compiler_params=_params("parallel", "parallel"),
    )(qkv, qkv, qkv)


def _attn_bwd(qkv, do, *, name):
    S = qkv.shape[0]
    tq = _tile(S, 256, 16)
    sub = _tile(tq, 128, 16)

    def body(q_ref, k_ref, v_ref, do_ref, dq_ref, dk_ref, dv_ref):
        first = jnp.logical_and(pl.program_id(1) == 0, pl.program_id(2) == 0)
        k, v = k_ref[...], v_ref[...]
        dk = dv = None
        for r in range(tq // sub):
            rows = slice(r * sub, (r + 1) * sub)
            q, dov = q_ref[rows, :], do_ref[rows, :]
            e, rl = _softmax_parts(q, k)
            dp = _dot_nt(dov, v)
            delta = jnp.sum(e * dp, axis=-1, keepdims=True) * rl
            dsb = (e * (dp - delta) * (rl * ATTN_SCALE)).astype(BF16)
            dq_ref[rows, :] = _dot(dsb, k)
            dk_r = _dot_tn(dsb, q)
            dv_r = _dot_tn(e.astype(BF16), (dov.astype(F32) * rl).astype(BF16))
            dk = dk_r if dk is None else dk + dk_r
            dv = dv_r if dv is None else dv + dv_r

        @pl.when(first)
        def _():
            dk_ref[...] = dk
            dv_ref[...] = dv

        @pl.when(jnp.logical_not(first))
        def _():
            dk_ref[...] += dk
            dv_ref[...] += dv

    qblk = pl.BlockSpec((tq, ATTN_HD), lambda kv, g, i: (i, kv * ATTN_GROUP + g))
    kvacc = pl.BlockSpec((S, ATTN_HD), lambda kv, g, i: (0, kv))
    return pl.pallas_call(
        body, name=name, grid=(ATTN_KV_HEADS, ATTN_GROUP, S // tq),
        in_specs=[qblk,
                  pl.BlockSpec((S, ATTN_HD), lambda kv, g, i: (0, ATTN_Q_HEADS + kv)),
                  pl.BlockSpec((S, ATTN_HD), lambda kv, g, i: (0, N_QK + kv)),
                  qblk],
        out_specs=[qblk, kvacc, kvacc],
        out_shape=[jax.ShapeDtypeStruct((S, ATTN_Q_HEADS * ATTN_HD), F32),
                   jax.ShapeDtypeStruct((S, ATTN_KV_HEADS * ATTN_HD), F32),
                   jax.ShapeDtypeStruct((S, ATTN_KV_HEADS * ATTN_HD), F32)],
        compiler_params=_params("parallel", "arbitrary", "arbitrary"),
    )(qkv, qkv, qkv, do)


GLA_KD = GLA_HEADS * GLA_DK
GLA_VD = GLA_HEADS * GLA_DV
GLA_PROJ = 2 * GLA_KD + 2 * GLA_VD + LANES
GLA_SCALE = GLA_DK ** -0.5


def _split3(x):
    hi = x.astype(BF16)
    r1 = x - hi.astype(F32)
    mid = r1.astype(BF16)
    lo = (r1 - mid.astype(F32)).astype(BF16)
    return hi, mid, lo


def _cumdot(t, x):
    hi, mid, lo = _split3(x)
    return _dot(t, hi) + _dot(t, mid) + _dot(t, lo)


def _gla_masks(d):
    c = GLA_CHUNK
    row = lax.broadcasted_iota(jnp.int32, (c, c), 0)
    col = lax.broadcasted_iota(jnp.int32, (c, c), 1)
    lower, upper = col <= row, col >= row
    if d == 0:
        return lower.astype(BF16), upper.astype(BF16), lower
    return upper.astype(BF16), lower.astype(BF16), col > row


def _gla_decay(lg, bias, cum, d):
    xl = lg + bias
    la = _log_sigmoid(xl) * (1.0 / GLA_GATE_NORMALIZER)
    b = _cumdot(cum, la)
    b_end = b[GLA_CHUNK - 1:GLA_CHUNK, :] if d == 0 else b[0:1, :]
    return xl, b, b_end


def _gla_specs(S, n):
    c = GLA_CHUNK
    up = lambda i: i
    down = lambda i: n - 1 - i
    def specs(order):
        return dict(
            q=pl.BlockSpec((c, GLA_KD), lambda i: (order(i), 0)),
            k=pl.BlockSpec((c, GLA_KD), lambda i: (order(i), 1)),
            v=pl.BlockSpec((c, GLA_VD), lambda i: (order(i), 1)),
            st=pl.BlockSpec((1, GLA_HEADS, GLA_DV, GLA_DK), lambda i: (order(i), 0, 0, 0)),
            wide=pl.BlockSpec((c, GLA_VD), lambda i: (order(i), 0)),
            qkv=pl.BlockSpec((c, 2 * GLA_KD + GLA_VD), lambda i: (order(i), 0)),
        )
    return specs(up), specs(down), up, down


def _gla_fwd(proj, logits, bias, *, name):
    S = proj.shape[0]
    c = GLA_CHUNK
    n = S // c
    su, sd, up, down = _gla_specs(S, n)

    def body(qf, kf, vf, lf, qb, kb, vb, lb, bias_ref, of, ob, sf, sb, st):
        @pl.when(pl.program_id(0) == 0)
        def _():
            st[...] = jnp.zeros_like(st)

        for d, (q_r, k_r, v_r, l_r, o_r, s_r) in enumerate(((qf, kf, vf, lf, of, sf), (qb, kb, vb, lb, ob, sb))):
            cum, _, mask = _gla_masks(d)
            _, b, b_end = _gla_decay(l_r[...], bias_ref[:, d * GLA_KD:(d + 1) * GLA_KD], cum, d)
            dend = jnp.exp(b_end)
            k = k_r[...]
            qd = (q_r[...] * GLA_SCALE * jnp.exp(b)).astype(BF16)
            ki = (k * jnp.exp(-b)).astype(BF16)
            ke = (k * jnp.exp(b_end - b)).astype(BF16)
            for h in range(GLA_HEADS):
                ks = slice(h * GLA_DK, (h + 1) * GLA_DK)
                vs = slice(h * GLA_DV, (h + 1) * GLA_DV)
                stp = st[d * GLA_HEADS + h]
                s_r[0, h] = stp
                v = v_r[:, vs].astype(BF16)
                att = jnp.where(mask, _dot_nt(qd[:, ks], ki[:, ks]), 0.0).astype(BF16)
                o_r[:, vs] = _dot(att, v) + _dot_nt(qd[:, ks], stp.astype(BF16))
                st[d * GLA_HEADS + h] = stp * dend[:, ks] + _dot_tn(v, ke[:, ks])

    lg_f = pl.BlockSpec((c, GLA_KD), lambda i: (up(i), 0))
    lg_b = pl.BlockSpec((c, GLA_KD), lambda i: (down(i), 1))
    return pl.pallas_call(
        body, name=name, grid=(n,),
        in_specs=[su["q"], su["k"], su["v"], lg_f, sd["q"], sd["k"], sd["v"], lg_b,
                  pl.BlockSpec((1, 2 * GLA_KD), lambda i: (0, 0))],
        out_specs=[su["wide"], sd["wide"], su["st"], sd["st"]],
        out_shape=[jax.ShapeDtypeStruct((S, GLA_VD), F32), jax.ShapeDtypeStruct((S, GLA_VD), F32),
                   jax.ShapeDtypeStruct((n, GLA_HEADS, GLA_DV, GLA_DK), F32),
                   jax.ShapeDtypeStruct((n, GLA_HEADS, GLA_DV, GLA_DK), F32)],
        scratch_shapes=[pltpu.VMEM((2 * GLA_HEADS, GLA_DV, GLA_DK), F32)],
        compiler_params=_params("arbitrary"),
    )(proj, proj, proj, logits, proj, proj, proj, logits, bias)


def _gla_bwd(proj, logits, bias, sf, sb, do, *, name):
    S = proj.shape[0]
    c = GLA_CHUNK
    n = S // c
    su, sd, up, down = _gla_specs(S, n)

    def body(qf, kf, vf, lf, stf, dof, qb, kb, vb, lb, stb, dob, bias_ref,
             dqkv_f, dlg_f, dqkv_b, dlg_b, dbias, dst):
        first = pl.program_id(0) == 0

        @pl.when(first)
        def _():
            dst[...] = jnp.zeros_like(dst)

        dbias_parts = []
        for d, (q_r, k_r, v_r, l_r, s_r, do_r, dqkv_r, dlg_r) in enumerate(
                ((qf, kf, vf, lf, stf, dof, dqkv_f, dlg_f), (qb, kb, vb, lb, stb, dob, dqkv_b, dlg_b))):
            cum, cum_t, mask = _gla_masks(d)
            xl, b, b_end = _gla_decay(l_r[...], bias_ref[:, d * GLA_KD:(d + 1) * GLA_KD], cum, d)
            e, ei, ee, dend = jnp.exp(b), jnp.exp(-b), jnp.exp(b_end - b), jnp.exp(b_end)
            k = k_r[...]
            qd32 = q_r[...] * GLA_SCALE * e
            ki32 = k * ei
            ke32 = k * ee
            qd, ki, ke = qd32.astype(BF16), ki32.astype(BF16), ke32.astype(BF16)
            db_parts, dbe_parts = [], []
            for h in range(GLA_HEADS):
                ks = slice(h * GLA_DK, (h + 1) * GLA_DK)
                vs = slice(h * GLA_DV, (h + 1) * GLA_DV)
                stp = s_r[0, h]
                dstn = dst[d * GLA_HEADS + h]
                dstn_b = dstn.astype(BF16)
                v = v_r[:, vs].astype(BF16)
                dov = do_r[:, vs]
                att = jnp.where(mask, _dot_nt(qd[:, ks], ki[:, ks]), 0.0).astype(BF16)
                datt = jnp.where(mask, _dot_nt(dov, v), 0.0).astype(BF16)
                dqkv_r[:, 2 * GLA_KD + h * GLA_DV:2 * GLA_KD + (h + 1) * GLA_DV] = (
                    _dot_tn(att, dov) + _dot_nt(ke[:, ks], dstn_b))
                dqd = _dot(datt, ki[:, ks]) + _dot(dov, stp.astype(BF16))
                dki = _dot_tn(datt, qd[:, ks])
                dke = _dot(v, dstn_b)
                d_dend = _colsum(stp * dstn)
                dst[d * GLA_HEADS + h] = _dot_tn(dov, qd[:, ks]) + dstn * dend[:, ks]
                dqkv_r[:, ks] = dqd * e[:, ks] * GLA_SCALE
                dqkv_r[:, GLA_KD + h * GLA_DK:GLA_KD + (h + 1) * GLA_DK] = dki * ei[:, ks] + dke * ee[:, ks]
                dke_ke = dke * ke32[:, ks]
                db_parts.append(dqd * qd32[:, ks] - dki * ki32[:, ks] - dke_ke)
                dbe_parts.append(_colsum(dke_ke) + d_dend * dend[:, ks])
            db = jnp.concatenate(db_parts, axis=1)
            db_end = jnp.concatenate(dbe_parts, axis=1)
            dla = _cumdot(cum_t, db) + db_end
            dlg = dla * (1.0 / GLA_GATE_NORMALIZER) * _sigmoid(-xl)
            dlg_r[...] = dlg
            dbias_parts.append(_colsum(dlg))
        dbv = jnp.concatenate(dbias_parts, axis=1)

        @pl.when(first)
        def _():
            dbias[...] = dbv

        @pl.when(jnp.logical_not(first))
        def _():
            dbias[...] += dbv

    lg_f = pl.BlockSpec((c, GLA_KD), lambda i: (down(i), 0))
    lg_b = pl.BlockSpec((c, GLA_KD), lambda i: (up(i), 1))
    dlg_f = pl.BlockSpec((c, GLA_KD), lambda i: (down(i), 0))
    dlg_b = pl.BlockSpec((c, GLA_KD), lambda i: (up(i), 0))
    return pl.pallas_call(
        body, name=name, grid=(n,),
        in_specs=[sd["q"], sd["k"], sd["v"], lg_f, sd["st"], sd["wide"],
                  su["q"], su["k"], su["v"], lg_b, su["st"], su["wide"],
                  pl.BlockSpec((1, 2 * GLA_KD), lambda i: (0, 0))],
        out_specs=[sd["qkv"], dlg_f, su["qkv"], dlg_b, pl.BlockSpec((1, 2 * GLA_KD), lambda i: (0, 0))],
        out_shape=[jax.ShapeDtypeStruct((S, 2 * GLA_KD + GLA_VD), F32), jax.ShapeDtypeStruct((S, GLA_KD), F32),
                   jax.ShapeDtypeStruct((S, 2 * GLA_KD + GLA_VD), F32), jax.ShapeDtypeStruct((S, GLA_KD), F32),
                   jax.ShapeDtypeStruct((1, 2 * GLA_KD), F32)],
        scratch_shapes=[pltpu.VMEM((2 * GLA_HEADS, GLA_DV, GLA_DK), F32)],
        compiler_params=_params("arbitrary"),
    )(proj, proj, proj, logits, sf, do, proj, proj, proj, logits, sb, do, bias)


def _gla_gate_fwd(of, ob, proj, w, *, name):
    S = of.shape[0]
    ts = _tile(S, 256, 16)

    def body(of_ref, ob_ref, g_ref, w_ref, y_ref):
        for h in range(GLA_HEADS):
            vs = slice(h * GLA_DV, (h + 1) * GLA_DV)
            o = of_ref[:, vs] + ob_ref[:, vs]
            r = lax.rsqrt(jnp.mean(o * o, axis=-1, keepdims=True) + NORM_EPS)
            g = g_ref[:, vs]
            y_ref[:, vs] = (o * r * w_ref[...] * (g * _sigmoid(g))).astype(BF16)

    wide = pl.BlockSpec((ts, GLA_VD), lambda i: (i, 0))
    return pl.pallas_call(
        body, name=name, grid=(S // ts,),
        in_specs=[wide, wide, pl.BlockSpec((ts, GLA_VD), lambda i: (i, 2)), pl.BlockSpec((1, GLA_DV), lambda i: (0, 0))],
        out_specs=wide,
        out_shape=jax.ShapeDtypeStruct((S, GLA_VD), BF16),
        compiler_params=_params("parallel"),
    )(of, ob, proj, w)


def _gla_gate_bwd(of, ob, proj, w, dy, *, name):
    S = of.shape[0]
    ts = _tile(S, 256, 16)

    def body(of_ref, ob_ref, g_ref, w_ref, dy_ref, do_ref, dg_ref, dw_ref):
        i = pl.program_id(0)
        acc = jnp.zeros((1, GLA_DV), F32)
        for h in range(GLA_HEADS):
            vs = slice(h * GLA_DV, (h + 1) * GLA_DV)
            o = of_ref[:, vs] + ob_ref[:, vs]
            r = lax.rsqrt(jnp.mean(o * o, axis=-1, keepdims=True) + NORM_EPS)
            oh = o * r
            g = g_ref[:, vs]
            sg = _sigmoid(g)
            dyv = dy_ref[:, vs]
            dn = dyv * (g * sg)
            dg_ref[:, vs] = dyv * (oh * w_ref[...]) * (sg * (1.0 + g * (1.0 - sg)))
            acc = acc + _colsum(dn * oh)
            gg = dn * w_ref[...]
            do_ref[:, vs] = (r * (gg - oh * jnp.mean(gg * oh, axis=-1, keepdims=True))).astype(BF16)

        @pl.when(i == 0)
        def _():
            dw_ref[...] = acc

        @pl.when(i > 0)
        def _():
            dw_ref[...] += acc

    wide = pl.BlockSpec((ts, GLA_VD), lambda i: (i, 0))
    vec = pl.BlockSpec((1, GLA_DV), lambda i: (0, 0))
    return pl.pallas_call(
        body, name=name, grid=(S // ts,),
        in_specs=[wide, wide, pl.BlockSpec((ts, GLA_VD), lambda i: (i, 2)), vec, wide],
        out_specs=[wide, wide, vec],
        out_shape=[jax.ShapeDtypeStruct((S, GLA_VD), BF16), jax.ShapeDtypeStruct((S, GLA_VD), F32),
                   jax.ShapeDtypeStruct((1, GLA_DV), F32)],
        compiler_params=_params("arbitrary"),
    )(of, ob, proj, w, dy)


def _gla_combine(dqkv_f, dqkv_b, dg, dr, *, name):
    S = dg.shape[0]
    ts = _tile(S, 512, 16)
    nqkv = 2 * GLA_KD + GLA_VD

    def body(f_ref, b_ref, g_ref, r_ref, o_ref):
        o_ref[:, :nqkv] = (f_ref[...] + b_ref[...]).astype(BF16)
        o_ref[:, nqkv:nqkv + GLA_VD] = g_ref[...].astype(BF16)
        o_ref[:, nqkv + GLA_VD:] = r_ref[...].astype(BF16)

    return pl.pallas_call(
        body, name=name, grid=(S // ts,),
        in_specs=[pl.BlockSpec((ts, nqkv), lambda i: (i, 0)), pl.BlockSpec((ts, nqkv), lambda i: (i, 0)),
                  pl.BlockSpec((ts, GLA_VD), lambda i: (i, 0)), pl.BlockSpec((ts, LANES), lambda i: (i, 0))],
        out_specs=pl.BlockSpec((ts, GLA_PROJ), lambda i: (i, 0)),
        out_shape=jax.ShapeDtypeStruct((S, GLA_PROJ), BF16),
        compiler_params=_params("parallel"),
    )(dqkv_f, dqkv_b, dg, dr)


def _loss_head(y, t, *, name):
    S, D = y.shape
    ts = _tile(S, 512, 16)
    n = S // ts

    def body(y_ref, t_ref, dy_ref, l_ref, acc):
        i = pl.program_id(0)
        diff = y_ref[...] - t_ref[...]
        dy_ref[...] = diff * (1.0 / D)
        part = _colsum(diff * diff)

        @pl.when(i == 0)
        def _():
            acc[...] = part

        @pl.when(i > 0)
        def _():
            acc[...] += part

        @pl.when(i == n - 1)
        def _():
            l_ref[...] = jnp.full(l_ref.shape, 0.5 / D, F32) * jnp.sum(acc[...])

    row = pl.BlockSpec((ts, D), lambda i: (i, 0))
    return pl.pallas_call(
        body, name=name, grid=(n,),
        in_specs=[row, row],
        out_specs=[row, pl.BlockSpec((SUBLANES, LANES), lambda i: (0, 0))],
        out_shape=[jax.ShapeDtypeStruct((S, D), F32), jax.ShapeDtypeStruct((SUBLANES, LANES), F32)],
        scratch_shapes=[pltpu.VMEM((1, D), F32)],
        compiler_params=_params("arbitrary"),
    )(y, t)


def _adamw(w, g, m, v, *, name):
    R, C = w.shape
    tr = _tile(R, 256, SUBLANES)

    def body(w_ref, g_ref, m_ref, v_ref, d_ref, nm_ref, nv_ref):
        gv = g_ref[...]
        nm = ADAM_B1 * m_ref[...] + (1.0 - ADAM_B1) * gv
        nv = ADAM_B2 * v_ref[...] + (1.0 - ADAM_B2) * (gv * gv)
        m_hat = nm / (1.0 - ADAM_B1 ** ADAM_STEP)
        v_hat = nv / (1.0 - ADAM_B2 ** ADAM_STEP)
        d_ref[...] = -ADAM_LR * (m_hat / (jnp.sqrt(v_hat) + ADAM_EPS) + ADAM_WD * w_ref[...])
        nm_ref[...] = nm
        nv_ref[...] = nv

    blk = pl.BlockSpec((tr, C), lambda i: (i, 0))
    shp = jax.ShapeDtypeStruct((R, C), F32)
    return pl.pallas_call(
        body, name=name, grid=(R // tr,),
        in_specs=[blk] * 4, out_specs=[blk] * 3, out_shape=[shp] * 3,
        compiler_params=_params("parallel"),
    )(w, g, m, v)


def _place():
    return lax.axis_index("x"), lax.axis_index("y"), lax.axis_index("c")


def _all_gather(block, *, name):
    R, L = block.shape

    def body(x_ref, out_ref, send_sems, recv_sems, local_sem):
        x, y, c = _place()
        me, sibling = (x, y, c), (x, y, 1 - c)
        chips = [(1 - x, y), (x, 1 - y), (1 - x, 1 - y)]

        def slot(px, py, pc):
            return out_ref.at[4 * px + 2 * py + pc]

        def copy(k, blk, to, src=None):
            return pltpu.make_async_remote_copy(
                src_ref=slot(*blk) if src is None else src, dst_ref=slot(*blk),
                send_sem=send_sems.at[k], recv_sem=recv_sems.at[k], device_id=to, device_id_type=MESH)

        mine = pltpu.make_async_copy(x_ref, slot(*me), local_sem)
        mine.start()
        first = [copy(0, me, sibling, src=x_ref)]
        first += [copy(1 + j, me, (*chip, c), src=x_ref) for j, chip in enumerate(chips)]
        for cp in first:
            cp.start()
        passed = [copy(4 + j, (*chip, c), sibling) for j, chip in enumerate(chips)]
        for j, chip in enumerate(chips):
            copy(1 + j, (*chip, c), me).wait_recv()
            passed[j].start()
        copy(0, sibling, me).wait_recv()
        for j, chip in enumerate(chips):
            copy(4 + j, (*chip, 1 - c), me).wait_recv()
        for cp in first + passed:
            cp.wait_send()
        mine.wait()

    return pl.pallas_call(
        body, name=name, in_specs=[ANY], out_specs=ANY,
        out_shape=jax.ShapeDtypeStruct((N_DEV, R, L), block.dtype),
        scratch_shapes=[pltpu.SemaphoreType.DMA((7,)), pltpu.SemaphoreType.DMA((7,)), pltpu.SemaphoreType.DMA],
    )(block)


HBM_SPEC = pl.BlockSpec(memory_space=pltpu.HBM)
SEM_SPEC = pl.BlockSpec(memory_space=pltpu.SEMAPHORE)
DATAFLOW = pltpu.SideEffectType.DATAFLOW_SIDE_EFFECTING


def _split_start(plan, srcs, lands, *, dep=(), name):
    make_copies, count = plan
    ns, nb = len(srcs), len(srcs) + len(lands)
    bufs = [pltpu.with_memory_space_constraint(a, pltpu.HBM) for a in list(srcs) + list(lands)]
    n_in = nb + len(dep)

    def body(*refs):
        send_sems, recv_sems, token = refs[n_in], refs[n_in + 1], refs[-1]
        for cp in make_copies(refs[:ns], refs[ns:nb], send_sems, recv_sems):
            cp.start()
        token[...] = jnp.zeros_like(token)

    outs = pl.pallas_call(
        body, name=name, in_specs=[HBM_SPEC] * nb + [ANY] * len(dep),
        out_specs=(SEM_SPEC, SEM_SPEC, *[HBM_SPEC] * nb, pl.BlockSpec(memory_space=pltpu.VMEM)),
        out_shape=(pltpu.SemaphoreType.DMA((count,)), pltpu.SemaphoreType.DMA((count,)),
                   *[pltpu.HBM(a.shape, a.dtype) for a in bufs], jax.ShapeDtypeStruct((SUBLANES, LANES), F32)),
        input_output_aliases={i: 2 + i for i in range(nb)},
        compiler_params=pltpu.CompilerParams(has_side_effects=DATAFLOW),
    )(*bufs, *dep)
    return dict(plan=plan, ns=ns, send=outs[0], recv=outs[1], bufs=list(outs[2:2 + nb]), token=outs[-1])


def _split_wait(started, after, *, name):
    make_copies, _ = started["plan"]
    ns, nb = started["ns"], len(started["bufs"])
    after = tuple(after) if isinstance(after, (tuple, list)) else (after,)

    def body(*refs):
        for cp in make_copies(refs[:ns], refs[ns:nb], refs[nb], refs[nb + 1]):
            cp.wait_send()
            cp.wait_recv()

    outs = pl.pallas_call(
        body, name=name, in_specs=[HBM_SPEC] * nb + [SEM_SPEC, SEM_SPEC] + [ANY] * len(after),
        out_specs=[HBM_SPEC] * nb,
        out_shape=[pltpu.HBM(a.shape, a.dtype) for a in started["bufs"]],
        input_output_aliases={i: i for i in range(nb)},
        compiler_params=pltpu.CompilerParams(has_side_effects=DATAFLOW),
    )(*started["bufs"], started["send"], started["recv"], *after)
    return list(outs[:ns]), list(outs[ns:])


def _remote(src, dst, send_sems, recv_sems, k, to):
    return pltpu.make_async_remote_copy(src_ref=src, dst_ref=dst, send_sem=send_sems.at[k], recv_sem=recv_sems.at[k],
                                        device_id=to, device_id_type=MESH)


def _other_chips(x, y):
    return [(1 - x, y), (x, 1 - y), (1 - x, 1 - y)]


def _gather_send_plan(n):
    def make(srcs, lands, send_sems, recv_sems):
        x, y, c = _place()
        targets = [(x, y, 1 - c)] + [(cx, cy, c) for cx, cy in _other_chips(x, y)]
        return [_remote(srcs[t], lands[t].at[4 * x + 2 * y + c], send_sems, recv_sems, 4 * t + k, to)
                for t in range(n) for k, to in enumerate(targets)]
    return make, 4 * n


def _gather_pass_plan(n):
    def make(srcs, lands, send_sems, recv_sems):
        x, y, c = _place()
        cps = []
        for t in range(n):
            for j, (cx, cy) in enumerate(_other_chips(x, y)):
                slot = lands[t].at[4 * cx + 2 * cy + c]
                cps.append(_remote(slot, slot, send_sems, recv_sems, 3 * t + j, (x, y, 1 - c)))
        return cps
    return make, 3 * n


def _reduce_sibling_plan(n):
    def make(srcs, lands, send_sems, recv_sems):
        x, y, c = _place()
        return [_remote(srcs[t].at[2 * k + 1 - c], lands[t].at[k], send_sems, recv_sems, 4 * t + k, (x, y, 1 - c))
                for t in range(n) for k in range(4)]
    return make, 4 * n


def _reduce_chip_plan(n):
    def make(srcs, lands, send_sems, recv_sems):
        x, y, c = _place()
        return [_remote(srcs[t].at[2 * cx + cy], lands[t].at[2 * x + y], send_sems, recv_sems, 3 * t + j, (cx, cy, c))
                for t in range(n) for j, (cx, cy) in enumerate(_other_chips(x, y))]
    return make, 3 * n


def _unshard_cols(g, own, dev_idx, groups, width, *, name):
    _, A, Bs = g.shape
    ta = _tile(A, 256, 16)

    def body(dev_ref, g_ref, own_ref, *o_refs):
        for o_ref, devs in zip(o_refs, groups):
            for q, d in enumerate(devs):
                o_ref[:, q * Bs:(q + 1) * Bs] = jnp.where(dev_ref[0] == d, own_ref[...], g_ref[d])
            if len(devs) * Bs < width:
                o_ref[:, len(devs) * Bs:] = jnp.zeros((ta, width - len(devs) * Bs), g.dtype)

    return pl.pallas_call(
        body, name=name,
        grid_spec=pltpu.PrefetchScalarGridSpec(
            num_scalar_prefetch=1, grid=(A // ta,),
            in_specs=[pl.BlockSpec((N_DEV, ta, Bs), lambda i, d: (0, i, 0)), pl.BlockSpec((ta, Bs), lambda i, d: (i, 0))],
            out_specs=[pl.BlockSpec((ta, width), lambda i, d: (i, 0)) for _ in groups]),
        out_shape=[jax.ShapeDtypeStruct((A, width), g.dtype) for _ in groups],
        compiler_params=_params("parallel"),
    )(dev_idx, g, own)


def _place_own(g, own, dev_idx, *, name):
    _, As, B = g.shape
    ta = _tile(As, 256, 16)

    def body(dev_ref, _, own_ref, o_ref):
        o_ref[...] = own_ref[...]

    out = pl.pallas_call(
        body, name=name,
        grid_spec=pltpu.PrefetchScalarGridSpec(
            num_scalar_prefetch=1, grid=(As // ta,),
            in_specs=[ANY, pl.BlockSpec((ta, B), lambda i, d: (i, 0))],
            out_specs=pl.BlockSpec((None, ta, B), lambda i, d: (d[0], i, 0))),
        out_shape=jax.ShapeDtypeStruct(g.shape, g.dtype),
        input_output_aliases={1: 0},
        compiler_params=_params("parallel"),
    )(dev_idx, g, own)
    return out.reshape(N_DEV * As, B)


def _add_sibling(g, buf, c_idx, *, name):
    _, A, B = g.shape
    ta = _tile(A, 256, 16)

    def body(c_ref, g_ref, b_ref, o_ref):
        o_ref[...] = (g_ref[...].astype(F32) + b_ref[...].astype(F32)).astype(BF16)

    blk = pl.BlockSpec((None, ta, B), lambda k, i, c_ref: (k, i, 0))
    return pl.pallas_call(
        body, name=name,
        grid_spec=pltpu.PrefetchScalarGridSpec(
            num_scalar_prefetch=1, grid=(4, A // ta),
            in_specs=[pl.BlockSpec((None, ta, B), lambda k, i, c_ref: (2 * k + c_ref[0], i, 0)), blk],
            out_specs=blk),
        out_shape=jax.ShapeDtypeStruct((4, A, B), BF16),
        compiler_params=_params("parallel", "parallel"),
    )(c_idx, g, buf)


def _adamw_layer(w, m, v, own, parts, chip_idx, outs, layer, *, name):
    _, A, B = w.shape
    ta = _tile(A, 256, 16)

    def body(chip_ref, w_ref, m_ref, v_ref, own_ref, p_ref, *rest):
        g_ref, d_ref, nm_ref, nv_ref = rest[4:]
        gv = None
        for j in range(4):
            part = jnp.where(chip_ref[0] == j, own_ref[...], p_ref[j]).astype(F32)
            gv = part if gv is None else gv + part
        nm = ADAM_B1 * m_ref[...] + (1.0 - ADAM_B1) * gv
        nv = ADAM_B2 * v_ref[...] + (1.0 - ADAM_B2) * (gv * gv)
        m_hat = nm / (1.0 - ADAM_B1 ** ADAM_STEP)
        v_hat = nv / (1.0 - ADAM_B2 ** ADAM_STEP)
        g_ref[...] = gv
        d_ref[...] = -ADAM_LR * (m_hat / (jnp.sqrt(v_hat) + ADAM_EPS) + ADAM_WD * w_ref[...])
        nm_ref[...] = nm
        nv_ref[...] = nv

    blk = pl.BlockSpec((None, ta, B), lambda i, ch: (layer, i, 0))
    return pl.pallas_call(
        body, name=name,
        grid_spec=pltpu.PrefetchScalarGridSpec(
            num_scalar_prefetch=1, grid=(A // ta,),
            in_specs=[blk, blk, blk, pl.BlockSpec((None, ta, B), lambda i, ch: (ch[0], i, 0)),
                      pl.BlockSpec((4, ta, B), lambda i, ch: (0, i, 0))] + [ANY] * 4,
            out_specs=[blk] * 4),
        out_shape=[jax.ShapeDtypeStruct(o.shape, o.dtype) for o in outs],
        input_output_aliases={6 + q: q for q in range(4)},
        compiler_params=_params("parallel"),
    )(chip_idx, w, m, v, own, parts, *outs)


def _sum_slots(buf, *, name):
    n, R, L = buf.shape
    tr = _tile(R, 512, SUBLANES)

    def body(b_ref, o_ref):
        acc = b_ref[0]
        for j in range(1, n):
            acc = acc + b_ref[j]
        o_ref[...] = acc

    return pl.pallas_call(
        body, name=name, grid=(R // tr,),
        in_specs=[pl.BlockSpec((n, tr, L), lambda i: (0, i, 0))],
        out_specs=pl.BlockSpec((tr, L), lambda i: (i, 0)),
        out_shape=jax.ShapeDtypeStruct((R, L), buf.dtype),
        compiler_params=_params("parallel"),
    )(buf)


BIG = ("gla_w_in", "gla_w_out", "attn_w_qkv", "attn_w_out", "ffn_w_up", "ffn_w_down")
SMALL_SHARDED = ("gla_w_gate_up_f", "gla_w_gate_up_b", "ffn_w_conv")
REPLICATED = ("norm_mix", "norm_ffn", "gla_b_gate_f", "gla_b_gate_b", "gla_norm", "attn_q_norm", "attn_k_norm",
              "ffn_b_conv")
WEIGHTS = ("norm_mix", "norm_ffn", "gla_w_in", "gla_w_gate_up_f", "gla_b_gate_f", "gla_w_gate_up_b", "gla_b_gate_b",
           "gla_norm", "gla_w_out", "attn_w_qkv", "attn_q_norm", "attn_k_norm", "attn_w_out", "ffn_w_up", "ffn_w_conv",
           "ffn_b_conv", "ffn_w_down")


def _rows(flat, row_align):
    n = flat.shape[0]
    per = row_align * LANES
    padded = -(-n // per) * per
    return jnp.pad(flat, (0, padded - n)).reshape(padded // LANES, LANES)


def _side_by_side(gathered, own, dev):
    n, a, b = gathered.shape
    whole = lax.dynamic_update_index_in_dim(gathered, own, dev, 0)
    return jnp.transpose(whole, (1, 0, 2)).reshape(a, n * b)


def _layer_shards(w, i, mixer, ffn):
    j = i // 2
    parts = []
    if mixer and i % 2 == 0:
        parts += [("mix_in", w["gla_w_in"][j].astype(BF16)), ("mix_out", w["gla_w_out"][j].astype(BF16)),
                  ("gate_f", w["gla_w_gate_up_f"][j].astype(BF16)), ("gate_b", w["gla_w_gate_up_b"][j].astype(BF16))]
    elif mixer:
        parts += [("mix_in", w["attn_w_qkv"][j].astype(BF16)), ("mix_out", w["attn_w_out"][j].astype(BF16))]
    if ffn:
        parts += [("up", w["ffn_w_up"][i].astype(BF16)), ("down", w["ffn_w_down"][i].astype(BF16)),
                  ("conv", w["ffn_w_conv"][i])]
    return [n for n, _ in parts], [a for _, a in parts]


def _layer_weights(names, own, gathered, dev, dev_idx, i):
    own, got = dict(zip(names, own)), dict(zip(names, gathered))
    every = tuple(range(N_DEV))
    half = N_DEV // 2
    tag = "_l%d" % i
    out = {}
    if "mix_in" in got:
        width = GLA_PROJ if i % 2 == 0 else QKV_DIM
        (out["mix_in"],) = _unshard_cols(got["mix_in"], own["mix_in"], dev_idx, [every], width,
                                         name="unshard_mix_in" + tag)
        out["mix_out"] = _place_own(got["mix_out"], own["mix_out"], dev_idx, name="place_mix_out" + tag)
    if "gate_f" in got:
        out["gate"] = _gate_matrix(_side_by_side(got["gate_f"], own["gate_f"], dev),
                                   _side_by_side(got["gate_b"], own["gate_b"], dev))
    if "up" in got:
        f = got["up"].shape[-1] * half
        out["up_val"], out["up_gate"] = _unshard_cols(got["up"], own["up"], dev_idx, [every[:half], every[half:]], f,
                                                      name="unshard_ffn_up" + tag)
        out["down"] = _place_own(got["down"], own["down"], dev_idx, name="place_ffn_down" + tag)
        out["conv"] = _side_by_side(got["conv"], own["conv"], dev)
    return out


def _rope_tables(S):
    rows = S // GRID_W
    pairs = ATTN_HD // 4
    row_idx = jnp.repeat(jnp.arange(rows, dtype=F32), GRID_W)
    col_idx = jnp.tile(jnp.arange(GRID_W, dtype=F32), rows)
    inv_freq = ROPE_THETA ** (-jnp.arange(pairs, dtype=F32) / pairs)
    ang = jnp.concatenate([row_idx[:, None] * inv_freq, col_idx[:, None] * inv_freq], axis=-1)
    cos, sin = jnp.cos(ang), jnp.sin(ang)
    return jnp.concatenate([cos, cos], axis=-1), jnp.concatenate([-sin, sin], axis=-1)


def _gate_matrix(w_f, w_b):
    rk = w_f.shape[0]
    top = jnp.concatenate([w_f, jnp.zeros_like(w_f)], axis=1)
    mid = jnp.concatenate([jnp.zeros_like(w_b), w_b], axis=1)
    pad = jnp.zeros((LANES - 2 * rk, 2 * GLA_KD), w_f.dtype)
    return jnp.concatenate([top, mid, pad], axis=0)


def _local_step(x, target, rep, w, dev, idx):
    S, D = x.shape
    depth = rep["norm_mix"].shape[0]
    cs, sn = _rope_tables(S)
    row = lambda a: a.reshape(1, -1)
    ranks_cols = (GLA_PROJ - LANES, LANES)

    groups = [(0, True, False), (0, False, True)] + [(i, True, True) for i in range(1, depth)]
    sent = []
    for g, (i, mixer, ffn) in enumerate(groups):
        names, srcs = _layer_shards(w, i, mixer, ffn)
        lands = [lax.empty((N_DEV,) + a.shape, a.dtype) for a in srcs]
        dep = (sent[-1][1]["token"],) if sent else ()
        sent.append((names, _split_start(_gather_send_plan(len(srcs)), srcs, lands, dep=dep,
                                         name="weights_send_g%d" % g)))

    def arrive(g, after):
        names, started = sent[g]
        own, lands = _split_wait(started, after, name="weights_arrive_g%d" % g)
        return names, own, _split_start(_gather_pass_plan(len(lands)), [], lands, name="weights_pass_g%d" % g)

    def ready(g, passing, after):
        names, own, started = passing
        _, lands = _split_wait(started, after, name="weights_passed_g%d" % g)
        return _layer_weights(names, own, lands, dev, idx["dev"], groups[g][0])

    passing = arrive(0, sent[-1][1]["token"])
    wl = ready(0, passing, passing[2]["token"])
    passing = arrive(1, wl["mix_in"])

    saved = []
    for i in range(depth):
        j = i // 2
        sv = {"x0": x, "w": wl}
        h1 = _rmsnorm_fwd(x, row(rep["norm_mix"][i]), dep=(passing[2]["token"],) if i == 0 else (), name="norm_mix_fwd")
        sv["h1"] = h1
        if i % 2 == 0:
            bias = jnp.concatenate([rep["gla_b_gate_f"][j], rep["gla_b_gate_b"][j]]).reshape(1, -1)
            proj = _mm(h1, wl["mix_in"], name="gla_in_proj")
            logits = _mm(proj, wl["gate"], a_cols=ranks_cols, name="gla_gate_logits")
            of, ob, sf, sb = _gla_fwd(proj, logits, bias, name="gla_fwd")
            y = _gla_gate_fwd(of, ob, proj, row(rep["gla_norm"][j]), name="gla_gate_fwd")
            x = _mm(y, wl["mix_out"], res=x, name="gla_out_proj")
            sv.update(bias=bias, proj=proj, logits=logits, of=of, ob=ob, sf=sf, sb=sb, y=y)
        else:
            proj = _mm(h1, wl["mix_in"], name="attn_qkv_proj")
            qkv = _attn_prep(proj, cs, sn, row(rep["attn_q_norm"][j]), row(rep["attn_k_norm"][j]), name="attn_prep")
            o = _attn_fwd(qkv, name="attn_fwd")
            x = _mm(o, wl["mix_out"], res=x, name="attn_out_proj")
            sv.update(proj=proj, qkv=qkv, o=o)
        sv["x1"] = x
        if i == 0:
            wl.update(ready(1, passing, x))
        dep = ()
        if i + 1 < depth:
            passing = arrive(i + 2, x)
            dep = (passing[2]["token"],)
        h2 = _rmsnorm_fwd(x, row(rep["norm_ffn"][i]), dep=dep, name="norm_ffn_fwd")
        F = wl["down"].shape[0]
        wc, bc = wl["conv"], rep["ffn_b_conv"][i]
        wcv, wcg, bcv, bcg = wc[:, :F], wc[:, F:], row(bc[:F]), row(bc[F:])
        uv = _mm(h2, wl["up_val"], name="ffn_up_val")
        ug = _mm(h2, wl["up_gate"], name="ffn_up_gate")
        act = _conv_act_fwd(uv, ug, wcv, wcg, bcv, bcg, name="ffn_conv_act")
        x = _mm(act, wl["down"], res=x, name="ffn_down")
        sv.update(h2=h2, uv=uv, ug=ug, act=act, wcv=wcv, wcg=wcg, bcv=bcv, bcg=bcg)
        saved.append(sv)
        if i + 1 < depth:
            wl = ready(i + 2, passing, x)

    dx, loss_tile = _loss_head(x, target, name="loss_head")
    loss = loss_tile[0, 0]

    in_sibling_stage, in_chip_stage, reduced = [], [], []

    def advance(group, after):
        tokens = []
        for tag, keys, started in in_chip_stage:
            partial, lands = _split_wait(started, after, name="grads_chips_arrive_" + tag)
            reduced.append((keys, partial, lands))
        in_chip_stage.clear()
        for tag, keys, started in in_sibling_stage:
            stacks, lands = _split_wait(started, after, name="grads_sibling_arrive_" + tag)
            partial = [_add_sibling(s, b, idx["core"], name="grads_add_sibling_%s_%d" % (tag, q))
                       for q, (s, b) in enumerate(zip(stacks, lands))]
            bufs = [lax.empty(p.shape, p.dtype) for p in partial]
            started = _split_start(_reduce_chip_plan(len(partial)), partial, bufs, name="grads_chips_send_" + tag)
            in_chip_stage.append((tag, keys, started))
            tokens.append(started["token"])
        in_sibling_stage.clear()
        if group is not None:
            tag, keys, stacks = group
            bufs = [lax.empty((4,) + s.shape[1:], s.dtype) for s in stacks]
            started = _split_start(_reduce_sibling_plan(len(stacks)), stacks, bufs, name="grads_sibling_send_" + tag)
            in_sibling_stage.append((tag, keys, started))
            tokens.append(started["token"])
        return tuple(tokens)

    def stack_for(name):
        return lax.empty((N_DEV,) + tuple(w[name].shape[1:]), BF16)

    gl = {k: [None] * depth for k in ("norm_mix", "norm_ffn", "ffn_w_conv", "ffn_b_conv")}
    gm = {k: [None] * (depth // 2) for k in ("gla_w_gate_up_f", "gla_b_gate_f", "gla_w_gate_up_b", "gla_b_gate_b",
                                             "gla_norm", "attn_q_norm", "attn_k_norm")}
    rk = GLA_GATE_RANK
    dep = ()
    for i in reversed(range(depth)):
        j = i // 2
        sv = saved[i]
        wl = sv["w"]
        dact = _mm(dx, wl["down"], tb=True, dep=dep, name="ffn_down_dgrad")
        g_down = _wgrad(sv["act"], dx, stack_for("ffn_w_down"), shard="rows", name="ffn_down_wgrad")
        dcv, dcg, dwv, dwg, dbv, dbg = _conv_act_bwd(sv["uv"], sv["ug"], sv["wcv"], sv["wcg"], sv["bcv"], sv["bcg"],
                                                     dact, name="ffn_conv_act_bwd")
        gl["ffn_w_conv"][i] = jnp.concatenate([dwv, dwg], axis=1)
        gl["ffn_b_conv"][i] = jnp.concatenate([dbv, dbg], axis=1)[0]
        mid = advance(None, dcv)
        duv = _conv_t(dcv, sv["wcv"], name="ffn_conv_t")
        dug = _conv_t(dcg, sv["wcg"], name="ffn_conv_t")
        dh2 = _mm(duv, wl["up_val"], tb=True, dep=mid, name="ffn_up_dgrad_val")
        dh2 = _mm(dug, wl["up_gate"], tb=True, res=dh2, name="ffn_up_dgrad_gate")
        g_up = _wgrad(sv["h2"], duv, stack_for("ffn_w_up"), shard="cols", group=0, name="ffn_up_wgrad_val")
        g_up = _wgrad(sv["h2"], dug, g_up, shard="cols", group=1, name="ffn_up_wgrad_gate")
        dx, dn = _rmsnorm_bwd(sv["x1"], row(rep["norm_ffn"][i]), dh2, dx, name="norm_ffn_bwd")
        gl["norm_ffn"][i] = dn[0]
        dep = advance(("ffn_l%d" % i, [("ffn_w_up", i), ("ffn_w_down", i)], [g_up, g_down]), dx)
        if i % 2 == 0:
            dy = _mm(dx, wl["mix_out"], tb=True, dep=dep, name="gla_out_dgrad")
            g_out = _wgrad(sv["y"], dx, stack_for("gla_w_out"), shard="rows", name="gla_out_wgrad")
            do, dg, dgn = _gla_gate_bwd(sv["of"], sv["ob"], sv["proj"], row(rep["gla_norm"][j]), dy, name="gla_gate_bwd")
            gm["gla_norm"][j] = dgn[0]
            dqkv_f, dlg_f, dqkv_b, dlg_b, dbias = _gla_bwd(sv["proj"], sv["logits"], sv["bias"], sv["sf"], sv["sb"], do,
                                                           name="gla_bwd")
            gm["gla_b_gate_f"][j] = dbias[0, :GLA_KD]
            gm["gla_b_gate_b"][j] = dbias[0, GLA_KD:]
            mid = advance(None, dqkv_f)
            dlogits = jnp.concatenate([dlg_f, dlg_b], axis=1)
            dr = _mm(dlogits, wl["gate"], tb=True, dep=mid, name="gla_gate_dgrad")
            dwg_full = _mm(sv["proj"], dlogits, ta=True, a_cols=ranks_cols, name="gla_gate_wgrad")
            gm["gla_w_gate_up_f"][j] = dwg_full[:rk, :GLA_KD]
            gm["gla_w_gate_up_b"][j] = dwg_full[rk:2 * rk, GLA_KD:]
            dproj = _gla_combine(dqkv_f, dqkv_b, dg, dr, name="gla_combine")
            dh1 = _mm(dproj, wl["mix_in"], tb=True, name="gla_in_dgrad")
            g_in = _wgrad(sv["h1"], dproj, stack_for("gla_w_in"), shard="cols", name="gla_in_wgrad")
            keys = [("gla_w_in", j), ("gla_w_out", j)]
        else:
            do = _mm(dx, wl["mix_out"], tb=True, out_dtype=BF16, dep=dep, name="attn_out_dgrad")
            g_out = _wgrad(sv["o"], dx, stack_for("attn_w_out"), shard="rows", name="attn_out_wgrad")
            dq, dk, dv = _attn_bwd(sv["qkv"], do, name="attn_bwd")
            mid = advance(None, dq)
            dproj, dqn, dkn = _attn_prep_bwd(sv["proj"], dq, dk, dv, cs, sn, row(rep["attn_q_norm"][j]),
                                             row(rep["attn_k_norm"][j]), name="attn_prep_bwd")
            gm["attn_q_norm"][j] = dqn[0]
            gm["attn_k_norm"][j] = dkn[0]
            dh1 = _mm(dproj, wl["mix_in"], tb=True, dep=mid, name="attn_qkv_dgrad")
            g_in = _wgrad(sv["h1"], dproj, stack_for("attn_w_qkv"), shard="cols", name="attn_qkv_wgrad")
            keys = [("attn_w_qkv", j), ("attn_w_out", j)]
        dx, dn = _rmsnorm_bwd(sv["x0"], row(rep["norm_mix"][i]), dh1, dx, name="norm_mix_bwd")
        gl["norm_mix"][i] = dn[0]
        dep = advance(("mix_l%d" % i, keys, [g_in, g_out]), dx)

    small = {k: jnp.stack(v) for k, v in {**gl, **gm}.items()}
    return loss, dx, reduced, small, advance


def kernel(x, norm_mix, norm_ffn, gla_w_in, gla_w_gate_up_f, gla_b_gate_f, gla_w_gate_up_b, gla_b_gate_b, gla_norm, gla_w_out, attn_w_qkv, attn_q_norm, attn_k_norm, attn_w_out, ffn_w_up, ffn_w_conv, ffn_b_conv, ffn_w_down, loss_target, m_norm_mix, m_norm_ffn, m_gla_w_in, m_gla_w_gate_up_f, m_gla_b_gate_f, m_gla_w_gate_up_b, m_gla_b_gate_b, m_gla_norm, m_gla_w_out, m_attn_w_qkv, m_attn_q_norm, m_attn_k_norm, m_attn_w_out, m_ffn_w_up, m_ffn_w_conv, m_ffn_b_conv, m_ffn_w_down, v_norm_mix, v_norm_ffn, v_gla_w_in, v_gla_w_gate_up_f, v_gla_b_gate_f, v_gla_w_gate_up_b, v_gla_b_gate_b, v_gla_norm, v_gla_w_out, v_attn_w_qkv, v_attn_q_norm, v_attn_k_norm, v_attn_w_out, v_ffn_w_up, v_ffn_w_conv, v_ffn_b_conv, v_ffn_w_down):
    given = dict(locals())
    w = {n: given[n] for n in WEIGHTS}
    m = {n: given["m_" + n] for n in WEIGHTS}
    v = {n: given["v_" + n] for n in WEIGHTS}
    shards = {n: w[n] for n in BIG + SMALL_SHARDED}
    rep = {n: w[n] for n in REPLICATED}

    x_pos, y_pos, c_pos = _place()
    dev = 4 * x_pos + 2 * y_pos + c_pos
    as_operand = lambda s: jnp.asarray(s, jnp.int32).reshape(1)
    idx = dict(dev=as_operand(dev), chip=as_operand(2 * x_pos + y_pos), core=as_operand(c_pos))

    loss_local, grad_x, reduced, small, advance = _local_step(x[0], loss_target[0], rep, shards, dev, idx)
    loss = lax.psum(loss_local, ("x", "y", "c"))

    big = {n: [lax.empty(w[n].shape, F32) for _ in range(4)] for n in BIG}

    def update_reduced():
        for keys, own, parts in reduced:
            for (n, layer), p_own, p_others in zip(keys, own, parts):
                big[n] = _adamw_layer(w[n], m[n], v[n], p_own, p_others, idx["chip"], big[n], layer,
                                      name="adamw_%s_l%d" % (n, layer))
        reduced.clear()

    advance(None, grad_x)
    update_reduced()

    rest = REPLICATED + SMALL_SHARDED
    flat = _rows(jnp.concatenate([small[n].reshape(-1) for n in rest]), SUBLANES)
    total = _sum_slots(_all_gather(flat, name="small_grads_all_gather"), name="small_grads_sum").reshape(-1)
    advance(None, [total, *big["ffn_w_up"]])
    update_reduced()
    g, off = {}, 0
    for n in rest:
        whole = total[off:off + small[n].size].reshape(small[n].shape)
        off += small[n].size
        width = w[n].shape[-1]
        g[n] = whole if n in REPLICATED else lax.dynamic_slice_in_dim(whole, dev * width, width, axis=whole.ndim - 1)

    delta, new_m, new_v = {}, {}, {}
    for n in WEIGHTS:
        if n in BIG:
            g[n], delta[n], new_m[n], new_v[n] = big[n]
        else:
            shape = w[n].shape
            two_d = (-1, shape[-1])
            d2, m2, v2 = _adamw(w[n].reshape(two_d), g[n].reshape(two_d), m[n].reshape(two_d), v[n].reshape(two_d),
                                name="adamw_" + n)
            delta[n], new_m[n], new_v[n] = d2.reshape(shape), m2.reshape(shape), v2.reshape(shape)

    return (loss, grad_x[None], *[g[n] for n in WEIGHTS], *[delta[n] for n in WEIGHTS],
            *[new_m[n] for n in WEIGHTS], *[new_v[n] for n in WEIGHTS])
```

```python
import math

import jax
import jax.numpy as jnp
from jax import lax
from jax.experimental import pallas as pl
from jax.experimental.pallas import tpu as pltpu

F32 = jnp.float32
BF16 = jnp.bfloat16
MESH = pl.DeviceIdType.MESH

N_DEV = 8
LANES = 128
SUBLANES = 8
VMEM_LIMIT = 56 * 1024 * 1024

NORM_EPS = 1e-6
GRID_W = 64
ROPE_THETA = 10000.0
GLA_HEADS = 4
GLA_DK = 128
GLA_DV = 256
GLA_CHUNK = 64
GLA_GATE_RANK = 16
GLA_GATE_NORMALIZER = 16.0
ATTN_HD = 128
ATTN_Q_HEADS = 8
ATTN_KV_HEADS = 2
ATTN_GROUP = ATTN_Q_HEADS // ATTN_KV_HEADS

ADAM_LR = 0.001
ADAM_B1 = 0.9
ADAM_B2 = 0.999
ADAM_EPS = 1e-08
ADAM_WD = 0.01
ADAM_STEP = 10


def _tile(n, target, align=LANES):
    if n <= target:
        return n
    t = (target // align) * align
    while t >= align:
        if n % t == 0:
            return t
        t -= align
    return n


def _params(*sem):
    return pltpu.CompilerParams(dimension_semantics=sem, vmem_limit_bytes=VMEM_LIMIT)


def _dot(a, b):
    return lax.dot_general(a, b, (((1,), (0,)), ((), ())), preferred_element_type=F32)


def _dot_nt(a, b):
    return lax.dot_general(a, b, (((1,), (1,)), ((), ())), preferred_element_type=F32)


def _dot_tn(a, b):
    return lax.dot_general(a, b, (((0,), (0,)), ((), ())), preferred_element_type=F32)


def _sigmoid(x):
    return 1.0 / (1.0 + jnp.exp(-x))


def _log_sigmoid(x):
    return jnp.minimum(x, 0.0) - jnp.log(1.0 + jnp.exp(-jnp.abs(x)))


def _colsum(x):
    return jnp.sum(x, axis=0, keepdims=True)


ANY = pl.BlockSpec(memory_space=pl.ANY)


def _mm(a, b, *, ta=False, tb=False, res=None, out_dtype=F32, layer=None, a_cols=None, dep=(), name):
    if tb:
        N, K = b.shape[-2:]
    else:
        K, N = b.shape[-2:]
    a_rows, a_width = a.shape
    a_off = 0
    if a_cols is not None:
        a_off, a_width = a_cols
    if ta:
        M = a_width
        assert a_rows == K, (a.shape, b.shape, ta, tb)
    else:
        M = a_rows
        assert a_width == K, (a.shape, b.shape, ta, tb)
    tm = _tile(M, 1408) if ta else _tile(M, 1024, 16)
    tn = _tile(N, 1408)
    tk = _tile(K, 512, 16) if ta else _tile(K, 1408)
    nk = K // tk
    dims = (((0 if ta else 1,), (1 if tb else 0,)), ((), ()))

    n_in = 2 + (res is not None) + len(dep)

    def body(*refs):
        a_ref, b_ref = refs[:2]
        r_ref = refs[2] if res is not None else None
        o_ref = refs[n_in]
        scr = refs[n_in + 1:]
        part = lax.dot_general(a_ref[...].astype(BF16), b_ref[...].astype(BF16), dims, preferred_element_type=F32)

        def finish(acc):
            if r_ref is not None:
                acc = acc + r_ref[...]
            o_ref[...] = acc.astype(out_dtype)

        if nk == 1:
            finish(part)
        else:
            acc_ref = scr[0]
            k = pl.program_id(2)

            @pl.when(k == 0)
            def _():
                acc_ref[...] = part

            @pl.when(k > 0)
            def _():
                acc_ref[...] += part

            @pl.when(k == nk - 1)
            def _():
                finish(acc_ref[...])

    a_blk = a_off // (tm if ta else tk)
    assert a_off % (tm if ta else tk) == 0
    a_spec = (pl.BlockSpec((tk, tm), lambda i, j, k: (k, a_blk + i)) if ta
              else pl.BlockSpec((tm, tk), lambda i, j, k: (i, a_blk + k)))
    if layer is None:
        b_spec = pl.BlockSpec((tn, tk), lambda i, j, k: (j, k)) if tb else pl.BlockSpec((tk, tn), lambda i, j, k: (k, j))
    else:
        b_spec = (pl.BlockSpec((None, tn, tk), lambda i, j, k: (layer, j, k)) if tb
                  else pl.BlockSpec((None, tk, tn), lambda i, j, k: (layer, k, j)))
    o_spec = pl.BlockSpec((tm, tn), lambda i, j, k: (i, j))
    in_specs = [a_spec, b_spec] + ([o_spec] if res is not None else []) + [ANY] * len(dep)
    args = (a, b) + ((res,) if res is not None else ()) + tuple(dep)
    return pl.pallas_call(
        body, name=name, grid=(M // tm, N // tn, nk),
        in_specs=in_specs, out_specs=o_spec,
        out_shape=jax.ShapeDtypeStruct((M, N), out_dtype),
        scratch_shapes=[pltpu.VMEM((tm, tn), F32)] if nk > 1 else [],
        compiler_params=_params("parallel", "parallel", "arbitrary"),
    )(*args)


def _wgrad(a, b, stack, *, shard, group=0, name):
    S, N = b.shape
    M = a.shape[1]
    As, Bs = stack.shape[-2:]
    tk = _tile(S, 1024, 16)
    nk = S // tk
    if shard == "cols":
        n = N // Bs
        tm = _tile(M, 512)
        tn = N
        grid = (M // tm, 1, nk)
        o_spec = pl.BlockSpec((n, tm, Bs), lambda i, j, k: (group, i, 0))
    else:
        per = As * LANES // math.gcd(As, LANES)
        tm = M if M <= 1408 else _tile(M, 1408, per)
        n = tm // As
        tn = _tile(N, 1024)
        grid = (M // tm, N // tn, nk)
        o_spec = pl.BlockSpec((n, As, tn), lambda i, j, k: (i, 0, j))

    def body(a_ref, b_ref, _, o_ref, acc_ref):
        k = pl.program_id(2)
        part = _dot_tn(a_ref[...].astype(BF16), b_ref[...].astype(BF16))

        @pl.when(k == 0)
        def _():
            acc_ref[...] = part

        @pl.when(k > 0)
        def _():
            acc_ref[...] += part

        @pl.when(k == nk - 1)
        def _():
            for q in range(n):
                if shard == "cols":
                    o_ref[q] = acc_ref[:, q * Bs:(q + 1) * Bs].astype(stack.dtype)
                else:
                    o_ref[q] = acc_ref[q * As:(q + 1) * As, :].astype(stack.dtype)

    return pl.pallas_call(
        body, name=name, grid=grid,
        in_specs=[pl.BlockSpec((tk, tm), lambda i, j, k: (k, i)), pl.BlockSpec((tk, tn), lambda i, j, k: (k, j)),
                  pl.BlockSpec(memory_space=pl.ANY)],
        out_specs=o_spec,
        out_shape=jax.ShapeDtypeStruct(stack.shape, stack.dtype),
        input_output_aliases={2: 0},
        scratch_shapes=[pltpu.VMEM((tm, tn), F32)],
        compiler_params=_params("parallel", "parallel", "arbitrary"),
    )(a, b, stack)


def _rmsnorm_fwd(x, w, *, dep=(), name):
    S, D = x.shape
    ts = _tile(S, 512, 16)

    def body(x_ref, w_ref, *rest):
        o_ref = rest[-1]
        xv = x_ref[...]
        r = lax.rsqrt(jnp.mean(xv * xv, axis=-1, keepdims=True) + NORM_EPS)
        o_ref[...] = (xv * r * w_ref[...]).astype(BF16)

    return pl.pallas_call(
        body, name=name, grid=(S // ts,),
        in_specs=[pl.BlockSpec((ts, D), lambda i: (i, 0)), pl.BlockSpec((1, D), lambda i: (0, 0))] + [ANY] * len(dep),
        out_specs=pl.BlockSpec((ts, D), lambda i: (i, 0)),
        out_shape=jax.ShapeDtypeStruct((S, D), BF16),
        compiler_params=_params("parallel"),
    )(x, w, *dep)


def _rmsnorm_bwd(x, w, dh, dres, *, name):
    S, D = x.shape
    ts = _tile(S, 512, 16)
    n = S // ts

    def body(x_ref, w_ref, dh_ref, dr_ref, dx_ref, dw_ref):
        i = pl.program_id(0)
        xv = x_ref[...]
        r = lax.rsqrt(jnp.mean(xv * xv, axis=-1, keepdims=True) + NORM_EPS)
        xh = xv * r
        d = dh_ref[...]
        g = d * w_ref[...]
        dx_ref[...] = dr_ref[...] + r * (g - xh * jnp.mean(g * xh, axis=-1, keepdims=True))
        part = _colsum(d * xh)

        @pl.when(i == 0)
        def _():
            dw_ref[...] = part

        @pl.when(i > 0)
        def _():
            dw_ref[...] += part

    row = pl.BlockSpec((ts, D), lambda i: (i, 0))
    vec = pl.BlockSpec((1, D), lambda i: (0, 0))
    return pl.pallas_call(
        body, name=name, grid=(n,),
        in_specs=[row, vec, row, row], out_specs=[row, vec],
        out_shape=[jax.ShapeDtypeStruct((S, D), F32), jax.ShapeDtypeStruct((1, D), F32)],
        compiler_params=_params("arbitrary"),
    )(x, w, dh, dres)


def _halo_specs(S, ts, tf, row_axis):
    g = ts // SUBLANES
    last = S // SUBLANES - 1
    col_axis = 1 - row_axis
    main = pl.BlockSpec((ts, tf), lambda *ij: (ij[row_axis], ij[col_axis]))
    prev = pl.BlockSpec((SUBLANES, tf), lambda *ij: (jnp.maximum(ij[row_axis] * g - 1, 0), ij[col_axis]))
    nxt = pl.BlockSpec((SUBLANES, tf), lambda *ij: (jnp.minimum((ij[row_axis] + 1) * g, last), ij[col_axis]))
    return [main, prev, nxt]


def _shifted(u, prev_ref, next_ref, i, n):
    ts = u.shape[0]
    rid = lax.broadcasted_iota(jnp.int32, u.shape, 0)
    before = jnp.where(i > 0, prev_ref[SUBLANES - 1:SUBLANES, :], 0.0)
    after = jnp.where(i < n - 1, next_ref[0:1, :], 0.0)
    um1 = jnp.where(rid == 0, before, pltpu.roll(u, 1, 0))
    up1 = jnp.where(rid == ts - 1, after, pltpu.roll(u, ts - 1, 0))
    return um1, up1


def _conv3(u, prev_ref, next_ref, w_ref, i, n):
    um1, up1 = _shifted(u, prev_ref, next_ref, i, n)
    return w_ref[0:1, :] * um1 + w_ref[1:2, :] * u + w_ref[2:3, :] * up1


def _conv_act_fwd(uv, ug, wv, wg, bv, bg, *, name):
    S, F = uv.shape
    ts = _tile(S, 512, 16)
    tf = _tile(F, 1408)
    n = S // ts

    def body(v_ref, vp_ref, vn_ref, g_ref, gp_ref, gn_ref, wv_ref, wg_ref, bv_ref, bg_ref, o_ref):
        i = pl.program_id(0)
        val = _conv3(v_ref[...], vp_ref, vn_ref, wv_ref, i, n) + bv_ref[...]
        gate = _conv3(g_ref[...], gp_ref, gn_ref, wg_ref, i, n) + bg_ref[...]
        o_ref[...] = (gate * _sigmoid(gate) * val).astype(BF16)

    halo = _halo_specs(S, ts, tf, 0)
    w3 = pl.BlockSpec((3, tf), lambda i, j: (0, j))
    b1 = pl.BlockSpec((1, tf), lambda i, j: (0, j))
    return pl.pallas_call(
        body, name=name, grid=(n, F // tf),
        in_specs=halo + halo + [w3, w3, b1, b1],
        out_specs=pl.BlockSpec((ts, tf), lambda i, j: (i, j)),
        out_shape=jax.ShapeDtypeStruct((S, F), BF16),
        compiler_params=_params("parallel", "parallel"),
    )(uv, uv, uv, ug, ug, ug, wv, wg, bv, bg)


def _conv_act_bwd(uv, ug, wv, wg, bv, bg, dact, *, name):
    S, F = uv.shape
    ts = _tile(S, 512, 16)
    tf = _tile(F, 1408)
    n = S // ts

    def body(v_ref, vp_ref, vn_ref, g_ref, gp_ref, gn_ref, wv_ref, wg_ref, bv_ref, bg_ref, da_ref,
             dv_ref, dg_ref, dwv_ref, dwg_ref, dbv_ref, dbg_ref):
        i = pl.program_id(1)
        uvv, ugv = v_ref[...], g_ref[...]
        vm1, vp1 = _shifted(uvv, vp_ref, vn_ref, i, n)
        gm1, gp1 = _shifted(ugv, gp_ref, gn_ref, i, n)
        val = wv_ref[0:1, :] * vm1 + wv_ref[1:2, :] * uvv + wv_ref[2:3, :] * vp1 + bv_ref[...]
        gate = wg_ref[0:1, :] * gm1 + wg_ref[1:2, :] * ugv + wg_ref[2:3, :] * gp1 + bg_ref[...]
        sg = _sigmoid(gate)
        da = da_ref[...]
        dval = da * (gate * sg)
        dgate = da * val * (sg * (1.0 + gate * (1.0 - sg)))
        dv_ref[...] = dval
        dg_ref[...] = dgate
        sums = [(dwv_ref, 0, vm1 * dval), (dwv_ref, 1, uvv * dval), (dwv_ref, 2, vp1 * dval),
                (dwg_ref, 0, gm1 * dgate), (dwg_ref, 1, ugv * dgate), (dwg_ref, 2, gp1 * dgate),
                (dbv_ref, 0, dval), (dbg_ref, 0, dgate)]
        parts = [(ref, r, _colsum(t)) for ref, r, t in sums]

        @pl.when(i == 0)
        def _():
            for ref, r, part in parts:
                ref[r:r + 1, :] = part

        @pl.when(i > 0)
        def _():
            for ref, r, part in parts:
                ref[r:r + 1, :] += part

    halo = _halo_specs(S, ts, tf, 1)
    w3 = pl.BlockSpec((3, tf), lambda j, i: (0, j))
    b1 = pl.BlockSpec((1, tf), lambda j, i: (0, j))
    blk = pl.BlockSpec((ts, tf), lambda j, i: (i, j))
    return pl.pallas_call(
        body, name=name, grid=(F // tf, n),
        in_specs=halo + halo + [w3, w3, b1, b1, blk],
        out_specs=[blk, blk, w3, w3, b1, b1],
        out_shape=[jax.ShapeDtypeStruct((S, F), F32), jax.ShapeDtypeStruct((S, F), F32),
                   jax.ShapeDtypeStruct((3, F), F32), jax.ShapeDtypeStruct((3, F), F32),
                   jax.ShapeDtypeStruct((1, F), F32), jax.ShapeDtypeStruct((1, F), F32)],
        compiler_params=_params("parallel", "arbitrary"),
    )(uv, uv, uv, ug, ug, ug, wv, wg, bv, bg, dact)


def _conv_t(duc, w, *, name):
    S, F = duc.shape
    ts = _tile(S, 512, 16)
    tf = _tile(F, 1408)
    n = S // ts

    def body(d_ref, dp_ref, dn_ref, w_ref, o_ref):
        i = pl.program_id(0)
        d = d_ref[...]
        dm1, dp1 = _shifted(d, dp_ref, dn_ref, i, n)
        o_ref[...] = (w_ref[0:1, :] * dp1 + w_ref[1:2, :] * d + w_ref[2:3, :] * dm1).astype(BF16)

    return pl.pallas_call(
        body, name=name, grid=(n, F // tf),
        in_specs=_halo_specs(S, ts, tf, 0) + [pl.BlockSpec((3, tf), lambda i, j: (0, j))],
        out_specs=pl.BlockSpec((ts, tf), lambda i, j: (i, j)),
        out_shape=jax.ShapeDtypeStruct((S, F), BF16),
        compiler_params=_params("parallel", "parallel"),
    )(duc, duc, duc, w)


N_QK = ATTN_Q_HEADS + ATTN_KV_HEADS
QKV_DIM = (ATTN_Q_HEADS + 2 * ATTN_KV_HEADS) * ATTN_HD


def _head(hd):
    return slice(hd * ATTN_HD, (hd + 1) * ATTN_HD)


def _attn_prep(proj, cs, sn, qn, kn, *, name):
    S = proj.shape[0]
    ts = _tile(S, 256, 16)

    def body(p_ref, c_ref, s_ref, qn_ref, kn_ref, o_ref):
        c, s = c_ref[...], s_ref[...]
        for hd in range(N_QK):
            xv = p_ref[:, _head(hd)]
            w = qn_ref[...] if hd < ATTN_Q_HEADS else kn_ref[...]
            r = lax.rsqrt(jnp.mean(xv * xv, axis=-1, keepdims=True) + NORM_EPS)
            nrm = xv * r * w
            o_ref[:, _head(hd)] = (nrm * c + pltpu.roll(nrm, ATTN_HD // 2, 1) * s).astype(BF16)
        o_ref[:, N_QK * ATTN_HD:] = p_ref[:, N_QK * ATTN_HD:].astype(BF16)

    row = pl.BlockSpec((ts, QKV_DIM), lambda i: (i, 0))
    rot = pl.BlockSpec((ts, ATTN_HD), lambda i: (i, 0))
    vec = pl.BlockSpec((1, ATTN_HD), lambda i: (0, 0))
    return pl.pallas_call(
        body, name=name, grid=(S // ts,),
        in_specs=[row, rot, rot, vec, vec], out_specs=row,
        out_shape=jax.ShapeDtypeStruct((S, QKV_DIM), BF16),
        compiler_params=_params("parallel"),
    )(proj, cs, sn, qn, kn)


def _attn_prep_bwd(proj, dq, dk, dv, cs, sn, qn, kn, *, name):
    S = proj.shape[0]
    ts = _tile(S, 256, 16)
    nq = ATTN_Q_HEADS * ATTN_HD
    nkv = ATTN_KV_HEADS * ATTN_HD

    def body(p_ref, dq_ref, dk_ref, dv_ref, c_ref, s_ref, qn_ref, kn_ref, o_ref, dqn_ref, dkn_ref):
        i = pl.program_id(0)
        c, s = c_ref[...], s_ref[...]
        acc = [jnp.zeros((1, ATTN_HD), F32), jnp.zeros((1, ATTN_HD), F32)]
        for hd in range(N_QK):
            is_k = hd >= ATTN_Q_HEADS
            xv = p_ref[:, _head(hd)]
            w = kn_ref[...] if is_k else qn_ref[...]
            r = lax.rsqrt(jnp.mean(xv * xv, axis=-1, keepdims=True) + NORM_EPS)
            xh = xv * r
            dout = dk_ref[:, _head(hd - ATTN_Q_HEADS)] if is_k else dq_ref[:, _head(hd)]
            dn = dout * c + pltpu.roll(dout * s, ATTN_HD // 2, 1)
            acc[int(is_k)] = acc[int(is_k)] + _colsum(dn * xh)
            g = dn * w
            o_ref[:, _head(hd)] = (r * (g - xh * jnp.mean(g * xh, axis=-1, keepdims=True))).astype(BF16)
        o_ref[:, N_QK * ATTN_HD:] = dv_ref[...].astype(BF16)

        @pl.when(i == 0)
        def _():
            dqn_ref[...] = acc[0]
            dkn_ref[...] = acc[1]

        @pl.when(i > 0)
        def _():
            dqn_ref[...] += acc[0]
            dkn_ref[...] += acc[1]

    row = pl.BlockSpec((ts, QKV_DIM), lambda i: (i, 0))
    rot = pl.BlockSpec((ts, ATTN_HD), lambda i: (i, 0))
    vec = pl.BlockSpec((1, ATTN_HD), lambda i: (0, 0))
    return pl.pallas_call(
        body, name=name, grid=(S // ts,),
        in_specs=[row, pl.BlockSpec((ts, nq), lambda i: (i, 0)), pl.BlockSpec((ts, nkv), lambda i: (i, 0)),
                  pl.BlockSpec((ts, nkv), lambda i: (i, 0)), rot, rot, vec, vec],
        out_specs=[row, vec, vec],
        out_shape=[jax.ShapeDtypeStruct((S, QKV_DIM), BF16), jax.ShapeDtypeStruct((1, ATTN_HD), F32),
                   jax.ShapeDtypeStruct((1, ATTN_HD), F32)],
        compiler_params=_params("arbitrary"),
    )(proj, dq, dk, dv, cs, sn, qn, kn)


ATTN_SCALE = ATTN_HD ** -0.5


def _softmax_of(s):
    e = jnp.exp2((s - jnp.max(s, axis=-1, keepdims=True)) * (ATTN_SCALE * math.log2(math.e)))
    return e, 1.0 / jnp.sum(e, axis=-1, keepdims=True)


def _softmax_parts(q, k):
    return _softmax_of(_dot_nt(q, k))


def _attn_fwd(qkv, *, name):
    S = qkv.shape[0]
    tq = _tile(S, 512, 16)
    sub = _tile(tq, 256, 16)

    def body(q_ref, k_ref, v_ref, o_ref):
        k, v = k_ref[...], v_ref[...]
        for r in range(tq // sub):
            rows = slice(r * sub, (r + 1) * sub)
            e, rl = _softmax_parts(q_ref[rows, :], k)
            o_ref[rows, :] = (_dot(e.astype(BF16), v) * rl).astype(BF16)

    return pl.pallas_call(
        body, name=name, grid=(ATTN_Q_HEADS, S // tq),
        in_specs=[pl.BlockSpec((tq, ATTN_HD), lambda h, i: (i, h)),
                  pl.BlockSpec((S, ATTN_HD), lambda h, i: (0, ATTN_Q_HEADS + h // ATTN_GROUP)),
                  pl.BlockSpec((S, ATTN_HD), lambda h, i: (0, N_QK + h // ATTN_GROUP))],
        out_specs=pl.BlockSpec((tq, ATTN_HD), lambda h, i: (i, h)),
        out_shape=jax.ShapeDtypeStruct((S, ATTN_Q_HEADS * ATTN_HD), BF16),
        compiler_params=_params("parallel", "parallel"),
    )(qkv, qkv, qkv)


def _attn_bwd(qkv, do, *, name):
    S = qkv.shape[0]
    tq = _tile(S, 256, 16)
    n = S // tq

    def body(q_ref, do_ref, qn_ref, don_ref, k_ref, v_ref, dq_ref, dk_ref, dv_ref, s_a, dp_a, s_b, dp_b):
        i = pl.program_id(2)

        @pl.when(i == 0)
        def _():
            s_a[...] = _dot_nt(q_ref[...], k_ref[...])
            dp_a[...] = _dot_nt(do_ref[...], v_ref[...])

            @pl.when(pl.program_id(1) == 0)
            def _():
                dk_ref[...] = jnp.zeros_like(dk_ref)
                dv_ref[...] = jnp.zeros_like(dv_ref)

        def step(s_cur, dp_cur, s_next, dp_next):
            k, v = k_ref[...], v_ref[...]
            q, dov = q_ref[...], do_ref[...]
            s_next[...] = _dot_nt(qn_ref[...], k)
            dp_next[...] = _dot_nt(don_ref[...], v)
            e, rl = _softmax_of(s_cur[...])
            dp = dp_cur[...]
            delta = jnp.sum(e * dp, axis=-1, keepdims=True) * rl
            dsb = (e * (dp - delta) * (rl * ATTN_SCALE)).astype(BF16)
            dq_ref[...] = _dot(dsb, k)
            dk_ref[...] += _dot_tn(dsb, q)
            dv_ref[...] += _dot_tn(e.astype(BF16), (dov.astype(F32) * rl).astype(BF16))

        @pl.when(i % 2 == 0)
        def _():
            step(s_a, dp_a, s_b, dp_b)

        @pl.when(i % 2 == 1)
        def _():
            step(s_b, dp_b, s_a, dp_a)

    qblk = pl.BlockSpec((tq, ATTN_HD), lambda kv, g, i: (i, kv * ATTN_GROUP + g))
    qnext = pl.BlockSpec((tq, ATTN_HD), lambda kv, g, i: (jnp.minimum(i + 1, n - 1), kv * ATTN_GROUP + g))
    kvacc = pl.BlockSpec((S, ATTN_HD), lambda kv, g, i: (0, kv))
    return pl.pallas_call(
        body, name=name, grid=(ATTN_KV_HEADS, ATTN_GROUP, n),
        in_specs=[qblk, qblk, qnext, qnext,
                  pl.BlockSpec((S, ATTN_HD), lambda kv, g, i: (0, ATTN_Q_HEADS + kv)),
                  pl.BlockSpec((S, ATTN_HD), lambda kv, g, i: (0, N_QK + kv))],
        out_specs=[qblk, kvacc, kvacc],
        out_shape=[jax.ShapeDtypeStruct((S, ATTN_Q_HEADS * ATTN_HD), F32),
                   jax.ShapeDtypeStruct((S, ATTN_KV_HEADS * ATTN_HD), F32),
                   jax.ShapeDtypeStruct((S, ATTN_KV_HEADS * ATTN_HD), F32)],
        scratch_shapes=[pltpu.VMEM((tq, S), F32) for _ in range(4)],
        compiler_params=_params("parallel", "arbitrary", "arbitrary"),
    )(qkv, do, qkv, do, qkv, qkv)


GLA_KD = GLA_HEADS * GLA_DK
GLA_VD = GLA_HEADS * GLA_DV
GLA_PROJ = 2 * GLA_KD + 2 * GLA_VD + LANES
GLA_SCALE = GLA_DK ** -0.5


def _split3(x):
    hi = x.astype(BF16)
    r1 = x - hi.astype(F32)
    mid = r1.astype(BF16)
    lo = (r1 - mid.astype(F32)).astype(BF16)
    return hi, mid, lo


def _cumdot(t, x):
    hi, mid, lo = _split3(x)
    return _dot(t, hi) + _dot(t, mid) + _dot(t, lo)


def _gla_masks(d):
    c = GLA_CHUNK
    row = lax.broadcasted_iota(jnp.int32, (c, c), 0)
    col = lax.broadcasted_iota(jnp.int32, (c, c), 1)
    lower, upper = col <= row, col >= row
    if d == 0:
        return lower.astype(BF16), upper.astype(BF16), lower
    return upper.astype(BF16), lower.astype(BF16), col > row


def _gla_decay(lg, bias, cum, d):
    xl = lg + bias
    la = _log_sigmoid(xl) * (1.0 / GLA_GATE_NORMALIZER)
    b = _cumdot(cum, la)
    b_end = b[GLA_CHUNK - 1:GLA_CHUNK, :] if d == 0 else b[0:1, :]
    return xl, b, b_end


def _gla_specs(S, n):
    c = GLA_CHUNK
    up = lambda i: i
    down = lambda i: n - 1 - i
    def specs(order):
        return dict(
            q=pl.BlockSpec((c, GLA_KD), lambda i: (order(i), 0)),
            k=pl.BlockSpec((c, GLA_KD), lambda i: (order(i), 1)),
            v=pl.BlockSpec((c, GLA_VD), lambda i: (order(i), 1)),
            st=pl.BlockSpec((1, GLA_HEADS, GLA_DV, GLA_DK), lambda i: (order(i), 0, 0, 0)),
            wide=pl.BlockSpec((c, GLA_VD), lambda i: (order(i), 0)),
            qkv=pl.BlockSpec((c, 2 * GLA_KD + GLA_VD), lambda i: (order(i), 0)),
        )
    return specs(up), specs(down), up, down


def _gla_fwd(proj, logits, bias, *, name):
    S = proj.shape[0]
    c = GLA_CHUNK
    n = S // c
    su, sd, up, down = _gla_specs(S, n)

    def body(qf, kf, vf, lf, qb, kb, vb, lb, bias_ref, of, ob, sf, sb, st):
        @pl.when(pl.program_id(0) == 0)
        def _():
            st[...] = jnp.zeros_like(st)

        for d, (q_r, k_r, v_r, l_r, o_r, s_r) in enumerate(((qf, kf, vf, lf, of, sf), (qb, kb, vb, lb, ob, sb))):
            cum, _, mask = _gla_masks(d)
            _, b, b_end = _gla_decay(l_r[...], bias_ref[:, d * GLA_KD:(d + 1) * GLA_KD], cum, d)
            dend = jnp.exp(b_end)
            k = k_r[...]
            qd = (q_r[...] * GLA_SCALE * jnp.exp(b)).astype(BF16)
            ki = (k * jnp.exp(-b)).astype(BF16)
            ke = (k * jnp.exp(b_end - b)).astype(BF16)
            for h in range(GLA_HEADS):
                ks = slice(h * GLA_DK, (h + 1) * GLA_DK)
                vs = slice(h * GLA_DV, (h + 1) * GLA_DV)
                stp = st[d * GLA_HEADS + h]
                s_r[0, h] = stp
                v = v_r[:, vs].astype(BF16)
                att = jnp.where(mask, _dot_nt(qd[:, ks], ki[:, ks]), 0.0).astype(BF16)
                o_r[:, vs] = _dot(att, v) + _dot_nt(qd[:, ks], stp.astype(BF16))
                st[d * GLA_HEADS + h] = stp * dend[:, ks] + _dot_tn(v, ke[:, ks])

    lg_f = pl.BlockSpec((c, GLA_KD), lambda i: (up(i), 0))
    lg_b = pl.BlockSpec((c, GLA_KD), lambda i: (down(i), 1))
    return pl.pallas_call(
        body, name=name, grid=(n,),
        in_specs=[su["q"], su["k"], su["v"], lg_f, sd["q"], sd["k"], sd["v"], lg_b,
                  pl.BlockSpec((1, 2 * GLA_KD), lambda i: (0, 0))],
        out_specs=[su["wide"], sd["wide"], su["st"], sd["st"]],
        out_shape=[jax.ShapeDtypeStruct((S, GLA_VD), F32), jax.ShapeDtypeStruct((S, GLA_VD), F32),
                   jax.ShapeDtypeStruct((n, GLA_HEADS, GLA_DV, GLA_DK), F32),
                   jax.ShapeDtypeStruct((n, GLA_HEADS, GLA_DV, GLA_DK), F32)],
        scratch_shapes=[pltpu.VMEM((2 * GLA_HEADS, GLA_DV, GLA_DK), F32)],
        compiler_params=_params("arbitrary"),
    )(proj, proj, proj, logits, proj, proj, proj, logits, bias)


def _gla_bwd(proj, logits, bias, sf, sb, do, *, name):
    S = proj.shape[0]
    c = GLA_CHUNK
    n = S // c
    su, sd, up, down = _gla_specs(S, n)

    def body(qf, kf, vf, lf, stf, dof, qb, kb, vb, lb, stb, dob, bias_ref,
             dqkv_f, dlg_f, dqkv_b, dlg_b, dbias, dst):
        first = pl.program_id(0) == 0

        @pl.when(first)
        def _():
            dst[...] = jnp.zeros_like(dst)

        dbias_parts = []
        for d, (q_r, k_r, v_r, l_r, s_r, do_r, dqkv_r, dlg_r) in enumerate(
                ((qf, kf, vf, lf, stf, dof, dqkv_f, dlg_f), (qb, kb, vb, lb, stb, dob, dqkv_b, dlg_b))):
            cum, cum_t, mask = _gla_masks(d)
            xl, b, b_end = _gla_decay(l_r[...], bias_ref[:, d * GLA_KD:(d + 1) * GLA_KD], cum, d)
            e, ei, ee, dend = jnp.exp(b), jnp.exp(-b), jnp.exp(b_end - b), jnp.exp(b_end)
            k = k_r[...]
            qd32 = q_r[...] * GLA_SCALE * e
            ki32 = k * ei
            ke32 = k * ee
            qd, ki, ke = qd32.astype(BF16), ki32.astype(BF16), ke32.astype(BF16)
            db_parts, dbe_parts = [], []
            for h in range(GLA_HEADS):
                ks = slice(h * GLA_DK, (h + 1) * GLA_DK)
                vs = slice(h * GLA_DV, (h + 1) * GLA_DV)
                stp = s_r[0, h]
                dstn = dst[d * GLA_HEADS + h]
                dstn_b = dstn.astype(BF16)
                v = v_r[:, vs].astype(BF16)
                dov = do_r[:, vs]
                att = jnp.where(mask, _dot_nt(qd[:, ks], ki[:, ks]), 0.0).astype(BF16)
                datt = jnp.where(mask, _dot_nt(dov, v), 0.0).astype(BF16)
                dqkv_r[:, 2 * GLA_KD + h * GLA_DV:2 * GLA_KD + (h + 1) * GLA_DV] = (
                    _dot_tn(att, dov) + _dot_nt(ke[:, ks], dstn_b))
                dqd = _dot(datt, ki[:, ks]) + _dot(dov, stp.astype(BF16))
                dki = _dot_tn(datt, qd[:, ks])
                dke = _dot(v, dstn_b)
                d_dend = _colsum(stp * dstn)
                dst[d * GLA_HEADS + h] = _dot_tn(dov, qd[:, ks]) + dstn * dend[:, ks]
                dqkv_r[:, ks] = dqd * e[:, ks] * GLA_SCALE
                dqkv_r[:, GLA_KD + h * GLA_DK:GLA_KD + (h + 1) * GLA_DK] = dki * ei[:, ks] + dke * ee[:, ks]
                dke_ke = dke * ke32[:, ks]
                db_parts.append(dqd * qd32[:, ks] - dki * ki32[:, ks] - dke_ke)
                dbe_parts.append(_colsum(dke_ke) + d_dend * dend[:, ks])
            db = jnp.concatenate(db_parts, axis=1)
            db_end = jnp.concatenate(dbe_parts, axis=1)
            dla = _cumdot(cum_t, db) + db_end
            dlg = dla * (1.0 / GLA_GATE_NORMALIZER) * _sigmoid(-xl)
            dlg_r[...] = dlg
            dbias_parts.append(_colsum(dlg))
        dbv = jnp.concatenate(dbias_parts, axis=1)

        @pl.when(first)
        def _():
            dbias[...] = dbv

        @pl.when(jnp.logical_not(first))
        def _():
            dbias[...] += dbv

    lg_f = pl.BlockSpec((c, GLA_KD), lambda i: (down(i), 0))
    lg_b = pl.BlockSpec((c, GLA_KD), lambda i: (up(i), 1))
    dlg_f = pl.BlockSpec((c, GLA_KD), lambda i: (down(i), 0))
    dlg_b = pl.BlockSpec((c, GLA_KD), lambda i: (up(i), 0))
    return pl.pallas_call(
        body, name=name, grid=(n,),
        in_specs=[sd["q"], sd["k"], sd["v"], lg_f, sd["st"], sd["wide"],
                  su["q"], su["k"], su["v"], lg_b, su["st"], su["wide"],
                  pl.BlockSpec((1, 2 * GLA_KD), lambda i: (0, 0))],
        out_specs=[sd["qkv"], dlg_f, su["qkv"], dlg_b, pl.BlockSpec((1, 2 * GLA_KD), lambda i: (0, 0))],
        out_shape=[jax.ShapeDtypeStruct((S, 2 * GLA_KD + GLA_VD), F32), jax.ShapeDtypeStruct((S, GLA_KD), F32),
                   jax.ShapeDtypeStruct((S, 2 * GLA_KD + GLA_VD), F32), jax.ShapeDtypeStruct((S, GLA_KD), F32),
                   jax.ShapeDtypeStruct((1, 2 * GLA_KD), F32)],
        scratch_shapes=[pltpu.VMEM((2 * GLA_HEADS, GLA_DV, GLA_DK), F32)],
        compiler_params=_params("arbitrary"),
    )(proj, proj, proj, logits, sf, do, proj, proj, proj, logits, sb, do, bias)


def _gla_gate_fwd(of, ob, proj, w, *, name):
    S = of.shape[0]
    ts = _tile(S, 256, 16)

    def body(of_ref, ob_ref, g_ref, w_ref, y_ref):
        for h in range(GLA_HEADS):
            vs = slice(h * GLA_DV, (h + 1) * GLA_DV)
            o = of_ref[:, vs] + ob_ref[:, vs]
            r = lax.rsqrt(jnp.mean(o * o, axis=-1, keepdims=True) + NORM_EPS)
            g = g_ref[:, vs]
            y_ref[:, vs] = (o * r * w_ref[...] * (g * _sigmoid(g))).astype(BF16)

    wide = pl.BlockSpec((ts, GLA_VD), lambda i: (i, 0))
    return pl.pallas_call(
        body, name=name, grid=(S // ts,),
        in_specs=[wide, wide, pl.BlockSpec((ts, GLA_VD), lambda i: (i, 2)), pl.BlockSpec((1, GLA_DV), lambda i: (0, 0))],
        out_specs=wide,
        out_shape=jax.ShapeDtypeStruct((S, GLA_VD), BF16),
        compiler_params=_params("parallel"),
    )(of, ob, proj, w)


def _gla_gate_bwd(of, ob, proj, w, dy, *, name):
    S = of.shape[0]
    ts = _tile(S, 256, 16)

    def body(of_ref, ob_ref, g_ref, w_ref, dy_ref, do_ref, dg_ref, dw_ref):
        i = pl.program_id(0)
        acc = jnp.zeros((1, GLA_DV), F32)
        for h in range(GLA_HEADS):
            vs = slice(h * GLA_DV, (h + 1) * GLA_DV)
            o = of_ref[:, vs] + ob_ref[:, vs]
            r = lax.rsqrt(jnp.mean(o * o, axis=-1, keepdims=True) + NORM_EPS)
            oh = o * r
            g = g_ref[:, vs]
            sg = _sigmoid(g)
            dyv = dy_ref[:, vs]
            dn = dyv * (g * sg)
            dg_ref[:, vs] = dyv * (oh * w_ref[...]) * (sg * (1.0 + g * (1.0 - sg)))
            acc = acc + _colsum(dn * oh)
            gg = dn * w_ref[...]
            do_ref[:, vs] = (r * (gg - oh * jnp.mean(gg * oh, axis=-1, keepdims=True))).astype(BF16)

        @pl.when(i == 0)
        def _():
            dw_ref[...] = acc

        @pl.when(i > 0)
        def _():
            dw_ref[...] += acc

    wide = pl.BlockSpec((ts, GLA_VD), lambda i: (i, 0))
    vec = pl.BlockSpec((1, GLA_DV), lambda i: (0, 0))
    return pl.pallas_call(
        body, name=name, grid=(S // ts,),
        in_specs=[wide, wide, pl.BlockSpec((ts, GLA_VD), lambda i: (i, 2)), vec, wide],
        out_specs=[wide, wide, vec],
        out_shape=[jax.ShapeDtypeStruct((S, GLA_VD), BF16), jax.ShapeDtypeStruct((S, GLA_VD), F32),
                   jax.ShapeDtypeStruct((1, GLA_DV), F32)],
        compiler_params=_params("arbitrary"),
    )(of, ob, proj, w, dy)


def _gla_combine(dqkv_f, dqkv_b, dg, dr, *, name):
    S = dg.shape[0]
    ts = _tile(S, 512, 16)
    nqkv = 2 * GLA_KD + GLA_VD

    def body(f_ref, b_ref, g_ref, r_ref, o_ref):
        o_ref[:, :nqkv] = (f_ref[...] + b_ref[...]).astype(BF16)
        o_ref[:, nqkv:nqkv + GLA_VD] = g_ref[...].astype(BF16)
        o_ref[:, nqkv + GLA_VD:] = r_ref[...].astype(BF16)

    return pl.pallas_call(
        body, name=name, grid=(S // ts,),
        in_specs=[pl.BlockSpec((ts, nqkv), lambda i: (i, 0)), pl.BlockSpec((ts, nqkv), lambda i: (i, 0)),
                  pl.BlockSpec((ts, GLA_VD), lambda i: (i, 0)), pl.BlockSpec((ts, LANES), lambda i: (i, 0))],
        out_specs=pl.BlockSpec((ts, GLA_PROJ), lambda i: (i, 0)),
        out_shape=jax.ShapeDtypeStruct((S, GLA_PROJ), BF16),
        compiler_params=_params("parallel"),
    )(dqkv_f, dqkv_b, dg, dr)


def _loss_head(y, t, *, name):
    S, D = y.shape
    ts = _tile(S, 512, 16)
    n = S // ts

    def body(y_ref, t_ref, dy_ref, l_ref, acc):
        i = pl.program_id(0)
        diff = y_ref[...] - t_ref[...]
        dy_ref[...] = diff * (1.0 / D)
        part = _colsum(diff * diff)

        @pl.when(i == 0)
        def _():
            acc[...] = part

        @pl.when(i > 0)
        def _():
            acc[...] += part

        @pl.when(i == n - 1)
        def _():
            l_ref[...] = jnp.full(l_ref.shape, 0.5 / D, F32) * jnp.sum(acc[...])

    row = pl.BlockSpec((ts, D), lambda i: (i, 0))
    return pl.pallas_call(
        body, name=name, grid=(n,),
        in_specs=[row, row],
        out_specs=[row, pl.BlockSpec((SUBLANES, LANES), lambda i: (0, 0))],
        out_shape=[jax.ShapeDtypeStruct((S, D), F32), jax.ShapeDtypeStruct((SUBLANES, LANES), F32)],
        scratch_shapes=[pltpu.VMEM((1, D), F32)],
        compiler_params=_params("arbitrary"),
    )(y, t)


def _adamw(w, g, m, v, *, name):
    R, C = w.shape
    tr = _tile(R, 256, SUBLANES)

    def body(w_ref, g_ref, m_ref, v_ref, d_ref, nm_ref, nv_ref):
        gv = g_ref[...]
        nm = ADAM_B1 * m_ref[...] + (1.0 - ADAM_B1) * gv
        nv = ADAM_B2 * v_ref[...] + (1.0 - ADAM_B2) * (gv * gv)
        m_hat = nm / (1.0 - ADAM_B1 ** ADAM_STEP)
        v_hat = nv / (1.0 - ADAM_B2 ** ADAM_STEP)
        d_ref[...] = -ADAM_LR * (m_hat / (jnp.sqrt(v_hat) + ADAM_EPS) + ADAM_WD * w_ref[...])
        nm_ref[...] = nm
        nv_ref[...] = nv

    blk = pl.BlockSpec((tr, C), lambda i: (i, 0))
    shp = jax.ShapeDtypeStruct((R, C), F32)
    return pl.pallas_call(
        body, name=name, grid=(R // tr,),
        in_specs=[blk] * 4, out_specs=[blk] * 3, out_shape=[shp] * 3,
        compiler_params=_params("parallel"),
    )(w, g, m, v)


def _place():
    return lax.axis_index("x"), lax.axis_index("y"), lax.axis_index("c")


def _all_gather(block, *, name):
    R, L = block.shape

    def body(x_ref, out_ref, send_sems, recv_sems, local_sem):
        x, y, c = _place()
        me, sibling = (x, y, c), (x, y, 1 - c)
        chips = [(1 - x, y), (x, 1 - y), (1 - x, 1 - y)]

        def slot(px, py, pc):
            return out_ref.at[4 * px + 2 * py + pc]

        def copy(k, blk, to, src=None):
            return pltpu.make_async_remote_copy(
                src_ref=slot(*blk) if src is None else src, dst_ref=slot(*blk),
                send_sem=send_sems.at[k], recv_sem=recv_sems.at[k], device_id=to, device_id_type=MESH)

        mine = pltpu.make_async_copy(x_ref, slot(*me), local_sem)
        mine.start()
        first = [copy(0, me, sibling, src=x_ref)]
        first += [copy(1 + j, me, (*chip, c), src=x_ref) for j, chip in enumerate(chips)]
        for cp in first:
            cp.start()
        passed = [copy(4 + j, (*chip, c), sibling) for j, chip in enumerate(chips)]
        for j, chip in enumerate(chips):
            copy(1 + j, (*chip, c), me).wait_recv()
            passed[j].start()
        copy(0, sibling, me).wait_recv()
        for j, chip in enumerate(chips):
            copy(4 + j, (*chip, 1 - c), me).wait_recv()
        for cp in first + passed:
            cp.wait_send()
        mine.wait()

    return pl.pallas_call(
        body, name=name, in_specs=[ANY], out_specs=ANY,
        out_shape=jax.ShapeDtypeStruct((N_DEV, R, L), block.dtype),
        scratch_shapes=[pltpu.SemaphoreType.DMA((7,)), pltpu.SemaphoreType.DMA((7,)), pltpu.SemaphoreType.DMA],
    )(block)


HBM_SPEC = pl.BlockSpec(memory_space=pltpu.HBM)
SEM_SPEC = pl.BlockSpec(memory_space=pltpu.SEMAPHORE)
DATAFLOW = pltpu.SideEffectType.DATAFLOW_SIDE_EFFECTING


def _split_start(plan, srcs, lands, *, dep=(), name):
    make_copies, count = plan
    ns, nb = len(srcs), len(srcs) + len(lands)
    bufs = [pltpu.with_memory_space_constraint(a, pltpu.HBM) for a in list(srcs) + list(lands)]
    n_in = nb + len(dep)

    def body(*refs):
        send_sems, recv_sems, token = refs[n_in], refs[n_in + 1], refs[-1]
        for cp in make_copies(refs[:ns], refs[ns:nb], send_sems, recv_sems):
            cp.start()
        token[...] = jnp.zeros_like(token)

    outs = pl.pallas_call(
        body, name=name, in_specs=[HBM_SPEC] * nb + [ANY] * len(dep),
        out_specs=(SEM_SPEC, SEM_SPEC, *[HBM_SPEC] * nb, pl.BlockSpec(memory_space=pltpu.VMEM)),
        out_shape=(pltpu.SemaphoreType.DMA((count,)), pltpu.SemaphoreType.DMA((count,)),
                   *[pltpu.HBM(a.shape, a.dtype) for a in bufs], jax.ShapeDtypeStruct((SUBLANES, LANES), F32)),
        input_output_aliases={i: 2 + i for i in range(nb)},
        compiler_params=pltpu.CompilerParams(has_side_effects=DATAFLOW),
    )(*bufs, *dep)
    return dict(plan=plan, ns=ns, send=outs[0], recv=outs[1], bufs=list(outs[2:2 + nb]), token=outs[-1])


def _split_wait(started, after, *, name):
    make_copies, _ = started["plan"]
    ns, nb = started["ns"], len(started["bufs"])
    after = tuple(after) if isinstance(after, (tuple, list)) else (after,)

    def body(*refs):
        for cp in make_copies(refs[:ns], refs[ns:nb], refs[nb], refs[nb + 1]):
            cp.wait_send()
            cp.wait_recv()

    outs = pl.pallas_call(
        body, name=name, in_specs=[HBM_SPEC] * nb + [SEM_SPEC, SEM_SPEC] + [ANY] * len(after),
        out_specs=[HBM_SPEC] * nb,
        out_shape=[pltpu.HBM(a.shape, a.dtype) for a in started["bufs"]],
        input_output_aliases={i: i for i in range(nb)},
        compiler_params=pltpu.CompilerParams(has_side_effects=DATAFLOW),
    )(*started["bufs"], started["send"], started["recv"], *after)
    return list(outs[:ns]), list(outs[ns:])


def _remote(src, dst, send_sems, recv_sems, k, to):
    return pltpu.make_async_remote_copy(src_ref=src, dst_ref=dst, send_sem=send_sems.at[k], recv_sem=recv_sems.at[k],
                                        device_id=to, device_id_type=MESH)


def _other_chips(x, y):
    return [(1 - x, y), (x, 1 - y), (1 - x, 1 - y)]


def _gather_send_plan(n):
    def make(srcs, lands, send_sems, recv_sems):
        x, y, c = _place()
        targets = [(x, y, 1 - c)] + [(cx, cy, c) for cx, cy in _other_chips(x, y)]
        return [_remote(srcs[t], lands[t].at[4 * x + 2 * y + c], send_sems, recv_sems, 4 * t + k, to)
                for t in range(n) for k, to in enumerate(targets)]
    return make, 4 * n


def _gather_pass_plan(n):
    def make(srcs, lands, send_sems, recv_sems):
        x, y, c = _place()
        cps = []
        for t in range(n):
            for j, (cx, cy) in enumerate(_other_chips(x, y)):
                slot = lands[t].at[4 * cx + 2 * cy + c]
                cps.append(_remote(slot, slot, send_sems, recv_sems, 3 * t + j, (x, y, 1 - c)))
        return cps
    return make, 3 * n


def _reduce_sibling_plan(n):
    def make(srcs, lands, send_sems, recv_sems):
        x, y, c = _place()
        return [_remote(srcs[t].at[2 * k + 1 - c], lands[t].at[k], send_sems, recv_sems, 4 * t + k, (x, y, 1 - c))
                for t in range(n) for k in range(4)]
    return make, 4 * n


def _reduce_chip_plan(n):
    def make(srcs, lands, send_sems, recv_sems):
        x, y, c = _place()
        return [_remote(srcs[t].at[2 * cx + cy], lands[t].at[2 * x + y], send_sems, recv_sems, 3 * t + j, (cx, cy, c))
                for t in range(n) for j, (cx, cy) in enumerate(_other_chips(x, y))]
    return make, 3 * n


def _unshard_cols(g, own, dev_idx, groups, width, *, name):
    _, A, Bs = g.shape
    ta = _tile(A, 256, 16)

    def body(dev_ref, g_ref, own_ref, *o_refs):
        for o_ref, devs in zip(o_refs, groups):
            for q, d in enumerate(devs):
                o_ref[:, q * Bs:(q + 1) * Bs] = jnp.where(dev_ref[0] == d, own_ref[...], g_ref[d])
            if len(devs) * Bs < width:
                o_ref[:, len(devs) * Bs:] = jnp.zeros((ta, width - len(devs) * Bs), g.dtype)

    return pl.pallas_call(
        body, name=name,
        grid_spec=pltpu.PrefetchScalarGridSpec(
            num_scalar_prefetch=1, grid=(A // ta,),
            in_specs=[pl.BlockSpec((N_DEV, ta, Bs), lambda i, d: (0, i, 0)), pl.BlockSpec((ta, Bs), lambda i, d: (i, 0))],
            out_specs=[pl.BlockSpec((ta, width), lambda i, d: (i, 0)) for _ in groups]),
        out_shape=[jax.ShapeDtypeStruct((A, width), g.dtype) for _ in groups],
        compiler_params=_params("parallel"),
    )(dev_idx, g, own)


def _place_own(g, own, dev_idx, *, name):
    _, As, B = g.shape
    ta = _tile(As, 256, 16)

    def body(dev_ref, _, own_ref, o_ref):
        o_ref[...] = own_ref[...]

    out = pl.pallas_call(
        body, name=name,
        grid_spec=pltpu.PrefetchScalarGridSpec(
            num_scalar_prefetch=1, grid=(As // ta,),
            in_specs=[ANY, pl.BlockSpec((ta, B), lambda i, d: (i, 0))],
            out_specs=pl.BlockSpec((None, ta, B), lambda i, d: (d[0], i, 0))),
        out_shape=jax.ShapeDtypeStruct(g.shape, g.dtype),
        input_output_aliases={1: 0},
        compiler_params=_params("parallel"),
    )(dev_idx, g, own)
    return out.reshape(N_DEV * As, B)


def _add_sibling(g, buf, c_idx, *, name):
    _, A, B = g.shape
    ta = _tile(A, 256, 16)

    def body(c_ref, g_ref, b_ref, o_ref):
        o_ref[...] = (g_ref[...].astype(F32) + b_ref[...].astype(F32)).astype(BF16)

    blk = pl.BlockSpec((None, ta, B), lambda k, i, c_ref: (k, i, 0))
    return pl.pallas_call(
        body, name=name,
        grid_spec=pltpu.PrefetchScalarGridSpec(
            num_scalar_prefetch=1, grid=(4, A // ta),
            in_specs=[pl.BlockSpec((None, ta, B), lambda k, i, c_ref: (2 * k + c_ref[0], i, 0)), blk],
            out_specs=blk),
        out_shape=jax.ShapeDtypeStruct((4, A, B), BF16),
        compiler_params=_params("parallel", "parallel"),
    )(c_idx, g, buf)


def _adamw_layer(w, m, v, own, parts, chip_idx, outs, layer, *, name):
    _, A, B = w.shape
    ta = _tile(A, 256, 16)

    def body(chip_ref, w_ref, m_ref, v_ref, own_ref, p_ref, *rest):
        g_ref, d_ref, nm_ref, nv_ref = rest[4:]
        gv = None
        for j in range(4):
            part = jnp.where(chip_ref[0] == j, own_ref[...], p_ref[j]).astype(F32)
            gv = part if gv is None else gv + part
        nm = ADAM_B1 * m_ref[...] + (1.0 - ADAM_B1) * gv
        nv = ADAM_B2 * v_ref[...] + (1.0 - ADAM_B2) * (gv * gv)
        m_hat = nm / (1.0 - ADAM_B1 ** ADAM_STEP)
        v_hat = nv / (1.0 - ADAM_B2 ** ADAM_STEP)
        g_ref[...] = gv
        d_ref[...] = -ADAM_LR * (m_hat / (jnp.sqrt(v_hat) + ADAM_EPS) + ADAM_WD * w_ref[...])
        nm_ref[...] = nm
        nv_ref[...] = nv

    blk = pl.BlockSpec((None, ta, B), lambda i, ch: (layer, i, 0))
    return pl.pallas_call(
        body, name=name,
        grid_spec=pltpu.PrefetchScalarGridSpec(
            num_scalar_prefetch=1, grid=(A // ta,),
            in_specs=[blk, blk, blk, pl.BlockSpec((None, ta, B), lambda i, ch: (ch[0], i, 0)),
                      pl.BlockSpec((4, ta, B), lambda i, ch: (0, i, 0))] + [ANY] * 4,
            out_specs=[blk] * 4),
        out_shape=[jax.ShapeDtypeStruct(o.shape, o.dtype) for o in outs],
        input_output_aliases={6 + q: q for q in range(4)},
        compiler_params=_params("parallel"),
    )(chip_idx, w, m, v, own, parts, *outs)


def _sum_slots(buf, *, name):
    n, R, L = buf.shape
    tr = _tile(R, 512, SUBLANES)

    def body(b_ref, o_ref):
        acc = b_ref[0]
        for j in range(1, n):
            acc = acc + b_ref[j]
        o_ref[...] = acc

    return pl.pallas_call(
        body, name=name, grid=(R // tr,),
        in_specs=[pl.BlockSpec((n, tr, L), lambda i: (0, i, 0))],
        out_specs=pl.BlockSpec((tr, L), lambda i: (i, 0)),
        out_shape=jax.ShapeDtypeStruct((R, L), buf.dtype),
        compiler_params=_params("parallel"),
    )(buf)


BIG = ("gla_w_in", "gla_w_out", "attn_w_qkv", "attn_w_out", "ffn_w_up", "ffn_w_down")
SMALL_SHARDED = ("gla_w_gate_up_f", "gla_w_gate_up_b", "ffn_w_conv")
REPLICATED = ("norm_mix", "norm_ffn", "gla_b_gate_f", "gla_b_gate_b", "gla_norm", "attn_q_norm", "attn_k_norm",
              "ffn_b_conv")
WEIGHTS = ("norm_mix", "norm_ffn", "gla_w_in", "gla_w_gate_up_f", "gla_b_gate_f", "gla_w_gate_up_b", "gla_b_gate_b",
           "gla_norm", "gla_w_out", "attn_w_qkv", "attn_q_norm", "attn_k_norm", "attn_w_out", "ffn_w_up", "ffn_w_conv",
           "ffn_b_conv", "ffn_w_down")


def _rows(flat, row_align):
    n = flat.shape[0]
    per = row_align * LANES
    padded = -(-n // per) * per
    return jnp.pad(flat, (0, padded - n)).reshape(padded // LANES, LANES)


def _side_by_side(gathered, own, dev):
    n, a, b = gathered.shape
    whole = lax.dynamic_update_index_in_dim(gathered, own, dev, 0)
    return jnp.transpose(whole, (1, 0, 2)).reshape(a, n * b)


def _layer_shards(w, i, mixer, ffn):
    j = i // 2
    parts = []
    if mixer and i % 2 == 0:
        parts += [("mix_in", w["gla_w_in"][j].astype(BF16)), ("mix_out", w["gla_w_out"][j].astype(BF16)),
                  ("gate_f", w["gla_w_gate_up_f"][j].astype(BF16)), ("gate_b", w["gla_w_gate_up_b"][j].astype(BF16))]
    elif mixer:
        parts += [("mix_in", w["attn_w_qkv"][j].astype(BF16)), ("mix_out", w["attn_w_out"][j].astype(BF16))]
    if ffn:
        parts += [("up", w["ffn_w_up"][i].astype(BF16)), ("down", w["ffn_w_down"][i].astype(BF16)),
                  ("conv", w["ffn_w_conv"][i])]
    return [n for n, _ in parts], [a for _, a in parts]


def _layer_weights(names, own, gathered, dev, dev_idx, i):
    own, got = dict(zip(names, own)), dict(zip(names, gathered))
    every = tuple(range(N_DEV))
    half = N_DEV // 2
    tag = "_l%d" % i
    out = {}
    if "mix_in" in got:
        width = GLA_PROJ if i % 2 == 0 else QKV_DIM
        (out["mix_in"],) = _unshard_cols(got["mix_in"], own["mix_in"], dev_idx, [every], width,
                                         name="unshard_mix_in" + tag)
        out["mix_out"] = _place_own(got["mix_out"], own["mix_out"], dev_idx, name="place_mix_out" + tag)
    if "gate_f" in got:
        out["gate"] = _gate_matrix(_side_by_side(got["gate_f"], own["gate_f"], dev),
                                   _side_by_side(got["gate_b"], own["gate_b"], dev))
    if "up" in got:
        f = got["up"].shape[-1] * half
        out["up_val"], out["up_gate"] = _unshard_cols(got["up"], own["up"], dev_idx, [every[:half], every[half:]], f,
                                                      name="unshard_ffn_up" + tag)
        out["down"] = _place_own(got["down"], own["down"], dev_idx, name="place_ffn_down" + tag)
        out["conv"] = _side_by_side(got["conv"], own["conv"], dev)
    return out


def _rope_tables(S):
    rows = S // GRID_W
    pairs = ATTN_HD // 4
    row_idx = jnp.repeat(jnp.arange(rows, dtype=F32), GRID_W)
    col_idx = jnp.tile(jnp.arange(GRID_W, dtype=F32), rows)
    inv_freq = ROPE_THETA ** (-jnp.arange(pairs, dtype=F32) / pairs)
    ang = jnp.concatenate([row_idx[:, None] * inv_freq, col_idx[:, None] * inv_freq], axis=-1)
    cos, sin = jnp.cos(ang), jnp.sin(ang)
    return jnp.concatenate([cos, cos], axis=-1), jnp.concatenate([-sin, sin], axis=-1)


def _gate_matrix(w_f, w_b):
    rk = w_f.shape[0]
    top = jnp.concatenate([w_f, jnp.zeros_like(w_f)], axis=1)
    mid = jnp.concatenate([jnp.zeros_like(w_b), w_b], axis=1)
    pad = jnp.zeros((LANES - 2 * rk, 2 * GLA_KD), w_f.dtype)
    return jnp.concatenate([top, mid, pad], axis=0)


def _local_step(x, target, rep, w, dev, idx):
    S, D = x.shape
    depth = rep["norm_mix"].shape[0]
    cs, sn = _rope_tables(S)
    row = lambda a: a.reshape(1, -1)
    ranks_cols = (GLA_PROJ - LANES, LANES)

    groups = [(0, True, False), (0, False, True)] + [(i, True, True) for i in range(1, depth)]
    sent = []
    for g, (i, mixer, ffn) in enumerate(groups):
        names, srcs = _layer_shards(w, i, mixer, ffn)
        lands = [lax.empty((N_DEV,) + a.shape, a.dtype) for a in srcs]
        dep = (sent[-1][1]["token"],) if sent else ()
        sent.append((names, _split_start(_gather_send_plan(len(srcs)), srcs, lands, dep=dep,
                                         name="weights_send_g%d" % g)))

    def arrive(g, after):
        names, started = sent[g]
        own, lands = _split_wait(started, after, name="weights_arrive_g%d" % g)
        return names, own, _split_start(_gather_pass_plan(len(lands)), [], lands, name="weights_pass_g%d" % g)

    def ready(g, passing, after):
        names, own, started = passing
        _, lands = _split_wait(started, after, name="weights_passed_g%d" % g)
        return _layer_weights(names, own, lands, dev, idx["dev"], groups[g][0])

    passing = arrive(0, sent[-1][1]["token"])
    wl = ready(0, passing, passing[2]["token"])
    passing = arrive(1, wl["mix_in"])

    saved = []
    for i in range(depth):
        j = i // 2
        sv = {"x0": x, "w": wl}
        h1 = _rmsnorm_fwd(x, row(rep["norm_mix"][i]), dep=(passing[2]["token"],) if i == 0 else (), name="norm_mix_fwd")
        sv["h1"] = h1
        if i % 2 == 0:
            bias = jnp.concatenate([rep["gla_b_gate_f"][j], rep["gla_b_gate_b"][j]]).reshape(1, -1)
            proj = _mm(h1, wl["mix_in"], name="gla_in_proj")
            logits = _mm(proj, wl["gate"], a_cols=ranks_cols, name="gla_gate_logits")
            of, ob, sf, sb = _gla_fwd(proj, logits, bias, name="gla_fwd")
            y = _gla_gate_fwd(of, ob, proj, row(rep["gla_norm"][j]), name="gla_gate_fwd")
            x = _mm(y, wl["mix_out"], res=x, name="gla_out_proj")
            sv.update(bias=bias, proj=proj, logits=logits, of=of, ob=ob, sf=sf, sb=sb, y=y)
        else:
            proj = _mm(h1, wl["mix_in"], name="attn_qkv_proj")
            qkv = _attn_prep(proj, cs, sn, row(rep["attn_q_norm"][j]), row(rep["attn_k_norm"][j]), name="attn_prep")
            o = _attn_fwd(qkv, name="attn_fwd")
            x = _mm(o, wl["mix_out"], res=x, name="attn_out_proj")
            sv.update(proj=proj, qkv=qkv, o=o)
        sv["x1"] = x
        if i == 0:
            wl.update(ready(1, passing, x))
        dep = ()
        if i + 1 < depth:
            passing = arrive(i + 2, x)
            dep = (passing[2]["token"],)
        h2 = _rmsnorm_fwd(x, row(rep["norm_ffn"][i]), dep=dep, name="norm_ffn_fwd")
        F = wl["down"].shape[0]
        wc, bc = wl["conv"], rep["ffn_b_conv"][i]
        wcv, wcg, bcv, bcg = wc[:, :F], wc[:, F:], row(bc[:F]), row(bc[F:])
        uv = _mm(h2, wl["up_val"], name="ffn_up_val")
        ug = _mm(h2, wl["up_gate"], name="ffn_up_gate")
        act = _conv_act_fwd(uv, ug, wcv, wcg, bcv, bcg, name="ffn_conv_act")
        x = _mm(act, wl["down"], res=x, name="ffn_down")
        sv.update(h2=h2, uv=uv, ug=ug, act=act, wcv=wcv, wcg=wcg, bcv=bcv, bcg=bcg)
        saved.append(sv)
        if i + 1 < depth:
            wl = ready(i + 2, passing, x)

    dx, loss_tile = _loss_head(x, target, name="loss_head")
    loss = loss_tile[0, 0]

    in_sibling_stage, in_chip_stage, reduced = [], [], []

    def advance(group, after):
        tokens = []
        for tag, keys, started in in_chip_stage:
            partial, lands = _split_wait(started, after, name="grads_chips_arrive_" + tag)
            reduced.append((keys, partial, lands))
        in_chip_stage.clear()
        for tag, keys, started in in_sibling_stage:
            stacks, lands = _split_wait(started, after, name="grads_sibling_arrive_" + tag)
            partial = [_add_sibling(s, b, idx["core"], name="grads_add_sibling_%s_%d" % (tag, q))
                       for q, (s, b) in enumerate(zip(stacks, lands))]
            bufs = [lax.empty(p.shape, p.dtype) for p in partial]
            started = _split_start(_reduce_chip_plan(len(partial)), partial, bufs, name="grads_chips_send_" + tag)
            in_chip_stage.append((tag, keys, started))
            tokens.append(started["token"])
        in_sibling_stage.clear()
        if group is not None:
            tag, keys, stacks = group
            bufs = [lax.empty((4,) + s.shape[1:], s.dtype) for s in stacks]
            started = _split_start(_reduce_sibling_plan(len(stacks)), stacks, bufs, name="grads_sibling_send_" + tag)
            in_sibling_stage.append((tag, keys, started))
            tokens.append(started["token"])
        return tuple(tokens)

    def stack_for(name):
        return lax.empty((N_DEV,) + tuple(w[name].shape[1:]), BF16)

    gl = {k: [None] * depth for k in ("norm_mix", "norm_ffn", "ffn_w_conv", "ffn_b_conv")}
    gm = {k: [None] * (depth // 2) for k in ("gla_w_gate_up_f", "gla_b_gate_f", "gla_w_gate_up_b", "gla_b_gate_b",
                                             "gla_norm", "attn_q_norm", "attn_k_norm")}
    rk = GLA_GATE_RANK
    dep = ()
    for i in reversed(range(depth)):
        j = i // 2
        sv = saved[i]
        wl = sv["w"]
        dact = _mm(dx, wl["down"], tb=True, dep=dep, name="ffn_down_dgrad")
        g_down = _wgrad(sv["act"], dx, stack_for("ffn_w_down"), shard="rows", name="ffn_down_wgrad")
        dcv, dcg, dwv, dwg, dbv, dbg = _conv_act_bwd(sv["uv"], sv["ug"], sv["wcv"], sv["wcg"], sv["bcv"], sv["bcg"],
                                                     dact, name="ffn_conv_act_bwd")
        gl["ffn_w_conv"][i] = jnp.concatenate([dwv, dwg], axis=1)
        gl["ffn_b_conv"][i] = jnp.concatenate([dbv, dbg], axis=1)[0]
        mid = advance(None, dcv)
        duv = _conv_t(dcv, sv["wcv"], name="ffn_conv_t")
        dug = _conv_t(dcg, sv["wcg"], name="ffn_conv_t")
        dh2 = _mm(duv, wl["up_val"], tb=True, dep=mid, name="ffn_up_dgrad_val")
        dh2 = _mm(dug, wl["up_gate"], tb=True, res=dh2, name="ffn_up_dgrad_gate")
        g_up = _wgrad(sv["h2"], duv, stack_for("ffn_w_up"), shard="cols", group=0, name="ffn_up_wgrad_val")
        g_up = _wgrad(sv["h2"], dug, g_up, shard="cols", group=1, name="ffn_up_wgrad_gate")
        dx, dn = _rmsnorm_bwd(sv["x1"], row(rep["norm_ffn"][i]), dh2, dx, name="norm_ffn_bwd")
        gl["norm_ffn"][i] = dn[0]
        dep = advance(("ffn_l%d" % i, [("ffn_w_up", i), ("ffn_w_down", i)], [g_up, g_down]), dx)
        if i % 2 == 0:
            dy = _mm(dx, wl["mix_out"], tb=True, dep=dep, name="gla_out_dgrad")
            g_out = _wgrad(sv["y"], dx, stack_for("gla_w_out"), shard="rows", name="gla_out_wgrad")
            do, dg, dgn = _gla_gate_bwd(sv["of"], sv["ob"], sv["proj"], row(rep["gla_norm"][j]), dy, name="gla_gate_bwd")
            gm["gla_norm"][j] = dgn[0]
            dqkv_f, dlg_f, dqkv_b, dlg_b, dbias = _gla_bwd(sv["proj"], sv["logits"], sv["bias"], sv["sf"], sv["sb"], do,
                                                           name="gla_bwd")
            gm["gla_b_gate_f"][j] = dbias[0, :GLA_KD]
            gm["gla_b_gate_b"][j] = dbias[0, GLA_KD:]
            mid = advance(None, dqkv_f)
            dlogits = jnp.concatenate([dlg_f, dlg_b], axis=1)
            dr = _mm(dlogits, wl["gate"], tb=True, dep=mid, name="gla_gate_dgrad")
            dwg_full = _mm(sv["proj"], dlogits, ta=True, a_cols=ranks_cols, name="gla_gate_wgrad")
            gm["gla_w_gate_up_f"][j] = dwg_full[:rk, :GLA_KD]
            gm["gla_w_gate_up_b"][j] = dwg_full[rk:2 * rk, GLA_KD:]
            dproj = _gla_combine(dqkv_f, dqkv_b, dg, dr, name="gla_combine")
            dh1 = _mm(dproj, wl["mix_in"], tb=True, name="gla_in_dgrad")
            g_in = _wgrad(sv["h1"], dproj, stack_for("gla_w_in"), shard="cols", name="gla_in_wgrad")
            keys = [("gla_w_in", j), ("gla_w_out", j)]
        else:
            do = _mm(dx, wl["mix_out"], tb=True, out_dtype=BF16, dep=dep, name="attn_out_dgrad")
            g_out = _wgrad(sv["o"], dx, stack_for("attn_w_out"), shard="rows", name="attn_out_wgrad")
            dq, dk, dv = _attn_bwd(sv["qkv"], do, name="attn_bwd")
            mid = advance(None, dq)
            dproj, dqn, dkn = _attn_prep_bwd(sv["proj"], dq, dk, dv, cs, sn, row(rep["attn_q_norm"][j]),
                                             row(rep["attn_k_norm"][j]), name="attn_prep_bwd")
            gm["attn_q_norm"][j] = dqn[0]
            gm["attn_k_norm"][j] = dkn[0]
            dh1 = _mm(dproj, wl["mix_in"], tb=True, dep=mid, name="attn_qkv_dgrad")
            g_in = _wgrad(sv["h1"], dproj, stack_for("attn_w_qkv"), shard="cols", name="attn_qkv_wgrad")
            keys = [("attn_w_qkv", j), ("attn_w_out", j)]
        dx, dn = _rmsnorm_bwd(sv["x0"], row(rep["norm_mix"][i]), dh1, dx, name="norm_mix_bwd")
        gl["norm_mix"][i] = dn[0]
        dep = advance(("mix_l%d" % i, keys, [g_in, g_out]), dx)

    small = {k: jnp.stack(v) for k, v in {**gl, **gm}.items()}
    return loss, dx, reduced, small, advance


def kernel(x, norm_mix, norm_ffn, gla_w_in, gla_w_gate_up_f, gla_b_gate_f, gla_w_gate_up_b, gla_b_gate_b, gla_norm, gla_w_out, attn_w_qkv, attn_q_norm, attn_k_norm, attn_w_out, ffn_w_up, ffn_w_conv, ffn_b_conv, ffn_w_down, loss_target, m_norm_mix, m_norm_ffn, m_gla_w_in, m_gla_w_gate_up_f, m_gla_b_gate_f, m_gla_w_gate_up_b, m_gla_b_gate_b, m_gla_norm, m_gla_w_out, m_attn_w_qkv, m_attn_q_norm, m_attn_k_norm, m_attn_w_out, m_ffn_w_up, m_ffn_w_conv, m_ffn_b_conv, m_ffn_w_down, v_norm_mix, v_norm_ffn, v_gla_w_in, v_gla_w_gate_up_f, v_gla_b_gate_f, v_gla_w_gate_up_b, v_gla_b_gate_b, v_gla_norm, v_gla_w_out, v_attn_w_qkv, v_attn_q_norm, v_attn_k_norm, v_attn_w_out, v_ffn_w_up, v_ffn_w_conv, v_ffn_b_conv, v_ffn_w_down):
    given = dict(locals())
    w = {n: given[n] for n in WEIGHTS}
    m = {n: given["m_" + n] for n in WEIGHTS}
    v = {n: given["v_" + n] for n in WEIGHTS}
    shards = {n: w[n] for n in BIG + SMALL_SHARDED}
    rep = {n: w[n] for n in REPLICATED}

    x_pos, y_pos, c_pos = _place()
    dev = 4 * x_pos + 2 * y_pos + c_pos
    as_operand = lambda s: jnp.asarray(s, jnp.int32).reshape(1)
    idx = dict(dev=as_operand(dev), chip=as_operand(2 * x_pos + y_pos), core=as_operand(c_pos))

    loss_local, grad_x, reduced, small, advance = _local_step(x[0], loss_target[0], rep, shards, dev, idx)
    loss = lax.psum(loss_local, ("x", "y", "c"))

    big = {n: [lax.empty(w[n].shape, F32) for _ in range(4)] for n in BIG}

    def update_reduced():
        for keys, own, parts in reduced:
            for (n, layer), p_own, p_others in zip(keys, own, parts):
                big[n] = _adamw_layer(w[n], m[n], v[n], p_own, p_others, idx["chip"], big[n], layer,
                                      name="adamw_%s_l%d" % (n, layer))
        reduced.clear()

    advance(None, grad_x)
    update_reduced()

    rest = REPLICATED + SMALL_SHARDED
    flat = _rows(jnp.concatenate([small[n].reshape(-1) for n in rest]), SUBLANES)
    total = _sum_slots(_all_gather(flat, name="small_grads_all_gather"), name="small_grads_sum").reshape(-1)
    advance(None, [total, *big["ffn_w_up"]])
    update_reduced()
    g, off = {}, 0
    for n in rest:
        whole = total[off:off + small[n].size].reshape(small[n].shape)
        off += small[n].size
        width = w[n].shape[-1]
        g[n] = whole if n in REPLICATED else lax.dynamic_slice_in_dim(whole, dev * width, width, axis=whole.ndim - 1)

    delta, new_m, new_v = {}, {}, {}
    for n in WEIGHTS:
        if n in BIG:
            g[n], delta[n], new_m[n], new_v[n] = big[n]
        else:
            shape = w[n].shape
            two_d = (-1, shape[-1])
            d2, m2, v2 = _adamw(w[n].reshape(two_d), g[n].reshape(two_d), m[n].reshape(two_d), v[n].reshape(two_d),
                                name="adamw_" + n)
            delta[n], new_m[n], new_v[n] = d2.reshape(shape), m2.reshape(shape), v2.reshape(shape)

    return (loss, grad_x[None], *[g[n] for n in WEIGHTS], *[delta[n] for n in WEIGHTS],
            *[new_m[n] for n in WEIGHTS], *[new_v[n] for n in WEIGHTS])
```

```python
import math

import jax
import jax.numpy as jnp
from jax import lax
from jax.experimental import pallas as pl
from jax.experimental.pallas import tpu as pltpu

F32 = jnp.float32
BF16 = jnp.bfloat16
MESH = pl.DeviceIdType.MESH

N_DEV = 8
LANES = 128
SUBLANES = 8
VMEM_LIMIT = 56 * 1024 * 1024

NORM_EPS = 1e-6
GRID_W = 64
ROPE_THETA = 10000.0
GLA_HEADS = 4
GLA_DK = 128
GLA_DV = 256
GLA_CHUNK = 64
GLA_GATE_RANK = 16
GLA_GATE_NORMALIZER = 16.0
ATTN_HD = 128
ATTN_Q_HEADS = 8
ATTN_KV_HEADS = 2
ATTN_GROUP = ATTN_Q_HEADS // ATTN_KV_HEADS

ADAM_LR = 0.001
ADAM_B1 = 0.9
ADAM_B2 = 0.999
ADAM_EPS = 1e-08
ADAM_WD = 0.01
ADAM_STEP = 10


def _tile(n, target, align=LANES):
    if n <= target:
        return n
    t = (target // align) * align
    while t >= align:
        if n % t == 0:
            return t
        t -= align
    return n


def _params(*sem):
    return pltpu.CompilerParams(dimension_semantics=sem, vmem_limit_bytes=VMEM_LIMIT)


def _dot(a, b):
    return lax.dot_general(a, b, (((1,), (0,)), ((), ())), preferred_element_type=F32)


def _dot_nt(a, b):
    return lax.dot_general(a, b, (((1,), (1,)), ((), ())), preferred_element_type=F32)


def _dot_tn(a, b):
    return lax.dot_general(a, b, (((0,), (0,)), ((), ())), preferred_element_type=F32)


def _sigmoid(x):
    return 0.5 * jnp.tanh(0.5 * x) + 0.5


def _log_sigmoid(x):
    return jnp.minimum(x, 0.0) - jnp.log(1.0 + jnp.exp(-jnp.abs(x)))


def _colsum(x):
    return jnp.sum(x, axis=0, keepdims=True)


ANY = pl.BlockSpec(memory_space=pl.ANY)


def _mm(a, b, *, ta=False, tb=False, res=None, out_dtype=F32, layer=None, a_cols=None, dep=(), name):
    if tb:
        N, K = b.shape[-2:]
    else:
        K, N = b.shape[-2:]
    a_rows, a_width = a.shape
    a_off = 0
    if a_cols is not None:
        a_off, a_width = a_cols
    if ta:
        M = a_width
        assert a_rows == K, (a.shape, b.shape, ta, tb)
    else:
        M = a_rows
        assert a_width == K, (a.shape, b.shape, ta, tb)
    tm = _tile(M, 1408) if ta else _tile(M, 1024, 16)
    tn = _tile(N, 1408)
    tk = _tile(K, 512, 16) if ta else _tile(K, 1408)
    nk = K // tk
    dims = (((0 if ta else 1,), (1 if tb else 0,)), ((), ()))

    n_in = 2 + (res is not None) + len(dep)

    def body(*refs):
        a_ref, b_ref = refs[:2]
        r_ref = refs[2] if res is not None else None
        o_ref = refs[n_in]
        scr = refs[n_in + 1:]
        part = lax.dot_general(a_ref[...].astype(BF16), b_ref[...].astype(BF16), dims, preferred_element_type=F32)

        def finish(acc):
            if r_ref is not None:
                acc = acc + r_ref[...]
            o_ref[...] = acc.astype(out_dtype)

        if nk == 1:
            finish(part)
        else:
            acc_ref = scr[0]
            k = pl.program_id(2)

            @pl.when(k == 0)
            def _():
                acc_ref[...] = part

            @pl.when(k > 0)
            def _():
                acc_ref[...] += part

            @pl.when(k == nk - 1)
            def _():
                finish(acc_ref[...])

    a_blk = a_off // (tm if ta else tk)
    assert a_off % (tm if ta else tk) == 0
    a_spec = (pl.BlockSpec((tk, tm), lambda i, j, k: (k, a_blk + i)) if ta
              else pl.BlockSpec((tm, tk), lambda i, j, k: (i, a_blk + k)))
    if layer is None:
        b_spec = pl.BlockSpec((tn, tk), lambda i, j, k: (j, k)) if tb else pl.BlockSpec((tk, tn), lambda i, j, k: (k, j))
    else:
        b_spec = (pl.BlockSpec((None, tn, tk), lambda i, j, k: (layer, j, k)) if tb
                  else pl.BlockSpec((None, tk, tn), lambda i, j, k: (layer, k, j)))
    o_spec = pl.BlockSpec((tm, tn), lambda i, j, k: (i, j))
    in_specs = [a_spec, b_spec] + ([o_spec] if res is not None else []) + [ANY] * len(dep)
    args = (a, b) + ((res,) if res is not None else ()) + tuple(dep)
    return pl.pallas_call(
        body, name=name, grid=(M // tm, N // tn, nk),
        in_specs=in_specs, out_specs=o_spec,
        out_shape=jax.ShapeDtypeStruct((M, N), out_dtype),
        scratch_shapes=[pltpu.VMEM((tm, tn), F32)] if nk > 1 else [],
        compiler_params=_params("parallel", "parallel", "arbitrary"),
    )(*args)


def _wgrad(a, b, stack, *, shard, group=0, name):
    S, N = b.shape
    M = a.shape[1]
    As, Bs = stack.shape[-2:]
    tk = _tile(S, 1024, 16)
    nk = S // tk
    if shard == "cols":
        n = N // Bs
        tm = _tile(M, 512)
        tn = N
        grid = (M // tm, 1, nk)
        o_spec = pl.BlockSpec((n, tm, Bs), lambda i, j, k: (group, i, 0))
    else:
        per = As * LANES // math.gcd(As, LANES)
        tm = M if M <= 1408 else _tile(M, 1408, per)
        n = tm // As
        tn = _tile(N, 1024)
        grid = (M // tm, N // tn, nk)
        o_spec = pl.BlockSpec((n, As, tn), lambda i, j, k: (i, 0, j))

    def body(a_ref, b_ref, _, o_ref, acc_ref):
        k = pl.program_id(2)
        part = _dot_tn(a_ref[...].astype(BF16), b_ref[...].astype(BF16))

        @pl.when(k == 0)
        def _():
            acc_ref[...] = part

        @pl.when(k > 0)
        def _():
            acc_ref[...] += part

        @pl.when(k == nk - 1)
        def _():
            for q in range(n):
                if shard == "cols":
                    o_ref[q] = acc_ref[:, q * Bs:(q + 1) * Bs].astype(stack.dtype)
                else:
                    o_ref[q] = acc_ref[q * As:(q + 1) * As, :].astype(stack.dtype)

    return pl.pallas_call(
        body, name=name, grid=grid,
        in_specs=[pl.BlockSpec((tk, tm), lambda i, j, k: (k, i)), pl.BlockSpec((tk, tn), lambda i, j, k: (k, j)),
                  pl.BlockSpec(memory_space=pl.ANY)],
        out_specs=o_spec,
        out_shape=jax.ShapeDtypeStruct(stack.shape, stack.dtype),
        input_output_aliases={2: 0},
        scratch_shapes=[pltpu.VMEM((tm, tn), F32)],
        compiler_params=_params("parallel", "parallel", "arbitrary"),
    )(a, b, stack)


def _rmsnorm_fwd(x, w, *, dep=(), name):
    S, D = x.shape
    ts = _tile(S, 512, 16)

    def body(x_ref, w_ref, *rest):
        o_ref = rest[-1]
        xv = x_ref[...]
        r = lax.rsqrt(jnp.mean(xv * xv, axis=-1, keepdims=True) + NORM_EPS)
        o_ref[...] = (xv * r * w_ref[...]).astype(BF16)

    return pl.pallas_call(
        body, name=name, grid=(S // ts,),
        in_specs=[pl.BlockSpec((ts, D), lambda i: (i, 0)), pl.BlockSpec((1, D), lambda i: (0, 0))] + [ANY] * len(dep),
        out_specs=pl.BlockSpec((ts, D), lambda i: (i, 0)),
        out_shape=jax.ShapeDtypeStruct((S, D), BF16),
        compiler_params=_params("parallel"),
    )(x, w, *dep)


def _rmsnorm_bwd(x, w, dh, dres, *, name):
    S, D = x.shape
    ts = _tile(S, 512, 16)
    n = S // ts

    def body(x_ref, w_ref, dh_ref, dr_ref, dx_ref, dw_ref):
        i = pl.program_id(0)
        xv = x_ref[...]
        r = lax.rsqrt(jnp.mean(xv * xv, axis=-1, keepdims=True) + NORM_EPS)
        xh = xv * r
        d = dh_ref[...]
        g = d * w_ref[...]
        dx_ref[...] = dr_ref[...] + r * (g - xh * jnp.mean(g * xh, axis=-1, keepdims=True))
        part = _colsum(d * xh)

        @pl.when(i == 0)
        def _():
            dw_ref[...] = part

        @pl.when(i > 0)
        def _():
            dw_ref[...] += part

    row = pl.BlockSpec((ts, D), lambda i: (i, 0))
    vec = pl.BlockSpec((1, D), lambda i: (0, 0))
    return pl.pallas_call(
        body, name=name, grid=(n,),
        in_specs=[row, vec, row, row], out_specs=[row, vec],
        out_shape=[jax.ShapeDtypeStruct((S, D), F32), jax.ShapeDtypeStruct((1, D), F32)],
        compiler_params=_params("arbitrary"),
    )(x, w, dh, dres)


def _halo_specs(S, ts, tf, row_axis, rows=SUBLANES):
    g = ts // rows
    last = S // rows - 1
    col_axis = 1 - row_axis
    main = pl.BlockSpec((ts, tf), lambda *ij: (ij[row_axis], ij[col_axis]))
    prev = pl.BlockSpec((rows, tf), lambda *ij: (jnp.maximum(ij[row_axis] * g - 1, 0), ij[col_axis]))
    nxt = pl.BlockSpec((rows, tf), lambda *ij: (jnp.minimum((ij[row_axis] + 1) * g, last), ij[col_axis]))
    return [main, prev, nxt]


def _shifted(u, prev_ref, next_ref, i, n):
    ts = u.shape[0]
    rows = prev_ref.shape[0]
    rid = lax.broadcasted_iota(jnp.int32, u.shape, 0)
    before = jnp.where(i > 0, prev_ref[rows - 1:rows, :].astype(F32), 0.0)
    after = jnp.where(i < n - 1, next_ref[0:1, :].astype(F32), 0.0)
    um1 = jnp.where(rid == 0, before, pltpu.roll(u, 1, 0))
    up1 = jnp.where(rid == ts - 1, after, pltpu.roll(u, ts - 1, 0))
    return um1, up1


def _conv3(u, prev_ref, next_ref, w_ref, i, n):
    um1, up1 = _shifted(u, prev_ref, next_ref, i, n)
    return w_ref[0:1, :] * um1 + w_ref[1:2, :] * u + w_ref[2:3, :] * up1


def _conv_act_fwd(uv, ug, wv, wg, bv, bg, *, name):
    S, F = uv.shape
    ts = _tile(S, 512, 16)
    tf = _tile(F, 1408)
    n = S // ts

    def body(v_ref, vp_ref, vn_ref, g_ref, gp_ref, gn_ref, wv_ref, wg_ref, bv_ref, bg_ref, o_ref):
        i = pl.program_id(0)
        val = _conv3(v_ref[...], vp_ref, vn_ref, wv_ref, i, n) + bv_ref[...]
        gate = _conv3(g_ref[...], gp_ref, gn_ref, wg_ref, i, n) + bg_ref[...]
        o_ref[...] = (gate * _sigmoid(gate) * val).astype(BF16)

    halo = _halo_specs(S, ts, tf, 0)
    w3 = pl.BlockSpec((3, tf), lambda i, j: (0, j))
    b1 = pl.BlockSpec((1, tf), lambda i, j: (0, j))
    return pl.pallas_call(
        body, name=name, grid=(n, F // tf),
        in_specs=halo + halo + [w3, w3, b1, b1],
        out_specs=pl.BlockSpec((ts, tf), lambda i, j: (i, j)),
        out_shape=jax.ShapeDtypeStruct((S, F), BF16),
        compiler_params=_params("parallel", "parallel"),
    )(uv, uv, uv, ug, ug, ug, wv, wg, bv, bg)


def _conv_act_bwd(uv, ug, wv, wg, bv, bg, dact, *, name):
    S, F = uv.shape
    ts = _tile(S, 512, 16)
    tf = _tile(F, 1408)
    n = S // ts

    def body(v_ref, vp_ref, vn_ref, g_ref, gp_ref, gn_ref, wv_ref, wg_ref, bv_ref, bg_ref, da_ref,
             dv_ref, dg_ref, dwv_ref, dwg_ref, dbv_ref, dbg_ref):
        i = pl.program_id(1)
        uvv, ugv = v_ref[...], g_ref[...]
        vm1, vp1 = _shifted(uvv, vp_ref, vn_ref, i, n)
        gm1, gp1 = _shifted(ugv, gp_ref, gn_ref, i, n)
        val = wv_ref[0:1, :] * vm1 + wv_ref[1:2, :] * uvv + wv_ref[2:3, :] * vp1 + bv_ref[...]
        gate = wg_ref[0:1, :] * gm1 + wg_ref[1:2, :] * ugv + wg_ref[2:3, :] * gp1 + bg_ref[...]
        sg = _sigmoid(gate)
        da = da_ref[...]
        dval = da * (gate * sg)
        dgate = da * val * (sg * (1.0 + gate * (1.0 - sg)))
        dv_ref[...] = dval.astype(BF16)
        dg_ref[...] = dgate.astype(BF16)
        sums = [(dwv_ref, 0, vm1 * dval), (dwv_ref, 1, uvv * dval), (dwv_ref, 2, vp1 * dval),
                (dwg_ref, 0, gm1 * dgate), (dwg_ref, 1, ugv * dgate), (dwg_ref, 2, gp1 * dgate),
                (dbv_ref, 0, dval), (dbg_ref, 0, dgate)]
        parts = [(ref, r, _colsum(t)) for ref, r, t in sums]

        @pl.when(i == 0)
        def _():
            for ref, r, part in parts:
                ref[r:r + 1, :] = part

        @pl.when(i > 0)
        def _():
            for ref, r, part in parts:
                ref[r:r + 1, :] += part

    halo = _halo_specs(S, ts, tf, 1)
    w3 = pl.BlockSpec((3, tf), lambda j, i: (0, j))
    b1 = pl.BlockSpec((1, tf), lambda j, i: (0, j))
    blk = pl.BlockSpec((ts, tf), lambda j, i: (i, j))
    return pl.pallas_call(
        body, name=name, grid=(F // tf, n),
        in_specs=halo + halo + [w3, w3, b1, b1, blk],
        out_specs=[blk, blk, w3, w3, b1, b1],
        out_shape=[jax.ShapeDtypeStruct((S, F), BF16), jax.ShapeDtypeStruct((S, F), BF16),
                   jax.ShapeDtypeStruct((3, F), F32), jax.ShapeDtypeStruct((3, F), F32),
                   jax.ShapeDtypeStruct((1, F), F32), jax.ShapeDtypeStruct((1, F), F32)],
        compiler_params=_params("parallel", "arbitrary"),
    )(uv, uv, uv, ug, ug, ug, wv, wg, bv, bg, dact)


def _conv_t(duc, w, *, name):
    S, F = duc.shape
    ts = _tile(S, 512, 16)
    tf = _tile(F, 1408)
    n = S // ts

    def body(d_ref, dp_ref, dn_ref, w_ref, o_ref):
        i = pl.program_id(0)
        d = d_ref[...].astype(F32)
        dm1, dp1 = _shifted(d, dp_ref, dn_ref, i, n)
        o_ref[...] = (w_ref[0:1, :] * dp1 + w_ref[1:2, :] * d + w_ref[2:3, :] * dm1).astype(BF16)

    return pl.pallas_call(
        body, name=name, grid=(n, F // tf),
        in_specs=_halo_specs(S, ts, tf, 0, rows=16) + [pl.BlockSpec((3, tf), lambda i, j: (0, j))],
        out_specs=pl.BlockSpec((ts, tf), lambda i, j: (i, j)),
        out_shape=jax.ShapeDtypeStruct((S, F), BF16),
        compiler_params=_params("parallel", "parallel"),
    )(duc, duc, duc, w)


N_QK = ATTN_Q_HEADS + ATTN_KV_HEADS
QKV_DIM = (ATTN_Q_HEADS + 2 * ATTN_KV_HEADS) * ATTN_HD


def _head(hd):
    return slice(hd * ATTN_HD, (hd + 1) * ATTN_HD)


def _attn_prep(proj, cs, sn, qn, kn, *, name):
    S = proj.shape[0]
    ts = _tile(S, 256, 16)

    def body(p_ref, c_ref, s_ref, qn_ref, kn_ref, o_ref):
        c, s = c_ref[...], s_ref[...]
        for hd in range(N_QK):
            xv = p_ref[:, _head(hd)]
            w = qn_ref[...] if hd < ATTN_Q_HEADS else kn_ref[...]
            r = lax.rsqrt(jnp.mean(xv * xv, axis=-1, keepdims=True) + NORM_EPS)
            nrm = xv * r * w
            o_ref[:, _head(hd)] = (nrm * c + pltpu.roll(nrm, ATTN_HD // 2, 1) * s).astype(BF16)
        o_ref[:, N_QK * ATTN_HD:] = p_ref[:, N_QK * ATTN_HD:].astype(BF16)

    row = pl.BlockSpec((ts, QKV_DIM), lambda i: (i, 0))
    rot = pl.BlockSpec((ts, ATTN_HD), lambda i: (i, 0))
    vec = pl.BlockSpec((1, ATTN_HD), lambda i: (0, 0))
    return pl.pallas_call(
        body, name=name, grid=(S // ts,),
        in_specs=[row, rot, rot, vec, vec], out_specs=row,
        out_shape=jax.ShapeDtypeStruct((S, QKV_DIM), BF16),
        compiler_params=_params("parallel"),
    )(proj, cs, sn, qn, kn)


def _attn_prep_bwd(proj, dq, dk, dv, cs, sn, qn, kn, *, name):
    S = proj.shape[0]
    ts = _tile(S, 256, 16)
    nq = ATTN_Q_HEADS * ATTN_HD
    nkv = ATTN_KV_HEADS * ATTN_HD

    def body(p_ref, dq_ref, dk_ref, dv_ref, c_ref, s_ref, qn_ref, kn_ref, o_ref, dqn_ref, dkn_ref):
        i = pl.program_id(0)
        c, s = c_ref[...], s_ref[...]
        acc = [jnp.zeros((1, ATTN_HD), F32), jnp.zeros((1, ATTN_HD), F32)]
        for hd in range(N_QK):
            is_k = hd >= ATTN_Q_HEADS
            xv = p_ref[:, _head(hd)]
            w = kn_ref[...] if is_k else qn_ref[...]
            r = lax.rsqrt(jnp.mean(xv * xv, axis=-1, keepdims=True) + NORM_EPS)
            xh = xv * r
            dout = dk_ref[:, _head(hd - ATTN_Q_HEADS)] if is_k else dq_ref[:, _head(hd)]
            dn = dout * c + pltpu.roll(dout * s, ATTN_HD // 2, 1)
            acc[int(is_k)] = acc[int(is_k)] + _colsum(dn * xh)
            g = dn * w
            o_ref[:, _head(hd)] = (r * (g - xh * jnp.mean(g * xh, axis=-1, keepdims=True))).astype(BF16)
        o_ref[:, N_QK * ATTN_HD:] = dv_ref[...].astype(BF16)

        @pl.when(i == 0)
        def _():
            dqn_ref[...] = acc[0]
            dkn_ref[...] = acc[1]

        @pl.when(i > 0)
        def _():
            dqn_ref[...] += acc[0]
            dkn_ref[...] += acc[1]

    row = pl.BlockSpec((ts, QKV_DIM), lambda i: (i, 0))
    rot = pl.BlockSpec((ts, ATTN_HD), lambda i: (i, 0))
    vec = pl.BlockSpec((1, ATTN_HD), lambda i: (0, 0))
    return pl.pallas_call(
        body, name=name, grid=(S // ts,),
        in_specs=[row, pl.BlockSpec((ts, nq), lambda i: (i, 0)), pl.BlockSpec((ts, nkv), lambda i: (i, 0)),
                  pl.BlockSpec((ts, nkv), lambda i: (i, 0)), rot, rot, vec, vec],
        out_specs=[row, vec, vec],
        out_shape=[jax.ShapeDtypeStruct((S, QKV_DIM), BF16), jax.ShapeDtypeStruct((1, ATTN_HD), F32),
                   jax.ShapeDtypeStruct((1, ATTN_HD), F32)],
        compiler_params=_params("arbitrary"),
    )(proj, dq, dk, dv, cs, sn, qn, kn)


ATTN_SCALE = ATTN_HD ** -0.5


def _softmax_of(s):
    e = jnp.exp2((s - jnp.max(s, axis=-1, keepdims=True)) * (ATTN_SCALE * math.log2(math.e)))
    return e, 1.0 / jnp.sum(e, axis=-1, keepdims=True)


def _softmax_parts(q, k):
    return _softmax_of(_dot_nt(q, k))


def _attn_fwd(qkv, *, name):
    S = qkv.shape[0]
    tq = _tile(S, 512, 16)
    sub = _tile(tq, 256, 16)

    def body(q_ref, k_ref, v_ref, o_ref):
        k, v = k_ref[...], v_ref[...]
        for r in range(tq // sub):
            rows = slice(r * sub, (r + 1) * sub)
            e, rl = _softmax_parts(q_ref[rows, :], k)
            o_ref[rows, :] = (_dot(e.astype(BF16), v) * rl).astype(BF16)

    return pl.pallas_call(
        body, name=name, grid=(ATTN_Q_HEADS, S // tq),
        in_specs=[pl.BlockSpec((tq, ATTN_HD), lambda h, i: (i, h)),
                  pl.BlockSpec((S, ATTN_HD), lambda h, i: (0, ATTN_Q_HEADS + h // ATTN_GROUP)),
                  pl.BlockSpec((S, ATTN_HD), lambda h, i: (0, N_QK + h // ATTN_GROUP))],
        out_specs=pl.BlockSpec((tq, ATTN_HD), lambda h, i: (i, h)),
        out_shape=jax.ShapeDtypeStruct((S, ATTN_Q_HEADS * ATTN_HD), BF16),
        compiler_params=_params("parallel", "parallel"),
    )(qkv, qkv, qkv)


def _attn_bwd(qkv, do, *, name):
    S = qkv.shape[0]
    tq = _tile(S, 256, 16)
    n = S // tq

    def body(q_ref, do_ref, qn_ref, don_ref, k_ref, v_ref, dq_ref, dk_ref, dv_ref, s_a, dp_a, s_b, dp_b):
        i = pl.program_id(2)

        @pl.when(i == 0)
        def _():
            s_a[...] = _dot_nt(q_ref[...], k_ref[...])
            dp_a[...] = _dot_nt(do_ref[...], v_ref[...])

            @pl.when(pl.program_id(1) == 0)
            def _():
                dk_ref[...] = jnp.zeros_like(dk_ref)
                dv_ref[...] = jnp.zeros_like(dv_ref)

        def step(s_cur, dp_cur, s_next, dp_next):
            k, v = k_ref[...], v_ref[...]
            q, dov = q_ref[...], do_ref[...]
            s_next[...] = _dot_nt(qn_ref[...], k)
            dp_next[...] = _dot_nt(don_ref[...], v)
            e, rl = _softmax_of(s_cur[...])
            dp = dp_cur[...]
            delta = jnp.sum(e * dp, axis=-1, keepdims=True) * rl
            dsb = (e * (dp - delta) * (rl * ATTN_SCALE)).astype(BF16)
            dq_ref[...] = _dot(dsb, k)
            dk_ref[...] += _dot_tn(dsb, q)
            dv_ref[...] += _dot_tn(e.astype(BF16), (dov.astype(F32) * rl).astype(BF16))

        @pl.when(i % 2 == 0)
        def _():
            step(s_a, dp_a, s_b, dp_b)

        @pl.when(i % 2 == 1)
        def _():
            step(s_b, dp_b, s_a, dp_a)

    qblk = pl.BlockSpec((tq, ATTN_HD), lambda kv, g, i: (i, kv * ATTN_GROUP + g))
    qnext = pl.BlockSpec((tq, ATTN_HD), lambda kv, g, i: (jnp.minimum(i + 1, n - 1), kv * ATTN_GROUP + g))
    kvacc = pl.BlockSpec((S, ATTN_HD), lambda kv, g, i: (0, kv))
    return pl.pallas_call(
        body, name=name, grid=(ATTN_KV_HEADS, ATTN_GROUP, n),
        in_specs=[qblk, qblk, qnext, qnext,
                  pl.BlockSpec((S, ATTN_HD), lambda kv, g, i: (0, ATTN_Q_HEADS + kv)),
                  pl.BlockSpec((S, ATTN_HD), lambda kv, g, i: (0, N_QK + kv))],
        out_specs=[qblk, kvacc, kvacc],
        out_shape=[jax.ShapeDtypeStruct((S, ATTN_Q_HEADS * ATTN_HD), F32),
                   jax.ShapeDtypeStruct((S, ATTN_KV_HEADS * ATTN_HD), F32),
                   jax.ShapeDtypeStruct((S, ATTN_KV_HEADS * ATTN_HD), F32)],
        scratch_shapes=[pltpu.VMEM((tq, S), F32) for _ in range(4)],
        compiler_params=_params("parallel", "arbitrary", "arbitrary"),
    )(qkv, do, qkv, do, qkv, qkv)


GLA_KD = GLA_HEADS * GLA_DK
GLA_VD = GLA_HEADS * GLA_DV
GLA_PROJ = 2 * GLA_KD + 2 * GLA_VD + LANES
GLA_SCALE = GLA_DK ** -0.5


def _split3(x):
    hi = x.astype(BF16)
    r1 = x - hi.astype(F32)
    mid = r1.astype(BF16)
    lo = (r1 - mid.astype(F32)).astype(BF16)
    return hi, mid, lo


def _cumdot(t, x):
    hi, mid, lo = _split3(x)
    return _dot(t, hi) + _dot(t, mid) + _dot(t, lo)


def _gla_masks(d):
    c = GLA_CHUNK
    row = lax.broadcasted_iota(jnp.int32, (c, c), 0)
    col = lax.broadcasted_iota(jnp.int32, (c, c), 1)
    lower, upper = col <= row, col >= row
    if d == 0:
        return lower.astype(BF16), upper.astype(BF16), lower
    return upper.astype(BF16), lower.astype(BF16), col > row


def _gla_decay(lg, bias, cum, d):
    xl = lg + bias
    la = _log_sigmoid(xl) * (1.0 / GLA_GATE_NORMALIZER)
    b = _cumdot(cum, la)
    b_end = b[GLA_CHUNK - 1:GLA_CHUNK, :] if d == 0 else b[0:1, :]
    return xl, b, b_end


def _gla_specs(S, n):
    c = GLA_CHUNK
    up = lambda i: i
    down = lambda i: n - 1 - i
    def specs(order):
        return dict(
            q=pl.BlockSpec((c, GLA_KD), lambda i: (order(i), 0)),
            k=pl.BlockSpec((c, GLA_KD), lambda i: (order(i), 1)),
            v=pl.BlockSpec((c, GLA_VD), lambda i: (order(i), 1)),
            st=pl.BlockSpec((1, GLA_HEADS, GLA_DV, GLA_DK), lambda i: (order(i), 0, 0, 0)),
            wide=pl.BlockSpec((c, GLA_VD), lambda i: (order(i), 0)),
            qkv=pl.BlockSpec((c, 2 * GLA_KD + GLA_VD), lambda i: (order(i), 0)),
        )
    return specs(up), specs(down), up, down


def _gla_fwd(proj, logits, bias, *, name):
    S = proj.shape[0]
    c = GLA_CHUNK
    n = S // c
    su, sd, up, down = _gla_specs(S, n)

    def body(qf, kf, vf, lf, qb, kb, vb, lb, bias_ref, of, ob, sf, sb, st):
        @pl.when(pl.program_id(0) == 0)
        def _():
            st[...] = jnp.zeros_like(st)

        for d, (q_r, k_r, v_r, l_r, o_r, s_r) in enumerate(((qf, kf, vf, lf, of, sf), (qb, kb, vb, lb, ob, sb))):
            cum, _, mask = _gla_masks(d)
            _, b, b_end = _gla_decay(l_r[...], bias_ref[:, d * GLA_KD:(d + 1) * GLA_KD], cum, d)
            dend = jnp.exp(b_end)
            k = k_r[...]
            qd = (q_r[...] * GLA_SCALE * jnp.exp(b)).astype(BF16)
            ki = (k * jnp.exp(-b)).astype(BF16)
            ke = (k * jnp.exp(b_end - b)).astype(BF16)
            for h in range(GLA_HEADS):
                ks = slice(h * GLA_DK, (h + 1) * GLA_DK)
                vs = slice(h * GLA_DV, (h + 1) * GLA_DV)
                stp = st[d * GLA_HEADS + h]
                s_r[0, h] = stp
                v = v_r[:, vs].astype(BF16)
                att = jnp.where(mask, _dot_nt(qd[:, ks], ki[:, ks]), 0.0).astype(BF16)
                o_r[:, vs] = _dot(att, v) + _dot_nt(qd[:, ks], stp.astype(BF16))
                st[d * GLA_HEADS + h] = stp * dend[:, ks] + _dot_tn(v, ke[:, ks])

    lg_f = pl.BlockSpec((c, GLA_KD), lambda i: (up(i), 0))
    lg_b = pl.BlockSpec((c, GLA_KD), lambda i: (down(i), 1))
    return pl.pallas_call(
        body, name=name, grid=(n,),
        in_specs=[su["q"], su["k"], su["v"], lg_f, sd["q"], sd["k"], sd["v"], lg_b,
                  pl.BlockSpec((1, 2 * GLA_KD), lambda i: (0, 0))],
        out_specs=[su["wide"], sd["wide"], su["st"], sd["st"]],
        out_shape=[jax.ShapeDtypeStruct((S, GLA_VD), F32), jax.ShapeDtypeStruct((S, GLA_VD), F32),
                   jax.ShapeDtypeStruct((n, GLA_HEADS, GLA_DV, GLA_DK), F32),
                   jax.ShapeDtypeStruct((n, GLA_HEADS, GLA_DV, GLA_DK), F32)],
        scratch_shapes=[pltpu.VMEM((2 * GLA_HEADS, GLA_DV, GLA_DK), F32)],
        compiler_params=_params("arbitrary"),
    )(proj, proj, proj, logits, proj, proj, proj, logits, bias)


def _gla_bwd(proj, logits, bias, sf, sb, do, *, name):
    S = proj.shape[0]
    c = GLA_CHUNK
    n = S // c
    su, sd, up, down = _gla_specs(S, n)

    def body(qf, kf, vf, lf, stf, dof, qb, kb, vb, lb, stb, dob, bias_ref,
             dqkv_f, dlg_f, dqkv_b, dlg_b, dbias, dst):
        first = pl.program_id(0) == 0

        @pl.when(first)
        def _():
            dst[...] = jnp.zeros_like(dst)

        dbias_parts = []
        for d, (q_r, k_r, v_r, l_r, s_r, do_r, dqkv_r, dlg_r) in enumerate(
                ((qf, kf, vf, lf, stf, dof, dqkv_f, dlg_f), (qb, kb, vb, lb, stb, dob, dqkv_b, dlg_b))):
            cum, cum_t, mask = _gla_masks(d)
            xl, b, b_end = _gla_decay(l_r[...], bias_ref[:, d * GLA_KD:(d + 1) * GLA_KD], cum, d)
            e, ei, ee, dend = jnp.exp(b), jnp.exp(-b), jnp.exp(b_end - b), jnp.exp(b_end)
            k = k_r[...]
            qd32 = q_r[...] * GLA_SCALE * e
            ki32 = k * ei
            ke32 = k * ee
            qd, ki, ke = qd32.astype(BF16), ki32.astype(BF16), ke32.astype(BF16)
            db_parts, dbe_parts = [], []
            for h in range(GLA_HEADS):
                ks = slice(h * GLA_DK, (h + 1) * GLA_DK)
                vs = slice(h * GLA_DV, (h + 1) * GLA_DV)
                stp = s_r[0, h]
                dstn = dst[d * GLA_HEADS + h]
                dstn_b = dstn.astype(BF16)
                v = v_r[:, vs].astype(BF16)
                dov = do_r[:, vs]
                att = jnp.where(mask, _dot_nt(qd[:, ks], ki[:, ks]), 0.0).astype(BF16)
                datt = jnp.where(mask, _dot_nt(dov, v), 0.0).astype(BF16)
                dqkv_r[:, 2 * GLA_KD + h * GLA_DV:2 * GLA_KD + (h + 1) * GLA_DV] = (
                    _dot_tn(att, dov) + _dot_nt(ke[:, ks], dstn_b))
                dqd = _dot(datt, ki[:, ks]) + _dot(dov, stp.astype(BF16))
                dki = _dot_tn(datt, qd[:, ks])
                dke = _dot(v, dstn_b)
                d_dend = _colsum(stp * dstn)
                dst[d * GLA_HEADS + h] = _dot_tn(dov, qd[:, ks]) + dstn * dend[:, ks]
                dqkv_r[:, ks] = dqd * e[:, ks] * GLA_SCALE
                dqkv_r[:, GLA_KD + h * GLA_DK:GLA_KD + (h + 1) * GLA_DK] = dki * ei[:, ks] + dke * ee[:, ks]
                dke_ke = dke * ke32[:, ks]
                db_parts.append(dqd * qd32[:, ks] - dki * ki32[:, ks] - dke_ke)
                dbe_parts.append(_colsum(dke_ke) + d_dend * dend[:, ks])
            db = jnp.concatenate(db_parts, axis=1)
            db_end = jnp.concatenate(dbe_parts, axis=1)
            dla = _cumdot(cum_t, db) + db_end
            dlg = dla * (1.0 / GLA_GATE_NORMALIZER) * _sigmoid(-xl)
            dlg_r[...] = dlg
            dbias_parts.append(_colsum(dlg))
        dbv = jnp.concatenate(dbias_parts, axis=1)

        @pl.when(first)
        def _():
            dbias[...] = dbv

        @pl.when(jnp.logical_not(first))
        def _():
            dbias[...] += dbv

    lg_f = pl.BlockSpec((c, GLA_KD), lambda i: (down(i), 0))
    lg_b = pl.BlockSpec((c, GLA_KD), lambda i: (up(i), 1))
    dlg_f = pl.BlockSpec((c, GLA_KD), lambda i: (down(i), 0))
    dlg_b = pl.BlockSpec((c, GLA_KD), lambda i: (up(i), 0))
    return pl.pallas_call(
        body, name=name, grid=(n,),
        in_specs=[sd["q"], sd["k"], sd["v"], lg_f, sd["st"], sd["wide"],
                  su["q"], su["k"], su["v"], lg_b, su["st"], su["wide"],
                  pl.BlockSpec((1, 2 * GLA_KD), lambda i: (0, 0))],
        out_specs=[sd["qkv"], dlg_f, su["qkv"], dlg_b, pl.BlockSpec((1, 2 * GLA_KD), lambda i: (0, 0))],
        out_shape=[jax.ShapeDtypeStruct((S, 2 * GLA_KD + GLA_VD), F32), jax.ShapeDtypeStruct((S, GLA_KD), F32),
                   jax.ShapeDtypeStruct((S, 2 * GLA_KD + GLA_VD), F32), jax.ShapeDtypeStruct((S, GLA_KD), F32),
                   jax.ShapeDtypeStruct((1, 2 * GLA_KD), F32)],
        scratch_shapes=[pltpu.VMEM((2 * GLA_HEADS, GLA_DV, GLA_DK), F32)],
        compiler_params=_params("arbitrary"),
    )(proj, proj, proj, logits, sf, do, proj, proj, proj, logits, sb, do, bias)


def _gla_gate_fwd(of, ob, proj, w, *, name):
    S = of.shape[0]
    ts = _tile(S, 256, 16)

    def body(of_ref, ob_ref, g_ref, w_ref, y_ref):
        for h in range(GLA_HEADS):
            vs = slice(h * GLA_DV, (h + 1) * GLA_DV)
            o = of_ref[:, vs] + ob_ref[:, vs]
            r = lax.rsqrt(jnp.mean(o * o, axis=-1, keepdims=True) + NORM_EPS)
            g = g_ref[:, vs]
            y_ref[:, vs] = (o * r * w_ref[...] * (g * _sigmoid(g))).astype(BF16)

    wide = pl.BlockSpec((ts, GLA_VD), lambda i: (i, 0))
    return pl.pallas_call(
        body, name=name, grid=(S // ts,),
        in_specs=[wide, wide, pl.BlockSpec((ts, GLA_VD), lambda i: (i, 2)), pl.BlockSpec((1, GLA_DV), lambda i: (0, 0))],
        out_specs=wide,
        out_shape=jax.ShapeDtypeStruct((S, GLA_VD), BF16),
        compiler_params=_params("parallel"),
    )(of, ob, proj, w)


def _gla_gate_bwd(of, ob, proj, w, dy, *, name):
    S = of.shape[0]
    ts = _tile(S, 256, 16)

    def body(of_ref, ob_ref, g_ref, w_ref, dy_ref, do_ref, dg_ref, dw_ref):
        i = pl.program_id(0)
        acc = jnp.zeros((1, GLA_DV), F32)
        for h in range(GLA_HEADS):
            vs = slice(h * GLA_DV, (h + 1) * GLA_DV)
            o = of_ref[:, vs] + ob_ref[:, vs]
            r = lax.rsqrt(jnp.mean(o * o, axis=-1, keepdims=True) + NORM_EPS)
            oh = o * r
            g = g_ref[:, vs]
            sg = _sigmoid(g)
            dyv = dy_ref[:, vs]
            dn = dyv * (g * sg)
            dg_ref[:, vs] = dyv * (oh * w_ref[...]) * (sg * (1.0 + g * (1.0 - sg)))
            acc = acc + _colsum(dn * oh)
            gg = dn * w_ref[...]
            do_ref[:, vs] = (r * (gg - oh * jnp.mean(gg * oh, axis=-1, keepdims=True))).astype(BF16)

        @pl.when(i == 0)
        def _():
            dw_ref[...] = acc

        @pl.when(i > 0)
        def _():
            dw_ref[...] += acc

    wide = pl.BlockSpec((ts, GLA_VD), lambda i: (i, 0))
    vec = pl.BlockSpec((1, GLA_DV), lambda i: (0, 0))
    return pl.pallas_call(
        body, name=name, grid=(S // ts,),
        in_specs=[wide, wide, pl.BlockSpec((ts, GLA_VD), lambda i: (i, 2)), vec, wide],
        out_specs=[wide, wide, vec],
        out_shape=[jax.ShapeDtypeStruct((S, GLA_VD), BF16), jax.ShapeDtypeStruct((S, GLA_VD), F32),
                   jax.ShapeDtypeStruct((1, GLA_DV), F32)],
        compiler_params=_params("arbitrary"),
    )(of, ob, proj, w, dy)


def _gla_combine(dqkv_f, dqkv_b, dg, dr, *, name):
    S = dg.shape[0]
    ts = _tile(S, 512, 16)
    nqkv = 2 * GLA_KD + GLA_VD

    def body(f_ref, b_ref, g_ref, r_ref, o_ref):
        o_ref[:, :nqkv] = (f_ref[...] + b_ref[...]).astype(BF16)
        o_ref[:, nqkv:nqkv + GLA_VD] = g_ref[...].astype(BF16)
        o_ref[:, nqkv + GLA_VD:] = r_ref[...].astype(BF16)

    return pl.pallas_call(
        body, name=name, grid=(S // ts,),
        in_specs=[pl.BlockSpec((ts, nqkv), lambda i: (i, 0)), pl.BlockSpec((ts, nqkv), lambda i: (i, 0)),
                  pl.BlockSpec((ts, GLA_VD), lambda i: (i, 0)), pl.BlockSpec((ts, LANES), lambda i: (i, 0))],
        out_specs=pl.BlockSpec((ts, GLA_PROJ), lambda i: (i, 0)),
        out_shape=jax.ShapeDtypeStruct((S, GLA_PROJ), BF16),
        compiler_params=_params("parallel"),
    )(dqkv_f, dqkv_b, dg, dr)


def _loss_head(y, t, *, name):
    S, D = y.shape
    ts = _tile(S, 512, 16)
    n = S // ts

    def body(y_ref, t_ref, dy_ref, l_ref, acc):
        i = pl.program_id(0)
        diff = y_ref[...] - t_ref[...]
        dy_ref[...] = diff * (1.0 / D)
        part = _colsum(diff * diff)

        @pl.when(i == 0)
        def _():
            acc[...] = part

        @pl.when(i > 0)
        def _():
            acc[...] += part

        @pl.when(i == n - 1)
        def _():
            l_ref[...] = jnp.full(l_ref.shape, 0.5 / D, F32) * jnp.sum(acc[...])

    row = pl.BlockSpec((ts, D), lambda i: (i, 0))
    return pl.pallas_call(
        body, name=name, grid=(n,),
        in_specs=[row, row],
        out_specs=[row, pl.BlockSpec((SUBLANES, LANES), lambda i: (0, 0))],
        out_shape=[jax.ShapeDtypeStruct((S, D), F32), jax.ShapeDtypeStruct((SUBLANES, LANES), F32)],
        scratch_shapes=[pltpu.VMEM((1, D), F32)],
        compiler_params=_params("arbitrary"),
    )(y, t)


def _adamw(w, g, m, v, *, name):
    R, C = w.shape
    tr = _tile(R, 256, SUBLANES)

    def body(w_ref, g_ref, m_ref, v_ref, d_ref, nm_ref, nv_ref):
        gv = g_ref[...]
        nm = ADAM_B1 * m_ref[...] + (1.0 - ADAM_B1) * gv
        nv = ADAM_B2 * v_ref[...] + (1.0 - ADAM_B2) * (gv * gv)
        m_hat = nm / (1.0 - ADAM_B1 ** ADAM_STEP)
        v_hat = nv / (1.0 - ADAM_B2 ** ADAM_STEP)
        d_ref[...] = -ADAM_LR * (m_hat / (jnp.sqrt(v_hat) + ADAM_EPS) + ADAM_WD * w_ref[...])
        nm_ref[...] = nm
        nv_ref[...] = nv

    blk = pl.BlockSpec((tr, C), lambda i: (i, 0))
    shp = jax.ShapeDtypeStruct((R, C), F32)
    return pl.pallas_call(
        body, name=name, grid=(R // tr,),
        in_specs=[blk] * 4, out_specs=[blk] * 3, out_shape=[shp] * 3,
        compiler_params=_params("parallel"),
    )(w, g, m, v)


def _place():
    return lax.axis_index("x"), lax.axis_index("y"), lax.axis_index("c")


def _all_gather(block, *, name):
    R, L = block.shape

    def body(x_ref, out_ref, send_sems, recv_sems, local_sem):
        x, y, c = _place()
        me, sibling = (x, y, c), (x, y, 1 - c)
        chips = [(1 - x, y), (x, 1 - y), (1 - x, 1 - y)]

        def slot(px, py, pc):
            return out_ref.at[4 * px + 2 * py + pc]

        def copy(k, blk, to, src=None):
            return pltpu.make_async_remote_copy(
                src_ref=slot(*blk) if src is None else src, dst_ref=slot(*blk),
                send_sem=send_sems.at[k], recv_sem=recv_sems.at[k], device_id=to, device_id_type=MESH)

        mine = pltpu.make_async_copy(x_ref, slot(*me), local_sem)
        mine.start()
        first = [copy(0, me, sibling, src=x_ref)]
        first += [copy(1 + j, me, (*chip, c), src=x_ref) for j, chip in enumerate(chips)]
        for cp in first:
            cp.start()
        passed = [copy(4 + j, (*chip, c), sibling) for j, chip in enumerate(chips)]
        for j, chip in enumerate(chips):
            copy(1 + j, (*chip, c), me).wait_recv()
            passed[j].start()
        copy(0, sibling, me).wait_recv()
        for j, chip in enumerate(chips):
            copy(4 + j, (*chip, 1 - c), me).wait_recv()
        for cp in first + passed:
            cp.wait_send()
        mine.wait()

    return pl.pallas_call(
        body, name=name, in_specs=[ANY], out_specs=ANY,
        out_shape=jax.ShapeDtypeStruct((N_DEV, R, L), block.dtype),
        scratch_shapes=[pltpu.SemaphoreType.DMA((7,)), pltpu.SemaphoreType.DMA((7,)), pltpu.SemaphoreType.DMA],
    )(block)


HBM_SPEC = pl.BlockSpec(memory_space=pltpu.HBM)
SEM_SPEC = pl.BlockSpec(memory_space=pltpu.SEMAPHORE)
DATAFLOW = pltpu.SideEffectType.DATAFLOW_SIDE_EFFECTING


def _split_start(plan, srcs, lands, *, dep=(), name):
    make_copies, count = plan
    ns, nb = len(srcs), len(srcs) + len(lands)
    bufs = [pltpu.with_memory_space_constraint(a, pltpu.HBM) for a in list(srcs) + list(lands)]
    n_in = nb + len(dep)

    def body(*refs):
        send_sems, recv_sems, token = refs[n_in], refs[n_in + 1], refs[-1]
        for cp in make_copies(refs[:ns], refs[ns:nb], send_sems, recv_sems):
            cp.start()
        token[...] = jnp.zeros_like(token)

    outs = pl.pallas_call(
        body, name=name, in_specs=[HBM_SPEC] * nb + [ANY] * len(dep),
        out_specs=(SEM_SPEC, SEM_SPEC, *[HBM_SPEC] * nb, pl.BlockSpec(memory_space=pltpu.VMEM)),
        out_shape=(pltpu.SemaphoreType.DMA((count,)), pltpu.SemaphoreType.DMA((count,)),
                   *[pltpu.HBM(a.shape, a.dtype) for a in bufs], jax.ShapeDtypeStruct((SUBLANES, LANES), F32)),
        input_output_aliases={i: 2 + i for i in range(nb)},
        compiler_params=pltpu.CompilerParams(has_side_effects=DATAFLOW),
    )(*bufs, *dep)
    return dict(plan=plan, ns=ns, send=outs[0], recv=outs[1], bufs=list(outs[2:2 + nb]), token=outs[-1])


def _split_wait(started, after, *, name):
    make_copies, _ = started["plan"]
    ns, nb = started["ns"], len(started["bufs"])
    after = tuple(after) if isinstance(after, (tuple, list)) else (after,)

    def body(*refs):
        for cp in make_copies(refs[:ns], refs[ns:nb], refs[nb], refs[nb + 1]):
            cp.wait_send()
            cp.wait_recv()

    outs = pl.pallas_call(
        body, name=name, in_specs=[HBM_SPEC] * nb + [SEM_SPEC, SEM_SPEC] + [ANY] * len(after),
        out_specs=[HBM_SPEC] * nb,
        out_shape=[pltpu.HBM(a.shape, a.dtype) for a in started["bufs"]],
        input_output_aliases={i: i for i in range(nb)},
        compiler_params=pltpu.CompilerParams(has_side_effects=DATAFLOW),
    )(*started["bufs"], started["send"], started["recv"], *after)
    return list(outs[:ns]), list(outs[ns:])


def _remote(src, dst, send_sems, recv_sems, k, to):
    return pltpu.make_async_remote_copy(src_ref=src, dst_ref=dst, send_sem=send_sems.at[k], recv_sem=recv_sems.at[k],
                                        device_id=to, device_id_type=MESH)


def _other_chips(x, y):
    return [(1 - x, y), (x, 1 - y), (1 - x, 1 - y)]


def _gather_send_plan(n):
    def make(srcs, lands, send_sems, recv_sems):
        x, y, c = _place()
        targets = [(x, y, 1 - c)] + [(cx, cy, c) for cx, cy in _other_chips(x, y)]
        return [_remote(srcs[t], lands[t].at[4 * x + 2 * y + c], send_sems, recv_sems, 4 * t + k, to)
                for t in range(n) for k, to in enumerate(targets)]
    return make, 4 * n


def _gather_pass_plan(n):
    def make(srcs, lands, send_sems, recv_sems):
        x, y, c = _place()
        cps = []
        for t in range(n):
            for j, (cx, cy) in enumerate(_other_chips(x, y)):
                slot = lands[t].at[4 * cx + 2 * cy + c]
                cps.append(_remote(slot, slot, send_sems, recv_sems, 3 * t + j, (x, y, 1 - c)))
        return cps
    return make, 3 * n


def _reduce_sibling_plan(n):
    def make(srcs, lands, send_sems, recv_sems):
        x, y, c = _place()
        return [_remote(srcs[t].at[2 * k + 1 - c], lands[t].at[k], send_sems, recv_sems, 4 * t + k, (x, y, 1 - c))
                for t in range(n) for k in range(4)]
    return make, 4 * n


def _reduce_chip_plan(n):
    def make(srcs, lands, send_sems, recv_sems):
        x, y, c = _place()
        return [_remote(srcs[t].at[2 * cx + cy], lands[t].at[2 * x + y], send_sems, recv_sems, 3 * t + j, (cx, cy, c))
                for t in range(n) for j, (cx, cy) in enumerate(_other_chips(x, y))]
    return make, 3 * n


def _unshard_cols(g, own, dev_idx, groups, width, *, name):
    _, A, Bs = g.shape
    ta = _tile(A, 256, 16)

    def body(dev_ref, g_ref, own_ref, *o_refs):
        for o_ref, devs in zip(o_refs, groups):
            for q, d in enumerate(devs):
                o_ref[:, q * Bs:(q + 1) * Bs] = jnp.where(dev_ref[0] == d, own_ref[...], g_ref[d])
            if len(devs) * Bs < width:
                o_ref[:, len(devs) * Bs:] = jnp.zeros((ta, width - len(devs) * Bs), g.dtype)

    return pl.pallas_call(
        body, name=name,
        grid_spec=pltpu.PrefetchScalarGridSpec(
            num_scalar_prefetch=1, grid=(A // ta,),
            in_specs=[pl.BlockSpec((N_DEV, ta, Bs), lambda i, d: (0, i, 0)), pl.BlockSpec((ta, Bs), lambda i, d: (i, 0))],
            out_specs=[pl.BlockSpec((ta, width), lambda i, d: (i, 0)) for _ in groups]),
        out_shape=[jax.ShapeDtypeStruct((A, width), g.dtype) for _ in groups],
        compiler_params=_params("parallel"),
    )(dev_idx, g, own)


def _place_own(g, own, dev_idx, *, name):
    _, As, B = g.shape
    ta = _tile(As, 256, 16)

    def body(dev_ref, _, own_ref, o_ref):
        o_ref[...] = own_ref[...]

    out = pl.pallas_call(
        body, name=name,
        grid_spec=pltpu.PrefetchScalarGridSpec(
            num_scalar_prefetch=1, grid=(As // ta,),
            in_specs=[ANY, pl.BlockSpec((ta, B), lambda i, d: (i, 0))],
            out_specs=pl.BlockSpec((None, ta, B), lambda i, d: (d[0], i, 0))),
        out_shape=jax.ShapeDtypeStruct(g.shape, g.dtype),
        input_output_aliases={1: 0},
        compiler_params=_params("parallel"),
    )(dev_idx, g, own)
    return out.reshape(N_DEV * As, B)


def _add_sibling(g, buf, c_idx, *, name):
    _, A, B = g.shape
    ta = _tile(A, 256, 16)

    def body(c_ref, g_ref, b_ref, o_ref):
        o_ref[...] = (g_ref[...].astype(F32) + b_ref[...].astype(F32)).astype(BF16)

    blk = pl.BlockSpec((None, ta, B), lambda k, i, c_ref: (k, i, 0))
    return pl.pallas_call(
        body, name=name,
        grid_spec=pltpu.PrefetchScalarGridSpec(
            num_scalar_prefetch=1, grid=(4, A // ta),
            in_specs=[pl.BlockSpec((None, ta, B), lambda k, i, c_ref: (2 * k + c_ref[0], i, 0)), blk],
            out_specs=blk),
        out_shape=jax.ShapeDtypeStruct((4, A, B), BF16),
        compiler_params=_params("parallel", "parallel"),
    )(c_idx, g, buf)


def _adamw_layer(w, m, v, own, parts, chip_idx, outs, layer, *, name):
    _, A, B = w.shape
    ta = _tile(A, 256, 16)

    def body(chip_ref, w_ref, m_ref, v_ref, own_ref, p_ref, *rest):
        g_ref, d_ref, nm_ref, nv_ref = rest[4:]
        gv = None
        for j in range(4):
            part = jnp.where(chip_ref[0] == j, own_ref[...], p_ref[j]).astype(F32)
            gv = part if gv is None else gv + part
        nm = ADAM_B1 * m_ref[...] + (1.0 - ADAM_B1) * gv
        nv = ADAM_B2 * v_ref[...] + (1.0 - ADAM_B2) * (gv * gv)
        m_hat = nm / (1.0 - ADAM_B1 ** ADAM_STEP)
        v_hat = nv / (1.0 - ADAM_B2 ** ADAM_STEP)
        g_ref[...] = gv
        d_ref[...] = -ADAM_LR * (m_hat / (jnp.sqrt(v_hat) + ADAM_EPS) + ADAM_WD * w_ref[...])
        nm_ref[...] = nm
        nv_ref[...] = nv

    blk = pl.BlockSpec((None, ta, B), lambda i, ch: (layer, i, 0))
    return pl.pallas_call(
        body, name=name,
        grid_spec=pltpu.PrefetchScalarGridSpec(
            num_scalar_prefetch=1, grid=(A // ta,),
            in_specs=[blk, blk, blk, pl.BlockSpec((None, ta, B), lambda i, ch: (ch[0], i, 0)),
                      pl.BlockSpec((4, ta, B), lambda i, ch: (0, i, 0))] + [ANY] * 4,
            out_specs=[blk] * 4),
        out_shape=[jax.ShapeDtypeStruct(o.shape, o.dtype) for o in outs],
        input_output_aliases={6 + q: q for q in range(4)},
        compiler_params=_params("parallel"),
    )(chip_idx, w, m, v, own, parts, *outs)


def _sum_slots(buf, *, name):
    n, R, L = buf.shape
    tr = _tile(R, 512, SUBLANES)

    def body(b_ref, o_ref):
        acc = b_ref[0]
        for j in range(1, n):
            acc = acc + b_ref[j]
        o_ref[...] = acc

    return pl.pallas_call(
        body, name=name, grid=(R // tr,),
        in_specs=[pl.BlockSpec((n, tr, L), lambda i: (0, i, 0))],
        out_specs=pl.BlockSpec((tr, L), lambda i: (i, 0)),
        out_shape=jax.ShapeDtypeStruct((R, L), buf.dtype),
        compiler_params=_params("parallel"),
    )(buf)


BIG = ("gla_w_in", "gla_w_out", "attn_w_qkv", "attn_w_out", "ffn_w_up", "ffn_w_down")
SMALL_SHARDED = ("gla_w_gate_up_f", "gla_w_gate_up_b", "ffn_w_conv")
REPLICATED = ("norm_mix", "norm_ffn", "gla_b_gate_f", "gla_b_gate_b", "gla_norm", "attn_q_norm", "attn_k_norm",
              "ffn_b_conv")
WEIGHTS = ("norm_mix", "norm_ffn", "gla_w_in", "gla_w_gate_up_f", "gla_b_gate_f", "gla_w_gate_up_b", "gla_b_gate_b",
           "gla_norm", "gla_w_out", "attn_w_qkv", "attn_q_norm", "attn_k_norm", "attn_w_out", "ffn_w_up", "ffn_w_conv",
           "ffn_b_conv", "ffn_w_down")


def _rows(flat, row_align):
    n = flat.shape[0]
    per = row_align * LANES
    padded = -(-n // per) * per
    return jnp.pad(flat, (0, padded - n)).reshape(padded // LANES, LANES)


def _side_by_side(gathered, own, dev):
    n, a, b = gathered.shape
    whole = lax.dynamic_update_index_in_dim(gathered, own, dev, 0)
    return jnp.transpose(whole, (1, 0, 2)).reshape(a, n * b)


def _layer_shards(w, i, mixer, ffn):
    j = i // 2
    parts = []
    if mixer and i % 2 == 0:
        parts += [("mix_in", w["gla_w_in"][j].astype(BF16)), ("mix_out", w["gla_w_out"][j].astype(BF16)),
                  ("gate_f", w["gla_w_gate_up_f"][j].astype(BF16)), ("gate_b", w["gla_w_gate_up_b"][j].astype(BF16))]
    elif mixer:
        parts += [("mix_in", w["attn_w_qkv"][j].astype(BF16)), ("mix_out", w["attn_w_out"][j].astype(BF16))]
    if ffn:
        parts += [("up", w["ffn_w_up"][i].astype(BF16)), ("down", w["ffn_w_down"][i].astype(BF16)),
                  ("conv", w["ffn_w_conv"][i])]
    return [n for n, _ in parts], [a for _, a in parts]


def _layer_weights(names, own, gathered, dev, dev_idx, i):
    own, got = dict(zip(names, own)), dict(zip(names, gathered))
    every = tuple(range(N_DEV))
    half = N_DEV // 2
    tag = "_l%d" % i
    out = {}
    if "mix_in" in got:
        width = GLA_PROJ if i % 2 == 0 else QKV_DIM
        (out["mix_in"],) = _unshard_cols(got["mix_in"], own["mix_in"], dev_idx, [every], width,
                                         name="unshard_mix_in" + tag)
        out["mix_out"] = _place_own(got["mix_out"], own["mix_out"], dev_idx, name="place_mix_out" + tag)
    if "gate_f" in got:
        out["gate"] = _gate_matrix(_side_by_side(got["gate_f"], own["gate_f"], dev),
                                   _side_by_side(got["gate_b"], own["gate_b"], dev))
    if "up" in got:
        f = got["up"].shape[-1] * half
        out["up_val"], out["up_gate"] = _unshard_cols(got["up"], own["up"], dev_idx, [every[:half], every[half:]], f,
                                                      name="unshard_ffn_up" + tag)
        out["down"] = _place_own(got["down"], own["down"], dev_idx, name="place_ffn_down" + tag)
        out["conv"] = _side_by_side(got["conv"], own["conv"], dev)
    return out


def _rope_tables(S):
    rows = S // GRID_W
    pairs = ATTN_HD // 4
    row_idx = jnp.repeat(jnp.arange(rows, dtype=F32), GRID_W)
    col_idx = jnp.tile(jnp.arange(GRID_W, dtype=F32), rows)
    inv_freq = ROPE_THETA ** (-jnp.arange(pairs, dtype=F32) / pairs)
    ang = jnp.concatenate([row_idx[:, None] * inv_freq, col_idx[:, None] * inv_freq], axis=-1)
    cos, sin = jnp.cos(ang), jnp.sin(ang)
    return jnp.concatenate([cos, cos], axis=-1), jnp.concatenate([-sin, sin], axis=-1)


def _gate_matrix(w_f, w_b):
    rk = w_f.shape[0]
    top = jnp.concatenate([w_f, jnp.zeros_like(w_f)], axis=1)
    mid = jnp.concatenate([jnp.zeros_like(w_b), w_b], axis=1)
    pad = jnp.zeros((LANES - 2 * rk, 2 * GLA_KD), w_f.dtype)
    return jnp.concatenate([top, mid, pad], axis=0)


def _local_step(x, target, rep, w, dev, idx):
    S, D = x.shape
    depth = rep["norm_mix"].shape[0]
    cs, sn = _rope_tables(S)
    row = lambda a: a.reshape(1, -1)
    ranks_cols = (GLA_PROJ - LANES, LANES)

    groups = [(0, True, False), (0, False, True)] + [(i, True, True) for i in range(1, depth)]
    sent = []
    for g, (i, mixer, ffn) in enumerate(groups):
        names, srcs = _layer_shards(w, i, mixer, ffn)
        lands = [lax.empty((N_DEV,) + a.shape, a.dtype) for a in srcs]
        dep = (sent[-1][1]["token"],) if sent else ()
        sent.append((names, _split_start(_gather_send_plan(len(srcs)), srcs, lands, dep=dep,
                                         name="weights_send_g%d" % g)))

    def arrive(g, after):
        names, started = sent[g]
        own, lands = _split_wait(started, after, name="weights_arrive_g%d" % g)
        return names, own, _split_start(_gather_pass_plan(len(lands)), [], lands, name="weights_pass_g%d" % g)

    def ready(g, passing, after):
        names, own, started = passing
        _, lands = _split_wait(started, after, name="weights_passed_g%d" % g)
        return _layer_weights(names, own, lands, dev, idx["dev"], groups[g][0])

    passing = arrive(0, sent[-1][1]["token"])
    wl = ready(0, passing, passing[2]["token"])
    passing = arrive(1, wl["mix_in"])

    saved = []
    for i in range(depth):
        j = i // 2
        sv = {"x0": x, "w": wl}
        h1 = _rmsnorm_fwd(x, row(rep["norm_mix"][i]), dep=(passing[2]["token"],) if i == 0 else (), name="norm_mix_fwd")
        sv["h1"] = h1
        if i % 2 == 0:
            bias = jnp.concatenate([rep["gla_b_gate_f"][j], rep["gla_b_gate_b"][j]]).reshape(1, -1)
            proj = _mm(h1, wl["mix_in"], name="gla_in_proj")
            logits = _mm(proj, wl["gate"], a_cols=ranks_cols, name="gla_gate_logits")
            of, ob, sf, sb = _gla_fwd(proj, logits, bias, name="gla_fwd")
            y = _gla_gate_fwd(of, ob, proj, row(rep["gla_norm"][j]), name="gla_gate_fwd")
            x = _mm(y, wl["mix_out"], res=x, name="gla_out_proj")
            sv.update(bias=bias, proj=proj, logits=logits, of=of, ob=ob, sf=sf, sb=sb, y=y)
        else:
            proj = _mm(h1, wl["mix_in"], name="attn_qkv_proj")
            qkv = _attn_prep(proj, cs, sn, row(rep["attn_q_norm"][j]), row(rep["attn_k_norm"][j]), name="attn_prep")
            o = _attn_fwd(qkv, name="attn_fwd")
            x = _mm(o, wl["mix_out"], res=x, name="attn_out_proj")
            sv.update(proj=proj, qkv=qkv, o=o)
        sv["x1"] = x
        if i == 0:
            wl.update(ready(1, passing, x))
        dep = ()
        if i + 1 < depth:
            passing = arrive(i + 2, x)
            dep = (passing[2]["token"],)
        h2 = _rmsnorm_fwd(x, row(rep["norm_ffn"][i]), dep=dep, name="norm_ffn_fwd")
        F = wl["down"].shape[0]
        wc, bc = wl["conv"], rep["ffn_b_conv"][i]
        wcv, wcg, bcv, bcg = wc[:, :F], wc[:, F:], row(bc[:F]), row(bc[F:])
        uv = _mm(h2, wl["up_val"], name="ffn_up_val")
        ug = _mm(h2, wl["up_gate"], name="ffn_up_gate")
        act = _conv_act_fwd(uv, ug, wcv, wcg, bcv, bcg, name="ffn_conv_act")
        x = _mm(act, wl["down"], res=x, name="ffn_down")
        sv.update(h2=h2, uv=uv, ug=ug, act=act, wcv=wcv, wcg=wcg, bcv=bcv, bcg=bcg)
        saved.append(sv)
        if i + 1 < depth:
            wl = ready(i + 2, passing, x)

    dx, loss_tile = _loss_head(x, target, name="loss_head")
    loss = loss_tile[0, 0]

    in_sibling_stage, in_chip_stage, reduced = [], [], []

    def advance(group, after):
        tokens = []
        for tag, keys, started in in_chip_stage:
            partial, lands = _split_wait(started, after, name="grads_chips_arrive_" + tag)
            reduced.append((keys, partial, lands))
        in_chip_stage.clear()
        for tag, keys, started in in_sibling_stage:
            stacks, lands = _split_wait(started, after, name="grads_sibling_arrive_" + tag)
            partial = [_add_sibling(s, b, idx["core"], name="grads_add_sibling_%s_%d" % (tag, q))
                       for q, (s, b) in enumerate(zip(stacks, lands))]
            bufs = [lax.empty(p.shape, p.dtype) for p in partial]
            started = _split_start(_reduce_chip_plan(len(partial)), partial, bufs, name="grads_chips_send_" + tag)
            in_chip_stage.append((tag, keys, started))
            tokens.append(started["token"])
        in_sibling_stage.clear()
        if group is not None:
            tag, keys, stacks = group
            bufs = [lax.empty((4,) + s.shape[1:], s.dtype) for s in stacks]
            started = _split_start(_reduce_sibling_plan(len(stacks)), stacks, bufs, name="grads_sibling_send_" + tag)
            in_sibling_stage.append((tag, keys, started))
            tokens.append(started["token"])
        return tuple(tokens)

    def stack_for(name):
        return lax.empty((N_DEV,) + tuple(w[name].shape[1:]), BF16)

    gl = {k: [None] * depth for k in ("norm_mix", "norm_ffn", "ffn_w_conv", "ffn_b_conv")}
    gm = {k: [None] * (depth // 2) for k in ("gla_w_gate_up_f", "gla_b_gate_f", "gla_w_gate_up_b", "gla_b_gate_b",
                                             "gla_norm", "attn_q_norm", "attn_k_norm")}
    rk = GLA_GATE_RANK
    dep = ()
    for i in reversed(range(depth)):
        j = i // 2
        sv = saved[i]
        wl = sv["w"]
        dact = _mm(dx, wl["down"], tb=True, dep=dep, name="ffn_down_dgrad")
        g_down = _wgrad(sv["act"], dx, stack_for("ffn_w_down"), shard="rows", name="ffn_down_wgrad")
        dcv, dcg, dwv, dwg, dbv, dbg = _conv_act_bwd(sv["uv"], sv["ug"], sv["wcv"], sv["wcg"], sv["bcv"], sv["bcg"],
                                                     dact, name="ffn_conv_act_bwd")
        gl["ffn_w_conv"][i] = jnp.concatenate([dwv, dwg], axis=1)
        gl["ffn_b_conv"][i] = jnp.concatenate([dbv, dbg], axis=1)[0]
        mid = advance(None, dcv)
        duv = _conv_t(dcv, sv["wcv"], name="ffn_conv_t")
        dug = _conv_t(dcg, sv["wcg"], name="ffn_conv_t")
        dh2 = _mm(duv, wl["up_val"], tb=True, dep=mid, name="ffn_up_dgrad_val")
        dh2 = _mm(dug, wl["up_gate"], tb=True, res=dh2, name="ffn_up_dgrad_gate")
        g_up = _wgrad(sv["h2"], duv, stack_for("ffn_w_up"), shard="cols", group=0, name="ffn_up_wgrad_val")
        g_up = _wgrad(sv["h2"], dug, g_up, shard="cols", group=1, name="ffn_up_wgrad_gate")
        dx, dn = _rmsnorm_bwd(sv["x1"], row(rep["norm_ffn"][i]), dh2, dx, name="norm_ffn_bwd")
        gl["norm_ffn"][i] = dn[0]
        dep = advance(("ffn_l%d" % i, [("ffn_w_up", i), ("ffn_w_down", i)], [g_up, g_down]), dx)
        if i % 2 == 0:
            dy = _mm(dx, wl["mix_out"], tb=True, dep=dep, name="gla_out_dgrad")
            g_out = _wgrad(sv["y"], dx, stack_for("gla_w_out"), shard="rows", name="gla_out_wgrad")
            do, dg, dgn = _gla_gate_bwd(sv["of"], sv["ob"], sv["proj"], row(rep["gla_norm"][j]), dy, name="gla_gate_bwd")
            gm["gla_norm"][j] = dgn[0]
            dqkv_f, dlg_f, dqkv_b, dlg_b, dbias = _gla_bwd(sv["proj"], sv["logits"], sv["bias"], sv["sf"], sv["sb"], do,
                                                           name="gla_bwd")
            gm["gla_b_gate_f"][j] = dbias[0, :GLA_KD]
            gm["gla_b_gate_b"][j] = dbias[0, GLA_KD:]
            mid = advance(None, dqkv_f)
            dlogits = jnp.concatenate([dlg_f, dlg_b], axis=1)
            dr = _mm(dlogits, wl["gate"], tb=True, dep=mid, name="gla_gate_dgrad")
            dwg_full = _mm(sv["proj"], dlogits, ta=True, a_cols=ranks_cols, name="gla_gate_wgrad")
            gm["gla_w_gate_up_f"][j] = dwg_full[:rk, :GLA_KD]
            gm["gla_w_gate_up_b"][j] = dwg_full[rk:2 * rk, GLA_KD:]
            dproj = _gla_combine(dqkv_f, dqkv_b, dg, dr, name="gla_combine")
            dh1 = _mm(dproj, wl["mix_in"], tb=True, name="gla_in_dgrad")
            g_in = _wgrad(sv["h1"], dproj, stack_for("gla_w_in"), shard="cols", name="gla_in_wgrad")
            keys = [("gla_w_in", j), ("gla_w_out", j)]
        else:
            do = _mm(dx, wl["mix_out"], tb=True, out_dtype=BF16, dep=dep, name="attn_out_dgrad")
            g_out = _wgrad(sv["o"], dx, stack_for("attn_w_out"), shard="rows", name="attn_out_wgrad")
            dq, dk, dv = _attn_bwd(sv["qkv"], do, name="attn_bwd")
            mid = advance(None, dq)
            dproj, dqn, dkn = _attn_prep_bwd(sv["proj"], dq, dk, dv, cs, sn, row(rep["attn_q_norm"][j]),
                                             row(rep["attn_k_norm"][j]), name="attn_prep_bwd")
            gm["attn_q_norm"][j] = dqn[0]
            gm["attn_k_norm"][j] = dkn[0]
            dh1 = _mm(dproj, wl["mix_in"], tb=True, dep=mid, name="attn_qkv_dgrad")
            g_in = _wgrad(sv["h1"], dproj, stack_for("attn_w_qkv"), shard="cols", name="attn_qkv_wgrad")
            keys = [("attn_w_qkv", j), ("attn_w_out", j)]
        dx, dn = _rmsnorm_bwd(sv["x0"], row(rep["norm_mix"][i]), dh1, dx, name="norm_mix_bwd")
        gl["norm_mix"][i] = dn[0]
        dep = advance(("mix_l%d" % i, keys, [g_in, g_out]), dx)

    small = {k: jnp.stack(v) for k, v in {**gl, **gm}.items()}
    return loss, dx, reduced, small, advance


def kernel(x, norm_mix, norm_ffn, gla_w_in, gla_w_gate_up_f, gla_b_gate_f, gla_w_gate_up_b, gla_b_gate_b, gla_norm, gla_w_out, attn_w_qkv, attn_q_norm, attn_k_norm, attn_w_out, ffn_w_up, ffn_w_conv, ffn_b_conv, ffn_w_down, loss_target, m_norm_mix, m_norm_ffn, m_gla_w_in, m_gla_w_gate_up_f, m_gla_b_gate_f, m_gla_w_gate_up_b, m_gla_b_gate_b, m_gla_norm, m_gla_w_out, m_attn_w_qkv, m_attn_q_norm, m_attn_k_norm, m_attn_w_out, m_ffn_w_up, m_ffn_w_conv, m_ffn_b_conv, m_ffn_w_down, v_norm_mix, v_norm_ffn, v_gla_w_in, v_gla_w_gate_up_f, v_gla_b_gate_f, v_gla_w_gate_up_b, v_gla_b_gate_b, v_gla_norm, v_gla_w_out, v_attn_w_qkv, v_attn_q_norm, v_attn_k_norm, v_attn_w_out, v_ffn_w_up, v_ffn_w_conv, v_ffn_b_conv, v_ffn_w_down):
    given = dict(locals())
    w = {n: given[n] for n in WEIGHTS}
    m = {n: given["m_" + n] for n in WEIGHTS}
    v = {n: given["v_" + n] for n in WEIGHTS}
    shards = {n: w[n] for n in BIG + SMALL_SHARDED}
    rep = {n: w[n] for n in REPLICATED}

    x_pos, y_pos, c_pos = _place()
    dev = 4 * x_pos + 2 * y_pos + c_pos
    as_operand = lambda s: jnp.asarray(s, jnp.int32).reshape(1)
    idx = dict(dev=as_operand(dev), chip=as_operand(2 * x_pos + y_pos), core=as_operand(c_pos))

    loss_local, grad_x, reduced, small, advance = _local_step(x[0], loss_target[0], rep, shards, dev, idx)
    loss = lax.psum(loss_local, ("x", "y", "c"))

    big = {n: [lax.empty(w[n].shape, F32) for _ in range(4)] for n in BIG}

    def update_reduced():
        for keys, own, parts in reduced:
            for (n, layer), p_own, p_others in zip(keys, own, parts):
                big[n] = _adamw_layer(w[n], m[n], v[n], p_own, p_others, idx["chip"], big[n], layer,
                                      name="adamw_%s_l%d" % (n, layer))
        reduced.clear()

    advance(None, grad_x)
    update_reduced()

    rest = REPLICATED + SMALL_SHARDED
    flat = _rows(jnp.concatenate([small[n].reshape(-1) for n in rest]), SUBLANES)
    total = _sum_slots(_all_gather(flat, name="small_grads_all_gather"), name="small_grads_sum").reshape(-1)
    advance(None, [total, *big["ffn_w_up"]])
    update_reduced()
    g, off = {}, 0
    for n in rest:
        whole = total[off:off + small[n].size].reshape(small[n].shape)
        off += small[n].size
        width = w[n].shape[-1]
        g[n] = whole if n in REPLICATED else lax.dynamic_slice_in_dim(whole, dev * width, width, axis=whole.ndim - 1)

    delta, new_m, new_v = {}, {}, {}
    for n in WEIGHTS:
        if n in BIG:
            g[n], delta[n], new_m[n], new_v[n] = big[n]
        else:
            shape = w[n].shape
            two_d = (-1, shape[-1])
            d2, m2, v2 = _adamw(w[n].reshape(two_d), g[n].reshape(two_d), m[n].reshape(two_d), v[n].reshape(two_d),
                                name="adamw_" + n)
            delta[n], new_m[n], new_v[n] = d2.reshape(shape), m2.reshape(shape), v2.reshape(shape)

    return (loss, grad_x[None], *[g[n] for n in WEIGHTS], *[delta[n] for n in WEIGHTS],
            *[new_m[n] for n in WEIGHTS], *[new_v[n] for n in WEIGHTS])
```

```python
import math

import jax
import jax.numpy as jnp
from jax import lax
from jax.experimental import pallas as pl
from jax.experimental.pallas import tpu as pltpu

F32 = jnp.float32
BF16 = jnp.bfloat16
MESH = pl.DeviceIdType.MESH

N_DEV = 8
LANES = 128
SUBLANES = 8
VMEM_LIMIT = 56 * 1024 * 1024

NORM_EPS = 1e-6
GRID_W = 64
ROPE_THETA = 10000.0
GLA_HEADS = 4
GLA_DK = 128
GLA_DV = 256
GLA_CHUNK = 64
GLA_GATE_RANK = 16
GLA_GATE_NORMALIZER = 16.0
ATTN_HD = 128
ATTN_Q_HEADS = 8
ATTN_KV_HEADS = 2
ATTN_GROUP = ATTN_Q_HEADS // ATTN_KV_HEADS

ADAM_LR = 0.001
ADAM_B1 = 0.9
ADAM_B2 = 0.999
ADAM_EPS = 1e-08
ADAM_WD = 0.01
ADAM_STEP = 10


def _tile(n, target, align=LANES):
    if n <= target:
        return n
    t = (target // align) * align
    while t >= align:
        if n % t == 0:
            return t
        t -= align
    return n


def _params(*sem):
    return pltpu.CompilerParams(dimension_semantics=sem, vmem_limit_bytes=VMEM_LIMIT)


def _dot(a, b):
    return lax.dot_general(a, b, (((1,), (0,)), ((), ())), preferred_element_type=F32)


def _dot_nt(a, b):
    return lax.dot_general(a, b, (((1,), (1,)), ((), ())), preferred_element_type=F32)


def _dot_tn(a, b):
    return lax.dot_general(a, b, (((0,), (0,)), ((), ())), preferred_element_type=F32)


def _sigmoid(x):
    return 0.5 * jnp.tanh(0.5 * x) + 0.5


def _log_sigmoid(x):
    return jnp.minimum(x, 0.0) - jnp.log(1.0 + jnp.exp(-jnp.abs(x)))


def _colsum(x):
    return jnp.sum(x, axis=0, keepdims=True)


ANY = pl.BlockSpec(memory_space=pl.ANY)


def _mm(a, b, *, ta=False, tb=False, res=None, out_dtype=F32, layer=None, a_cols=None, dep=(), name):
    if tb:
        N, K = b.shape[-2:]
    else:
        K, N = b.shape[-2:]
    a_rows, a_width = a.shape
    a_off = 0
    if a_cols is not None:
        a_off, a_width = a_cols
    if ta:
        M = a_width
        assert a_rows == K, (a.shape, b.shape, ta, tb)
    else:
        M = a_rows
        assert a_width == K, (a.shape, b.shape, ta, tb)
    tm = _tile(M, 1408) if ta else _tile(M, 1024, 16)
    tn = _tile(N, 1408)
    tk = _tile(K, 512, 16) if ta else _tile(K, 1408)
    nk = K // tk
    dims = (((0 if ta else 1,), (1 if tb else 0,)), ((), ()))

    n_in = 2 + (res is not None) + len(dep)

    def body(*refs):
        a_ref, b_ref = refs[:2]
        r_ref = refs[2] if res is not None else None
        o_ref = refs[n_in]
        scr = refs[n_in + 1:]
        part = lax.dot_general(a_ref[...].astype(BF16), b_ref[...].astype(BF16), dims, preferred_element_type=F32)

        def finish(acc):
            if r_ref is not None:
                acc = acc + r_ref[...]
            o_ref[...] = acc.astype(out_dtype)

        if nk == 1:
            finish(part)
        else:
            acc_ref = scr[0]
            k = pl.program_id(2)

            @pl.when(k == 0)
            def _():
                acc_ref[...] = part

            @pl.when(k > 0)
            def _():
                acc_ref[...] += part

            @pl.when(k == nk - 1)
            def _():
                finish(acc_ref[...])

    a_blk = a_off // (tm if ta else tk)
    assert a_off % (tm if ta else tk) == 0
    a_spec = (pl.BlockSpec((tk, tm), lambda i, j, k: (k, a_blk + i)) if ta
              else pl.BlockSpec((tm, tk), lambda i, j, k: (i, a_blk + k)))
    if layer is None:
        b_spec = pl.BlockSpec((tn, tk), lambda i, j, k: (j, k)) if tb else pl.BlockSpec((tk, tn), lambda i, j, k: (k, j))
    else:
        b_spec = (pl.BlockSpec((None, tn, tk), lambda i, j, k: (layer, j, k)) if tb
                  else pl.BlockSpec((None, tk, tn), lambda i, j, k: (layer, k, j)))
    o_spec = pl.BlockSpec((tm, tn), lambda i, j, k: (i, j))
    in_specs = [a_spec, b_spec] + ([o_spec] if res is not None else []) + [ANY] * len(dep)
    args = (a, b) + ((res,) if res is not None else ()) + tuple(dep)
    return pl.pallas_call(
        body, name=name, grid=(M // tm, N // tn, nk),
        in_specs=in_specs, out_specs=o_spec,
        out_shape=jax.ShapeDtypeStruct((M, N), out_dtype),
        scratch_shapes=[pltpu.VMEM((tm, tn), F32)] if nk > 1 else [],
        compiler_params=_params("parallel", "parallel", "arbitrary"),
    )(*args)


def _wgrad(a, b, stack, *, shard, group=0, name):
    S, N = b.shape
    M = a.shape[1]
    As, Bs = stack.shape[-2:]
    tk = _tile(S, 1024, 16)
    nk = S // tk
    if shard == "cols":
        n = N // Bs
        tm = _tile(M, 512)
        tn = N
        grid = (M // tm, 1, nk)
        o_spec = pl.BlockSpec((n, tm, Bs), lambda i, j, k: (group, i, 0))
    else:
        per = As * LANES // math.gcd(As, LANES)
        tm = M if M <= 1408 else _tile(M, 1408, per)
        n = tm // As
        tn = _tile(N, 1024)
        grid = (M // tm, N // tn, nk)
        o_spec = pl.BlockSpec((n, As, tn), lambda i, j, k: (i, 0, j))

    def body(a_ref, b_ref, _, o_ref, acc_ref):
        k = pl.program_id(2)
        part = _dot_tn(a_ref[...].astype(BF16), b_ref[...].astype(BF16))

        @pl.when(k == 0)
        def _():
            acc_ref[...] = part

        @pl.when(k > 0)
        def _():
            acc_ref[...] += part

        @pl.when(k == nk - 1)
        def _():
            for q in range(n):
                if shard == "cols":
                    o_ref[q] = acc_ref[:, q * Bs:(q + 1) * Bs].astype(stack.dtype)
                else:
                    o_ref[q] = acc_ref[q * As:(q + 1) * As, :].astype(stack.dtype)

    return pl.pallas_call(
        body, name=name, grid=grid,
        in_specs=[pl.BlockSpec((tk, tm), lambda i, j, k: (k, i)), pl.BlockSpec((tk, tn), lambda i, j, k: (k, j)),
                  pl.BlockSpec(memory_space=pl.ANY)],
        out_specs=o_spec,
        out_shape=jax.ShapeDtypeStruct(stack.shape, stack.dtype),
        input_output_aliases={2: 0},
        scratch_shapes=[pltpu.VMEM((tm, tn), F32)],
        compiler_params=_params("parallel", "parallel", "arbitrary"),
    )(a, b, stack)


def _rmsnorm_fwd(x, w, *, dep=(), name):
    S, D = x.shape
    ts = _tile(S, 512, 16)

    def body(x_ref, w_ref, *rest):
        o_ref = rest[-1]
        xv = x_ref[...]
        r = lax.rsqrt(jnp.mean(xv * xv, axis=-1, keepdims=True) + NORM_EPS)
        o_ref[...] = (xv * r * w_ref[...]).astype(BF16)

    return pl.pallas_call(
        body, name=name, grid=(S // ts,),
        in_specs=[pl.BlockSpec((ts, D), lambda i: (i, 0)), pl.BlockSpec((1, D), lambda i: (0, 0))] + [ANY] * len(dep),
        out_specs=pl.BlockSpec((ts, D), lambda i: (i, 0)),
        out_shape=jax.ShapeDtypeStruct((S, D), BF16),
        compiler_params=_params("parallel"),
    )(x, w, *dep)


def _rmsnorm_bwd(x, w, dh, dres, *, name):
    S, D = x.shape
    ts = _tile(S, 512, 16)
    n = S // ts

    def body(x_ref, w_ref, dh_ref, dr_ref, dx_ref, dw_ref):
        i = pl.program_id(0)
        xv = x_ref[...]
        r = lax.rsqrt(jnp.mean(xv * xv, axis=-1, keepdims=True) + NORM_EPS)
        xh = xv * r
        d = dh_ref[...]
        g = d * w_ref[...]
        dx_ref[...] = dr_ref[...] + r * (g - xh * jnp.mean(g * xh, axis=-1, keepdims=True))
        part = _colsum(d * xh)

        @pl.when(i == 0)
        def _():
            dw_ref[...] = part

        @pl.when(i > 0)
        def _():
            dw_ref[...] += part

    row = pl.BlockSpec((ts, D), lambda i: (i, 0))
    vec = pl.BlockSpec((1, D), lambda i: (0, 0))
    return pl.pallas_call(
        body, name=name, grid=(n,),
        in_specs=[row, vec, row, row], out_specs=[row, vec],
        out_shape=[jax.ShapeDtypeStruct((S, D), F32), jax.ShapeDtypeStruct((1, D), F32)],
        compiler_params=_params("arbitrary"),
    )(x, w, dh, dres)


def _halo_specs(S, ts, tf, row_axis, rows=SUBLANES):
    g = ts // rows
    last = S // rows - 1
    col_axis = 1 - row_axis
    main = pl.BlockSpec((ts, tf), lambda *ij: (ij[row_axis], ij[col_axis]))
    prev = pl.BlockSpec((rows, tf), lambda *ij: (jnp.maximum(ij[row_axis] * g - 1, 0), ij[col_axis]))
    nxt = pl.BlockSpec((rows, tf), lambda *ij: (jnp.minimum((ij[row_axis] + 1) * g, last), ij[col_axis]))
    return [main, prev, nxt]


def _shifted(u, prev_ref, next_ref, i, n):
    ts = u.shape[0]
    rows = prev_ref.shape[0]
    rid = lax.broadcasted_iota(jnp.int32, u.shape, 0)
    before = jnp.where(i > 0, prev_ref[rows - 1:rows, :].astype(F32), 0.0)
    after = jnp.where(i < n - 1, next_ref[0:1, :].astype(F32), 0.0)
    um1 = jnp.where(rid == 0, before, pltpu.roll(u, 1, 0))
    up1 = jnp.where(rid == ts - 1, after, pltpu.roll(u, ts - 1, 0))
    return um1, up1


def _conv3(u, prev_ref, next_ref, w_ref, i, n):
    um1, up1 = _shifted(u, prev_ref, next_ref, i, n)
    return w_ref[0:1, :] * um1 + w_ref[1:2, :] * u + w_ref[2:3, :] * up1


def _conv_act_fwd(uv, ug, wv, wg, bv, bg, *, name):
    S, F = uv.shape
    ts = _tile(S, 512, 16)
    tf = _tile(F, 1408)
    n = S // ts

    def body(v_ref, vp_ref, vn_ref, g_ref, gp_ref, gn_ref, wv_ref, wg_ref, bv_ref, bg_ref, o_ref):
        i = pl.program_id(0)
        val = _conv3(v_ref[...], vp_ref, vn_ref, wv_ref, i, n) + bv_ref[...]
        gate = _conv3(g_ref[...], gp_ref, gn_ref, wg_ref, i, n) + bg_ref[...]
        o_ref[...] = (gate * _sigmoid(gate) * val).astype(BF16)

    halo = _halo_specs(S, ts, tf, 0)
    w3 = pl.BlockSpec((3, tf), lambda i, j: (0, j))
    b1 = pl.BlockSpec((1, tf), lambda i, j: (0, j))
    return pl.pallas_call(
        body, name=name, grid=(n, F // tf),
        in_specs=halo + halo + [w3, w3, b1, b1],
        out_specs=pl.BlockSpec((ts, tf), lambda i, j: (i, j)),
        out_shape=jax.ShapeDtypeStruct((S, F), BF16),
        compiler_params=_params("parallel", "parallel"),
    )(uv, uv, uv, ug, ug, ug, wv, wg, bv, bg)


def _conv_act_bwd(uv, ug, wv, wg, bv, bg, dact, *, name):
    S, F = uv.shape
    ts = _tile(S, 512, 16)
    tf = _tile(F, 1408)
    n = S // ts

    def body(v_ref, vp_ref, vn_ref, g_ref, gp_ref, gn_ref, wv_ref, wg_ref, bv_ref, bg_ref, da_ref,
             dv_ref, dg_ref, dwv_ref, dwg_ref, dbv_ref, dbg_ref):
        i = pl.program_id(1)
        uvv, ugv = v_ref[...], g_ref[...]
        vm1, vp1 = _shifted(uvv, vp_ref, vn_ref, i, n)
        gm1, gp1 = _shifted(ugv, gp_ref, gn_ref, i, n)
        val = wv_ref[0:1, :] * vm1 + wv_ref[1:2, :] * uvv + wv_ref[2:3, :] * vp1 + bv_ref[...]
        gate = wg_ref[0:1, :] * gm1 + wg_ref[1:2, :] * ugv + wg_ref[2:3, :] * gp1 + bg_ref[...]
        sg = _sigmoid(gate)
        da = da_ref[...]
        dval = da * (gate * sg)
        dgate = da * val * (sg * (1.0 + gate * (1.0 - sg)))
        dv_ref[...] = dval.astype(BF16)
        dg_ref[...] = dgate.astype(BF16)
        sums = [(dwv_ref, 0, vm1 * dval), (dwv_ref, 1, uvv * dval), (dwv_ref, 2, vp1 * dval),
                (dwg_ref, 0, gm1 * dgate), (dwg_ref, 1, ugv * dgate), (dwg_ref, 2, gp1 * dgate),
                (dbv_ref, 0, dval), (dbg_ref, 0, dgate)]
        parts = [(ref, r, _colsum(t)) for ref, r, t in sums]

        @pl.when(i == 0)
        def _():
            for ref, r, part in parts:
                ref[r:r + 1, :] = part

        @pl.when(i > 0)
        def _():
            for ref, r, part in parts:
                ref[r:r + 1, :] += part

    halo = _halo_specs(S, ts, tf, 1)
    w3 = pl.BlockSpec((3, tf), lambda j, i: (0, j))
    b1 = pl.BlockSpec((1, tf), lambda j, i: (0, j))
    blk = pl.BlockSpec((ts, tf), lambda j, i: (i, j))
    return pl.pallas_call(
        body, name=name, grid=(F // tf, n),
        in_specs=halo + halo + [w3, w3, b1, b1, blk],
        out_specs=[blk, blk, w3, w3, b1, b1],
        out_shape=[jax.ShapeDtypeStruct((S, F), BF16), jax.ShapeDtypeStruct((S, F), BF16),
                   jax.ShapeDtypeStruct((3, F), F32), jax.ShapeDtypeStruct((3, F), F32),
                   jax.ShapeDtypeStruct((1, F), F32), jax.ShapeDtypeStruct((1, F), F32)],
        compiler_params=_params("parallel", "arbitrary"),
    )(uv, uv, uv, ug, ug, ug, wv, wg, bv, bg, dact)


def _conv_t(duc, w, *, name):
    S, F = duc.shape
    ts = _tile(S, 512, 16)
    tf = _tile(F, 1408)
    n = S // ts

    def body(d_ref, dp_ref, dn_ref, w_ref, o_ref):
        i = pl.program_id(0)
        d = d_ref[...].astype(F32)
        dm1, dp1 = _shifted(d, dp_ref, dn_ref, i, n)
        o_ref[...] = (w_ref[0:1, :] * dp1 + w_ref[1:2, :] * d + w_ref[2:3, :] * dm1).astype(BF16)

    return pl.pallas_call(
        body, name=name, grid=(n, F // tf),
        in_specs=_halo_specs(S, ts, tf, 0, rows=16) + [pl.BlockSpec((3, tf), lambda i, j: (0, j))],
        out_specs=pl.BlockSpec((ts, tf), lambda i, j: (i, j)),
        out_shape=jax.ShapeDtypeStruct((S, F), BF16),
        compiler_params=_params("parallel", "parallel"),
    )(duc, duc, duc, w)


N_QK = ATTN_Q_HEADS + ATTN_KV_HEADS
QKV_DIM = (ATTN_Q_HEADS + 2 * ATTN_KV_HEADS) * ATTN_HD


def _head(hd):
    return slice(hd * ATTN_HD, (hd + 1) * ATTN_HD)


def _attn_prep(proj, cs, sn, qn, kn, *, name):
    S = proj.shape[0]
    ts = _tile(S, 256, 16)

    def body(p_ref, c_ref, s_ref, qn_ref, kn_ref, o_ref):
        c, s = c_ref[...], s_ref[...]
        for hd in range(N_QK):
            xv = p_ref[:, _head(hd)]
            w = qn_ref[...] if hd < ATTN_Q_HEADS else kn_ref[...]
            r = lax.rsqrt(jnp.mean(xv * xv, axis=-1, keepdims=True) + NORM_EPS)
            nrm = xv * r * w
            o_ref[:, _head(hd)] = (nrm * c + pltpu.roll(nrm, ATTN_HD // 2, 1) * s).astype(BF16)
        o_ref[:, N_QK * ATTN_HD:] = p_ref[:, N_QK * ATTN_HD:].astype(BF16)

    row = pl.BlockSpec((ts, QKV_DIM), lambda i: (i, 0))
    rot = pl.BlockSpec((ts, ATTN_HD), lambda i: (i, 0))
    vec = pl.BlockSpec((1, ATTN_HD), lambda i: (0, 0))
    return pl.pallas_call(
        body, name=name, grid=(S // ts,),
        in_specs=[row, rot, rot, vec, vec], out_specs=row,
        out_shape=jax.ShapeDtypeStruct((S, QKV_DIM), BF16),
        compiler_params=_params("parallel"),
    )(proj, cs, sn, qn, kn)


def _attn_prep_bwd(proj, dq, dk, dv, cs, sn, qn, kn, *, name):
    S = proj.shape[0]
    ts = _tile(S, 256, 16)
    nq = ATTN_Q_HEADS * ATTN_HD
    nkv = ATTN_KV_HEADS * ATTN_HD

    def body(p_ref, dq_ref, dk_ref, dv_ref, c_ref, s_ref, qn_ref, kn_ref, o_ref, dqn_ref, dkn_ref):
        i = pl.program_id(0)
        c, s = c_ref[...], s_ref[...]
        acc = [jnp.zeros((1, ATTN_HD), F32), jnp.zeros((1, ATTN_HD), F32)]
        for hd in range(N_QK):
            is_k = hd >= ATTN_Q_HEADS
            xv = p_ref[:, _head(hd)]
            w = kn_ref[...] if is_k else qn_ref[...]
            r = lax.rsqrt(jnp.mean(xv * xv, axis=-1, keepdims=True) + NORM_EPS)
            xh = xv * r
            dout = dk_ref[:, _head(hd - ATTN_Q_HEADS)] if is_k else dq_ref[:, _head(hd)]
            dn = dout * c + pltpu.roll(dout * s, ATTN_HD // 2, 1)
            acc[int(is_k)] = acc[int(is_k)] + _colsum(dn * xh)
            g = dn * w
            o_ref[:, _head(hd)] = (r * (g - xh * jnp.mean(g * xh, axis=-1, keepdims=True))).astype(BF16)
        o_ref[:, N_QK * ATTN_HD:] = dv_ref[...].astype(BF16)

        @pl.when(i == 0)
        def _():
            dqn_ref[...] = acc[0]
            dkn_ref[...] = acc[1]

        @pl.when(i > 0)
        def _():
            dqn_ref[...] += acc[0]
            dkn_ref[...] += acc[1]

    row = pl.BlockSpec((ts, QKV_DIM), lambda i: (i, 0))
    rot = pl.BlockSpec((ts, ATTN_HD), lambda i: (i, 0))
    vec = pl.BlockSpec((1, ATTN_HD), lambda i: (0, 0))
    return pl.pallas_call(
        body, name=name, grid=(S // ts,),
        in_specs=[row, pl.BlockSpec((ts, nq), lambda i: (i, 0)), pl.BlockSpec((ts, nkv), lambda i: (i, 0)),
                  pl.BlockSpec((ts, nkv), lambda i: (i, 0)), rot, rot, vec, vec],
        out_specs=[row, vec, vec],
        out_shape=[jax.ShapeDtypeStruct((S, QKV_DIM), BF16), jax.ShapeDtypeStruct((1, ATTN_HD), F32),
                   jax.ShapeDtypeStruct((1, ATTN_HD), F32)],
        compiler_params=_params("arbitrary"),
    )(proj, dq, dk, dv, cs, sn, qn, kn)


ATTN_SCALE = ATTN_HD ** -0.5


def _softmax_of(s):
    e = jnp.exp2((s - jnp.max(s, axis=-1, keepdims=True)) * (ATTN_SCALE * math.log2(math.e)))
    return e, 1.0 / jnp.sum(e, axis=-1, keepdims=True)


def _softmax_parts(q, k):
    return _softmax_of(_dot_nt(q, k))


def _attn_fwd(qkv, *, name):
    S = qkv.shape[0]
    tq = _tile(S, 512, 16)
    sub = _tile(tq, 256, 16)

    def body(q_ref, k_ref, v_ref, o_ref):
        k, v = k_ref[...], v_ref[...]
        for r in range(tq // sub):
            rows = slice(r * sub, (r + 1) * sub)
            e, rl = _softmax_parts(q_ref[rows, :], k)
            o_ref[rows, :] = (_dot(e.astype(BF16), v) * rl).astype(BF16)

    return pl.pallas_call(
        body, name=name, grid=(ATTN_Q_HEADS, S // tq),
        in_specs=[pl.BlockSpec((tq, ATTN_HD), lambda h, i: (i, h)),
                  pl.BlockSpec((S, ATTN_HD), lambda h, i: (0, ATTN_Q_HEADS + h // ATTN_GROUP)),
                  pl.BlockSpec((S, ATTN_HD), lambda h, i: (0, N_QK + h // ATTN_GROUP))],
        out_specs=pl.BlockSpec((tq, ATTN_HD), lambda h, i: (i, h)),
        out_shape=jax.ShapeDtypeStruct((S, ATTN_Q_HEADS * ATTN_HD), BF16),
        compiler_params=_params("parallel", "parallel"),
    )(qkv, qkv, qkv)


def _attn_bwd(qkv, do, *, name):
    S = qkv.shape[0]
    tq = _tile(S, 256, 16)
    n = S // tq

    def body(q_ref, do_ref, qn_ref, don_ref, k_ref, v_ref, dq_ref, dk_ref, dv_ref, s_a, dp_a, s_b, dp_b):
        i = pl.program_id(2)

        @pl.when(i == 0)
        def _():
            s_a[...] = _dot_nt(q_ref[...], k_ref[...])
            dp_a[...] = _dot_nt(do_ref[...], v_ref[...])

            @pl.when(pl.program_id(1) == 0)
            def _():
                dk_ref[...] = jnp.zeros_like(dk_ref)
                dv_ref[...] = jnp.zeros_like(dv_ref)

        def step(s_cur, dp_cur, s_next, dp_next):
            k, v = k_ref[...], v_ref[...]
            q, dov = q_ref[...], do_ref[...]
            s_next[...] = _dot_nt(qn_ref[...], k)
            dp_next[...] = _dot_nt(don_ref[...], v)
            e, rl = _softmax_of(s_cur[...])
            dp = dp_cur[...]
            delta = jnp.sum(e * dp, axis=-1, keepdims=True) * rl
            dsb = (e * (dp - delta) * (rl * ATTN_SCALE)).astype(BF16)
            dq_ref[...] = _dot(dsb, k)
            dk_ref[...] += _dot_tn(dsb, q)
            dv_ref[...] += _dot_tn(e.astype(BF16), (dov.astype(F32) * rl).astype(BF16))

        @pl.when(i % 2 == 0)
        def _():
            step(s_a, dp_a, s_b, dp_b)

        @pl.when(i % 2 == 1)
        def _():
            step(s_b, dp_b, s_a, dp_a)

    qblk = pl.BlockSpec((tq, ATTN_HD), lambda kv, g, i: (i, kv * ATTN_GROUP + g))
    qnext = pl.BlockSpec((tq, ATTN_HD), lambda kv, g, i: (jnp.minimum(i + 1, n - 1), kv * ATTN_GROUP + g))
    kvacc = pl.BlockSpec((S, ATTN_HD), lambda kv, g, i: (0, kv))
    return pl.pallas_call(
        body, name=name, grid=(ATTN_KV_HEADS, ATTN_GROUP, n),
        in_specs=[qblk, qblk, qnext, qnext,
                  pl.BlockSpec((S, ATTN_HD), lambda kv, g, i: (0, ATTN_Q_HEADS + kv)),
                  pl.BlockSpec((S, ATTN_HD), lambda kv, g, i: (0, N_QK + kv))],
        out_specs=[qblk, kvacc, kvacc],
        out_shape=[jax.ShapeDtypeStruct((S, ATTN_Q_HEADS * ATTN_HD), F32),
                   jax.ShapeDtypeStruct((S, ATTN_KV_HEADS * ATTN_HD), F32),
                   jax.ShapeDtypeStruct((S, ATTN_KV_HEADS * ATTN_HD), F32)],
        scratch_shapes=[pltpu.VMEM((tq, S), F32) for _ in range(4)],
        compiler_params=_params("parallel", "arbitrary", "arbitrary"),
    )(qkv, do, qkv, do, qkv, qkv)


GLA_KD = GLA_HEADS * GLA_DK
GLA_VD = GLA_HEADS * GLA_DV
GLA_PROJ = 2 * GLA_KD + 2 * GLA_VD + LANES
GLA_SCALE = GLA_DK ** -0.5


def _split3(x):
    hi = x.astype(BF16)
    r1 = x - hi.astype(F32)
    mid = r1.astype(BF16)
    lo = (r1 - mid.astype(F32)).astype(BF16)
    return hi, mid, lo


def _cumdot(t, x):
    hi, mid, lo = _split3(x)
    return _dot(t, hi) + _dot(t, mid) + _dot(t, lo)


def _gla_masks(d):
    c = GLA_CHUNK
    row = lax.broadcasted_iota(jnp.int32, (c, c), 0)
    col = lax.broadcasted_iota(jnp.int32, (c, c), 1)
    lower, upper = col <= row, col >= row
    if d == 0:
        return lower.astype(BF16), upper.astype(BF16), lower
    return upper.astype(BF16), lower.astype(BF16), col > row


def _gla_decay(lg, bias, cum, d):
    xl = lg + bias
    la = _log_sigmoid(xl) * (1.0 / GLA_GATE_NORMALIZER)
    b = _cumdot(cum, la)
    b_end = b[GLA_CHUNK - 1:GLA_CHUNK, :] if d == 0 else b[0:1, :]
    return xl, b, b_end


def _gla_specs(S, n):
    c = GLA_CHUNK
    up = lambda i: i
    down = lambda i: n - 1 - i
    def specs(order):
        return dict(
            q=pl.BlockSpec((c, GLA_KD), lambda i: (order(i), 0)),
            k=pl.BlockSpec((c, GLA_KD), lambda i: (order(i), 1)),
            v=pl.BlockSpec((c, GLA_VD), lambda i: (order(i), 1)),
            st=pl.BlockSpec((1, GLA_HEADS, GLA_DV, GLA_DK), lambda i: (order(i), 0, 0, 0)),
            wide=pl.BlockSpec((c, GLA_VD), lambda i: (order(i), 0)),
            qkv=pl.BlockSpec((c, 2 * GLA_KD + GLA_VD), lambda i: (order(i), 0)),
        )
    return specs(up), specs(down), up, down


def _gla_fwd(proj, logits, bias, *, name):
    S = proj.shape[0]
    c = GLA_CHUNK
    n = S // c
    su, sd, up, down = _gla_specs(S, n)

    def body(qf, kf, vf, lf, qb, kb, vb, lb, bias_ref, of, ob, sf, sb, st):
        @pl.when(pl.program_id(0) == 0)
        def _():
            st[...] = jnp.zeros_like(st)

        for d, (q_r, k_r, v_r, l_r, o_r, s_r) in enumerate(((qf, kf, vf, lf, of, sf), (qb, kb, vb, lb, ob, sb))):
            cum, _, mask = _gla_masks(d)
            _, b, b_end = _gla_decay(l_r[...], bias_ref[:, d * GLA_KD:(d + 1) * GLA_KD], cum, d)
            dend = jnp.exp(b_end)
            k = k_r[...]
            qd = (q_r[...] * GLA_SCALE * jnp.exp(b)).astype(BF16)
            ki = (k * jnp.exp(-b)).astype(BF16)
            ke = (k * jnp.exp(b_end - b)).astype(BF16)
            for h in range(GLA_HEADS):
                ks = slice(h * GLA_DK, (h + 1) * GLA_DK)
                vs = slice(h * GLA_DV, (h + 1) * GLA_DV)
                stp = st[d * GLA_HEADS + h]
                s_r[0, h] = stp
                v = v_r[:, vs].astype(BF16)
                att = jnp.where(mask, _dot_nt(qd[:, ks], ki[:, ks]), 0.0).astype(BF16)
                o_r[:, vs] = _dot(att, v) + _dot_nt(qd[:, ks], stp.astype(BF16))
                st[d * GLA_HEADS + h] = stp * dend[:, ks] + _dot_tn(v, ke[:, ks])

    lg_f = pl.BlockSpec((c, GLA_KD), lambda i: (up(i), 0))
    lg_b = pl.BlockSpec((c, GLA_KD), lambda i: (down(i), 1))
    return pl.pallas_call(
        body, name=name, grid=(n,),
        in_specs=[su["q"], su["k"], su["v"], lg_f, sd["q"], sd["k"], sd["v"], lg_b,
                  pl.BlockSpec((1, 2 * GLA_KD), lambda i: (0, 0))],
        out_specs=[su["wide"], sd["wide"], su["st"], sd["st"]],
        out_shape=[jax.ShapeDtypeStruct((S, GLA_VD), F32), jax.ShapeDtypeStruct((S, GLA_VD), F32),
                   jax.ShapeDtypeStruct((n, GLA_HEADS, GLA_DV, GLA_DK), F32),
                   jax.ShapeDtypeStruct((n, GLA_HEADS, GLA_DV, GLA_DK), F32)],
        scratch_shapes=[pltpu.VMEM((2 * GLA_HEADS, GLA_DV, GLA_DK), F32)],
        compiler_params=_params("arbitrary"),
    )(proj, proj, proj, logits, proj, proj, proj, logits, bias)


def _gla_bwd(proj, logits, bias, sf, sb, do, *, name):
    S = proj.shape[0]
    c = GLA_CHUNK
    n = S // c
    su, sd, up, down = _gla_specs(S, n)

    def body(qf, kf, vf, lf, stf, dof, qb, kb, vb, lb, stb, dob, bias_ref,
             dqkv_f, dlg_f, dqkv_b, dlg_b, dbias, dst):
        first = pl.program_id(0) == 0

        @pl.when(first)
        def _():
            dst[...] = jnp.zeros_like(dst)

        dbias_parts = []
        for d, (q_r, k_r, v_r, l_r, s_r, do_r, dqkv_r, dlg_r) in enumerate(
                ((qf, kf, vf, lf, stf, dof, dqkv_f, dlg_f), (qb, kb, vb, lb, stb, dob, dqkv_b, dlg_b))):
            cum, cum_t, mask = _gla_masks(d)
            xl, b, b_end = _gla_decay(l_r[...], bias_ref[:, d * GLA_KD:(d + 1) * GLA_KD], cum, d)
            e, ei, ee, dend = jnp.exp(b), jnp.exp(-b), jnp.exp(b_end - b), jnp.exp(b_end)
            k = k_r[...]
            qd32 = q_r[...] * GLA_SCALE * e
            ki32 = k * ei
            ke32 = k * ee
            qd, ki, ke = qd32.astype(BF16), ki32.astype(BF16), ke32.astype(BF16)
            db_parts, dbe_parts = [], []
            for h in range(GLA_HEADS):
                ks = slice(h * GLA_DK, (h + 1) * GLA_DK)
                vs = slice(h * GLA_DV, (h + 1) * GLA_DV)
                stp = s_r[0, h]
                dstn = dst[d * GLA_HEADS + h]
                dstn_b = dstn.astype(BF16)
                v = v_r[:, vs].astype(BF16)
                dov = do_r[:, vs]
                att = jnp.where(mask, _dot_nt(qd[:, ks], ki[:, ks]), 0.0).astype(BF16)
                datt = jnp.where(mask, _dot_nt(dov, v), 0.0).astype(BF16)
                dqkv_r[:, 2 * GLA_KD + h * GLA_DV:2 * GLA_KD + (h + 1) * GLA_DV] = (
                    _dot_tn(att, dov) + _dot_nt(ke[:, ks], dstn_b))
                dqd = _dot(datt, ki[:, ks]) + _dot(dov, stp.astype(BF16))
                dki = _dot_tn(datt, qd[:, ks])
                dke = _dot(v, dstn_b)
                d_dend = _colsum(stp * dstn)
                dst[d * GLA_HEADS + h] = _dot_tn(dov, qd[:, ks]) + dstn * dend[:, ks]
                dqkv_r[:, ks] = dqd * e[:, ks] * GLA_SCALE
                dqkv_r[:, GLA_KD + h * GLA_DK:GLA_KD + (h + 1) * GLA_DK] = dki * ei[:, ks] + dke * ee[:, ks]
                dke_ke = dke * ke32[:, ks]
                db_parts.append(dqd * qd32[:, ks] - dki * ki32[:, ks] - dke_ke)
                dbe_parts.append(_colsum(dke_ke) + d_dend * dend[:, ks])
            db = jnp.concatenate(db_parts, axis=1)
            db_end = jnp.concatenate(dbe_parts, axis=1)
            dla = _cumdot(cum_t, db) + db_end
            dlg = dla * (1.0 / GLA_GATE_NORMALIZER) * _sigmoid(-xl)
            dlg_r[...] = dlg
            dbias_parts.append(_colsum(dlg))
        dbv = jnp.concatenate(dbias_parts, axis=1)

        @pl.when(first)
        def _():
            dbias[...] = dbv

        @pl.when(jnp.logical_not(first))
        def _():
            dbias[...] += dbv

    lg_f = pl.BlockSpec((c, GLA_KD), lambda i: (down(i), 0))
    lg_b = pl.BlockSpec((c, GLA_KD), lambda i: (up(i), 1))
    dlg_f = pl.BlockSpec((c, GLA_KD), lambda i: (down(i), 0))
    dlg_b = pl.BlockSpec((c, GLA_KD), lambda i: (up(i), 0))
    return pl.pallas_call(
        body, name=name, grid=(n,),
        in_specs=[sd["q"], sd["k"], sd["v"], lg_f, sd["st"], sd["wide"],
                  su["q"], su["k"], su["v"], lg_b, su["st"], su["wide"],
                  pl.BlockSpec((1, 2 * GLA_KD), lambda i: (0, 0))],
        out_specs=[sd["qkv"], dlg_f, su["qkv"], dlg_b, pl.BlockSpec((1, 2 * GLA_KD), lambda i: (0, 0))],
        out_shape=[jax.ShapeDtypeStruct((S, 2 * GLA_KD + GLA_VD), F32), jax.ShapeDtypeStruct((S, GLA_KD), F32),
                   jax.ShapeDtypeStruct((S, 2 * GLA_KD + GLA_VD), F32), jax.ShapeDtypeStruct((S, GLA_KD), F32),
                   jax.ShapeDtypeStruct((1, 2 * GLA_KD), F32)],
        scratch_shapes=[pltpu.VMEM((2 * GLA_HEADS, GLA_DV, GLA_DK), F32)],
        compiler_params=_params("arbitrary"),
    )(proj, proj, proj, logits, sf, do, proj, proj, proj, logits, sb, do, bias)


def _gla_gate_fwd(of, ob, proj, w, *, name):
    S = of.shape[0]
    ts = _tile(S, 256, 16)

    def body(of_ref, ob_ref, g_ref, w_ref, y_ref):
        for h in range(GLA_HEADS):
            vs = slice(h * GLA_DV, (h + 1) * GLA_DV)
            o = of_ref[:, vs] + ob_ref[:, vs]
            r = lax.rsqrt(jnp.mean(o * o, axis=-1, keepdims=True) + NORM_EPS)
            g = g_ref[:, vs]
            y_ref[:, vs] = (o * r * w_ref[...] * (g * _sigmoid(g))).astype(BF16)

    wide = pl.BlockSpec((ts, GLA_VD), lambda i: (i, 0))
    return pl.pallas_call(
        body, name=name, grid=(S // ts,),
        in_specs=[wide, wide, pl.BlockSpec((ts, GLA_VD), lambda i: (i, 2)), pl.BlockSpec((1, GLA_DV), lambda i: (0, 0))],
        out_specs=wide,
        out_shape=jax.ShapeDtypeStruct((S, GLA_VD), BF16),
        compiler_params=_params("parallel"),
    )(of, ob, proj, w)


def _gla_gate_bwd(of, ob, proj, w, dy, *, name):
    S = of.shape[0]
    ts = _tile(S, 256, 16)

    def body(of_ref, ob_ref, g_ref, w_ref, dy_ref, do_ref, dg_ref, dw_ref):
        i = pl.program_id(0)
        acc = jnp.zeros((1, GLA_DV), F32)
        for h in range(GLA_HEADS):
            vs = slice(h * GLA_DV, (h + 1) * GLA_DV)
            o = of_ref[:, vs] + ob_ref[:, vs]
            r = lax.rsqrt(jnp.mean(o * o, axis=-1, keepdims=True) + NORM_EPS)
            oh = o * r
            g = g_ref[:, vs]
            sg = _sigmoid(g)
            dyv = dy_ref[:, vs]
            dn = dyv * (g * sg)
            dg_ref[:, vs] = dyv * (oh * w_ref[...]) * (sg * (1.0 + g * (1.0 - sg)))
            acc = acc + _colsum(dn * oh)
            gg = dn * w_ref[...]
            do_ref[:, vs] = (r * (gg - oh * jnp.mean(gg * oh, axis=-1, keepdims=True))).astype(BF16)

        @pl.when(i == 0)
        def _():
            dw_ref[...] = acc

        @pl.when(i > 0)
        def _():
            dw_ref[...] += acc

    wide = pl.BlockSpec((ts, GLA_VD), lambda i: (i, 0))
    vec = pl.BlockSpec((1, GLA_DV), lambda i: (0, 0))
    return pl.pallas_call(
        body, name=name, grid=(S // ts,),
        in_specs=[wide, wide, pl.BlockSpec((ts, GLA_VD), lambda i: (i, 2)), vec, wide],
        out_specs=[wide, wide, vec],
        out_shape=[jax.ShapeDtypeStruct((S, GLA_VD), BF16), jax.ShapeDtypeStruct((S, GLA_VD), F32),
                   jax.ShapeDtypeStruct((1, GLA_DV), F32)],
        compiler_params=_params("arbitrary"),
    )(of, ob, proj, w, dy)


def _gla_combine(dqkv_f, dqkv_b, dg, dr, *, name):
    S = dg.shape[0]
    ts = _tile(S, 512, 16)
    nqkv = 2 * GLA_KD + GLA_VD

    def body(f_ref, b_ref, g_ref, r_ref, o_ref):
        o_ref[:, :nqkv] = (f_ref[...] + b_ref[...]).astype(BF16)
        o_ref[:, nqkv:nqkv + GLA_VD] = g_ref[...].astype(BF16)
        o_ref[:, nqkv + GLA_VD:] = r_ref[...].astype(BF16)

    return pl.pallas_call(
        body, name=name, grid=(S // ts,),
        in_specs=[pl.BlockSpec((ts, nqkv), lambda i: (i, 0)), pl.BlockSpec((ts, nqkv), lambda i: (i, 0)),
                  pl.BlockSpec((ts, GLA_VD), lambda i: (i, 0)), pl.BlockSpec((ts, LANES), lambda i: (i, 0))],
        out_specs=pl.BlockSpec((ts, GLA_PROJ), lambda i: (i, 0)),
        out_shape=jax.ShapeDtypeStruct((S, GLA_PROJ), BF16),
        compiler_params=_params("parallel"),
    )(dqkv_f, dqkv_b, dg, dr)


def _loss_head(y, t, *, name):
    S, D = y.shape
    ts = _tile(S, 512, 16)
    n = S // ts

    def body(y_ref, t_ref, dy_ref, l_ref, acc):
        i = pl.program_id(0)
        diff = y_ref[...] - t_ref[...]
        dy_ref[...] = diff * (1.0 / D)
        part = _colsum(diff * diff)

        @pl.when(i == 0)
        def _():
            acc[...] = part

        @pl.when(i > 0)
        def _():
            acc[...] += part

        @pl.when(i == n - 1)
        def _():
            l_ref[...] = jnp.full(l_ref.shape, 0.5 / D, F32) * jnp.sum(acc[...])

    row = pl.BlockSpec((ts, D), lambda i: (i, 0))
    return pl.pallas_call(
        body, name=name, grid=(n,),
        in_specs=[row, row],
        out_specs=[row, pl.BlockSpec((SUBLANES, LANES), lambda i: (0, 0))],
        out_shape=[jax.ShapeDtypeStruct((S, D), F32), jax.ShapeDtypeStruct((SUBLANES, LANES), F32)],
        scratch_shapes=[pltpu.VMEM((1, D), F32)],
        compiler_params=_params("arbitrary"),
    )(y, t)


def _adamw(w, g, m, v, *, name):
    R, C = w.shape
    tr = _tile(R, 256, SUBLANES)

    def body(w_ref, g_ref, m_ref, v_ref, d_ref, nm_ref, nv_ref):
        gv = g_ref[...]
        nm = ADAM_B1 * m_ref[...] + (1.0 - ADAM_B1) * gv
        nv = ADAM_B2 * v_ref[...] + (1.0 - ADAM_B2) * (gv * gv)
        m_hat = nm / (1.0 - ADAM_B1 ** ADAM_STEP)
        v_hat = nv / (1.0 - ADAM_B2 ** ADAM_STEP)
        d_ref[...] = -ADAM_LR * (m_hat / (jnp.sqrt(v_hat) + ADAM_EPS) + ADAM_WD * w_ref[...])
        nm_ref[...] = nm
        nv_ref[...] = nv

    blk = pl.BlockSpec((tr, C), lambda i: (i, 0))
    shp = jax.ShapeDtypeStruct((R, C), F32)
    return pl.pallas_call(
        body, name=name, grid=(R // tr,),
        in_specs=[blk] * 4, out_specs=[blk] * 3, out_shape=[shp] * 3,
        compiler_params=_params("parallel"),
    )(w, g, m, v)


def _place():
    return lax.axis_index("x"), lax.axis_index("y"), lax.axis_index("c")


def _all_gather(block, *, name):
    R, L = block.shape

    def body(x_ref, out_ref, send_sems, recv_sems, local_sem):
        x, y, c = _place()
        me, sibling = (x, y, c), (x, y, 1 - c)
        chips = [(1 - x, y), (x, 1 - y), (1 - x, 1 - y)]

        def slot(px, py, pc):
            return out_ref.at[4 * px + 2 * py + pc]

        def copy(k, blk, to, src=None):
            return pltpu.make_async_remote_copy(
                src_ref=slot(*blk) if src is None else src, dst_ref=slot(*blk),
                send_sem=send_sems.at[k], recv_sem=recv_sems.at[k], device_id=to, device_id_type=MESH)

        mine = pltpu.make_async_copy(x_ref, slot(*me), local_sem)
        mine.start()
        first = [copy(0, me, sibling, src=x_ref)]
        first += [copy(1 + j, me, (*chip, c), src=x_ref) for j, chip in enumerate(chips)]
        for cp in first:
            cp.start()
        passed = [copy(4 + j, (*chip, c), sibling) for j, chip in enumerate(chips)]
        for j, chip in enumerate(chips):
            copy(1 + j, (*chip, c), me).wait_recv()
            passed[j].start()
        copy(0, sibling, me).wait_recv()
        for j, chip in enumerate(chips):
            copy(4 + j, (*chip, 1 - c), me).wait_recv()
        for cp in first + passed:
            cp.wait_send()
        mine.wait()

    return pl.pallas_call(
        body, name=name, in_specs=[ANY], out_specs=ANY,
        out_shape=jax.ShapeDtypeStruct((N_DEV, R, L), block.dtype),
        scratch_shapes=[pltpu.SemaphoreType.DMA((7,)), pltpu.SemaphoreType.DMA((7,)), pltpu.SemaphoreType.DMA],
    )(block)


HBM_SPEC = pl.BlockSpec(memory_space=pltpu.HBM)
SEM_SPEC = pl.BlockSpec(memory_space=pltpu.SEMAPHORE)
DATAFLOW = pltpu.SideEffectType.DATAFLOW_SIDE_EFFECTING


def _split_start(plan, srcs, lands, *, dep=(), name):
    make_copies, count = plan
    ns, nb = len(srcs), len(srcs) + len(lands)
    bufs = [pltpu.with_memory_space_constraint(a, pltpu.HBM) for a in list(srcs) + list(lands)]
    n_in = nb + len(dep)

    def body(*refs):
        send_sems, recv_sems, token = refs[n_in], refs[n_in + 1], refs[-1]
        for cp in make_copies(refs[:ns], refs[ns:nb], send_sems, recv_sems):
            cp.start()
        token[...] = jnp.zeros_like(token)

    outs = pl.pallas_call(
        body, name=name, in_specs=[HBM_SPEC] * nb + [ANY] * len(dep),
        out_specs=(SEM_SPEC, SEM_SPEC, *[HBM_SPEC] * nb, pl.BlockSpec(memory_space=pltpu.VMEM)),
        out_shape=(pltpu.SemaphoreType.DMA((count,)), pltpu.SemaphoreType.DMA((count,)),
                   *[pltpu.HBM(a.shape, a.dtype) for a in bufs], jax.ShapeDtypeStruct((SUBLANES, LANES), F32)),
        input_output_aliases={i: 2 + i for i in range(nb)},
        compiler_params=pltpu.CompilerParams(has_side_effects=DATAFLOW),
    )(*bufs, *dep)
    return dict(plan=plan, ns=ns, send=outs[0], recv=outs[1], bufs=list(outs[2:2 + nb]), token=outs[-1])


def _split_wait(started, after, *, name):
    make_copies, _ = started["plan"]
    ns, nb = started["ns"], len(started["bufs"])
    after = tuple(after) if isinstance(after, (tuple, list)) else (after,)

    def body(*refs):
        for cp in make_copies(refs[:ns], refs[ns:nb], refs[nb], refs[nb + 1]):
            cp.wait_send()
            cp.wait_recv()

    outs = pl.pallas_call(
        body, name=name, in_specs=[HBM_SPEC] * nb + [SEM_SPEC, SEM_SPEC] + [ANY] * len(after),
        out_specs=[HBM_SPEC] * nb,
        out_shape=[pltpu.HBM(a.shape, a.dtype) for a in started["bufs"]],
        input_output_aliases={i: i for i in range(nb)},
        compiler_params=pltpu.CompilerParams(has_side_effects=DATAFLOW),
    )(*started["bufs"], started["send"], started["recv"], *after)
    return list(outs[:ns]), list(outs[ns:])


def _remote(src, dst, send_sems, recv_sems, k, to):
    return pltpu.make_async_remote_copy(src_ref=src, dst_ref=dst, send_sem=send_sems.at[k], recv_sem=recv_sems.at[k],
                                        device_id=to, device_id_type=MESH)


def _other_chips(x, y):
    return [(1 - x, y), (x, 1 - y), (1 - x, 1 - y)]


def _gather_send_plan(n):
    def make(srcs, lands, send_sems, recv_sems):
        x, y, c = _place()
        targets = [(x, y, 1 - c)] + [(cx, cy, c) for cx, cy in _other_chips(x, y)]
        return [_remote(srcs[t], lands[t].at[4 * x + 2 * y + c], send_sems, recv_sems, 4 * t + k, to)
                for t in range(n) for k, to in enumerate(targets)]
    return make, 4 * n


def _gather_pass_plan(n):
    def make(srcs, lands, send_sems, recv_sems):
        x, y, c = _place()
        cps = []
        for t in range(n):
            for j, (cx, cy) in enumerate(_other_chips(x, y)):
                slot = lands[t].at[4 * cx + 2 * cy + c]
                cps.append(_remote(slot, slot, send_sems, recv_sems, 3 * t + j, (x, y, 1 - c)))
        return cps
    return make, 3 * n


def _reduce_sibling_plan(n):
    def make(srcs, lands, send_sems, recv_sems):
        x, y, c = _place()
        return [_remote(srcs[t].at[2 * k + 1 - c], lands[t].at[k], send_sems, recv_sems, 4 * t + k, (x, y, 1 - c))
                for t in range(n) for k in range(4)]
    return make, 4 * n


def _reduce_chip_plan(n):
    def make(srcs, lands, send_sems, recv_sems):
        x, y, c = _place()
        return [_remote(srcs[t].at[2 * cx + cy], lands[t].at[2 * x + y], send_sems, recv_sems, 3 * t + j, (cx, cy, c))
                for t in range(n) for j, (cx, cy) in enumerate(_other_chips(x, y))]
    return make, 3 * n


def _unshard_cols(g, own, dev_idx, groups, width, *, name):
    _, A, Bs = g.shape
    ta = _tile(A, 512, 16)

    def body(dev_ref, g_ref, own_ref, *o_refs):
        for o_ref, devs in zip(o_refs, groups):
            for q, d in enumerate(devs):
                o_ref[:, q * Bs:(q + 1) * Bs] = jnp.where(dev_ref[0] == d, own_ref[...], g_ref[d])
            if len(devs) * Bs < width:
                o_ref[:, len(devs) * Bs:] = jnp.zeros((ta, width - len(devs) * Bs), g.dtype)

    return pl.pallas_call(
        body, name=name,
        grid_spec=pltpu.PrefetchScalarGridSpec(
            num_scalar_prefetch=1, grid=(A // ta,),
            in_specs=[pl.BlockSpec((N_DEV, ta, Bs), lambda i, d: (0, i, 0)), pl.BlockSpec((ta, Bs), lambda i, d: (i, 0))],
            out_specs=[pl.BlockSpec((ta, width), lambda i, d: (i, 0)) for _ in groups]),
        out_shape=[jax.ShapeDtypeStruct((A, width), g.dtype) for _ in groups],
        compiler_params=_params("parallel"),
    )(dev_idx, g, own)


def _place_own(g, own, dev_idx, *, name):
    _, As, B = g.shape
    ta = _tile(As, 256, 16)

    def body(dev_ref, _, own_ref, o_ref):
        o_ref[...] = own_ref[...]

    out = pl.pallas_call(
        body, name=name,
        grid_spec=pltpu.PrefetchScalarGridSpec(
            num_scalar_prefetch=1, grid=(As // ta,),
            in_specs=[ANY, pl.BlockSpec((ta, B), lambda i, d: (i, 0))],
            out_specs=pl.BlockSpec((None, ta, B), lambda i, d: (d[0], i, 0))),
        out_shape=jax.ShapeDtypeStruct(g.shape, g.dtype),
        input_output_aliases={1: 0},
        compiler_params=_params("parallel"),
    )(dev_idx, g, own)
    return out.reshape(N_DEV * As, B)


def _add_sibling(g, buf, c_idx, *, name):
    _, A, B = g.shape
    ta = _tile(A, 1024, 16)

    def body(c_ref, g_ref, b_ref, o_ref):
        o_ref[...] = (g_ref[...].astype(F32) + b_ref[...].astype(F32)).astype(BF16)

    blk = pl.BlockSpec((None, ta, B), lambda k, i, c_ref: (k, i, 0))
    return pl.pallas_call(
        body, name=name,
        grid_spec=pltpu.PrefetchScalarGridSpec(
            num_scalar_prefetch=1, grid=(4, A // ta),
            in_specs=[pl.BlockSpec((None, ta, B), lambda k, i, c_ref: (2 * k + c_ref[0], i, 0)), blk],
            out_specs=blk),
        out_shape=jax.ShapeDtypeStruct((4, A, B), BF16),
        compiler_params=_params("parallel", "parallel"),
    )(c_idx, g, buf)


def _adamw_layer(w, m, v, own, parts, chip_idx, outs, layer, *, name):
    _, A, B = w.shape
    ta = _tile(A, 512, 16)

    def body(chip_ref, w_ref, m_ref, v_ref, own_ref, p_ref, *rest):
        g_ref, d_ref, nm_ref, nv_ref = rest[4:]
        gv = None
        for j in range(4):
            part = jnp.where(chip_ref[0] == j, own_ref[...], p_ref[j]).astype(F32)
            gv = part if gv is None else gv + part
        nm = ADAM_B1 * m_ref[...] + (1.0 - ADAM_B1) * gv
        nv = ADAM_B2 * v_ref[...] + (1.0 - ADAM_B2) * (gv * gv)
        m_hat = nm / (1.0 - ADAM_B1 ** ADAM_STEP)
        v_hat = nv / (1.0 - ADAM_B2 ** ADAM_STEP)
        g_ref[...] = gv
        d_ref[...] = -ADAM_LR * (m_hat / (jnp.sqrt(v_hat) + ADAM_EPS) + ADAM_WD * w_ref[...])
        nm_ref[...] = nm
        nv_ref[...] = nv

    blk = pl.BlockSpec((None, ta, B), lambda i, ch: (layer, i, 0))
    return pl.pallas_call(
        body, name=name,
        grid_spec=pltpu.PrefetchScalarGridSpec(
            num_scalar_prefetch=1, grid=(A // ta,),
            in_specs=[blk, blk, blk, pl.BlockSpec((None, ta, B), lambda i, ch: (ch[0], i, 0)),
                      pl.BlockSpec((4, ta, B), lambda i, ch: (0, i, 0))] + [ANY] * 4,
            out_specs=[blk] * 4),
        out_shape=[jax.ShapeDtypeStruct(o.shape, o.dtype) for o in outs],
        input_output_aliases={6 + q: q for q in range(4)},
        compiler_params=_params("parallel"),
    )(chip_idx, w, m, v, own, parts, *outs)


def _sum_slots(buf, *, name):
    n, R, L = buf.shape
    tr = _tile(R, 512, SUBLANES)

    def body(b_ref, o_ref):
        acc = b_ref[0]
        for j in range(1, n):
            acc = acc + b_ref[j]
        o_ref[...] = acc

    return pl.pallas_call(
        body, name=name, grid=(R // tr,),
        in_specs=[pl.BlockSpec((n, tr, L), lambda i: (0, i, 0))],
        out_specs=pl.BlockSpec((tr, L), lambda i: (i, 0)),
        out_shape=jax.ShapeDtypeStruct((R, L), buf.dtype),
        compiler_params=_params("parallel"),
    )(buf)


BIG = ("gla_w_in", "gla_w_out", "attn_w_qkv", "attn_w_out", "ffn_w_up", "ffn_w_down")
SMALL_SHARDED = ("gla_w_gate_up_f", "gla_w_gate_up_b", "ffn_w_conv")
REPLICATED = ("norm_mix", "norm_ffn", "gla_b_gate_f", "gla_b_gate_b", "gla_norm", "attn_q_norm", "attn_k_norm",
              "ffn_b_conv")
WEIGHTS = ("norm_mix", "norm_ffn", "gla_w_in", "gla_w_gate_up_f", "gla_b_gate_f", "gla_w_gate_up_b", "gla_b_gate_b",
           "gla_norm", "gla_w_out", "attn_w_qkv", "attn_q_norm", "attn_k_norm", "attn_w_out", "ffn_w_up", "ffn_w_conv",
           "ffn_b_conv", "ffn_w_down")


def _rows(flat, row_align):
    n = flat.shape[0]
    per = row_align * LANES
    padded = -(-n // per) * per
    return jnp.pad(flat, (0, padded - n)).reshape(padded // LANES, LANES)


def _side_by_side(gathered, own, dev):
    n, a, b = gathered.shape
    whole = lax.dynamic_update_index_in_dim(gathered, own, dev, 0)
    return jnp.transpose(whole, (1, 0, 2)).reshape(a, n * b)


def _layer_shards(w, i, mixer, ffn):
    j = i // 2
    parts = []
    if mixer and i % 2 == 0:
        parts += [("mix_in", w["gla_w_in"][j].astype(BF16)), ("mix_out", w["gla_w_out"][j].astype(BF16)),
                  ("gate_f", w["gla_w_gate_up_f"][j].astype(BF16)), ("gate_b", w["gla_w_gate_up_b"][j].astype(BF16))]
    elif mixer:
        parts += [("mix_in", w["attn_w_qkv"][j].astype(BF16)), ("mix_out", w["attn_w_out"][j].astype(BF16))]
    if ffn:
        parts += [("up", w["ffn_w_up"][i].astype(BF16)), ("down", w["ffn_w_down"][i].astype(BF16)),
                  ("conv", w["ffn_w_conv"][i])]
    return [n for n, _ in parts], [a for _, a in parts]


def _layer_weights(names, own, gathered, dev, dev_idx, i):
    own, got = dict(zip(names, own)), dict(zip(names, gathered))
    every = tuple(range(N_DEV))
    half = N_DEV // 2
    tag = "_l%d" % i
    out = {}
    if "mix_in" in got:
        width = GLA_PROJ if i % 2 == 0 else QKV_DIM
        (out["mix_in"],) = _unshard_cols(got["mix_in"], own["mix_in"], dev_idx, [every], width,
                                         name="unshard_mix_in" + tag)
        out["mix_out"] = _place_own(got["mix_out"], own["mix_out"], dev_idx, name="place_mix_out" + tag)
    if "gate_f" in got:
        out["gate"] = _gate_matrix(_side_by_side(got["gate_f"], own["gate_f"], dev),
                                   _side_by_side(got["gate_b"], own["gate_b"], dev))
    if "up" in got:
        f = got["up"].shape[-1] * half
        out["up_val"], out["up_gate"] = _unshard_cols(got["up"], own["up"], dev_idx, [every[:half], every[half:]], f,
                                                      name="unshard_ffn_up" + tag)
        out["down"] = _place_own(got["down"], own["down"], dev_idx, name="place_ffn_down" + tag)
        out["conv"] = _side_by_side(got["conv"], own["conv"], dev)
    return out


def _rope_tables(S):
    rows = S // GRID_W
    pairs = ATTN_HD // 4
    row_idx = jnp.repeat(jnp.arange(rows, dtype=F32), GRID_W)
    col_idx = jnp.tile(jnp.arange(GRID_W, dtype=F32), rows)
    inv_freq = ROPE_THETA ** (-jnp.arange(pairs, dtype=F32) / pairs)
    ang = jnp.concatenate([row_idx[:, None] * inv_freq, col_idx[:, None] * inv_freq], axis=-1)
    cos, sin = jnp.cos(ang), jnp.sin(ang)
    return jnp.concatenate([cos, cos], axis=-1), jnp.concatenate([-sin, sin], axis=-1)


def _gate_matrix(w_f, w_b):
    rk = w_f.shape[0]
    top = jnp.concatenate([w_f, jnp.zeros_like(w_f)], axis=1)
    mid = jnp.concatenate([jnp.zeros_like(w_b), w_b], axis=1)
    pad = jnp.zeros((LANES - 2 * rk, 2 * GLA_KD), w_f.dtype)
    return jnp.concatenate([top, mid, pad], axis=0)


def _local_step(x, target, rep, w, dev, idx):
    S, D = x.shape
    depth = rep["norm_mix"].shape[0]
    cs, sn = _rope_tables(S)
    row = lambda a: a.reshape(1, -1)
    ranks_cols = (GLA_PROJ - LANES, LANES)

    groups = [(0, True, False), (0, False, True)] + [(i, True, True) for i in range(1, depth)]
    sent = []
    for g, (i, mixer, ffn) in enumerate(groups):
        names, srcs = _layer_shards(w, i, mixer, ffn)
        lands = [lax.empty((N_DEV,) + a.shape, a.dtype) for a in srcs]
        dep = (sent[-1][1]["token"],) if sent else ()
        sent.append((names, _split_start(_gather_send_plan(len(srcs)), srcs, lands, dep=dep,
                                         name="weights_send_g%d" % g)))

    def arrive(g, after):
        names, started = sent[g]
        own, lands = _split_wait(started, after, name="weights_arrive_g%d" % g)
        return names, own, _split_start(_gather_pass_plan(len(lands)), [], lands, name="weights_pass_g%d" % g)

    def ready(g, passing, after):
        names, own, started = passing
        _, lands = _split_wait(started, after, name="weights_passed_g%d" % g)
        return _layer_weights(names, own, lands, dev, idx["dev"], groups[g][0])

    passing = arrive(0, sent[-1][1]["token"])
    wl = ready(0, passing, passing[2]["token"])
    passing = arrive(1, wl["mix_in"])

    saved = []
    for i in range(depth):
        j = i // 2
        sv = {"x0": x, "w": wl}
        h1 = _rmsnorm_fwd(x, row(rep["norm_mix"][i]), dep=(passing[2]["token"],) if i == 0 else (), name="norm_mix_fwd")
        sv["h1"] = h1
        if i % 2 == 0:
            bias = jnp.concatenate([rep["gla_b_gate_f"][j], rep["gla_b_gate_b"][j]]).reshape(1, -1)
            proj = _mm(h1, wl["mix_in"], name="gla_in_proj")
            logits = _mm(proj, wl["gate"], a_cols=ranks_cols, name="gla_gate_logits")
            of, ob, sf, sb = _gla_fwd(proj, logits, bias, name="gla_fwd")
            y = _gla_gate_fwd(of, ob, proj, row(rep["gla_norm"][j]), name="gla_gate_fwd")
            x = _mm(y, wl["mix_out"], res=x, name="gla_out_proj")
            sv.update(bias=bias, proj=proj, logits=logits, of=of, ob=ob, sf=sf, sb=sb, y=y)
        else:
            proj = _mm(h1, wl["mix_in"], name="attn_qkv_proj")
            qkv = _attn_prep(proj, cs, sn, row(rep["attn_q_norm"][j]), row(rep["attn_k_norm"][j]), name="attn_prep")
            o = _attn_fwd(qkv, name="attn_fwd")
            x = _mm(o, wl["mix_out"], res=x, name="attn_out_proj")
            sv.update(proj=proj, qkv=qkv, o=o)
        sv["x1"] = x
        if i == 0:
            wl.update(ready(1, passing, x))
        dep = ()
        if i + 1 < depth:
            passing = arrive(i + 2, x)
            dep = (passing[2]["token"],)
        h2 = _rmsnorm_fwd(x, row(rep["norm_ffn"][i]), dep=dep, name="norm_ffn_fwd")
        F = wl["down"].shape[0]
        wc, bc = wl["conv"], rep["ffn_b_conv"][i]
        wcv, wcg, bcv, bcg = wc[:, :F], wc[:, F:], row(bc[:F]), row(bc[F:])
        uv = _mm(h2, wl["up_val"], name="ffn_up_val")
        ug = _mm(h2, wl["up_gate"], name="ffn_up_gate")
        act = _conv_act_fwd(uv, ug, wcv, wcg, bcv, bcg, name="ffn_conv_act")
        x = _mm(act, wl["down"], res=x, name="ffn_down")
        sv.update(h2=h2, uv=uv, ug=ug, act=act, wcv=wcv, wcg=wcg, bcv=bcv, bcg=bcg)
        saved.append(sv)
        if i + 1 < depth:
            wl = ready(i + 2, passing, x)

    dx, loss_tile = _loss_head(x, target, name="loss_head")
    loss = loss_tile[0, 0]

    in_sibling_stage, in_chip_stage, reduced = [], [], []

    def advance(group, after):
        tokens = []
        for tag, keys, started in in_chip_stage:
            partial, lands = _split_wait(started, after, name="grads_chips_arrive_" + tag)
            reduced.append((keys, partial, lands))
        in_chip_stage.clear()
        for tag, keys, started in in_sibling_stage:
            stacks, lands = _split_wait(started, after, name="grads_sibling_arrive_" + tag)
            partial = [_add_sibling(s, b, idx["core"], name="grads_add_sibling_%s_%d" % (tag, q))
                       for q, (s, b) in enumerate(zip(stacks, lands))]
            bufs = [lax.empty(p.shape, p.dtype) for p in partial]
            started = _split_start(_reduce_chip_plan(len(partial)), partial, bufs, name="grads_chips_send_" + tag)
            in_chip_stage.append((tag, keys, started))
            tokens.append(started["token"])
        in_sibling_stage.clear()
        if group is not None:
            tag, keys, stacks = group
            bufs = [lax.empty((4,) + s.shape[1:], s.dtype) for s in stacks]
            started = _split_start(_reduce_sibling_plan(len(stacks)), stacks, bufs, name="grads_sibling_send_" + tag)
            in_sibling_stage.append((tag, keys, started))
            tokens.append(started["token"])
        return tuple(tokens)

    def stack_for(name):
        return lax.empty((N_DEV,) + tuple(w[name].shape[1:]), BF16)

    gl = {k: [None] * depth for k in ("norm_mix", "norm_ffn", "ffn_w_conv", "ffn_b_conv")}
    gm = {k: [None] * (depth // 2) for k in ("gla_w_gate_up_f", "gla_b_gate_f", "gla_w_gate_up_b", "gla_b_gate_b",
                                             "gla_norm", "attn_q_norm", "attn_k_norm")}
    rk = GLA_GATE_RANK
    dep = ()
    for i in reversed(range(depth)):
        j = i // 2
        sv = saved[i]
        wl = sv["w"]
        dact = _mm(dx, wl["down"], tb=True, dep=dep, name="ffn_down_dgrad")
        g_down = _wgrad(sv["act"], dx, stack_for("ffn_w_down"), shard="rows", name="ffn_down_wgrad")
        dcv, dcg, dwv, dwg, dbv, dbg = _conv_act_bwd(sv["uv"], sv["ug"], sv["wcv"], sv["wcg"], sv["bcv"], sv["bcg"],
                                                     dact, name="ffn_conv_act_bwd")
        gl["ffn_w_conv"][i] = jnp.concatenate([dwv, dwg], axis=1)
        gl["ffn_b_conv"][i] = jnp.concatenate([dbv, dbg], axis=1)[0]
        mid = advance(None, dcv)
        duv = _conv_t(dcv, sv["wcv"], name="ffn_conv_t")
        dug = _conv_t(dcg, sv["wcg"], name="ffn_conv_t")
        dh2 = _mm(duv, wl["up_val"], tb=True, dep=mid, name="ffn_up_dgrad_val")
        dh2 = _mm(dug, wl["up_gate"], tb=True, res=dh2, name="ffn_up_dgrad_gate")
        g_up = _wgrad(sv["h2"], duv, stack_for("ffn_w_up"), shard="cols", group=0, name="ffn_up_wgrad_val")
        g_up = _wgrad(sv["h2"], dug, g_up, shard="cols", group=1, name="ffn_up_wgrad_gate")
        dx, dn = _rmsnorm_bwd(sv["x1"], row(rep["norm_ffn"][i]), dh2, dx, name="norm_ffn_bwd")
        gl["norm_ffn"][i] = dn[0]
        dep = advance(("ffn_l%d" % i, [("ffn_w_up", i), ("ffn_w_down", i)], [g_up, g_down]), dx)
        if i % 2 == 0:
            dy = _mm(dx, wl["mix_out"], tb=True, dep=dep, name="gla_out_dgrad")
            g_out = _wgrad(sv["y"], dx, stack_for("gla_w_out"), shard="rows", name="gla_out_wgrad")
            do, dg, dgn = _gla_gate_bwd(sv["of"], sv["ob"], sv["proj"], row(rep["gla_norm"][j]), dy, name="gla_gate_bwd")
            gm["gla_norm"][j] = dgn[0]
            dqkv_f, dlg_f, dqkv_b, dlg_b, dbias = _gla_bwd(sv["proj"], sv["logits"], sv["bias"], sv["sf"], sv["sb"], do,
                                                           name="gla_bwd")
            gm["gla_b_gate_f"][j] = dbias[0, :GLA_KD]
            gm["gla_b_gate_b"][j] = dbias[0, GLA_KD:]
            mid = advance(None, dqkv_f)
            dlogits = jnp.concatenate([dlg_f, dlg_b], axis=1)
            dr = _mm(dlogits, wl["gate"], tb=True, dep=mid, name="gla_gate_dgrad")
            dwg_full = _mm(sv["proj"], dlogits, ta=True, a_cols=ranks_cols, name="gla_gate_wgrad")
            gm["gla_w_gate_up_f"][j] = dwg_full[:rk, :GLA_KD]
            gm["gla_w_gate_up_b"][j] = dwg_full[rk:2 * rk, GLA_KD:]
            dproj = _gla_combine(dqkv_f, dqkv_b, dg, dr, name="gla_combine")
            dh1 = _mm(dproj, wl["mix_in"], tb=True, name="gla_in_dgrad")
            g_in = _wgrad(sv["h1"], dproj, stack_for("gla_w_in"), shard="cols", name="gla_in_wgrad")
            keys = [("gla_w_in", j), ("gla_w_out", j)]
        else:
            do = _mm(dx, wl["mix_out"], tb=True, out_dtype=BF16, dep=dep, name="attn_out_dgrad")
            g_out = _wgrad(sv["o"], dx, stack_for("attn_w_out"), shard="rows", name="attn_out_wgrad")
            dq, dk, dv = _attn_bwd(sv["qkv"], do, name="attn_bwd")
            mid = advance(None, dq)
            dproj, dqn, dkn = _attn_prep_bwd(sv["proj"], dq, dk, dv, cs, sn, row(rep["attn_q_norm"][j]),
                                             row(rep["attn_k_norm"][j]), name="attn_prep_bwd")
            gm["attn_q_norm"][j] = dqn[0]
            gm["attn_k_norm"][j] = dkn[0]
            dh1 = _mm(dproj, wl["mix_in"], tb=True, dep=mid, name="attn_qkv_dgrad")
            g_in = _wgrad(sv["h1"], dproj, stack_for("attn_w_qkv"), shard="cols", name="attn_qkv_wgrad")
            keys = [("attn_w_qkv", j), ("attn_w_out", j)]
        dx, dn = _rmsnorm_bwd(sv["x0"], row(rep["norm_mix"][i]), dh1, dx, name="norm_mix_bwd")
        gl["norm_mix"][i] = dn[0]
        dep = advance(("mix_l%d" % i, keys, [g_in, g_out]), dx)

    small = {k: jnp.stack(v) for k, v in {**gl, **gm}.items()}
    return loss, dx, reduced, small, advance


def kernel(x, norm_mix, norm_ffn, gla_w_in, gla_w_gate_up_f, gla_b_gate_f, gla_w_gate_up_b, gla_b_gate_b, gla_norm, gla_w_out, attn_w_qkv, attn_q_norm, attn_k_norm, attn_w_out, ffn_w_up, ffn_w_conv, ffn_b_conv, ffn_w_down, loss_target, m_norm_mix, m_norm_ffn, m_gla_w_in, m_gla_w_gate_up_f, m_gla_b_gate_f, m_gla_w_gate_up_b, m_gla_b_gate_b, m_gla_norm, m_gla_w_out, m_attn_w_qkv, m_attn_q_norm, m_attn_k_norm, m_attn_w_out, m_ffn_w_up, m_ffn_w_conv, m_ffn_b_conv, m_ffn_w_down, v_norm_mix, v_norm_ffn, v_gla_w_in, v_gla_w_gate_up_f, v_gla_b_gate_f, v_gla_w_gate_up_b, v_gla_b_gate_b, v_gla_norm, v_gla_w_out, v_attn_w_qkv, v_attn_q_norm, v_attn_k_norm, v_attn_w_out, v_ffn_w_up, v_ffn_w_conv, v_ffn_b_conv, v_ffn_w_down):
    given = dict(locals())
    w = {n: given[n] for n in WEIGHTS}
    m = {n: given["m_" + n] for n in WEIGHTS}
    v = {n: given["v_" + n] for n in WEIGHTS}
    shards = {n: w[n] for n in BIG + SMALL_SHARDED}
    rep = {n: w[n] for n in REPLICATED}

    x_pos, y_pos, c_pos = _place()
    dev = 4 * x_pos + 2 * y_pos + c_pos
    as_operand = lambda s: jnp.asarray(s, jnp.int32).reshape(1)
    idx = dict(dev=as_operand(dev), chip=as_operand(2 * x_pos + y_pos), core=as_operand(c_pos))

    loss_local, grad_x, reduced, small, advance = _local_step(x[0], loss_target[0], rep, shards, dev, idx)
    loss = lax.psum(loss_local, ("x", "y", "c"))

    big = {n: [lax.empty(w[n].shape, F32) for _ in range(4)] for n in BIG}

    def update_reduced():
        for keys, own, parts in reduced:
            for (n, layer), p_own, p_others in zip(keys, own, parts):
                big[n] = _adamw_layer(w[n], m[n], v[n], p_own, p_others, idx["chip"], big[n], layer,
                                      name="adamw_%s_l%d" % (n, layer))
        reduced.clear()

    advance(None, grad_x)
    update_reduced()

    rest = REPLICATED + SMALL_SHARDED
    flat = _rows(jnp.concatenate([small[n].reshape(-1) for n in rest]), 512)
    total = _sum_slots(_all_gather(flat, name="small_grads_all_gather"), name="small_grads_sum").reshape(-1)
    advance(None, [total, *big["ffn_w_up"]])
    update_reduced()
    g, off = {}, 0
    for n in rest:
        whole = total[off:off + small[n].size].reshape(small[n].shape)
        off += small[n].size
        width = w[n].shape[-1]
        g[n] = whole if n in REPLICATED else lax.dynamic_slice_in_dim(whole, dev * width, width, axis=whole.ndim - 1)

    delta, new_m, new_v = {}, {}, {}
    for n in WEIGHTS:
        if n in BIG:
            g[n], delta[n], new_m[n], new_v[n] = big[n]
        else:
            shape = w[n].shape
            two_d = (-1, shape[-1])
            d2, m2, v2 = _adamw(w[n].reshape(two_d), g[n].reshape(two_d), m[n].reshape(two_d), v[n].reshape(two_d),
                                name="adamw_" + n)
            delta[n], new_m[n], new_v[n] = d2.reshape(shape), m2.reshape(shape), v2.reshape(shape)

    return (loss, grad_x[None], *[g[n] for n in WEIGHTS], *[delta[n] for n in WEIGHTS],
            *[new_m[n] for n in WEIGHTS], *[new_v[n] for n in WEIGHTS])
```

```python
import math

import jax
import jax.numpy as jnp
from jax import lax
from jax.experimental import pallas as pl
from jax.experimental.pallas import tpu as pltpu

F32 = jnp.float32
BF16 = jnp.bfloat16
MESH = pl.DeviceIdType.MESH

N_DEV = 8
LANES = 128
SUBLANES = 8
VMEM_LIMIT = 56 * 1024 * 1024

NORM_EPS = 1e-6
GRID_W = 64
ROPE_THETA = 10000.0
GLA_HEADS = 4
GLA_DK = 128
GLA_DV = 256
GLA_CHUNK = 64
GLA_GATE_RANK = 16
GLA_GATE_NORMALIZER = 16.0
ATTN_HD = 128
ATTN_Q_HEADS = 8
ATTN_KV_HEADS = 2
ATTN_GROUP = ATTN_Q_HEADS // ATTN_KV_HEADS

ADAM_LR = 0.001
ADAM_B1 = 0.9
ADAM_B2 = 0.999
ADAM_EPS = 1e-08
ADAM_WD = 0.01
ADAM_STEP = 10


def _tile(n, target, align=LANES):
    if n <= target:
        return n
    t = (target // align) * align
    while t >= align:
        if n % t == 0:
            return t
        t -= align
    return n


def _params(*sem):
    return pltpu.CompilerParams(dimension_semantics=sem, vmem_limit_bytes=VMEM_LIMIT)


def _dot(a, b):
    return lax.dot_general(a, b, (((1,), (0,)), ((), ())), preferred_element_type=F32)


def _dot_nt(a, b):
    return lax.dot_general(a, b, (((1,), (1,)), ((), ())), preferred_element_type=F32)


def _dot_tn(a, b):
    return lax.dot_general(a, b, (((0,), (0,)), ((), ())), preferred_element_type=F32)


def _sigmoid(x):
    return 0.5 * jnp.tanh(0.5 * x) + 0.5


def _log_sigmoid(x):
    return jnp.minimum(x, 0.0) - jnp.log(1.0 + jnp.exp(-jnp.abs(x)))


def _colsum(x):
    return jnp.sum(x, axis=0, keepdims=True)


ANY = pl.BlockSpec(memory_space=pl.ANY)


def _mm(a, b, *, ta=False, tb=False, res=None, out_dtype=F32, layer=None, a_cols=None, dep=(), name):
    if tb:
        N, K = b.shape[-2:]
    else:
        K, N = b.shape[-2:]
    a_rows, a_width = a.shape
    a_off = 0
    if a_cols is not None:
        a_off, a_width = a_cols
    if ta:
        M = a_width
        assert a_rows == K, (a.shape, b.shape, ta, tb)
    else:
        M = a_rows
        assert a_width == K, (a.shape, b.shape, ta, tb)
    tm = _tile(M, 1408) if ta else _tile(M, 1024, 16)
    tn = _tile(N, 1408)
    tk = _tile(K, 512, 16) if ta else _tile(K, 1408)
    nk = K // tk
    dims = (((0 if ta else 1,), (1 if tb else 0,)), ((), ()))

    n_in = 2 + (res is not None) + len(dep)

    def body(*refs):
        a_ref, b_ref = refs[:2]
        r_ref = refs[2] if res is not None else None
        o_ref = refs[n_in]
        scr = refs[n_in + 1:]
        part = lax.dot_general(a_ref[...].astype(BF16), b_ref[...].astype(BF16), dims, preferred_element_type=F32)

        def finish(acc):
            if r_ref is not None:
                acc = acc + r_ref[...]
            o_ref[...] = acc.astype(out_dtype)

        if nk == 1:
            finish(part)
        else:
            acc_ref = scr[0]
            k = pl.program_id(2)

            @pl.when(k == 0)
            def _():
                acc_ref[...] = part

            @pl.when(k > 0)
            def _():
                acc_ref[...] += part

            @pl.when(k == nk - 1)
            def _():
                finish(acc_ref[...])

    a_blk = a_off // (tm if ta else tk)
    assert a_off % (tm if ta else tk) == 0
    a_spec = (pl.BlockSpec((tk, tm), lambda i, j, k: (k, a_blk + i)) if ta
              else pl.BlockSpec((tm, tk), lambda i, j, k: (i, a_blk + k)))
    if layer is None:
        b_spec = pl.BlockSpec((tn, tk), lambda i, j, k: (j, k)) if tb else pl.BlockSpec((tk, tn), lambda i, j, k: (k, j))
    else:
        b_spec = (pl.BlockSpec((None, tn, tk), lambda i, j, k: (layer, j, k)) if tb
                  else pl.BlockSpec((None, tk, tn), lambda i, j, k: (layer, k, j)))
    o_spec = pl.BlockSpec((tm, tn), lambda i, j, k: (i, j))
    in_specs = [a_spec, b_spec] + ([o_spec] if res is not None else []) + [ANY] * len(dep)
    args = (a, b) + ((res,) if res is not None else ()) + tuple(dep)
    return pl.pallas_call(
        body, name=name, grid=(M // tm, N // tn, nk),
        in_specs=in_specs, out_specs=o_spec,
        out_shape=jax.ShapeDtypeStruct((M, N), out_dtype),
        scratch_shapes=[pltpu.VMEM((tm, tn), F32)] if nk > 1 else [],
        compiler_params=_params("parallel", "parallel", "arbitrary"),
    )(*args)


def _wgrad(a, b, stack, *, shard, group=0, name):
    S, N = b.shape
    M = a.shape[1]
    As, Bs = stack.shape[-2:]
    tk = _tile(S, 1024, 16)
    nk = S // tk
    if shard == "cols":
        n = N // Bs
        tm = _tile(M, 512)
        tn = N
        grid = (M // tm, 1, nk)
        o_spec = pl.BlockSpec((n, tm, Bs), lambda i, j, k: (group, i, 0))
    else:
        per = As * LANES // math.gcd(As, LANES)
        tm = M if M <= 1408 else _tile(M, 1408, per)
        n = tm // As
        tn = _tile(N, 1024)
        grid = (M // tm, N // tn, nk)
        o_spec = pl.BlockSpec((n, As, tn), lambda i, j, k: (i, 0, j))

    def body(a_ref, b_ref, _, o_ref, acc_ref):
        k = pl.program_id(2)
        part = _dot_tn(a_ref[...].astype(BF16), b_ref[...].astype(BF16))

        @pl.when(k == 0)
        def _():
            acc_ref[...] = part

        @pl.when(k > 0)
        def _():
            acc_ref[...] += part

        @pl.when(k == nk - 1)
        def _():
            for q in range(n):
                if shard == "cols":
                    o_ref[q] = acc_ref[:, q * Bs:(q + 1) * Bs].astype(stack.dtype)
                else:
                    o_ref[q] = acc_ref[q * As:(q + 1) * As, :].astype(stack.dtype)

    return pl.pallas_call(
        body, name=name, grid=grid,
        in_specs=[pl.BlockSpec((tk, tm), lambda i, j, k: (k, i)), pl.BlockSpec((tk, tn), lambda i, j, k: (k, j)),
                  pl.BlockSpec(memory_space=pl.ANY)],
        out_specs=o_spec,
        out_shape=jax.ShapeDtypeStruct(stack.shape, stack.dtype),
        input_output_aliases={2: 0},
        scratch_shapes=[pltpu.VMEM((tm, tn), F32)],
        compiler_params=_params("parallel", "parallel", "arbitrary"),
    )(a, b, stack)


def _rmsnorm_fwd(x, w, *, dep=(), name):
    S, D = x.shape
    ts = _tile(S, 1024, 16)

    def body(x_ref, w_ref, *rest):
        o_ref = rest[-1]
        xv = x_ref[...]
        r = lax.rsqrt(jnp.mean(xv * xv, axis=-1, keepdims=True) + NORM_EPS)
        o_ref[...] = (xv * r * w_ref[...]).astype(BF16)

    return pl.pallas_call(
        body, name=name, grid=(S // ts,),
        in_specs=[pl.BlockSpec((ts, D), lambda i: (i, 0)), pl.BlockSpec((1, D), lambda i: (0, 0))] + [ANY] * len(dep),
        out_specs=pl.BlockSpec((ts, D), lambda i: (i, 0)),
        out_shape=jax.ShapeDtypeStruct((S, D), BF16),
        compiler_params=_params("parallel"),
    )(x, w, *dep)


def _rmsnorm_bwd(x, w, dh, dres, *, name):
    S, D = x.shape
    ts = _tile(S, 512, 16)
    n = S // ts

    def body(x_ref, w_ref, dh_ref, dr_ref, dx_ref, dw_ref):
        i = pl.program_id(0)
        xv = x_ref[...]
        r = lax.rsqrt(jnp.mean(xv * xv, axis=-1, keepdims=True) + NORM_EPS)
        xh = xv * r
        d = dh_ref[...]
        g = d * w_ref[...]
        dx_ref[...] = dr_ref[...] + r * (g - xh * jnp.mean(g * xh, axis=-1, keepdims=True))
        part = _colsum(d * xh)

        @pl.when(i == 0)
        def _():
            dw_ref[...] = part

        @pl.when(i > 0)
        def _():
            dw_ref[...] += part

    row = pl.BlockSpec((ts, D), lambda i: (i, 0))
    vec = pl.BlockSpec((1, D), lambda i: (0, 0))
    return pl.pallas_call(
        body, name=name, grid=(n,),
        in_specs=[row, vec, row, row], out_specs=[row, vec],
        out_shape=[jax.ShapeDtypeStruct((S, D), F32), jax.ShapeDtypeStruct((1, D), F32)],
        compiler_params=_params("arbitrary"),
    )(x, w, dh, dres)


def _halo_specs(S, ts, tf, row_axis, rows=SUBLANES):
    g = ts // rows
    last = S // rows - 1
    col_axis = 1 - row_axis
    main = pl.BlockSpec((ts, tf), lambda *ij: (ij[row_axis], ij[col_axis]))
    prev = pl.BlockSpec((rows, tf), lambda *ij: (jnp.maximum(ij[row_axis] * g - 1, 0), ij[col_axis]))
    nxt = pl.BlockSpec((rows, tf), lambda *ij: (jnp.minimum((ij[row_axis] + 1) * g, last), ij[col_axis]))
    return [main, prev, nxt]


def _shifted(u, prev_ref, next_ref, i, n):
    ts = u.shape[0]
    rows = prev_ref.shape[0]
    rid = lax.broadcasted_iota(jnp.int32, u.shape, 0)
    before = jnp.where(i > 0, prev_ref[rows - 1:rows, :].astype(F32), 0.0)
    after = jnp.where(i < n - 1, next_ref[0:1, :].astype(F32), 0.0)
    um1 = jnp.where(rid == 0, before, pltpu.roll(u, 1, 0))
    up1 = jnp.where(rid == ts - 1, after, pltpu.roll(u, ts - 1, 0))
    return um1, up1


def _conv3(u, prev_ref, next_ref, w_ref, i, n):
    um1, up1 = _shifted(u, prev_ref, next_ref, i, n)
    return w_ref[0:1, :] * um1 + w_ref[1:2, :] * u + w_ref[2:3, :] * up1


def _conv_act_fwd(uv, ug, wv, wg, bv, bg, *, name):
    S, F = uv.shape
    ts = _tile(S, 512, 16)
    tf = _tile(F, 1408)
    n = S // ts

    def body(v_ref, vp_ref, vn_ref, g_ref, gp_ref, gn_ref, wv_ref, wg_ref, bv_ref, bg_ref, o_ref):
        i = pl.program_id(0)
        val = _conv3(v_ref[...], vp_ref, vn_ref, wv_ref, i, n) + bv_ref[...]
        gate = _conv3(g_ref[...], gp_ref, gn_ref, wg_ref, i, n) + bg_ref[...]
        o_ref[...] = (gate * _sigmoid(gate) * val).astype(BF16)

    halo = _halo_specs(S, ts, tf, 0)
    w3 = pl.BlockSpec((3, tf), lambda i, j: (0, j))
    b1 = pl.BlockSpec((1, tf), lambda i, j: (0, j))
    return pl.pallas_call(
        body, name=name, grid=(n, F // tf),
        in_specs=halo + halo + [w3, w3, b1, b1],
        out_specs=pl.BlockSpec((ts, tf), lambda i, j: (i, j)),
        out_shape=jax.ShapeDtypeStruct((S, F), BF16),
        compiler_params=_params("parallel", "parallel"),
    )(uv, uv, uv, ug, ug, ug, wv, wg, bv, bg)


def _conv_act_bwd(uv, ug, wv, wg, bv, bg, dact, *, name):
    S, F = uv.shape
    ts = _tile(S, 512, 16)
    tf = _tile(F, 1408)
    n = S // ts

    def body(v_ref, vp_ref, vn_ref, g_ref, gp_ref, gn_ref, wv_ref, wg_ref, bv_ref, bg_ref, da_ref,
             dv_ref, dg_ref, dwv_ref, dwg_ref, dbv_ref, dbg_ref):
        i = pl.program_id(1)
        uvv, ugv = v_ref[...], g_ref[...]
        vm1, vp1 = _shifted(uvv, vp_ref, vn_ref, i, n)
        gm1, gp1 = _shifted(ugv, gp_ref, gn_ref, i, n)
        val = wv_ref[0:1, :] * vm1 + wv_ref[1:2, :] * uvv + wv_ref[2:3, :] * vp1 + bv_ref[...]
        gate = wg_ref[0:1, :] * gm1 + wg_ref[1:2, :] * ugv + wg_ref[2:3, :] * gp1 + bg_ref[...]
        sg = _sigmoid(gate)
        da = da_ref[...]
        dval = da * (gate * sg)
        dgate = da * val * (sg * (1.0 + gate * (1.0 - sg)))
        dv_ref[...] = dval.astype(BF16)
        dg_ref[...] = dgate.astype(BF16)
        sums = [(dwv_ref, 0, vm1 * dval), (dwv_ref, 1, uvv * dval), (dwv_ref, 2, vp1 * dval),
                (dwg_ref, 0, gm1 * dgate), (dwg_ref, 1, ugv * dgate), (dwg_ref, 2, gp1 * dgate),
                (dbv_ref, 0, dval), (dbg_ref, 0, dgate)]
        parts = [(ref, r, _colsum(t)) for ref, r, t in sums]

        @pl.when(i == 0)
        def _():
            for ref, r, part in parts:
                ref[r:r + 1, :] = part

        @pl.when(i > 0)
        def _():
            for ref, r, part in parts:
                ref[r:r + 1, :] += part

    halo = _halo_specs(S, ts, tf, 1)
    w3 = pl.BlockSpec((3, tf), lambda j, i: (0, j))
    b1 = pl.BlockSpec((1, tf), lambda j, i: (0, j))
    blk = pl.BlockSpec((ts, tf), lambda j, i: (i, j))
    return pl.pallas_call(
        body, name=name, grid=(F // tf, n),
        in_specs=halo + halo + [w3, w3, b1, b1, blk],
        out_specs=[blk, blk, w3, w3, b1, b1],
        out_shape=[jax.ShapeDtypeStruct((S, F), BF16), jax.ShapeDtypeStruct((S, F), BF16),
                   jax.ShapeDtypeStruct((3, F), F32), jax.ShapeDtypeStruct((3, F), F32),
                   jax.ShapeDtypeStruct((1, F), F32), jax.ShapeDtypeStruct((1, F), F32)],
        compiler_params=_params("parallel", "arbitrary"),
    )(uv, uv, uv, ug, ug, ug, wv, wg, bv, bg, dact)


def _conv_t(duc, w, *, name):
    S, F = duc.shape
    ts = _tile(S, 512, 16)
    tf = _tile(F, 1408)
    n = S // ts

    def body(d_ref, dp_ref, dn_ref, w_ref, o_ref):
        i = pl.program_id(0)
        d = d_ref[...].astype(F32)
        dm1, dp1 = _shifted(d, dp_ref, dn_ref, i, n)
        o_ref[...] = (w_ref[0:1, :] * dp1 + w_ref[1:2, :] * d + w_ref[2:3, :] * dm1).astype(BF16)

    return pl.pallas_call(
        body, name=name, grid=(n, F // tf),
        in_specs=_halo_specs(S, ts, tf, 0, rows=16) + [pl.BlockSpec((3, tf), lambda i, j: (0, j))],
        out_specs=pl.BlockSpec((ts, tf), lambda i, j: (i, j)),
        out_shape=jax.ShapeDtypeStruct((S, F), BF16),
        compiler_params=_params("parallel", "parallel"),
    )(duc, duc, duc, w)


N_QK = ATTN_Q_HEADS + ATTN_KV_HEADS
QKV_DIM = (ATTN_Q_HEADS + 2 * ATTN_KV_HEADS) * ATTN_HD


def _head(hd):
    return slice(hd * ATTN_HD, (hd + 1) * ATTN_HD)


def _attn_prep(proj, cs, sn, qn, kn, *, name):
    S = proj.shape[0]
    ts = _tile(S, 512, 16)

    def body(p_ref, c_ref, s_ref, qn_ref, kn_ref, o_ref):
        c, s = c_ref[...], s_ref[...]
        for hd in range(N_QK):
            xv = p_ref[:, _head(hd)]
            w = qn_ref[...] if hd < ATTN_Q_HEADS else kn_ref[...]
            r = lax.rsqrt(jnp.mean(xv * xv, axis=-1, keepdims=True) + NORM_EPS)
            nrm = xv * r * w
            o_ref[:, _head(hd)] = (nrm * c + pltpu.roll(nrm, ATTN_HD // 2, 1) * s).astype(BF16)
        o_ref[:, N_QK * ATTN_HD:] = p_ref[:, N_QK * ATTN_HD:].astype(BF16)

    row = pl.BlockSpec((ts, QKV_DIM), lambda i: (i, 0))
    rot = pl.BlockSpec((ts, ATTN_HD), lambda i: (i, 0))
    vec = pl.BlockSpec((1, ATTN_HD), lambda i: (0, 0))
    return pl.pallas_call(
        body, name=name, grid=(S // ts,),
        in_specs=[row, rot, rot, vec, vec], out_specs=row,
        out_shape=jax.ShapeDtypeStruct((S, QKV_DIM), BF16),
        compiler_params=_params("parallel"),
    )(proj, cs, sn, qn, kn)


def _attn_prep_bwd(proj, dq, dk, dv, cs, sn, qn, kn, *, name):
    S = proj.shape[0]
    ts = _tile(S, 512, 16)
    nq = ATTN_Q_HEADS * ATTN_HD
    nkv = ATTN_KV_HEADS * ATTN_HD

    def body(p_ref, dq_ref, dk_ref, dv_ref, c_ref, s_ref, qn_ref, kn_ref, o_ref, dqn_ref, dkn_ref):
        i = pl.program_id(0)
        c, s = c_ref[...], s_ref[...]
        acc = [jnp.zeros((1, ATTN_HD), F32), jnp.zeros((1, ATTN_HD), F32)]
        for hd in range(N_QK):
            is_k = hd >= ATTN_Q_HEADS
            xv = p_ref[:, _head(hd)]
            w = kn_ref[...] if is_k else qn_ref[...]
            r = lax.rsqrt(jnp.mean(xv * xv, axis=-1, keepdims=True) + NORM_EPS)
            xh = xv * r
            dout = dk_ref[:, _head(hd - ATTN_Q_HEADS)] if is_k else dq_ref[:, _head(hd)]
            dn = dout * c + pltpu.roll(dout * s, ATTN_HD // 2, 1)
            acc[int(is_k)] = acc[int(is_k)] + _colsum(dn * xh)
            g = dn * w
            o_ref[:, _head(hd)] = (r * (g - xh * jnp.mean(g * xh, axis=-1, keepdims=True))).astype(BF16)
        o_ref[:, N_QK * ATTN_HD:] = dv_ref[...].astype(BF16)

        @pl.when(i == 0)
        def _():
            dqn_ref[...] = acc[0]
            dkn_ref[...] = acc[1]

        @pl.when(i > 0)
        def _():
            dqn_ref[...] += acc[0]
            dkn_ref[...] += acc[1]

    row = pl.BlockSpec((ts, QKV_DIM), lambda i: (i, 0))
    rot = pl.BlockSpec((ts, ATTN_HD), lambda i: (i, 0))
    vec = pl.BlockSpec((1, ATTN_HD), lambda i: (0, 0))
    return pl.pallas_call(
        body, name=name, grid=(S // ts,),
        in_specs=[row, pl.BlockSpec((ts, nq), lambda i: (i, 0)), pl.BlockSpec((ts, nkv), lambda i: (i, 0)),
                  pl.BlockSpec((ts, nkv), lambda i: (i, 0)), rot, rot, vec, vec],
        out_specs=[row, vec, vec],
        out_shape=[jax.ShapeDtypeStruct((S, QKV_DIM), BF16), jax.ShapeDtypeStruct((1, ATTN_HD), F32),
                   jax.ShapeDtypeStruct((1, ATTN_HD), F32)],
        compiler_params=_params("arbitrary"),
    )(proj, dq, dk, dv, cs, sn, qn, kn)


ATTN_SCALE = ATTN_HD ** -0.5


def _softmax_of(s):
    e = jnp.exp2((s - jnp.max(s, axis=-1, keepdims=True)) * (ATTN_SCALE * math.log2(math.e)))
    return e, 1.0 / jnp.sum(e, axis=-1, keepdims=True)


def _softmax_parts(q, k):
    return _softmax_of(_dot_nt(q, k))


def _attn_fwd(qkv, *, name):
    S = qkv.shape[0]
    tq = _tile(S, 512, 16)
    sub = _tile(tq, 256, 16)

    def body(q_ref, k_ref, v_ref, o_ref):
        k, v = k_ref[...], v_ref[...]
        for r in range(tq // sub):
            rows = slice(r * sub, (r + 1) * sub)
            e, rl = _softmax_parts(q_ref[rows, :], k)
            o_ref[rows, :] = (_dot(e.astype(BF16), v) * rl).astype(BF16)

    return pl.pallas_call(
        body, name=name, grid=(ATTN_Q_HEADS, S // tq),
        in_specs=[pl.BlockSpec((tq, ATTN_HD), lambda h, i: (i, h)),
                  pl.BlockSpec((S, ATTN_HD), lambda h, i: (0, ATTN_Q_HEADS + h // ATTN_GROUP)),
                  pl.BlockSpec((S, ATTN_HD), lambda h, i: (0, N_QK + h // ATTN_GROUP))],
        out_specs=pl.BlockSpec((tq, ATTN_HD), lambda h, i: (i, h)),
        out_shape=jax.ShapeDtypeStruct((S, ATTN_Q_HEADS * ATTN_HD), BF16),
        compiler_params=_params("parallel", "parallel"),
    )(qkv, qkv, qkv)


def _attn_bwd(qkv, do, *, name):
    S = qkv.shape[0]
    tq = _tile(S, 256, 16)
    n = S // tq

    def body(q_ref, do_ref, qn_ref, don_ref, k_ref, v_ref, dq_ref, dk_ref, dv_ref, s_a, dp_a, s_b, dp_b):
        i = pl.program_id(2)

        @pl.when(i == 0)
        def _():
            s_a[...] = _dot_nt(q_ref[...], k_ref[...])
            dp_a[...] = _dot_nt(do_ref[...], v_ref[...])

            @pl.when(pl.program_id(1) == 0)
            def _():
                dk_ref[...] = jnp.zeros_like(dk_ref)
                dv_ref[...] = jnp.zeros_like(dv_ref)

        def step(s_cur, dp_cur, s_next, dp_next):
            k, v = k_ref[...], v_ref[...]
            q, dov = q_ref[...], do_ref[...]
            s_next[...] = _dot_nt(qn_ref[...], k)
            dp_next[...] = _dot_nt(don_ref[...], v)
            e, rl = _softmax_of(s_cur[...])
            dp = dp_cur[...]
            delta = jnp.sum(e * dp, axis=-1, keepdims=True) * rl
            dsb = (e * (dp - delta) * (rl * ATTN_SCALE)).astype(BF16)
            dq_ref[...] = _dot(dsb, k)
            dk_ref[...] += _dot_tn(dsb, q)
            dv_ref[...] += _dot_tn(e.astype(BF16), (dov.astype(F32) * rl).astype(BF16))

        @pl.when(i % 2 == 0)
        def _():
            step(s_a, dp_a, s_b, dp_b)

        @pl.when(i % 2 == 1)
        def _():
            step(s_b, dp_b, s_a, dp_a)

    qblk = pl.BlockSpec((tq, ATTN_HD), lambda kv, g, i: (i, kv * ATTN_GROUP + g))
    qnext = pl.BlockSpec((tq, ATTN_HD), lambda kv, g, i: (jnp.minimum(i + 1, n - 1), kv * ATTN_GROUP + g))
    kvacc = pl.BlockSpec((S, ATTN_HD), lambda kv, g, i: (0, kv))
    return pl.pallas_call(
        body, name=name, grid=(ATTN_KV_HEADS, ATTN_GROUP, n),
        in_specs=[qblk, qblk, qnext, qnext,
                  pl.BlockSpec((S, ATTN_HD), lambda kv, g, i: (0, ATTN_Q_HEADS + kv)),
                  pl.BlockSpec((S, ATTN_HD), lambda kv, g, i: (0, N_QK + kv))],
        out_specs=[qblk, kvacc, kvacc],
        out_shape=[jax.ShapeDtypeStruct((S, ATTN_Q_HEADS * ATTN_HD), F32),
                   jax.ShapeDtypeStruct((S, ATTN_KV_HEADS * ATTN_HD), F32),
                   jax.ShapeDtypeStruct((S, ATTN_KV_HEADS * ATTN_HD), F32)],
        scratch_shapes=[pltpu.VMEM((tq, S), F32) for _ in range(4)],
        compiler_params=_params("parallel", "arbitrary", "arbitrary"),
    )(qkv, do, qkv, do, qkv, qkv)


GLA_KD = GLA_HEADS * GLA_DK
GLA_VD = GLA_HEADS * GLA_DV
GLA_PROJ = 2 * GLA_KD + 2 * GLA_VD + LANES
GLA_SCALE = GLA_DK ** -0.5


def _split3(x):
    hi = x.astype(BF16)
    r1 = x - hi.astype(F32)
    mid = r1.astype(BF16)
    lo = (r1 - mid.astype(F32)).astype(BF16)
    return hi, mid, lo


def _cumdot(t, x):
    hi, mid, lo = _split3(x)
    return _dot(t, hi) + _dot(t, mid) + _dot(t, lo)


def _gla_masks(d):
    c = GLA_CHUNK
    row = lax.broadcasted_iota(jnp.int32, (c, c), 0)
    col = lax.broadcasted_iota(jnp.int32, (c, c), 1)
    lower, upper = col <= row, col >= row
    if d == 0:
        return lower.astype(BF16), upper.astype(BF16), lower
    return upper.astype(BF16), lower.astype(BF16), col > row


def _gla_decay(lg, bias, cum, d):
    xl = lg + bias
    la = _log_sigmoid(xl) * (1.0 / GLA_GATE_NORMALIZER)
    b = _cumdot(cum, la)
    b_end = b[GLA_CHUNK - 1:GLA_CHUNK, :] if d == 0 else b[0:1, :]
    return xl, b, b_end


def _gla_specs(S, n):
    c = GLA_CHUNK
    up = lambda i: i
    down = lambda i: n - 1 - i
    def specs(order):
        return dict(
            q=pl.BlockSpec((c, GLA_KD), lambda i: (order(i), 0)),
            k=pl.BlockSpec((c, GLA_KD), lambda i: (order(i), 1)),
            v=pl.BlockSpec((c, GLA_VD), lambda i: (order(i), 1)),
            st=pl.BlockSpec((1, GLA_HEADS, GLA_DV, GLA_DK), lambda i: (order(i), 0, 0, 0)),
            wide=pl.BlockSpec((c, GLA_VD), lambda i: (order(i), 0)),
            qkv=pl.BlockSpec((c, 2 * GLA_KD + GLA_VD), lambda i: (order(i), 0)),
        )
    return specs(up), specs(down), up, down


def _gla_fwd(proj, logits, bias, *, name):
    S = proj.shape[0]
    c = GLA_CHUNK
    n = S // c
    su, sd, up, down = _gla_specs(S, n)

    def body(qf, kf, vf, lf, qb, kb, vb, lb, bias_ref, of, ob, sf, sb, st):
        @pl.when(pl.program_id(0) == 0)
        def _():
            st[...] = jnp.zeros_like(st)

        for d, (q_r, k_r, v_r, l_r, o_r, s_r) in enumerate(((qf, kf, vf, lf, of, sf), (qb, kb, vb, lb, ob, sb))):
            cum, _, mask = _gla_masks(d)
            _, b, b_end = _gla_decay(l_r[...], bias_ref[:, d * GLA_KD:(d + 1) * GLA_KD], cum, d)
            dend = jnp.exp(b_end)
            k = k_r[...]
            qd = (q_r[...] * GLA_SCALE * jnp.exp(b)).astype(BF16)
            ki = (k * jnp.exp(-b)).astype(BF16)
            ke = (k * jnp.exp(b_end - b)).astype(BF16)
            for h in range(GLA_HEADS):
                ks = slice(h * GLA_DK, (h + 1) * GLA_DK)
                vs = slice(h * GLA_DV, (h + 1) * GLA_DV)
                stp = st[d * GLA_HEADS + h]
                s_r[0, h] = stp
                v = v_r[:, vs].astype(BF16)
                att = jnp.where(mask, _dot_nt(qd[:, ks], ki[:, ks]), 0.0).astype(BF16)
                o_r[:, vs] = _dot(att, v) + _dot_nt(qd[:, ks], stp.astype(BF16))
                st[d * GLA_HEADS + h] = stp * dend[:, ks] + _dot_tn(v, ke[:, ks])

    lg_f = pl.BlockSpec((c, GLA_KD), lambda i: (up(i), 0))
    lg_b = pl.BlockSpec((c, GLA_KD), lambda i: (down(i), 1))
    return pl.pallas_call(
        body, name=name, grid=(n,),
        in_specs=[su["q"], su["k"], su["v"], lg_f, sd["q"], sd["k"], sd["v"], lg_b,
                  pl.BlockSpec((1, 2 * GLA_KD), lambda i: (0, 0))],
        out_specs=[su["wide"], sd["wide"], su["st"], sd["st"]],
        out_shape=[jax.ShapeDtypeStruct((S, GLA_VD), F32), jax.ShapeDtypeStruct((S, GLA_VD), F32),
                   jax.ShapeDtypeStruct((n, GLA_HEADS, GLA_DV, GLA_DK), F32),
                   jax.ShapeDtypeStruct((n, GLA_HEADS, GLA_DV, GLA_DK), F32)],
        scratch_shapes=[pltpu.VMEM((2 * GLA_HEADS, GLA_DV, GLA_DK), F32)],
        compiler_params=_params("arbitrary"),
    )(proj, proj, proj, logits, proj, proj, proj, logits, bias)


def _gla_bwd(proj, logits, bias, sf, sb, do, *, name):
    S = proj.shape[0]
    c = GLA_CHUNK
    n = S // c
    su, sd, up, down = _gla_specs(S, n)

    def body(qf, kf, vf, lf, stf, dof, qb, kb, vb, lb, stb, dob, bias_ref,
             dqkv_f, dlg_f, dqkv_b, dlg_b, dbias, dst):
        first = pl.program_id(0) == 0

        @pl.when(first)
        def _():
            dst[...] = jnp.zeros_like(dst)

        dbias_parts = []
        for d, (q_r, k_r, v_r, l_r, s_r, do_r, dqkv_r, dlg_r) in enumerate(
                ((qf, kf, vf, lf, stf, dof, dqkv_f, dlg_f), (qb, kb, vb, lb, stb, dob, dqkv_b, dlg_b))):
            cum, cum_t, mask = _gla_masks(d)
            xl, b, b_end = _gla_decay(l_r[...], bias_ref[:, d * GLA_KD:(d + 1) * GLA_KD], cum, d)
            e, ei, ee, dend = jnp.exp(b), jnp.exp(-b), jnp.exp(b_end - b), jnp.exp(b_end)
            k = k_r[...]
            qd32 = q_r[...] * GLA_SCALE * e
            ki32 = k * ei
            ke32 = k * ee
            qd, ki, ke = qd32.astype(BF16), ki32.astype(BF16), ke32.astype(BF16)
            db_parts, dbe_parts = [], []
            for h in range(GLA_HEADS):
                ks = slice(h * GLA_DK, (h + 1) * GLA_DK)
                vs = slice(h * GLA_DV, (h + 1) * GLA_DV)
                stp = s_r[0, h]
                dstn = dst[d * GLA_HEADS + h]
                dstn_b = dstn.astype(BF16)
                v = v_r[:, vs].astype(BF16)
                dov = do_r[:, vs]
                att = jnp.where(mask, _dot_nt(qd[:, ks], ki[:, ks]), 0.0).astype(BF16)
                datt = jnp.where(mask, _dot_nt(dov, v), 0.0).astype(BF16)
                dqkv_r[:, 2 * GLA_KD + h * GLA_DV:2 * GLA_KD + (h + 1) * GLA_DV] = (
                    _dot_tn(att, dov) + _dot_nt(ke[:, ks], dstn_b))
                dqd = _dot(datt, ki[:, ks]) + _dot(dov, stp.astype(BF16))
                dki = _dot_tn(datt, qd[:, ks])
                dke = _dot(v, dstn_b)
                d_dend = _colsum(stp * dstn)
                dst[d * GLA_HEADS + h] = _dot_tn(dov, qd[:, ks]) + dstn * dend[:, ks]
                dqkv_r[:, ks] = dqd * e[:, ks] * GLA_SCALE
                dqkv_r[:, GLA_KD + h * GLA_DK:GLA_KD + (h + 1) * GLA_DK] = dki * ei[:, ks] + dke * ee[:, ks]
                dke_ke = dke * ke32[:, ks]
                db_parts.append(dqd * qd32[:, ks] - dki * ki32[:, ks] - dke_ke)
                dbe_parts.append(_colsum(dke_ke) + d_dend * dend[:, ks])
            db = jnp.concatenate(db_parts, axis=1)
            db_end = jnp.concatenate(dbe_parts, axis=1)
            dla = _cumdot(cum_t, db) + db_end
            dlg = dla * (1.0 / GLA_GATE_NORMALIZER) * _sigmoid(-xl)
            dlg_r[...] = dlg
            dbias_parts.append(_colsum(dlg))
        dbv = jnp.concatenate(dbias_parts, axis=1)

        @pl.when(first)
        def _():
            dbias[...] = dbv

        @pl.when(jnp.logical_not(first))
        def _():
            dbias[...] += dbv

    lg_f = pl.BlockSpec((c, GLA_KD), lambda i: (down(i), 0))
    lg_b = pl.BlockSpec((c, GLA_KD), lambda i: (up(i), 1))
    dlg_f = pl.BlockSpec((c, GLA_KD), lambda i: (down(i), 0))
    dlg_b = pl.BlockSpec((c, GLA_KD), lambda i: (up(i), 0))
    return pl.pallas_call(
        body, name=name, grid=(n,),
        in_specs=[sd["q"], sd["k"], sd["v"], lg_f, sd["st"], sd["wide"],
                  su["q"], su["k"], su["v"], lg_b, su["st"], su["wide"],
                  pl.BlockSpec((1, 2 * GLA_KD), lambda i: (0, 0))],
        out_specs=[sd["qkv"], dlg_f, su["qkv"], dlg_b, pl.BlockSpec((1, 2 * GLA_KD), lambda i: (0, 0))],
        out_shape=[jax.ShapeDtypeStruct((S, 2 * GLA_KD + GLA_VD), F32), jax.ShapeDtypeStruct((S, GLA_KD), F32),
                   jax.ShapeDtypeStruct((S, 2 * GLA_KD + GLA_VD), F32), jax.ShapeDtypeStruct((S, GLA_KD), F32),
                   jax.ShapeDtypeStruct((1, 2 * GLA_KD), F32)],
        scratch_shapes=[pltpu.VMEM((2 * GLA_HEADS, GLA_DV, GLA_DK), F32)],
        compiler_params=_params("arbitrary"),
    )(proj, proj, proj, logits, sf, do, proj, proj, proj, logits, sb, do, bias)


def _gla_gate_fwd(of, ob, proj, w, *, name):
    S = of.shape[0]
    ts = _tile(S, 512, 16)

    def body(of_ref, ob_ref, g_ref, w_ref, y_ref):
        for h in range(GLA_HEADS):
            vs = slice(h * GLA_DV, (h + 1) * GLA_DV)
            o = of_ref[:, vs] + ob_ref[:, vs]
            r = lax.rsqrt(jnp.mean(o * o, axis=-1, keepdims=True) + NORM_EPS)
            g = g_ref[:, vs]
            y_ref[:, vs] = (o * r * w_ref[...] * (g * _sigmoid(g))).astype(BF16)

    wide = pl.BlockSpec((ts, GLA_VD), lambda i: (i, 0))
    return pl.pallas_call(
        body, name=name, grid=(S // ts,),
        in_specs=[wide, wide, pl.BlockSpec((ts, GLA_VD), lambda i: (i, 2)), pl.BlockSpec((1, GLA_DV), lambda i: (0, 0))],
        out_specs=wide,
        out_shape=jax.ShapeDtypeStruct((S, GLA_VD), BF16),
        compiler_params=_params("parallel"),
    )(of, ob, proj, w)


def _gla_gate_bwd(of, ob, proj, w, dy, *, name):
    S = of.shape[0]
    ts = _tile(S, 512, 16)

    def body(of_ref, ob_ref, g_ref, w_ref, dy_ref, do_ref, dg_ref, dw_ref):
        i = pl.program_id(0)
        acc = jnp.zeros((1, GLA_DV), F32)
        for h in range(GLA_HEADS):
            vs = slice(h * GLA_DV, (h + 1) * GLA_DV)
            o = of_ref[:, vs] + ob_ref[:, vs]
            r = lax.rsqrt(jnp.mean(o * o, axis=-1, keepdims=True) + NORM_EPS)
            oh = o * r
            g = g_ref[:, vs]
            sg = _sigmoid(g)
            dyv = dy_ref[:, vs]
            dn = dyv * (g * sg)
            dg_ref[:, vs] = dyv * (oh * w_ref[...]) * (sg * (1.0 + g * (1.0 - sg)))
            acc = acc + _colsum(dn * oh)
            gg = dn * w_ref[...]
            do_ref[:, vs] = (r * (gg - oh * jnp.mean(gg * oh, axis=-1, keepdims=True))).astype(BF16)

        @pl.when(i == 0)
        def _():
            dw_ref[...] = acc

        @pl.when(i > 0)
        def _():
            dw_ref[...] += acc

    wide = pl.BlockSpec((ts, GLA_VD), lambda i: (i, 0))
    vec = pl.BlockSpec((1, GLA_DV), lambda i: (0, 0))
    return pl.pallas_call(
        body, name=name, grid=(S // ts,),
        in_specs=[wide, wide, pl.BlockSpec((ts, GLA_VD), lambda i: (i, 2)), vec, wide],
        out_specs=[wide, wide, vec],
        out_shape=[jax.ShapeDtypeStruct((S, GLA_VD), BF16), jax.ShapeDtypeStruct((S, GLA_VD), F32),
                   jax.ShapeDtypeStruct((1, GLA_DV), F32)],
        compiler_params=_params("arbitrary"),
    )(of, ob, proj, w, dy)


def _gla_combine(dqkv_f, dqkv_b, dg, dr, *, name):
    S = dg.shape[0]
    ts = _tile(S, 512, 16)
    nqkv = 2 * GLA_KD + GLA_VD

    def body(f_ref, b_ref, g_ref, r_ref, o_ref):
        o_ref[:, :nqkv] = (f_ref[...] + b_ref[...]).astype(BF16)
        o_ref[:, nqkv:nqkv + GLA_VD] = g_ref[...].astype(BF16)
        o_ref[:, nqkv + GLA_VD:] = r_ref[...].astype(BF16)

    return pl.pallas_call(
        body, name=name, grid=(S // ts,),
        in_specs=[pl.BlockSpec((ts, nqkv), lambda i: (i, 0)), pl.BlockSpec((ts, nqkv), lambda i: (i, 0)),
                  pl.BlockSpec((ts, GLA_VD), lambda i: (i, 0)), pl.BlockSpec((ts, LANES), lambda i: (i, 0))],
        out_specs=pl.BlockSpec((ts, GLA_PROJ), lambda i: (i, 0)),
        out_shape=jax.ShapeDtypeStruct((S, GLA_PROJ), BF16),
        compiler_params=_params("parallel"),
    )(dqkv_f, dqkv_b, dg, dr)


def _loss_head(y, t, *, name):
    S, D = y.shape
    ts = _tile(S, 512, 16)
    n = S // ts

    def body(y_ref, t_ref, dy_ref, l_ref, acc):
        i = pl.program_id(0)
        diff = y_ref[...] - t_ref[...]
        dy_ref[...] = diff * (1.0 / D)
        part = _colsum(diff * diff)

        @pl.when(i == 0)
        def _():
            acc[...] = part

        @pl.when(i > 0)
        def _():
            acc[...] += part

        @pl.when(i == n - 1)
        def _():
            l_ref[...] = jnp.full(l_ref.shape, 0.5 / D, F32) * jnp.sum(acc[...])

    row = pl.BlockSpec((ts, D), lambda i: (i, 0))
    return pl.pallas_call(
        body, name=name, grid=(n,),
        in_specs=[row, row],
        out_specs=[row, pl.BlockSpec((SUBLANES, LANES), lambda i: (0, 0))],
        out_shape=[jax.ShapeDtypeStruct((S, D), F32), jax.ShapeDtypeStruct((SUBLANES, LANES), F32)],
        scratch_shapes=[pltpu.VMEM((1, D), F32)],
        compiler_params=_params("arbitrary"),
    )(y, t)


def _adamw(w, g, m, v, *, name):
    R, C = w.shape
    tr = _tile(R, 256, SUBLANES)

    def body(w_ref, g_ref, m_ref, v_ref, d_ref, nm_ref, nv_ref):
        gv = g_ref[...]
        nm = ADAM_B1 * m_ref[...] + (1.0 - ADAM_B1) * gv
        nv = ADAM_B2 * v_ref[...] + (1.0 - ADAM_B2) * (gv * gv)
        m_hat = nm / (1.0 - ADAM_B1 ** ADAM_STEP)
        v_hat = nv / (1.0 - ADAM_B2 ** ADAM_STEP)
        d_ref[...] = -ADAM_LR * (m_hat / (jnp.sqrt(v_hat) + ADAM_EPS) + ADAM_WD * w_ref[...])
        nm_ref[...] = nm
        nv_ref[...] = nv

    blk = pl.BlockSpec((tr, C), lambda i: (i, 0))
    shp = jax.ShapeDtypeStruct((R, C), F32)
    return pl.pallas_call(
        body, name=name, grid=(R // tr,),
        in_specs=[blk] * 4, out_specs=[blk] * 3, out_shape=[shp] * 3,
        compiler_params=_params("parallel"),
    )(w, g, m, v)


def _place():
    return lax.axis_index("x"), lax.axis_index("y"), lax.axis_index("c")


def _all_gather(block, *, name):
    R, L = block.shape

    def body(x_ref, out_ref, send_sems, recv_sems, local_sem):
        x, y, c = _place()
        me, sibling = (x, y, c), (x, y, 1 - c)
        chips = [(1 - x, y), (x, 1 - y), (1 - x, 1 - y)]

        def slot(px, py, pc):
            return out_ref.at[4 * px + 2 * py + pc]

        def copy(k, blk, to, src=None):
            return pltpu.make_async_remote_copy(
                src_ref=slot(*blk) if src is None else src, dst_ref=slot(*blk),
                send_sem=send_sems.at[k], recv_sem=recv_sems.at[k], device_id=to, device_id_type=MESH)

        mine = pltpu.make_async_copy(x_ref, slot(*me), local_sem)
        mine.start()
        first = [copy(0, me, sibling, src=x_ref)]
        first += [copy(1 + j, me, (*chip, c), src=x_ref) for j, chip in enumerate(chips)]
        for cp in first:
            cp.start()
        passed = [copy(4 + j, (*chip, c), sibling) for j, chip in enumerate(chips)]
        for j, chip in enumerate(chips):
            copy(1 + j, (*chip, c), me).wait_recv()
            passed[j].start()
        copy(0, sibling, me).wait_recv()
        for j, chip in enumerate(chips):
            copy(4 + j, (*chip, 1 - c), me).wait_recv()
        for cp in first + passed:
            cp.wait_send()
        mine.wait()

    return pl.pallas_call(
        body, name=name, in_specs=[ANY], out_specs=ANY,
        out_shape=jax.ShapeDtypeStruct((N_DEV, R, L), block.dtype),
        scratch_shapes=[pltpu.SemaphoreType.DMA((7,)), pltpu.SemaphoreType.DMA((7,)), pltpu.SemaphoreType.DMA],
    )(block)


HBM_SPEC = pl.BlockSpec(memory_space=pltpu.HBM)
SEM_SPEC = pl.BlockSpec(memory_space=pltpu.SEMAPHORE)
DATAFLOW = pltpu.SideEffectType.DATAFLOW_SIDE_EFFECTING


def _split_start(plan, srcs, lands, *, dep=(), name):
    make_copies, count = plan
    ns, nb = len(srcs), len(srcs) + len(lands)
    bufs = [pltpu.with_memory_space_constraint(a, pltpu.HBM) for a in list(srcs) + list(lands)]
    n_in = nb + len(dep)

    def body(*refs):
        send_sems, recv_sems, token = refs[n_in], refs[n_in + 1], refs[-1]
        for cp in make_copies(refs[:ns], refs[ns:nb], send_sems, recv_sems):
            cp.start()
        token[...] = jnp.zeros_like(token)

    outs = pl.pallas_call(
        body, name=name, in_specs=[HBM_SPEC] * nb + [ANY] * len(dep),
        out_specs=(SEM_SPEC, SEM_SPEC, *[HBM_SPEC] * nb, pl.BlockSpec(memory_space=pltpu.VMEM)),
        out_shape=(pltpu.SemaphoreType.DMA((count,)), pltpu.SemaphoreType.DMA((count,)),
                   *[pltpu.HBM(a.shape, a.dtype) for a in bufs], jax.ShapeDtypeStruct((SUBLANES, LANES), F32)),
        input_output_aliases={i: 2 + i for i in range(nb)},
        compiler_params=pltpu.CompilerParams(has_side_effects=DATAFLOW),
    )(*bufs, *dep)
    return dict(plan=plan, ns=ns, send=outs[0], recv=outs[1], bufs=list(outs[2:2 + nb]), token=outs[-1])


def _split_wait(started, after, *, name):
    make_copies, _ = started["plan"]
    ns, nb = started["ns"], len(started["bufs"])
    after = tuple(after) if isinstance(after, (tuple, list)) else (after,)

    def body(*refs):
        for cp in make_copies(refs[:ns], refs[ns:nb], refs[nb], refs[nb + 1]):
            cp.wait_send()
            cp.wait_recv()

    outs = pl.pallas_call(
        body, name=name, in_specs=[HBM_SPEC] * nb + [SEM_SPEC, SEM_SPEC] + [ANY] * len(after),
        out_specs=[HBM_SPEC] * nb,
        out_shape=[pltpu.HBM(a.shape, a.dtype) for a in started["bufs"]],
        input_output_aliases={i: i for i in range(nb)},
        compiler_params=pltpu.CompilerParams(has_side_effects=DATAFLOW),
    )(*started["bufs"], started["send"], started["recv"], *after)
    return list(outs[:ns]), list(outs[ns:])


def _remote(src, dst, send_sems, recv_sems, k, to):
    return pltpu.make_async_remote_copy(src_ref=src, dst_ref=dst, send_sem=send_sems.at[k], recv_sem=recv_sems.at[k],
                                        device_id=to, device_id_type=MESH)


def _other_chips(x, y):
    return [(1 - x, y), (x, 1 - y), (1 - x, 1 - y)]


def _gather_send_plan(n):
    def make(srcs, lands, send_sems, recv_sems):
        x, y, c = _place()
        targets = [(x, y, 1 - c)] + [(cx, cy, c) for cx, cy in _other_chips(x, y)]
        return [_remote(srcs[t], lands[t].at[4 * x + 2 * y + c], send_sems, recv_sems, 4 * t + k, to)
                for t in range(n) for k, to in enumerate(targets)]
    return make, 4 * n


def _gather_pass_plan(n):
    def make(srcs, lands, send_sems, recv_sems):
        x, y, c = _place()
        cps = []
        for t in range(n):
            for j, (cx, cy) in enumerate(_other_chips(x, y)):
                slot = lands[t].at[4 * cx + 2 * cy + c]
                cps.append(_remote(slot, slot, send_sems, recv_sems, 3 * t + j, (x, y, 1 - c)))
        return cps
    return make, 3 * n


def _reduce_sibling_plan(n):
    def make(srcs, lands, send_sems, recv_sems):
        x, y, c = _place()
        return [_remote(srcs[t].at[2 * k + 1 - c], lands[t].at[k], send_sems, recv_sems, 4 * t + k, (x, y, 1 - c))
                for t in range(n) for k in range(4)]
    return make, 4 * n


def _reduce_chip_plan(n):
    def make(srcs, lands, send_sems, recv_sems):
        x, y, c = _place()
        return [_remote(srcs[t].at[2 * cx + cy], lands[t].at[2 * x + y], send_sems, recv_sems, 3 * t + j, (cx, cy, c))
                for t in range(n) for j, (cx, cy) in enumerate(_other_chips(x, y))]
    return make, 3 * n


def _unshard_cols(g, own, dev_idx, groups, width, *, name):
    _, A, Bs = g.shape
    ta = _tile(A, 512, 16)

    def body(dev_ref, g_ref, own_ref, *o_refs):
        for o_ref, devs in zip(o_refs, groups):
            for q, d in enumerate(devs):
                o_ref[:, q * Bs:(q + 1) * Bs] = jnp.where(dev_ref[0] == d, own_ref[...], g_ref[d])
            if len(devs) * Bs < width:
                o_ref[:, len(devs) * Bs:] = jnp.zeros((ta, width - len(devs) * Bs), g.dtype)

    return pl.pallas_call(
        body, name=name,
        grid_spec=pltpu.PrefetchScalarGridSpec(
            num_scalar_prefetch=1, grid=(A // ta,),
            in_specs=[pl.BlockSpec((N_DEV, ta, Bs), lambda i, d: (0, i, 0)), pl.BlockSpec((ta, Bs), lambda i, d: (i, 0))],
            out_specs=[pl.BlockSpec((ta, width), lambda i, d: (i, 0)) for _ in groups]),
        out_shape=[jax.ShapeDtypeStruct((A, width), g.dtype) for _ in groups],
        compiler_params=_params("parallel"),
    )(dev_idx, g, own)


def _place_own(g, own, dev_idx, *, name):
    _, As, B = g.shape
    ta = _tile(As, 256, 16)

    def body(dev_ref, _, own_ref, o_ref):
        o_ref[...] = own_ref[...]

    out = pl.pallas_call(
        body, name=name,
        grid_spec=pltpu.PrefetchScalarGridSpec(
            num_scalar_prefetch=1, grid=(As // ta,),
            in_specs=[ANY, pl.BlockSpec((ta, B), lambda i, d: (i, 0))],
            out_specs=pl.BlockSpec((None, ta, B), lambda i, d: (d[0], i, 0))),
        out_shape=jax.ShapeDtypeStruct(g.shape, g.dtype),
        input_output_aliases={1: 0},
        compiler_params=_params("parallel"),
    )(dev_idx, g, own)
    return out.reshape(N_DEV * As, B)


def _add_sibling(g, buf, c_idx, *, name):
    _, A, B = g.shape
    ta = _tile(A, 1024, 16)

    def body(c_ref, g_ref, b_ref, o_ref):
        o_ref[...] = (g_ref[...].astype(F32) + b_ref[...].astype(F32)).astype(BF16)

    blk = pl.BlockSpec((None, ta, B), lambda k, i, c_ref: (k, i, 0))
    return pl.pallas_call(
        body, name=name,
        grid_spec=pltpu.PrefetchScalarGridSpec(
            num_scalar_prefetch=1, grid=(4, A // ta),
            in_specs=[pl.BlockSpec((None, ta, B), lambda k, i, c_ref: (2 * k + c_ref[0], i, 0)), blk],
            out_specs=blk),
        out_shape=jax.ShapeDtypeStruct((4, A, B), BF16),
        compiler_params=_params("parallel", "parallel"),
    )(c_idx, g, buf)


def _adamw_layer(w, m, v, own, parts, chip_idx, outs, layer, *, name):
    _, A, B = w.shape
    ta = _tile(A, 512, 16)

    def body(chip_ref, w_ref, m_ref, v_ref, own_ref, p_ref, *rest):
        g_ref, d_ref, nm_ref, nv_ref = rest[4:]
        gv = None
        for j in range(4):
            part = jnp.where(chip_ref[0] == j, own_ref[...], p_ref[j]).astype(F32)
            gv = part if gv is None else gv + part
        nm = ADAM_B1 * m_ref[...] + (1.0 - ADAM_B1) * gv
        nv = ADAM_B2 * v_ref[...] + (1.0 - ADAM_B2) * (gv * gv)
        m_hat = nm / (1.0 - ADAM_B1 ** ADAM_STEP)
        v_hat = nv / (1.0 - ADAM_B2 ** ADAM_STEP)
        g_ref[...] = gv
        d_ref[...] = -ADAM_LR * (m_hat / (jnp.sqrt(v_hat) + ADAM_EPS) + ADAM_WD * w_ref[...])
        nm_ref[...] = nm
        nv_ref[...] = nv

    blk = pl.BlockSpec((None, ta, B), lambda i, ch: (layer, i, 0))
    return pl.pallas_call(
        body, name=name,
        grid_spec=pltpu.PrefetchScalarGridSpec(
            num_scalar_prefetch=1, grid=(A // ta,),
            in_specs=[blk, blk, blk, pl.BlockSpec((None, ta, B), lambda i, ch: (ch[0], i, 0)),
                      pl.BlockSpec((4, ta, B), lambda i, ch: (0, i, 0))] + [ANY] * 4,
            out_specs=[blk] * 4),
        out_shape=[jax.ShapeDtypeStruct(o.shape, o.dtype) for o in outs],
        input_output_aliases={6 + q: q for q in range(4)},
        compiler_params=_params("parallel"),
    )(chip_idx, w, m, v, own, parts, *outs)


def _sum_slots(buf, *, name):
    n, R, L = buf.shape
    tr = _tile(R, 512, SUBLANES)

    def body(b_ref, o_ref):
        acc = b_ref[0]
        for j in range(1, n):
            acc = acc + b_ref[j]
        o_ref[...] = acc

    return pl.pallas_call(
        body, name=name, grid=(R // tr,),
        in_specs=[pl.BlockSpec((n, tr, L), lambda i: (0, i, 0))],
        out_specs=pl.BlockSpec((tr, L), lambda i: (i, 0)),
        out_shape=jax.ShapeDtypeStruct((R, L), buf.dtype),
        compiler_params=_params("parallel"),
    )(buf)


BIG = ("gla_w_in", "gla_w_out", "attn_w_qkv", "attn_w_out", "ffn_w_up", "ffn_w_down")
SMALL_SHARDED = ("gla_w_gate_up_f", "gla_w_gate_up_b", "ffn_w_conv")
REPLICATED = ("norm_mix", "norm_ffn", "gla_b_gate_f", "gla_b_gate_b", "gla_norm", "attn_q_norm", "attn_k_norm",
              "ffn_b_conv")
WEIGHTS = ("norm_mix", "norm_ffn", "gla_w_in", "gla_w_gate_up_f", "gla_b_gate_f", "gla_w_gate_up_b", "gla_b_gate_b",
           "gla_norm", "gla_w_out", "attn_w_qkv", "attn_q_norm", "attn_k_norm", "attn_w_out", "ffn_w_up", "ffn_w_conv",
           "ffn_b_conv", "ffn_w_down")


def _rows(flat, row_align):
    n = flat.shape[0]
    per = row_align * LANES
    padded = -(-n // per) * per
    return jnp.pad(flat, (0, padded - n)).reshape(padded // LANES, LANES)


def _side_by_side(gathered, own, dev):
    n, a, b = gathered.shape
    whole = lax.dynamic_update_index_in_dim(gathered, own, dev, 0)
    return jnp.transpose(whole, (1, 0, 2)).reshape(a, n * b)


def _layer_shards(w, i, mixer, ffn):
    j = i // 2
    parts = []
    if mixer and i % 2 == 0:
        parts += [("mix_in", w["gla_w_in"][j].astype(BF16)), ("mix_out", w["gla_w_out"][j].astype(BF16)),
                  ("gate_f", w["gla_w_gate_up_f"][j].astype(BF16)), ("gate_b", w["gla_w_gate_up_b"][j].astype(BF16))]
    elif mixer:
        parts += [("mix_in", w["attn_w_qkv"][j].astype(BF16)), ("mix_out", w["attn_w_out"][j].astype(BF16))]
    if ffn:
        parts += [("up", w["ffn_w_up"][i].astype(BF16)), ("down", w["ffn_w_down"][i].astype(BF16)),
                  ("conv", w["ffn_w_conv"][i])]
    return [n for n, _ in parts], [a for _, a in parts]


def _layer_weights(names, own, gathered, dev, dev_idx, i):
    own, got = dict(zip(names, own)), dict(zip(names, gathered))
    every = tuple(range(N_DEV))
    half = N_DEV // 2
    tag = "_l%d" % i
    out = {}
    if "mix_in" in got:
        width = GLA_PROJ if i % 2 == 0 else QKV_DIM
        (out["mix_in"],) = _unshard_cols(got["mix_in"], own["mix_in"], dev_idx, [every], width,
                                         name="unshard_mix_in" + tag)
        out["mix_out"] = _place_own(got["mix_out"], own["mix_out"], dev_idx, name="place_mix_out" + tag)
    if "gate_f" in got:
        out["gate"] = _gate_matrix(_side_by_side(got["gate_f"], own["gate_f"], dev),
                                   _side_by_side(got["gate_b"], own["gate_b"], dev))
    if "up" in got:
        f = got["up"].shape[-1] * half
        out["up_val"], out["up_gate"] = _unshard_cols(got["up"], own["up"], dev_idx, [every[:half], every[half:]], f,
                                                      name="unshard_ffn_up" + tag)
        out["down"] = _place_own(got["down"], own["down"], dev_idx, name="place_ffn_down" + tag)
        out["conv"] = _side_by_side(got["conv"], own["conv"], dev)
    return out


def _rope_tables(S):
    rows = S // GRID_W
    pairs = ATTN_HD // 4
    row_idx = jnp.repeat(jnp.arange(rows, dtype=F32), GRID_W)
    col_idx = jnp.tile(jnp.arange(GRID_W, dtype=F32), rows)
    inv_freq = ROPE_THETA ** (-jnp.arange(pairs, dtype=F32) / pairs)
    ang = jnp.concatenate([row_idx[:, None] * inv_freq, col_idx[:, None] * inv_freq], axis=-1)
    cos, sin = jnp.cos(ang), jnp.sin(ang)
    return jnp.concatenate([cos, cos], axis=-1), jnp.concatenate([-sin, sin], axis=-1)


def _gate_matrix(w_f, w_b):
    rk = w_f.shape[0]
    top = jnp.concatenate([w_f, jnp.zeros_like(w_f)], axis=1)
    mid = jnp.concatenate([jnp.zeros_like(w_b), w_b], axis=1)
    pad = jnp.zeros((LANES - 2 * rk, 2 * GLA_KD), w_f.dtype)
    return jnp.concatenate([top, mid, pad], axis=0)


def _local_step(x, target, rep, w, dev, idx):
    S, D = x.shape
    depth = rep["norm_mix"].shape[0]
    cs, sn = _rope_tables(S)
    row = lambda a: a.reshape(1, -1)
    ranks_cols = (GLA_PROJ - LANES, LANES)

    groups = [(0, True, False), (0, False, True)] + [(i, True, True) for i in range(1, depth)]
    sent = []
    for g, (i, mixer, ffn) in enumerate(groups):
        names, srcs = _layer_shards(w, i, mixer, ffn)
        lands = [lax.empty((N_DEV,) + a.shape, a.dtype) for a in srcs]
        dep = (sent[-1][1]["token"],) if sent else ()
        sent.append((names, _split_start(_gather_send_plan(len(srcs)), srcs, lands, dep=dep,
                                         name="weights_send_g%d" % g)))

    def arrive(g, after):
        names, started = sent[g]
        own, lands = _split_wait(started, after, name="weights_arrive_g%d" % g)
        return names, own, _split_start(_gather_pass_plan(len(lands)), [], lands, name="weights_pass_g%d" % g)

    def ready(g, passing, after):
        names, own, started = passing
        _, lands = _split_wait(started, after, name="weights_passed_g%d" % g)
        return _layer_weights(names, own, lands, dev, idx["dev"], groups[g][0])

    passing = arrive(0, sent[-1][1]["token"])
    wl = ready(0, passing, passing[2]["token"])
    passing = arrive(1, wl["mix_in"])

    saved = []
    for i in range(depth):
        j = i // 2
        sv = {"x0": x, "w": wl}
        h1 = _rmsnorm_fwd(x, row(rep["norm_mix"][i]), dep=(passing[2]["token"],) if i == 0 else (), name="norm_mix_fwd")
        sv["h1"] = h1
        if i % 2 == 0:
            bias = jnp.concatenate([rep["gla_b_gate_f"][j], rep["gla_b_gate_b"][j]]).reshape(1, -1)
            proj = _mm(h1, wl["mix_in"], name="gla_in_proj")
            logits = _mm(proj, wl["gate"], a_cols=ranks_cols, name="gla_gate_logits")
            of, ob, sf, sb = _gla_fwd(proj, logits, bias, name="gla_fwd")
            y = _gla_gate_fwd(of, ob, proj, row(rep["gla_norm"][j]), name="gla_gate_fwd")
            x = _mm(y, wl["mix_out"], res=x, name="gla_out_proj")
            sv.update(bias=bias, proj=proj, logits=logits, of=of, ob=ob, sf=sf, sb=sb, y=y)
        else:
            proj = _mm(h1, wl["mix_in"], name="attn_qkv_proj")
            qkv = _attn_prep(proj, cs, sn, row(rep["attn_q_norm"][j]), row(rep["attn_k_norm"][j]), name="attn_prep")
            o = _attn_fwd(qkv, name="attn_fwd")
            x = _mm(o, wl["mix_out"], res=x, name="attn_out_proj")
            sv.update(proj=proj, qkv=qkv, o=o)
        sv["x1"] = x
        if i == 0:
            wl.update(ready(1, passing, x))
        dep = ()
        if i + 1 < depth:
            passing = arrive(i + 2, x)
            dep = (passing[2]["token"],)
        h2 = _rmsnorm_fwd(x, row(rep["norm_ffn"][i]), dep=dep, name="norm_ffn_fwd")
        F = wl["down"].shape[0]
        wc, bc = wl["conv"], rep["ffn_b_conv"][i]
        wcv, wcg, bcv, bcg = wc[:, :F], wc[:, F:], row(bc[:F]), row(bc[F:])
        uv = _mm(h2, wl["up_val"], name="ffn_up_val")
        ug = _mm(h2, wl["up_gate"], name="ffn_up_gate")
        act = _conv_act_fwd(uv, ug, wcv, wcg, bcv, bcg, name="ffn_conv_act")
        x = _mm(act, wl["down"], res=x, name="ffn_down")
        sv.update(h2=h2, uv=uv, ug=ug, act=act, wcv=wcv, wcg=wcg, bcv=bcv, bcg=bcg)
        saved.append(sv)
        if i + 1 < depth:
            wl = ready(i + 2, passing, x)

    dx, loss_tile = _loss_head(x, target, name="loss_head")
    loss = loss_tile[0, 0]

    in_sibling_stage, in_chip_stage, reduced = [], [], []

    def advance(group, after):
        tokens = []
        for tag, keys, started in in_chip_stage:
            partial, lands = _split_wait(started, after, name="grads_chips_arrive_" + tag)
            reduced.append((keys, partial, lands))
        in_chip_stage.clear()
        for tag, keys, started in in_sibling_stage:
            stacks, lands = _split_wait(started, after, name="grads_sibling_arrive_" + tag)
            partial = [_add_sibling(s, b, idx["core"], name="grads_add_sibling_%s_%d" % (tag, q))
                       for q, (s, b) in enumerate(zip(stacks, lands))]
            bufs = [lax.empty(p.shape, p.dtype) for p in partial]
            started = _split_start(_reduce_chip_plan(len(partial)), partial, bufs, name="grads_chips_send_" + tag)
            in_chip_stage.append((tag, keys, started))
            tokens.append(started["token"])
        in_sibling_stage.clear()
        if group is not None:
            tag, keys, stacks = group
            bufs = [lax.empty((4,) + s.shape[1:], s.dtype) for s in stacks]
            started = _split_start(_reduce_sibling_plan(len(stacks)), stacks, bufs, name="grads_sibling_send_" + tag)
            in_sibling_stage.append((tag, keys, started))
            tokens.append(started["token"])
        return tuple(tokens)

    def stack_for(name):
        return lax.empty((N_DEV,) + tuple(w[name].shape[1:]), BF16)

    gl = {k: [None] * depth for k in ("norm_mix", "norm_ffn", "ffn_w_conv", "ffn_b_conv")}
    gm = {k: [None] * (depth // 2) for k in ("gla_w_gate_up_f", "gla_b_gate_f", "gla_w_gate_up_b", "gla_b_gate_b",
                                             "gla_norm", "attn_q_norm", "attn_k_norm")}
    rk = GLA_GATE_RANK
    dep = ()
    for i in reversed(range(depth)):
        j = i // 2
        sv = saved[i]
        wl = sv["w"]
        dact = _mm(dx, wl["down"], tb=True, dep=dep, name="ffn_down_dgrad")
        g_down = _wgrad(sv["act"], dx, stack_for("ffn_w_down"), shard="rows", name="ffn_down_wgrad")
        dcv, dcg, dwv, dwg, dbv, dbg = _conv_act_bwd(sv["uv"], sv["ug"], sv["wcv"], sv["wcg"], sv["bcv"], sv["bcg"],
                                                     dact, name="ffn_conv_act_bwd")
        gl["ffn_w_conv"][i] = jnp.concatenate([dwv, dwg], axis=1)
        gl["ffn_b_conv"][i] = jnp.concatenate([dbv, dbg], axis=1)[0]
        mid = advance(None, dcv)
        duv = _conv_t(dcv, sv["wcv"], name="ffn_conv_t")
        dug = _conv_t(dcg, sv["wcg"], name="ffn_conv_t")
        dh2 = _mm(duv, wl["up_val"], tb=True, dep=mid, name="ffn_up_dgrad_val")
        dh2 = _mm(dug, wl["up_gate"], tb=True, res=dh2, name="ffn_up_dgrad_gate")
        g_up = _wgrad(sv["h2"], duv, stack_for("ffn_w_up"), shard="cols", group=0, name="ffn_up_wgrad_val")
        g_up = _wgrad(sv["h2"], dug, g_up, shard="cols", group=1, name="ffn_up_wgrad_gate")
        dx, dn = _rmsnorm_bwd(sv["x1"], row(rep["norm_ffn"][i]), dh2, dx, name="norm_ffn_bwd")
        gl["norm_ffn"][i] = dn[0]
        dep = advance(("ffn_l%d" % i, [("ffn_w_up", i), ("ffn_w_down", i)], [g_up, g_down]), dx)
        if i % 2 == 0:
            dy = _mm(dx, wl["mix_out"], tb=True, dep=dep, name="gla_out_dgrad")
            g_out = _wgrad(sv["y"], dx, stack_for("gla_w_out"), shard="rows", name="gla_out_wgrad")
            do, dg, dgn = _gla_gate_bwd(sv["of"], sv["ob"], sv["proj"], row(rep["gla_norm"][j]), dy, name="gla_gate_bwd")
            gm["gla_norm"][j] = dgn[0]
            dqkv_f, dlg_f, dqkv_b, dlg_b, dbias = _gla_bwd(sv["proj"], sv["logits"], sv["bias"], sv["sf"], sv["sb"], do,
                                                           name="gla_bwd")
            gm["gla_b_gate_f"][j] = dbias[0, :GLA_KD]
            gm["gla_b_gate_b"][j] = dbias[0, GLA_KD:]
            mid = advance(None, dqkv_f)
            dlogits = jnp.concatenate([dlg_f, dlg_b], axis=1)
            dr = _mm(dlogits, wl["gate"], tb=True, dep=mid, name="gla_gate_dgrad")
            dwg_full = _mm(sv["proj"], dlogits, ta=True, a_cols=ranks_cols, name="gla_gate_wgrad")
            gm["gla_w_gate_up_f"][j] = dwg_full[:rk, :GLA_KD]
            gm["gla_w_gate_up_b"][j] = dwg_full[rk:2 * rk, GLA_KD:]
            dproj = _gla_combine(dqkv_f, dqkv_b, dg, dr, name="gla_combine")
            dh1 = _mm(dproj, wl["mix_in"], tb=True, name="gla_in_dgrad")
            g_in = _wgrad(sv["h1"], dproj, stack_for("gla_w_in"), shard="cols", name="gla_in_wgrad")
            keys = [("gla_w_in", j), ("gla_w_out", j)]
        else:
            do = _mm(dx, wl["mix_out"], tb=True, out_dtype=BF16, dep=dep, name="attn_out_dgrad")
            g_out = _wgrad(sv["o"], dx, stack_for("attn_w_out"), shard="rows", name="attn_out_wgrad")
            dq, dk, dv = _attn_bwd(sv["qkv"], do, name="attn_bwd")
            mid = advance(None, dq)
            dproj, dqn, dkn = _attn_prep_bwd(sv["proj"], dq, dk, dv, cs, sn, row(rep["attn_q_norm"][j]),
                                             row(rep["attn_k_norm"][j]), name="attn_prep_bwd")
            gm["attn_q_norm"][j] = dqn[0]
            gm["attn_k_norm"][j] = dkn[0]
            dh1 = _mm(dproj, wl["mix_in"], tb=True, dep=mid, name="attn_qkv_dgrad")
            g_in = _wgrad(sv["h1"], dproj, stack_for("attn_w_qkv"), shard="cols", name="attn_qkv_wgrad")
            keys = [("attn_w_qkv", j), ("attn_w_out", j)]
        dx, dn = _rmsnorm_bwd(sv["x0"], row(rep["norm_mix"][i]), dh1, dx, name="norm_mix_bwd")
        gl["norm_mix"][i] = dn[0]
        dep = advance(("mix_l%d" % i, keys, [g_in, g_out]), dx)

    small = {k: jnp.stack(v) for k, v in {**gl, **gm}.items()}
    return loss, dx, reduced, small, advance


def kernel(x, norm_mix, norm_ffn, gla_w_in, gla_w_gate_up_f, gla_b_gate_f, gla_w_gate_up_b, gla_b_gate_b, gla_norm, gla_w_out, attn_w_qkv, attn_q_norm, attn_k_norm, attn_w_out, ffn_w_up, ffn_w_conv, ffn_b_conv, ffn_w_down, loss_target, m_norm_mix, m_norm_ffn, m_gla_w_in, m_gla_w_gate_up_f, m_gla_b_gate_f, m_gla_w_gate_up_b, m_gla_b_gate_b, m_gla_norm, m_gla_w_out, m_attn_w_qkv, m_attn_q_norm, m_attn_k_norm, m_attn_w_out, m_ffn_w_up, m_ffn_w_conv, m_ffn_b_conv, m_ffn_w_down, v_norm_mix, v_norm_ffn, v_gla_w_in, v_gla_w_gate_up_f, v_gla_b_gate_f, v_gla_w_gate_up_b, v_gla_b_gate_b, v_gla_norm, v_gla_w_out, v_attn_w_qkv, v_attn_q_norm, v_attn_k_norm, v_attn_w_out, v_ffn_w_up, v_ffn_w_conv, v_ffn_b_conv, v_ffn_w_down):
    given = dict(locals())
    w = {n: given[n] for n in WEIGHTS}
    m = {n: given["m_" + n] for n in WEIGHTS}
    v = {n: given["v_" + n] for n in WEIGHTS}
    shards = {n: w[n] for n in BIG + SMALL_SHARDED}
    rep = {n: w[n] for n in REPLICATED}

    x_pos, y_pos, c_pos = _place()
    dev = 4 * x_pos + 2 * y_pos + c_pos
    as_operand = lambda s: jnp.asarray(s, jnp.int32).reshape(1)
    idx = dict(dev=as_operand(dev), chip=as_operand(2 * x_pos + y_pos), core=as_operand(c_pos))

    loss_local, grad_x, reduced, small, advance = _local_step(x[0], loss_target[0], rep, shards, dev, idx)
    loss = lax.psum(loss_local, ("x", "y", "c"))

    big = {n: [lax.empty(w[n].shape, F32) for _ in range(4)] for n in BIG}

    def update_reduced():
        for keys, own, parts in reduced:
            for (n, layer), p_own, p_others in zip(keys, own, parts):
                big[n] = _adamw_layer(w[n], m[n], v[n], p_own, p_others, idx["chip"], big[n], layer,
                                      name="adamw_%s_l%d" % (n, layer))
        reduced.clear()

    advance(None, grad_x)
    update_reduced()

    rest = REPLICATED + SMALL_SHARDED
    flat = _rows(jnp.concatenate([small[n].reshape(-1) for n in rest]), 512)
    total = _sum_slots(_all_gather(flat, name="small_grads_all_gather"), name="small_grads_sum").reshape(-1)
    advance(None, [total, *big["ffn_w_up"]])
    update_reduced()
    g, off = {}, 0
    for n in rest:
        whole = total[off:off + small[n].size].reshape(small[n].shape)
        off += small[n].size
        width = w[n].shape[-1]
        g[n] = whole if n in REPLICATED else lax.dynamic_slice_in_dim(whole, dev * width, width, axis=whole.ndim - 1)

    delta, new_m, new_v = {}, {}, {}
    for n in WEIGHTS:
        if n in BIG:
            g[n], delta[n], new_m[n], new_v[n] = big[n]
        else:
            shape = w[n].shape
            two_d = (-1, shape[-1])
            d2, m2, v2 = _adamw(w[n].reshape(two_d), g[n].reshape(two_d), m[n].reshape(two_d), v[n].reshape(two_d),
                                name="adamw_" + n)
            delta[n], new_m[n], new_v[n] = d2.reshape(shape), m2.reshape(shape), v2.reshape(shape)

    return (loss, grad_x[None], *[g[n] for n in WEIGHTS], *[delta[n] for n in WEIGHTS],
            *[new_m[n] for n in WEIGHTS], *[new_v[n] for n in WEIGHTS])
```

```python
import math

import jax
import jax.numpy as jnp
from jax import lax
from jax.experimental import pallas as pl
from jax.experimental.pallas import tpu as pltpu

F32 = jnp.float32
BF16 = jnp.bfloat16
MESH = pl.DeviceIdType.MESH

N_DEV = 8
LANES = 128
SUBLANES = 8
VMEM_LIMIT = 56 * 1024 * 1024

NORM_EPS = 1e-6
GRID_W = 64
ROPE_THETA = 10000.0
GLA_HEADS = 4
GLA_DK = 128
GLA_DV = 256
GLA_CHUNK = 64
GLA_GATE_RANK = 16
GLA_GATE_NORMALIZER = 16.0
ATTN_HD = 128
ATTN_Q_HEADS = 8
ATTN_KV_HEADS = 2
ATTN_GROUP = ATTN_Q_HEADS // ATTN_KV_HEADS

ADAM_LR = 0.001
ADAM_B1 = 0.9
ADAM_B2 = 0.999
ADAM_EPS = 1e-08
ADAM_WD = 0.01
ADAM_STEP = 10


def _tile(n, target, align=LANES):
    if n <= target:
        return n
    t = (target // align) * align
    while t >= align:
        if n % t == 0:
            return t
        t -= align
    return n


def _params(*sem):
    return pltpu.CompilerParams(dimension_semantics=sem, vmem_limit_bytes=VMEM_LIMIT)


def _dot(a, b):
    return lax.dot_general(a, b, (((1,), (0,)), ((), ())), preferred_element_type=F32)


def _dot_nt(a, b):
    return lax.dot_general(a, b, (((1,), (1,)), ((), ())), preferred_element_type=F32)


def _dot_tn(a, b):
    return lax.dot_general(a, b, (((0,), (0,)), ((), ())), preferred_element_type=F32)


def _sigmoid(x):
    return 0.5 * jnp.tanh(0.5 * x) + 0.5


def _log_sigmoid(x):
    return jnp.minimum(x, 0.0) - jnp.log(1.0 + jnp.exp(-jnp.abs(x)))


def _colsum(x):
    return jnp.sum(x, axis=0, keepdims=True)


ANY = pl.BlockSpec(memory_space=pl.ANY)


def _mm(a, b, *, ta=False, tb=False, res=None, out_dtype=F32, layer=None, a_cols=None, dep=(), name):
    if tb:
        N, K = b.shape[-2:]
    else:
        K, N = b.shape[-2:]
    a_rows, a_width = a.shape
    a_off = 0
    if a_cols is not None:
        a_off, a_width = a_cols
    if ta:
        M = a_width
        assert a_rows == K, (a.shape, b.shape, ta, tb)
    else:
        M = a_rows
        assert a_width == K, (a.shape, b.shape, ta, tb)
    tm = _tile(M, 1408) if ta else _tile(M, 1024, 16)
    tn = _tile(N, 1408)
    tk = _tile(K, 512, 16) if ta else _tile(K, 1408)
    nk = K // tk
    dims = (((0 if ta else 1,), (1 if tb else 0,)), ((), ()))

    n_in = 2 + (res is not None) + len(dep)

    def body(*refs):
        a_ref, b_ref = refs[:2]
        r_ref = refs[2] if res is not None else None
        o_ref = refs[n_in]
        scr = refs[n_in + 1:]
        part = lax.dot_general(a_ref[...].astype(BF16), b_ref[...].astype(BF16), dims, preferred_element_type=F32)

        def finish(acc):
            if r_ref is not None:
                acc = acc + r_ref[...]
            o_ref[...] = acc.astype(out_dtype)

        if nk == 1:
            finish(part)
        else:
            acc_ref = scr[0]
            k = pl.program_id(2)

            @pl.when(k == 0)
            def _():
                acc_ref[...] = part

            @pl.when(k > 0)
            def _():
                acc_ref[...] += part

            @pl.when(k == nk - 1)
            def _():
                finish(acc_ref[...])

    a_blk = a_off // (tm if ta else tk)
    assert a_off % (tm if ta else tk) == 0
    a_spec = (pl.BlockSpec((tk, tm), lambda i, j, k: (k, a_blk + i)) if ta
              else pl.BlockSpec((tm, tk), lambda i, j, k: (i, a_blk + k)))
    if layer is None:
        b_spec = pl.BlockSpec((tn, tk), lambda i, j, k: (j, k)) if tb else pl.BlockSpec((tk, tn), lambda i, j, k: (k, j))
    else:
        b_spec = (pl.BlockSpec((None, tn, tk), lambda i, j, k: (layer, j, k)) if tb
                  else pl.BlockSpec((None, tk, tn), lambda i, j, k: (layer, k, j)))
    o_spec = pl.BlockSpec((tm, tn), lambda i, j, k: (i, j))
    in_specs = [a_spec, b_spec] + ([o_spec] if res is not None else []) + [ANY] * len(dep)
    args = (a, b) + ((res,) if res is not None else ()) + tuple(dep)
    return pl.pallas_call(
        body, name=name, grid=(M // tm, N // tn, nk),
        in_specs=in_specs, out_specs=o_spec,
        out_shape=jax.ShapeDtypeStruct((M, N), out_dtype),
        scratch_shapes=[pltpu.VMEM((tm, tn), F32)] if nk > 1 else [],
        compiler_params=_params("parallel", "parallel", "arbitrary"),
    )(*args)


def _wgrad(a, b, stack, *, shard, group=0, name):
    S, N = b.shape
    M = a.shape[1]
    As, Bs = stack.shape[-2:]
    tk = _tile(S, 1024, 16)
    nk = S // tk
    if shard == "cols":
        n = N // Bs
        tm = _tile(M, 512)
        tn = N
        grid = (M // tm, 1, nk)
        o_spec = pl.BlockSpec((n, tm, Bs), lambda i, j, k: (group, i, 0))
    else:
        per = As * LANES // math.gcd(As, LANES)
        tm = M if M <= 1408 else _tile(M, 1408, per)
        n = tm // As
        tn = _tile(N, 1024)
        grid = (M // tm, N // tn, nk)
        o_spec = pl.BlockSpec((n, As, tn), lambda i, j, k: (i, 0, j))

    def body(a_ref, b_ref, _, o_ref, acc_ref):
        k = pl.program_id(2)
        part = _dot_tn(a_ref[...].astype(BF16), b_ref[...].astype(BF16))

        @pl.when(k == 0)
        def _():
            acc_ref[...] = part

        @pl.when(k > 0)
        def _():
            acc_ref[...] += part

        @pl.when(k == nk - 1)
        def _():
            for q in range(n):
                if shard == "cols":
                    o_ref[q] = acc_ref[:, q * Bs:(q + 1) * Bs].astype(stack.dtype)
                else:
                    o_ref[q] = acc_ref[q * As:(q + 1) * As, :].astype(stack.dtype)

    return pl.pallas_call(
        body, name=name, grid=grid,
        in_specs=[pl.BlockSpec((tk, tm), lambda i, j, k: (k, i)), pl.BlockSpec((tk, tn), lambda i, j, k: (k, j)),
                  pl.BlockSpec(memory_space=pl.ANY)],
        out_specs=o_spec,
        out_shape=jax.ShapeDtypeStruct(stack.shape, stack.dtype),
        input_output_aliases={2: 0},
        scratch_shapes=[pltpu.VMEM((tm, tn), F32)],
        compiler_params=_params("parallel", "parallel", "arbitrary"),
    )(a, b, stack)


def _rmsnorm_fwd(x, w, *, dep=(), name):
    S, D = x.shape
    ts = _tile(S, 1024, 16)

    def body(x_ref, w_ref, *rest):
        o_ref = rest[-1]
        xv = x_ref[...]
        r = lax.rsqrt(jnp.mean(xv * xv, axis=-1, keepdims=True) + NORM_EPS)
        o_ref[...] = (xv * r * w_ref[...]).astype(BF16)

    return pl.pallas_call(
        body, name=name, grid=(S // ts,),
        in_specs=[pl.BlockSpec((ts, D), lambda i: (i, 0)), pl.BlockSpec((1, D), lambda i: (0, 0))] + [ANY] * len(dep),
        out_specs=pl.BlockSpec((ts, D), lambda i: (i, 0)),
        out_shape=jax.ShapeDtypeStruct((S, D), BF16),
        compiler_params=_params("parallel"),
    )(x, w, *dep)


def _rmsnorm_bwd(x, w, dh, dres, *, name):
    S, D = x.shape
    ts = _tile(S, 1024, 16)
    n = S // ts

    def body(x_ref, w_ref, dh_ref, dr_ref, dx_ref, dw_ref):
        i = pl.program_id(0)
        xv = x_ref[...]
        r = lax.rsqrt(jnp.mean(xv * xv, axis=-1, keepdims=True) + NORM_EPS)
        xh = xv * r
        d = dh_ref[...]
        g = d * w_ref[...]
        dx_ref[...] = dr_ref[...] + r * (g - xh * jnp.mean(g * xh, axis=-1, keepdims=True))
        part = _colsum(d * xh)

        @pl.when(i == 0)
        def _():
            dw_ref[...] = part

        @pl.when(i > 0)
        def _():
            dw_ref[...] += part

    row = pl.BlockSpec((ts, D), lambda i: (i, 0))
    vec = pl.BlockSpec((1, D), lambda i: (0, 0))
    return pl.pallas_call(
        body, name=name, grid=(n,),
        in_specs=[row, vec, row, row], out_specs=[row, vec],
        out_shape=[jax.ShapeDtypeStruct((S, D), F32), jax.ShapeDtypeStruct((1, D), F32)],
        compiler_params=_params("arbitrary"),
    )(x, w, dh, dres)


def _halo_specs(S, ts, tf, row_axis, rows=SUBLANES):
    g = ts // rows
    last = S // rows - 1
    col_axis = 1 - row_axis
    main = pl.BlockSpec((ts, tf), lambda *ij: (ij[row_axis], ij[col_axis]))
    prev = pl.BlockSpec((rows, tf), lambda *ij: (jnp.maximum(ij[row_axis] * g - 1, 0), ij[col_axis]))
    nxt = pl.BlockSpec((rows, tf), lambda *ij: (jnp.minimum((ij[row_axis] + 1) * g, last), ij[col_axis]))
    return [main, prev, nxt]


def _shifted(u, prev_ref, next_ref, i, n):
    ts = u.shape[0]
    rows = prev_ref.shape[0]
    rid = lax.broadcasted_iota(jnp.int32, u.shape, 0)
    before = jnp.where(i > 0, prev_ref[rows - 1:rows, :].astype(F32), 0.0)
    after = jnp.where(i < n - 1, next_ref[0:1, :].astype(F32), 0.0)
    um1 = jnp.where(rid == 0, before, pltpu.roll(u, 1, 0))
    up1 = jnp.where(rid == ts - 1, after, pltpu.roll(u, ts - 1, 0))
    return um1, up1


def _conv3(u, prev_ref, next_ref, w_ref, i, n):
    um1, up1 = _shifted(u, prev_ref, next_ref, i, n)
    return w_ref[0:1, :] * um1 + w_ref[1:2, :] * u + w_ref[2:3, :] * up1


def _conv_act_fwd(uv, ug, wv, wg, bv, bg, *, name):
    S, F = uv.shape
    ts = _tile(S, 512, 16)
    tf = _tile(F, 1408)
    n = S // ts

    def body(v_ref, vp_ref, vn_ref, g_ref, gp_ref, gn_ref, wv_ref, wg_ref, bv_ref, bg_ref, o_ref):
        i = pl.program_id(0)
        val = _conv3(v_ref[...], vp_ref, vn_ref, wv_ref, i, n) + bv_ref[...]
        gate = _conv3(g_ref[...], gp_ref, gn_ref, wg_ref, i, n) + bg_ref[...]
        o_ref[...] = (gate * _sigmoid(gate) * val).astype(BF16)

    halo = _halo_specs(S, ts, tf, 0)
    w3 = pl.BlockSpec((3, tf), lambda i, j: (0, j))
    b1 = pl.BlockSpec((1, tf), lambda i, j: (0, j))
    return pl.pallas_call(
        body, name=name, grid=(n, F // tf),
        in_specs=halo + halo + [w3, w3, b1, b1],
        out_specs=pl.BlockSpec((ts, tf), lambda i, j: (i, j)),
        out_shape=jax.ShapeDtypeStruct((S, F), BF16),
        compiler_params=_params("parallel", "parallel"),
    )(uv, uv, uv, ug, ug, ug, wv, wg, bv, bg)


def _conv_act_bwd(uv, ug, wv, wg, bv, bg, dact, *, name):
    S, F = uv.shape
    ts = _tile(S, 512, 16)
    tf = _tile(F, 1408)
    n = S // ts

    def body(v_ref, vp_ref, vn_ref, g_ref, gp_ref, gn_ref, wv_ref, wg_ref, bv_ref, bg_ref, da_ref,
             dv_ref, dg_ref, dwv_ref, dwg_ref, dbv_ref, dbg_ref):
        i = pl.program_id(1)
        uvv, ugv = v_ref[...], g_ref[...]
        vm1, vp1 = _shifted(uvv, vp_ref, vn_ref, i, n)
        gm1, gp1 = _shifted(ugv, gp_ref, gn_ref, i, n)
        val = wv_ref[0:1, :] * vm1 + wv_ref[1:2, :] * uvv + wv_ref[2:3, :] * vp1 + bv_ref[...]
        gate = wg_ref[0:1, :] * gm1 + wg_ref[1:2, :] * ugv + wg_ref[2:3, :] * gp1 + bg_ref[...]
        sg = _sigmoid(gate)
        da = da_ref[...]
        dval = da * (gate * sg)
        dgate = da * val * (sg * (1.0 + gate * (1.0 - sg)))
        dv_ref[...] = dval.astype(BF16)
        dg_ref[...] = dgate.astype(BF16)
        sums = [(dwv_ref, 0, vm1 * dval), (dwv_ref, 1, uvv * dval), (dwv_ref, 2, vp1 * dval),
                (dwg_ref, 0, gm1 * dgate), (dwg_ref, 1, ugv * dgate), (dwg_ref, 2, gp1 * dgate),
                (dbv_ref, 0, dval), (dbg_ref, 0, dgate)]
        parts = [(ref, r, _colsum(t)) for ref, r, t in sums]

        @pl.when(i == 0)
        def _():
            for ref, r, part in parts:
                ref[r:r + 1, :] = part

        @pl.when(i > 0)
        def _():
            for ref, r, part in parts:
                ref[r:r + 1, :] += part

    halo = _halo_specs(S, ts, tf, 1)
    w3 = pl.BlockSpec((3, tf), lambda j, i: (0, j))
    b1 = pl.BlockSpec((1, tf), lambda j, i: (0, j))
    blk = pl.BlockSpec((ts, tf), lambda j, i: (i, j))
    return pl.pallas_call(
        body, name=name, grid=(F // tf, n),
        in_specs=halo + halo + [w3, w3, b1, b1, blk],
        out_specs=[blk, blk, w3, w3, b1, b1],
        out_shape=[jax.ShapeDtypeStruct((S, F), BF16), jax.ShapeDtypeStruct((S, F), BF16),
                   jax.ShapeDtypeStruct((3, F), F32), jax.ShapeDtypeStruct((3, F), F32),
                   jax.ShapeDtypeStruct((1, F), F32), jax.ShapeDtypeStruct((1, F), F32)],
        compiler_params=_params("parallel", "arbitrary"),
    )(uv, uv, uv, ug, ug, ug, wv, wg, bv, bg, dact)


def _conv_t(duc, w, *, name):
    S, F = duc.shape
    ts = _tile(S, 512, 16)
    tf = _tile(F, 1408)
    n = S // ts

    def body(d_ref, dp_ref, dn_ref, w_ref, o_ref):
        i = pl.program_id(0)
        d = d_ref[...].astype(F32)
        dm1, dp1 = _shifted(d, dp_ref, dn_ref, i, n)
        o_ref[...] = (w_ref[0:1, :] * dp1 + w_ref[1:2, :] * d + w_ref[2:3, :] * dm1).astype(BF16)

    return pl.pallas_call(
        body, name=name, grid=(n, F // tf),
        in_specs=_halo_specs(S, ts, tf, 0, rows=16) + [pl.BlockSpec((3, tf), lambda i, j: (0, j))],
        out_specs=pl.BlockSpec((ts, tf), lambda i, j: (i, j)),
        out_shape=jax.ShapeDtypeStruct((S, F), BF16),
        compiler_params=_params("parallel", "parallel"),
    )(duc, duc, duc, w)


N_QK = ATTN_Q_HEADS + ATTN_KV_HEADS
QKV_DIM = (ATTN_Q_HEADS + 2 * ATTN_KV_HEADS) * ATTN_HD


def _head(hd):
    return slice(hd * ATTN_HD, (hd + 1) * ATTN_HD)


def _attn_prep(proj, cs, sn, qn, kn, *, name):
    S = proj.shape[0]
    ts = _tile(S, 512, 16)

    def body(p_ref, c_ref, s_ref, qn_ref, kn_ref, o_ref):
        c, s = c_ref[...], s_ref[...]
        for hd in range(N_QK):
            xv = p_ref[:, _head(hd)]
            w = qn_ref[...] if hd < ATTN_Q_HEADS else kn_ref[...]
            r = lax.rsqrt(jnp.mean(xv * xv, axis=-1, keepdims=True) + NORM_EPS)
            nrm = xv * r * w
            o_ref[:, _head(hd)] = (nrm * c + pltpu.roll(nrm, ATTN_HD // 2, 1) * s).astype(BF16)
        o_ref[:, N_QK * ATTN_HD:] = p_ref[:, N_QK * ATTN_HD:].astype(BF16)

    row = pl.BlockSpec((ts, QKV_DIM), lambda i: (i, 0))
    rot = pl.BlockSpec((ts, ATTN_HD), lambda i: (i, 0))
    vec = pl.BlockSpec((1, ATTN_HD), lambda i: (0, 0))
    return pl.pallas_call(
        body, name=name, grid=(S // ts,),
        in_specs=[row, rot, rot, vec, vec], out_specs=row,
        out_shape=jax.ShapeDtypeStruct((S, QKV_DIM), BF16),
        compiler_params=_params("parallel"),
    )(proj, cs, sn, qn, kn)


def _attn_prep_bwd(proj, dq, dk, dv, cs, sn, qn, kn, *, name):
    S = proj.shape[0]
    ts = _tile(S, 512, 16)
    nq = ATTN_Q_HEADS * ATTN_HD
    nkv = ATTN_KV_HEADS * ATTN_HD

    def body(p_ref, dq_ref, dk_ref, dv_ref, c_ref, s_ref, qn_ref, kn_ref, o_ref, dqn_ref, dkn_ref):
        i = pl.program_id(0)
        c, s = c_ref[...], s_ref[...]
        acc = [jnp.zeros((1, ATTN_HD), F32), jnp.zeros((1, ATTN_HD), F32)]
        for hd in range(N_QK):
            is_k = hd >= ATTN_Q_HEADS
            xv = p_ref[:, _head(hd)]
            w = kn_ref[...] if is_k else qn_ref[...]
            r = lax.rsqrt(jnp.mean(xv * xv, axis=-1, keepdims=True) + NORM_EPS)
            xh = xv * r
            dout = dk_ref[:, _head(hd - ATTN_Q_HEADS)] if is_k else dq_ref[:, _head(hd)]
            dn = dout * c + pltpu.roll(dout * s, ATTN_HD // 2, 1)
            acc[int(is_k)] = acc[int(is_k)] + _colsum(dn * xh)
            g = dn * w
            o_ref[:, _head(hd)] = (r * (g - xh * jnp.mean(g * xh, axis=-1, keepdims=True))).astype(BF16)
        o_ref[:, N_QK * ATTN_HD:] = dv_ref[...].astype(BF16)

        @pl.when(i == 0)
        def _():
            dqn_ref[...] = acc[0]
            dkn_ref[...] = acc[1]

        @pl.when(i > 0)
        def _():
            dqn_ref[...] += acc[0]
            dkn_ref[...] += acc[1]

    row = pl.BlockSpec((ts, QKV_DIM), lambda i: (i, 0))
    rot = pl.BlockSpec((ts, ATTN_HD), lambda i: (i, 0))
    vec = pl.BlockSpec((1, ATTN_HD), lambda i: (0, 0))
    return pl.pallas_call(
        body, name=name, grid=(S // ts,),
        in_specs=[row, pl.BlockSpec((ts, nq), lambda i: (i, 0)), pl.BlockSpec((ts, nkv), lambda i: (i, 0)),
                  pl.BlockSpec((ts, nkv), lambda i: (i, 0)), rot, rot, vec, vec],
        out_specs=[row, vec, vec],
        out_shape=[jax.ShapeDtypeStruct((S, QKV_DIM), BF16), jax.ShapeDtypeStruct((1, ATTN_HD), F32),
                   jax.ShapeDtypeStruct((1, ATTN_HD), F32)],
        compiler_params=_params("arbitrary"),
    )(proj, dq, dk, dv, cs, sn, qn, kn)


ATTN_SCALE = ATTN_HD ** -0.5


def _softmax_of(s):
    e = jnp.exp2((s - jnp.max(s, axis=-1, keepdims=True)) * (ATTN_SCALE * math.log2(math.e)))
    return e, 1.0 / jnp.sum(e, axis=-1, keepdims=True)


def _softmax_parts(q, k):
    return _softmax_of(_dot_nt(q, k))


def _attn_fwd(qkv, *, name):
    S = qkv.shape[0]
    tq = _tile(S, 512, 16)
    sub = _tile(tq, 256, 16)

    def body(q_ref, k_ref, v_ref, o_ref):
        k, v = k_ref[...], v_ref[...]
        for r in range(tq // sub):
            rows = slice(r * sub, (r + 1) * sub)
            e, rl = _softmax_parts(q_ref[rows, :], k)
            o_ref[rows, :] = (_dot(e.astype(BF16), v) * rl).astype(BF16)

    return pl.pallas_call(
        body, name=name, grid=(ATTN_Q_HEADS, S // tq),
        in_specs=[pl.BlockSpec((tq, ATTN_HD), lambda h, i: (i, h)),
                  pl.BlockSpec((S, ATTN_HD), lambda h, i: (0, ATTN_Q_HEADS + h // ATTN_GROUP)),
                  pl.BlockSpec((S, ATTN_HD), lambda h, i: (0, N_QK + h // ATTN_GROUP))],
        out_specs=pl.BlockSpec((tq, ATTN_HD), lambda h, i: (i, h)),
        out_shape=jax.ShapeDtypeStruct((S, ATTN_Q_HEADS * ATTN_HD), BF16),
        compiler_params=_params("parallel", "parallel"),
    )(qkv, qkv, qkv)


def _attn_bwd(qkv, do, *, name):
    S = qkv.shape[0]
    tq = _tile(S, 256, 16)
    n = S // tq

    def body(q_ref, do_ref, qn_ref, don_ref, k_ref, v_ref, dq_ref, dk_ref, dv_ref, s_a, dp_a, s_b, dp_b):
        i = pl.program_id(2)

        @pl.when(i == 0)
        def _():
            s_a[...] = _dot_nt(q_ref[...], k_ref[...])
            dp_a[...] = _dot_nt(do_ref[...], v_ref[...])

            @pl.when(pl.program_id(1) == 0)
            def _():
                dk_ref[...] = jnp.zeros_like(dk_ref)
                dv_ref[...] = jnp.zeros_like(dv_ref)

        def step(s_cur, dp_cur, s_next, dp_next):
            k, v = k_ref[...], v_ref[...]
            q, dov = q_ref[...], do_ref[...]
            s_next[...] = _dot_nt(qn_ref[...], k)
            dp_next[...] = _dot_nt(don_ref[...], v)
            e, rl = _softmax_of(s_cur[...])
            dp = dp_cur[...]
            delta = jnp.sum(e * dp, axis=-1, keepdims=True) * rl
            dsb = (e * (dp - delta) * (rl * ATTN_SCALE)).astype(BF16)
            dq_ref[...] = _dot(dsb, k)
            dk_ref[...] += _dot_tn(dsb, q)
            dv_ref[...] += _dot_tn(e.astype(BF16), (dov.astype(F32) * rl).astype(BF16))

        @pl.when(i % 2 == 0)
        def _():
            step(s_a, dp_a, s_b, dp_b)

        @pl.when(i % 2 == 1)
        def _():
            step(s_b, dp_b, s_a, dp_a)

    qblk = pl.BlockSpec((tq, ATTN_HD), lambda kv, g, i: (i, kv * ATTN_GROUP + g))
    qnext = pl.BlockSpec((tq, ATTN_HD), lambda kv, g, i: (jnp.minimum(i + 1, n - 1), kv * ATTN_GROUP + g))
    kvacc = pl.BlockSpec((S, ATTN_HD), lambda kv, g, i: (0, kv))
    return pl.pallas_call(
        body, name=name, grid=(ATTN_KV_HEADS, ATTN_GROUP, n),
        in_specs=[qblk, qblk, qnext, qnext,
                  pl.BlockSpec((S, ATTN_HD), lambda kv, g, i: (0, ATTN_Q_HEADS + kv)),
                  pl.BlockSpec((S, ATTN_HD), lambda kv, g, i: (0, N_QK + kv))],
        out_specs=[qblk, kvacc, kvacc],
        out_shape=[jax.ShapeDtypeStruct((S, ATTN_Q_HEADS * ATTN_HD), F32),
                   jax.ShapeDtypeStruct((S, ATTN_KV_HEADS * ATTN_HD), F32),
                   jax.ShapeDtypeStruct((S, ATTN_KV_HEADS * ATTN_HD), F32)],
        scratch_shapes=[pltpu.VMEM((tq, S), F32) for _ in range(4)],
        compiler_params=_params("parallel", "arbitrary", "arbitrary"),
    )(qkv, do, qkv, do, qkv, qkv)


GLA_KD = GLA_HEADS * GLA_DK
GLA_VD = GLA_HEADS * GLA_DV
GLA_PROJ = 2 * GLA_KD + 2 * GLA_VD + LANES
GLA_SCALE = GLA_DK ** -0.5


def _split3(x):
    hi = x.astype(BF16)
    r1 = x - hi.astype(F32)
    mid = r1.astype(BF16)
    lo = (r1 - mid.astype(F32)).astype(BF16)
    return hi, mid, lo


def _cumdot(t, x):
    hi, mid, lo = _split3(x)
    return _dot(t, hi) + _dot(t, mid) + _dot(t, lo)


def _gla_masks(d):
    c = GLA_CHUNK
    row = lax.broadcasted_iota(jnp.int32, (c, c), 0)
    col = lax.broadcasted_iota(jnp.int32, (c, c), 1)
    lower, upper = col <= row, col >= row
    if d == 0:
        return lower.astype(BF16), upper.astype(BF16), lower
    return upper.astype(BF16), lower.astype(BF16), col > row


def _gla_decay(lg, bias, cum, d):
    xl = lg + bias
    la = _log_sigmoid(xl) * (1.0 / GLA_GATE_NORMALIZER)
    b = _cumdot(cum, la)
    b_end = b[GLA_CHUNK - 1:GLA_CHUNK, :] if d == 0 else b[0:1, :]
    return xl, b, b_end


def _gla_specs(S, n):
    c = GLA_CHUNK
    up = lambda i: i
    down = lambda i: n - 1 - i
    def specs(order):
        return dict(
            q=pl.BlockSpec((c, GLA_KD), lambda i: (order(i), 0)),
            k=pl.BlockSpec((c, GLA_KD), lambda i: (order(i), 1)),
            v=pl.BlockSpec((c, GLA_VD), lambda i: (order(i), 1)),
            st=pl.BlockSpec((1, GLA_HEADS, GLA_DV, GLA_DK), lambda i: (order(i), 0, 0, 0)),
            wide=pl.BlockSpec((c, GLA_VD), lambda i: (order(i), 0)),
            qkv=pl.BlockSpec((c, 2 * GLA_KD + GLA_VD), lambda i: (order(i), 0)),
        )
    return specs(up), specs(down), up, down


def _gla_fwd(proj, logits, bias, *, name):
    S = proj.shape[0]
    c = GLA_CHUNK
    n = S // c
    su, sd, up, down = _gla_specs(S, n)

    def body(qf, kf, vf, lf, qb, kb, vb, lb, bias_ref, of, ob, sf, sb, st):
        @pl.when(pl.program_id(0) == 0)
        def _():
            st[...] = jnp.zeros_like(st)

        for d, (q_r, k_r, v_r, l_r, o_r, s_r) in enumerate(((qf, kf, vf, lf, of, sf), (qb, kb, vb, lb, ob, sb))):
            cum, _, mask = _gla_masks(d)
            _, b, b_end = _gla_decay(l_r[...], bias_ref[:, d * GLA_KD:(d + 1) * GLA_KD], cum, d)
            dend = jnp.exp(b_end)
            k = k_r[...]
            qd = (q_r[...] * GLA_SCALE * jnp.exp(b)).astype(BF16)
            ki = (k * jnp.exp(-b)).astype(BF16)
            ke = (k * jnp.exp(b_end - b)).astype(BF16)
            for h in range(GLA_HEADS):
                ks = slice(h * GLA_DK, (h + 1) * GLA_DK)
                vs = slice(h * GLA_DV, (h + 1) * GLA_DV)
                stp = st[d * GLA_HEADS + h]
                s_r[0, h] = stp
                v = v_r[:, vs].astype(BF16)
                att = jnp.where(mask, _dot_nt(qd[:, ks], ki[:, ks]), 0.0).astype(BF16)
                o_r[:, vs] = _dot(att, v) + _dot_nt(qd[:, ks], stp.astype(BF16))
                st[d * GLA_HEADS + h] = stp * dend[:, ks] + _dot_tn(v, ke[:, ks])

    lg_f = pl.BlockSpec((c, GLA_KD), lambda i: (up(i), 0))
    lg_b = pl.BlockSpec((c, GLA_KD), lambda i: (down(i), 1))
    return pl.pallas_call(
        body, name=name, grid=(n,),
        in_specs=[su["q"], su["k"], su["v"], lg_f, sd["q"], sd["k"], sd["v"], lg_b,
                  pl.BlockSpec((1, 2 * GLA_KD), lambda i: (0, 0))],
        out_specs=[su["wide"], sd["wide"], su["st"], sd["st"]],
        out_shape=[jax.ShapeDtypeStruct((S, GLA_VD), F32), jax.ShapeDtypeStruct((S, GLA_VD), F32),
                   jax.ShapeDtypeStruct((n, GLA_HEADS, GLA_DV, GLA_DK), F32),
                   jax.ShapeDtypeStruct((n, GLA_HEADS, GLA_DV, GLA_DK), F32)],
        scratch_shapes=[pltpu.VMEM((2 * GLA_HEADS, GLA_DV, GLA_DK), F32)],
        compiler_params=_params("arbitrary"),
    )(proj, proj, proj, logits, proj, proj, proj, logits, bias)


def _gla_bwd(proj, logits, bias, sf, sb, do, *, name):
    S = proj.shape[0]
    c = GLA_CHUNK
    n = S // c
    su, sd, up, down = _gla_specs(S, n)

    def body(qf, kf, vf, lf, stf, dof, qb, kb, vb, lb, stb, dob, bias_ref,
             dqkv_f, dlg_f, dqkv_b, dlg_b, dbias, dst):
        first = pl.program_id(0) == 0

        @pl.when(first)
        def _():
            dst[...] = jnp.zeros_like(dst)

        dbias_parts = []
        for d, (q_r, k_r, v_r, l_r, s_r, do_r, dqkv_r, dlg_r) in enumerate(
                ((qf, kf, vf, lf, stf, dof, dqkv_f, dlg_f), (qb, kb, vb, lb, stb, dob, dqkv_b, dlg_b))):
            cum, cum_t, mask = _gla_masks(d)
            xl, b, b_end = _gla_decay(l_r[...], bias_ref[:, d * GLA_KD:(d + 1) * GLA_KD], cum, d)
            e, ei, ee, dend = jnp.exp(b), jnp.exp(-b), jnp.exp(b_end - b), jnp.exp(b_end)
            k = k_r[...]
            qd32 = q_r[...] * GLA_SCALE * e
            ki32 = k * ei
            ke32 = k * ee
            qd, ki, ke = qd32.astype(BF16), ki32.astype(BF16), ke32.astype(BF16)
            db_parts, dbe_parts = [], []
            for h in range(GLA_HEADS):
                ks = slice(h * GLA_DK, (h + 1) * GLA_DK)
                vs = slice(h * GLA_DV, (h + 1) * GLA_DV)
                stp = s_r[0, h]
                dstn = dst[d * GLA_HEADS + h]
                dstn_b = dstn.astype(BF16)
                v = v_r[:, vs].astype(BF16)
                dov = do_r[:, vs]
                att = jnp.where(mask, _dot_nt(qd[:, ks], ki[:, ks]), 0.0).astype(BF16)
                datt = jnp.where(mask, _dot_nt(dov, v), 0.0).astype(BF16)
                dqkv_r[:, 2 * GLA_KD + h * GLA_DV:2 * GLA_KD + (h + 1) * GLA_DV] = (
                    _dot_tn(att, dov) + _dot_nt(ke[:, ks], dstn_b))
                dqd = _dot(datt, ki[:, ks]) + _dot(dov, stp.astype(BF16))
                dki = _dot_tn(datt, qd[:, ks])
                dke = _dot(v, dstn_b)
                d_dend = _colsum(stp * dstn)
                dst[d * GLA_HEADS + h] = _dot_tn(dov, qd[:, ks]) + dstn * dend[:, ks]
                dqkv_r[:, ks] = dqd * e[:, ks] * GLA_SCALE
                dqkv_r[:, GLA_KD + h * GLA_DK:GLA_KD + (h + 1) * GLA_DK] = dki * ei[:, ks] + dke * ee[:, ks]
                dke_ke = dke * ke32[:, ks]
                db_parts.append(dqd * qd32[:, ks] - dki * ki32[:, ks] - dke_ke)
                dbe_parts.append(_colsum(dke_ke) + d_dend * dend[:, ks])
            db = jnp.concatenate(db_parts, axis=1)
            db_end = jnp.concatenate(dbe_parts, axis=1)
            dla = _cumdot(cum_t, db) + db_end
            dlg = dla * (1.0 / GLA_GATE_NORMALIZER) * _sigmoid(-xl)
            dlg_r[...] = dlg
            dbias_parts.append(_colsum(dlg))
        dbv = jnp.concatenate(dbias_parts, axis=1)

        @pl.when(first)
        def _():
            dbias[...] = dbv

        @pl.when(jnp.logical_not(first))
        def _():
            dbias[...] += dbv

    lg_f = pl.BlockSpec((c, GLA_KD), lambda i: (down(i), 0))
    lg_b = pl.BlockSpec((c, GLA_KD), lambda i: (up(i), 1))
    dlg_f = pl.BlockSpec((c, GLA_KD), lambda i: (down(i), 0))
    dlg_b = pl.BlockSpec((c, GLA_KD), lambda i: (up(i), 0))
    return pl.pallas_call(
        body, name=name, grid=(n,),
        in_specs=[sd["q"], sd["k"], sd["v"], lg_f, sd["st"], sd["wide"],
                  su["q"], su["k"], su["v"], lg_b, su["st"], su["wide"],
                  pl.BlockSpec((1, 2 * GLA_KD), lambda i: (0, 0))],
        out_specs=[sd["qkv"], dlg_f, su["qkv"], dlg_b, pl.BlockSpec((1, 2 * GLA_KD), lambda i: (0, 0))],
        out_shape=[jax.ShapeDtypeStruct((S, 2 * GLA_KD + GLA_VD), F32), jax.ShapeDtypeStruct((S, GLA_KD), F32),
                   jax.ShapeDtypeStruct((S, 2 * GLA_KD + GLA_VD), F32), jax.ShapeDtypeStruct((S, GLA_KD), F32),
                   jax.ShapeDtypeStruct((1, 2 * GLA_KD), F32)],
        scratch_shapes=[pltpu.VMEM((2 * GLA_HEADS, GLA_DV, GLA_DK), F32)],
        compiler_params=_params("arbitrary"),
    )(proj, proj, proj, logits, sf, do, proj, proj, proj, logits, sb, do, bias)


def _gla_gate_fwd(of, ob, proj, w, *, name):
    S = of.shape[0]
    ts = _tile(S, 512, 16)

    def body(of_ref, ob_ref, g_ref, w_ref, y_ref):
        for h in range(GLA_HEADS):
            vs = slice(h * GLA_DV, (h + 1) * GLA_DV)
            o = of_ref[:, vs] + ob_ref[:, vs]
            r = lax.rsqrt(jnp.mean(o * o, axis=-1, keepdims=True) + NORM_EPS)
            g = g_ref[:, vs]
            y_ref[:, vs] = (o * r * w_ref[...] * (g * _sigmoid(g))).astype(BF16)

    wide = pl.BlockSpec((ts, GLA_VD), lambda i: (i, 0))
    return pl.pallas_call(
        body, name=name, grid=(S // ts,),
        in_specs=[wide, wide, pl.BlockSpec((ts, GLA_VD), lambda i: (i, 2)), pl.BlockSpec((1, GLA_DV), lambda i: (0, 0))],
        out_specs=wide,
        out_shape=jax.ShapeDtypeStruct((S, GLA_VD), BF16),
        compiler_params=_params("parallel"),
    )(of, ob, proj, w)


def _gla_gate_bwd(of, ob, proj, w, dy, *, name):
    S = of.shape[0]
    ts = _tile(S, 512, 16)

    def body(of_ref, ob_ref, g_ref, w_ref, dy_ref, do_ref, dg_ref, dw_ref):
        i = pl.program_id(0)
        acc = jnp.zeros((1, GLA_DV), F32)
        for h in range(GLA_HEADS):
            vs = slice(h * GLA_DV, (h + 1) * GLA_DV)
            o = of_ref[:, vs] + ob_ref[:, vs]
            r = lax.rsqrt(jnp.mean(o * o, axis=-1, keepdims=True) + NORM_EPS)
            oh = o * r
            g = g_ref[:, vs]
            sg = _sigmoid(g)
            dyv = dy_ref[:, vs]
            dn = dyv * (g * sg)
            dg_ref[:, vs] = dyv * (oh * w_ref[...]) * (sg * (1.0 + g * (1.0 - sg)))
            acc = acc + _colsum(dn * oh)
            gg = dn * w_ref[...]
            do_ref[:, vs] = (r * (gg - oh * jnp.mean(gg * oh, axis=-1, keepdims=True))).astype(BF16)

        @pl.when(i == 0)
        def _():
            dw_ref[...] = acc

        @pl.when(i > 0)
        def _():
            dw_ref[...] += acc

    wide = pl.BlockSpec((ts, GLA_VD), lambda i: (i, 0))
    vec = pl.BlockSpec((1, GLA_DV), lambda i: (0, 0))
    return pl.pallas_call(
        body, name=name, grid=(S // ts,),
        in_specs=[wide, wide, pl.BlockSpec((ts, GLA_VD), lambda i: (i, 2)), vec, wide],
        out_specs=[wide, wide, vec],
        out_shape=[jax.ShapeDtypeStruct((S, GLA_VD), BF16), jax.ShapeDtypeStruct((S, GLA_VD), F32),
                   jax.ShapeDtypeStruct((1, GLA_DV), F32)],
        compiler_params=_params("arbitrary"),
    )(of, ob, proj, w, dy)


def _gla_combine(dqkv_f, dqkv_b, dg, dr, *, name):
    S = dg.shape[0]
    ts = _tile(S, 512, 16)
    nqkv = 2 * GLA_KD + GLA_VD

    def body(f_ref, b_ref, g_ref, r_ref, o_ref):
        o_ref[:, :nqkv] = (f_ref[...] + b_ref[...]).astype(BF16)
        o_ref[:, nqkv:nqkv + GLA_VD] = g_ref[...].astype(BF16)
        o_ref[:, nqkv + GLA_VD:] = r_ref[...].astype(BF16)

    return pl.pallas_call(
        body, name=name, grid=(S // ts,),
        in_specs=[pl.BlockSpec((ts, nqkv), lambda i: (i, 0)), pl.BlockSpec((ts, nqkv), lambda i: (i, 0)),
                  pl.BlockSpec((ts, GLA_VD), lambda i: (i, 0)), pl.BlockSpec((ts, LANES), lambda i: (i, 0))],
        out_specs=pl.BlockSpec((ts, GLA_PROJ), lambda i: (i, 0)),
        out_shape=jax.ShapeDtypeStruct((S, GLA_PROJ), BF16),
        compiler_params=_params("parallel"),
    )(dqkv_f, dqkv_b, dg, dr)


def _loss_head(y, t, *, name):
    S, D = y.shape
    ts = _tile(S, 512, 16)
    n = S // ts

    def body(y_ref, t_ref, dy_ref, l_ref, acc):
        i = pl.program_id(0)
        diff = y_ref[...] - t_ref[...]
        dy_ref[...] = diff * (1.0 / D)
        part = _colsum(diff * diff)

        @pl.when(i == 0)
        def _():
            acc[...] = part

        @pl.when(i > 0)
        def _():
            acc[...] += part

        @pl.when(i == n - 1)
        def _():
            l_ref[...] = jnp.full(l_ref.shape, 0.5 / D, F32) * jnp.sum(acc[...])

    row = pl.BlockSpec((ts, D), lambda i: (i, 0))
    return pl.pallas_call(
        body, name=name, grid=(n,),
        in_specs=[row, row],
        out_specs=[row, pl.BlockSpec((SUBLANES, LANES), lambda i: (0, 0))],
        out_shape=[jax.ShapeDtypeStruct((S, D), F32), jax.ShapeDtypeStruct((SUBLANES, LANES), F32)],
        scratch_shapes=[pltpu.VMEM((1, D), F32)],
        compiler_params=_params("arbitrary"),
    )(y, t)


def _adamw(w, g, m, v, *, name):
    R, C = w.shape
    tr = _tile(R, 256, SUBLANES)

    def body(w_ref, g_ref, m_ref, v_ref, d_ref, nm_ref, nv_ref):
        gv = g_ref[...]
        nm = ADAM_B1 * m_ref[...] + (1.0 - ADAM_B1) * gv
        nv = ADAM_B2 * v_ref[...] + (1.0 - ADAM_B2) * (gv * gv)
        m_hat = nm / (1.0 - ADAM_B1 ** ADAM_STEP)
        v_hat = nv / (1.0 - ADAM_B2 ** ADAM_STEP)
        d_ref[...] = -ADAM_LR * (m_hat / (jnp.sqrt(v_hat) + ADAM_EPS) + ADAM_WD * w_ref[...])
        nm_ref[...] = nm
        nv_ref[...] = nv

    blk = pl.BlockSpec((tr, C), lambda i: (i, 0))
    shp = jax.ShapeDtypeStruct((R, C), F32)
    return pl.pallas_call(
        body, name=name, grid=(R // tr,),
        in_specs=[blk] * 4, out_specs=[blk] * 3, out_shape=[shp] * 3,
        compiler_params=_params("parallel"),
    )(w, g, m, v)


def _place():
    return lax.axis_index("x"), lax.axis_index("y"), lax.axis_index("c")


def _all_gather(block, *, name):
    R, L = block.shape

    def body(x_ref, out_ref, send_sems, recv_sems, local_sem):
        x, y, c = _place()
        me, sibling = (x, y, c), (x, y, 1 - c)
        chips = [(1 - x, y), (x, 1 - y), (1 - x, 1 - y)]

        def slot(px, py, pc):
            return out_ref.at[4 * px + 2 * py + pc]

        def copy(k, blk, to, src=None):
            return pltpu.make_async_remote_copy(
                src_ref=slot(*blk) if src is None else src, dst_ref=slot(*blk),
                send_sem=send_sems.at[k], recv_sem=recv_sems.at[k], device_id=to, device_id_type=MESH)

        mine = pltpu.make_async_copy(x_ref, slot(*me), local_sem)
        mine.start()
        first = [copy(0, me, sibling, src=x_ref)]
        first += [copy(1 + j, me, (*chip, c), src=x_ref) for j, chip in enumerate(chips)]
        for cp in first:
            cp.start()
        passed = [copy(4 + j, (*chip, c), sibling) for j, chip in enumerate(chips)]
        for j, chip in enumerate(chips):
            copy(1 + j, (*chip, c), me).wait_recv()
            passed[j].start()
        copy(0, sibling, me).wait_recv()
        for j, chip in enumerate(chips):
            copy(4 + j, (*chip, 1 - c), me).wait_recv()
        for cp in first + passed:
            cp.wait_send()
        mine.wait()

    return pl.pallas_call(
        body, name=name, in_specs=[ANY], out_specs=ANY,
        out_shape=jax.ShapeDtypeStruct((N_DEV, R, L), block.dtype),
        scratch_shapes=[pltpu.SemaphoreType.DMA((7,)), pltpu.SemaphoreType.DMA((7,)), pltpu.SemaphoreType.DMA],
    )(block)


HBM_SPEC = pl.BlockSpec(memory_space=pltpu.HBM)
SEM_SPEC = pl.BlockSpec(memory_space=pltpu.SEMAPHORE)
DATAFLOW = pltpu.SideEffectType.DATAFLOW_SIDE_EFFECTING


def _split_start(plan, srcs, lands, *, dep=(), name):
    make_copies, count = plan
    ns, nb = len(srcs), len(srcs) + len(lands)
    bufs = [pltpu.with_memory_space_constraint(a, pltpu.HBM) for a in list(srcs) + list(lands)]
    n_in = nb + len(dep)

    def body(*refs):
        send_sems, recv_sems, token = refs[n_in], refs[n_in + 1], refs[-1]
        for cp in make_copies(refs[:ns], refs[ns:nb], send_sems, recv_sems):
            cp.start()
        token[...] = jnp.zeros_like(token)

    outs = pl.pallas_call(
        body, name=name, in_specs=[HBM_SPEC] * nb + [ANY] * len(dep),
        out_specs=(SEM_SPEC, SEM_SPEC, *[HBM_SPEC] * nb, pl.BlockSpec(memory_space=pltpu.VMEM)),
        out_shape=(pltpu.SemaphoreType.DMA((count,)), pltpu.SemaphoreType.DMA((count,)),
                   *[pltpu.HBM(a.shape, a.dtype) for a in bufs], jax.ShapeDtypeStruct((SUBLANES, LANES), F32)),
        input_output_aliases={i: 2 + i for i in range(nb)},
        compiler_params=pltpu.CompilerParams(has_side_effects=DATAFLOW),
    )(*bufs, *dep)
    return dict(plan=plan, ns=ns, send=outs[0], recv=outs[1], bufs=list(outs[2:2 + nb]), token=outs[-1])


def _split_wait(started, after, *, name):
    make_copies, _ = started["plan"]
    ns, nb = started["ns"], len(started["bufs"])
    after = tuple(after) if isinstance(after, (tuple, list)) else (after,)

    def body(*refs):
        for cp in make_copies(refs[:ns], refs[ns:nb], refs[nb], refs[nb + 1]):
            cp.wait_send()
            cp.wait_recv()

    outs = pl.pallas_call(
        body, name=name, in_specs=[HBM_SPEC] * nb + [SEM_SPEC, SEM_SPEC] + [ANY] * len(after),
        out_specs=[HBM_SPEC] * nb,
        out_shape=[pltpu.HBM(a.shape, a.dtype) for a in started["bufs"]],
        input_output_aliases={i: i for i in range(nb)},
        compiler_params=pltpu.CompilerParams(has_side_effects=DATAFLOW),
    )(*started["bufs"], started["send"], started["recv"], *after)
    return list(outs[:ns]), list(outs[ns:])


def _remote(src, dst, send_sems, recv_sems, k, to):
    return pltpu.make_async_remote_copy(src_ref=src, dst_ref=dst, send_sem=send_sems.at[k], recv_sem=recv_sems.at[k],
                                        device_id=to, device_id_type=MESH)


def _other_chips(x, y):
    return [(1 - x, y), (x, 1 - y), (1 - x, 1 - y)]


def _gather_send_plan(n):
    def make(srcs, lands, send_sems, recv_sems):
        x, y, c = _place()
        targets = [(x, y, 1 - c)] + [(cx, cy, c) for cx, cy in _other_chips(x, y)]
        return [_remote(srcs[t], lands[t].at[4 * x + 2 * y + c], send_sems, recv_sems, 4 * t + k, to)
                for t in range(n) for k, to in enumerate(targets)]
    return make, 4 * n


def _gather_pass_plan(n):
    def make(srcs, lands, send_sems, recv_sems):
        x, y, c = _place()
        cps = []
        for t in range(n):
            for j, (cx, cy) in enumerate(_other_chips(x, y)):
                slot = lands[t].at[4 * cx + 2 * cy + c]
                cps.append(_remote(slot, slot, send_sems, recv_sems, 3 * t + j, (x, y, 1 - c)))
        return cps
    return make, 3 * n


def _reduce_sibling_plan(n):
    def make(srcs, lands, send_sems, recv_sems):
        x, y, c = _place()
        return [_remote(srcs[t].at[2 * k + 1 - c], lands[t].at[k], send_sems, recv_sems, 4 * t + k, (x, y, 1 - c))
                for t in range(n) for k in range(4)]
    return make, 4 * n


def _reduce_chip_plan(n):
    def make(srcs, lands, send_sems, recv_sems):
        x, y, c = _place()
        return [_remote(srcs[t].at[2 * cx + cy], lands[t].at[2 * x + y], send_sems, recv_sems, 3 * t + j, (cx, cy, c))
                for t in range(n) for j, (cx, cy) in enumerate(_other_chips(x, y))]
    return make, 3 * n


def _unshard_cols(g, own, dev_idx, groups, width, *, name):
    _, A, Bs = g.shape
    ta = _tile(A, 512, 16)

    def body(dev_ref, g_ref, own_ref, *o_refs):
        for o_ref, devs in zip(o_refs, groups):
            for q, d in enumerate(devs):
                o_ref[:, q * Bs:(q + 1) * Bs] = jnp.where(dev_ref[0] == d, own_ref[...], g_ref[d])
            if len(devs) * Bs < width:
                o_ref[:, len(devs) * Bs:] = jnp.zeros((ta, width - len(devs) * Bs), g.dtype)

    return pl.pallas_call(
        body, name=name,
        grid_spec=pltpu.PrefetchScalarGridSpec(
            num_scalar_prefetch=1, grid=(A // ta,),
            in_specs=[pl.BlockSpec((N_DEV, ta, Bs), lambda i, d: (0, i, 0)), pl.BlockSpec((ta, Bs), lambda i, d: (i, 0))],
            out_specs=[pl.BlockSpec((ta, width), lambda i, d: (i, 0)) for _ in groups]),
        out_shape=[jax.ShapeDtypeStruct((A, width), g.dtype) for _ in groups],
        compiler_params=_params("parallel"),
    )(dev_idx, g, own)


def _place_own(g, own, dev_idx, *, name):
    _, As, B = g.shape
    ta = _tile(As, 256, 16)

    def body(dev_ref, _, own_ref, o_ref):
        o_ref[...] = own_ref[...]

    out = pl.pallas_call(
        body, name=name,
        grid_spec=pltpu.PrefetchScalarGridSpec(
            num_scalar_prefetch=1, grid=(As // ta,),
            in_specs=[ANY, pl.BlockSpec((ta, B), lambda i, d: (i, 0))],
            out_specs=pl.BlockSpec((None, ta, B), lambda i, d: (d[0], i, 0))),
        out_shape=jax.ShapeDtypeStruct(g.shape, g.dtype),
        input_output_aliases={1: 0},
        compiler_params=_params("parallel"),
    )(dev_idx, g, own)
    return out.reshape(N_DEV * As, B)


def _add_sibling(g, buf, c_idx, *, name):
    _, A, B = g.shape
    ta = _tile(A, 1024, 16)

    def body(c_ref, g_ref, b_ref, o_ref):
        o_ref[...] = (g_ref[...].astype(F32) + b_ref[...].astype(F32)).astype(BF16)

    blk = pl.BlockSpec((None, ta, B), lambda k, i, c_ref: (k, i, 0))
    return pl.pallas_call(
        body, name=name,
        grid_spec=pltpu.PrefetchScalarGridSpec(
            num_scalar_prefetch=1, grid=(4, A // ta),
            in_specs=[pl.BlockSpec((None, ta, B), lambda k, i, c_ref: (2 * k + c_ref[0], i, 0)), blk],
            out_specs=blk),
        out_shape=jax.ShapeDtypeStruct((4, A, B), BF16),
        compiler_params=_params("parallel", "parallel"),
    )(c_idx, g, buf)


def _adamw_layer(w, m, v, own, parts, chip_idx, outs, layer, *, name):
    _, A, B = w.shape
    ta = _tile(A, 512, 16)

    def body(chip_ref, w_ref, m_ref, v_ref, own_ref, p_ref, *rest):
        g_ref, d_ref, nm_ref, nv_ref = rest[4:]
        gv = None
        for j in range(4):
            part = jnp.where(chip_ref[0] == j, own_ref[...], p_ref[j]).astype(F32)
            gv = part if gv is None else gv + part
        nm = ADAM_B1 * m_ref[...] + (1.0 - ADAM_B1) * gv
        nv = ADAM_B2 * v_ref[...] + (1.0 - ADAM_B2) * (gv * gv)
        m_hat = nm / (1.0 - ADAM_B1 ** ADAM_STEP)
        v_hat = nv / (1.0 - ADAM_B2 ** ADAM_STEP)
        g_ref[...] = gv
        d_ref[...] = -ADAM_LR * (m_hat / (jnp.sqrt(v_hat) + ADAM_EPS) + ADAM_WD * w_ref[...])
        nm_ref[...] = nm
        nv_ref[...] = nv

    blk = pl.BlockSpec((None, ta, B), lambda i, ch: (layer, i, 0))
    return pl.pallas_call(
        body, name=name,
        grid_spec=pltpu.PrefetchScalarGridSpec(
            num_scalar_prefetch=1, grid=(A // ta,),
            in_specs=[blk, blk, blk, pl.BlockSpec((None, ta, B), lambda i, ch: (ch[0], i, 0)),
                      pl.BlockSpec((4, ta, B), lambda i, ch: (0, i, 0))] + [ANY] * 4,
            out_specs=[blk] * 4),
        out_shape=[jax.ShapeDtypeStruct(o.shape, o.dtype) for o in outs],
        input_output_aliases={6 + q: q for q in range(4)},
        compiler_params=_params("parallel"),
    )(chip_idx, w, m, v, own, parts, *outs)


def _sum_slots(buf, *, name):
    n, R, L = buf.shape
    tr = _tile(R, 512, SUBLANES)

    def body(b_ref, o_ref):
        acc = b_ref[0]
        for j in range(1, n):
            acc = acc + b_ref[j]
        o_ref[...] = acc

    return pl.pallas_call(
        body, name=name, grid=(R // tr,),
        in_specs=[pl.BlockSpec((n, tr, L), lambda i: (0, i, 0))],
        out_specs=pl.BlockSpec((tr, L), lambda i: (i, 0)),
        out_shape=jax.ShapeDtypeStruct((R, L), buf.dtype),
        compiler_params=_params("parallel"),
    )(buf)


BIG = ("gla_w_in", "gla_w_out", "attn_w_qkv", "attn_w_out", "ffn_w_up", "ffn_w_down")
SMALL_SHARDED = ("gla_w_gate_up_f", "gla_w_gate_up_b", "ffn_w_conv")
REPLICATED = ("norm_mix", "norm_ffn", "gla_b_gate_f", "gla_b_gate_b", "gla_norm", "attn_q_norm", "attn_k_norm",
              "ffn_b_conv")
WEIGHTS = ("norm_mix", "norm_ffn", "gla_w_in", "gla_w_gate_up_f", "gla_b_gate_f", "gla_w_gate_up_b", "gla_b_gate_b",
           "gla_norm", "gla_w_out", "attn_w_qkv", "attn_q_norm", "attn_k_norm", "attn_w_out", "ffn_w_up", "ffn_w_conv",
           "ffn_b_conv", "ffn_w_down")


def _rows(flat, row_align):
    n = flat.shape[0]
    per = row_align * LANES
    padded = -(-n // per) * per
    return jnp.pad(flat, (0, padded - n)).reshape(padded // LANES, LANES)


def _side_by_side(gathered, own, dev):
    n, a, b = gathered.shape
    whole = lax.dynamic_update_index_in_dim(gathered, own, dev, 0)
    return jnp.transpose(whole, (1, 0, 2)).reshape(a, n * b)


def _layer_shards(w, i, mixer, ffn):
    j = i // 2
    parts = []
    if mixer and i % 2 == 0:
        parts += [("mix_in", w["gla_w_in"][j].astype(BF16)), ("mix_out", w["gla_w_out"][j].astype(BF16)),
                  ("gate_f", w["gla_w_gate_up_f"][j].astype(BF16)), ("gate_b", w["gla_w_gate_up_b"][j].astype(BF16))]
    elif mixer:
        parts += [("mix_in", w["attn_w_qkv"][j].astype(BF16)), ("mix_out", w["attn_w_out"][j].astype(BF16))]
    if ffn:
        parts += [("up", w["ffn_w_up"][i].astype(BF16)), ("down", w["ffn_w_down"][i].astype(BF16)),
                  ("conv", w["ffn_w_conv"][i])]
    return [n for n, _ in parts], [a for _, a in parts]


def _layer_weights(names, own, gathered, dev, dev_idx, i):
    own, got = dict(zip(names, own)), dict(zip(names, gathered))
    every = tuple(range(N_DEV))
    half = N_DEV // 2
    tag = "_l%d" % i
    out = {}
    if "mix_in" in got:
        width = GLA_PROJ if i % 2 == 0 else QKV_DIM
        (out["mix_in"],) = _unshard_cols(got["mix_in"], own["mix_in"], dev_idx, [every], width,
                                         name="unshard_mix_in" + tag)
        out["mix_out"] = _place_own(got["mix_out"], own["mix_out"], dev_idx, name="place_mix_out" + tag)
    if "gate_f" in got:
        out["gate"] = _gate_matrix(_side_by_side(got["gate_f"], own["gate_f"], dev),
                                   _side_by_side(got["gate_b"], own["gate_b"], dev))
    if "up" in got:
        f = got["up"].shape[-1] * half
        out["up_val"], out["up_gate"] = _unshard_cols(got["up"], own["up"], dev_idx, [every[:half], every[half:]], f,
                                                      name="unshard_ffn_up" + tag)
        out["down"] = _place_own(got["down"], own["down"], dev_idx, name="place_ffn_down" + tag)
        out["conv"] = _side_by_side(got["conv"], own["conv"], dev)
    return out


def _rope_tables(S):
    rows = S // GRID_W
    pairs = ATTN_HD // 4
    row_idx = jnp.repeat(jnp.arange(rows, dtype=F32), GRID_W)
    col_idx = jnp.tile(jnp.arange(GRID_W, dtype=F32), rows)
    inv_freq = ROPE_THETA ** (-jnp.arange(pairs, dtype=F32) / pairs)
    ang = jnp.concatenate([row_idx[:, None] * inv_freq, col_idx[:, None] * inv_freq], axis=-1)
    cos, sin = jnp.cos(ang), jnp.sin(ang)
    return jnp.concatenate([cos, cos], axis=-1), jnp.concatenate([-sin, sin], axis=-1)


def _gate_matrix(w_f, w_b):
    rk = w_f.shape[0]
    top = jnp.concatenate([w_f, jnp.zeros_like(w_f)], axis=1)
    mid = jnp.concatenate([jnp.zeros_like(w_b), w_b], axis=1)
    pad = jnp.zeros((LANES - 2 * rk, 2 * GLA_KD), w_f.dtype)
    return jnp.concatenate([top, mid, pad], axis=0)


def _local_step(x, target, rep, w, dev, idx):
    S, D = x.shape
    depth = rep["norm_mix"].shape[0]
    cs, sn = _rope_tables(S)
    row = lambda a: a.reshape(1, -1)
    ranks_cols = (GLA_PROJ - LANES, LANES)

    groups = [(0, True, False), (0, False, True)] + [(i, True, True) for i in range(1, depth)]
    sent = []
    for g, (i, mixer, ffn) in enumerate(groups):
        names, srcs = _layer_shards(w, i, mixer, ffn)
        lands = [lax.empty((N_DEV,) + a.shape, a.dtype) for a in srcs]
        dep = (sent[-1][1]["token"],) if sent else ()
        sent.append((names, _split_start(_gather_send_plan(len(srcs)), srcs, lands, dep=dep,
                                         name="weights_send_g%d" % g)))

    def arrive(g, after):
        names, started = sent[g]
        own, lands = _split_wait(started, after, name="weights_arrive_g%d" % g)
        return names, own, _split_start(_gather_pass_plan(len(lands)), [], lands, name="weights_pass_g%d" % g)

    def ready(g, passing, after):
        names, own, started = passing
        _, lands = _split_wait(started, after, name="weights_passed_g%d" % g)
        return _layer_weights(names, own, lands, dev, idx["dev"], groups[g][0])

    passing = arrive(0, sent[-1][1]["token"])
    wl = ready(0, passing, passing[2]["token"])
    passing = arrive(1, wl["mix_in"])

    saved = []
    for i in range(depth):
        j = i // 2
        sv = {"x0": x, "w": wl}
        h1 = _rmsnorm_fwd(x, row(rep["norm_mix"][i]), dep=(passing[2]["token"],) if i == 0 else (), name="norm_mix_fwd")
        sv["h1"] = h1
        if i % 2 == 0:
            bias = jnp.concatenate([rep["gla_b_gate_f"][j], rep["gla_b_gate_b"][j]]).reshape(1, -1)
            proj = _mm(h1, wl["mix_in"], name="gla_in_proj")
            logits = _mm(proj, wl["gate"], a_cols=ranks_cols, name="gla_gate_logits")
            of, ob, sf, sb = _gla_fwd(proj, logits, bias, name="gla_fwd")
            y = _gla_gate_fwd(of, ob, proj, row(rep["gla_norm"][j]), name="gla_gate_fwd")
            x = _mm(y, wl["mix_out"], res=x, name="gla_out_proj")
            sv.update(bias=bias, proj=proj, logits=logits, of=of, ob=ob, sf=sf, sb=sb, y=y)
        else:
            proj = _mm(h1, wl["mix_in"], name="attn_qkv_proj")
            qkv = _attn_prep(proj, cs, sn, row(rep["attn_q_norm"][j]), row(rep["attn_k_norm"][j]), name="attn_prep")
            o = _attn_fwd(qkv, name="attn_fwd")
            x = _mm(o, wl["mix_out"], res=x, name="attn_out_proj")
            sv.update(proj=proj, qkv=qkv, o=o)
        sv["x1"] = x
        if i == 0:
            wl.update(ready(1, passing, x))
        dep = ()
        if i + 1 < depth:
            passing = arrive(i + 2, x)
            dep = (passing[2]["token"],)
        h2 = _rmsnorm_fwd(x, row(rep["norm_ffn"][i]), dep=dep, name="norm_ffn_fwd")
        F = wl["down"].shape[0]
        wc, bc = wl["conv"], rep["ffn_b_conv"][i]
        wcv, wcg, bcv, bcg = wc[:, :F], wc[:, F:], row(bc[:F]), row(bc[F:])
        uv = _mm(h2, wl["up_val"], name="ffn_up_val")
        ug = _mm(h2, wl["up_gate"], name="ffn_up_gate")
        act = _conv_act_fwd(uv, ug, wcv, wcg, bcv, bcg, name="ffn_conv_act")
        x = _mm(act, wl["down"], res=x, name="ffn_down")
        sv.update(h2=h2, uv=uv, ug=ug, act=act, wcv=wcv, wcg=wcg, bcv=bcv, bcg=bcg)
        saved.append(sv)
        if i + 1 < depth:
            wl = ready(i + 2, passing, x)

    dx, loss_tile = _loss_head(x, target, name="loss_head")
    loss = loss_tile[0, 0]

    in_sibling_stage, in_chip_stage, reduced = [], [], []

    def advance(group, after):
        tokens = []
        for tag, keys, started in in_chip_stage:
            partial, lands = _split_wait(started, after, name="grads_chips_arrive_" + tag)
            reduced.append((keys, partial, lands))
        in_chip_stage.clear()
        for tag, keys, started in in_sibling_stage:
            stacks, lands = _split_wait(started, after, name="grads_sibling_arrive_" + tag)
            partial = [_add_sibling(s, b, idx["core"], name="grads_add_sibling_%s_%d" % (tag, q))
                       for q, (s, b) in enumerate(zip(stacks, lands))]
            bufs = [lax.empty(p.shape, p.dtype) for p in partial]
            started = _split_start(_reduce_chip_plan(len(partial)), partial, bufs, name="grads_chips_send_" + tag)
            in_chip_stage.append((tag, keys, started))
            tokens.append(started["token"])
        in_sibling_stage.clear()
        if group is not None:
            tag, keys, stacks = group
            bufs = [lax.empty((4,) + s.shape[1:], s.dtype) for s in stacks]
            started = _split_start(_reduce_sibling_plan(len(stacks)), stacks, bufs, name="grads_sibling_send_" + tag)
            in_sibling_stage.append((tag, keys, started))
            tokens.append(started["token"])
        return tuple(tokens)

    def stack_for(name):
        return lax.empty((N_DEV,) + tuple(w[name].shape[1:]), BF16)

    gl = {k: [None] * depth for k in ("norm_mix", "norm_ffn", "ffn_w_conv", "ffn_b_conv")}
    gm = {k: [None] * (depth // 2) for k in ("gla_w_gate_up_f", "gla_b_gate_f", "gla_w_gate_up_b", "gla_b_gate_b",
                                             "gla_norm", "attn_q_norm", "attn_k_norm")}
    rk = GLA_GATE_RANK
    dep = ()
    for i in reversed(range(depth)):
        j = i // 2
        sv = saved[i]
        wl = sv["w"]
        dact = _mm(dx, wl["down"], tb=True, dep=dep, name="ffn_down_dgrad")
        g_down = _wgrad(sv["act"], dx, stack_for("ffn_w_down"), shard="rows", name="ffn_down_wgrad")
        dcv, dcg, dwv, dwg, dbv, dbg = _conv_act_bwd(sv["uv"], sv["ug"], sv["wcv"], sv["wcg"], sv["bcv"], sv["bcg"],
                                                     dact, name="ffn_conv_act_bwd")
        gl["ffn_w_conv"][i] = jnp.concatenate([dwv, dwg], axis=1)
        gl["ffn_b_conv"][i] = jnp.concatenate([dbv, dbg], axis=1)[0]
        mid = advance(None, dcv)
        duv = _conv_t(dcv, sv["wcv"], name="ffn_conv_t")
        dug = _conv_t(dcg, sv["wcg"], name="ffn_conv_t")
        dh2 = _mm(duv, wl["up_val"], tb=True, dep=mid, name="ffn_up_dgrad_val")
        dh2 = _mm(dug, wl["up_gate"], tb=True, res=dh2, name="ffn_up_dgrad_gate")
        g_up = _wgrad(sv["h2"], duv, stack_for("ffn_w_up"), shard="cols", group=0, name="ffn_up_wgrad_val")
        g_up = _wgrad(sv["h2"], dug, g_up, shard="cols", group=1, name="ffn_up_wgrad_gate")
        dx, dn = _rmsnorm_bwd(sv["x1"], row(rep["norm_ffn"][i]), dh2, dx, name="norm_ffn_bwd")
        gl["norm_ffn"][i] = dn[0]
        dep = advance(("ffn_l%d" % i, [("ffn_w_up", i), ("ffn_w_down", i)], [g_up, g_down]), dx)
        if i % 2 == 0:
            dy = _mm(dx, wl["mix_out"], tb=True, dep=dep, name="gla_out_dgrad")
            g_out = _wgrad(sv["y"], dx, stack_for("gla_w_out"), shard="rows", name="gla_out_wgrad")
            do, dg, dgn = _gla_gate_bwd(sv["of"], sv["ob"], sv["proj"], row(rep["gla_norm"][j]), dy, name="gla_gate_bwd")
            gm["gla_norm"][j] = dgn[0]
            dqkv_f, dlg_f, dqkv_b, dlg_b, dbias = _gla_bwd(sv["proj"], sv["logits"], sv["bias"], sv["sf"], sv["sb"], do,
                                                           name="gla_bwd")
            gm["gla_b_gate_f"][j] = dbias[0, :GLA_KD]
            gm["gla_b_gate_b"][j] = dbias[0, GLA_KD:]
            mid = advance(None, dqkv_f)
            dlogits = jnp.concatenate([dlg_f, dlg_b], axis=1)
            dr = _mm(dlogits, wl["gate"], tb=True, dep=mid, name="gla_gate_dgrad")
            dwg_full = _mm(sv["proj"], dlogits, ta=True, a_cols=ranks_cols, name="gla_gate_wgrad")
            gm["gla_w_gate_up_f"][j] = dwg_full[:rk, :GLA_KD]
            gm["gla_w_gate_up_b"][j] = dwg_full[rk:2 * rk, GLA_KD:]
            dproj = _gla_combine(dqkv_f, dqkv_b, dg, dr, name="gla_combine")
            dh1 = _mm(dproj, wl["mix_in"], tb=True, name="gla_in_dgrad")
            g_in = _wgrad(sv["h1"], dproj, stack_for("gla_w_in"), shard="cols", name="gla_in_wgrad")
            keys = [("gla_w_in", j), ("gla_w_out", j)]
        else:
            do = _mm(dx, wl["mix_out"], tb=True, out_dtype=BF16, dep=dep, name="attn_out_dgrad")
            g_out = _wgrad(sv["o"], dx, stack_for("attn_w_out"), shard="rows", name="attn_out_wgrad")
            dq, dk, dv = _attn_bwd(sv["qkv"], do, name="attn_bwd")
            mid = advance(None, dq)
            dproj, dqn, dkn = _attn_prep_bwd(sv["proj"], dq, dk, dv, cs, sn, row(rep["attn_q_norm"][j]),
                                             row(rep["attn_k_norm"][j]), name="attn_prep_bwd")
            gm["attn_q_norm"][j] = dqn[0]
            gm["attn_k_norm"][j] = dkn[0]
            dh1 = _mm(dproj, wl["mix_in"], tb=True, dep=mid, name="attn_qkv_dgrad")
            g_in = _wgrad(sv["h1"], dproj, stack_for("attn_w_qkv"), shard="cols", name="attn_qkv_wgrad")
            keys = [("attn_w_qkv", j), ("attn_w_out", j)]
        dx, dn = _rmsnorm_bwd(sv["x0"], row(rep["norm_mix"][i]), dh1, dx, name="norm_mix_bwd")
        gl["norm_mix"][i] = dn[0]
        dep = advance(("mix_l%d" % i, keys, [g_in, g_out]), dx)

    small = {k: jnp.stack(v) for k, v in {**gl, **gm}.items()}
    return loss, dx, reduced, small, advance


def kernel(x, norm_mix, norm_ffn, gla_w_in, gla_w_gate_up_f, gla_b_gate_f, gla_w_gate_up_b, gla_b_gate_b, gla_norm, gla_w_out, attn_w_qkv, attn_q_norm, attn_k_norm, attn_w_out, ffn_w_up, ffn_w_conv, ffn_b_conv, ffn_w_down, loss_target, m_norm_mix, m_norm_ffn, m_gla_w_in, m_gla_w_gate_up_f, m_gla_b_gate_f, m_gla_w_gate_up_b, m_gla_b_gate_b, m_gla_norm, m_gla_w_out, m_attn_w_qkv, m_attn_q_norm, m_attn_k_norm, m_attn_w_out, m_ffn_w_up, m_ffn_w_conv, m_ffn_b_conv, m_ffn_w_down, v_norm_mix, v_norm_ffn, v_gla_w_in, v_gla_w_gate_up_f, v_gla_b_gate_f, v_gla_w_gate_up_b, v_gla_b_gate_b, v_gla_norm, v_gla_w_out, v_attn_w_qkv, v_attn_q_norm, v_attn_k_norm, v_attn_w_out, v_ffn_w_up, v_ffn_w_conv, v_ffn_b_conv, v_ffn_w_down):
    given = dict(locals())
    w = {n: given[n] for n in WEIGHTS}
    m = {n: given["m_" + n] for n in WEIGHTS}
    v = {n: given["v_" + n] for n in WEIGHTS}
    shards = {n: w[n] for n in BIG + SMALL_SHARDED}
    rep = {n: w[n] for n in REPLICATED}

    x_pos, y_pos, c_pos = _place()
    dev = 4 * x_pos + 2 * y_pos + c_pos
    as_operand = lambda s: jnp.asarray(s, jnp.int32).reshape(1)
    idx = dict(dev=as_operand(dev), chip=as_operand(2 * x_pos + y_pos), core=as_operand(c_pos))

    loss_local, grad_x, reduced, small, advance = _local_step(x[0], loss_target[0], rep, shards, dev, idx)
    loss = lax.psum(loss_local, ("x", "y", "c"))

    big = {n: [lax.empty(w[n].shape, F32) for _ in range(4)] for n in BIG}

    def update_reduced():
        for keys, own, parts in reduced:
            for (n, layer), p_own, p_others in zip(keys, own, parts):
                big[n] = _adamw_layer(w[n], m[n], v[n], p_own, p_others, idx["chip"], big[n], layer,
                                      name="adamw_%s_l%d" % (n, layer))
        reduced.clear()

    advance(None, grad_x)
    update_reduced()

    rest = REPLICATED + SMALL_SHARDED
    flat = _rows(jnp.concatenate([small[n].reshape(-1) for n in rest]), 512)
    total = _sum_slots(_all_gather(flat, name="small_grads_all_gather"), name="small_grads_sum").reshape(-1)
    advance(None, [total, *big["ffn_w_up"]])
    update_reduced()
    g, off = {}, 0
    for n in rest:
        whole = total[off:off + small[n].size].reshape(small[n].shape)
        off += small[n].size
        width = w[n].shape[-1]
        g[n] = whole if n in REPLICATED else lax.dynamic_slice_in_dim(whole, dev * width, width, axis=whole.ndim - 1)

    delta, new_m, new_v = {}, {}, {}
    for n in WEIGHTS:
        if n in BIG:
            g[n], delta[n], new_m[n], new_v[n] = big[n]
        else:
            shape = w[n].shape
            two_d = (-1, shape[-1])
            d2, m2, v2 = _adamw(w[n].reshape(two_d), g[n].reshape(two_d), m[n].reshape(two_d), v[n].reshape(two_d),
                                name="adamw_" + n)
            delta[n], new_m[n], new_v[n] = d2.reshape(shape), m2.reshape(shape), v2.reshape(shape)

    return (loss, grad_x[None], *[g[n] for n in WEIGHTS], *[delta[n] for n in WEIGHTS],
            *[new_m[n] for n in WEIGHTS], *[new_v[n] for n in WEIGHTS])
```

```python
import math

import jax
import jax.numpy as jnp
from jax import lax
from jax.experimental import pallas as pl
from jax.experimental.pallas import tpu as pltpu

F32 = jnp.float32
BF16 = jnp.bfloat16
MESH = pl.DeviceIdType.MESH

N_DEV = 8
LANES = 128
SUBLANES = 8
VMEM_LIMIT = 56 * 1024 * 1024

NORM_EPS = 1e-6
GRID_W = 64
ROPE_THETA = 10000.0
GLA_HEADS = 4
GLA_DK = 128
GLA_DV = 256
GLA_CHUNK = 64
GLA_GATE_RANK = 16
GLA_GATE_NORMALIZER = 16.0
ATTN_HD = 128
ATTN_Q_HEADS = 8
ATTN_KV_HEADS = 2
ATTN_GROUP = ATTN_Q_HEADS // ATTN_KV_HEADS

ADAM_LR = 0.001
ADAM_B1 = 0.9
ADAM_B2 = 0.999
ADAM_EPS = 1e-08
ADAM_WD = 0.01
ADAM_STEP = 10


def _tile(n, target, align=LANES):
    if n <= target:
        return n
    t = (target // align) * align
    while t >= align:
        if n % t == 0:
            return t
        t -= align
    return n


def _params(*sem):
    return pltpu.CompilerParams(dimension_semantics=sem, vmem_limit_bytes=VMEM_LIMIT)


def _dot(a, b):
    return lax.dot_general(a, b, (((1,), (0,)), ((), ())), preferred_element_type=F32)


def _dot_nt(a, b):
    return lax.dot_general(a, b, (((1,), (1,)), ((), ())), preferred_element_type=F32)


def _dot_tn(a, b):
    return lax.dot_general(a, b, (((0,), (0,)), ((), ())), preferred_element_type=F32)


def _sigmoid(x):
    return 0.5 * jnp.tanh(0.5 * x) + 0.5


def _log_sigmoid(x):
    return jnp.minimum(x, 0.0) - jnp.log(1.0 + jnp.exp(-jnp.abs(x)))


def _colsum(x):
    return jnp.sum(x, axis=0, keepdims=True)


ANY = pl.BlockSpec(memory_space=pl.ANY)


def _mm(a, b, *, ta=False, tb=False, res=None, out_dtype=F32, layer=None, a_cols=None, dep=(), name):
    if tb:
        N, K = b.shape[-2:]
    else:
        K, N = b.shape[-2:]
    a_rows, a_width = a.shape
    a_off = 0
    if a_cols is not None:
        a_off, a_width = a_cols
    if ta:
        M = a_width
        assert a_rows == K, (a.shape, b.shape, ta, tb)
    else:
        M = a_rows
        assert a_width == K, (a.shape, b.shape, ta, tb)
    tm = _tile(M, 1408) if ta else _tile(M, 1024, 16)
    tn = _tile(N, 1408)
    tk = _tile(K, 512, 16) if ta else _tile(K, 1408)
    nk = K // tk
    dims = (((0 if ta else 1,), (1 if tb else 0,)), ((), ()))

    n_in = 2 + (res is not None) + len(dep)

    def body(*refs):
        a_ref, b_ref = refs[:2]
        r_ref = refs[2] if res is not None else None
        o_ref = refs[n_in]
        scr = refs[n_in + 1:]
        part = lax.dot_general(a_ref[...].astype(BF16), b_ref[...].astype(BF16), dims, preferred_element_type=F32)

        def finish(acc):
            if r_ref is not None:
                acc = acc + r_ref[...]
            o_ref[...] = acc.astype(out_dtype)

        if nk == 1:
            finish(part)
        else:
            acc_ref = scr[0]
            k = pl.program_id(2)

            @pl.when(k == 0)
            def _():
                acc_ref[...] = part

            @pl.when(k > 0)
            def _():
                acc_ref[...] += part

            @pl.when(k == nk - 1)
            def _():
                finish(acc_ref[...])

    a_blk = a_off // (tm if ta else tk)
    assert a_off % (tm if ta else tk) == 0
    a_spec = (pl.BlockSpec((tk, tm), lambda i, j, k: (k, a_blk + i)) if ta
              else pl.BlockSpec((tm, tk), lambda i, j, k: (i, a_blk + k)))
    if layer is None:
        b_spec = pl.BlockSpec((tn, tk), lambda i, j, k: (j, k)) if tb else pl.BlockSpec((tk, tn), lambda i, j, k: (k, j))
    else:
        b_spec = (pl.BlockSpec((None, tn, tk), lambda i, j, k: (layer, j, k)) if tb
                  else pl.BlockSpec((None, tk, tn), lambda i, j, k: (layer, k, j)))
    o_spec = pl.BlockSpec((tm, tn), lambda i, j, k: (i, j))
    in_specs = [a_spec, b_spec] + ([o_spec] if res is not None else []) + [ANY] * len(dep)
    args = (a, b) + ((res,) if res is not None else ()) + tuple(dep)
    return pl.pallas_call(
        body, name=name, grid=(M // tm, N // tn, nk),
        in_specs=in_specs, out_specs=o_spec,
        out_shape=jax.ShapeDtypeStruct((M, N), out_dtype),
        scratch_shapes=[pltpu.VMEM((tm, tn), F32)] if nk > 1 else [],
        compiler_params=_params("parallel", "parallel", "arbitrary"),
    )(*args)


def _wgrad(a, b, stack, *, shard, group=0, name):
    S, N = b.shape
    M = a.shape[1]
    As, Bs = stack.shape[-2:]
    tk = _tile(S, 1024, 16)
    nk = S // tk
    if shard == "cols":
        n = N // Bs
        tm = _tile(M, 512)
        tn = N
        grid = (M // tm, 1, nk)
        o_spec = pl.BlockSpec((n, tm, Bs), lambda i, j, k: (group, i, 0))
    else:
        per = As * LANES // math.gcd(As, LANES)
        tm = M if M <= 1408 else _tile(M, 1408, per)
        n = tm // As
        tn = _tile(N, 1024)
        grid = (M // tm, N // tn, nk)
        o_spec = pl.BlockSpec((n, As, tn), lambda i, j, k: (i, 0, j))

    def body(a_ref, b_ref, _, o_ref, acc_ref):
        k = pl.program_id(2)
        part = _dot_tn(a_ref[...].astype(BF16), b_ref[...].astype(BF16))

        @pl.when(k == 0)
        def _():
            acc_ref[...] = part

        @pl.when(k > 0)
        def _():
            acc_ref[...] += part

        @pl.when(k == nk - 1)
        def _():
            for q in range(n):
                if shard == "cols":
                    o_ref[q] = acc_ref[:, q * Bs:(q + 1) * Bs].astype(stack.dtype)
                else:
                    o_ref[q] = acc_ref[q * As:(q + 1) * As, :].astype(stack.dtype)

    return pl.pallas_call(
        body, name=name, grid=grid,
        in_specs=[pl.BlockSpec((tk, tm), lambda i, j, k: (k, i)), pl.BlockSpec((tk, tn), lambda i, j, k: (k, j)),
                  pl.BlockSpec(memory_space=pl.ANY)],
        out_specs=o_spec,
        out_shape=jax.ShapeDtypeStruct(stack.shape, stack.dtype),
        input_output_aliases={2: 0},
        scratch_shapes=[pltpu.VMEM((tm, tn), F32)],
        compiler_params=_params("parallel", "parallel", "arbitrary"),
    )(a, b, stack)


def _rmsnorm_fwd(x, w, *, dep=(), name):
    S, D = x.shape
    ts = _tile(S, 1024, 16)

    def body(x_ref, w_ref, *rest):
        o_ref = rest[-1]
        xv = x_ref[...]
        r = lax.rsqrt(jnp.mean(xv * xv, axis=-1, keepdims=True) + NORM_EPS)
        o_ref[...] = (xv * r * w_ref[...]).astype(BF16)

    return pl.pallas_call(
        body, name=name, grid=(S // ts,),
        in_specs=[pl.BlockSpec((ts, D), lambda i: (i, 0)), pl.BlockSpec((1, D), lambda i: (0, 0))] + [ANY] * len(dep),
        out_specs=pl.BlockSpec((ts, D), lambda i: (i, 0)),
        out_shape=jax.ShapeDtypeStruct((S, D), BF16),
        compiler_params=_params("parallel"),
    )(x, w, *dep)


def _rmsnorm_bwd(x, w, dh, dres, *, name):
    S, D = x.shape
    ts = _tile(S, 512, 16)
    n = S // ts

    def body(x_ref, w_ref, dh_ref, dr_ref, dx_ref, dw_ref):
        i = pl.program_id(0)
        xv = x_ref[...]
        r = lax.rsqrt(jnp.mean(xv * xv, axis=-1, keepdims=True) + NORM_EPS)
        xh = xv * r
        d = dh_ref[...]
        g = d * w_ref[...]
        dx_ref[...] = dr_ref[...] + r * (g - xh * jnp.mean(g * xh, axis=-1, keepdims=True))
        part = _colsum(d * xh)

        @pl.when(i == 0)
        def _():
            dw_ref[...] = part

        @pl.when(i > 0)
        def _():
            dw_ref[...] += part

    row = pl.BlockSpec((ts, D), lambda i: (i, 0))
    vec = pl.BlockSpec((1, D), lambda i: (0, 0))
    return pl.pallas_call(
        body, name=name, grid=(n,),
        in_specs=[row, vec, row, row], out_specs=[row, vec],
        out_shape=[jax.ShapeDtypeStruct((S, D), F32), jax.ShapeDtypeStruct((1, D), F32)],
        compiler_params=_params("arbitrary"),
    )(x, w, dh, dres)


def _halo_specs(S, ts, tf, row_axis, rows=SUBLANES):
    g = ts // rows
    last = S // rows - 1
    col_axis = 1 - row_axis
    main = pl.BlockSpec((ts, tf), lambda *ij: (ij[row_axis], ij[col_axis]))
    prev = pl.BlockSpec((rows, tf), lambda *ij: (jnp.maximum(ij[row_axis] * g - 1, 0), ij[col_axis]))
    nxt = pl.BlockSpec((rows, tf), lambda *ij: (jnp.minimum((ij[row_axis] + 1) * g, last), ij[col_axis]))
    return [main, prev, nxt]


def _shifted(u, prev_ref, next_ref, i, n):
    ts = u.shape[0]
    rows = prev_ref.shape[0]
    rid = lax.broadcasted_iota(jnp.int32, u.shape, 0)
    before = jnp.where(i > 0, prev_ref[rows - 1:rows, :].astype(F32), 0.0)
    after = jnp.where(i < n - 1, next_ref[0:1, :].astype(F32), 0.0)
    um1 = jnp.where(rid == 0, before, pltpu.roll(u, 1, 0))
    up1 = jnp.where(rid == ts - 1, after, pltpu.roll(u, ts - 1, 0))
    return um1, up1


def _conv3(u, prev_ref, next_ref, w_ref, i, n):
    um1, up1 = _shifted(u, prev_ref, next_ref, i, n)
    return w_ref[0:1, :] * um1 + w_ref[1:2, :] * u + w_ref[2:3, :] * up1


def _conv_act_fwd(uv, ug, wv, wg, bv, bg, *, name):
    S, F = uv.shape
    ts = _tile(S, 512, 16)
    tf = _tile(F, 1408)
    n = S // ts

    def body(v_ref, vp_ref, vn_ref, g_ref, gp_ref, gn_ref, wv_ref, wg_ref, bv_ref, bg_ref, o_ref):
        i = pl.program_id(0)
        val = _conv3(v_ref[...], vp_ref, vn_ref, wv_ref, i, n) + bv_ref[...]
        gate = _conv3(g_ref[...], gp_ref, gn_ref, wg_ref, i, n) + bg_ref[...]
        o_ref[...] = (gate * _sigmoid(gate) * val).astype(BF16)

    halo = _halo_specs(S, ts, tf, 0)
    w3 = pl.BlockSpec((3, tf), lambda i, j: (0, j))
    b1 = pl.BlockSpec((1, tf), lambda i, j: (0, j))
    return pl.pallas_call(
        body, name=name, grid=(n, F // tf),
        in_specs=halo + halo + [w3, w3, b1, b1],
        out_specs=pl.BlockSpec((ts, tf), lambda i, j: (i, j)),
        out_shape=jax.ShapeDtypeStruct((S, F), BF16),
        compiler_params=_params("parallel", "parallel"),
    )(uv, uv, uv, ug, ug, ug, wv, wg, bv, bg)


def _conv_act_bwd(uv, ug, wv, wg, bv, bg, dact, *, name):
    S, F = uv.shape
    ts = _tile(S, 512, 16)
    tf = _tile(F, 1408)
    n = S // ts

    def body(v_ref, vp_ref, vn_ref, g_ref, gp_ref, gn_ref, wv_ref, wg_ref, bv_ref, bg_ref, da_ref,
             dv_ref, dg_ref, dwv_ref, dwg_ref, dbv_ref, dbg_ref):
        i = pl.program_id(1)
        uvv, ugv = v_ref[...], g_ref[...]
        vm1, vp1 = _shifted(uvv, vp_ref, vn_ref, i, n)
        gm1, gp1 = _shifted(ugv, gp_ref, gn_ref, i, n)
        val = wv_ref[0:1, :] * vm1 + wv_ref[1:2, :] * uvv + wv_ref[2:3, :] * vp1 + bv_ref[...]
        gate = wg_ref[0:1, :] * gm1 + wg_ref[1:2, :] * ugv + wg_ref[2:3, :] * gp1 + bg_ref[...]
        sg = _sigmoid(gate)
        da = da_ref[...]
        dval = da * (gate * sg)
        dgate = da * val * (sg * (1.0 + gate * (1.0 - sg)))
        dv_ref[...] = dval.astype(BF16)
        dg_ref[...] = dgate.astype(BF16)
        sums = [(dwv_ref, 0, vm1 * dval), (dwv_ref, 1, uvv * dval), (dwv_ref, 2, vp1 * dval),
                (dwg_ref, 0, gm1 * dgate), (dwg_ref, 1, ugv * dgate), (dwg_ref, 2, gp1 * dgate),
                (dbv_ref, 0, dval), (dbg_ref, 0, dgate)]
        parts = [(ref, r, _colsum(t)) for ref, r, t in sums]

        @pl.when(i == 0)
        def _():
            for ref, r, part in parts:
                ref[r:r + 1, :] = part

        @pl.when(i > 0)
        def _():
            for ref, r, part in parts:
                ref[r:r + 1, :] += part

    halo = _halo_specs(S, ts, tf, 1)
    w3 = pl.BlockSpec((3, tf), lambda j, i: (0, j))
    b1 = pl.BlockSpec((1, tf), lambda j, i: (0, j))
    blk = pl.BlockSpec((ts, tf), lambda j, i: (i, j))
    return pl.pallas_call(
        body, name=name, grid=(F // tf, n),
        in_specs=halo + halo + [w3, w3, b1, b1, blk],
        out_specs=[blk, blk, w3, w3, b1, b1],
        out_shape=[jax.ShapeDtypeStruct((S, F), BF16), jax.ShapeDtypeStruct((S, F), BF16),
                   jax.ShapeDtypeStruct((3, F), F32), jax.ShapeDtypeStruct((3, F), F32),
                   jax.ShapeDtypeStruct((1, F), F32), jax.ShapeDtypeStruct((1, F), F32)],
        compiler_params=_params("parallel", "arbitrary"),
    )(uv, uv, uv, ug, ug, ug, wv, wg, bv, bg, dact)


def _conv_t(duc, w, *, name):
    S, F = duc.shape
    ts = _tile(S, 512, 16)
    tf = _tile(F, 1408)
    n = S // ts

    def body(d_ref, dp_ref, dn_ref, w_ref, o_ref):
        i = pl.program_id(0)
        d = d_ref[...].astype(F32)
        dm1, dp1 = _shifted(d, dp_ref, dn_ref, i, n)
        o_ref[...] = (w_ref[0:1, :] * dp1 + w_ref[1:2, :] * d + w_ref[2:3, :] * dm1).astype(BF16)

    return pl.pallas_call(
        body, name=name, grid=(n, F // tf),
        in_specs=_halo_specs(S, ts, tf, 0, rows=16) + [pl.BlockSpec((3, tf), lambda i, j: (0, j))],
        out_specs=pl.BlockSpec((ts, tf), lambda i, j: (i, j)),
        out_shape=jax.ShapeDtypeStruct((S, F), BF16),
        compiler_params=_params("parallel", "parallel"),
    )(duc, duc, duc, w)


N_QK = ATTN_Q_HEADS + ATTN_KV_HEADS
QKV_DIM = (ATTN_Q_HEADS + 2 * ATTN_KV_HEADS) * ATTN_HD


def _head(hd):
    return slice(hd * ATTN_HD, (hd + 1) * ATTN_HD)


def _attn_prep(proj, cs, sn, qn, kn, *, name):
    S = proj.shape[0]
    ts = _tile(S, 512, 16)

    def body(p_ref, c_ref, s_ref, qn_ref, kn_ref, o_ref):
        c, s = c_ref[...], s_ref[...]
        for hd in range(N_QK):
            xv = p_ref[:, _head(hd)]
            w = qn_ref[...] if hd < ATTN_Q_HEADS else kn_ref[...]
            r = lax.rsqrt(jnp.mean(xv * xv, axis=-1, keepdims=True) + NORM_EPS)
            nrm = xv * r * w
            o_ref[:, _head(hd)] = (nrm * c + pltpu.roll(nrm, ATTN_HD // 2, 1) * s).astype(BF16)
        o_ref[:, N_QK * ATTN_HD:] = p_ref[:, N_QK * ATTN_HD:].astype(BF16)

    row = pl.BlockSpec((ts, QKV_DIM), lambda i: (i, 0))
    rot = pl.BlockSpec((ts, ATTN_HD), lambda i: (i, 0))
    vec = pl.BlockSpec((1, ATTN_HD), lambda i: (0, 0))
    return pl.pallas_call(
        body, name=name, grid=(S // ts,),
        in_specs=[row, rot, rot, vec, vec], out_specs=row,
        out_shape=jax.ShapeDtypeStruct((S, QKV_DIM), BF16),
        compiler_params=_params("parallel"),
    )(proj, cs, sn, qn, kn)


def _attn_prep_bwd(proj, dq, dk, dv, cs, sn, qn, kn, *, name):
    S = proj.shape[0]
    ts = _tile(S, 512, 16)
    nq = ATTN_Q_HEADS * ATTN_HD
    nkv = ATTN_KV_HEADS * ATTN_HD

    def body(p_ref, dq_ref, dk_ref, dv_ref, c_ref, s_ref, qn_ref, kn_ref, o_ref, dqn_ref, dkn_ref):
        i = pl.program_id(0)
        c, s = c_ref[...], s_ref[...]
        acc = [jnp.zeros((1, ATTN_HD), F32), jnp.zeros((1, ATTN_HD), F32)]
        for hd in range(N_QK):
            is_k = hd >= ATTN_Q_HEADS
            xv = p_ref[:, _head(hd)]
            w = kn_ref[...] if is_k else qn_ref[...]
            r = lax.rsqrt(jnp.mean(xv * xv, axis=-1, keepdims=True) + NORM_EPS)
            xh = xv * r
            dout = dk_ref[:, _head(hd - ATTN_Q_HEADS)] if is_k else dq_ref[:, _head(hd)]
            dn = dout * c + pltpu.roll(dout * s, ATTN_HD // 2, 1)
            acc[int(is_k)] = acc[int(is_k)] + _colsum(dn * xh)
            g = dn * w
            o_ref[:, _head(hd)] = (r * (g - xh * jnp.mean(g * xh, axis=-1, keepdims=True))).astype(BF16)
        o_ref[:, N_QK * ATTN_HD:] = dv_ref[...].astype(BF16)

        @pl.when(i == 0)
        def _():
            dqn_ref[...] = acc[0]
            dkn_ref[...] = acc[1]

        @pl.when(i > 0)
        def _():
            dqn_ref[...] += acc[0]
            dkn_ref[...] += acc[1]

    row = pl.BlockSpec((ts, QKV_DIM), lambda i: (i, 0))
    rot = pl.BlockSpec((ts, ATTN_HD), lambda i: (i, 0))
    vec = pl.BlockSpec((1, ATTN_HD), lambda i: (0, 0))
    return pl.pallas_call(
        body, name=name, grid=(S // ts,),
        in_specs=[row, pl.BlockSpec((ts, nq), lambda i: (i, 0)), pl.BlockSpec((ts, nkv), lambda i: (i, 0)),
                  pl.BlockSpec((ts, nkv), lambda i: (i, 0)), rot, rot, vec, vec],
        out_specs=[row, vec, vec],
        out_shape=[jax.ShapeDtypeStruct((S, QKV_DIM), BF16), jax.ShapeDtypeStruct((1, ATTN_HD), F32),
                   jax.ShapeDtypeStruct((1, ATTN_HD), F32)],
        compiler_params=_params("arbitrary"),
    )(proj, dq, dk, dv, cs, sn, qn, kn)


ATTN_SCALE = ATTN_HD ** -0.5


def _softmax_of(s):
    e = jnp.exp2((s - jnp.max(s, axis=-1, keepdims=True)) * (ATTN_SCALE * math.log2(math.e)))
    return e, 1.0 / jnp.sum(e, axis=-1, keepdims=True)


def _softmax_parts(q, k):
    return _softmax_of(_dot_nt(q, k))


def _attn_fwd(qkv, *, name):
    S = qkv.shape[0]
    tq = _tile(S, 256, 16)
    n = S // tq

    def body(q_ref, qn_ref, k_ref, v_ref, o_ref, s_a, s_b):
        i = pl.program_id(1)

        @pl.when(i == 0)
        def _():
            s_a[...] = _dot_nt(q_ref[...], k_ref[...])

        def step(s_cur, s_next):
            k, v = k_ref[...], v_ref[...]
            s_next[...] = _dot_nt(qn_ref[...], k)
            e, rl = _softmax_of(s_cur[...])
            o_ref[...] = (_dot(e.astype(BF16), v) * rl).astype(BF16)

        @pl.when(i % 2 == 0)
        def _():
            step(s_a, s_b)

        @pl.when(i % 2 == 1)
        def _():
            step(s_b, s_a)

    return pl.pallas_call(
        body, name=name, grid=(ATTN_Q_HEADS, n),
        in_specs=[pl.BlockSpec((tq, ATTN_HD), lambda h, i: (i, h)),
                  pl.BlockSpec((tq, ATTN_HD), lambda h, i: (jnp.minimum(i + 1, n - 1), h)),
                  pl.BlockSpec((S, ATTN_HD), lambda h, i: (0, ATTN_Q_HEADS + h // ATTN_GROUP)),
                  pl.BlockSpec((S, ATTN_HD), lambda h, i: (0, N_QK + h // ATTN_GROUP))],
        out_specs=pl.BlockSpec((tq, ATTN_HD), lambda h, i: (i, h)),
        out_shape=jax.ShapeDtypeStruct((S, ATTN_Q_HEADS * ATTN_HD), BF16),
        scratch_shapes=[pltpu.VMEM((tq, S), F32), pltpu.VMEM((tq, S), F32)],
        compiler_params=_params("parallel", "arbitrary"),
    )(qkv, qkv, qkv, qkv)


def _attn_bwd(qkv, do, *, name):
    S = qkv.shape[0]
    tq = _tile(S, 256, 16)
    n = S // tq

    def body(q_ref, do_ref, qn_ref, don_ref, k_ref, v_ref, dq_ref, dk_ref, dv_ref, s_a, dp_a, s_b, dp_b):
        i = pl.program_id(2)

        @pl.when(i == 0)
        def _():
            s_a[...] = _dot_nt(q_ref[...], k_ref[...])
            dp_a[...] = _dot_nt(do_ref[...], v_ref[...])

            @pl.when(pl.program_id(1) == 0)
            def _():
                dk_ref[...] = jnp.zeros_like(dk_ref)
                dv_ref[...] = jnp.zeros_like(dv_ref)

        def step(s_cur, dp_cur, s_next, dp_next):
            k, v = k_ref[...], v_ref[...]
            q, dov = q_ref[...], do_ref[...]
            s_next[...] = _dot_nt(qn_ref[...], k)
            dp_next[...] = _dot_nt(don_ref[...], v)
            e, rl = _softmax_of(s_cur[...])
            dp = dp_cur[...]
            delta = jnp.sum(e * dp, axis=-1, keepdims=True) * rl
            dsb = (e * (dp - delta) * (rl * ATTN_SCALE)).astype(BF16)
            dq_ref[...] = _dot(dsb, k)
            dk_ref[...] += _dot_tn(dsb, q)
            dv_ref[...] += _dot_tn(e.astype(BF16), (dov.astype(F32) * rl).astype(BF16))

        @pl.when(i % 2 == 0)
        def _():
            step(s_a, dp_a, s_b, dp_b)

        @pl.when(i % 2 == 1)
        def _():
            step(s_b, dp_b, s_a, dp_a)

    qblk = pl.BlockSpec((tq, ATTN_HD), lambda kv, g, i: (i, kv * ATTN_GROUP + g))
    qnext = pl.BlockSpec((tq, ATTN_HD), lambda kv, g, i: (jnp.minimum(i + 1, n - 1), kv * ATTN_GROUP + g))
    kvacc = pl.BlockSpec((S, ATTN_HD), lambda kv, g, i: (0, kv))
    return pl.pallas_call(
        body, name=name, grid=(ATTN_KV_HEADS, ATTN_GROUP, n),
        in_specs=[qblk, qblk, qnext, qnext,
                  pl.BlockSpec((S, ATTN_HD), lambda kv, g, i: (0, ATTN_Q_HEADS + kv)),
                  pl.BlockSpec((S, ATTN_HD), lambda kv, g, i: (0, N_QK + kv))],
        out_specs=[qblk, kvacc, kvacc],
        out_shape=[jax.ShapeDtypeStruct((S, ATTN_Q_HEADS * ATTN_HD), F32),
                   jax.ShapeDtypeStruct((S, ATTN_KV_HEADS * ATTN_HD), F32),
                   jax.ShapeDtypeStruct((S, ATTN_KV_HEADS * ATTN_HD), F32)],
        scratch_shapes=[pltpu.VMEM((tq, S), F32) for _ in range(4)],
        compiler_params=_params("parallel", "arbitrary", "arbitrary"),
    )(qkv, do, qkv, do, qkv, qkv)


GLA_KD = GLA_HEADS * GLA_DK
GLA_VD = GLA_HEADS * GLA_DV
GLA_PROJ = 2 * GLA_KD + 2 * GLA_VD + LANES
GLA_SCALE = GLA_DK ** -0.5


def _split3(x):
    hi = x.astype(BF16)
    r1 = x - hi.astype(F32)
    mid = r1.astype(BF16)
    lo = (r1 - mid.astype(F32)).astype(BF16)
    return hi, mid, lo


def _cumdot(t, x):
    hi, mid, lo = _split3(x)
    return _dot(t, hi) + _dot(t, mid) + _dot(t, lo)


def _gla_masks(d):
    c = GLA_CHUNK
    row = lax.broadcasted_iota(jnp.int32, (c, c), 0)
    col = lax.broadcasted_iota(jnp.int32, (c, c), 1)
    lower, upper = col <= row, col >= row
    if d == 0:
        return lower.astype(BF16), upper.astype(BF16), lower
    return upper.astype(BF16), lower.astype(BF16), col > row


def _gla_decay(lg, bias, cum, d):
    xl = lg + bias
    la = _log_sigmoid(xl) * (1.0 / GLA_GATE_NORMALIZER)
    b = _cumdot(cum, la)
    b_end = b[GLA_CHUNK - 1:GLA_CHUNK, :] if d == 0 else b[0:1, :]
    return xl, b, b_end


def _gla_specs(S, n):
    c = GLA_CHUNK
    up = lambda i: i
    down = lambda i: n - 1 - i
    def specs(order):
        return dict(
            q=pl.BlockSpec((c, GLA_KD), lambda i: (order(i), 0)),
            k=pl.BlockSpec((c, GLA_KD), lambda i: (order(i), 1)),
            v=pl.BlockSpec((c, GLA_VD), lambda i: (order(i), 1)),
            st=pl.BlockSpec((1, GLA_HEADS, GLA_DV, GLA_DK), lambda i: (order(i), 0, 0, 0)),
            wide=pl.BlockSpec((c, GLA_VD), lambda i: (order(i), 0)),
            qkv=pl.BlockSpec((c, 2 * GLA_KD + GLA_VD), lambda i: (order(i), 0)),
        )
    return specs(up), specs(down), up, down


def _gla_fwd(proj, logits, bias, *, name):
    S = proj.shape[0]
    c = GLA_CHUNK
    n = S // c
    su, sd, up, down = _gla_specs(S, n)

    def body(qf, kf, vf, lf, qb, kb, vb, lb, bias_ref, of, ob, sf, sb, st):
        @pl.when(pl.program_id(0) == 0)
        def _():
            st[...] = jnp.zeros_like(st)

        for d, (q_r, k_r, v_r, l_r, o_r, s_r) in enumerate(((qf, kf, vf, lf, of, sf), (qb, kb, vb, lb, ob, sb))):
            cum, _, mask = _gla_masks(d)
            _, b, b_end = _gla_decay(l_r[...], bias_ref[:, d * GLA_KD:(d + 1) * GLA_KD], cum, d)
            dend = jnp.exp(b_end)
            k = k_r[...]
            qd = (q_r[...] * GLA_SCALE * jnp.exp(b)).astype(BF16)
            ki = (k * jnp.exp(-b)).astype(BF16)
            ke = (k * jnp.exp(b_end - b)).astype(BF16)
            for h in range(GLA_HEADS):
                ks = slice(h * GLA_DK, (h + 1) * GLA_DK)
                vs = slice(h * GLA_DV, (h + 1) * GLA_DV)
                stp = st[d * GLA_HEADS + h]
                s_r[0, h] = stp
                v = v_r[:, vs].astype(BF16)
                att = jnp.where(mask, _dot_nt(qd[:, ks], ki[:, ks]), 0.0).astype(BF16)
                o_r[:, vs] = _dot(att, v) + _dot_nt(qd[:, ks], stp.astype(BF16))
                st[d * GLA_HEADS + h] = stp * dend[:, ks] + _dot_tn(v, ke[:, ks])

    lg_f = pl.BlockSpec((c, GLA_KD), lambda i: (up(i), 0))
    lg_b = pl.BlockSpec((c, GLA_KD), lambda i: (down(i), 1))
    return pl.pallas_call(
        body, name=name, grid=(n,),
        in_specs=[su["q"], su["k"], su["v"], lg_f, sd["q"], sd["k"], sd["v"], lg_b,
                  pl.BlockSpec((1, 2 * GLA_KD), lambda i: (0, 0))],
        out_specs=[su["wide"], sd["wide"], su["st"], sd["st"]],
        out_shape=[jax.ShapeDtypeStruct((S, GLA_VD), F32), jax.ShapeDtypeStruct((S, GLA_VD), F32),
                   jax.ShapeDtypeStruct((n, GLA_HEADS, GLA_DV, GLA_DK), F32),
                   jax.ShapeDtypeStruct((n, GLA_HEADS, GLA_DV, GLA_DK), F32)],
        scratch_shapes=[pltpu.VMEM((2 * GLA_HEADS, GLA_DV, GLA_DK), F32)],
        compiler_params=_params("arbitrary"),
    )(proj, proj, proj, logits, proj, proj, proj, logits, bias)


def _gla_bwd(proj, logits, bias, sf, sb, do, *, name):
    S = proj.shape[0]
    c = GLA_CHUNK
    n = S // c
    su, sd, up, down = _gla_specs(S, n)

    def body(qf, kf, vf, lf, stf, dof, qb, kb, vb, lb, stb, dob, bias_ref,
             dqkv_f, dlg_f, dqkv_b, dlg_b, dbias, dst):
        first = pl.program_id(0) == 0

        @pl.when(first)
        def _():
            dst[...] = jnp.zeros_like(dst)

        dbias_parts = []
        for d, (q_r, k_r, v_r, l_r, s_r, do_r, dqkv_r, dlg_r) in enumerate(
                ((qf, kf, vf, lf, stf, dof, dqkv_f, dlg_f), (qb, kb, vb, lb, stb, dob, dqkv_b, dlg_b))):
            cum, cum_t, mask = _gla_masks(d)
            xl, b, b_end = _gla_decay(l_r[...], bias_ref[:, d * GLA_KD:(d + 1) * GLA_KD], cum, d)
            e, ei, ee, dend = jnp.exp(b), jnp.exp(-b), jnp.exp(b_end - b), jnp.exp(b_end)
            k = k_r[...]
            qd32 = q_r[...] * GLA_SCALE * e
            ki32 = k * ei
            ke32 = k * ee
            qd, ki, ke = qd32.astype(BF16), ki32.astype(BF16), ke32.astype(BF16)
            db_parts, dbe_parts = [], []
            for h in range(GLA_HEADS):
                ks = slice(h * GLA_DK, (h + 1) * GLA_DK)
                vs = slice(h * GLA_DV, (h + 1) * GLA_DV)
                stp = s_r[0, h]
                dstn = dst[d * GLA_HEADS + h]
                dstn_b = dstn.astype(BF16)
                v = v_r[:, vs].astype(BF16)
                dov = do_r[:, vs]
                att = jnp.where(mask, _dot_nt(qd[:, ks], ki[:, ks]), 0.0).astype(BF16)
                datt = jnp.where(mask, _dot_nt(dov, v), 0.0).astype(BF16)
                dqkv_r[:, 2 * GLA_KD + h * GLA_DV:2 * GLA_KD + (h + 1) * GLA_DV] = (
                    _dot_tn(att, dov) + _dot_nt(ke[:, ks], dstn_b))
                dqd = _dot(datt, ki[:, ks]) + _dot(dov, stp.astype(BF16))
                dki = _dot_tn(datt, qd[:, ks])
                dke = _dot(v, dstn_b)
                d_dend = _colsum(stp * dstn)
                dst[d * GLA_HEADS + h] = _dot_tn(dov, qd[:, ks]) + dstn * dend[:, ks]
                dqkv_r[:, ks] = dqd * e[:, ks] * GLA_SCALE
                dqkv_r[:, GLA_KD + h * GLA_DK:GLA_KD + (h + 1) * GLA_DK] = dki * ei[:, ks] + dke * ee[:, ks]
                dke_ke = dke * ke32[:, ks]
                db_parts.append(dqd * qd32[:, ks] - dki * ki32[:, ks] - dke_ke)
                dbe_parts.append(_colsum(dke_ke) + d_dend * dend[:, ks])
            db = jnp.concatenate(db_parts, axis=1)
            db_end = jnp.concatenate(dbe_parts, axis=1)
            dla = _cumdot(cum_t, db) + db_end
            dlg = dla * (1.0 / GLA_GATE_NORMALIZER) * _sigmoid(-xl)
            dlg_r[...] = dlg
            dbias_parts.append(_colsum(dlg))
        dbv = jnp.concatenate(dbias_parts, axis=1)

        @pl.when(first)
        def _():
            dbias[...] = dbv

        @pl.when(jnp.logical_not(first))
        def _():
            dbias[...] += dbv

    lg_f = pl.BlockSpec((c, GLA_KD), lambda i: (down(i), 0))
    lg_b = pl.BlockSpec((c, GLA_KD), lambda i: (up(i), 1))
    dlg_f = pl.BlockSpec((c, GLA_KD), lambda i: (down(i), 0))
    dlg_b = pl.BlockSpec((c, GLA_KD), lambda i: (up(i), 0))
    return pl.pallas_call(
        body, name=name, grid=(n,),
        in_specs=[sd["q"], sd["k"], sd["v"], lg_f, sd["st"], sd["wide"],
                  su["q"], su["k"], su["v"], lg_b, su["st"], su["wide"],
                  pl.BlockSpec((1, 2 * GLA_KD), lambda i: (0, 0))],
        out_specs=[sd["qkv"], dlg_f, su["qkv"], dlg_b, pl.BlockSpec((1, 2 * GLA_KD), lambda i: (0, 0))],
        out_shape=[jax.ShapeDtypeStruct((S, 2 * GLA_KD + GLA_VD), F32), jax.ShapeDtypeStruct((S, GLA_KD), F32),
                   jax.ShapeDtypeStruct((S, 2 * GLA_KD + GLA_VD), F32), jax.ShapeDtypeStruct((S, GLA_KD), F32),
                   jax.ShapeDtypeStruct((1, 2 * GLA_KD), F32)],
        scratch_shapes=[pltpu.VMEM((2 * GLA_HEADS, GLA_DV, GLA_DK), F32)],
        compiler_params=_params("arbitrary"),
    )(proj, proj, proj, logits, sf, do, proj, proj, proj, logits, sb, do, bias)


def _gla_gate_fwd(of, ob, proj, w, *, name):
    S = of.shape[0]
    ts = _tile(S, 512, 16)

    def body(of_ref, ob_ref, g_ref, w_ref, y_ref):
        for h in range(GLA_HEADS):
            vs = slice(h * GLA_DV, (h + 1) * GLA_DV)
            o = of_ref[:, vs] + ob_ref[:, vs]
            r = lax.rsqrt(jnp.mean(o * o, axis=-1, keepdims=True) + NORM_EPS)
            g = g_ref[:, vs]
            y_ref[:, vs] = (o * r * w_ref[...] * (g * _sigmoid(g))).astype(BF16)

    wide = pl.BlockSpec((ts, GLA_VD), lambda i: (i, 0))
    return pl.pallas_call(
        body, name=name, grid=(S // ts,),
        in_specs=[wide, wide, pl.BlockSpec((ts, GLA_VD), lambda i: (i, 2)), pl.BlockSpec((1, GLA_DV), lambda i: (0, 0))],
        out_specs=wide,
        out_shape=jax.ShapeDtypeStruct((S, GLA_VD), BF16),
        compiler_params=_params("parallel"),
    )(of, ob, proj, w)


def _gla_gate_bwd(of, ob, proj, w, dy, *, name):
    S = of.shape[0]
    ts = _tile(S, 512, 16)

    def body(of_ref, ob_ref, g_ref, w_ref, dy_ref, do_ref, dg_ref, dw_ref):
        i = pl.program_id(0)
        acc = jnp.zeros((1, GLA_DV), F32)
        for h in range(GLA_HEADS):
            vs = slice(h * GLA_DV, (h + 1) * GLA_DV)
            o = of_ref[:, vs] + ob_ref[:, vs]
            r = lax.rsqrt(jnp.mean(o * o, axis=-1, keepdims=True) + NORM_EPS)
            oh = o * r
            g = g_ref[:, vs]
            sg = _sigmoid(g)
            dyv = dy_ref[:, vs]
            dn = dyv * (g * sg)
            dg_ref[:, vs] = dyv * (oh * w_ref[...]) * (sg * (1.0 + g * (1.0 - sg)))
            acc = acc + _colsum(dn * oh)
            gg = dn * w_ref[...]
            do_ref[:, vs] = (r * (gg - oh * jnp.mean(gg * oh, axis=-1, keepdims=True))).astype(BF16)

        @pl.when(i == 0)
        def _():
            dw_ref[...] = acc

        @pl.when(i > 0)
        def _():
            dw_ref[...] += acc

    wide = pl.BlockSpec((ts, GLA_VD), lambda i: (i, 0))
    vec = pl.BlockSpec((1, GLA_DV), lambda i: (0, 0))
    return pl.pallas_call(
        body, name=name, grid=(S // ts,),
        in_specs=[wide, wide, pl.BlockSpec((ts, GLA_VD), lambda i: (i, 2)), vec, wide],
        out_specs=[wide, wide, vec],
        out_shape=[jax.ShapeDtypeStruct((S, GLA_VD), BF16), jax.ShapeDtypeStruct((S, GLA_VD), F32),
                   jax.ShapeDtypeStruct((1, GLA_DV), F32)],
        compiler_params=_params("arbitrary"),
    )(of, ob, proj, w, dy)


def _gla_combine(dqkv_f, dqkv_b, dg, dr, *, name):
    S = dg.shape[0]
    ts = _tile(S, 512, 16)
    nqkv = 2 * GLA_KD + GLA_VD

    def body(f_ref, b_ref, g_ref, r_ref, o_ref):
        o_ref[:, :nqkv] = (f_ref[...] + b_ref[...]).astype(BF16)
        o_ref[:, nqkv:nqkv + GLA_VD] = g_ref[...].astype(BF16)
        o_ref[:, nqkv + GLA_VD:] = r_ref[...].astype(BF16)

    return pl.pallas_call(
        body, name=name, grid=(S // ts,),
        in_specs=[pl.BlockSpec((ts, nqkv), lambda i: (i, 0)), pl.BlockSpec((ts, nqkv), lambda i: (i, 0)),
                  pl.BlockSpec((ts, GLA_VD), lambda i: (i, 0)), pl.BlockSpec((ts, LANES), lambda i: (i, 0))],
        out_specs=pl.BlockSpec((ts, GLA_PROJ), lambda i: (i, 0)),
        out_shape=jax.ShapeDtypeStruct((S, GLA_PROJ), BF16),
        compiler_params=_params("parallel"),
    )(dqkv_f, dqkv_b, dg, dr)


def _loss_head(y, t, *, name):
    S, D = y.shape
    ts = _tile(S, 512, 16)
    n = S // ts

    def body(y_ref, t_ref, dy_ref, l_ref, acc):
        i = pl.program_id(0)
        diff = y_ref[...] - t_ref[...]
        dy_ref[...] = diff * (1.0 / D)
        part = _colsum(diff * diff)

        @pl.when(i == 0)
        def _():
            acc[...] = part

        @pl.when(i > 0)
        def _():
            acc[...] += part

        @pl.when(i == n - 1)
        def _():
            l_ref[...] = jnp.full(l_ref.shape, 0.5 / D, F32) * jnp.sum(acc[...])

    row = pl.BlockSpec((ts, D), lambda i: (i, 0))
    return pl.pallas_call(
        body, name=name, grid=(n,),
        in_specs=[row, row],
        out_specs=[row, pl.BlockSpec((SUBLANES, LANES), lambda i: (0, 0))],
        out_shape=[jax.ShapeDtypeStruct((S, D), F32), jax.ShapeDtypeStruct((SUBLANES, LANES), F32)],
        scratch_shapes=[pltpu.VMEM((1, D), F32)],
        compiler_params=_params("arbitrary"),
    )(y, t)


def _adamw(w, g, m, v, *, name):
    R, C = w.shape
    tr = _tile(R, 256, SUBLANES)

    def body(w_ref, g_ref, m_ref, v_ref, d_ref, nm_ref, nv_ref):
        gv = g_ref[...]
        nm = ADAM_B1 * m_ref[...] + (1.0 - ADAM_B1) * gv
        nv = ADAM_B2 * v_ref[...] + (1.0 - ADAM_B2) * (gv * gv)
        m_hat = nm / (1.0 - ADAM_B1 ** ADAM_STEP)
        v_hat = nv / (1.0 - ADAM_B2 ** ADAM_STEP)
        d_ref[...] = -ADAM_LR * (m_hat / (jnp.sqrt(v_hat) + ADAM_EPS) + ADAM_WD * w_ref[...])
        nm_ref[...] = nm
        nv_ref[...] = nv

    blk = pl.BlockSpec((tr, C), lambda i: (i, 0))
    shp = jax.ShapeDtypeStruct((R, C), F32)
    return pl.pallas_call(
        body, name=name, grid=(R // tr,),
        in_specs=[blk] * 4, out_specs=[blk] * 3, out_shape=[shp] * 3,
        compiler_params=_params("parallel"),
    )(w, g, m, v)


def _place():
    return lax.axis_index("x"), lax.axis_index("y"), lax.axis_index("c")


def _all_gather(block, *, name):
    R, L = block.shape

    def body(x_ref, out_ref, send_sems, recv_sems, local_sem):
        x, y, c = _place()
        me, sibling = (x, y, c), (x, y, 1 - c)
        chips = [(1 - x, y), (x, 1 - y), (1 - x, 1 - y)]

        def slot(px, py, pc):
            return out_ref.at[4 * px + 2 * py + pc]

        def copy(k, blk, to, src=None):
            return pltpu.make_async_remote_copy(
                src_ref=slot(*blk) if src is None else src, dst_ref=slot(*blk),
                send_sem=send_sems.at[k], recv_sem=recv_sems.at[k], device_id=to, device_id_type=MESH)

        mine = pltpu.make_async_copy(x_ref, slot(*me), local_sem)
        mine.start()
        first = [copy(0, me, sibling, src=x_ref)]
        first += [copy(1 + j, me, (*chip, c), src=x_ref) for j, chip in enumerate(chips)]
        for cp in first:
            cp.start()
        passed = [copy(4 + j, (*chip, c), sibling) for j, chip in enumerate(chips)]
        for j, chip in enumerate(chips):
            copy(1 + j, (*chip, c), me).wait_recv()
            passed[j].start()
        copy(0, sibling, me).wait_recv()
        for j, chip in enumerate(chips):
            copy(4 + j, (*chip, 1 - c), me).wait_recv()
        for cp in first + passed:
            cp.wait_send()
        mine.wait()

    return pl.pallas_call(
        body, name=name, in_specs=[ANY], out_specs=ANY,
        out_shape=jax.ShapeDtypeStruct((N_DEV, R, L), block.dtype),
        scratch_shapes=[pltpu.SemaphoreType.DMA((7,)), pltpu.SemaphoreType.DMA((7,)), pltpu.SemaphoreType.DMA],
    )(block)


HBM_SPEC = pl.BlockSpec(memory_space=pltpu.HBM)
SEM_SPEC = pl.BlockSpec(memory_space=pltpu.SEMAPHORE)
DATAFLOW = pltpu.SideEffectType.DATAFLOW_SIDE_EFFECTING


def _split_start(plan, srcs, lands, *, dep=(), name):
    make_copies, count = plan
    ns, nb = len(srcs), len(srcs) + len(lands)
    bufs = [pltpu.with_memory_space_constraint(a, pltpu.HBM) for a in list(srcs) + list(lands)]
    n_in = nb + len(dep)

    def body(*refs):
        send_sems, recv_sems, token = refs[n_in], refs[n_in + 1], refs[-1]
        for cp in make_copies(refs[:ns], refs[ns:nb], send_sems, recv_sems):
            cp.start()
        token[...] = jnp.zeros_like(token)

    outs = pl.pallas_call(
        body, name=name, in_specs=[HBM_SPEC] * nb + [ANY] * len(dep),
        out_specs=(SEM_SPEC, SEM_SPEC, *[HBM_SPEC] * nb, pl.BlockSpec(memory_space=pltpu.VMEM)),
        out_shape=(pltpu.SemaphoreType.DMA((count,)), pltpu.SemaphoreType.DMA((count,)),
                   *[pltpu.HBM(a.shape, a.dtype) for a in bufs], jax.ShapeDtypeStruct((SUBLANES, LANES), F32)),
        input_output_aliases={i: 2 + i for i in range(nb)},
        compiler_params=pltpu.CompilerParams(has_side_effects=DATAFLOW),
    )(*bufs, *dep)
    return dict(plan=plan, ns=ns, send=outs[0], recv=outs[1], bufs=list(outs[2:2 + nb]), token=outs[-1])


def _split_wait(started, after, *, name):
    make_copies, _ = started["plan"]
    ns, nb = started["ns"], len(started["bufs"])
    after = tuple(after) if isinstance(after, (tuple, list)) else (after,)

    def body(*refs):
        for cp in make_copies(refs[:ns], refs[ns:nb], refs[nb], refs[nb + 1]):
            cp.wait_send()
            cp.wait_recv()

    outs = pl.pallas_call(
        body, name=name, in_specs=[HBM_SPEC] * nb + [SEM_SPEC, SEM_SPEC] + [ANY] * len(after),
        out_specs=[HBM_SPEC] * nb,
        out_shape=[pltpu.HBM(a.shape, a.dtype) for a in started["bufs"]],
        input_output_aliases={i: i for i in range(nb)},
        compiler_params=pltpu.CompilerParams(has_side_effects=DATAFLOW),
    )(*started["bufs"], started["send"], started["recv"], *after)
    return list(outs[:ns]), list(outs[ns:])


def _remote(src, dst, send_sems, recv_sems, k, to):
    return pltpu.make_async_remote_copy(src_ref=src, dst_ref=dst, send_sem=send_sems.at[k], recv_sem=recv_sems.at[k],
                                        device_id=to, device_id_type=MESH)


def _other_chips(x, y):
    return [(1 - x, y), (x, 1 - y), (1 - x, 1 - y)]


def _gather_send_plan(n):
    def make(srcs, lands, send_sems, recv_sems):
        x, y, c = _place()
        targets = [(x, y, 1 - c)] + [(cx, cy, c) for cx, cy in _other_chips(x, y)]
        return [_remote(srcs[t], lands[t].at[4 * x + 2 * y + c], send_sems, recv_sems, 4 * t + k, to)
                for t in range(n) for k, to in enumerate(targets)]
    return make, 4 * n


def _gather_pass_plan(n):
    def make(srcs, lands, send_sems, recv_sems):
        x, y, c = _place()
        cps = []
        for t in range(n):
            for j, (cx, cy) in enumerate(_other_chips(x, y)):
                slot = lands[t].at[4 * cx + 2 * cy + c]
                cps.append(_remote(slot, slot, send_sems, recv_sems, 3 * t + j, (x, y, 1 - c)))
        return cps
    return make, 3 * n


def _reduce_sibling_plan(n):
    def make(srcs, lands, send_sems, recv_sems):
        x, y, c = _place()
        return [_remote(srcs[t].at[2 * k + 1 - c], lands[t].at[k], send_sems, recv_sems, 4 * t + k, (x, y, 1 - c))
                for t in range(n) for k in range(4)]
    return make, 4 * n


def _reduce_chip_plan(n):
    def make(srcs, lands, send_sems, recv_sems):
        x, y, c = _place()
        return [_remote(srcs[t].at[2 * cx + cy], lands[t].at[2 * x + y], send_sems, recv_sems, 3 * t + j, (cx, cy, c))
                for t in range(n) for j, (cx, cy) in enumerate(_other_chips(x, y))]
    return make, 3 * n


def _unshard_cols(g, own, dev_idx, groups, width, *, name):
    _, A, Bs = g.shape
    ta = _tile(A, 512, 16)

    def body(dev_ref, g_ref, own_ref, *o_refs):
        for o_ref, devs in zip(o_refs, groups):
            for q, d in enumerate(devs):
                o_ref[:, q * Bs:(q + 1) * Bs] = jnp.where(dev_ref[0] == d, own_ref[...], g_ref[d])
            if len(devs) * Bs < width:
                o_ref[:, len(devs) * Bs:] = jnp.zeros((ta, width - len(devs) * Bs), g.dtype)

    return pl.pallas_call(
        body, name=name,
        grid_spec=pltpu.PrefetchScalarGridSpec(
            num_scalar_prefetch=1, grid=(A // ta,),
            in_specs=[pl.BlockSpec((N_DEV, ta, Bs), lambda i, d: (0, i, 0)), pl.BlockSpec((ta, Bs), lambda i, d: (i, 0))],
            out_specs=[pl.BlockSpec((ta, width), lambda i, d: (i, 0)) for _ in groups]),
        out_shape=[jax.ShapeDtypeStruct((A, width), g.dtype) for _ in groups],
        compiler_params=_params("parallel"),
    )(dev_idx, g, own)


def _place_own(g, own, dev_idx, *, name):
    _, As, B = g.shape
    ta = _tile(As, 256, 16)

    def body(dev_ref, _, own_ref, o_ref):
        o_ref[...] = own_ref[...]

    out = pl.pallas_call(
        body, name=name,
        grid_spec=pltpu.PrefetchScalarGridSpec(
            num_scalar_prefetch=1, grid=(As // ta,),
            in_specs=[ANY, pl.BlockSpec((ta, B), lambda i, d: (i, 0))],
            out_specs=pl.BlockSpec((None, ta, B), lambda i, d: (d[0], i, 0))),
        out_shape=jax.ShapeDtypeStruct(g.shape, g.dtype),
        input_output_aliases={1: 0},
        compiler_params=_params("parallel"),
    )(dev_idx, g, own)
    return out.reshape(N_DEV * As, B)


def _add_sibling(g, buf, c_idx, *, name):
    _, A, B = g.shape
    ta = _tile(A, 1024, 16)

    def body(c_ref, g_ref, b_ref, o_ref):
        o_ref[...] = (g_ref[...].astype(F32) + b_ref[...].astype(F32)).astype(BF16)

    blk = pl.BlockSpec((None, ta, B), lambda k, i, c_ref: (k, i, 0))
    return pl.pallas_call(
        body, name=name,
        grid_spec=pltpu.PrefetchScalarGridSpec(
            num_scalar_prefetch=1, grid=(4, A // ta),
            in_specs=[pl.BlockSpec((None, ta, B), lambda k, i, c_ref: (2 * k + c_ref[0], i, 0)), blk],
            out_specs=blk),
        out_shape=jax.ShapeDtypeStruct((4, A, B), BF16),
        compiler_params=_params("parallel", "parallel"),
    )(c_idx, g, buf)


def _adamw_layer(w, m, v, own, parts, chip_idx, outs, layer, *, name):
    _, A, B = w.shape
    ta = _tile(A, 512, 16)

    def body(chip_ref, w_ref, m_ref, v_ref, own_ref, p_ref, *rest):
        g_ref, d_ref, nm_ref, nv_ref = rest[4:]
        gv = None
        for j in range(4):
            part = jnp.where(chip_ref[0] == j, own_ref[...], p_ref[j]).astype(F32)
            gv = part if gv is None else gv + part
        nm = ADAM_B1 * m_ref[...] + (1.0 - ADAM_B1) * gv
        nv = ADAM_B2 * v_ref[...] + (1.0 - ADAM_B2) * (gv * gv)
        m_hat = nm / (1.0 - ADAM_B1 ** ADAM_STEP)
        v_hat = nv / (1.0 - ADAM_B2 ** ADAM_STEP)
        g_ref[...] = gv
        d_ref[...] = -ADAM_LR * (m_hat / (jnp.sqrt(v_hat) + ADAM_EPS) + ADAM_WD * w_ref[...])
        nm_ref[...] = nm
        nv_ref[...] = nv

    blk = pl.BlockSpec((None, ta, B), lambda i, ch: (layer, i, 0))
    return pl.pallas_call(
        body, name=name,
        grid_spec=pltpu.PrefetchScalarGridSpec(
            num_scalar_prefetch=1, grid=(A // ta,),
            in_specs=[blk, blk, blk, pl.BlockSpec((None, ta, B), lambda i, ch: (ch[0], i, 0)),
                      pl.BlockSpec((4, ta, B), lambda i, ch: (0, i, 0))] + [ANY] * 4,
            out_specs=[blk] * 4),
        out_shape=[jax.ShapeDtypeStruct(o.shape, o.dtype) for o in outs],
        input_output_aliases={6 + q: q for q in range(4)},
        compiler_params=_params("parallel"),
    )(chip_idx, w, m, v, own, parts, *outs)


def _sum_slots(buf, *, name):
    n, R, L = buf.shape
    tr = _tile(R, 512, SUBLANES)

    def body(b_ref, o_ref):
        acc = b_ref[0]
        for j in range(1, n):
            acc = acc + b_ref[j]
        o_ref[...] = acc

    return pl.pallas_call(
        body, name=name, grid=(R // tr,),
        in_specs=[pl.BlockSpec((n, tr, L), lambda i: (0, i, 0))],
        out_specs=pl.BlockSpec((tr, L), lambda i: (i, 0)),
        out_shape=jax.ShapeDtypeStruct((R, L), buf.dtype),
        compiler_params=_params("parallel"),
    )(buf)


BIG = ("gla_w_in", "gla_w_out", "attn_w_qkv", "attn_w_out", "ffn_w_up", "ffn_w_down")
SMALL_SHARDED = ("gla_w_gate_up_f", "gla_w_gate_up_b", "ffn_w_conv")
REPLICATED = ("norm_mix", "norm_ffn", "gla_b_gate_f", "gla_b_gate_b", "gla_norm", "attn_q_norm", "attn_k_norm",
              "ffn_b_conv")
WEIGHTS = ("norm_mix", "norm_ffn", "gla_w_in", "gla_w_gate_up_f", "gla_b_gate_f", "gla_w_gate_up_b", "gla_b_gate_b",
           "gla_norm", "gla_w_out", "attn_w_qkv", "attn_q_norm", "attn_k_norm", "attn_w_out", "ffn_w_up", "ffn_w_conv",
           "ffn_b_conv", "ffn_w_down")


def _rows(flat, row_align):
    n = flat.shape[0]
    per = row_align * LANES
    padded = -(-n // per) * per
    return jnp.pad(flat, (0, padded - n)).reshape(padded // LANES, LANES)


def _side_by_side(gathered, own, dev):
    n, a, b = gathered.shape
    whole = lax.dynamic_update_index_in_dim(gathered, own, dev, 0)
    return jnp.transpose(whole, (1, 0, 2)).reshape(a, n * b)


def _layer_shards(w, i, mixer, ffn):
    j = i // 2
    parts = []
    if mixer and i % 2 == 0:
        parts += [("mix_in", w["gla_w_in"][j].astype(BF16)), ("mix_out", w["gla_w_out"][j].astype(BF16)),
                  ("gate_f", w["gla_w_gate_up_f"][j].astype(BF16)), ("gate_b", w["gla_w_gate_up_b"][j].astype(BF16))]
    elif mixer:
        parts += [("mix_in", w["attn_w_qkv"][j].astype(BF16)), ("mix_out", w["attn_w_out"][j].astype(BF16))]
    if ffn:
        parts += [("up", w["ffn_w_up"][i].astype(BF16)), ("down", w["ffn_w_down"][i].astype(BF16)),
                  ("conv", w["ffn_w_conv"][i])]
    return [n for n, _ in parts], [a for _, a in parts]


def _layer_weights(names, own, gathered, dev, dev_idx, i):
    own, got = dict(zip(names, own)), dict(zip(names, gathered))
    every = tuple(range(N_DEV))
    half = N_DEV // 2
    tag = "_l%d" % i
    out = {}
    if "mix_in" in got:
        width = GLA_PROJ if i % 2 == 0 else QKV_DIM
        (out["mix_in"],) = _unshard_cols(got["mix_in"], own["mix_in"], dev_idx, [every], width,
                                         name="unshard_mix_in" + tag)
        out["mix_out"] = _place_own(got["mix_out"], own["mix_out"], dev_idx, name="place_mix_out" + tag)
    if "gate_f" in got:
        out["gate"] = _gate_matrix(_side_by_side(got["gate_f"], own["gate_f"], dev),
                                   _side_by_side(got["gate_b"], own["gate_b"], dev))
    if "up" in got:
        f = got["up"].shape[-1] * half
        out["up_val"], out["up_gate"] = _unshard_cols(got["up"], own["up"], dev_idx, [every[:half], every[half:]], f,
                                                      name="unshard_ffn_up" + tag)
        out["down"] = _place_own(got["down"], own["down"], dev_idx, name="place_ffn_down" + tag)
        out["conv"] = _side_by_side(got["conv"], own["conv"], dev)
    return out


def _rope_tables(S):
    rows = S // GRID_W
    pairs = ATTN_HD // 4
    row_idx = jnp.repeat(jnp.arange(rows, dtype=F32), GRID_W)
    col_idx = jnp.tile(jnp.arange(GRID_W, dtype=F32), rows)
    inv_freq = ROPE_THETA ** (-jnp.arange(pairs, dtype=F32) / pairs)
    ang = jnp.concatenate([row_idx[:, None] * inv_freq, col_idx[:, None] * inv_freq], axis=-1)
    cos, sin = jnp.cos(ang), jnp.sin(ang)
    return jnp.concatenate([cos, cos], axis=-1), jnp.concatenate([-sin, sin], axis=-1)


def _gate_matrix(w_f, w_b):
    rk = w_f.shape[0]
    top = jnp.concatenate([w_f, jnp.zeros_like(w_f)], axis=1)
    mid = jnp.concatenate([jnp.zeros_like(w_b), w_b], axis=1)
    pad = jnp.zeros((LANES - 2 * rk, 2 * GLA_KD), w_f.dtype)
    return jnp.concatenate([top, mid, pad], axis=0)


def _local_step(x, target, rep, w, dev, idx):
    S, D = x.shape
    depth = rep["norm_mix"].shape[0]
    cs, sn = _rope_tables(S)
    row = lambda a: a.reshape(1, -1)
    ranks_cols = (GLA_PROJ - LANES, LANES)

    groups = [(0, True, False), (0, False, True)] + [(i, True, True) for i in range(1, depth)]
    sent = []
    for g, (i, mixer, ffn) in enumerate(groups):
        names, srcs = _layer_shards(w, i, mixer, ffn)
        lands = [lax.empty((N_DEV,) + a.shape, a.dtype) for a in srcs]
        dep = (sent[-1][1]["token"],) if sent else ()
        sent.append((names, _split_start(_gather_send_plan(len(srcs)), srcs, lands, dep=dep,
                                         name="weights_send_g%d" % g)))

    def arrive(g, after):
        names, started = sent[g]
        own, lands = _split_wait(started, after, name="weights_arrive_g%d" % g)
        return names, own, _split_start(_gather_pass_plan(len(lands)), [], lands, name="weights_pass_g%d" % g)

    def ready(g, passing, after):
        names, own, started = passing
        _, lands = _split_wait(started, after, name="weights_passed_g%d" % g)
        return _layer_weights(names, own, lands, dev, idx["dev"], groups[g][0])

    passing = arrive(0, sent[-1][1]["token"])
    wl = ready(0, passing, passing[2]["token"])
    passing = arrive(1, wl["mix_in"])

    saved = []
    for i in range(depth):
        j = i // 2
        sv = {"x0": x, "w": wl}
        h1 = _rmsnorm_fwd(x, row(rep["norm_mix"][i]), dep=(passing[2]["token"],) if i == 0 else (), name="norm_mix_fwd")
        sv["h1"] = h1
        if i % 2 == 0:
            bias = jnp.concatenate([rep["gla_b_gate_f"][j], rep["gla_b_gate_b"][j]]).reshape(1, -1)
            proj = _mm(h1, wl["mix_in"], name="gla_in_proj")
            logits = _mm(proj, wl["gate"], a_cols=ranks_cols, name="gla_gate_logits")
            of, ob, sf, sb = _gla_fwd(proj, logits, bias, name="gla_fwd")
            y = _gla_gate_fwd(of, ob, proj, row(rep["gla_norm"][j]), name="gla_gate_fwd")
            x = _mm(y, wl["mix_out"], res=x, name="gla_out_proj")
            sv.update(bias=bias, proj=proj, logits=logits, of=of, ob=ob, sf=sf, sb=sb, y=y)
        else:
            proj = _mm(h1, wl["mix_in"], name="attn_qkv_proj")
            qkv = _attn_prep(proj, cs, sn, row(rep["attn_q_norm"][j]), row(rep["attn_k_norm"][j]), name="attn_prep")
            o = _attn_fwd(qkv, name="attn_fwd")
            x = _mm(o, wl["mix_out"], res=x, name="attn_out_proj")
            sv.update(proj=proj, qkv=qkv, o=o)
        sv["x1"] = x
        if i == 0:
            wl.update(ready(1, passing, x))
        dep = ()
        if i + 1 < depth:
            passing = arrive(i + 2, x)
            dep = (passing[2]["token"],)
        h2 = _rmsnorm_fwd(x, row(rep["norm_ffn"][i]), dep=dep, name="norm_ffn_fwd")
        F = wl["down"].shape[0]
        wc, bc = wl["conv"], rep["ffn_b_conv"][i]
        wcv, wcg, bcv, bcg = wc[:, :F], wc[:, F:], row(bc[:F]), row(bc[F:])
        uv = _mm(h2, wl["up_val"], name="ffn_up_val")
        ug = _mm(h2, wl["up_gate"], name="ffn_up_gate")
        act = _conv_act_fwd(uv, ug, wcv, wcg, bcv, bcg, name="ffn_conv_act")
        x = _mm(act, wl["down"], res=x, name="ffn_down")
        sv.update(h2=h2, uv=uv, ug=ug, act=act, wcv=wcv, wcg=wcg, bcv=bcv, bcg=bcg)
        saved.append(sv)
        if i + 1 < depth:
            wl = ready(i + 2, passing, x)

    dx, loss_tile = _loss_head(x, target, name="loss_head")
    loss = loss_tile[0, 0]

    in_sibling_stage, in_chip_stage, reduced = [], [], []

    def advance(group, after):
        tokens = []
        for tag, keys, started in in_chip_stage:
            partial, lands = _split_wait(started, after, name="grads_chips_arrive_" + tag)
            reduced.append((keys, partial, lands))
        in_chip_stage.clear()
        for tag, keys, started in in_sibling_stage:
            stacks, lands = _split_wait(started, after, name="grads_sibling_arrive_" + tag)
            partial = [_add_sibling(s, b, idx["core"], name="grads_add_sibling_%s_%d" % (tag, q))
                       for q, (s, b) in enumerate(zip(stacks, lands))]
            bufs = [lax.empty(p.shape, p.dtype) for p in partial]
            started = _split_start(_reduce_chip_plan(len(partial)), partial, bufs, name="grads_chips_send_" + tag)
            in_chip_stage.append((tag, keys, started))
            tokens.append(started["token"])
        in_sibling_stage.clear()
        if group is not None:
            tag, keys, stacks = group
            bufs = [lax.empty((4,) + s.shape[1:], s.dtype) for s in stacks]
            started = _split_start(_reduce_sibling_plan(len(stacks)), stacks, bufs, name="grads_sibling_send_" + tag)
            in_sibling_stage.append((tag, keys, started))
            tokens.append(started["token"])
        return tuple(tokens)

    def stack_for(name):
        return lax.empty((N_DEV,) + tuple(w[name].shape[1:]), BF16)

    gl = {k: [None] * depth for k in ("norm_mix", "norm_ffn", "ffn_w_conv", "ffn_b_conv")}
    gm = {k: [None] * (depth // 2) for k in ("gla_w_gate_up_f", "gla_b_gate_f", "gla_w_gate_up_b", "gla_b_gate_b",
                                             "gla_norm", "attn_q_norm", "attn_k_norm")}
    rk = GLA_GATE_RANK
    dep = ()
    for i in reversed(range(depth)):
        j = i // 2
        sv = saved[i]
        wl = sv["w"]
        dact = _mm(dx, wl["down"], tb=True, dep=dep, name="ffn_down_dgrad")
        g_down = _wgrad(sv["act"], dx, stack_for("ffn_w_down"), shard="rows", name="ffn_down_wgrad")
        dcv, dcg, dwv, dwg, dbv, dbg = _conv_act_bwd(sv["uv"], sv["ug"], sv["wcv"], sv["wcg"], sv["bcv"], sv["bcg"],
                                                     dact, name="ffn_conv_act_bwd")
        gl["ffn_w_conv"][i] = jnp.concatenate([dwv, dwg], axis=1)
        gl["ffn_b_conv"][i] = jnp.concatenate([dbv, dbg], axis=1)[0]
        mid = advance(None, dcv)
        duv = _conv_t(dcv, sv["wcv"], name="ffn_conv_t")
        dug = _conv_t(dcg, sv["wcg"], name="ffn_conv_t")
        dh2 = _mm(duv, wl["up_val"], tb=True, dep=mid, name="ffn_up_dgrad_val")
        dh2 = _mm(dug, wl["up_gate"], tb=True, res=dh2, name="ffn_up_dgrad_gate")
        g_up = _wgrad(sv["h2"], duv, stack_for("ffn_w_up"), shard="cols", group=0, name="ffn_up_wgrad_val")
        g_up = _wgrad(sv["h2"], dug, g_up, shard="cols", group=1, name="ffn_up_wgrad_gate")
        dx, dn = _rmsnorm_bwd(sv["x1"], row(rep["norm_ffn"][i]), dh2, dx, name="norm_ffn_bwd")
        gl["norm_ffn"][i] = dn[0]
        dep = advance(("ffn_l%d" % i, [("ffn_w_up", i), ("ffn_w_down", i)], [g_up, g_down]), dx)
        if i % 2 == 0:
            dy = _mm(dx, wl["mix_out"], tb=True, dep=dep, name="gla_out_dgrad")
            g_out = _wgrad(sv["y"], dx, stack_for("gla_w_out"), shard="rows", name="gla_out_wgrad")
            do, dg, dgn = _gla_gate_bwd(sv["of"], sv["ob"], sv["proj"], row(rep["gla_norm"][j]), dy, name="gla_gate_bwd")
            gm["gla_norm"][j] = dgn[0]
            dqkv_f, dlg_f, dqkv_b, dlg_b, dbias = _gla_bwd(sv["proj"], sv["logits"], sv["bias"], sv["sf"], sv["sb"], do,
                                                           name="gla_bwd")
            gm["gla_b_gate_f"][j] = dbias[0, :GLA_KD]
            gm["gla_b_gate_b"][j] = dbias[0, GLA_KD:]
            mid = advance(None, dqkv_f)
            dlogits = jnp.concatenate([dlg_f, dlg_b], axis=1)
            dr = _mm(dlogits, wl["gate"], tb=True, dep=mid, name="gla_gate_dgrad")
            dwg_full = _mm(sv["proj"], dlogits, ta=True, a_cols=ranks_cols, name="gla_gate_wgrad")
            gm["gla_w_gate_up_f"][j] = dwg_full[:rk, :GLA_KD]
            gm["gla_w_gate_up_b"][j] = dwg_full[rk:2 * rk, GLA_KD:]
            dproj = _gla_combine(dqkv_f, dqkv_b, dg, dr, name="gla_combine")
            dh1 = _mm(dproj, wl["mix_in"], tb=True, name="gla_in_dgrad")
            g_in = _wgrad(sv["h1"], dproj, stack_for("gla_w_in"), shard="cols", name="gla_in_wgrad")
            keys = [("gla_w_in", j), ("gla_w_out", j)]
        else:
            do = _mm(dx, wl["mix_out"], tb=True, out_dtype=BF16, dep=dep, name="attn_out_dgrad")
            g_out = _wgrad(sv["o"], dx, stack_for("attn_w_out"), shard="rows", name="attn_out_wgrad")
            dq, dk, dv = _attn_bwd(sv["qkv"], do, name="attn_bwd")
            mid = advance(None, dq)
            dproj, dqn, dkn = _attn_prep_bwd(sv["proj"], dq, dk, dv, cs, sn, row(rep["attn_q_norm"][j]),
                                             row(rep["attn_k_norm"][j]), name="attn_prep_bwd")
            gm["attn_q_norm"][j] = dqn[0]
            gm["attn_k_norm"][j] = dkn[0]
            dh1 = _mm(dproj, wl["mix_in"], tb=True, dep=mid, name="attn_qkv_dgrad")
            g_in = _wgrad(sv["h1"], dproj, stack_for("attn_w_qkv"), shard="cols", name="attn_qkv_wgrad")
            keys = [("attn_w_qkv", j), ("attn_w_out", j)]
        dx, dn = _rmsnorm_bwd(sv["x0"], row(rep["norm_mix"][i]), dh1, dx, name="norm_mix_bwd")
        gl["norm_mix"][i] = dn[0]
        dep = advance(("mix_l%d" % i, keys, [g_in, g_out]), dx)

    small = {k: jnp.stack(v) for k, v in {**gl, **gm}.items()}
    return loss, dx, reduced, small, advance


def kernel(x, norm_mix, norm_ffn, gla_w_in, gla_w_gate_up_f, gla_b_gate_f, gla_w_gate_up_b, gla_b_gate_b, gla_norm, gla_w_out, attn_w_qkv, attn_q_norm, attn_k_norm, attn_w_out, ffn_w_up, ffn_w_conv, ffn_b_conv, ffn_w_down, loss_target, m_norm_mix, m_norm_ffn, m_gla_w_in, m_gla_w_gate_up_f, m_gla_b_gate_f, m_gla_w_gate_up_b, m_gla_b_gate_b, m_gla_norm, m_gla_w_out, m_attn_w_qkv, m_attn_q_norm, m_attn_k_norm, m_attn_w_out, m_ffn_w_up, m_ffn_w_conv, m_ffn_b_conv, m_ffn_w_down, v_norm_mix, v_norm_ffn, v_gla_w_in, v_gla_w_gate_up_f, v_gla_b_gate_f, v_gla_w_gate_up_b, v_gla_b_gate_b, v_gla_norm, v_gla_w_out, v_attn_w_qkv, v_attn_q_norm, v_attn_k_norm, v_attn_w_out, v_ffn_w_up, v_ffn_w_conv, v_ffn_b_conv, v_ffn_w_down):
    given = dict(locals())
    w = {n: given[n] for n in WEIGHTS}
    m = {n: given["m_" + n] for n in WEIGHTS}
    v = {n: given["v_" + n] for n in WEIGHTS}
    shards = {n: w[n] for n in BIG + SMALL_SHARDED}
    rep = {n: w[n] for n in REPLICATED}

    x_pos, y_pos, c_pos = _place()
    dev = 4 * x_pos + 2 * y_pos + c_pos
    as_operand = lambda s: jnp.asarray(s, jnp.int32).reshape(1)
    idx = dict(dev=as_operand(dev), chip=as_operand(2 * x_pos + y_pos), core=as_operand(c_pos))

    loss_local, grad_x, reduced, small, advance = _local_step(x[0], loss_target[0], rep, shards, dev, idx)
    loss = lax.psum(loss_local, ("x", "y", "c"))

    big = {n: [lax.empty(w[n].shape, F32) for _ in range(4)] for n in BIG}

    def update_reduced():
        for keys, own, parts in reduced:
            for (n, layer), p_own, p_others in zip(keys, own, parts):
                big[n] = _adamw_layer(w[n], m[n], v[n], p_own, p_others, idx["chip"], big[n], layer,
                                      name="adamw_%s_l%d" % (n, layer))
        reduced.clear()

    advance(None, grad_x)
    update_reduced()

    rest = REPLICATED + SMALL_SHARDED
    flat = _rows(jnp.concatenate([small[n].reshape(-1) for n in rest]), 512)
    total = _sum_slots(_all_gather(flat, name="small_grads_all_gather"), name="small_grads_sum").reshape(-1)
    advance(None, [total, *big["ffn_w_up"]])
    update_reduced()
    g, off = {}, 0
    for n in rest:
        whole = total[off:off + small[n].size].reshape(small[n].shape)
        off += small[n].size
        width = w[n].shape[-1]
        g[n] = whole if n in REPLICATED else lax.dynamic_slice_in_dim(whole, dev * width, width, axis=whole.ndim - 1)

    delta, new_m, new_v = {}, {}, {}
    for n in WEIGHTS:
        if n in BIG:
            g[n], delta[n], new_m[n], new_v[n] = big[n]
        else:
            shape = w[n].shape
            two_d = (-1, shape[-1])
            d2, m2, v2 = _adamw(w[n].reshape(two_d), g[n].reshape(two_d), m[n].reshape(two_d), v[n].reshape(two_d),
                                name="adamw_" + n)
            delta[n], new_m[n], new_v[n] = d2.reshape(shape), m2.reshape(shape), v2.reshape(shape)

    return (loss, grad_x[None], *[g[n] for n in WEIGHTS], *[delta[n] for n in WEIGHTS],
            *[new_m[n] for n in WEIGHTS], *[new_v[n] for n in WEIGHTS])
```
